```python
import math
import jax, jax.numpy as jnp
from jax import lax
import numpy as np

D_MODEL = 2048
BATCH = 16
SEQ = 2048
DEPTH = 1

ATTN_HEADS = 16
ATTN_KV_HEADS = 4
ATTN_HEAD_DIM = 64
ATTN_GROUP = ATTN_HEADS // ATTN_KV_HEADS
WINDOW = 128
ATTN_BLOCK = WINDOW
ATTN_WIDTH = ATTN_HEADS * ATTN_HEAD_DIM
KV_WIDTH = ATTN_KV_HEADS * ATTN_HEAD_DIM
HGRN_HEADS = 8
HGRN_KEY_DIM = 128
HGRN_VALUE_DIM = 128
HGRN_WIDTH = HGRN_HEADS * HGRN_VALUE_DIM
HGRN_CHUNK = 64
REL_BUCKETS = 32
REL_MAX_DIST = 128
NORM_EPS = 1e-6
IN_WIDTHS = (ATTN_WIDTH, KV_WIDTH, KV_WIDTH, ATTN_WIDTH,
             HGRN_HEADS * HGRN_KEY_DIM, HGRN_HEADS * HGRN_KEY_DIM, HGRN_WIDTH, HGRN_WIDTH,
             D_MODEL, D_MODEL)
IN_PROJ_WIDTH = 10752

kernel_name = 'hybrid_swa_sinks_hgrn2_gated_merge'


def rms_norm(x, gain):
    xf = x.astype(jnp.float32)
    y = xf * lax.rsqrt(jnp.mean(xf * xf, axis=-1, keepdims=True) + NORM_EPS)
    return (y * gain.astype(jnp.float32)).astype(x.dtype)


def t5_bucket(dist):
    max_exact = REL_BUCKETS // 2
    d = jnp.maximum(dist, 0)
    df = jnp.maximum(d, 1).astype(jnp.float32)
    large = max_exact + (jnp.log(df / max_exact) / math.log(REL_MAX_DIST / max_exact)
                         * (REL_BUCKETS - max_exact)).astype(jnp.int32)
    large = jnp.minimum(large, REL_BUCKETS - 1)
    return jnp.where(d < max_exact, d, large)


def sliding_window_attention(q, k, v, sinks, rel_bias):
    B, S, _ = q.shape
    nb = S // ATTN_BLOCK
    qb = q.astype(jnp.float32).reshape(B, nb, ATTN_BLOCK, ATTN_KV_HEADS, ATTN_GROUP, ATTN_HEAD_DIM)

    def windows(t):
        t = t.astype(jnp.float32).reshape(B, S, ATTN_KV_HEADS, ATTN_HEAD_DIM)
        t = jnp.pad(t, ((0, 0), (ATTN_BLOCK, 0), (0, 0), (0, 0)))
        t = t.reshape(B, nb + 1, ATTN_BLOCK, ATTN_KV_HEADS, ATTN_HEAD_DIM)
        return jnp.concatenate([t[:, :-1], t[:, 1:]], axis=2)

    kw, vw = windows(k), windows(v)
    scores = jnp.einsum('bnqhgd,bnshd->bhgnqs', qb, kw) * (ATTN_HEAD_DIM ** -0.5)
    qi = jnp.arange(ATTN_BLOCK)[:, None]
    si = jnp.arange(2 * ATTN_BLOCK)[None, :]
    dist = qi + ATTN_BLOCK - si
    band = (dist >= 0) & (dist < WINDOW)
    key_pos = jnp.arange(nb)[:, None] * ATTN_BLOCK - ATTN_BLOCK + jnp.arange(2 * ATTN_BLOCK)[None, :]
    mask = band[None] & (key_pos >= 0)[:, None, :]
    bias = rel_bias[t5_bucket(dist)].astype(jnp.float32)
    bias = jnp.transpose(bias, (2, 0, 1)).reshape(ATTN_KV_HEADS, ATTN_GROUP, 1, ATTN_BLOCK, 2 * ATTN_BLOCK)
    scores = jnp.where(mask, scores + bias, -jnp.inf)
    sink = jnp.broadcast_to(sinks.astype(jnp.float32).reshape(ATTN_KV_HEADS, ATTN_GROUP, 1, 1, 1),
                            scores.shape[:-1] + (1,))
    probs = jax.nn.softmax(jnp.concatenate([scores, sink], axis=-1), axis=-1)[..., :-1]
    out = jnp.einsum('bhgnqs,bnshd->bnqhgd', probs, vw)
    return out.reshape(B, S, ATTN_WIDTH)


def hgrn2_recurrence(q, f_pre, i, lb):
    B, S, _ = q.shape
    nc = S // HGRN_CHUNK

    def chunks(t):
        return t.reshape(B, nc, HGRN_CHUNK, HGRN_HEADS, -1).transpose(1, 0, 3, 2, 4)

    lbf = lb.astype(jnp.float32)
    f = lbf + (1.0 - lbf) * jax.nn.sigmoid(f_pre.astype(jnp.float32))
    qc = chunks(jax.nn.silu(q.astype(jnp.float32)))
    kc = chunks(1.0 - f)
    vc = chunks(i.astype(jnp.float32))
    gc = jnp.cumsum(chunks(jnp.log(f)), axis=3)
    causal = jnp.tril(jnp.ones((HGRN_CHUNK, HGRN_CHUNK), dtype=bool))

    def step(state, inp):
        qt, kt, vt, gt = inp
        inter = jnp.einsum('bhcd,bhde->bhce', qt * jnp.exp(gt), state)
        diff = gt[:, :, :, None, :] - gt[:, :, None, :, :]
        decay = jnp.exp(jnp.where(causal[:, :, None], diff, -jnp.inf))
        attn = jnp.einsum('bhtd,bhsd,bhtsd->bhts', qt, kt, decay)
        intra = jnp.einsum('bhts,bhse->bhte', attn, vt)
        g_last = gt[:, :, -1]
        k_dec = kt * jnp.exp(g_last[:, :, None, :] - gt)
        new_state = jnp.exp(g_last)[..., None] * state + jnp.einsum('bhsd,bhse->bhde', k_dec, vt)
        return new_state, inter + intra

    s0 = jnp.zeros((B, HGRN_HEADS, HGRN_KEY_DIM, HGRN_VALUE_DIM), jnp.float32)
    _, o = lax.scan(step, s0, (qc, kc, vc, gc))
    return o.transpose(1, 0, 3, 2, 4).reshape(B, S, HGRN_HEADS, HGRN_VALUE_DIM)


def _fwd_setup_inputs(seed: int = 0) -> dict:
    key = jax.random.key(seed)
    ks = jax.random.split(key, 12)
    f32 = jnp.float32
    return {
        'x': jax.random.normal(ks[0], (BATCH, SEQ, D_MODEL), f32),
        'norm_pre': 1.0 + 0.1 * jax.random.normal(ks[1], (DEPTH, D_MODEL), f32),
        'w_in': jax.random.normal(ks[2], (DEPTH, D_MODEL, IN_PROJ_WIDTH), f32) * D_MODEL ** -0.5,
        'rel_bias': 0.5 * jax.random.normal(ks[3], (REL_BUCKETS, ATTN_HEADS), f32),
        'attn_sinks': jax.random.normal(ks[4], (DEPTH, ATTN_HEADS), f32),
        'lb_logits': 0.5 * jax.random.normal(ks[5], (DEPTH + 1, HGRN_HEADS * HGRN_KEY_DIM), f32),
        'hgrn_norm': 1.0 + 0.1 * jax.random.normal(ks[6], (DEPTH, HGRN_HEADS, HGRN_VALUE_DIM), f32),
        'w_branch_attn': jax.random.normal(ks[7], (DEPTH, ATTN_WIDTH, D_MODEL), f32) * ATTN_WIDTH ** -0.5,
        'w_branch_hgrn': jax.random.normal(ks[8], (DEPTH, HGRN_WIDTH, D_MODEL), f32) * HGRN_WIDTH ** -0.5,
        'w_out': jax.random.normal(ks[9], (DEPTH, D_MODEL, D_MODEL), f32) * D_MODEL ** -0.5,
        'norm_post': 1.0 + 0.1 * jax.random.normal(ks[10], (DEPTH, D_MODEL), f32),
    }


def _fwd_reference(x, norm_pre, w_in, rel_bias, attn_sinks, lb_logits, hgrn_norm,
              w_branch_attn, w_branch_hgrn, w_out, norm_post):
    split_points = [int(p) for p in np.cumsum(IN_WIDTHS)[:-1]]
    lower_bounds = jnp.cumsum(jax.nn.softmax(lb_logits.astype(jnp.float32), axis=0), axis=0)[:DEPTH]
    for layer in range(DEPTH):
        h = rms_norm(x, norm_pre[layer])
        proj = jnp.matmul(h, w_in[layer])
        aq, ak, av, ag, hq, hf, hi, hg, gate_a, gate_h = jnp.split(proj, split_points, axis=-1)
        ya = sliding_window_attention(aq, ak, av, attn_sinks[layer], rel_bias)
        ya = (ya * jax.nn.silu(ag.astype(jnp.float32))).astype(x.dtype)
        oh = rms_norm(hgrn2_recurrence(hq, hf, hi, lower_bounds[layer]), hgrn_norm[layer])
        yh = (oh.reshape(oh.shape[0], oh.shape[1], HGRN_WIDTH)
              * jax.nn.silu(hg.astype(jnp.float32))).astype(x.dtype)
        ua = jnp.matmul(ya, w_branch_attn[layer])
        uh = jnp.matmul(yh, w_branch_hgrn[layer])
        merged = jax.nn.sigmoid(gate_a) * ua + jax.nn.sigmoid(gate_h) * uh
        y = jnp.matmul(merged, w_out[layer])
        x = x + rms_norm(y, norm_post[layer]).astype(x.dtype)
    return x


import jax as _jax
import jax.numpy as _jnp

TWIN_FORMAT = 'train_step'
FWD_PARAMS = ['x', 'norm_pre', 'w_in', 'rel_bias', 'attn_sinks', 'lb_logits', 'hgrn_norm', 'w_branch_attn', 'w_branch_hgrn', 'w_out', 'norm_post']
TWIN_WEIGHTS = ['norm_pre', 'w_in', 'rel_bias', 'attn_sinks', 'lb_logits', 'hgrn_norm', 'w_branch_attn', 'w_branch_hgrn', 'w_out', 'norm_post']
TWIN_DIFF_INPUT = 'x'
TWIN_INPUTS = ['x', 'norm_pre', 'w_in', 'rel_bias', 'attn_sinks', 'lb_logits', 'hgrn_norm', 'w_branch_attn', 'w_branch_hgrn', 'w_out', 'norm_post', 'loss_target', 'm_norm_pre', 'm_w_in', 'm_rel_bias', 'm_attn_sinks', 'm_lb_logits', 'm_hgrn_norm', 'm_w_branch_attn', 'm_w_branch_hgrn', 'm_w_out', 'm_norm_post', 'v_norm_pre', 'v_w_in', 'v_rel_bias', 'v_attn_sinks', 'v_lb_logits', 'v_hgrn_norm', 'v_w_branch_attn', 'v_w_branch_hgrn', 'v_w_out', 'v_norm_post']
TWIN_OUTPUTS = ['loss', 'grad_x', 'grad_norm_pre', 'grad_w_in', 'grad_rel_bias', 'grad_attn_sinks', 'grad_lb_logits', 'grad_hgrn_norm', 'grad_w_branch_attn', 'grad_w_branch_hgrn', 'grad_w_out', 'grad_norm_post', 'delta_norm_pre', 'delta_w_in', 'delta_rel_bias', 'delta_attn_sinks', 'delta_lb_logits', 'delta_hgrn_norm', 'delta_w_branch_attn', 'delta_w_branch_hgrn', 'delta_w_out', 'delta_norm_post', 'new_m_norm_pre', 'new_m_w_in', 'new_m_rel_bias', 'new_m_attn_sinks', 'new_m_lb_logits', 'new_m_hgrn_norm', 'new_m_w_branch_attn', 'new_m_w_branch_hgrn', 'new_m_w_out', 'new_m_norm_post', 'new_v_norm_pre', 'new_v_w_in', 'new_v_rel_bias', 'new_v_attn_sinks', 'new_v_lb_logits', 'new_v_hgrn_norm', 'new_v_w_branch_attn', 'new_v_w_branch_hgrn', 'new_v_w_out', 'new_v_norm_post']
TWIN_LEAF_KINDS = {'loss': 'loss', 'grad_x': 'grad_x', 'grad_norm_pre': 'grad_w', 'grad_w_in': 'grad_w', 'grad_rel_bias': 'grad_w', 'grad_attn_sinks': 'grad_w', 'grad_lb_logits': 'grad_w', 'grad_hgrn_norm': 'grad_w', 'grad_w_branch_attn': 'grad_w', 'grad_w_branch_hgrn': 'grad_w', 'grad_w_out': 'grad_w', 'grad_norm_post': 'grad_w', 'delta_norm_pre': 'delta_w', 'delta_w_in': 'delta_w', 'delta_rel_bias': 'delta_w', 'delta_attn_sinks': 'delta_w', 'delta_lb_logits': 'delta_w', 'delta_hgrn_norm': 'delta_w', 'delta_w_branch_attn': 'delta_w', 'delta_w_branch_hgrn': 'delta_w', 'delta_w_out': 'delta_w', 'delta_norm_post': 'delta_w', 'new_m_norm_pre': 'new_m', 'new_m_w_in': 'new_m', 'new_m_rel_bias': 'new_m', 'new_m_attn_sinks': 'new_m', 'new_m_lb_logits': 'new_m', 'new_m_hgrn_norm': 'new_m', 'new_m_w_branch_attn': 'new_m', 'new_m_w_branch_hgrn': 'new_m', 'new_m_w_out': 'new_m', 'new_m_norm_post': 'new_m', 'new_v_norm_pre': 'new_v', 'new_v_w_in': 'new_v', 'new_v_rel_bias': 'new_v', 'new_v_attn_sinks': 'new_v', 'new_v_lb_logits': 'new_v', 'new_v_hgrn_norm': 'new_v', 'new_v_w_branch_attn': 'new_v', 'new_v_w_branch_hgrn': 'new_v', 'new_v_w_out': 'new_v', 'new_v_norm_post': 'new_v'}


def _forward(args):
    return _fwd_reference(*[args[k] for k in FWD_PARAMS])


def _output_shape():
    out = _jax.eval_shape(lambda: _forward(_fwd_setup_inputs(0)))
    return out.shape, out.dtype

N_MICROBATCH = 1
ADAM_LR = 0.001
ADAM_B1 = 0.9
ADAM_B2 = 0.999
ADAM_EPS = 1e-08
ADAM_WD = 0.01
ADAM_STEP = 10
PER_EXAMPLE_BATCH_AXIS = {'x': 0, 'loss_target': 0}
SHARED_INPUTS = []
_WEIGHT_DTYPES = {'norm_pre': _jnp.float32, 'w_in': _jnp.float32, 'rel_bias': _jnp.float32, 'attn_sinks': _jnp.float32, 'lb_logits': _jnp.float32, 'hgrn_norm': _jnp.float32, 'w_branch_attn': _jnp.float32, 'w_branch_hgrn': _jnp.float32, 'w_out': _jnp.float32, 'norm_post': _jnp.float32}
MOMENT_SCALE = {'norm_pre': 1.972003e-01, 'w_in': 8.574063e-02, 'rel_bias': 4.773241e-02, 'attn_sinks': 2.135302e-02, 'lb_logits': 1.574521e-02, 'hgrn_norm': 1.999075e-01, 'w_branch_attn': 2.743780e-02, 'w_branch_hgrn': 1.404481e-01, 'w_out': 1.455924e-01, 'norm_post': 1.610144e+01}


def _to_microbatches(a, axis):
    t = _jnp.moveaxis(a, axis, 0)
    t = t.reshape((N_MICROBATCH, t.shape[0] // N_MICROBATCH) + t.shape[1:])
    return _jnp.moveaxis(t, 1, axis + 1)


def setup_inputs(seed: int = 0) -> dict:
    inp = _fwd_setup_inputs(seed)
    key = _jax.random.fold_in(_jax.random.key(seed), 7919)
    shape, _ = _output_shape()
    out = dict(inp)
    out["loss_target"] = _jax.random.normal(_jax.random.fold_in(key, 0), shape, _jnp.float32)
    for i, name in enumerate(TWIN_WEIGHTS):
        w = inp[name].astype(_jnp.float32)
        if MOMENT_SCALE is None:
            s = _jnp.sqrt(_jnp.mean(_jnp.square(w)) + 1e-30)
        else:
            s = MOMENT_SCALE[name]
        km, kv = _jax.random.split(_jax.random.fold_in(key, i + 1))
        out[name] = w
        out["m_" + name] = s * _jax.random.normal(km, w.shape, _jnp.float32)
        out["v_" + name] = (s * s) * _jax.random.uniform(kv, w.shape, _jnp.float32, 0.5, 1.5)
    if N_MICROBATCH > 1:
        for name, axis in PER_EXAMPLE_BATCH_AXIS.items():
            out[name] = _to_microbatches(out[name], axis)
    return {'x': out['x'], 'norm_pre': out['norm_pre'], 'w_in': out['w_in'], 'rel_bias': out['rel_bias'], 'attn_sinks': out['attn_sinks'], 'lb_logits': out['lb_logits'], 'hgrn_norm': out['hgrn_norm'], 'w_branch_attn': out['w_branch_attn'], 'w_branch_hgrn': out['w_branch_hgrn'], 'w_out': out['w_out'], 'norm_post': out['norm_post'], 'loss_target': out['loss_target'], 'm_norm_pre': out['m_norm_pre'], 'm_w_in': out['m_w_in'], 'm_rel_bias': out['m_rel_bias'], 'm_attn_sinks': out['m_attn_sinks'], 'm_lb_logits': out['m_lb_logits'], 'm_hgrn_norm': out['m_hgrn_norm'], 'm_w_branch_attn': out['m_w_branch_attn'], 'm_w_branch_hgrn': out['m_w_branch_hgrn'], 'm_w_out': out['m_w_out'], 'm_norm_post': out['m_norm_post'], 'v_norm_pre': out['v_norm_pre'], 'v_w_in': out['v_w_in'], 'v_rel_bias': out['v_rel_bias'], 'v_attn_sinks': out['v_attn_sinks'], 'v_lb_logits': out['v_lb_logits'], 'v_hgrn_norm': out['v_hgrn_norm'], 'v_w_branch_attn': out['v_w_branch_attn'], 'v_w_branch_hgrn': out['v_w_branch_hgrn'], 'v_w_out': out['v_w_out'], 'v_norm_post': out['v_norm_post']}


def _loss(weights, diff, rest, loss_target):
    with _jax.named_scope("forward"):
        args = {**rest, TWIN_DIFF_INPUT: diff, **{k: w.astype(_WEIGHT_DTYPES[k]) for k, w in weights.items()}}
        y = _forward(args)
    with _jax.named_scope("loss_head"):
        err = _jnp.square(y.astype(_jnp.float32) - loss_target)
        return 0.5 * _jnp.sum(_jnp.mean(err, axis=-1)) if err.ndim else 0.5 * err


def _adamw(w, g, m, v):
    m = ADAM_B1 * m + (1.0 - ADAM_B1) * g
    v = ADAM_B2 * v + (1.0 - ADAM_B2) * _jnp.square(g)
    m_hat = m / (1.0 - ADAM_B1 ** ADAM_STEP)
    v_hat = v / (1.0 - ADAM_B2 ** ADAM_STEP)
    delta = -ADAM_LR * (m_hat / (_jnp.sqrt(v_hat) + ADAM_EPS) + ADAM_WD * w)
    return delta, m, v


def reference(x, norm_pre, w_in, rel_bias, attn_sinks, lb_logits, hgrn_norm, w_branch_attn, w_branch_hgrn, w_out, norm_post, loss_target, m_norm_pre, m_w_in, m_rel_bias, m_attn_sinks, m_lb_logits, m_hgrn_norm, m_w_branch_attn, m_w_branch_hgrn, m_w_out, m_norm_post, v_norm_pre, v_w_in, v_rel_bias, v_attn_sinks, v_lb_logits, v_hgrn_norm, v_w_branch_attn, v_w_branch_hgrn, v_w_out, v_norm_post):
    given = dict(x=x, norm_pre=norm_pre, w_in=w_in, rel_bias=rel_bias, attn_sinks=attn_sinks, lb_logits=lb_logits, hgrn_norm=hgrn_norm, w_branch_attn=w_branch_attn, w_branch_hgrn=w_branch_hgrn, w_out=w_out, norm_post=norm_post, loss_target=loss_target, m_norm_pre=m_norm_pre, m_w_in=m_w_in, m_rel_bias=m_rel_bias, m_attn_sinks=m_attn_sinks, m_lb_logits=m_lb_logits, m_hgrn_norm=m_hgrn_norm, m_w_branch_attn=m_w_branch_attn, m_w_branch_hgrn=m_w_branch_hgrn, m_w_out=m_w_out, m_norm_post=m_norm_post, v_norm_pre=v_norm_pre, v_w_in=v_w_in, v_rel_bias=v_rel_bias, v_attn_sinks=v_attn_sinks, v_lb_logits=v_lb_logits, v_hgrn_norm=v_hgrn_norm, v_w_branch_attn=v_w_branch_attn, v_w_branch_hgrn=v_w_branch_hgrn, v_w_out=v_w_out, v_norm_post=v_norm_post)
    weights = {n: given[n] for n in TWIN_WEIGHTS}
    shared = {n: given[n] for n in SHARED_INPUTS}
    per_example = {n: given[n] for n in ['x']}
    grad_fn = _jax.value_and_grad(_loss, argnums=(0, 1))

    def one_microbatch(ex, loss_target):
        ex = dict(ex)
        diff = ex.pop(TWIN_DIFF_INPUT)
        return grad_fn(weights, diff, {**shared, **ex}, loss_target)

    if N_MICROBATCH == 1:
        loss, (grad_w, grad_x) = one_microbatch(per_example, given["loss_target"])
    else:
        def body(carry, xs):
            loss_sum, grad_sum = carry
            l_k, (gw_k, gx_k) = one_microbatch(xs[0], xs[1])
            with _jax.named_scope("update"):
                return (loss_sum + l_k, _jax.tree.map(_jnp.add, grad_sum, gw_k)), gx_k

        init = (_jnp.zeros((), _jnp.float32), _jax.tree.map(_jnp.zeros_like, weights))
        (loss, grad_w), grad_x = _jax.lax.scan(body, init, (per_example, given["loss_target"]))
    with _jax.named_scope("update"):
        delta_w, new_m, new_v = {}, {}, {}
        for n in TWIN_WEIGHTS:
            delta_w[n], new_m[n], new_v[n] = _adamw(weights[n], grad_w[n], given["m_" + n], given["v_" + n])
    return (loss, grad_x, *[grad_w[n] for n in TWIN_WEIGHTS], *[delta_w[n] for n in TWIN_WEIGHTS],
            *[new_m[n] for n in TWIN_WEIGHTS], *[new_v[n] for n in TWIN_WEIGHTS])
```

```python
import functools
import math

import numpy as np
import jax
import jax.numpy as jnp
from jax import lax
from jax.experimental import pallas as pl
from jax.experimental.pallas import tpu as pltpu

F32 = jnp.float32
BF16 = jnp.bfloat16

D_MODEL = 2048
ATTN_HEADS = 16
ATTN_HEAD_DIM = 64
ATTN_BLOCK = 128
HGRN_HEADS = 8
HGRN_DIM = 128
HGRN_CHUNK = 64
HGRN_SUB = 16
REL_BUCKETS = 32
REL_MAX_DIST = 128
NORM_EPS = 1e-6
C_AQ, C_AK, C_AV, C_AG = 0, 1024, 1280, 1536
C_HQ, C_HF, C_HI, C_HG = 2560, 3584, 4608, 5632
C_GA, C_GH = 6656, 8704
IN_WIDTH = 10752
N_DEV = 8

ADAM_LR = 0.001
ADAM_B1 = 0.9
ADAM_B2 = 0.999
ADAM_EPS = 1e-08
ADAM_WD = 0.01
ADAM_STEP = 10

VMEM_LIMIT_V7X = 56 * 1024 * 1024
NEG_BIG = -1e30

NT = (((1,), (1,)), ((), ()))
TN = (((0,), (0,)), ((), ()))
NN = (((1,), (0,)), ((), ()))


def _dot(a, b, dims=NN):
    return lax.dot_general(a, b, dims, preferred_element_type=F32)


def _params(sem=None):
    return pltpu.CompilerParams(dimension_semantics=sem, vmem_limit_bytes=VMEM_LIMIT_V7X)


def _sigmoid(x):
    return 1.0 / (1.0 + jnp.exp(-x))


def _t5_bucket_table():
    qi = np.arange(ATTN_BLOCK)[:, None]
    si = np.arange(2 * ATTN_BLOCK)[None, :]
    dist = qi + ATTN_BLOCK - si
    max_exact = REL_BUCKETS // 2
    d = np.maximum(dist, 0)
    df = np.maximum(d, 1).astype(np.float32)
    large = max_exact + (np.log(df / np.float32(max_exact)).astype(np.float32)
                         / np.float32(math.log(REL_MAX_DIST / max_exact))
                         * np.float32(REL_BUCKETS - max_exact)).astype(np.int32)
    large = np.minimum(large, REL_BUCKETS - 1)
    return np.where(d < max_exact, d, large).astype(np.int32)


def _inproj(x2, gpre, wt_in):
    T = x2.shape[0]
    tm = min(1024, T)
    tn = 768

    def body(x_ref, g_ref, w_ref, proj_ref, h_ref):
        @pl.when(pl.program_id(1) == 0)
        def _():
            x = x_ref[...]
            r = lax.rsqrt(jnp.mean(x * x, axis=-1, keepdims=True) + NORM_EPS)
            h_ref[...] = (x * r * g_ref[...]).astype(BF16)

        proj_ref[...] = _dot(h_ref[...], w_ref[...], NT)

    return pl.pallas_call(
        body, name="inproj",
        grid=(T // tm, IN_WIDTH // tn),
        in_specs=[pl.BlockSpec((tm, D_MODEL), lambda i, j: (i, 0)),
                  pl.BlockSpec((1, D_MODEL), lambda i, j: (0, 0)),
                  pl.BlockSpec((tn, D_MODEL), lambda i, j: (j, 0))],
        out_specs=[pl.BlockSpec((tm, tn), lambda i, j: (i, j)),
                   pl.BlockSpec((tm, D_MODEL), lambda i, j: (i, 0))],
        out_shape=[jax.ShapeDtypeStruct((T, IN_WIDTH), F32),
                   jax.ShapeDtypeStruct((T, D_MODEL), BF16)],
        compiler_params=_params(("arbitrary", "arbitrary")),
    )(x2, gpre, wt_in)


def _bias_table(rel_bias, bucket):
    def body(rb_ref, bk_ref, out_ref):
        h = pl.program_id(0)
        bk = bk_ref[...]
        acc = jnp.zeros(bk.shape, F32)
        for b in range(REL_BUCKETS):
            acc = jnp.where(bk == b, rb_ref[b, h], acc)
        out_ref[...] = acc

    return pl.pallas_call(
        body, name="bias_table",
        grid=(ATTN_HEADS,),
        in_specs=[pl.BlockSpec(memory_space=pltpu.SMEM),
                  pl.BlockSpec((ATTN_BLOCK, 2 * ATTN_BLOCK), lambda h: (0, 0))],
        out_specs=pl.BlockSpec((None, ATTN_BLOCK, 2 * ATTN_BLOCK), lambda h: (h, 0, 0)),
        out_shape=jax.ShapeDtypeStruct((ATTN_HEADS, ATTN_BLOCK, 2 * ATTN_BLOCK), F32),
        compiler_params=_params(("arbitrary",)),
    )(rel_bias, bucket)


def _bias_table_bwd(dbias, bucket):
    def body(db_ref, bk_ref, out_ref):
        h = pl.program_id(0)
        bk = bk_ref[...]
        db = db_ref[...]
        for b in range(REL_BUCKETS):
            out_ref[b, h] = jnp.sum(jnp.where(bk == b, db, 0.0))

    return pl.pallas_call(
        body, name="bias_table_bwd",
        grid=(ATTN_HEADS,),
        in_specs=[pl.BlockSpec((None, ATTN_BLOCK, 2 * ATTN_BLOCK), lambda h: (h, 0, 0)),
                  pl.BlockSpec((ATTN_BLOCK, 2 * ATTN_BLOCK), lambda h: (0, 0))],
        out_specs=pl.BlockSpec(memory_space=pltpu.SMEM),
        out_shape=jax.ShapeDtypeStruct((REL_BUCKETS, ATTN_HEADS), F32),
        compiler_params=_params(("arbitrary",)),
    )(dbias, bucket)


def _attn_common(qkvg, kv_prev, blk):
    lane = lax.broadcasted_iota(jnp.int32, (1, 128), 1)
    half = (lane < ATTN_HEAD_DIM, lane >= ATTN_HEAD_DIM)
    kv_cur = qkvg[:, C_AK:C_AG]
    win = jnp.concatenate([kv_prev, kv_cur], axis=0)
    k_slab, v_slab = [], []
    for r in range(2):
        ks = win[:, 128 * r:128 * r + 128]
        vs = win[:, 256 + 128 * r:256 + 128 * r + 128]
        k_slab.append((ks, pltpu.roll(ks, ATTN_HEAD_DIM, 1)))
        v_slab.append((vs, pltpu.roll(vs, ATTN_HEAD_DIM, 1)))
    qi = lax.broadcasted_iota(jnp.int32, (ATTN_BLOCK, 2 * ATTN_BLOCK), 0)
    si = lax.broadcasted_iota(jnp.int32, (ATTN_BLOCK, 2 * ATTN_BLOCK), 1)
    valid = (si > qi) & (si <= qi + ATTN_BLOCK) & ((si >= ATTN_BLOCK) | (blk > 0))
    return half, k_slab, v_slab, valid


def _attn_head(h, qkvg, half, k_slab, v_slab, valid, bias_ref, sinks_ref):
    p, a = h // 2, h % 2
    j = h // 4
    r, aj = j // 2, j % 2
    swapped = 0 if aj == a else 1
    qm = jnp.where(half[a], qkvg[:, 128 * p:128 * p + 128], 0.0).astype(BF16)
    kk = k_slab[r][swapped]
    vv = v_slab[r][swapped]
    s = _dot(qm, kk.astype(BF16), NT) * (ATTN_HEAD_DIM ** -0.5) + bias_ref[h]
    s = jnp.where(valid, s, NEG_BIG)
    sink = sinks_ref[0, h]
    m = jnp.maximum(jnp.max(s, axis=-1, keepdims=True), sink)
    e = jnp.exp(s - m)
    es = jnp.exp(sink - m)
    inv = 1.0 / (jnp.sum(e, axis=-1, keepdims=True) + es)
    pn = e * inv
    vx = jnp.where(half[a], vv, 0.0).astype(BF16)
    o_h = _dot(pn.astype(BF16), vx)
    return dict(p=p, a=a, r=r, swapped=swapped, qm=qm, kk=kk, vv=vv, pn=pn, psink=es * inv, o_h=o_h)


def _attn_specs(nb):
    row = lambda b, i: b * nb + i
    return [
        pl.BlockSpec((ATTN_BLOCK, C_HQ), lambda b, i: (row(b, i), 0)),
        pl.BlockSpec((ATTN_BLOCK, 512), lambda b, i: (row(b, jnp.maximum(i - 1, 0)), 2)),
        pl.BlockSpec((ATTN_HEADS, ATTN_BLOCK, 2 * ATTN_BLOCK), lambda b, i: (0, 0, 0)),
        pl.BlockSpec(memory_space=pltpu.SMEM),
    ]


def _attn_fwd(proj, bias, sinks, nseq, nb):
    T = proj.shape[0]

    def body(qkvg_ref, kvp_ref, bias_ref, sinks_ref, ya_ref):
        qkvg = qkvg_ref[...]
        half, k_slab, v_slab, valid = _attn_common(qkvg, kvp_ref[...], pl.program_id(1))
        slabs = []
        for p in range(ATTN_HEADS // 2):
            o = None
            for h in (2 * p, 2 * p + 1):
                hd = _attn_head(h, qkvg, half, k_slab, v_slab, valid, bias_ref, sinks_ref)
                o = hd["o_h"] if o is None else o + hd["o_h"]
            slabs.append(o)
        o_all = jnp.concatenate(slabs, axis=1)
        g = qkvg[:, C_AG:C_HQ]
        ya_ref[...] = (o_all * (g * _sigmoid(g))).astype(BF16)

    return pl.pallas_call(
        body, name="attn_fwd",
        grid=(nseq, nb),
        in_specs=_attn_specs(nb),
        out_specs=pl.BlockSpec((ATTN_BLOCK, 1024), lambda b, i: (b * nb + i, 0)),
        out_shape=jax.ShapeDtypeStruct((T, 1024), BF16),
        compiler_params=_params(("arbitrary", "arbitrary")),
    )(proj, proj, bias, sinks)


def _attn_bwd(proj, bias, sinks, d_ya, nseq, nb):
    T = proj.shape[0]
    S = nb * ATTN_BLOCK
    scale = ATTN_HEAD_DIM ** -0.5

    def body(qkvg_ref, kvp_ref, bias_ref, sinks_ref, dya_ref, dq_ref, dkv_ref, dg_ref, dbias_ref, dsinks_ref):
        b, i = pl.program_id(0), pl.program_id(1)
        first = (b == 0) & (i == 0)

        @pl.when(first)
        def _():
            dbias_ref[...] = jnp.zeros(dbias_ref.shape, F32)
            for h in range(ATTN_HEADS):
                dsinks_ref[0, h] = 0.0

        qkvg = qkvg_ref[...]
        half, k_slab, v_slab, valid = _attn_common(qkvg, kvp_ref[...], i)
        g = qkvg[:, C_AG:C_HQ]
        sg = _sigmoid(g)
        silu_g = g * sg
        dya = dya_ref[...].astype(F32)
        do_all = dya * silu_g
        dq_slabs, o_slabs = [], []
        dk_acc = [[None, None], [None, None]]
        dv_acc = [[None, None], [None, None]]

        def add(acc, r, sw, val):
            acc[r][sw] = val if acc[r][sw] is None else acc[r][sw] + val

        for p in range(ATTN_HEADS // 2):
            dq_p, o_p = None, None
            do_slab = do_all[:, 128 * p:128 * p + 128]
            for h in (2 * p, 2 * p + 1):
                hd = _attn_head(h, qkvg, half, k_slab, v_slab, valid, bias_ref, sinks_ref)
                a, r, sw, pn = hd["a"], hd["r"], hd["swapped"], hd["pn"]
                dom = jnp.where(half[a], do_slab, 0.0)
                domb = dom.astype(BF16)
                dp = _dot(domb, hd["vv"].astype(BF16), NT)
                delta = jnp.sum(dom * hd["o_h"], axis=-1, keepdims=True)
                ds = pn * (dp - delta)
                dsinks_ref[0, h] += -jnp.sum(hd["psink"] * delta)
                dbias_ref[h] += ds
                dsb = ds.astype(BF16)
                kxm = jnp.where(half[a], hd["kk"], 0.0).astype(BF16)
                dq_h = _dot(dsb, kxm) * scale
                dq_p = dq_h if dq_p is None else dq_p + dq_h
                o_p = hd["o_h"] if o_p is None else o_p + hd["o_h"]
                add(dk_acc, r, sw, _dot(dsb, hd["qm"], TN) * scale)
                add(dv_acc, r, sw, _dot(pn.astype(BF16), domb, TN))
            dq_slabs.append(dq_p)
            o_slabs.append(o_p)

        dq_ref[...] = jnp.concatenate(dq_slabs, axis=1).astype(BF16)
        o_all = jnp.concatenate(o_slabs, axis=1)
        dg_ref[...] = (dya * o_all * (sg * (1.0 + g * (1.0 - sg)))).astype(BF16)

        pieces = []
        for acc in (dk_acc, dv_acc):
            for r in range(2):
                pieces.append(acc[r][0] + pltpu.roll(acc[r][1], ATTN_HEAD_DIM, 1))
        dkv = jnp.concatenate(pieces, axis=1)
        cur = pl.multiple_of(i * ATTN_BLOCK, ATTN_BLOCK)
        dkv_ref[pl.ds(cur, ATTN_BLOCK), :] = dkv[ATTN_BLOCK:].astype(BF16)

        @pl.when(i > 0)
        def _():
            prev = pl.multiple_of((i - 1) * ATTN_BLOCK, ATTN_BLOCK)
            old = dkv_ref[pl.ds(prev, ATTN_BLOCK), :].astype(F32)
            dkv_ref[pl.ds(prev, ATTN_BLOCK), :] = (old + dkv[:ATTN_BLOCK]).astype(BF16)

    row_spec = lambda w: pl.BlockSpec((ATTN_BLOCK, w), lambda b, i: (b * nb + i, 0))
    return pl.pallas_call(
        body, name="attn_bwd",
        grid=(nseq, nb),
        in_specs=_attn_specs(nb) + [row_spec(1024)],
        out_specs=[row_spec(1024),
                   pl.BlockSpec((S, 512), lambda b, i: (b, 0)),
                   row_spec(1024),
                   pl.BlockSpec((ATTN_HEADS, ATTN_BLOCK, 2 * ATTN_BLOCK), lambda b, i: (0, 0, 0)),
                   pl.BlockSpec(memory_space=pltpu.SMEM)],
        out_shape=[jax.ShapeDtypeStruct((T, 1024), BF16),
                   jax.ShapeDtypeStruct((T, 512), BF16),
                   jax.ShapeDtypeStruct((T, 1024), BF16),
                   jax.ShapeDtypeStruct((ATTN_HEADS, ATTN_BLOCK, 2 * ATTN_BLOCK), F32),
                   jax.ShapeDtypeStruct((1, ATTN_HEADS), F32)],
        compiler_params=_params(("arbitrary", "arbitrary")),
    )(proj, proj, bias, sinks, d_ya)


def _split3(x):
    hi = x.astype(BF16)
    r1 = x - hi.astype(F32)
    mid = r1.astype(BF16)
    lo = (r1 - mid.astype(F32)).astype(BF16)
    return jnp.concatenate([hi, mid, lo], axis=1)


def _tri_sum(tri, x):
    y = _dot(tri, _split3(x))
    return y[:, :128] + y[:, 128:256] + y[:, 256:]


def _hgrn_chunk(hq, hf, hi, lb):
    C = HGRN_CHUNK
    t = lax.broadcasted_iota(jnp.int32, (C, C), 0)
    s = lax.broadcasted_iota(jnp.int32, (C, C), 1)
    causal = s <= t
    sf = _sigmoid(hf)
    f = lb + (1.0 - lb) * sf
    G = _tri_sum(causal.astype(BF16), jnp.log(f))
    sq = _sigmoid(hq)
    qs = hq * sq
    k = 1.0 - f
    rowblk = lax.broadcasted_iota(jnp.int32, (C, 1), 0) // HGRN_SUB
    qt, kt, eq, ek = [], [], [], []
    for i in range(C // HGRN_SUB):
        lo = HGRN_SUB * i
        ref = G[lo + HGRN_SUB // 2:lo + HGRN_SUB // 2 + 1, :]
        eq_i = jnp.exp(G[lo:lo + HGRN_SUB] - ref)
        ek_i = jnp.exp(jnp.where(rowblk <= i, ref - G, 0.0))
        eq.append(eq_i)
        ek.append(ek_i)
        qt.append((qs[lo:lo + HGRN_SUB] * eq_i).astype(BF16))
        kt.append((k * ek_i).astype(BF16))
    A = jnp.concatenate([_dot(qt[i], kt[i], NT) for i in range(C // HGRN_SUB)], axis=0)
    A = jnp.where(causal, A, 0.0)
    glast = G[C - 1:C, :]
    eG = jnp.exp(G)
    edec = jnp.exp(glast - G)
    return dict(causal=causal, sf=sf, f=f, G=G, sq=sq, qs=qs, k=k, qt=qt, kt=kt, eq=eq, ek=ek, A=A,
                glast=glast, eG=eG, edec=edec, qhat=qs * eG, kdec=k * edec, v=hi)


def _hgrn_specs(nseq, ng, rows, reverse):
    gi = (lambda g: ng - 1 - g) if reverse else (lambda g: g)
    col = lambda c0: pl.BlockSpec((rows, HGRN_DIM), lambda h, b, g: (b * ng + gi(g), c0 // HGRN_DIM + h))
    return gi, [col(C_HQ), col(C_HF), col(C_HI), col(C_HG),
                pl.BlockSpec((2, HGRN_DIM), lambda h, b, g: (0, h)),
                pl.BlockSpec((None, 1, HGRN_DIM), lambda h, b, g: (h, 0, 0))]


def _hgrn_fwd(proj, lb_logits, gain3, nseq, S):
    T = proj.shape[0]
    nc = S // HGRN_CHUNK
    cg = min(8, nc)
    ng = nc // cg
    rows = cg * HGRN_CHUNK

    def body(hq_ref, hf_ref, hi_ref, hg_ref, lbl_ref, gain_ref, o_ref, yh_ref, st_ref, state):
        @pl.when(pl.program_id(2) == 0)
        def _():
            state[...] = jnp.zeros(state.shape, F32)

        lb = _sigmoid(lbl_ref[0:1, :] - lbl_ref[1:2, :])
        gain = gain_ref[...]

        def chunk(c, carry):
            rs = pl.ds(pl.multiple_of(c * HGRN_CHUNK, HGRN_CHUNK), HGRN_CHUNK)
            ch = _hgrn_chunk(hq_ref[rs, :], hf_ref[rs, :], hi_ref[rs, :], lb)
            st = state[...]
            st_ref[c] = st
            vb = ch["v"].astype(BF16)
            o = _dot(ch["qhat"].astype(BF16), st.astype(BF16), NT) + _dot(ch["A"].astype(BF16), vb)
            state[...] = st * jnp.exp(ch["glast"]) + _dot(vb, ch["kdec"].astype(BF16), TN)
            o_ref[rs, :] = o
            r = lax.rsqrt(jnp.mean(o * o, axis=-1, keepdims=True) + NORM_EPS)
            hg = hg_ref[rs, :]
            yh_ref[rs, :] = (o * r * gain * (hg * _sigmoid(hg))).astype(BF16)
            return carry

        lax.fori_loop(0, cg, chunk, 0)

    _, in_specs = _hgrn_specs(nseq, ng, rows, False)
    out_row = lambda: pl.BlockSpec((rows, HGRN_DIM), lambda h, b, g: (b * ng + g, h))
    return pl.pallas_call(
        body, name="hgrn_fwd",
        grid=(HGRN_HEADS, nseq, ng),
        in_specs=in_specs,
        out_specs=[out_row(), out_row(),
                   pl.BlockSpec((None, None, cg, HGRN_DIM, HGRN_DIM), lambda h, b, g: (b, h, g, 0, 0))],
        out_shape=[jax.ShapeDtypeStruct((T, 1024), F32),
                   jax.ShapeDtypeStruct((T, 1024), BF16),
                   jax.ShapeDtypeStruct((nseq, HGRN_HEADS, nc, HGRN_DIM, HGRN_DIM), F32)],
        scratch_shapes=[pltpu.VMEM((HGRN_DIM, HGRN_DIM), F32)],
        compiler_params=_params(("arbitrary", "arbitrary", "arbitrary")),
    )(proj, proj, proj, proj, lb_logits, gain3)


def _hgrn_bwd(proj, lb_logits, gain3, o, d_yh, states, nseq, S):
    T = proj.shape[0]
    nc = S // HGRN_CHUNK
    cg = min(8, nc)
    ng = nc // cg
    rows = cg * HGRN_CHUNK
    C = HGRN_CHUNK
    nsub = C // HGRN_SUB

    def body(hq_ref, hf_ref, hi_ref, hg_ref, lbl_ref, gain_ref, o_ref, dyh_ref, st_ref,
             dhq_ref, dhf_ref, dhi_ref, dhg_ref, dgain_ref, dlbl_ref, dstate, dlb_acc):
        b, g = pl.program_id(1), pl.program_id(2)

        @pl.when(g == 0)
        def _():
            dstate[...] = jnp.zeros(dstate.shape, F32)

        @pl.when((b == 0) & (g == 0))
        def _():
            dgain_ref[...] = jnp.zeros(dgain_ref.shape, F32)
            dlb_acc[...] = jnp.zeros(dlb_acc.shape, F32)

        lb = _sigmoid(lbl_ref[0:1, :] - lbl_ref[1:2, :])
        gain = gain_ref[...]

        def chunk(cc, carry):
            c = cg - 1 - cc
            rs = pl.ds(pl.multiple_of(c * C, C), C)
            hq, hg = hq_ref[rs, :], hg_ref[rs, :]
            ch = _hgrn_chunk(hq, hf_ref[rs, :], hi_ref[rs, :], lb)
            ov = o_ref[rs, :]
            dyh = dyh_ref[rs, :].astype(F32)
            r = lax.rsqrt(jnp.mean(ov * ov, axis=-1, keepdims=True) + NORM_EPS)
            on = ov * r
            sg = _sigmoid(hg)
            doh = dyh * (hg * sg)
            dhg_ref[rs, :] = (dyh * on * gain * (sg * (1.0 + hg * (1.0 - sg)))).astype(BF16)
            dgain_ref[...] += jnp.sum(doh * on, axis=0, keepdims=True)
            don = doh * gain
            do = r * (don - on * jnp.mean(don * on, axis=-1, keepdims=True))
            dob = do.astype(BF16)
            st = st_ref[c]
            dst = dstate[...]
            stb, dstb = st.astype(BF16), dst.astype(BF16)
            vb = ch["v"].astype(BF16)
            qhatb = ch["qhat"].astype(BF16)
            eglast = jnp.exp(ch["glast"])
            dqhat = _dot(dob, stb)
            dkdec = _dot(vb, dstb)
            dv = _dot(ch["kdec"].astype(BF16), dstb, NT)
            deg = jnp.sum(dst * st, axis=0, keepdims=True)
            dstate[...] = dst * eglast + _dot(dob, qhatb, TN)
            dA = jnp.where(ch["causal"], _dot(dob, vb, NT), 0.0)
            dv = dv + _dot(ch["A"].astype(BF16), dob, TN)
            dAb = dA.astype(BF16)
            dqs_parts, dgq_parts = [], []
            dk_intra, dgk = None, None
            for i in range(nsub):
                dA_i = dAb[HGRN_SUB * i:HGRN_SUB * (i + 1)]
                dqt = _dot(dA_i, ch["kt"][i])
                dkt = _dot(dA_i, ch["qt"][i], TN)
                dqs_parts.append(dqt * ch["eq"][i])
                dgq_parts.append(dqt * ch["qt"][i].astype(F32))
                dk_i = dkt * ch["ek"][i]
                dgk_i = dkt * ch["kt"][i].astype(F32)
                dk_intra = dk_i if dk_intra is None else dk_intra + dk_i
                dgk = dgk_i if dgk is None else dgk + dgk_i
            dqs_inter = dqhat * ch["eG"]
            dk_state = dkdec * ch["edec"]
            dqs = jnp.concatenate(dqs_parts, axis=0) + dqs_inter
            dk = dk_intra + dk_state
            dG = jnp.concatenate(dgq_parts, axis=0) - dgk + ch["qs"] * dqs_inter - ch["k"] * dk_state
            last_row = lax.broadcasted_iota(jnp.int32, (C, 1), 0) == C - 1
            tail = jnp.sum(dkdec * ch["kdec"], axis=0, keepdims=True) + deg * eglast
            dG = dG + jnp.where(last_row, tail, 0.0)
            anti = (lax.broadcasted_iota(jnp.int32, (C, C), 1)
                    >= lax.broadcasted_iota(jnp.int32, (C, C), 0)).astype(BF16)
            dlf = _tri_sum(anti, dG)
            df = dlf / ch["f"] - dk
            sf = ch["sf"]
            dhf_ref[rs, :] = (df * (1.0 - lb) * sf * (1.0 - sf)).astype(BF16)
            dlb_acc[...] += jnp.sum(df * (1.0 - sf), axis=0, keepdims=True)
            sq = ch["sq"]
            dhq_ref[rs, :] = (dqs * (sq * (1.0 + hq * (1.0 - sq)))).astype(BF16)
            dhi_ref[rs, :] = dv.astype(BF16)
            return carry

        lax.fori_loop(0, cg, chunk, 0)

        @pl.when((b == nseq - 1) & (g == ng - 1))
        def _():
            dl0 = dlb_acc[...] * lb * (1.0 - lb)
            dlbl_ref[0:1, :] = dl0
            dlbl_ref[1:2, :] = -dl0

    gi, in_specs = _hgrn_specs(nseq, ng, rows, True)
    row = lambda: pl.BlockSpec((rows, HGRN_DIM), lambda h, b, g: (b * ng + gi(g), h))
    return pl.pallas_call(
        body, name="hgrn_bwd",
        grid=(HGRN_HEADS, nseq, ng),
        in_specs=in_specs + [row(), row(),
                             pl.BlockSpec((None, None, cg, HGRN_DIM, HGRN_DIM), lambda h, b, g: (b, h, gi(g), 0, 0))],
        out_specs=[row(), row(), row(), row(),
                   pl.BlockSpec((None, 1, HGRN_DIM), lambda h, b, g: (h, 0, 0)),
                   pl.BlockSpec((2, HGRN_DIM), lambda h, b, g: (0, h))],
        out_shape=[jax.ShapeDtypeStruct((T, 1024), BF16)] * 4
                  + [jax.ShapeDtypeStruct((HGRN_HEADS, 1, HGRN_DIM), F32),
                     jax.ShapeDtypeStruct((2, HGRN_HEADS * HGRN_DIM), F32)],
        scratch_shapes=[pltpu.VMEM((HGRN_DIM, HGRN_DIM), F32), pltpu.VMEM((1, HGRN_DIM), F32)],
        compiler_params=_params(("arbitrary", "arbitrary", "arbitrary")),
    )(proj, proj, proj, proj, lb_logits, gain3, o, d_yh, states)


def _mid(proj, ya, yh, x2, tgt2, gpost, wa_t, wh_t, wout):
    T = proj.shape[0]
    tm = min(128, T)
    nt = T // tm

    def body(*refs):
        ga_refs, gh_refs = refs[0:4], refs[4:8]
        (ya_ref, yh_ref, x_ref, t_ref, gpost_ref, wa_hbm, wh_hbm, wo_hbm,
         merged_ref, dy_ref, dua_ref, duh_ref, dga_ref, dgh_ref, dya_ref, dyh_ref, dout_ref,
         loss_ref, dgpost_ref, wa, wh, wo) = refs[8:]
        i = pl.program_id(0)

        @pl.when(i == 0)
        def _():
            pltpu.sync_copy(wa_hbm, wa)
            pltpu.sync_copy(wh_hbm, wh)
            pltpu.sync_copy(wo_hbm, wo)
            loss_ref[...] = jnp.zeros(loss_ref.shape, F32)
            dgpost_ref[...] = jnp.zeros(dgpost_ref.shape, F32)

        ga = jnp.concatenate([r[...] for r in ga_refs], axis=1)
        gh = jnp.concatenate([r[...] for r in gh_refs], axis=1)
        sa, sh = _sigmoid(ga), _sigmoid(gh)
        ua = _dot(ya_ref[...], wa[...], NT)
        uh = _dot(yh_ref[...], wh[...], NT)
        merged = (sa * ua + sh * uh).astype(BF16)
        merged_ref[...] = merged
        y = _dot(merged, wo[...])
        r2 = lax.rsqrt(jnp.mean(y * y, axis=-1, keepdims=True) + NORM_EPS)
        yn = y * r2
        gpost = gpost_ref[...]
        err = x_ref[...] + yn * gpost - t_ref[...]
        loss_ref[...] += jnp.sum(err * err, axis=0, keepdims=True)
        dout = err * (1.0 / D_MODEL)
        dout_ref[...] = dout
        dgpost_ref[...] += jnp.sum(dout * yn, axis=0, keepdims=True)
        dyn = dout * gpost
        dy = (r2 * (dyn - yn * jnp.mean(dyn * yn, axis=-1, keepdims=True))).astype(BF16)
        dy_ref[...] = dy
        dm = _dot(dy, wo[...], NT)
        dua = (dm * sa).astype(BF16)
        duh = (dm * sh).astype(BF16)
        dua_ref[...] = dua
        duh_ref[...] = duh
        dga_ref[...] = (dm * ua * (sa * (1.0 - sa))).astype(BF16)
        dgh_ref[...] = (dm * uh * (sh * (1.0 - sh))).astype(BF16)
        dya_ref[...] = _dot(dua, wa[...]).astype(BF16)
        dyh_ref[...] = _dot(duh, wh[...]).astype(BF16)

    gate_spec = lambda c0, q: pl.BlockSpec((tm, 512), lambda i: (i, c0 // 512 + q))
    rowb = lambda w: pl.BlockSpec((tm, w), lambda i: (i, 0))
    vec = lambda: pl.BlockSpec((1, D_MODEL), lambda i: (0, 0))
    anyspec = lambda: pl.BlockSpec(memory_space=pl.ANY)
    out_shapes = ([jax.ShapeDtypeStruct((T, D_MODEL), BF16)] * 6
                  + [jax.ShapeDtypeStruct((T, 1024), BF16)] * 2
                  + [jax.ShapeDtypeStruct((T, D_MODEL), F32),
                     jax.ShapeDtypeStruct((1, D_MODEL), F32), jax.ShapeDtypeStruct((1, D_MODEL), F32)])
    return pl.pallas_call(
        body, name="mid",
        grid=(nt,),
        in_specs=[gate_spec(C_GA, q) for q in range(4)] + [gate_spec(C_GH, q) for q in range(4)]
                 + [rowb(1024), rowb(1024), rowb(D_MODEL), rowb(D_MODEL), vec(), anyspec(), anyspec(), anyspec()],
        out_specs=[rowb(D_MODEL)] * 6 + [rowb(1024)] * 2 + [rowb(D_MODEL), vec(), vec()],
        out_shape=out_shapes,
        scratch_shapes=[pltpu.VMEM((D_MODEL, 1024), BF16), pltpu.VMEM((D_MODEL, 1024), BF16),
                        pltpu.VMEM((D_MODEL, D_MODEL), BF16)],
        compiler_params=_params(("arbitrary",)),
    )(*([proj] * 8), ya, yh, x2, tgt2, gpost, wa_t, wh_t, wout)


def _tn_matmul(L, R, bm, bn, name):
    T, M = L.shape
    N = R.shape[1]

    def body(l_ref, r_ref, out_ref):
        out_ref[...] = _dot(l_ref[...], r_ref[...], TN).astype(BF16)

    return pl.pallas_call(
        body, name=name,
        grid=(N // bn, M // bm),
        in_specs=[pl.BlockSpec((T, bm), lambda j, i: (0, i)),
                  pl.BlockSpec((T, bn), lambda j, i: (0, j))],
        out_specs=pl.BlockSpec((bm, bn), lambda j, i: (i, j)),
        out_shape=jax.ShapeDtypeStruct((M, N), BF16),
        compiler_params=_params(("arbitrary", "arbitrary")),
    )(L, R)


def _dh_prenorm_bwd(dproj, wt_in, x2, dout, gpre):
    T = x2.shape[0]
    tm = min(512, T)
    tk = 768
    nk = IN_WIDTH // tk

    def body(dp_ref, w_ref, x_ref, dout_ref, g_ref, gx_ref, dg_ref, acc):
        i, k = pl.program_id(0), pl.program_id(1)

        @pl.when((i == 0) & (k == 0))
        def _():
            dg_ref[...] = jnp.zeros(dg_ref.shape, F32)

        part = _dot(dp_ref[...], w_ref[...])

        @pl.when(k == 0)
        def _():
            acc[...] = part

        @pl.when(k > 0)
        def _():
            acc[...] += part

        @pl.when(k == nk - 1)
        def _():
            dh = acc[...]
            x = x_ref[...]
            r = lax.rsqrt(jnp.mean(x * x, axis=-1, keepdims=True) + NORM_EPS)
            xn = x * r
            dg_ref[...] += jnp.sum(dh * xn, axis=0, keepdims=True)
            dxn = dh * g_ref[...]
            gx_ref[...] = dout_ref[...] + r * (dxn - xn * jnp.mean(dxn * xn, axis=-1, keepdims=True))

    rowb = lambda: pl.BlockSpec((tm, D_MODEL), lambda i, k: (i, 0))
    vec = lambda: pl.BlockSpec((1, D_MODEL), lambda i, k: (0, 0))
    return pl.pallas_call(
        body, name="dh_prenorm_bwd",
        grid=(T // tm, nk),
        in_specs=[pl.BlockSpec((tm, tk), lambda i, k: (i, k)),
                  pl.BlockSpec((tk, D_MODEL), lambda i, k: (k, 0)),
                  rowb(), rowb(), vec()],
        out_specs=[rowb(), vec()],
        out_shape=[jax.ShapeDtypeStruct((T, D_MODEL), F32), jax.ShapeDtypeStruct((1, D_MODEL), F32)],
        scratch_shapes=[pltpu.VMEM((tm, D_MODEL), F32)],
        compiler_params=_params(("arbitrary", "arbitrary")),
    )(dproj, wt_in, x2, dout, gpre)


def _sum_slots(recv, br, name):
    _, R, C = recv.shape

    def body(r_ref, out_ref):
        acc = r_ref[0].astype(F32)
        for s in range(1, N_DEV):
            acc = acc + r_ref[s].astype(F32)
        out_ref[...] = acc

    return pl.pallas_call(
        body, name=name,
        grid=(R // br,),
        in_specs=[pl.BlockSpec((N_DEV, br, C), lambda i: (0, i, 0))],
        out_specs=pl.BlockSpec((br, C), lambda i: (i, 0)),
        out_shape=jax.ShapeDtypeStruct((R, C), F32),
        compiler_params=_params(("arbitrary",)),
    )(recv)


def _adamw_math(w, g, m, v):
    m = ADAM_B1 * m + (1.0 - ADAM_B1) * g
    v = ADAM_B2 * v + (1.0 - ADAM_B2) * (g * g)
    m_hat = m / (1.0 - ADAM_B1 ** ADAM_STEP)
    v_hat = v / (1.0 - ADAM_B2 ** ADAM_STEP)
    delta = -ADAM_LR * (m_hat / (jnp.sqrt(v_hat) + ADAM_EPS) + ADAM_WD * w)
    return delta, m, v


def _adamw(w, g, m, v, br, name):
    R, C = w.shape

    def body(w_ref, g_ref, m_ref, v_ref, d_ref, nm_ref, nv_ref):
        d_ref[...], nm_ref[...], nv_ref[...] = _adamw_math(w_ref[...], g_ref[...], m_ref[...], v_ref[...])

    spec = lambda: pl.BlockSpec((br, C), lambda i: (i, 0))
    return pl.pallas_call(
        body, name=name,
        grid=(R // br,),
        in_specs=[spec(), spec(), spec(), spec()],
        out_specs=[spec(), spec(), spec()],
        out_shape=[jax.ShapeDtypeStruct((R, C), F32)] * 3,
        compiler_params=_params(("arbitrary",)),
    )(w, g, m, v)


def _place():
    return lax.axis_index("x"), lax.axis_index("y"), lax.axis_index("c")


def _all_gather_rows(shards):
    n = len(shards)

    def body(*refs):
        ins, outs = refs[:n], refs[n:2 * n]
        send_sems, recv_sems, local_sems = refs[2 * n:]
        x, y, c = _place()
        me, sibling = (x, y, c), (x, y, 1 - c)
        chips = [(1 - x, y), (x, 1 - y), (1 - x, 1 - y)]

        def rows(a, dev):
            r = ins[a].shape[0]
            return outs[a].at[pl.ds((4 * dev[0] + 2 * dev[1] + dev[2]) * r, r), :]

        def copy(a, k, block, to, src=None):
            return pltpu.make_async_remote_copy(
                src_ref=rows(a, block) if src is None else src, dst_ref=rows(a, block),
                send_sem=send_sems.at[a, k], recv_sem=recv_sems.at[a, k],
                device_id=to, device_id_type=pl.DeviceIdType.MESH)

        mine = [pltpu.make_async_copy(ins[a], rows(a, me), local_sems.at[a]) for a in range(n)]
        for cp in mine:
            cp.start()
        first = []
        for a in range(n):
            first.append(copy(a, 0, me, sibling, src=ins[a]))
            first += [copy(a, 1 + j, me, (*chip, c), src=ins[a]) for j, chip in enumerate(chips)]
        for cp in first:
            cp.start()
        passed = []
        for j, chip in enumerate(chips):
            for a in range(n):
                copy(a, 1 + j, (*chip, c), me).wait_recv()
                fwd = copy(a, 4 + j, (*chip, c), sibling)
                fwd.start()
                passed.append(fwd)
        for a in range(n):
            copy(a, 0, sibling, me).wait_recv()
            for j, chip in enumerate(chips):
                copy(a, 4 + j, (*chip, 1 - c), me).wait_recv()
        for cp in first + passed:
            cp.wait_send()
        for cp in mine:
            cp.wait()

    anyspec = lambda: pl.BlockSpec(memory_space=pl.ANY)
    return pl.pallas_call(
        body, name="gather_weights",
        in_specs=[anyspec() for _ in shards],
        out_specs=[anyspec() for _ in shards],
        out_shape=[jax.ShapeDtypeStruct((N_DEV * s.shape[0], s.shape[1]), s.dtype) for s in shards],
        scratch_shapes=[pltpu.SemaphoreType.DMA((n, 7)), pltpu.SemaphoreType.DMA((n, 7)),
                        pltpu.SemaphoreType.DMA((n,))],
    )(*shards)


def _scatter_rows(partials):
    n = len(partials)

    def body(*refs):
        ins, outs = refs[:n], refs[n:2 * n]
        send_sems, recv_sems, local_sems = refs[2 * n:]
        x, y, c = _place()
        my_slot = 4 * x + 2 * y + c

        def peer(k):
            return (x ^ ((k >> 2) & 1), y ^ ((k >> 1) & 1), c ^ (k & 1))

        def copy(a, k):
            r = outs[a].shape[1]
            p = peer(k)
            return pltpu.make_async_remote_copy(
                src_ref=ins[a].at[pl.ds((4 * p[0] + 2 * p[1] + p[2]) * r, r), :],
                dst_ref=outs[a].at[my_slot],
                send_sem=send_sems.at[a, k - 1], recv_sem=recv_sems.at[a, k - 1],
                device_id=p, device_id_type=pl.DeviceIdType.MESH)

        def arrival(a, k):
            r = outs[a].shape[1]
            p = peer(k)
            return pltpu.make_async_remote_copy(
                src_ref=ins[a].at[pl.ds(0, r), :],
                dst_ref=outs[a].at[4 * p[0] + 2 * p[1] + p[2]],
                send_sem=send_sems.at[a, k - 1], recv_sem=recv_sems.at[a, k - 1],
                device_id=p, device_id_type=pl.DeviceIdType.MESH)

        mine = []
        for a in range(n):
            r = outs[a].shape[1]
            mine.append(pltpu.make_async_copy(ins[a].at[pl.ds(my_slot * r, r), :], outs[a].at[my_slot],
                                              local_sems.at[a]))
        for cp in mine:
            cp.start()
        sends = [copy(a, k) for k in range(1, N_DEV) for a in range(n)]
        for cp in sends:
            cp.start()
        for k in range(1, N_DEV):
            for a in range(n):
                arrival(a, k).wait_recv()
        for cp in sends:
            cp.wait_send()
        for cp in mine:
            cp.wait()

    anyspec = lambda: pl.BlockSpec(memory_space=pl.ANY)
    return pl.pallas_call(
        body, name="scatter_grads",
        in_specs=[anyspec() for _ in partials],
        out_specs=[anyspec() for _ in partials],
        out_shape=[jax.ShapeDtypeStruct((N_DEV, p.shape[0] // N_DEV, p.shape[1]), p.dtype) for p in partials],
        scratch_shapes=[pltpu.SemaphoreType.DMA((n, 7)), pltpu.SemaphoreType.DMA((n, 7)),
                        pltpu.SemaphoreType.DMA((n,))],
    )(*partials)


def _all_reduce_small(packed):
    shape = packed.shape

    def body(in_ref, out_ref, slots, send_sems, recv_sems):
        x, y, c = _place()
        my_slot = 4 * x + 2 * y + c

        def peer(k):
            return (x ^ ((k >> 2) & 1), y ^ ((k >> 1) & 1), c ^ (k & 1))

        def copy(k):
            p = peer(k)
            return pltpu.make_async_remote_copy(
                src_ref=in_ref, dst_ref=slots.at[my_slot],
                send_sem=send_sems.at[k - 1], recv_sem=recv_sems.at[k - 1],
                device_id=p, device_id_type=pl.DeviceIdType.MESH)

        def arrival(k):
            p = peer(k)
            return pltpu.make_async_remote_copy(
                src_ref=in_ref, dst_ref=slots.at[4 * p[0] + 2 * p[1] + p[2]],
                send_sem=send_sems.at[k - 1], recv_sem=recv_sems.at[k - 1],
                device_id=p, device_id_type=pl.DeviceIdType.MESH)

        sends = [copy(k) for k in range(1, N_DEV)]
        for cp in sends:
            cp.start()
        slots[my_slot] = in_ref[...]
        for k in range(1, N_DEV):
            arrival(k).wait_recv()
        for cp in sends:
            cp.wait_send()
        acc = slots[0]
        for s in range(1, N_DEV):
            acc = acc + slots[s]
        out_ref[...] = acc

    return pl.pallas_call(
        body, name="all_reduce_small",
        in_specs=[pl.BlockSpec(memory_space=pltpu.VMEM)],
        out_specs=pl.BlockSpec(memory_space=pltpu.VMEM),
        out_shape=jax.ShapeDtypeStruct(shape, F32),
        scratch_shapes=[pltpu.VMEM((N_DEV,) + shape, F32),
                        pltpu.SemaphoreType.DMA((7,)), pltpu.SemaphoreType.DMA((7,))],
    )(packed)


def _pack_small(norm_pre, norm_post, lb_logits, hgrn_norm, rel_bias, sinks, extra=None):
    tail = [hgrn_norm.reshape(1, 1024), rel_bias.reshape(1, 512), sinks.reshape(1, 16)]
    used = 1024 + 512 + 16
    if extra is not None:
        tail.append(extra.reshape(1, 1))
        used += 1
    tail.append(jnp.zeros((1, D_MODEL - used), F32))
    rows = [norm_pre.reshape(1, D_MODEL), norm_post.reshape(1, D_MODEL), lb_logits.reshape(1, D_MODEL),
            jnp.concatenate(tail, axis=1), jnp.zeros((4, D_MODEL), F32)]
    return jnp.concatenate(rows, axis=0)


def _unpack_small(p):
    return (p[0:1], p[3, 1024:1536].reshape(REL_BUCKETS, ATTN_HEADS), p[3:4, 1536:1552],
            p[2].reshape(2, 1024), p[3, 0:1024].reshape(1, HGRN_HEADS, HGRN_DIM), p[1:2])


def _local_step(nseq, S, x2, tgt2, norm_pre, rel_bias, attn_sinks, lb_logits, hgrn_norm, norm_post,
                wt_in, wout, wa_t, wh_t):
    nb = S // ATTN_BLOCK
    bucket = jnp.asarray(_t5_bucket_table())
    gain3 = hgrn_norm.reshape(HGRN_HEADS, 1, HGRN_DIM)

    proj, h = _inproj(x2, norm_pre, wt_in)
    bias = _bias_table(rel_bias, bucket)
    ya = _attn_fwd(proj, bias, attn_sinks, nseq, nb)
    o, yh, states = _hgrn_fwd(proj, lb_logits, gain3, nseq, S)
    (merged, dy, dua, duh, dga, dgh, dya, dyh, dout, loss_cols, d_gpost) = _mid(
        proj, ya, yh, x2, tgt2, norm_post, wa_t, wh_t, wout)

    dq, dkv, dg, dbias, d_sinks = _attn_bwd(proj, bias, attn_sinks, dya, nseq, nb)
    d_rel_bias = _bias_table_bwd(dbias, bucket)
    dhq, dhf, dhi, dhg, d_gain, d_lbl = _hgrn_bwd(proj, lb_logits, gain3, o, dyh, states, nseq, S)
    dproj = jnp.concatenate([dq, dkv, dg, dhq, dhf, dhi, dhg, dga, dgh], axis=1)
    grad_x2, d_gpre = _dh_prenorm_bwd(dproj, wt_in, x2, dout, norm_pre)

    p_in = _tn_matmul(dproj, h, 768, 1024, "dw_in")
    p_out = _tn_matmul(merged, dy, 256, 1024, "dw_out")
    p_a = _tn_matmul(dua, ya, 256, 1024, "dw_branch_attn")
    p_h = _tn_matmul(duh, yh, 256, 1024, "dw_branch_hgrn")
    return grad_x2, p_in, p_out, p_a, p_h, d_gpre, d_gpost, d_lbl, d_gain, d_rel_bias, d_sinks, loss_cols


def kernel(x, norm_pre, w_in, rel_bias, attn_sinks, lb_logits, hgrn_norm, w_branch_attn, w_branch_hgrn, w_out, norm_post, loss_target, m_norm_pre, m_w_in, m_rel_bias, m_attn_sinks, m_lb_logits, m_hgrn_norm, m_w_branch_attn, m_w_branch_hgrn, m_w_out, m_norm_post, v_norm_pre, v_w_in, v_rel_bias, v_attn_sinks, v_lb_logits, v_hgrn_norm, v_w_branch_attn, v_w_branch_hgrn, v_w_out, v_norm_post):
    nseq, S, _ = x.shape
    T = nseq * S
    x2 = x.reshape(T, D_MODEL)
    tgt2 = loss_target.reshape(T, D_MODEL)

    wt_in, wout, wa_t, wh_t = _all_gather_rows([
        w_in[0].T.astype(BF16), w_out[0].astype(BF16),
        w_branch_attn[0].T.astype(BF16), w_branch_hgrn[0].T.astype(BF16)])

    (grad_x2, p_in, p_out, p_a, p_h, d_gpre, d_gpost, d_lbl, d_gain, d_rel_bias, d_sinks, loss_cols) = _local_step(
        nseq, S, x2, tgt2, norm_pre, rel_bias, attn_sinks, lb_logits, hgrn_norm, norm_post, wt_in, wout, wa_t, wh_t)

    r_in, r_out, r_a, r_h = _scatter_rows([p_in, p_out, p_a, p_h])
    g_in = _sum_slots(r_in, 192, "sum_dw_in").T
    g_out = _sum_slots(r_out, 128, "sum_dw_out")
    g_a = _sum_slots(r_a, 128, "sum_dw_branch_attn").T
    g_h = _sum_slots(r_h, 128, "sum_dw_branch_hgrn").T

    loss_part = 0.5 / D_MODEL * jnp.sum(loss_cols)
    packed = _pack_small(d_gpre, d_gpost, d_lbl, d_gain, d_rel_bias, d_sinks, extra=loss_part)
    total = _all_reduce_small(packed)
    loss = total[3, 1024 + 512 + 16]
    sm_w = _pack_small(norm_pre, norm_post, lb_logits, hgrn_norm, rel_bias, attn_sinks)
    sm_m = _pack_small(m_norm_pre, m_norm_post, m_lb_logits, m_hgrn_norm, m_rel_bias, m_attn_sinks)
    sm_v = _pack_small(v_norm_pre, v_norm_post, v_lb_logits, v_hgrn_norm, v_rel_bias, v_attn_sinks)
    sm_d, sm_nm, sm_nv = _adamw(sm_w, total, sm_m, sm_v, 8, "adamw_small")

    d_in, nm_in, nv_in = _adamw(w_in[0], g_in, m_w_in[0], v_w_in[0], 256, "adamw_w_in")
    d_out, nm_out, nv_out = _adamw(w_out[0], g_out, m_w_out[0], v_w_out[0], 128, "adamw_w_out")
    d_a, nm_a, nv_a = _adamw(w_branch_attn[0], g_a, m_w_branch_attn[0], v_w_branch_attn[0], 256, "adamw_w_branch_attn")
    d_h, nm_h, nv_h = _adamw(w_branch_hgrn[0], g_h, m_w_branch_hgrn[0], v_w_branch_hgrn[0], 256, "adamw_w_branch_hgrn")

    def group(small, big_in, big_a, big_h, big_out):
        npre, rb, sk, lbl, hn, npost = _unpack_small(small)
        return (npre, big_in[None], rb, sk, lbl, hn, big_a[None], big_h[None], big_out[None], npost)

    return (loss, grad_x2.reshape(nseq, S, D_MODEL),
            *group(total, g_in, g_a, g_h, g_out),
            *group(sm_d, d_in, d_a, d_h, d_out),
            *group(sm_nm, nm_in, nm_a, nm_h, nm_out),
            *group(sm_nv, nv_in, nv_a, nv_h, nv_out))
```

```python
import functools
import math

import numpy as np
import jax
import jax.numpy as jnp
from jax import lax
from jax.experimental import pallas as pl
from jax.experimental.pallas import tpu as pltpu

F32 = jnp.float32
BF16 = jnp.bfloat16

D_MODEL = 2048
ATTN_HEADS = 16
ATTN_HEAD_DIM = 64
ATTN_BLOCK = 128
HGRN_HEADS = 8
HGRN_DIM = 128
HGRN_CHUNK = 64
HGRN_SUB = 16
HGRN_PAR = 4
REL_BUCKETS = 32
REL_MAX_DIST = 128
NORM_EPS = 1e-6
C_AQ, C_AK, C_AV, C_AG = 0, 1024, 1280, 1536
C_HQ, C_HF, C_HI, C_HG = 2560, 3584, 4608, 5632
C_GA, C_GH = 6656, 8704
IN_WIDTH = 10752
N_DEV = 8
assert all(c0 % (HGRN_PAR * HGRN_DIM) == 0 for c0 in (C_HQ, C_HF, C_HI, C_HG))

ADAM_LR = 0.001
ADAM_B1 = 0.9
ADAM_B2 = 0.999
ADAM_EPS = 1e-08
ADAM_WD = 0.01
ADAM_STEP = 10

VMEM_LIMIT_V7X = 56 * 1024 * 1024
NEG_BIG = -1e30

NT = (((1,), (1,)), ((), ()))
TN = (((0,), (0,)), ((), ()))
NN = (((1,), (0,)), ((), ()))


def _dot(a, b, dims=NN):
    return lax.dot_general(a, b, dims, preferred_element_type=F32)


def _params(sem=None):
    return pltpu.CompilerParams(dimension_semantics=sem, vmem_limit_bytes=VMEM_LIMIT_V7X)


def _sigmoid(x):
    return 1.0 / (1.0 + jnp.exp(-x))


def _t5_bucket_table():
    qi = np.arange(ATTN_BLOCK)[:, None]
    si = np.arange(2 * ATTN_BLOCK)[None, :]
    dist = qi + ATTN_BLOCK - si
    max_exact = REL_BUCKETS // 2
    d = np.maximum(dist, 0)
    df = np.maximum(d, 1).astype(np.float32)
    large = max_exact + (np.log(df / np.float32(max_exact)).astype(np.float32)
                         / np.float32(math.log(REL_MAX_DIST / max_exact))
                         * np.float32(REL_BUCKETS - max_exact)).astype(np.int32)
    large = np.minimum(large, REL_BUCKETS - 1)
    return np.where(d < max_exact, d, large).astype(np.int32)


def _place():
    return lax.axis_index("x"), lax.axis_index("y"), lax.axis_index("c")


INPROJ_TILE = 896


def _gather_inproj(x2, gpre, shards):
    T = x2.shape[0]
    tm = min(512, T)
    nm = T // tm
    tn = INPROJ_TILE
    ntile = IN_WIDTH // tn
    n = len(shards)
    chip0 = (2 * lax.axis_index("x") + lax.axis_index("y")).astype(jnp.int32).reshape(1)

    def tile_of(j, p0):
        k = j // 3
        return 3 * (p0 ^ (((k & 1) << 1) | (k >> 1))) + j % 3

    def body(p0_ref, x_ref, g_ref, *refs):
        ins = refs[:n]
        proj_ref, h_hbm = refs[n], refs[n + 1]
        outs = refs[n + 2:2 * n + 2]
        hbuf, wtile, send_sems, recv_sems, local_sems, h_sems, w_sem = refs[2 * n + 2:]
        j, i = pl.program_id(0), pl.program_id(1)
        x, y, c = _place()
        me, sibling = (x, y, c), (x, y, 1 - c)
        chips = [(1 - x, y), (x, 1 - y), (1 - x, 1 - y)]

        def rows(a, dev):
            r = ins[a].shape[0]
            return outs[a].at[pl.ds((4 * dev[0] + 2 * dev[1] + dev[2]) * r, r), :]

        def copy(a, k, block, to, src=None):
            return pltpu.make_async_remote_copy(
                src_ref=rows(a, block) if src is None else src, dst_ref=rows(a, block),
                send_sem=send_sems.at[a, k], recv_sem=recv_sems.at[a, k],
                device_id=to, device_id_type=pl.DeviceIdType.MESH)

        def local(a):
            return pltpu.make_async_copy(ins[a], rows(a, me), local_sems.at[a])

        def first_sends(a):
            return [copy(a, 0, me, sibling, src=ins[a])] + [
                copy(a, 1 + q, me, (*chip, c), src=ins[a]) for q, chip in enumerate(chips)]

        def forward(a, q):
            return copy(a, 4 + q, (*chips[q], c), sibling)

        def h_copy(ii):
            rs = pl.ds(pl.multiple_of(ii * tm, tm), tm)
            return pltpu.make_async_copy(hbuf.at[rs, :], h_hbm.at[rs, :], h_sems.at[ii])

        @pl.when((j == 0) & (i == 0))
        def _():
            for a in range(n):
                local(a).start()
            for a in range(n):
                for cp in first_sends(a):
                    cp.start()

        @pl.when(j == 0)
        def _():
            xv = x_ref[...]
            r = lax.rsqrt(jnp.mean(xv * xv, axis=-1, keepdims=True) + NORM_EPS)
            hbuf[pl.ds(pl.multiple_of(i * tm, tm), tm), :] = (xv * r * g_ref[...]).astype(BF16)
            h_copy(i).start()

        for k in range(4):
            @pl.when((j == 3 * k) & (i == 0))
            def _(k=k):
                if k == 0:
                    local(0).wait()
                    copy(0, 0, sibling, me).wait_recv()
                else:
                    q = k - 1
                    copy(0, 1 + q, (*chips[q], c), me).wait_recv()
                    forward(0, q).start()
                    copy(0, 4 + q, (*chips[q], 1 - c), me).wait_recv()

        @pl.when(i == 0)
        def _():
            t = tile_of(j, p0_ref[0])
            cp = pltpu.make_async_copy(outs[0].at[pl.ds(pl.multiple_of(t * tn, tn), tn), :], wtile, w_sem)
            cp.start()
            cp.wait()

        hv = hbuf[pl.ds(pl.multiple_of(i * tm, tm), tm), :]
        proj_ref[...] = _dot(hv, wtile[...], NT)

        @pl.when((j == ntile - 1) & (i == nm - 1))
        def _():
            for a in range(1, n):
                local(a).wait()
                copy(a, 0, sibling, me).wait_recv()
                for q in range(3):
                    copy(a, 1 + q, (*chips[q], c), me).wait_recv()
                    forward(a, q).start()
            for a in range(1, n):
                for q in range(3):
                    copy(a, 4 + q, (*chips[q], 1 - c), me).wait_recv()
            for a in range(n):
                for cp in first_sends(a) + [forward(a, q) for q in range(3)]:
                    cp.wait_send()
            for ii in range(nm):
                h_copy(ii).wait()

    anyspec = lambda: pl.BlockSpec(memory_space=pl.ANY)
    grid_spec = pltpu.PrefetchScalarGridSpec(
        num_scalar_prefetch=1,
        grid=(ntile, nm),
        in_specs=[pl.BlockSpec((tm, D_MODEL), lambda j, i, p0: (jnp.where(j == 0, i, nm - 1), 0)),
                  pl.BlockSpec((1, D_MODEL), lambda j, i, p0: (0, 0))] + [anyspec() for _ in shards],
        out_specs=[pl.BlockSpec((tm, tn), lambda j, i, p0: (i, tile_of(j, p0[0]))), anyspec()]
                  + [anyspec() for _ in shards],
        scratch_shapes=[pltpu.VMEM((T, D_MODEL), BF16), pltpu.VMEM((tn, D_MODEL), BF16),
                        pltpu.SemaphoreType.DMA((n, 7)), pltpu.SemaphoreType.DMA((n, 7)),
                        pltpu.SemaphoreType.DMA((n,)), pltpu.SemaphoreType.DMA((nm,)),
                        pltpu.SemaphoreType.DMA],
    )
    return pl.pallas_call(
        body, name="gather_inproj",
        grid_spec=grid_spec,
        out_shape=[jax.ShapeDtypeStruct((T, IN_WIDTH), F32), jax.ShapeDtypeStruct((T, D_MODEL), BF16)]
                  + [jax.ShapeDtypeStruct((N_DEV * s.shape[0], s.shape[1]), s.dtype) for s in shards],
        compiler_params=_params(("arbitrary", "arbitrary")),
    )(chip0, x2, gpre, *shards)


def _bias_table(rel_bias, bucket):
    def body(rb_ref, bk_ref, out_ref):
        h = pl.program_id(0)
        bk = bk_ref[...]
        acc = jnp.zeros(bk.shape, F32)
        for b in range(REL_BUCKETS):
            acc = jnp.where(bk == b, rb_ref[b, h], acc)
        out_ref[...] = acc

    return pl.pallas_call(
        body, name="bias_table",
        grid=(ATTN_HEADS,),
        in_specs=[pl.BlockSpec(memory_space=pltpu.SMEM),
                  pl.BlockSpec((ATTN_BLOCK, 2 * ATTN_BLOCK), lambda h: (0, 0))],
        out_specs=pl.BlockSpec((None, ATTN_BLOCK, 2 * ATTN_BLOCK), lambda h: (h, 0, 0)),
        out_shape=jax.ShapeDtypeStruct((ATTN_HEADS, ATTN_BLOCK, 2 * ATTN_BLOCK), F32),
        compiler_params=_params(("arbitrary",)),
    )(rel_bias, bucket)


def _bias_table_bwd(dbias, bucket):
    def body(db_ref, bk_ref, out_ref):
        h = pl.program_id(0)
        bk = bk_ref[...]
        db = db_ref[...]
        for b in range(REL_BUCKETS):
            out_ref[b, h] = jnp.sum(jnp.where(bk == b, db, 0.0))

    return pl.pallas_call(
        body, name="bias_table_bwd",
        grid=(ATTN_HEADS,),
        in_specs=[pl.BlockSpec((None, ATTN_BLOCK, 2 * ATTN_BLOCK), lambda h: (h, 0, 0)),
                  pl.BlockSpec((ATTN_BLOCK, 2 * ATTN_BLOCK), lambda h: (0, 0))],
        out_specs=pl.BlockSpec(memory_space=pltpu.SMEM),
        out_shape=jax.ShapeDtypeStruct((REL_BUCKETS, ATTN_HEADS), F32),
        compiler_params=_params(("arbitrary",)),
    )(dbias, bucket)


def _attn_common(qkvg, kv_prev, blk):
    lane = lax.broadcasted_iota(jnp.int32, (1, 128), 1)
    half = (lane < ATTN_HEAD_DIM, lane >= ATTN_HEAD_DIM)
    kv_cur = qkvg[:, C_AK:C_AG]
    win = jnp.concatenate([kv_prev, kv_cur], axis=0)
    k_slab, v_slab = [], []
    for r in range(2):
        ks = win[:, 128 * r:128 * r + 128]
        vs = win[:, 256 + 128 * r:256 + 128 * r + 128]
        k_slab.append((ks, pltpu.roll(ks, ATTN_HEAD_DIM, 1)))
        v_slab.append((vs, pltpu.roll(vs, ATTN_HEAD_DIM, 1)))
    qi = lax.broadcasted_iota(jnp.int32, (ATTN_BLOCK, 2 * ATTN_BLOCK), 0)
    si = lax.broadcasted_iota(jnp.int32, (ATTN_BLOCK, 2 * ATTN_BLOCK), 1)
    valid = (si > qi) & (si <= qi + ATTN_BLOCK) & ((si >= ATTN_BLOCK) | (blk > 0))
    return half, k_slab, v_slab, valid


def _attn_head(h, qkvg, half, k_slab, v_slab, valid, bias_ref, sinks_ref):
    p, a = h // 2, h % 2
    j = h // 4
    r, aj = j // 2, j % 2
    swapped = 0 if aj == a else 1
    qm = jnp.where(half[a], qkvg[:, 128 * p:128 * p + 128], 0.0).astype(BF16)
    kk = k_slab[r][swapped]
    vv = v_slab[r][swapped]
    s = _dot(qm, kk.astype(BF16), NT) * (ATTN_HEAD_DIM ** -0.5) + bias_ref[h]
    s = jnp.where(valid, s, NEG_BIG)
    sink = sinks_ref[0, h]
    m = jnp.maximum(jnp.max(s, axis=-1, keepdims=True), sink)
    e = jnp.exp(s - m)
    es = jnp.exp(sink - m)
    inv = 1.0 / (jnp.sum(e, axis=-1, keepdims=True) + es)
    pn = e * inv
    vx = jnp.where(half[a], vv, 0.0).astype(BF16)
    o_h = _dot(pn.astype(BF16), vx)
    return dict(p=p, a=a, r=r, swapped=swapped, qm=qm, kk=kk, vv=vv, pn=pn, psink=es * inv, o_h=o_h)


def _attn_specs(nb):
    row = lambda b, i: b * nb + i
    return [
        pl.BlockSpec((ATTN_BLOCK, C_HQ), lambda b, i: (row(b, i), 0)),
        pl.BlockSpec((ATTN_BLOCK, 512), lambda b, i: (row(b, jnp.maximum(i - 1, 0)), 2)),
        pl.BlockSpec((ATTN_HEADS, ATTN_BLOCK, 2 * ATTN_BLOCK), lambda b, i: (0, 0, 0)),
        pl.BlockSpec(memory_space=pltpu.SMEM),
    ]


def _attn_fwd(proj, bias, sinks, nseq, nb):
    T = proj.shape[0]

    def body(qkvg_ref, kvp_ref, bias_ref, sinks_ref, ya_ref):
        qkvg = qkvg_ref[...]
        half, k_slab, v_slab, valid = _attn_common(qkvg, kvp_ref[...], pl.program_id(1))
        slabs = []
        for p in range(ATTN_HEADS // 2):
            o = None
            for h in (2 * p, 2 * p + 1):
                hd = _attn_head(h, qkvg, half, k_slab, v_slab, valid, bias_ref, sinks_ref)
                o = hd["o_h"] if o is None else o + hd["o_h"]
            slabs.append(o)
        o_all = jnp.concatenate(slabs, axis=1)
        g = qkvg[:, C_AG:C_HQ]
        ya_ref[...] = (o_all * (g * _sigmoid(g))).astype(BF16)

    return pl.pallas_call(
        body, name="attn_fwd",
        grid=(nseq, nb),
        in_specs=_attn_specs(nb),
        out_specs=pl.BlockSpec((ATTN_BLOCK, 1024), lambda b, i: (b * nb + i, 0)),
        out_shape=jax.ShapeDtypeStruct((T, 1024), BF16),
        compiler_params=_params(("arbitrary", "arbitrary")),
    )(proj, proj, bias, sinks)


def _attn_bwd(proj, bias, sinks, d_ya, nseq, nb):
    T = proj.shape[0]
    S = nb * ATTN_BLOCK
    scale = ATTN_HEAD_DIM ** -0.5

    def body(qkvg_ref, kvp_ref, bias_ref, sinks_ref, dya_ref, dq_ref, dkv_ref, dg_ref, dbias_ref, dsinks_ref):
        b, i = pl.program_id(0), pl.program_id(1)
        first = (b == 0) & (i == 0)

        @pl.when(first)
        def _():
            dbias_ref[...] = jnp.zeros(dbias_ref.shape, F32)
            for h in range(ATTN_HEADS):
                dsinks_ref[0, h] = 0.0

        qkvg = qkvg_ref[...]
        half, k_slab, v_slab, valid = _attn_common(qkvg, kvp_ref[...], i)
        g = qkvg[:, C_AG:C_HQ]
        sg = _sigmoid(g)
        silu_g = g * sg
        dya = dya_ref[...].astype(F32)
        do_all = dya * silu_g
        dq_slabs, o_slabs = [], []
        dk_acc = [[None, None], [None, None]]
        dv_acc = [[None, None], [None, None]]

        def add(acc, r, sw, val):
            acc[r][sw] = val if acc[r][sw] is None else acc[r][sw] + val

        for p in range(ATTN_HEADS // 2):
            dq_p, o_p = None, None
            do_slab = do_all[:, 128 * p:128 * p + 128]
            for h in (2 * p, 2 * p + 1):
                hd = _attn_head(h, qkvg, half, k_slab, v_slab, valid, bias_ref, sinks_ref)
                a, r, sw, pn = hd["a"], hd["r"], hd["swapped"], hd["pn"]
                dom = jnp.where(half[a], do_slab, 0.0)
                domb = dom.astype(BF16)
                dp = _dot(domb, hd["vv"].astype(BF16), NT)
                delta = jnp.sum(dom * hd["o_h"], axis=-1, keepdims=True)
                ds = pn * (dp - delta)
                dsinks_ref[0, h] += -jnp.sum(hd["psink"] * delta)
                dbias_ref[h] += ds
                dsb = ds.astype(BF16)
                kxm = jnp.where(half[a], hd["kk"], 0.0).astype(BF16)
                dq_h = _dot(dsb, kxm) * scale
                dq_p = dq_h if dq_p is None else dq_p + dq_h
                o_p = hd["o_h"] if o_p is None else o_p + hd["o_h"]
                add(dk_acc, r, sw, _dot(dsb, hd["qm"], TN) * scale)
                add(dv_acc, r, sw, _dot(pn.astype(BF16), domb, TN))
            dq_slabs.append(dq_p)
            o_slabs.append(o_p)

        dq_ref[...] = jnp.concatenate(dq_slabs, axis=1).astype(BF16)
        o_all = jnp.concatenate(o_slabs, axis=1)
        dg_ref[...] = (dya * o_all * (sg * (1.0 + g * (1.0 - sg)))).astype(BF16)

        pieces = []
        for acc in (dk_acc, dv_acc):
            for r in range(2):
                pieces.append(acc[r][0] + pltpu.roll(acc[r][1], ATTN_HEAD_DIM, 1))
        dkv = jnp.concatenate(pieces, axis=1)
        cur = pl.multiple_of(i * ATTN_BLOCK, ATTN_BLOCK)
        dkv_ref[pl.ds(cur, ATTN_BLOCK), :] = dkv[ATTN_BLOCK:].astype(BF16)

        @pl.when(i > 0)
        def _():
            prev = pl.multiple_of((i - 1) * ATTN_BLOCK, ATTN_BLOCK)
            old = dkv_ref[pl.ds(prev, ATTN_BLOCK), :].astype(F32)
            dkv_ref[pl.ds(prev, ATTN_BLOCK), :] = (old + dkv[:ATTN_BLOCK]).astype(BF16)

    row_spec = lambda w: pl.BlockSpec((ATTN_BLOCK, w), lambda b, i: (b * nb + i, 0))
    return pl.pallas_call(
        body, name="attn_bwd",
        grid=(nseq, nb),
        in_specs=_attn_specs(nb) + [row_spec(1024)],
        out_specs=[row_spec(1024),
                   pl.BlockSpec((S, 512), lambda b, i: (b, 0)),
                   row_spec(1024),
                   pl.BlockSpec((ATTN_HEADS, ATTN_BLOCK, 2 * ATTN_BLOCK), lambda b, i: (0, 0, 0)),
                   pl.BlockSpec(memory_space=pltpu.SMEM)],
        out_shape=[jax.ShapeDtypeStruct((T, 1024), BF16),
                   jax.ShapeDtypeStruct((T, 512), BF16),
                   jax.ShapeDtypeStruct((T, 1024), BF16),
                   jax.ShapeDtypeStruct((ATTN_HEADS, ATTN_BLOCK, 2 * ATTN_BLOCK), F32),
                   jax.ShapeDtypeStruct((1, ATTN_HEADS), F32)],
        compiler_params=_params(("arbitrary", "arbitrary")),
    )(proj, proj, bias, sinks, d_ya)


def _split3(x):
    hi = x.astype(BF16)
    r1 = x - hi.astype(F32)
    mid = r1.astype(BF16)
    lo = (r1 - mid.astype(F32)).astype(BF16)
    return jnp.concatenate([hi, mid, lo], axis=1)


def _tri_sum(tri, x):
    y = _dot(tri, _split3(x))
    return y[:, :128] + y[:, 128:256] + y[:, 256:]


def _hgrn_chunk(hq, hf, hi, lb):
    C = HGRN_CHUNK
    t = lax.broadcasted_iota(jnp.int32, (C, C), 0)
    s = lax.broadcasted_iota(jnp.int32, (C, C), 1)
    causal = s <= t
    sf = _sigmoid(hf)
    f = lb + (1.0 - lb) * sf
    G = _tri_sum(causal.astype(BF16), jnp.log(f))
    sq = _sigmoid(hq)
    qs = hq * sq
    k = 1.0 - f
    rowblk = lax.broadcasted_iota(jnp.int32, (C, 1), 0) // HGRN_SUB
    qt, kt, eq, ek = [], [], [], []
    for i in range(C // HGRN_SUB):
        lo = HGRN_SUB * i
        ref = G[lo + HGRN_SUB // 2:lo + HGRN_SUB // 2 + 1, :]
        eq_i = jnp.exp(G[lo:lo + HGRN_SUB] - ref)
        ek_i = jnp.exp(jnp.where(rowblk <= i, ref - G, 0.0))
        eq.append(eq_i)
        ek.append(ek_i)
        qt.append((qs[lo:lo + HGRN_SUB] * eq_i).astype(BF16))
        kt.append((k * ek_i).astype(BF16))
    A = jnp.concatenate([_dot(qt[i], kt[i], NT) for i in range(C // HGRN_SUB)], axis=0)
    A = jnp.where(causal, A, 0.0)
    glast = G[C - 1:C, :]
    eG = jnp.exp(G)
    edec = jnp.exp(glast - G)
    return dict(causal=causal, sf=sf, f=f, G=G, sq=sq, qs=qs, k=k, qt=qt, kt=kt, eq=eq, ek=ek, A=A,
                glast=glast, eG=eG, edec=edec, qhat=qs * eG, kdec=k * edec, v=hi)


def _hgrn_specs(nseq, ng, rows, reverse):
    W = HGRN_PAR
    gi = (lambda g: ng - 1 - g) if reverse else (lambda g: g)
    col = lambda c0: pl.BlockSpec((rows, W * HGRN_DIM), lambda h, b, g: (b * ng + gi(g), c0 // (W * HGRN_DIM) + h))
    return gi, [col(C_HQ), col(C_HF), col(C_HI), col(C_HG),
                pl.BlockSpec((2, W * HGRN_DIM), lambda h, b, g: (0, h)),
                pl.BlockSpec((W, 1, HGRN_DIM), lambda h, b, g: (h, 0, 0))]


def _hgrn_fwd(proj, lb_logits, gain3, nseq, S):
    T = proj.shape[0]
    W = HGRN_PAR
    nc = S // HGRN_CHUNK
    cg = min(8, nc)
    ng = nc // cg
    rows = cg * HGRN_CHUNK

    def body(hq_ref, hf_ref, hi_ref, hg_ref, lbl_ref, gain_ref, o_ref, yh_ref, st_ref, state):
        @pl.when(pl.program_id(2) == 0)
        def _():
            state[...] = jnp.zeros(state.shape, F32)

        lb_all = _sigmoid(lbl_ref[0:1, :] - lbl_ref[1:2, :])

        def chunk(c, carry):
            rs = pl.ds(pl.multiple_of(c * HGRN_CHUNK, HGRN_CHUNK), HGRN_CHUNK)
            for w in range(W):
                ls = slice(HGRN_DIM * w, HGRN_DIM * (w + 1))
                ch = _hgrn_chunk(hq_ref[rs, ls], hf_ref[rs, ls], hi_ref[rs, ls], lb_all[:, ls])
                st = state[w]
                st_ref[w, c] = st
                vb = ch["v"].astype(BF16)
                o = _dot(ch["qhat"].astype(BF16), st.astype(BF16), NT) + _dot(ch["A"].astype(BF16), vb)
                state[w] = st * jnp.exp(ch["glast"]) + _dot(vb, ch["kdec"].astype(BF16), TN)
                o_ref[rs, ls] = o
                r = lax.rsqrt(jnp.mean(o * o, axis=-1, keepdims=True) + NORM_EPS)
                hg = hg_ref[rs, ls]
                yh_ref[rs, ls] = (o * r * gain_ref[w] * (hg * _sigmoid(hg))).astype(BF16)
            return carry

        lax.fori_loop(0, cg, chunk, 0)

    _, in_specs = _hgrn_specs(nseq, ng, rows, False)
    out_row = lambda: pl.BlockSpec((rows, W * HGRN_DIM), lambda h, b, g: (b * ng + g, h))
    return pl.pallas_call(
        body, name="hgrn_fwd",
        grid=(HGRN_HEADS // W, nseq, ng),
        in_specs=in_specs,
        out_specs=[out_row(), out_row(),
                   pl.BlockSpec((None, W, cg, HGRN_DIM, HGRN_DIM), lambda h, b, g: (b, h, g, 0, 0))],
        out_shape=[jax.ShapeDtypeStruct((T, 1024), F32),
                   jax.ShapeDtypeStruct((T, 1024), BF16),
                   jax.ShapeDtypeStruct((nseq, HGRN_HEADS, nc, HGRN_DIM, HGRN_DIM), F32)],
        scratch_shapes=[pltpu.VMEM((W, HGRN_DIM, HGRN_DIM), F32)],
        compiler_params=_params(("arbitrary", "arbitrary", "arbitrary")),
    )(proj, proj, proj, proj, lb_logits, gain3)


def _hgrn_bwd(proj, lb_logits, gain3, o, d_yh, states, nseq, S):
    T = proj.shape[0]
    W = HGRN_PAR
    nc = S // HGRN_CHUNK
    cg = min(8, nc)
    ng = nc // cg
    rows = cg * HGRN_CHUNK
    C = HGRN_CHUNK
    nsub = C // HGRN_SUB

    def body(hq_ref, hf_ref, hi_ref, hg_ref, lbl_ref, gain_ref, o_ref, dyh_ref, st_ref,
             dhq_ref, dhf_ref, dhi_ref, dhg_ref, dgain_ref, dlbl_ref, dstate, dlb_acc):
        b, g = pl.program_id(1), pl.program_id(2)

        @pl.when(g == 0)
        def _():
            dstate[...] = jnp.zeros(dstate.shape, F32)

        @pl.when((b == 0) & (g == 0))
        def _():
            dgain_ref[...] = jnp.zeros(dgain_ref.shape, F32)
            dlb_acc[...] = jnp.zeros(dlb_acc.shape, F32)

        lb_all = _sigmoid(lbl_ref[0:1, :] - lbl_ref[1:2, :])

        def head_chunk(w, c):
            rs = pl.ds(pl.multiple_of(c * C, C), C)
            ls = slice(HGRN_DIM * w, HGRN_DIM * (w + 1))
            lb = lb_all[:, ls]
            gain = gain_ref[w]
            hq, hg = hq_ref[rs, ls], hg_ref[rs, ls]
            ch = _hgrn_chunk(hq, hf_ref[rs, ls], hi_ref[rs, ls], lb)
            ov = o_ref[rs, ls]
            dyh = dyh_ref[rs, ls].astype(F32)
            r = lax.rsqrt(jnp.mean(ov * ov, axis=-1, keepdims=True) + NORM_EPS)
            on = ov * r
            sg = _sigmoid(hg)
            doh = dyh * (hg * sg)
            dhg_ref[rs, ls] = (dyh * on * gain * (sg * (1.0 + hg * (1.0 - sg)))).astype(BF16)
            dgain_ref[w] += jnp.sum(doh * on, axis=0, keepdims=True)
            don = doh * gain
            do = r * (don - on * jnp.mean(don * on, axis=-1, keepdims=True))
            dob = do.astype(BF16)
            st = st_ref[w, c]
            dst = dstate[w]
            stb, dstb = st.astype(BF16), dst.astype(BF16)
            vb = ch["v"].astype(BF16)
            qhatb = ch["qhat"].astype(BF16)
            eglast = jnp.exp(ch["glast"])
            dqhat = _dot(dob, stb)
            dkdec = _dot(vb, dstb)
            dv = _dot(ch["kdec"].astype(BF16), dstb, NT)
            deg = jnp.sum(dst * st, axis=0, keepdims=True)
            dstate[w] = dst * eglast + _dot(dob, qhatb, TN)
            dA = jnp.where(ch["causal"], _dot(dob, vb, NT), 0.0)
            dv = dv + _dot(ch["A"].astype(BF16), dob, TN)
            dAb = dA.astype(BF16)
            dqs_parts, dgq_parts = [], []
            dk_intra, dgk = None, None
            for i in range(nsub):
                dA_i = dAb[HGRN_SUB * i:HGRN_SUB * (i + 1)]
                dqt = _dot(dA_i, ch["kt"][i])
                dkt = _dot(dA_i, ch["qt"][i], TN)
                dqs_parts.append(dqt * ch["eq"][i])
                dgq_parts.append(dqt * ch["qt"][i].astype(F32))
                dk_i = dkt * ch["ek"][i]
                dgk_i = dkt * ch["kt"][i].astype(F32)
                dk_intra = dk_i if dk_intra is None else dk_intra + dk_i
                dgk = dgk_i if dgk is None else dgk + dgk_i
            dqs_inter = dqhat * ch["eG"]
            dk_state = dkdec * ch["edec"]
            dqs = jnp.concatenate(dqs_parts, axis=0) + dqs_inter
            dk = dk_intra + dk_state
            dG = jnp.concatenate(dgq_parts, axis=0) - dgk + ch["qs"] * dqs_inter - ch["k"] * dk_state
            last_row = lax.broadcasted_iota(jnp.int32, (C, 1), 0) == C - 1
            tail = jnp.sum(dkdec * ch["kdec"], axis=0, keepdims=True) + deg * eglast
            dG = dG + jnp.where(last_row, tail, 0.0)
            anti = (lax.broadcasted_iota(jnp.int32, (C, C), 1)
                    >= lax.broadcasted_iota(jnp.int32, (C, C), 0)).astype(BF16)
            dlf = _tri_sum(anti, dG)
            df = dlf / ch["f"] - dk
            sf = ch["sf"]
            dhf_ref[rs, ls] = (df * (1.0 - lb) * sf * (1.0 - sf)).astype(BF16)
            dlb_acc[:, ls] += jnp.sum(df * (1.0 - sf), axis=0, keepdims=True)
            sq = ch["sq"]
            dhq_ref[rs, ls] = (dqs * (sq * (1.0 + hq * (1.0 - sq)))).astype(BF16)
            dhi_ref[rs, ls] = dv.astype(BF16)

        def chunk(cc, carry):
            for w in range(W):
                head_chunk(w, cg - 1 - cc)
            return carry

        lax.fori_loop(0, cg, chunk, 0)

        @pl.when((b == nseq - 1) & (g == ng - 1))
        def _():
            dl0 = dlb_acc[...] * lb_all * (1.0 - lb_all)
            dlbl_ref[0:1, :] = dl0
            dlbl_ref[1:2, :] = -dl0

    gi, in_specs = _hgrn_specs(nseq, ng, rows, True)
    row = lambda: pl.BlockSpec((rows, W * HGRN_DIM), lambda h, b, g: (b * ng + gi(g), h))
    return pl.pallas_call(
        body, name="hgrn_bwd",
        grid=(HGRN_HEADS // W, nseq, ng),
        in_specs=in_specs + [row(), row(),
                             pl.BlockSpec((None, W, cg, HGRN_DIM, HGRN_DIM), lambda h, b, g: (b, h, gi(g), 0, 0))],
        out_specs=[row(), row(), row(), row(),
                   pl.BlockSpec((W, 1, HGRN_DIM), lambda h, b, g: (h, 0, 0)),
                   pl.BlockSpec((2, W * HGRN_DIM), lambda h, b, g: (0, h))],
        out_shape=[jax.ShapeDtypeStruct((T, 1024), BF16)] * 4
                  + [jax.ShapeDtypeStruct((HGRN_HEADS, 1, HGRN_DIM), F32),
                     jax.ShapeDtypeStruct((2, HGRN_HEADS * HGRN_DIM), F32)],
        scratch_shapes=[pltpu.VMEM((W, HGRN_DIM, HGRN_DIM), F32), pltpu.VMEM((1, W * HGRN_DIM), F32)],
        compiler_params=_params(("arbitrary", "arbitrary", "arbitrary")),
    )(proj, proj, proj, proj, lb_logits, gain3, o, d_yh, states)


def _mid(proj, ya, yh, x2, tgt2, gpost, wa_t, wh_t, wout):
    T = proj.shape[0]
    tm = min(128, T)
    nt = T // tm

    def body(*refs):
        ga_refs, gh_refs = refs[0:4], refs[4:8]
        (ya_ref, yh_ref, x_ref, t_ref, gpost_ref, wa_hbm, wh_hbm, wo_hbm,
         merged_ref, dy_ref, dua_ref, duh_ref, dga_ref, dgh_ref, dya_ref, dyh_ref, dout_ref,
         loss_ref, dgpost_ref, wa, wh, wo) = refs[8:]
        i = pl.program_id(0)

        @pl.when(i == 0)
        def _():
            pltpu.sync_copy(wa_hbm, wa)
            pltpu.sync_copy(wh_hbm, wh)
            pltpu.sync_copy(wo_hbm, wo)
            loss_ref[...] = jnp.zeros(loss_ref.shape, F32)
            dgpost_ref[...] = jnp.zeros(dgpost_ref.shape, F32)

        ga = jnp.concatenate([r[...] for r in ga_refs], axis=1)
        gh = jnp.concatenate([r[...] for r in gh_refs], axis=1)
        sa, sh = _sigmoid(ga), _sigmoid(gh)
        ua = _dot(ya_ref[...], wa[...], NT)
        uh = _dot(yh_ref[...], wh[...], NT)
        merged = (sa * ua + sh * uh).astype(BF16)
        merged_ref[...] = merged
        y = _dot(merged, wo[...])
        r2 = lax.rsqrt(jnp.mean(y * y, axis=-1, keepdims=True) + NORM_EPS)
        yn = y * r2
        gpost = gpost_ref[...]
        err = x_ref[...] + yn * gpost - t_ref[...]
        loss_ref[...] += jnp.sum(err * err, axis=0, keepdims=True)
        dout = err * (1.0 / D_MODEL)
        dout_ref[...] = dout
        dgpost_ref[...] += jnp.sum(dout * yn, axis=0, keepdims=True)
        dyn = dout * gpost
        dy = (r2 * (dyn - yn * jnp.mean(dyn * yn, axis=-1, keepdims=True))).astype(BF16)
        dy_ref[...] = dy
        dm = _dot(dy, wo[...], NT)
        dua = (dm * sa).astype(BF16)
        duh = (dm * sh).astype(BF16)
        dua_ref[...] = dua
        duh_ref[...] = duh
        dga_ref[...] = (dm * ua * (sa * (1.0 - sa))).astype(BF16)
        dgh_ref[...] = (dm * uh * (sh * (1.0 - sh))).astype(BF16)
        dya_ref[...] = _dot(dua, wa[...]).astype(BF16)
        dyh_ref[...] = _dot(duh, wh[...]).astype(BF16)

    gate_spec = lambda c0, q: pl.BlockSpec((tm, 512), lambda i: (i, c0 // 512 + q))
    rowb = lambda w: pl.BlockSpec((tm, w), lambda i: (i, 0))
    vec = lambda: pl.BlockSpec((1, D_MODEL), lambda i: (0, 0))
    anyspec = lambda: pl.BlockSpec(memory_space=pl.ANY)
    out_shapes = ([jax.ShapeDtypeStruct((T, D_MODEL), BF16)] * 6
                  + [jax.ShapeDtypeStruct((T, 1024), BF16)] * 2
                  + [jax.ShapeDtypeStruct((T, D_MODEL), F32),
                     jax.ShapeDtypeStruct((1, D_MODEL), F32), jax.ShapeDtypeStruct((1, D_MODEL), F32)])
    return pl.pallas_call(
        body, name="mid",
        grid=(nt,),
        in_specs=[gate_spec(C_GA, q) for q in range(4)] + [gate_spec(C_GH, q) for q in range(4)]
                 + [rowb(1024), rowb(1024), rowb(D_MODEL), rowb(D_MODEL), vec(), anyspec(), anyspec(), anyspec()],
        out_specs=[rowb(D_MODEL)] * 6 + [rowb(1024)] * 2 + [rowb(D_MODEL), vec(), vec()],
        out_shape=out_shapes,
        scratch_shapes=[pltpu.VMEM((D_MODEL, 1024), BF16), pltpu.VMEM((D_MODEL, 1024), BF16),
                        pltpu.VMEM((D_MODEL, D_MODEL), BF16)],
        compiler_params=_params(("arbitrary",)),
    )(*([proj] * 8), ya, yh, x2, tgt2, gpost, wa_t, wh_t, wout)


def _tn_matmul(L, R, bm, bn, name):
    T, M = L.shape
    N = R.shape[1]

    def body(l_ref, r_ref, out_ref):
        out_ref[...] = _dot(l_ref[...], r_ref[...], TN).astype(BF16)

    return pl.pallas_call(
        body, name=name,
        grid=(N // bn, M // bm),
        in_specs=[pl.BlockSpec((T, bm), lambda j, i: (0, i)),
                  pl.BlockSpec((T, bn), lambda j, i: (0, j))],
        out_specs=pl.BlockSpec((bm, bn), lambda j, i: (i, j)),
        out_shape=jax.ShapeDtypeStruct((M, N), BF16),
        compiler_params=_params(("arbitrary", "arbitrary")),
    )(L, R)


def _dh_prenorm_bwd(dproj, wt_in, x2, dout, gpre, chip_sums):
    T = x2.shape[0]
    tm = min(512, T)
    tk = 768
    nk = IN_WIDTH // tk
    nt = T // tm
    n = len(chip_sums)

    def body(dp_ref, w_ref, x_ref, dout_ref, g_ref, *refs):
        ins = refs[:n]
        gx_ref, dg_ref = refs[n], refs[n + 1]
        outs = refs[n + 2:2 * n + 2]
        acc, send_sems, recv_sems, local_sems = refs[2 * n + 2:]
        i, k = pl.program_id(0), pl.program_id(1)
        x, y, c = _place()
        my_chip = 2 * x + y

        def peer(q):
            return (x ^ (q >> 1), y ^ (q & 1))

        def send(a, q):
            px, py = peer(q)
            return pltpu.make_async_remote_copy(
                src_ref=ins[a].at[2 * px + py], dst_ref=outs[a].at[my_chip],
                send_sem=send_sems.at[a, q - 1], recv_sem=recv_sems.at[a, q - 1],
                device_id=(px, py, c), device_id_type=pl.DeviceIdType.MESH)

        def arrival(a, q):
            px, py = peer(q)
            return pltpu.make_async_remote_copy(
                src_ref=ins[a].at[my_chip], dst_ref=outs[a].at[2 * px + py],
                send_sem=send_sems.at[a, q - 1], recv_sem=recv_sems.at[a, q - 1],
                device_id=(px, py, c), device_id_type=pl.DeviceIdType.MESH)

        def local(a):
            return pltpu.make_async_copy(ins[a].at[my_chip], outs[a].at[my_chip], local_sems.at[a])

        @pl.when((i == 0) & (k == 0))
        def _():
            dg_ref[...] = jnp.zeros(dg_ref.shape, F32)
            for a in range(n):
                local(a).start()
                for q in range(1, 4):
                    send(a, q).start()

        @pl.when((i == nt - 1) & (k == nk - 1))
        def _():
            for a in range(n):
                for q in range(1, 4):
                    arrival(a, q).wait_recv()
            for a in range(n):
                for q in range(1, 4):
                    send(a, q).wait_send()
                local(a).wait()

        part = _dot(dp_ref[...], w_ref[...])

        @pl.when(k == 0)
        def _():
            acc[...] = part

        @pl.when(k > 0)
        def _():
            acc[...] += part

        @pl.when(k == nk - 1)
        def _():
            dh = acc[...]
            x = x_ref[...]
            r = lax.rsqrt(jnp.mean(x * x, axis=-1, keepdims=True) + NORM_EPS)
            xn = x * r
            dg_ref[...] += jnp.sum(dh * xn, axis=0, keepdims=True)
            dxn = dh * g_ref[...]
            gx_ref[...] = dout_ref[...] + r * (dxn - xn * jnp.mean(dxn * xn, axis=-1, keepdims=True))

    rowb = lambda: pl.BlockSpec((tm, D_MODEL), lambda i, k: (i, 0))
    vec = lambda: pl.BlockSpec((1, D_MODEL), lambda i, k: (0, 0))
    anyspec = lambda: pl.BlockSpec(memory_space=pl.ANY)
    return pl.pallas_call(
        body, name="dh_prenorm_bwd",
        grid=(nt, nk),
        in_specs=[pl.BlockSpec((tm, tk), lambda i, k: (i, k)),
                  pl.BlockSpec((tk, D_MODEL), lambda i, k: (k, 0)),
                  rowb(), rowb(), vec()] + [anyspec() for _ in chip_sums],
        out_specs=[rowb(), vec()] + [anyspec() for _ in chip_sums],
        out_shape=[jax.ShapeDtypeStruct((T, D_MODEL), F32), jax.ShapeDtypeStruct((1, D_MODEL), F32)]
                  + [jax.ShapeDtypeStruct(s.shape, s.dtype) for s in chip_sums],
        scratch_shapes=[pltpu.VMEM((tm, D_MODEL), F32),
                        pltpu.SemaphoreType.DMA((n, 3)), pltpu.SemaphoreType.DMA((n, 3)),
                        pltpu.SemaphoreType.DMA((n,))],
        compiler_params=_params(("arbitrary", "arbitrary")),
    )(dproj, wt_in, x2, dout, gpre, *chip_sums)


def _sum_slots(recv, br, name):
    nslot, R, C = recv.shape

    def body(r_ref, out_ref):
        acc = r_ref[0].astype(F32)
        for s in range(1, nslot):
            acc = acc + r_ref[s].astype(F32)
        out_ref[...] = acc

    return pl.pallas_call(
        body, name=name,
        grid=(R // br,),
        in_specs=[pl.BlockSpec((nslot, br, C), lambda i: (0, i, 0))],
        out_specs=pl.BlockSpec((br, C), lambda i: (i, 0)),
        out_shape=jax.ShapeDtypeStruct((R, C), F32),
        compiler_params=_params(("arbitrary",)),
    )(recv)


def _adamw_math(w, g, m, v):
    m = ADAM_B1 * m + (1.0 - ADAM_B1) * g
    v = ADAM_B2 * v + (1.0 - ADAM_B2) * (g * g)
    m_hat = m / (1.0 - ADAM_B1 ** ADAM_STEP)
    v_hat = v / (1.0 - ADAM_B2 ** ADAM_STEP)
    delta = -ADAM_LR * (m_hat / (jnp.sqrt(v_hat) + ADAM_EPS) + ADAM_WD * w)
    return delta, m, v


def _adamw(w, g, m, v, br, name):
    R, C = w.shape

    def body(w_ref, g_ref, m_ref, v_ref, d_ref, nm_ref, nv_ref):
        d_ref[...], nm_ref[...], nv_ref[...] = _adamw_math(w_ref[...], g_ref[...], m_ref[...], v_ref[...])

    spec = lambda: pl.BlockSpec((br, C), lambda i: (i, 0))
    return pl.pallas_call(
        body, name=name,
        grid=(R // br,),
        in_specs=[spec(), spec(), spec(), spec()],
        out_specs=[spec(), spec(), spec()],
        out_shape=[jax.ShapeDtypeStruct((R, C), F32)] * 3,
        compiler_params=_params(("arbitrary",)),
    )(w, g, m, v)


def _sibling_exchange(partials):
    n = len(partials)

    def body(*refs):
        ins, outs = refs[:n], refs[n:2 * n]
        send_sems, recv_sems = refs[2 * n:]
        x, y, c = _place()

        def copy(a, p):
            return pltpu.make_async_remote_copy(
                src_ref=ins[a].at[p, 1 - c], dst_ref=outs[a].at[p],
                send_sem=send_sems.at[a, p], recv_sem=recv_sems.at[a, p],
                device_id=(x, y, 1 - c), device_id_type=pl.DeviceIdType.MESH)

        copies = [copy(a, p) for p in range(4) for a in range(n)]
        for cp in copies:
            cp.start()
        for cp in copies:
            cp.wait()

    anyspec = lambda: pl.BlockSpec(memory_space=pl.ANY)
    return pl.pallas_call(
        body, name="sibling_exchange",
        in_specs=[anyspec() for _ in partials],
        out_specs=[anyspec() for _ in partials],
        out_shape=[jax.ShapeDtypeStruct((4,) + p.shape[2:], p.dtype) for p in partials],
        scratch_shapes=[pltpu.SemaphoreType.DMA((n, 4)), pltpu.SemaphoreType.DMA((n, 4))],
    )(*partials)


def _chip_sum(partial, from_sibling, br, name):
    _, _, R, C = partial.shape
    cls = lax.axis_index("c").astype(jnp.int32).reshape(1)

    def body(c_ref, mine_ref, sib_ref, out_ref):
        out_ref[...] = (mine_ref[...].astype(F32) + sib_ref[...].astype(F32)).astype(BF16)

    grid_spec = pltpu.PrefetchScalarGridSpec(
        num_scalar_prefetch=1,
        grid=(4, R // br),
        in_specs=[pl.BlockSpec((None, None, br, C), lambda p, i, c: (p, c[0], i, 0)),
                  pl.BlockSpec((None, br, C), lambda p, i, c: (p, i, 0))],
        out_specs=pl.BlockSpec((None, br, C), lambda p, i, c: (p, i, 0)),
    )
    return pl.pallas_call(
        body, name=name, grid_spec=grid_spec,
        out_shape=jax.ShapeDtypeStruct((4, R, C), BF16),
        compiler_params=_params(("arbitrary", "arbitrary")),
    )(cls, partial, from_sibling)


def _all_reduce_small(packed):
    shape = packed.shape

    def body(in_ref, out_ref, slots, send_sems, recv_sems):
        x, y, c = _place()
        my_slot = 4 * x + 2 * y + c

        def peer(k):
            return (x ^ ((k >> 2) & 1), y ^ ((k >> 1) & 1), c ^ (k & 1))

        def copy(k):
            p = peer(k)
            return pltpu.make_async_remote_copy(
                src_ref=in_ref, dst_ref=slots.at[my_slot],
                send_sem=send_sems.at[k - 1], recv_sem=recv_sems.at[k - 1],
                device_id=p, device_id_type=pl.DeviceIdType.MESH)

        def arrival(k):
            p = peer(k)
            return pltpu.make_async_remote_copy(
                src_ref=in_ref, dst_ref=slots.at[4 * p[0] + 2 * p[1] + p[2]],
                send_sem=send_sems.at[k - 1], recv_sem=recv_sems.at[k - 1],
                device_id=p, device_id_type=pl.DeviceIdType.MESH)

        sends = [copy(k) for k in range(1, N_DEV)]
        for cp in sends:
            cp.start()
        slots[my_slot] = in_ref[...]
        for k in range(1, N_DEV):
            arrival(k).wait_recv()
        for cp in sends:
            cp.wait_send()
        acc = slots[0]
        for s in range(1, N_DEV):
            acc = acc + slots[s]
        out_ref[...] = acc

    return pl.pallas_call(
        body, name="all_reduce_small",
        in_specs=[pl.BlockSpec(memory_space=pltpu.VMEM)],
        out_specs=pl.BlockSpec(memory_space=pltpu.VMEM),
        out_shape=jax.ShapeDtypeStruct(shape, F32),
        scratch_shapes=[pltpu.VMEM((N_DEV,) + shape, F32),
                        pltpu.SemaphoreType.DMA((7,)), pltpu.SemaphoreType.DMA((7,))],
    )(packed)


def _pack_small(norm_pre, norm_post, lb_logits, hgrn_norm, rel_bias, sinks, extra=None):
    tail = [hgrn_norm.reshape(1, 1024), rel_bias.reshape(1, 512), sinks.reshape(1, 16)]
    used = 1024 + 512 + 16
    if extra is not None:
        tail.append(extra.reshape(1, 1))
        used += 1
    tail.append(jnp.zeros((1, D_MODEL - used), F32))
    rows = [norm_pre.reshape(1, D_MODEL), norm_post.reshape(1, D_MODEL), lb_logits.reshape(1, D_MODEL),
            jnp.concatenate(tail, axis=1), jnp.zeros((4, D_MODEL), F32)]
    return jnp.concatenate(rows, axis=0)


def _unpack_small(p):
    return (p[0:1], p[3, 1024:1536].reshape(REL_BUCKETS, ATTN_HEADS), p[3:4, 1536:1552],
            p[2].reshape(2, 1024), p[3, 0:1024].reshape(1, HGRN_HEADS, HGRN_DIM), p[1:2])


def _local_step(nseq, S, x2, tgt2, proj, h, rel_bias, attn_sinks, lb_logits, hgrn_norm, norm_post, wout, wa_t, wh_t):
    nb = S // ATTN_BLOCK
    bucket = jnp.asarray(_t5_bucket_table())
    gain3 = hgrn_norm.reshape(HGRN_HEADS, 1, HGRN_DIM)

    bias = _bias_table(rel_bias, bucket)
    ya = _attn_fwd(proj, bias, attn_sinks, nseq, nb)
    o, yh, states = _hgrn_fwd(proj, lb_logits, gain3, nseq, S)
    (merged, dy, dua, duh, dga, dgh, dya, dyh, dout, loss_cols, d_gpost) = _mid(
        proj, ya, yh, x2, tgt2, norm_post, wa_t, wh_t, wout)

    dq, dkv, dg, dbias, d_sinks = _attn_bwd(proj, bias, attn_sinks, dya, nseq, nb)
    d_rel_bias = _bias_table_bwd(dbias, bucket)
    dhq, dhf, dhi, dhg, d_gain, d_lbl = _hgrn_bwd(proj, lb_logits, gain3, o, dyh, states, nseq, S)
    dproj = jnp.concatenate([dq, dkv, dg, dhq, dhf, dhi, dhg, dga, dgh], axis=1)

    p_in = _tn_matmul(dproj, h, 768, 1024, "dw_in")
    p_out = _tn_matmul(merged, dy, 256, 1024, "dw_out")
    p_a = _tn_matmul(dua, ya, 256, 1024, "dw_branch_attn")
    p_h = _tn_matmul(duh, yh, 256, 1024, "dw_branch_hgrn")
    return dproj, dout, p_in, p_out, p_a, p_h, d_gpost, d_lbl, d_gain, d_rel_bias, d_sinks, loss_cols


def kernel(x, norm_pre, w_in, rel_bias, attn_sinks, lb_logits, hgrn_norm, w_branch_attn, w_branch_hgrn, w_out, norm_post, loss_target, m_norm_pre, m_w_in, m_rel_bias, m_attn_sinks, m_lb_logits, m_hgrn_norm, m_w_branch_attn, m_w_branch_hgrn, m_w_out, m_norm_post, v_norm_pre, v_w_in, v_rel_bias, v_attn_sinks, v_lb_logits, v_hgrn_norm, v_w_branch_attn, v_w_branch_hgrn, v_w_out, v_norm_post):
    nseq, S, _ = x.shape
    T = nseq * S
    x2 = x.reshape(T, D_MODEL)
    tgt2 = loss_target.reshape(T, D_MODEL)

    proj, h, wt_in, wout, wa_t, wh_t = _gather_inproj(x2, norm_pre, [
        w_in[0].T.astype(BF16), w_out[0].astype(BF16),
        w_branch_attn[0].T.astype(BF16), w_branch_hgrn[0].T.astype(BF16)])

    (dproj, dout, p_in, p_out, p_a, p_h, d_gpost, d_lbl, d_gain, d_rel_bias, d_sinks, loss_cols) = _local_step(
        nseq, S, x2, tgt2, proj, h, rel_bias, attn_sinks, lb_logits, hgrn_norm, norm_post, wout, wa_t, wh_t)

    partials = [p.reshape(4, 2, p.shape[0] // N_DEV, p.shape[1]) for p in (p_in, p_out, p_a, p_h)]
    from_sib = _sibling_exchange(partials)
    chip_sums = [_chip_sum(p, f, br, "chip_sum_" + nm) for p, f, br, nm in zip(
        partials, from_sib, (192, 128, 128, 128), ("dw_in", "dw_out", "dw_branch_attn", "dw_branch_hgrn"))]
    grad_x2, d_gpre, r_in, r_out, r_a, r_h = _dh_prenorm_bwd(dproj, wt_in, x2, dout, norm_pre, chip_sums)
    g_in = _sum_slots(r_in, 192, "sum_dw_in").T
    g_out = _sum_slots(r_out, 128, "sum_dw_out")
    g_a = _sum_slots(r_a, 128, "sum_dw_branch_attn").T
    g_h = _sum_slots(r_h, 128, "sum_dw_branch_hgrn").T

    loss_part = 0.5 / D_MODEL * jnp.sum(loss_cols)
    packed = _pack_small(d_gpre, d_gpost, d_lbl, d_gain, d_rel_bias, d_sinks, extra=loss_part)
    total = _all_reduce_small(packed)
    loss = total[3, 1024 + 512 + 16]
    sm_w = _pack_small(norm_pre, norm_post, lb_logits, hgrn_norm, rel_bias, attn_sinks)
    sm_m = _pack_small(m_norm_pre, m_norm_post, m_lb_logits, m_hgrn_norm, m_rel_bias, m_attn_sinks)
    sm_v = _pack_small(v_norm_pre, v_norm_post, v_lb_logits, v_hgrn_norm, v_rel_bias, v_attn_sinks)
    sm_d, sm_nm, sm_nv = _adamw(sm_w, total, sm_m, sm_v, 8, "adamw_small")

    d_in, nm_in, nv_in = _adamw(w_in[0], g_in, m_w_in[0], v_w_in[0], 256, "adamw_w_in")
    d_out, nm_out, nv_out = _adamw(w_out[0], g_out, m_w_out[0], v_w_out[0], 128, "adamw_w_out")
    d_a, nm_a, nv_a = _adamw(w_branch_attn[0], g_a, m_w_branch_attn[0], v_w_branch_attn[0], 256, "adamw_w_branch_attn")
    d_h, nm_h, nv_h = _adamw(w_branch_hgrn[0], g_h, m_w_branch_hgrn[0], v_w_branch_hgrn[0], 256, "adamw_w_branch_hgrn")

    def group(small, big_in, big_a, big_h, big_out):
        npre, rb, sk, lbl, hn, npost = _unpack_small(small)
        return (npre, big_in[None], rb, sk, lbl, hn, big_a[None], big_h[None], big_out[None], npost)

    return (loss, grad_x2.reshape(nseq, S, D_MODEL),
            *group(total, g_in, g_a, g_h, g_out),
            *group(sm_d, d_in, d_a, d_h, d_out),
            *group(sm_nm, nm_in, nm_a, nm_h, nm_out),
            *group(sm_nv, nv_in, nv_a, nv_h, nv_out))
```

```python
import functools
import math

import numpy as np
import jax
import jax.numpy as jnp
from jax import lax
from jax.experimental import pallas as pl
from jax.experimental.pallas import tpu as pltpu

F32 = jnp.float32
BF16 = jnp.bfloat16

D_MODEL = 2048
ATTN_HEADS = 16
ATTN_HEAD_DIM = 64
ATTN_BLOCK = 128
HGRN_HEADS = 8
HGRN_DIM = 128
HGRN_CHUNK = 64
HGRN_SUB = 16
HGRN_PAR = 4
REL_BUCKETS = 32
REL_MAX_DIST = 128
NORM_EPS = 1e-6
C_AQ, C_AK, C_AV, C_AG = 0, 1024, 1280, 1536
C_HQ, C_HF, C_HI, C_HG = 2560, 3584, 4608, 5632
C_GA, C_GH = 6656, 8704
IN_WIDTH = 10752
N_DEV = 8
assert all(c0 % (HGRN_PAR * HGRN_DIM) == 0 for c0 in (C_HQ, C_HF, C_HI, C_HG))

ADAM_LR = 0.001
ADAM_B1 = 0.9
ADAM_B2 = 0.999
ADAM_EPS = 1e-08
ADAM_WD = 0.01
ADAM_STEP = 10

VMEM_LIMIT_V7X = 56 * 1024 * 1024
NEG_BIG = -1e30

NT = (((1,), (1,)), ((), ()))
TN = (((0,), (0,)), ((), ()))
NN = (((1,), (0,)), ((), ()))


def _dot(a, b, dims=NN):
    return lax.dot_general(a, b, dims, preferred_element_type=F32)


def _params(sem=None):
    return pltpu.CompilerParams(dimension_semantics=sem, vmem_limit_bytes=VMEM_LIMIT_V7X)


def _sigmoid(x):
    return 1.0 / (1.0 + jnp.exp(-x))


def _t5_bucket_table():
    qi = np.arange(ATTN_BLOCK)[:, None]
    si = np.arange(2 * ATTN_BLOCK)[None, :]
    dist = qi + ATTN_BLOCK - si
    max_exact = REL_BUCKETS // 2
    d = np.maximum(dist, 0)
    df = np.maximum(d, 1).astype(np.float32)
    large = max_exact + (np.log(df / np.float32(max_exact)).astype(np.float32)
                         / np.float32(math.log(REL_MAX_DIST / max_exact))
                         * np.float32(REL_BUCKETS - max_exact)).astype(np.int32)
    large = np.minimum(large, REL_BUCKETS - 1)
    return np.where(d < max_exact, d, large).astype(np.int32)


def _place():
    return lax.axis_index("x"), lax.axis_index("y"), lax.axis_index("c")


class _GatherOps:
    def __init__(self, ins, outs, send_sems, recv_sems, local_sems):
        self.ins, self.outs = ins, outs
        self.send_sems, self.recv_sems, self.local_sems = send_sems, recv_sems, local_sems
        x, y, c = _place()
        self.c = c
        self.me, self.sibling = (x, y, c), (x, y, 1 - c)
        self.chips = [(1 - x, y), (x, 1 - y), (1 - x, 1 - y)]

    def _rows(self, a, dev):
        r = self.ins[a].shape[0]
        return self.outs[a].at[pl.ds((4 * dev[0] + 2 * dev[1] + dev[2]) * r, r), :]

    def _copy(self, a, k, block, to, src=None):
        return pltpu.make_async_remote_copy(
            src_ref=self._rows(a, block) if src is None else src, dst_ref=self._rows(a, block),
            send_sem=self.send_sems.at[a, k], recv_sem=self.recv_sems.at[a, k],
            device_id=to, device_id_type=pl.DeviceIdType.MESH)

    def local(self, a):
        return pltpu.make_async_copy(self.ins[a], self._rows(a, self.me), self.local_sems.at[a])

    def to_sibling(self, a):
        return self._copy(a, 0, self.me, self.sibling, src=self.ins[a])

    def to_chip(self, a, q):
        return self._copy(a, 1 + q, self.me, (*self.chips[q], self.c), src=self.ins[a])

    def forward(self, a, q):
        return self._copy(a, 4 + q, (*self.chips[q], self.c), self.sibling)

    def from_sibling(self, a):
        return self._copy(a, 0, self.sibling, self.me)

    def from_chip(self, a, q):
        return self._copy(a, 1 + q, (*self.chips[q], self.c), self.me)

    def forwarded(self, a, q):
        return self._copy(a, 4 + q, (*self.chips[q], 1 - self.c), self.me)

    def sends(self, a):
        return [self.to_sibling(a)] + [self.to_chip(a, q) for q in range(3)] + [self.forward(a, q) for q in range(3)]


_GATHER_SEMS = 7

INPROJ_TILE = 896


def _gather_inproj(x2, gpre, wt_shard):
    T = x2.shape[0]
    tm = min(512, T)
    nm = T // tm
    tn = INPROJ_TILE
    ntile = IN_WIDTH // tn
    chip0 = (2 * lax.axis_index("x") + lax.axis_index("y")).astype(jnp.int32).reshape(1)

    def tile_of(j, p0):
        k = j // 3
        return 3 * (p0 ^ (((k & 1) << 1) | (k >> 1))) + j % 3

    def body(p0_ref, x_ref, g_ref, w_in, proj_ref, h_hbm, w_out, hbuf, wtile,
             send_sems, recv_sems, local_sems, h_sems, w_sem):
        j, i = pl.program_id(0), pl.program_id(1)
        ops = _GatherOps([w_in], [w_out], send_sems, recv_sems, local_sems)

        def h_copy(ii):
            rs = pl.ds(pl.multiple_of(ii * tm, tm), tm)
            return pltpu.make_async_copy(hbuf.at[rs, :], h_hbm.at[rs, :], h_sems.at[ii])

        @pl.when((j == 0) & (i == 0))
        def _():
            ops.local(0).start()
            ops.to_sibling(0).start()
            ops.to_chip(0, 0).start()
            ops.to_chip(0, 1).start()

        @pl.when(j == 0)
        def _():
            xv = x_ref[...]
            r = lax.rsqrt(jnp.mean(xv * xv, axis=-1, keepdims=True) + NORM_EPS)
            hbuf[pl.ds(pl.multiple_of(i * tm, tm), tm), :] = (xv * r * g_ref[...]).astype(BF16)
            h_copy(i).start()

        for k in range(4):
            @pl.when((j == 3 * k) & (i == 0))
            def _(k=k):
                if k == 0:
                    ops.local(0).wait()
                    ops.from_sibling(0).wait_recv()
                else:
                    q = k - 1
                    ops.from_chip(0, q).wait_recv()
                    ops.forward(0, q).start()
                    if q == 0:
                        ops.to_chip(0, 2).start()
                    ops.forwarded(0, q).wait_recv()

        @pl.when(i == 0)
        def _():
            t = tile_of(j, p0_ref[0])
            cp = pltpu.make_async_copy(w_out.at[pl.ds(pl.multiple_of(t * tn, tn), tn), :], wtile, w_sem)
            cp.start()
            cp.wait()

        hv = hbuf[pl.ds(pl.multiple_of(i * tm, tm), tm), :]
        proj_ref[...] = _dot(hv, wtile[...], NT)

        @pl.when((j == ntile - 1) & (i == nm - 1))
        def _():
            for cp in ops.sends(0):
                cp.wait_send()
            for ii in range(nm):
                h_copy(ii).wait()

    anyspec = lambda: pl.BlockSpec(memory_space=pl.ANY)
    grid_spec = pltpu.PrefetchScalarGridSpec(
        num_scalar_prefetch=1,
        grid=(ntile, nm),
        in_specs=[pl.BlockSpec((tm, D_MODEL), lambda j, i, p0: (jnp.where(j == 0, i, nm - 1), 0)),
                  pl.BlockSpec((1, D_MODEL), lambda j, i, p0: (0, 0)), anyspec()],
        out_specs=[pl.BlockSpec((tm, tn), lambda j, i, p0: (i, tile_of(j, p0[0]))), anyspec(), anyspec()],
        scratch_shapes=[pltpu.VMEM((T, D_MODEL), BF16), pltpu.VMEM((tn, D_MODEL), BF16),
                        pltpu.SemaphoreType.DMA((1, _GATHER_SEMS)), pltpu.SemaphoreType.DMA((1, _GATHER_SEMS)),
                        pltpu.SemaphoreType.DMA((1,)), pltpu.SemaphoreType.DMA((nm,)),
                        pltpu.SemaphoreType.DMA],
    )
    return pl.pallas_call(
        body, name="gather_inproj",
        grid_spec=grid_spec,
        out_shape=[jax.ShapeDtypeStruct((T, IN_WIDTH), F32), jax.ShapeDtypeStruct((T, D_MODEL), BF16),
                   jax.ShapeDtypeStruct((N_DEV * wt_shard.shape[0], D_MODEL), BF16)],
        compiler_params=_params(("arbitrary", "arbitrary")),
    )(chip0, x2, gpre, wt_shard)


def _bias_table(rel_bias, bucket):
    def body(rb_ref, bk_ref, out_ref):
        h = pl.program_id(0)
        bk = bk_ref[...]
        acc = jnp.zeros(bk.shape, F32)
        for b in range(REL_BUCKETS):
            acc = jnp.where(bk == b, rb_ref[b, h], acc)
        out_ref[...] = acc

    return pl.pallas_call(
        body, name="bias_table",
        grid=(ATTN_HEADS,),
        in_specs=[pl.BlockSpec(memory_space=pltpu.SMEM),
                  pl.BlockSpec((ATTN_BLOCK, 2 * ATTN_BLOCK), lambda h: (0, 0))],
        out_specs=pl.BlockSpec((None, ATTN_BLOCK, 2 * ATTN_BLOCK), lambda h: (h, 0, 0)),
        out_shape=jax.ShapeDtypeStruct((ATTN_HEADS, ATTN_BLOCK, 2 * ATTN_BLOCK), F32),
        compiler_params=_params(("arbitrary",)),
    )(rel_bias, bucket)


def _bias_table_bwd(dbias, bucket):
    def body(db_ref, bk_ref, out_ref):
        h = pl.program_id(0)
        bk = bk_ref[...]
        db = db_ref[...]
        for b in range(REL_BUCKETS):
            out_ref[b, h] = jnp.sum(jnp.where(bk == b, db, 0.0))

    return pl.pallas_call(
        body, name="bias_table_bwd",
        grid=(ATTN_HEADS,),
        in_specs=[pl.BlockSpec((None, ATTN_BLOCK, 2 * ATTN_BLOCK), lambda h: (h, 0, 0)),
                  pl.BlockSpec((ATTN_BLOCK, 2 * ATTN_BLOCK), lambda h: (0, 0))],
        out_specs=pl.BlockSpec(memory_space=pltpu.SMEM),
        out_shape=jax.ShapeDtypeStruct((REL_BUCKETS, ATTN_HEADS), F32),
        compiler_params=_params(("arbitrary",)),
    )(dbias, bucket)


def _attn_common(qkvg, kv_prev, blk):
    lane = lax.broadcasted_iota(jnp.int32, (1, 128), 1)
    half = (lane < ATTN_HEAD_DIM, lane >= ATTN_HEAD_DIM)
    kv_cur = qkvg[:, C_AK:C_AG]
    win = jnp.concatenate([kv_prev, kv_cur], axis=0)
    k_slab, v_slab = [], []
    for r in range(2):
        ks = win[:, 128 * r:128 * r + 128]
        vs = win[:, 256 + 128 * r:256 + 128 * r + 128]
        k_slab.append((ks, pltpu.roll(ks, ATTN_HEAD_DIM, 1)))
        v_slab.append((vs, pltpu.roll(vs, ATTN_HEAD_DIM, 1)))
    qi = lax.broadcasted_iota(jnp.int32, (ATTN_BLOCK, 2 * ATTN_BLOCK), 0)
    si = lax.broadcasted_iota(jnp.int32, (ATTN_BLOCK, 2 * ATTN_BLOCK), 1)
    valid = (si > qi) & (si <= qi + ATTN_BLOCK) & ((si >= ATTN_BLOCK) | (blk > 0))
    return half, k_slab, v_slab, valid


def _attn_head(h, qkvg, half, k_slab, v_slab, valid, bias_ref, sinks_ref):
    p, a = h // 2, h % 2
    j = h // 4
    r, aj = j // 2, j % 2
    swapped = 0 if aj == a else 1
    qm = jnp.where(half[a], qkvg[:, 128 * p:128 * p + 128], 0.0).astype(BF16)
    kk = k_slab[r][swapped]
    vv = v_slab[r][swapped]
    s = _dot(qm, kk.astype(BF16), NT) * (ATTN_HEAD_DIM ** -0.5) + bias_ref[h]
    s = jnp.where(valid, s, NEG_BIG)
    sink = sinks_ref[0, h]
    m = jnp.maximum(jnp.max(s, axis=-1, keepdims=True), sink)
    e = jnp.exp(s - m)
    es = jnp.exp(sink - m)
    inv = 1.0 / (jnp.sum(e, axis=-1, keepdims=True) + es)
    pn = e * inv
    vx = jnp.where(half[a], vv, 0.0).astype(BF16)
    o_h = _dot(pn.astype(BF16), vx)
    return dict(p=p, a=a, r=r, swapped=swapped, qm=qm, kk=kk, vv=vv, pn=pn, psink=es * inv, o_h=o_h)


def _attn_specs(nb):
    row = lambda b, i: b * nb + i
    return [
        pl.BlockSpec((ATTN_BLOCK, C_HQ), lambda b, i: (row(b, i), 0)),
        pl.BlockSpec((ATTN_BLOCK, 512), lambda b, i: (row(b, jnp.maximum(i - 1, 0)), 2)),
        pl.BlockSpec((ATTN_HEADS, ATTN_BLOCK, 2 * ATTN_BLOCK), lambda b, i: (0, 0, 0)),
        pl.BlockSpec(memory_space=pltpu.SMEM),
    ]


def _attn_fwd(proj, bias, sinks, nseq, nb):
    T = proj.shape[0]

    def body(qkvg_ref, kvp_ref, bias_ref, sinks_ref, ya_ref):
        qkvg = qkvg_ref[...]
        half, k_slab, v_slab, valid = _attn_common(qkvg, kvp_ref[...], pl.program_id(1))
        slabs = []
        for p in range(ATTN_HEADS // 2):
            o = None
            for h in (2 * p, 2 * p + 1):
                hd = _attn_head(h, qkvg, half, k_slab, v_slab, valid, bias_ref, sinks_ref)
                o = hd["o_h"] if o is None else o + hd["o_h"]
            slabs.append(o)
        o_all = jnp.concatenate(slabs, axis=1)
        g = qkvg[:, C_AG:C_HQ]
        ya_ref[...] = (o_all * (g * _sigmoid(g))).astype(BF16)

    return pl.pallas_call(
        body, name="attn_fwd",
        grid=(nseq, nb),
        in_specs=_attn_specs(nb),
        out_specs=pl.BlockSpec((ATTN_BLOCK, 1024), lambda b, i: (b * nb + i, 0)),
        out_shape=jax.ShapeDtypeStruct((T, 1024), BF16),
        compiler_params=_params(("arbitrary", "arbitrary")),
    )(proj, proj, bias, sinks)


def _attn_bwd(proj, bias, sinks, d_ya, nseq, nb):
    T = proj.shape[0]
    S = nb * ATTN_BLOCK
    scale = ATTN_HEAD_DIM ** -0.5

    def body(qkvg_ref, kvp_ref, bias_ref, sinks_ref, dya_ref, dq_ref, dkv_ref, dg_ref, dbias_ref, dsinks_ref):
        b, i = pl.program_id(0), pl.program_id(1)
        first = (b == 0) & (i == 0)

        @pl.when(first)
        def _():
            dbias_ref[...] = jnp.zeros(dbias_ref.shape, F32)
            for h in range(ATTN_HEADS):
                dsinks_ref[0, h] = 0.0

        qkvg = qkvg_ref[...]
        half, k_slab, v_slab, valid = _attn_common(qkvg, kvp_ref[...], i)
        g = qkvg[:, C_AG:C_HQ]
        sg = _sigmoid(g)
        silu_g = g * sg
        dya = dya_ref[...].astype(F32)
        do_all = dya * silu_g
        dq_slabs, o_slabs = [], []
        dk_acc = [[None, None], [None, None]]
        dv_acc = [[None, None], [None, None]]

        def add(acc, r, sw, val):
            acc[r][sw] = val if acc[r][sw] is None else acc[r][sw] + val

        for p in range(ATTN_HEADS // 2):
            dq_p, o_p = None, None
            do_slab = do_all[:, 128 * p:128 * p + 128]
            for h in (2 * p, 2 * p + 1):
                hd = _attn_head(h, qkvg, half, k_slab, v_slab, valid, bias_ref, sinks_ref)
                a, r, sw, pn = hd["a"], hd["r"], hd["swapped"], hd["pn"]
                dom = jnp.where(half[a], do_slab, 0.0)
                domb = dom.astype(BF16)
                dp = _dot(domb, hd["vv"].astype(BF16), NT)
                delta = jnp.sum(dom * hd["o_h"], axis=-1, keepdims=True)
                ds = pn * (dp - delta)
                dsinks_ref[0, h] += -jnp.sum(hd["psink"] * delta)
                dbias_ref[h] += ds
                dsb = ds.astype(BF16)
                kxm = jnp.where(half[a], hd["kk"], 0.0).astype(BF16)
                dq_h = _dot(dsb, kxm) * scale
                dq_p = dq_h if dq_p is None else dq_p + dq_h
                o_p = hd["o_h"] if o_p is None else o_p + hd["o_h"]
                add(dk_acc, r, sw, _dot(dsb, hd["qm"], TN) * scale)
                add(dv_acc, r, sw, _dot(pn.astype(BF16), domb, TN))
            dq_slabs.append(dq_p)
            o_slabs.append(o_p)

        dq_ref[...] = jnp.concatenate(dq_slabs, axis=1).astype(BF16)
        o_all = jnp.concatenate(o_slabs, axis=1)
        dg_ref[...] = (dya * o_all * (sg * (1.0 + g * (1.0 - sg)))).astype(BF16)

        pieces = []
        for acc in (dk_acc, dv_acc):
            for r in range(2):
                pieces.append(acc[r][0] + pltpu.roll(acc[r][1], ATTN_HEAD_DIM, 1))
        dkv = jnp.concatenate(pieces, axis=1)
        cur = pl.multiple_of(i * ATTN_BLOCK, ATTN_BLOCK)
        dkv_ref[pl.ds(cur, ATTN_BLOCK), :] = dkv[ATTN_BLOCK:].astype(BF16)

        @pl.when(i > 0)
        def _():
            prev = pl.multiple_of((i - 1) * ATTN_BLOCK, ATTN_BLOCK)
            old = dkv_ref[pl.ds(prev, ATTN_BLOCK), :].astype(F32)
            dkv_ref[pl.ds(prev, ATTN_BLOCK), :] = (old + dkv[:ATTN_BLOCK]).astype(BF16)

    row_spec = lambda w: pl.BlockSpec((ATTN_BLOCK, w), lambda b, i: (b * nb + i, 0))
    return pl.pallas_call(
        body, name="attn_bwd",
        grid=(nseq, nb),
        in_specs=_attn_specs(nb) + [row_spec(1024)],
        out_specs=[row_spec(1024),
                   pl.BlockSpec((S, 512), lambda b, i: (b, 0)),
                   row_spec(1024),
                   pl.BlockSpec((ATTN_HEADS, ATTN_BLOCK, 2 * ATTN_BLOCK), lambda b, i: (0, 0, 0)),
                   pl.BlockSpec(memory_space=pltpu.SMEM)],
        out_shape=[jax.ShapeDtypeStruct((T, 1024), BF16),
                   jax.ShapeDtypeStruct((T, 512), BF16),
                   jax.ShapeDtypeStruct((T, 1024), BF16),
                   jax.ShapeDtypeStruct((ATTN_HEADS, ATTN_BLOCK, 2 * ATTN_BLOCK), F32),
                   jax.ShapeDtypeStruct((1, ATTN_HEADS), F32)],
        compiler_params=_params(("arbitrary", "arbitrary")),
    )(proj, proj, bias, sinks, d_ya)


def _split3(x):
    hi = x.astype(BF16)
    r1 = x - hi.astype(F32)
    mid = r1.astype(BF16)
    lo = (r1 - mid.astype(F32)).astype(BF16)
    return jnp.concatenate([hi, mid, lo], axis=1)


def _tri_sum(tri, x):
    y = _dot(tri, _split3(x))
    return y[:, :128] + y[:, 128:256] + y[:, 256:]


def _hgrn_chunk(hq, hf, hi, lb):
    C = HGRN_CHUNK
    t = lax.broadcasted_iota(jnp.int32, (C, C), 0)
    s = lax.broadcasted_iota(jnp.int32, (C, C), 1)
    causal = s <= t
    sf = _sigmoid(hf)
    f = lb + (1.0 - lb) * sf
    G = _tri_sum(causal.astype(BF16), jnp.log(f))
    sq = _sigmoid(hq)
    qs = hq * sq
    k = 1.0 - f
    rowblk = lax.broadcasted_iota(jnp.int32, (C, 1), 0) // HGRN_SUB
    qt, kt, eq, ek = [], [], [], []
    for i in range(C // HGRN_SUB):
        lo = HGRN_SUB * i
        ref = G[lo + HGRN_SUB // 2:lo + HGRN_SUB // 2 + 1, :]
        eq_i = jnp.exp(G[lo:lo + HGRN_SUB] - ref)
        ek_i = jnp.exp(jnp.where(rowblk <= i, ref - G, 0.0))
        eq.append(eq_i)
        ek.append(ek_i)
        qt.append((qs[lo:lo + HGRN_SUB] * eq_i).astype(BF16))
        kt.append((k * ek_i).astype(BF16))
    A = jnp.concatenate([_dot(qt[i], kt[i], NT) for i in range(C // HGRN_SUB)], axis=0)
    A = jnp.where(causal, A, 0.0)
    glast = G[C - 1:C, :]
    eG = jnp.exp(G)
    edec = jnp.exp(glast - G)
    return dict(causal=causal, sf=sf, f=f, G=G, sq=sq, qs=qs, k=k, qt=qt, kt=kt, eq=eq, ek=ek, A=A,
                glast=glast, eG=eG, edec=edec, qhat=qs * eG, kdec=k * edec, v=hi)


def _hgrn_specs(nseq, ng, rows, reverse):
    W = HGRN_PAR
    gi = (lambda g: ng - 1 - g) if reverse else (lambda g: g)
    col = lambda c0: pl.BlockSpec((rows, W * HGRN_DIM), lambda h, b, g: (b * ng + gi(g), c0 // (W * HGRN_DIM) + h))
    return gi, [col(C_HQ), col(C_HF), col(C_HI), col(C_HG),
                pl.BlockSpec((2, W * HGRN_DIM), lambda h, b, g: (0, h)),
                pl.BlockSpec((W, 1, HGRN_DIM), lambda h, b, g: (h, 0, 0))]


def _hgrn_fwd(proj, lb_logits, gain3, nseq, S, shards):
    T = proj.shape[0]
    W = HGRN_PAR
    nc = S // HGRN_CHUNK
    cg = min(8, nc)
    ng = nc // cg
    rows = cg * HGRN_CHUNK
    n = len(shards)
    nh = HGRN_HEADS // W

    def body(hq_ref, hf_ref, hi_ref, hg_ref, lbl_ref, gain_ref, *refs):
        ins = refs[:n]
        o_ref, yh_ref, st_ref = refs[n:n + 3]
        outs = refs[n + 3:2 * n + 3]
        state, send_sems, recv_sems, local_sems = refs[2 * n + 3:]
        step = (pl.program_id(0), pl.program_id(1), pl.program_id(2))
        ops = _GatherOps(ins, outs, send_sems, recv_sems, local_sems)

        @pl.when((step[0] == 0) & (step[1] == 0) & (step[2] == 0))
        def _():
            for a in range(n):
                ops.local(a).start()
                ops.to_sibling(a).start()
                for q in range(3):
                    ops.to_chip(a, q).start()

        @pl.when(pl.program_id(2) == 0)
        def _():
            state[...] = jnp.zeros(state.shape, F32)

        lb_all = _sigmoid(lbl_ref[0:1, :] - lbl_ref[1:2, :])

        def chunk(c, carry):
            rs = pl.ds(pl.multiple_of(c * HGRN_CHUNK, HGRN_CHUNK), HGRN_CHUNK)
            for w in range(W):
                ls = slice(HGRN_DIM * w, HGRN_DIM * (w + 1))
                ch = _hgrn_chunk(hq_ref[rs, ls], hf_ref[rs, ls], hi_ref[rs, ls], lb_all[:, ls])
                st = state[w]
                st_ref[w, c] = st
                vb = ch["v"].astype(BF16)
                o = _dot(ch["qhat"].astype(BF16), st.astype(BF16), NT) + _dot(ch["A"].astype(BF16), vb)
                state[w] = st * jnp.exp(ch["glast"]) + _dot(vb, ch["kdec"].astype(BF16), TN)
                o_ref[rs, ls] = o
                r = lax.rsqrt(jnp.mean(o * o, axis=-1, keepdims=True) + NORM_EPS)
                hg = hg_ref[rs, ls]
                yh_ref[rs, ls] = (o * r * gain_ref[w] * (hg * _sigmoid(hg))).astype(BF16)
            return carry

        lax.fori_loop(0, cg, chunk, 0)

        @pl.when((step[0] == nh - 1) & (step[1] == nseq - 1) & (step[2] == ng - 1))
        def _():
            for a in range(n):
                ops.local(a).wait()
                ops.from_sibling(a).wait_recv()
                for q in range(3):
                    ops.from_chip(a, q).wait_recv()
                    ops.forward(a, q).start()
            for a in range(n):
                for q in range(3):
                    ops.forwarded(a, q).wait_recv()
                for cp in ops.sends(a):
                    cp.wait_send()

    _, in_specs = _hgrn_specs(nseq, ng, rows, False)
    out_row = lambda: pl.BlockSpec((rows, W * HGRN_DIM), lambda h, b, g: (b * ng + g, h))
    anyspec = lambda: pl.BlockSpec(memory_space=pl.ANY)
    return pl.pallas_call(
        body, name="hgrn_fwd",
        grid=(nh, nseq, ng),
        in_specs=in_specs + [anyspec() for _ in shards],
        out_specs=[out_row(), out_row(),
                   pl.BlockSpec((None, W, cg, HGRN_DIM, HGRN_DIM), lambda h, b, g: (b, h, g, 0, 0))]
                  + [anyspec() for _ in shards],
        out_shape=[jax.ShapeDtypeStruct((T, 1024), F32),
                   jax.ShapeDtypeStruct((T, 1024), BF16),
                   jax.ShapeDtypeStruct((nseq, HGRN_HEADS, nc, HGRN_DIM, HGRN_DIM), F32)]
                  + [jax.ShapeDtypeStruct((N_DEV * s.shape[0], s.shape[1]), s.dtype) for s in shards],
        scratch_shapes=[pltpu.VMEM((W, HGRN_DIM, HGRN_DIM), F32),
                        pltpu.SemaphoreType.DMA((n, _GATHER_SEMS)), pltpu.SemaphoreType.DMA((n, _GATHER_SEMS)),
                        pltpu.SemaphoreType.DMA((n,))],
        compiler_params=_params(("arbitrary", "arbitrary", "arbitrary")),
    )(proj, proj, proj, proj, lb_logits, gain3, *shards)


def _hgrn_bwd(proj, lb_logits, gain3, o, d_yh, states, nseq, S):
    T = proj.shape[0]
    W = HGRN_PAR
    nc = S // HGRN_CHUNK
    cg = min(8, nc)
    ng = nc // cg
    rows = cg * HGRN_CHUNK
    C = HGRN_CHUNK
    nsub = C // HGRN_SUB

    def body(hq_ref, hf_ref, hi_ref, hg_ref, lbl_ref, gain_ref, o_ref, dyh_ref, st_ref,
             dhq_ref, dhf_ref, dhi_ref, dhg_ref, dgain_ref, dlbl_ref, dstate, dlb_acc):
        b, g = pl.program_id(1), pl.program_id(2)

        @pl.when(g == 0)
        def _():
            dstate[...] = jnp.zeros(dstate.shape, F32)

        @pl.when((b == 0) & (g == 0))
        def _():
            dgain_ref[...] = jnp.zeros(dgain_ref.shape, F32)
            dlb_acc[...] = jnp.zeros(dlb_acc.shape, F32)

        lb_all = _sigmoid(lbl_ref[0:1, :] - lbl_ref[1:2, :])

        def head_chunk(w, c):
            rs = pl.ds(pl.multiple_of(c * C, C), C)
            ls = slice(HGRN_DIM * w, HGRN_DIM * (w + 1))
            lb = lb_all[:, ls]
            gain = gain_ref[w]
            hq, hg = hq_ref[rs, ls], hg_ref[rs, ls]
            ch = _hgrn_chunk(hq, hf_ref[rs, ls], hi_ref[rs, ls], lb)
            ov = o_ref[rs, ls]
            dyh = dyh_ref[rs, ls].astype(F32)
            r = lax.rsqrt(jnp.mean(ov * ov, axis=-1, keepdims=True) + NORM_EPS)
            on = ov * r
            sg = _sigmoid(hg)
            doh = dyh * (hg * sg)
            dhg_ref[rs, ls] = (dyh * on * gain * (sg * (1.0 + hg * (1.0 - sg)))).astype(BF16)
            dgain_ref[w] += jnp.sum(doh * on, axis=0, keepdims=True)
            don = doh * gain
            do = r * (don - on * jnp.mean(don * on, axis=-1, keepdims=True))
            dob = do.astype(BF16)
            st = st_ref[w, c]
            dst = dstate[w]
            stb, dstb = st.astype(BF16), dst.astype(BF16)
            vb = ch["v"].astype(BF16)
            qhatb = ch["qhat"].astype(BF16)
            eglast = jnp.exp(ch["glast"])
            dqhat = _dot(dob, stb)
            dkdec = _dot(vb, dstb)
            dv = _dot(ch["kdec"].astype(BF16), dstb, NT)
            deg = jnp.sum(dst * st, axis=0, keepdims=True)
            dstate[w] = dst * eglast + _dot(dob, qhatb, TN)
            dA = jnp.where(ch["causal"], _dot(dob, vb, NT), 0.0)
            dv = dv + _dot(ch["A"].astype(BF16), dob, TN)
            dAb = dA.astype(BF16)
            dqs_parts, dgq_parts = [], []
            dk_intra, dgk = None, None
            for i in range(nsub):
                dA_i = dAb[HGRN_SUB * i:HGRN_SUB * (i + 1)]
                dqt = _dot(dA_i, ch["kt"][i])
                dkt = _dot(dA_i, ch["qt"][i], TN)
                dqs_parts.append(dqt * ch["eq"][i])
                dgq_parts.append(dqt * ch["qt"][i].astype(F32))
                dk_i = dkt * ch["ek"][i]
                dgk_i = dkt * ch["kt"][i].astype(F32)
                dk_intra = dk_i if dk_intra is None else dk_intra + dk_i
                dgk = dgk_i if dgk is None else dgk + dgk_i
            dqs_inter = dqhat * ch["eG"]
            dk_state = dkdec * ch["edec"]
            dqs = jnp.concatenate(dqs_parts, axis=0) + dqs_inter
            dk = dk_intra + dk_state
            dG = jnp.concatenate(dgq_parts, axis=0) - dgk + ch["qs"] * dqs_inter - ch["k"] * dk_state
            last_row = lax.broadcasted_iota(jnp.int32, (C, 1), 0) == C - 1
            tail = jnp.sum(dkdec * ch["kdec"], axis=0, keepdims=True) + deg * eglast
            dG = dG + jnp.where(last_row, tail, 0.0)
            anti = (lax.broadcasted_iota(jnp.int32, (C, C), 1)
                    >= lax.broadcasted_iota(jnp.int32, (C, C), 0)).astype(BF16)
            dlf = _tri_sum(anti, dG)
            df = dlf / ch["f"] - dk
            sf = ch["sf"]
            dhf_ref[rs, ls] = (df * (1.0 - lb) * sf * (1.0 - sf)).astype(BF16)
            dlb_acc[:, ls] += jnp.sum(df * (1.0 - sf), axis=0, keepdims=True)
            sq = ch["sq"]
            dhq_ref[rs, ls] = (dqs * (sq * (1.0 + hq * (1.0 - sq)))).astype(BF16)
            dhi_ref[rs, ls] = dv.astype(BF16)

        def chunk(cc, carry):
            for w in range(W):
                head_chunk(w, cg - 1 - cc)
            return carry

        lax.fori_loop(0, cg, chunk, 0)

        @pl.when((b == nseq - 1) & (g == ng - 1))
        def _():
            dl0 = dlb_acc[...] * lb_all * (1.0 - lb_all)
            dlbl_ref[0:1, :] = dl0
            dlbl_ref[1:2, :] = -dl0

    gi, in_specs = _hgrn_specs(nseq, ng, rows, True)
    row = lambda: pl.BlockSpec((rows, W * HGRN_DIM), lambda h, b, g: (b * ng + gi(g), h))
    return pl.pallas_call(
        body, name="hgrn_bwd",
        grid=(HGRN_HEADS // W, nseq, ng),
        in_specs=in_specs + [row(), row(),
                             pl.BlockSpec((None, W, cg, HGRN_DIM, HGRN_DIM), lambda h, b, g: (b, h, gi(g), 0, 0))],
        out_specs=[row(), row(), row(), row(),
                   pl.BlockSpec((W, 1, HGRN_DIM), lambda h, b, g: (h, 0, 0)),
                   pl.BlockSpec((2, W * HGRN_DIM), lambda h, b, g: (0, h))],
        out_shape=[jax.ShapeDtypeStruct((T, 1024), BF16)] * 4
                  + [jax.ShapeDtypeStruct((HGRN_HEADS, 1, HGRN_DIM), F32),
                     jax.ShapeDtypeStruct((2, HGRN_HEADS * HGRN_DIM), F32)],
        scratch_shapes=[pltpu.VMEM((W, HGRN_DIM, HGRN_DIM), F32), pltpu.VMEM((1, W * HGRN_DIM), F32)],
        compiler_params=_params(("arbitrary", "arbitrary", "arbitrary")),
    )(proj, proj, proj, proj, lb_logits, gain3, o, d_yh, states)


def _mid(proj, ya, yh, x2, tgt2, gpost, wa_t, wh_t, wout):
    T = proj.shape[0]
    tm = min(128, T)
    nt = T // tm

    def body(*refs):
        ga_refs, gh_refs = refs[0:4], refs[4:8]
        (ya_ref, yh_ref, x_ref, t_ref, gpost_ref, wa_hbm, wh_hbm, wo_hbm,
         merged_ref, dy_ref, dua_ref, duh_ref, dga_ref, dgh_ref, dya_ref, dyh_ref, dout_ref,
         loss_ref, dgpost_ref, wa, wh, wo) = refs[8:]
        i = pl.program_id(0)

        @pl.when(i == 0)
        def _():
            pltpu.sync_copy(wa_hbm, wa)
            pltpu.sync_copy(wh_hbm, wh)
            pltpu.sync_copy(wo_hbm, wo)
            loss_ref[...] = jnp.zeros(loss_ref.shape, F32)
            dgpost_ref[...] = jnp.zeros(dgpost_ref.shape, F32)

        ga = jnp.concatenate([r[...] for r in ga_refs], axis=1)
        gh = jnp.concatenate([r[...] for r in gh_refs], axis=1)
        sa, sh = _sigmoid(ga), _sigmoid(gh)
        ua = _dot(ya_ref[...], wa[...], NT)
        uh = _dot(yh_ref[...], wh[...], NT)
        merged = (sa * ua + sh * uh).astype(BF16)
        merged_ref[...] = merged
        y = _dot(merged, wo[...])
        r2 = lax.rsqrt(jnp.mean(y * y, axis=-1, keepdims=True) + NORM_EPS)
        yn = y * r2
        gpost = gpost_ref[...]
        err = x_ref[...] + yn * gpost - t_ref[...]
        loss_ref[...] += jnp.sum(err * err, axis=0, keepdims=True)
        dout = err * (1.0 / D_MODEL)
        dout_ref[...] = dout
        dgpost_ref[...] += jnp.sum(dout * yn, axis=0, keepdims=True)
        dyn = dout * gpost
        dy = (r2 * (dyn - yn * jnp.mean(dyn * yn, axis=-1, keepdims=True))).astype(BF16)
        dy_ref[...] = dy
        dm = _dot(dy, wo[...], NT)
        dua = (dm * sa).astype(BF16)
        duh = (dm * sh).astype(BF16)
        dua_ref[...] = dua
        duh_ref[...] = duh
        dga_ref[...] = (dm * ua * (sa * (1.0 - sa))).astype(BF16)
        dgh_ref[...] = (dm * uh * (sh * (1.0 - sh))).astype(BF16)
        dya_ref[...] = _dot(dua, wa[...]).astype(BF16)
        dyh_ref[...] = _dot(duh, wh[...]).astype(BF16)

    gate_spec = lambda c0, q: pl.BlockSpec((tm, 512), lambda i: (i, c0 // 512 + q))
    rowb = lambda w: pl.BlockSpec((tm, w), lambda i: (i, 0))
    vec = lambda: pl.BlockSpec((1, D_MODEL), lambda i: (0, 0))
    anyspec = lambda: pl.BlockSpec(memory_space=pl.ANY)
    out_shapes = ([jax.ShapeDtypeStruct((T, D_MODEL), BF16)] * 6
                  + [jax.ShapeDtypeStruct((T, 1024), BF16)] * 2
                  + [jax.ShapeDtypeStruct((T, D_MODEL), F32),
                     jax.ShapeDtypeStruct((1, D_MODEL), F32), jax.ShapeDtypeStruct((1, D_MODEL), F32)])
    return pl.pallas_call(
        body, name="mid",
        grid=(nt,),
        in_specs=[gate_spec(C_GA, q) for q in range(4)] + [gate_spec(C_GH, q) for q in range(4)]
                 + [rowb(1024), rowb(1024), rowb(D_MODEL), rowb(D_MODEL), vec(), anyspec(), anyspec(), anyspec()],
        out_specs=[rowb(D_MODEL)] * 6 + [rowb(1024)] * 2 + [rowb(D_MODEL), vec(), vec()],
        out_shape=out_shapes,
        scratch_shapes=[pltpu.VMEM((D_MODEL, 1024), BF16), pltpu.VMEM((D_MODEL, 1024), BF16),
                        pltpu.VMEM((D_MODEL, D_MODEL), BF16)],
        compiler_params=_params(("arbitrary",)),
    )(*([proj] * 8), ya, yh, x2, tgt2, gpost, wa_t, wh_t, wout)


def _tn_matmul(L, R, bm, bn, name):
    T, M = L.shape
    N = R.shape[1]

    def body(l_ref, r_ref, out_ref):
        out_ref[...] = _dot(l_ref[...], r_ref[...], TN).astype(BF16)

    return pl.pallas_call(
        body, name=name,
        grid=(N // bn, M // bm),
        in_specs=[pl.BlockSpec((T, bm), lambda j, i: (0, i)),
                  pl.BlockSpec((T, bn), lambda j, i: (0, j))],
        out_specs=pl.BlockSpec((bm, bn), lambda j, i: (i, j)),
        out_shape=jax.ShapeDtypeStruct((M, N), BF16),
        compiler_params=_params(("arbitrary", "arbitrary")),
    )(L, R)


def _dh_prenorm_bwd(dproj, wt_in, x2, dout, gpre, chip_sums):
    T = x2.shape[0]
    tm = min(1024, T)
    ne = 2
    te = tm // ne
    tk = 768
    nk = IN_WIDTH // tk
    nt = T // tm
    n = len(chip_sums)

    def body(dp_ref, w_ref, x_ref, dout_ref, g_ref, *refs):
        ins = refs[:n]
        gx_ref, dg_ref = refs[n], refs[n + 1]
        outs = refs[n + 2:2 * n + 2]
        acc, send_sems, recv_sems, local_sems = refs[2 * n + 2:]
        i, k = pl.program_id(0), pl.program_id(1)
        x, y, c = _place()
        my_chip = 2 * x + y

        def peer(q):
            return (x ^ (q >> 1), y ^ (q & 1))

        def send(a, q):
            px, py = peer(q)
            return pltpu.make_async_remote_copy(
                src_ref=ins[a].at[2 * px + py], dst_ref=outs[a].at[my_chip],
                send_sem=send_sems.at[a, q - 1], recv_sem=recv_sems.at[a, q - 1],
                device_id=(px, py, c), device_id_type=pl.DeviceIdType.MESH)

        def arrival(a, q):
            px, py = peer(q)
            return pltpu.make_async_remote_copy(
                src_ref=ins[a].at[my_chip], dst_ref=outs[a].at[2 * px + py],
                send_sem=send_sems.at[a, q - 1], recv_sem=recv_sems.at[a, q - 1],
                device_id=(px, py, c), device_id_type=pl.DeviceIdType.MESH)

        def local(a):
            return pltpu.make_async_copy(ins[a].at[my_chip], outs[a].at[my_chip], local_sems.at[a])

        @pl.when((i == 0) & (k == 0))
        def _():
            dg_ref[...] = jnp.zeros(dg_ref.shape, F32)
            for a in range(n):
                local(a).start()
                for q in range(1, 4):
                    send(a, q).start()

        @pl.when((i == nt - 1) & (k == nk + ne - 1))
        def _():
            for a in range(n):
                for q in range(1, 4):
                    arrival(a, q).wait_recv()
            for a in range(n):
                for q in range(1, 4):
                    send(a, q).wait_send()
                local(a).wait()

        @pl.when(k == 0)
        def _():
            acc[...] = _dot(dp_ref[...], w_ref[...])

        @pl.when((k > 0) & (k < nk))
        def _():
            acc[...] += _dot(dp_ref[...], w_ref[...])

        @pl.when(k >= nk)
        def _():
            dh = acc[pl.ds(pl.multiple_of((k - nk) * te, te), te), :]
            x = x_ref[...]
            r = lax.rsqrt(jnp.mean(x * x, axis=-1, keepdims=True) + NORM_EPS)
            xn = x * r
            dg_ref[...] += jnp.sum(dh * xn, axis=0, keepdims=True)
            dxn = dh * g_ref[...]
            gx_ref[...] = dout_ref[...] + r * (dxn - xn * jnp.mean(dxn * xn, axis=-1, keepdims=True))

    rowb = lambda: pl.BlockSpec((te, D_MODEL), lambda i, k: (ne * i + jnp.clip(k - nk, 0, ne - 1), 0))
    vec = lambda: pl.BlockSpec((1, D_MODEL), lambda i, k: (0, 0))
    anyspec = lambda: pl.BlockSpec(memory_space=pl.ANY)
    return pl.pallas_call(
        body, name="dh_prenorm_bwd",
        grid=(nt, nk + ne),
        in_specs=[pl.BlockSpec((tm, tk), lambda i, k: (i, jnp.minimum(k, nk - 1))),
                  pl.BlockSpec((tk, D_MODEL), lambda i, k: (jnp.minimum(k, nk - 1), 0)),
                  rowb(), rowb(), vec()] + [anyspec() for _ in chip_sums],
        out_specs=[rowb(), vec()] + [anyspec() for _ in chip_sums],
        out_shape=[jax.ShapeDtypeStruct((T, D_MODEL), F32), jax.ShapeDtypeStruct((1, D_MODEL), F32)]
                  + [jax.ShapeDtypeStruct(s.shape, s.dtype) for s in chip_sums],
        scratch_shapes=[pltpu.VMEM((tm, D_MODEL), F32),
                        pltpu.SemaphoreType.DMA((n, 3)), pltpu.SemaphoreType.DMA((n, 3)),
                        pltpu.SemaphoreType.DMA((n,))],
        compiler_params=_params(("arbitrary", "arbitrary")),
    )(dproj, wt_in, x2, dout, gpre, *chip_sums)


def _sum_slots(recv, br, name):
    nslot, R, C = recv.shape

    def body(r_ref, out_ref):
        acc = r_ref[0].astype(F32)
        for s in range(1, nslot):
            acc = acc + r_ref[s].astype(F32)
        out_ref[...] = acc

    return pl.pallas_call(
        body, name=name,
        grid=(R // br,),
        in_specs=[pl.BlockSpec((nslot, br, C), lambda i: (0, i, 0))],
        out_specs=pl.BlockSpec((br, C), lambda i: (i, 0)),
        out_shape=jax.ShapeDtypeStruct((R, C), F32),
        compiler_params=_params(("arbitrary",)),
    )(recv)


def _adamw_math(w, g, m, v):
    m = ADAM_B1 * m + (1.0 - ADAM_B1) * g
    v = ADAM_B2 * v + (1.0 - ADAM_B2) * (g * g)
    m_hat = m / (1.0 - ADAM_B1 ** ADAM_STEP)
    v_hat = v / (1.0 - ADAM_B2 ** ADAM_STEP)
    delta = -ADAM_LR * (m_hat / (jnp.sqrt(v_hat) + ADAM_EPS) + ADAM_WD * w)
    return delta, m, v


def _adamw(w, g, m, v, br, name):
    R, C = w.shape

    def body(w_ref, g_ref, m_ref, v_ref, d_ref, nm_ref, nv_ref):
        d_ref[...], nm_ref[...], nv_ref[...] = _adamw_math(w_ref[...], g_ref[...], m_ref[...], v_ref[...])

    spec = lambda: pl.BlockSpec((br, C), lambda i: (i, 0))
    return pl.pallas_call(
        body, name=name,
        grid=(R // br,),
        in_specs=[spec(), spec(), spec(), spec()],
        out_specs=[spec(), spec(), spec()],
        out_shape=[jax.ShapeDtypeStruct((R, C), F32)] * 3,
        compiler_params=_params(("arbitrary",)),
    )(w, g, m, v)


def _sibling_exchange(partials):
    n = len(partials)

    def body(*refs):
        ins, outs = refs[:n], refs[n:2 * n]
        send_sems, recv_sems = refs[2 * n:]
        x, y, c = _place()

        def copy(a, p):
            return pltpu.make_async_remote_copy(
                src_ref=ins[a].at[p, 1 - c], dst_ref=outs[a].at[p],
                send_sem=send_sems.at[a, p], recv_sem=recv_sems.at[a, p],
                device_id=(x, y, 1 - c), device_id_type=pl.DeviceIdType.MESH)

        copies = [copy(a, p) for p in range(4) for a in range(n)]
        for cp in copies:
            cp.start()
        for cp in copies:
            cp.wait()

    anyspec = lambda: pl.BlockSpec(memory_space=pl.ANY)
    return pl.pallas_call(
        body, name="sibling_exchange",
        in_specs=[anyspec() for _ in partials],
        out_specs=[anyspec() for _ in partials],
        out_shape=[jax.ShapeDtypeStruct((4,) + p.shape[2:], p.dtype) for p in partials],
        scratch_shapes=[pltpu.SemaphoreType.DMA((n, 4)), pltpu.SemaphoreType.DMA((n, 4))],
    )(*partials)


def _chip_sum(partial, from_sibling, br, name):
    _, _, R, C = partial.shape
    cls = lax.axis_index("c").astype(jnp.int32).reshape(1)

    def body(c_ref, mine_ref, sib_ref, out_ref):
        out_ref[...] = (mine_ref[...].astype(F32) + sib_ref[...].astype(F32)).astype(BF16)

    grid_spec = pltpu.PrefetchScalarGridSpec(
        num_scalar_prefetch=1,
        grid=(4, R // br),
        in_specs=[pl.BlockSpec((None, None, br, C), lambda p, i, c: (p, c[0], i, 0)),
                  pl.BlockSpec((None, br, C), lambda p, i, c: (p, i, 0))],
        out_specs=pl.BlockSpec((None, br, C), lambda p, i, c: (p, i, 0)),
    )
    return pl.pallas_call(
        body, name=name, grid_spec=grid_spec,
        out_shape=jax.ShapeDtypeStruct((4, R, C), BF16),
        compiler_params=_params(("arbitrary", "arbitrary")),
    )(cls, partial, from_sibling)


def _all_reduce_small(packed):
    shape = packed.shape

    def body(in_ref, out_ref, slots, send_sems, recv_sems):
        x, y, c = _place()
        my_slot = 4 * x + 2 * y + c

        def peer(k):
            return (x ^ ((k >> 2) & 1), y ^ ((k >> 1) & 1), c ^ (k & 1))

        def copy(k):
            p = peer(k)
            return pltpu.make_async_remote_copy(
                src_ref=in_ref, dst_ref=slots.at[my_slot],
                send_sem=send_sems.at[k - 1], recv_sem=recv_sems.at[k - 1],
                device_id=p, device_id_type=pl.DeviceIdType.MESH)

        def arrival(k):
            p = peer(k)
            return pltpu.make_async_remote_copy(
                src_ref=in_ref, dst_ref=slots.at[4 * p[0] + 2 * p[1] + p[2]],
                send_sem=send_sems.at[k - 1], recv_sem=recv_sems.at[k - 1],
                device_id=p, device_id_type=pl.DeviceIdType.MESH)

        sends = [copy(k) for k in range(1, N_DEV)]
        for cp in sends:
            cp.start()
        slots[my_slot] = in_ref[...]
        for k in range(1, N_DEV):
            arrival(k).wait_recv()
        for cp in sends:
            cp.wait_send()
        acc = slots[0]
        for s in range(1, N_DEV):
            acc = acc + slots[s]
        out_ref[...] = acc

    return pl.pallas_call(
        body, name="all_reduce_small",
        in_specs=[pl.BlockSpec(memory_space=pltpu.VMEM)],
        out_specs=pl.BlockSpec(memory_space=pltpu.VMEM),
        out_shape=jax.ShapeDtypeStruct(shape, F32),
        scratch_shapes=[pltpu.VMEM((N_DEV,) + shape, F32),
                        pltpu.SemaphoreType.DMA((7,)), pltpu.SemaphoreType.DMA((7,))],
    )(packed)


def _pack_small(norm_pre, norm_post, lb_logits, hgrn_norm, rel_bias, sinks, extra=None):
    tail = [hgrn_norm.reshape(1, 1024), rel_bias.reshape(1, 512), sinks.reshape(1, 16)]
    used = 1024 + 512 + 16
    if extra is not None:
        tail.append(extra.reshape(1, 1))
        used += 1
    tail.append(jnp.zeros((1, D_MODEL - used), F32))
    rows = [norm_pre.reshape(1, D_MODEL), norm_post.reshape(1, D_MODEL), lb_logits.reshape(1, D_MODEL),
            jnp.concatenate(tail, axis=1), jnp.zeros((4, D_MODEL), F32)]
    return jnp.concatenate(rows, axis=0)


def _unpack_small(p):
    return (p[0:1], p[3, 1024:1536].reshape(REL_BUCKETS, ATTN_HEADS), p[3:4, 1536:1552],
            p[2].reshape(2, 1024), p[3, 0:1024].reshape(1, HGRN_HEADS, HGRN_DIM), p[1:2])


def _local_step(nseq, S, x2, tgt2, proj, h, rel_bias, attn_sinks, lb_logits, hgrn_norm, norm_post, shards):
    nb = S // ATTN_BLOCK
    bucket = jnp.asarray(_t5_bucket_table())
    gain3 = hgrn_norm.reshape(HGRN_HEADS, 1, HGRN_DIM)

    bias = _bias_table(rel_bias, bucket)
    ya = _attn_fwd(proj, bias, attn_sinks, nseq, nb)
    o, yh, states, wout, wa_t, wh_t = _hgrn_fwd(proj, lb_logits, gain3, nseq, S, shards)
    (merged, dy, dua, duh, dga, dgh, dya, dyh, dout, loss_cols, d_gpost) = _mid(
        proj, ya, yh, x2, tgt2, norm_post, wa_t, wh_t, wout)

    dq, dkv, dg, dbias, d_sinks = _attn_bwd(proj, bias, attn_sinks, dya, nseq, nb)
    d_rel_bias = _bias_table_bwd(dbias, bucket)
    dhq, dhf, dhi, dhg, d_gain, d_lbl = _hgrn_bwd(proj, lb_logits, gain3, o, dyh, states, nseq, S)
    dproj = jnp.concatenate([dq, dkv, dg, dhq, dhf, dhi, dhg, dga, dgh], axis=1)

    p_in = _tn_matmul(dproj, h, 768, 1024, "dw_in")
    p_out = _tn_matmul(merged, dy, 256, 1024, "dw_out")
    p_a = _tn_matmul(dua, ya, 256, 1024, "dw_branch_attn")
    p_h = _tn_matmul(duh, yh, 256, 1024, "dw_branch_hgrn")
    return dproj, dout, p_in, p_out, p_a, p_h, d_gpost, d_lbl, d_gain, d_rel_bias, d_sinks, loss_cols


def kernel(x, norm_pre, w_in, rel_bias, attn_sinks, lb_logits, hgrn_norm, w_branch_attn, w_branch_hgrn, w_out, norm_post, loss_target, m_norm_pre, m_w_in, m_rel_bias, m_attn_sinks, m_lb_logits, m_hgrn_norm, m_w_branch_attn, m_w_branch_hgrn, m_w_out, m_norm_post, v_norm_pre, v_w_in, v_rel_bias, v_attn_sinks, v_lb_logits, v_hgrn_norm, v_w_branch_attn, v_w_branch_hgrn, v_w_out, v_norm_post):
    nseq, S, _ = x.shape
    T = nseq * S
    x2 = x.reshape(T, D_MODEL)
    tgt2 = loss_target.reshape(T, D_MODEL)

    proj, h, wt_in = _gather_inproj(x2, norm_pre, w_in[0].T.astype(BF16))
    shards = [w_out[0].astype(BF16), w_branch_attn[0].T.astype(BF16), w_branch_hgrn[0].T.astype(BF16)]

    (dproj, dout, p_in, p_out, p_a, p_h, d_gpost, d_lbl, d_gain, d_rel_bias, d_sinks, loss_cols) = _local_step(
        nseq, S, x2, tgt2, proj, h, rel_bias, attn_sinks, lb_logits, hgrn_norm, norm_post, shards)

    partials = [p.reshape(4, 2, p.shape[0] // N_DEV, p.shape[1]) for p in (p_in, p_out, p_a, p_h)]
    from_sib = _sibling_exchange(partials)
    chip_sums = [_chip_sum(p, f, br, "chip_sum_" + nm) for p, f, br, nm in zip(
        partials, from_sib, (192, 128, 128, 128), ("dw_in", "dw_out", "dw_branch_attn", "dw_branch_hgrn"))]
    grad_x2, d_gpre, r_in, r_out, r_a, r_h = _dh_prenorm_bwd(dproj, wt_in, x2, dout, norm_pre, chip_sums)
    g_in = _sum_slots(r_in, 192, "sum_dw_in").T
    g_out = _sum_slots(r_out, 128, "sum_dw_out")
    g_a = _sum_slots(r_a, 128, "sum_dw_branch_attn").T
    g_h = _sum_slots(r_h, 128, "sum_dw_branch_hgrn").T

    loss_part = 0.5 / D_MODEL * jnp.sum(loss_cols)
    packed = _pack_small(d_gpre, d_gpost, d_lbl, d_gain, d_rel_bias, d_sinks, extra=loss_part)
    total = _all_reduce_small(packed)
    loss = total[3, 1024 + 512 + 16]
    sm_w = _pack_small(norm_pre, norm_post, lb_logits, hgrn_norm, rel_bias, attn_sinks)
    sm_m = _pack_small(m_norm_pre, m_norm_post, m_lb_logits, m_hgrn_norm, m_rel_bias, m_attn_sinks)
    sm_v = _pack_small(v_norm_pre, v_norm_post, v_lb_logits, v_hgrn_norm, v_rel_bias, v_attn_sinks)
    sm_d, sm_nm, sm_nv = _adamw(sm_w, total, sm_m, sm_v, 8, "adamw_small")

    d_in, nm_in, nv_in = _adamw(w_in[0], g_in, m_w_in[0], v_w_in[0], 256, "adamw_w_in")
    d_out, nm_out, nv_out = _adamw(w_out[0], g_out, m_w_out[0], v_w_out[0], 128, "adamw_w_out")
    d_a, nm_a, nv_a = _adamw(w_branch_attn[0], g_a, m_w_branch_attn[0], v_w_branch_attn[0], 256, "adamw_w_branch_attn")
    d_h, nm_h, nv_h = _adamw(w_branch_hgrn[0], g_h, m_w_branch_hgrn[0], v_w_branch_hgrn[0], 256, "adamw_w_branch_hgrn")

    def group(small, big_in, big_a, big_h, big_out):
        npre, rb, sk, lbl, hn, npost = _unpack_small(small)
        return (npre, big_in[None], rb, sk, lbl, hn, big_a[None], big_h[None], big_out[None], npost)

    return (loss, grad_x2.reshape(nseq, S, D_MODEL),
            *group(total, g_in, g_a, g_h, g_out),
            *group(sm_d, d_in, d_a, d_h, d_out),
            *group(sm_nm, nm_in, nm_a, nm_h, nm_out),
            *group(sm_nv, nv_in, nv_a, nv_h, nv_out))
```

```python
import functools
import math

import numpy as np
import jax
import jax.numpy as jnp
from jax import lax
from jax.experimental import pallas as pl
from jax.experimental.pallas import tpu as pltpu

F32 = jnp.float32
BF16 = jnp.bfloat16

D_MODEL = 2048
ATTN_HEADS = 16
ATTN_HEAD_DIM = 64
ATTN_BLOCK = 128
HGRN_HEADS = 8
HGRN_DIM = 128
HGRN_CHUNK = 64
HGRN_SUB = 16
HGRN_PAR = 4
REL_BUCKETS = 32
REL_MAX_DIST = 128
NORM_EPS = 1e-6
C_AQ, C_AK, C_AV, C_AG = 0, 1024, 1280, 1536
C_HQ, C_HF, C_HI, C_HG = 2560, 3584, 4608, 5632
C_GA, C_GH = 6656, 8704
IN_WIDTH = 10752
N_DEV = 8
assert all(c0 % (HGRN_PAR * HGRN_DIM) == 0 for c0 in (C_HQ, C_HF, C_HI, C_HG))

ADAM_LR = 0.001
ADAM_B1 = 0.9
ADAM_B2 = 0.999
ADAM_EPS = 1e-08
ADAM_WD = 0.01
ADAM_STEP = 10

VMEM_LIMIT_V7X = 56 * 1024 * 1024
NEG_BIG = -1e30

NT = (((1,), (1,)), ((), ()))
TN = (((0,), (0,)), ((), ()))
NN = (((1,), (0,)), ((), ()))


def _dot(a, b, dims=NN):
    return lax.dot_general(a, b, dims, preferred_element_type=F32)


def _params(sem=None):
    return pltpu.CompilerParams(dimension_semantics=sem, vmem_limit_bytes=VMEM_LIMIT_V7X)


def _sigmoid(x):
    return 1.0 / (1.0 + jnp.exp(-x))


def _t5_bucket_table():
    qi = np.arange(ATTN_BLOCK)[:, None]
    si = np.arange(2 * ATTN_BLOCK)[None, :]
    dist = qi + ATTN_BLOCK - si
    max_exact = REL_BUCKETS // 2
    d = np.maximum(dist, 0)
    df = np.maximum(d, 1).astype(np.float32)
    large = max_exact + (np.log(df / np.float32(max_exact)).astype(np.float32)
                         / np.float32(math.log(REL_MAX_DIST / max_exact))
                         * np.float32(REL_BUCKETS - max_exact)).astype(np.int32)
    large = np.minimum(large, REL_BUCKETS - 1)
    return np.where(d < max_exact, d, large).astype(np.int32)


def _place():
    return lax.axis_index("x"), lax.axis_index("y"), lax.axis_index("c")


class _GatherOps:
    def __init__(self, ins, outs, send_sems, recv_sems, local_sems):
        self.ins, self.outs = ins, outs
        self.send_sems, self.recv_sems, self.local_sems = send_sems, recv_sems, local_sems
        x, y, c = _place()
        self.c = c
        self.me, self.sibling = (x, y, c), (x, y, 1 - c)
        self.chips = [(1 - x, y), (x, 1 - y), (1 - x, 1 - y)]

    def _rows(self, a, dev):
        r = self.ins[a].shape[0]
        return self.outs[a].at[pl.ds((4 * dev[0] + 2 * dev[1] + dev[2]) * r, r), :]

    def _copy(self, a, k, block, to, src=None):
        return pltpu.make_async_remote_copy(
            src_ref=self._rows(a, block) if src is None else src, dst_ref=self._rows(a, block),
            send_sem=self.send_sems.at[a, k], recv_sem=self.recv_sems.at[a, k],
            device_id=to, device_id_type=pl.DeviceIdType.MESH)

    def local(self, a):
        return pltpu.make_async_copy(self.ins[a], self._rows(a, self.me), self.local_sems.at[a])

    def to_sibling(self, a):
        return self._copy(a, 0, self.me, self.sibling, src=self.ins[a])

    def to_chip(self, a, q):
        return self._copy(a, 1 + q, self.me, (*self.chips[q], self.c), src=self.ins[a])

    def forward(self, a, q):
        return self._copy(a, 4 + q, (*self.chips[q], self.c), self.sibling)

    def from_sibling(self, a):
        return self._copy(a, 0, self.sibling, self.me)

    def from_chip(self, a, q):
        return self._copy(a, 1 + q, (*self.chips[q], self.c), self.me)

    def forwarded(self, a, q):
        return self._copy(a, 4 + q, (*self.chips[q], 1 - self.c), self.me)

    def sends(self, a):
        return [self.to_sibling(a)] + [self.to_chip(a, q) for q in range(3)] + [self.forward(a, q) for q in range(3)]


_GATHER_SEMS = 7
LOCAL_DMA_THREAD = 1

INPROJ_TILE = 896


def _prenorm(x2, gpre):
    T = x2.shape[0]
    tm = min(512, T)

    def body(x_ref, g_ref, h_ref):
        x = x_ref[...]
        r = lax.rsqrt(jnp.mean(x * x, axis=-1, keepdims=True) + NORM_EPS)
        h_ref[...] = (x * r * g_ref[...]).astype(BF16)

    return pl.pallas_call(
        body, name="prenorm",
        grid=(T // tm,),
        in_specs=[pl.BlockSpec((tm, D_MODEL), lambda i: (i, 0)), pl.BlockSpec((1, D_MODEL), lambda i: (0, 0))],
        out_specs=pl.BlockSpec((tm, D_MODEL), lambda i: (i, 0)),
        out_shape=jax.ShapeDtypeStruct((T, D_MODEL), BF16),
        compiler_params=_params(("arbitrary",)),
    )(x2, gpre)


def _gather_inproj(h, wt_shard):
    T = h.shape[0]
    tm = min(1024, T)
    nm = T // tm
    tn = INPROJ_TILE
    ntile = IN_WIDTH // tn
    nstep = ntile * nm

    def body(h_hbm, w_in, proj_hbm, w_out, hbuf, wtile, obuf, send_sems, recv_sems, local_sems, h_sem, w_sems, o_sems):
        j, i = pl.program_id(0), pl.program_id(1)
        step = j * nm + i
        slot = step % 2
        ops = _GatherOps([w_in], [w_out], send_sems, recv_sems, local_sems)
        x, y, _ = _place()

        def tile_of(jj):
            k = jj // 3
            return 3 * ((2 * x + y) ^ (((k & 1) << 1) | (k >> 1))) + jj % 3

        tile = tile_of(j)

        def h_load():
            return pltpu.make_async_copy(h_hbm, hbuf, h_sem)

        def store(s, rows, cols):
            return pltpu.make_async_copy(obuf.at[s], proj_hbm.at[rows, cols], o_sems.at[s])

        def window(ii, t):
            return pl.ds(pl.multiple_of(ii * tm, tm), tm), pl.ds(pl.multiple_of(t * tn, tn), tn)

        @pl.when(step == 0)
        def _():
            h_load().start()
            ops.local(0).start()
            ops.to_sibling(0).start()
            ops.to_chip(0, 0).start()
            ops.to_chip(0, 1).start()
            h_load().wait()

        for kk in range(4):
            @pl.when((j == 3 * kk) & (i == 0))
            def _(kk=kk):
                if kk == 0:
                    ops.local(0).wait()
                    ops.from_sibling(0).wait_recv()
                else:
                    q = kk - 1
                    ops.from_chip(0, q).wait_recv()
                    ops.forward(0, q).start()
                    if q == 0:
                        ops.to_chip(0, 2).start()
                    ops.forwarded(0, q).wait_recv()

        wslot = j % 2

        def fetch(jj, sw):
            rows = pl.ds(pl.multiple_of(tile_of(jj) * tn, tn), tn)
            return pltpu.make_async_copy(w_out.at[rows, :], wtile.at[sw], w_sems.at[sw])

        @pl.when((i == 0) & (j % 3 == 0))
        def _():
            fetch(j, wslot).start(LOCAL_DMA_THREAD)

        @pl.when(i == 0)
        def _():
            fetch(j, wslot).wait()

        @pl.when((i == 0) & (j % 3 != 2))
        def _():
            fetch(j + 1, 1 - wslot).start(LOCAL_DMA_THREAD)

        @pl.when(step >= 2)
        def _():
            store(slot, *window(0, 0)).wait()

        hv = hbuf[pl.ds(pl.multiple_of(i * tm, tm), tm), :]
        obuf[slot] = _dot(hv, wtile[wslot], NT).astype(BF16)
        store(slot, *window(i, tile)).start(LOCAL_DMA_THREAD)

        @pl.when(step == nstep - 1)
        def _():
            for s in range(min(2, nstep)):
                store(s, *window(0, 0)).wait()
            for cp in ops.sends(0):
                cp.wait_send()

    anyspec = lambda: pl.BlockSpec(memory_space=pl.ANY)
    return pl.pallas_call(
        body, name="gather_inproj",
        grid=(ntile, nm),
        in_specs=[anyspec(), anyspec()],
        out_specs=[anyspec(), anyspec()],
        out_shape=[jax.ShapeDtypeStruct((T, IN_WIDTH), BF16),
                   jax.ShapeDtypeStruct((N_DEV * wt_shard.shape[0], D_MODEL), BF16)],
        scratch_shapes=[pltpu.VMEM((T, D_MODEL), BF16), pltpu.VMEM((2, tn, D_MODEL), BF16),
                        pltpu.VMEM((2, tm, tn), BF16),
                        pltpu.SemaphoreType.DMA((1, _GATHER_SEMS)), pltpu.SemaphoreType.DMA((1, _GATHER_SEMS)),
                        pltpu.SemaphoreType.DMA((1,)), pltpu.SemaphoreType.DMA, pltpu.SemaphoreType.DMA((2,)),
                        pltpu.SemaphoreType.DMA((2,))],
        compiler_params=_params(("arbitrary", "arbitrary")),
    )(h, wt_shard)


def _bias_table(rel_bias, bucket):
    def body(rb_ref, bk_ref, out_ref):
        h = pl.program_id(0)
        bk = bk_ref[...]
        acc = jnp.zeros(bk.shape, F32)
        for b in range(REL_BUCKETS):
            acc = jnp.where(bk == b, rb_ref[b, h], acc)
        out_ref[...] = acc

    return pl.pallas_call(
        body, name="bias_table",
        grid=(ATTN_HEADS,),
        in_specs=[pl.BlockSpec(memory_space=pltpu.SMEM),
                  pl.BlockSpec((ATTN_BLOCK, 2 * ATTN_BLOCK), lambda h: (0, 0))],
        out_specs=pl.BlockSpec((None, ATTN_BLOCK, 2 * ATTN_BLOCK), lambda h: (h, 0, 0)),
        out_shape=jax.ShapeDtypeStruct((ATTN_HEADS, ATTN_BLOCK, 2 * ATTN_BLOCK), F32),
        compiler_params=_params(("arbitrary",)),
    )(rel_bias, bucket)


def _bias_table_bwd(dbias, bucket):
    def body(db_ref, bk_ref, out_ref):
        h = pl.program_id(0)
        bk = bk_ref[...]
        db = db_ref[...]
        for b in range(REL_BUCKETS):
            out_ref[b, h] = jnp.sum(jnp.where(bk == b, db, 0.0))

    return pl.pallas_call(
        body, name="bias_table_bwd",
        grid=(ATTN_HEADS,),
        in_specs=[pl.BlockSpec((None, ATTN_BLOCK, 2 * ATTN_BLOCK), lambda h: (h, 0, 0)),
                  pl.BlockSpec((ATTN_BLOCK, 2 * ATTN_BLOCK), lambda h: (0, 0))],
        out_specs=pl.BlockSpec(memory_space=pltpu.SMEM),
        out_shape=jax.ShapeDtypeStruct((REL_BUCKETS, ATTN_HEADS), F32),
        compiler_params=_params(("arbitrary",)),
    )(dbias, bucket)


def _attn_common(qkvg, kv_prev, blk):
    lane = lax.broadcasted_iota(jnp.int32, (1, 128), 1)
    half = (lane < ATTN_HEAD_DIM, lane >= ATTN_HEAD_DIM)
    kv_cur = qkvg[:, C_AK:C_AG]
    win = jnp.concatenate([kv_prev, kv_cur], axis=0)
    k_slab, v_slab = [], []
    for r in range(2):
        ks = win[:, 128 * r:128 * r + 128]
        vs = win[:, 256 + 128 * r:256 + 128 * r + 128]
        k_slab.append((ks, pltpu.roll(ks, ATTN_HEAD_DIM, 1)))
        v_slab.append((vs, pltpu.roll(vs, ATTN_HEAD_DIM, 1)))
    qi = lax.broadcasted_iota(jnp.int32, (ATTN_BLOCK, 2 * ATTN_BLOCK), 0)
    si = lax.broadcasted_iota(jnp.int32, (ATTN_BLOCK, 2 * ATTN_BLOCK), 1)
    valid = (si > qi) & (si <= qi + ATTN_BLOCK) & ((si >= ATTN_BLOCK) | (blk > 0))
    return half, k_slab, v_slab, valid


def _attn_head(h, qkvg, half, k_slab, v_slab, valid, bias_ref, sinks_ref):
    p, a = h // 2, h % 2
    j = h // 4
    r, aj = j // 2, j % 2
    swapped = 0 if aj == a else 1
    qm = jnp.where(half[a], qkvg[:, 128 * p:128 * p + 128], 0.0).astype(BF16)
    kk = k_slab[r][swapped]
    vv = v_slab[r][swapped]
    s = _dot(qm, kk.astype(BF16), NT) * (ATTN_HEAD_DIM ** -0.5) + bias_ref[h]
    s = jnp.where(valid, s, NEG_BIG)
    sink = sinks_ref[0, h]
    m = jnp.maximum(jnp.max(s, axis=-1, keepdims=True), sink)
    e = jnp.exp(s - m)
    es = jnp.exp(sink - m)
    inv = 1.0 / (jnp.sum(e, axis=-1, keepdims=True) + es)
    pn = e * inv
    vx = jnp.where(half[a], vv, 0.0).astype(BF16)
    o_h = _dot(pn.astype(BF16), vx)
    return dict(p=p, a=a, r=r, swapped=swapped, qm=qm, kk=kk, vv=vv, pn=pn, psink=es * inv, o_h=o_h)


def _attn_specs(nb):
    row = lambda b, i: b * nb + i
    return [
        pl.BlockSpec((ATTN_BLOCK, C_HQ), lambda b, i: (row(b, i), 0)),
        pl.BlockSpec((ATTN_BLOCK, 512), lambda b, i: (row(b, jnp.maximum(i - 1, 0)), 2)),
        pl.BlockSpec((ATTN_HEADS, ATTN_BLOCK, 2 * ATTN_BLOCK), lambda b, i: (0, 0, 0)),
        pl.BlockSpec(memory_space=pltpu.SMEM),
    ]


def _attn_fwd(proj, bias, sinks, nseq, nb):
    T = proj.shape[0]

    def body(qkvg_ref, kvp_ref, bias_ref, sinks_ref, ya_ref):
        qkvg = qkvg_ref[...].astype(F32)
        half, k_slab, v_slab, valid = _attn_common(qkvg, kvp_ref[...].astype(F32), pl.program_id(1))
        slabs = []
        for p in range(ATTN_HEADS // 2):
            o = None
            for h in (2 * p, 2 * p + 1):
                hd = _attn_head(h, qkvg, half, k_slab, v_slab, valid, bias_ref, sinks_ref)
                o = hd["o_h"] if o is None else o + hd["o_h"]
            slabs.append(o)
        o_all = jnp.concatenate(slabs, axis=1)
        g = qkvg[:, C_AG:C_HQ]
        ya_ref[...] = (o_all * (g * _sigmoid(g))).astype(BF16)

    return pl.pallas_call(
        body, name="attn_fwd",
        grid=(nseq, nb),
        in_specs=_attn_specs(nb),
        out_specs=pl.BlockSpec((ATTN_BLOCK, 1024), lambda b, i: (b * nb + i, 0)),
        out_shape=jax.ShapeDtypeStruct((T, 1024), BF16),
        compiler_params=_params(("arbitrary", "arbitrary")),
    )(proj, proj, bias, sinks)


def _attn_bwd(proj, bias, sinks, d_ya, nseq, nb):
    T = proj.shape[0]
    S = nb * ATTN_BLOCK
    scale = ATTN_HEAD_DIM ** -0.5

    def body(qkvg_ref, kvp_ref, bias_ref, sinks_ref, dya_ref, dq_ref, dkv_ref, dg_ref, dbias_ref, dsinks_ref):
        b, i = pl.program_id(0), pl.program_id(1)
        first = (b == 0) & (i == 0)

        @pl.when(first)
        def _():
            dbias_ref[...] = jnp.zeros(dbias_ref.shape, F32)
            for h in range(ATTN_HEADS):
                dsinks_ref[0, h] = 0.0

        qkvg = qkvg_ref[...].astype(F32)
        half, k_slab, v_slab, valid = _attn_common(qkvg, kvp_ref[...].astype(F32), i)
        g = qkvg[:, C_AG:C_HQ]
        sg = _sigmoid(g)
        silu_g = g * sg
        dya = dya_ref[...].astype(F32)
        do_all = dya * silu_g
        dq_slabs, o_slabs = [], []
        dk_acc = [[None, None], [None, None]]
        dv_acc = [[None, None], [None, None]]

        def add(acc, r, sw, val):
            acc[r][sw] = val if acc[r][sw] is None else acc[r][sw] + val

        for p in range(ATTN_HEADS // 2):
            dq_p, o_p = None, None
            do_slab = do_all[:, 128 * p:128 * p + 128]
            for h in (2 * p, 2 * p + 1):
                hd = _attn_head(h, qkvg, half, k_slab, v_slab, valid, bias_ref, sinks_ref)
                a, r, sw, pn = hd["a"], hd["r"], hd["swapped"], hd["pn"]
                dom = jnp.where(half[a], do_slab, 0.0)
                domb = dom.astype(BF16)
                dp = _dot(domb, hd["vv"].astype(BF16), NT)
                delta = jnp.sum(dom * hd["o_h"], axis=-1, keepdims=True)
                ds = pn * (dp - delta)
                dsinks_ref[0, h] += -jnp.sum(hd["psink"] * delta)
                dbias_ref[h] += ds
                dsb = ds.astype(BF16)
                kxm = jnp.where(half[a], hd["kk"], 0.0).astype(BF16)
                dq_h = _dot(dsb, kxm) * scale
                dq_p = dq_h if dq_p is None else dq_p + dq_h
                o_p = hd["o_h"] if o_p is None else o_p + hd["o_h"]
                add(dk_acc, r, sw, _dot(dsb, hd["qm"], TN) * scale)
                add(dv_acc, r, sw, _dot(pn.astype(BF16), domb, TN))
            dq_slabs.append(dq_p)
            o_slabs.append(o_p)

        dq_ref[...] = jnp.concatenate(dq_slabs, axis=1).astype(BF16)
        o_all = jnp.concatenate(o_slabs, axis=1)
        dg_ref[...] = (dya * o_all * (sg * (1.0 + g * (1.0 - sg)))).astype(BF16)

        pieces = []
        for acc in (dk_acc, dv_acc):
            for r in range(2):
                pieces.append(acc[r][0] + pltpu.roll(acc[r][1], ATTN_HEAD_DIM, 1))
        dkv = jnp.concatenate(pieces, axis=1)
        cur = pl.multiple_of(i * ATTN_BLOCK, ATTN_BLOCK)
        dkv_ref[pl.ds(cur, ATTN_BLOCK), :] = dkv[ATTN_BLOCK:].astype(BF16)

        @pl.when(i > 0)
        def _():
            prev = pl.multiple_of((i - 1) * ATTN_BLOCK, ATTN_BLOCK)
            old = dkv_ref[pl.ds(prev, ATTN_BLOCK), :].astype(F32)
            dkv_ref[pl.ds(prev, ATTN_BLOCK), :] = (old + dkv[:ATTN_BLOCK]).astype(BF16)

    row_spec = lambda w: pl.BlockSpec((ATTN_BLOCK, w), lambda b, i: (b * nb + i, 0))
    return pl.pallas_call(
        body, name="attn_bwd",
        grid=(nseq, nb),
        in_specs=_attn_specs(nb) + [row_spec(1024)],
        out_specs=[row_spec(1024),
                   pl.BlockSpec((S, 512), lambda b, i: (b, 0)),
                   row_spec(1024),
                   pl.BlockSpec((ATTN_HEADS, ATTN_BLOCK, 2 * ATTN_BLOCK), lambda b, i: (0, 0, 0)),
                   pl.BlockSpec(memory_space=pltpu.SMEM)],
        out_shape=[jax.ShapeDtypeStruct((T, 1024), BF16),
                   jax.ShapeDtypeStruct((T, 512), BF16),
                   jax.ShapeDtypeStruct((T, 1024), BF16),
                   jax.ShapeDtypeStruct((ATTN_HEADS, ATTN_BLOCK, 2 * ATTN_BLOCK), F32),
                   jax.ShapeDtypeStruct((1, ATTN_HEADS), F32)],
        compiler_params=_params(("arbitrary", "arbitrary")),
    )(proj, proj, bias, sinks, d_ya)


def _split3(x):
    hi = x.astype(BF16)
    r1 = x - hi.astype(F32)
    mid = r1.astype(BF16)
    lo = (r1 - mid.astype(F32)).astype(BF16)
    return jnp.concatenate([hi, mid, lo], axis=1)


def _tri_sum(tri, x):
    y = _dot(tri, _split3(x))
    return y[:, :128] + y[:, 128:256] + y[:, 256:]


def _hgrn_chunk(hq, hf, hi, lb):
    C = HGRN_CHUNK
    t = lax.broadcasted_iota(jnp.int32, (C, C), 0)
    s = lax.broadcasted_iota(jnp.int32, (C, C), 1)
    causal = s <= t
    sf = _sigmoid(hf)
    f = lb + (1.0 - lb) * sf
    G = _tri_sum(causal.astype(BF16), jnp.log(f))
    sq = _sigmoid(hq)
    qs = hq * sq
    k = 1.0 - f
    rowblk = lax.broadcasted_iota(jnp.int32, (C, 1), 0) // HGRN_SUB
    qt, kt, eq, ek = [], [], [], []
    for i in range(C // HGRN_SUB):
        lo = HGRN_SUB * i
        ref = G[lo + HGRN_SUB // 2:lo + HGRN_SUB // 2 + 1, :]
        eq_i = jnp.exp(G[lo:lo + HGRN_SUB] - ref)
        ek_i = jnp.exp(jnp.where(rowblk <= i, ref - G, 0.0))
        eq.append(eq_i)
        ek.append(ek_i)
        qt.append((qs[lo:lo + HGRN_SUB] * eq_i).astype(BF16))
        kt.append((k * ek_i).astype(BF16))
    A = jnp.concatenate([_dot(qt[i], kt[i], NT) for i in range(C // HGRN_SUB)], axis=0)
    A = jnp.where(causal, A, 0.0)
    glast = G[C - 1:C, :]
    eG = jnp.exp(G)
    edec = jnp.exp(glast - G)
    return dict(causal=causal, sf=sf, f=f, G=G, sq=sq, qs=qs, k=k, qt=qt, kt=kt, eq=eq, ek=ek, A=A,
                glast=glast, eG=eG, edec=edec, qhat=qs * eG, kdec=k * edec, v=hi)


def _hgrn_specs(nseq, ng, rows, reverse):
    W = HGRN_PAR
    gi = (lambda g: ng - 1 - g) if reverse else (lambda g: g)
    col = lambda c0: pl.BlockSpec((rows, W * HGRN_DIM), lambda h, b, g: (b * ng + gi(g), c0 // (W * HGRN_DIM) + h))
    return gi, [col(C_HQ), col(C_HF), col(C_HI), col(C_HG),
                pl.BlockSpec((2, W * HGRN_DIM), lambda h, b, g: (0, h)),
                pl.BlockSpec((W, 1, HGRN_DIM), lambda h, b, g: (h, 0, 0))]


def _hgrn_fwd(proj, lb_logits, gain3, nseq, S, shards):
    T = proj.shape[0]
    W = HGRN_PAR
    nc = S // HGRN_CHUNK
    cg = min(8, nc)
    ng = nc // cg
    rows = cg * HGRN_CHUNK
    n = len(shards)
    nh = HGRN_HEADS // W

    def body(hq_ref, hf_ref, hi_ref, hg_ref, lbl_ref, gain_ref, *refs):
        ins = refs[:n]
        o_ref, yh_ref, st_ref = refs[n:n + 3]
        outs = refs[n + 3:2 * n + 3]
        state, send_sems, recv_sems, local_sems = refs[2 * n + 3:]
        step = (pl.program_id(0), pl.program_id(1), pl.program_id(2))
        ops = _GatherOps(ins, outs, send_sems, recv_sems, local_sems)

        @pl.when((step[0] == 0) & (step[1] == 0) & (step[2] == 0))
        def _():
            for a in range(n):
                ops.local(a).start()
                ops.to_sibling(a).start()
                for q in range(3):
                    ops.to_chip(a, q).start()

        @pl.when(pl.program_id(2) == 0)
        def _():
            state[...] = jnp.zeros(state.shape, F32)

        lb_all = _sigmoid(lbl_ref[0:1, :] - lbl_ref[1:2, :])

        def chunk(c, carry):
            rs = pl.ds(pl.multiple_of(c * HGRN_CHUNK, HGRN_CHUNK), HGRN_CHUNK)
            for w in range(W):
                ls = slice(HGRN_DIM * w, HGRN_DIM * (w + 1))
                ch = _hgrn_chunk(hq_ref[rs, ls].astype(F32), hf_ref[rs, ls].astype(F32),
                                 hi_ref[rs, ls].astype(F32), lb_all[:, ls])
                st = state[w]
                st_ref[w, c] = st
                vb = ch["v"].astype(BF16)
                o = _dot(ch["qhat"].astype(BF16), st.astype(BF16), NT) + _dot(ch["A"].astype(BF16), vb)
                state[w] = st * jnp.exp(ch["glast"]) + _dot(vb, ch["kdec"].astype(BF16), TN)
                o_ref[rs, ls] = o
                r = lax.rsqrt(jnp.mean(o * o, axis=-1, keepdims=True) + NORM_EPS)
                hg = hg_ref[rs, ls].astype(F32)
                yh_ref[rs, ls] = (o * r * gain_ref[w] * (hg * _sigmoid(hg))).astype(BF16)
            return carry

        lax.fori_loop(0, cg, chunk, 0)

        @pl.when((step[0] == nh - 1) & (step[1] == nseq - 1) & (step[2] == ng - 1))
        def _():
            for a in range(n):
                ops.local(a).wait()
                ops.from_sibling(a).wait_recv()
                for q in range(3):
                    ops.from_chip(a, q).wait_recv()
                    ops.forward(a, q).start()
            for a in range(n):
                for q in range(3):
                    ops.forwarded(a, q).wait_recv()
                for cp in ops.sends(a):
                    cp.wait_send()

    _, in_specs = _hgrn_specs(nseq, ng, rows, False)
    out_row = lambda: pl.BlockSpec((rows, W * HGRN_DIM), lambda h, b, g: (b * ng + g, h))
    anyspec = lambda: pl.BlockSpec(memory_space=pl.ANY)
    return pl.pallas_call(
        body, name="hgrn_fwd",
        grid=(nh, nseq, ng),
        in_specs=in_specs + [anyspec() for _ in shards],
        out_specs=[out_row(), out_row(),
                   pl.BlockSpec((None, W, cg, HGRN_DIM, HGRN_DIM), lambda h, b, g: (b, h, g, 0, 0))]
                  + [anyspec() for _ in shards],
        out_shape=[jax.ShapeDtypeStruct((T, 1024), F32),
                   jax.ShapeDtypeStruct((T, 1024), BF16),
                   jax.ShapeDtypeStruct((nseq, HGRN_HEADS, nc, HGRN_DIM, HGRN_DIM), F32)]
                  + [jax.ShapeDtypeStruct((N_DEV * s.shape[0], s.shape[1]), s.dtype) for s in shards],
        scratch_shapes=[pltpu.VMEM((W, HGRN_DIM, HGRN_DIM), F32),
                        pltpu.SemaphoreType.DMA((n, _GATHER_SEMS)), pltpu.SemaphoreType.DMA((n, _GATHER_SEMS)),
                        pltpu.SemaphoreType.DMA((n,))],
        compiler_params=_params(("arbitrary", "arbitrary", "arbitrary")),
    )(proj, proj, proj, proj, lb_logits, gain3, *shards)


def _hgrn_bwd(proj, lb_logits, gain3, o, d_yh, states, nseq, S):
    T = proj.shape[0]
    W = HGRN_PAR
    nc = S // HGRN_CHUNK
    cg = min(8, nc)
    ng = nc // cg
    rows = cg * HGRN_CHUNK
    C = HGRN_CHUNK
    nsub = C // HGRN_SUB

    def body(hq_ref, hf_ref, hi_ref, hg_ref, lbl_ref, gain_ref, o_ref, dyh_ref, st_ref,
             dhq_ref, dhf_ref, dhi_ref, dhg_ref, dgain_ref, dlbl_ref, dstate, dlb_acc):
        b, g = pl.program_id(1), pl.program_id(2)

        @pl.when(g == 0)
        def _():
            dstate[...] = jnp.zeros(dstate.shape, F32)

        @pl.when((b == 0) & (g == 0))
        def _():
            dgain_ref[...] = jnp.zeros(dgain_ref.shape, F32)
            dlb_acc[...] = jnp.zeros(dlb_acc.shape, F32)

        lb_all = _sigmoid(lbl_ref[0:1, :] - lbl_ref[1:2, :])

        def head_chunk(w, c):
            rs = pl.ds(pl.multiple_of(c * C, C), C)
            ls = slice(HGRN_DIM * w, HGRN_DIM * (w + 1))
            lb = lb_all[:, ls]
            gain = gain_ref[w]
            hq, hg = hq_ref[rs, ls].astype(F32), hg_ref[rs, ls].astype(F32)
            ch = _hgrn_chunk(hq, hf_ref[rs, ls].astype(F32), hi_ref[rs, ls].astype(F32), lb)
            ov = o_ref[rs, ls]
            dyh = dyh_ref[rs, ls].astype(F32)
            r = lax.rsqrt(jnp.mean(ov * ov, axis=-1, keepdims=True) + NORM_EPS)
            on = ov * r
            sg = _sigmoid(hg)
            doh = dyh * (hg * sg)
            dhg_ref[rs, ls] = (dyh * on * gain * (sg * (1.0 + hg * (1.0 - sg)))).astype(BF16)
            dgain_ref[w] += jnp.sum(doh * on, axis=0, keepdims=True)
            don = doh * gain
            do = r * (don - on * jnp.mean(don * on, axis=-1, keepdims=True))
            dob = do.astype(BF16)
            st = st_ref[w, c]
            dst = dstate[w]
            stb, dstb = st.astype(BF16), dst.astype(BF16)
            vb = ch["v"].astype(BF16)
            qhatb = ch["qhat"].astype(BF16)
            eglast = jnp.exp(ch["glast"])
            dqhat = _dot(dob, stb)
            dkdec = _dot(vb, dstb)
            dv = _dot(ch["kdec"].astype(BF16), dstb, NT)
            deg = jnp.sum(dst * st, axis=0, keepdims=True)
            dstate[w] = dst * eglast + _dot(dob, qhatb, TN)
            dA = jnp.where(ch["causal"], _dot(dob, vb, NT), 0.0)
            dv = dv + _dot(ch["A"].astype(BF16), dob, TN)
            dAb = dA.astype(BF16)
            dqs_parts, dgq_parts = [], []
            dk_intra, dgk = None, None
            for i in range(nsub):
                dA_i = dAb[HGRN_SUB * i:HGRN_SUB * (i + 1)]
                dqt = _dot(dA_i, ch["kt"][i])
                dkt = _dot(dA_i, ch["qt"][i], TN)
                dqs_parts.append(dqt * ch["eq"][i])
                dgq_parts.append(dqt * ch["qt"][i].astype(F32))
                dk_i = dkt * ch["ek"][i]
                dgk_i = dkt * ch["kt"][i].astype(F32)
                dk_intra = dk_i if dk_intra is None else dk_intra + dk_i
                dgk = dgk_i if dgk is None else dgk + dgk_i
            dqs_inter = dqhat * ch["eG"]
            dk_state = dkdec * ch["edec"]
            dqs = jnp.concatenate(dqs_parts, axis=0) + dqs_inter
            dk = dk_intra + dk_state
            dG = jnp.concatenate(dgq_parts, axis=0) - dgk + ch["qs"] * dqs_inter - ch["k"] * dk_state
            last_row = lax.broadcasted_iota(jnp.int32, (C, 1), 0) == C - 1
            tail = jnp.sum(dkdec * ch["kdec"], axis=0, keepdims=True) + deg * eglast
            dG = dG + jnp.where(last_row, tail, 0.0)
            anti = (lax.broadcasted_iota(jnp.int32, (C, C), 1)
                    >= lax.broadcasted_iota(jnp.int32, (C, C), 0)).astype(BF16)
            dlf = _tri_sum(anti, dG)
            df = dlf / ch["f"] - dk
            sf = ch["sf"]
            dhf_ref[rs, ls] = (df * (1.0 - lb) * sf * (1.0 - sf)).astype(BF16)
            dlb_acc[:, ls] += jnp.sum(df * (1.0 - sf), axis=0, keepdims=True)
            sq = ch["sq"]
            dhq_ref[rs, ls] = (dqs * (sq * (1.0 + hq * (1.0 - sq)))).astype(BF16)
            dhi_ref[rs, ls] = dv.astype(BF16)

        def chunk(cc, carry):
            for w in range(W):
                head_chunk(w, cg - 1 - cc)
            return carry

        lax.fori_loop(0, cg, chunk, 0)

        @pl.when((b == nseq - 1) & (g == ng - 1))
        def _():
            dl0 = dlb_acc[...] * lb_all * (1.0 - lb_all)
            dlbl_ref[0:1, :] = dl0
            dlbl_ref[1:2, :] = -dl0

    gi, in_specs = _hgrn_specs(nseq, ng, rows, True)
    row = lambda: pl.BlockSpec((rows, W * HGRN_DIM), lambda h, b, g: (b * ng + gi(g), h))
    return pl.pallas_call(
        body, name="hgrn_bwd",
        grid=(HGRN_HEADS // W, nseq, ng),
        in_specs=in_specs + [row(), row(),
                             pl.BlockSpec((None, W, cg, HGRN_DIM, HGRN_DIM), lambda h, b, g: (b, h, gi(g), 0, 0))],
        out_specs=[row(), row(), row(), row(),
                   pl.BlockSpec((W, 1, HGRN_DIM), lambda h, b, g: (h, 0, 0)),
                   pl.BlockSpec((2, W * HGRN_DIM), lambda h, b, g: (0, h))],
        out_shape=[jax.ShapeDtypeStruct((T, 1024), BF16)] * 4
                  + [jax.ShapeDtypeStruct((HGRN_HEADS, 1, HGRN_DIM), F32),
                     jax.ShapeDtypeStruct((2, HGRN_HEADS * HGRN_DIM), F32)],
        scratch_shapes=[pltpu.VMEM((W, HGRN_DIM, HGRN_DIM), F32), pltpu.VMEM((1, W * HGRN_DIM), F32)],
        compiler_params=_params(("arbitrary", "arbitrary", "arbitrary")),
    )(proj, proj, proj, proj, lb_logits, gain3, o, d_yh, states)


def _mid(proj, ya, yh, x2, tgt2, gpost, wa_t, wh_t, wout):
    T = proj.shape[0]
    tm = min(128, T)
    nt = T // tm

    def body(*refs):
        ga_refs, gh_refs = refs[0:4], refs[4:8]
        (ya_ref, yh_ref, x_ref, t_ref, gpost_ref, wa_hbm, wh_hbm, wo_hbm,
         merged_ref, dy_ref, dua_ref, duh_ref, dga_ref, dgh_ref, dya_ref, dyh_ref, dout_ref,
         loss_ref, dgpost_ref, wa, wh, wo) = refs[8:]
        i = pl.program_id(0)

        @pl.when(i == 0)
        def _():
            pltpu.sync_copy(wa_hbm, wa)
            pltpu.sync_copy(wh_hbm, wh)
            pltpu.sync_copy(wo_hbm, wo)
            loss_ref[...] = jnp.zeros(loss_ref.shape, F32)
            dgpost_ref[...] = jnp.zeros(dgpost_ref.shape, F32)

        ga = jnp.concatenate([r[...] for r in ga_refs], axis=1).astype(F32)
        gh = jnp.concatenate([r[...] for r in gh_refs], axis=1).astype(F32)
        sa, sh = _sigmoid(ga), _sigmoid(gh)
        ua = _dot(ya_ref[...], wa[...], NT)
        uh = _dot(yh_ref[...], wh[...], NT)
        merged = (sa * ua + sh * uh).astype(BF16)
        merged_ref[...] = merged
        y = _dot(merged, wo[...])
        r2 = lax.rsqrt(jnp.mean(y * y, axis=-1, keepdims=True) + NORM_EPS)
        yn = y * r2
        gpost = gpost_ref[...]
        err = x_ref[...] + yn * gpost - t_ref[...]
        loss_ref[...] += jnp.sum(err * err, axis=0, keepdims=True)
        dout = err * (1.0 / D_MODEL)
        dout_ref[...] = dout
        dgpost_ref[...] += jnp.sum(dout * yn, axis=0, keepdims=True)
        dyn = dout * gpost
        dy = (r2 * (dyn - yn * jnp.mean(dyn * yn, axis=-1, keepdims=True))).astype(BF16)
        dy_ref[...] = dy
        dm = _dot(dy, wo[...], NT)
        dua = (dm * sa).astype(BF16)
        duh = (dm * sh).astype(BF16)
        dua_ref[...] = dua
        duh_ref[...] = duh
        dga_ref[...] = (dm * ua * (sa * (1.0 - sa))).astype(BF16)
        dgh_ref[...] = (dm * uh * (sh * (1.0 - sh))).astype(BF16)
        dya_ref[...] = _dot(dua, wa[...]).astype(BF16)
        dyh_ref[...] = _dot(duh, wh[...]).astype(BF16)

    gate_spec = lambda c0, q: pl.BlockSpec((tm, 512), lambda i: (i, c0 // 512 + q))
    rowb = lambda w: pl.BlockSpec((tm, w), lambda i: (i, 0))
    vec = lambda: pl.BlockSpec((1, D_MODEL), lambda i: (0, 0))
    anyspec = lambda: pl.BlockSpec(memory_space=pl.ANY)
    out_shapes = ([jax.ShapeDtypeStruct((T, D_MODEL), BF16)] * 6
                  + [jax.ShapeDtypeStruct((T, 1024), BF16)] * 2
                  + [jax.ShapeDtypeStruct((T, D_MODEL), F32),
                     jax.ShapeDtypeStruct((1, D_MODEL), F32), jax.ShapeDtypeStruct((1, D_MODEL), F32)])
    return pl.pallas_call(
        body, name="mid",
        grid=(nt,),
        in_specs=[gate_spec(C_GA, q) for q in range(4)] + [gate_spec(C_GH, q) for q in range(4)]
                 + [rowb(1024), rowb(1024), rowb(D_MODEL), rowb(D_MODEL), vec(), anyspec(), anyspec(), anyspec()],
        out_specs=[rowb(D_MODEL)] * 6 + [rowb(1024)] * 2 + [rowb(D_MODEL), vec(), vec()],
        out_shape=out_shapes,
        scratch_shapes=[pltpu.VMEM((D_MODEL, 1024), BF16), pltpu.VMEM((D_MODEL, 1024), BF16),
                        pltpu.VMEM((D_MODEL, D_MODEL), BF16)],
        compiler_params=_params(("arbitrary",)),
    )(*([proj] * 8), ya, yh, x2, tgt2, gpost, wa_t, wh_t, wout)


def _tn_matmul(L, R, bm, bn, name):
    T, M = L.shape
    N = R.shape[1]

    def body(l_ref, r_ref, out_ref):
        out_ref[...] = _dot(l_ref[...], r_ref[...], TN).astype(BF16)

    return pl.pallas_call(
        body, name=name,
        grid=(N // bn, M // bm),
        in_specs=[pl.BlockSpec((T, bm), lambda j, i: (0, i)),
                  pl.BlockSpec((T, bn), lambda j, i: (0, j))],
        out_specs=pl.BlockSpec((bm, bn), lambda j, i: (i, j)),
        out_shape=jax.ShapeDtypeStruct((M, N), BF16),
        compiler_params=_params(("arbitrary", "arbitrary")),
    )(L, R)


def _dh_prenorm_bwd(dproj, wt_in, x2, dout, gpre, chip_sums):
    T = x2.shape[0]
    tm = min(1024, T)
    ne = 2
    te = tm // ne
    tk = 768
    nk = IN_WIDTH // tk
    nt = T // tm
    n = len(chip_sums)

    def body(dp_ref, w_ref, x_ref, dout_ref, g_ref, *refs):
        ins = refs[:n]
        gx_ref, dg_ref = refs[n], refs[n + 1]
        outs = refs[n + 2:2 * n + 2]
        acc, send_sems, recv_sems, local_sems = refs[2 * n + 2:]
        i, k = pl.program_id(0), pl.program_id(1)
        x, y, c = _place()
        my_chip = 2 * x + y

        def peer(q):
            return (x ^ (q >> 1), y ^ (q & 1))

        def send(a, q):
            px, py = peer(q)
            return pltpu.make_async_remote_copy(
                src_ref=ins[a].at[2 * px + py], dst_ref=outs[a].at[my_chip],
                send_sem=send_sems.at[a, q - 1], recv_sem=recv_sems.at[a, q - 1],
                device_id=(px, py, c), device_id_type=pl.DeviceIdType.MESH)

        def arrival(a, q):
            px, py = peer(q)
            return pltpu.make_async_remote_copy(
                src_ref=ins[a].at[my_chip], dst_ref=outs[a].at[2 * px + py],
                send_sem=send_sems.at[a, q - 1], recv_sem=recv_sems.at[a, q - 1],
                device_id=(px, py, c), device_id_type=pl.DeviceIdType.MESH)

        def local(a):
            return pltpu.make_async_copy(ins[a].at[my_chip], outs[a].at[my_chip], local_sems.at[a])

        @pl.when((i == 0) & (k == 0))
        def _():
            dg_ref[...] = jnp.zeros(dg_ref.shape, F32)
            for a in range(n):
                local(a).start()
                for q in range(1, 4):
                    send(a, q).start()

        @pl.when((i == nt - 1) & (k == nk + ne - 1))
        def _():
            for a in range(n):
                for q in range(1, 4):
                    arrival(a, q).wait_recv()
            for a in range(n):
                for q in range(1, 4):
                    send(a, q).wait_send()
                local(a).wait()

        @pl.when(k == 0)
        def _():
            acc[...] = _dot(dp_ref[...], w_ref[...])

        @pl.when((k > 0) & (k < nk))
        def _():
            acc[...] += _dot(dp_ref[...], w_ref[...])

        @pl.when(k >= nk)
        def _():
            dh = acc[pl.ds(pl.multiple_of((k - nk) * te, te), te), :]
            x = x_ref[...]
            r = lax.rsqrt(jnp.mean(x * x, axis=-1, keepdims=True) + NORM_EPS)
            xn = x * r
            dg_ref[...] += jnp.sum(dh * xn, axis=0, keepdims=True)
            dxn = dh * g_ref[...]
            gx_ref[...] = dout_ref[...] + r * (dxn - xn * jnp.mean(dxn * xn, axis=-1, keepdims=True))

    rowb = lambda: pl.BlockSpec((te, D_MODEL), lambda i, k: (ne * i + jnp.clip(k - nk, 0, ne - 1), 0))
    vec = lambda: pl.BlockSpec((1, D_MODEL), lambda i, k: (0, 0))
    anyspec = lambda: pl.BlockSpec(memory_space=pl.ANY)
    return pl.pallas_call(
        body, name="dh_prenorm_bwd",
        grid=(nt, nk + ne),
        in_specs=[pl.BlockSpec((tm, tk), lambda i, k: (i, jnp.minimum(k, nk - 1))),
                  pl.BlockSpec((tk, D_MODEL), lambda i, k: (jnp.minimum(k, nk - 1), 0)),
                  rowb(), rowb(), vec()] + [anyspec() for _ in chip_sums],
        out_specs=[rowb(), vec()] + [anyspec() for _ in chip_sums],
        out_shape=[jax.ShapeDtypeStruct((T, D_MODEL), F32), jax.ShapeDtypeStruct((1, D_MODEL), F32)]
                  + [jax.ShapeDtypeStruct(s.shape, s.dtype) for s in chip_sums],
        scratch_shapes=[pltpu.VMEM((tm, D_MODEL), F32),
                        pltpu.SemaphoreType.DMA((n, 3)), pltpu.SemaphoreType.DMA((n, 3)),
                        pltpu.SemaphoreType.DMA((n,))],
        compiler_params=_params(("arbitrary", "arbitrary")),
    )(dproj, wt_in, x2, dout, gpre, *chip_sums)


def _sum_slots(recv, br, name):
    nslot, R, C = recv.shape

    def body(r_ref, out_ref):
        acc = r_ref[0].astype(F32)
        for s in range(1, nslot):
            acc = acc + r_ref[s].astype(F32)
        out_ref[...] = acc

    return pl.pallas_call(
        body, name=name,
        grid=(R // br,),
        in_specs=[pl.BlockSpec((nslot, br, C), lambda i: (0, i, 0))],
        out_specs=pl.BlockSpec((br, C), lambda i: (i, 0)),
        out_shape=jax.ShapeDtypeStruct((R, C), F32),
        compiler_params=_params(("arbitrary",)),
    )(recv)


def _adamw_math(w, g, m, v):
    m = ADAM_B1 * m + (1.0 - ADAM_B1) * g
    v = ADAM_B2 * v + (1.0 - ADAM_B2) * (g * g)
    m_hat = m / (1.0 - ADAM_B1 ** ADAM_STEP)
    v_hat = v / (1.0 - ADAM_B2 ** ADAM_STEP)
    delta = -ADAM_LR * (m_hat / (jnp.sqrt(v_hat) + ADAM_EPS) + ADAM_WD * w)
    return delta, m, v


def _adamw(w, g, m, v, br, name):
    R, C = w.shape

    def body(w_ref, g_ref, m_ref, v_ref, d_ref, nm_ref, nv_ref):
        d_ref[...], nm_ref[...], nv_ref[...] = _adamw_math(w_ref[...], g_ref[...], m_ref[...], v_ref[...])

    spec = lambda: pl.BlockSpec((br, C), lambda i: (i, 0))
    return pl.pallas_call(
        body, name=name,
        grid=(R // br,),
        in_specs=[spec(), spec(), spec(), spec()],
        out_specs=[spec(), spec(), spec()],
        out_shape=[jax.ShapeDtypeStruct((R, C), F32)] * 3,
        compiler_params=_params(("arbitrary",)),
    )(w, g, m, v)


def _sibling_exchange(partials):
    n = len(partials)

    def body(*refs):
        ins, outs = refs[:n], refs[n:2 * n]
        send_sems, recv_sems = refs[2 * n:]
        x, y, c = _place()

        def copy(a, p):
            return pltpu.make_async_remote_copy(
                src_ref=ins[a].at[p, 1 - c], dst_ref=outs[a].at[p],
                send_sem=send_sems.at[a, p], recv_sem=recv_sems.at[a, p],
                device_id=(x, y, 1 - c), device_id_type=pl.DeviceIdType.MESH)

        copies = [copy(a, p) for p in range(4) for a in range(n)]
        for cp in copies:
            cp.start()
        for cp in copies:
            cp.wait()

    anyspec = lambda: pl.BlockSpec(memory_space=pl.ANY)
    return pl.pallas_call(
        body, name="sibling_exchange",
        in_specs=[anyspec() for _ in partials],
        out_specs=[anyspec() for _ in partials],
        out_shape=[jax.ShapeDtypeStruct((4,) + p.shape[2:], p.dtype) for p in partials],
        scratch_shapes=[pltpu.SemaphoreType.DMA((n, 4)), pltpu.SemaphoreType.DMA((n, 4))],
    )(*partials)


def _chip_sum(partial, from_sibling, br, name):
    _, _, R, C = partial.shape
    cls = lax.axis_index("c").astype(jnp.int32).reshape(1)

    def body(c_ref, mine_ref, sib_ref, out_ref):
        out_ref[...] = (mine_ref[...].astype(F32) + sib_ref[...].astype(F32)).astype(BF16)

    grid_spec = pltpu.PrefetchScalarGridSpec(
        num_scalar_prefetch=1,
        grid=(4, R // br),
        in_specs=[pl.BlockSpec((None, None, br, C), lambda p, i, c: (p, c[0], i, 0)),
                  pl.BlockSpec((None, br, C), lambda p, i, c: (p, i, 0))],
        out_specs=pl.BlockSpec((None, br, C), lambda p, i, c: (p, i, 0)),
    )
    return pl.pallas_call(
        body, name=name, grid_spec=grid_spec,
        out_shape=jax.ShapeDtypeStruct((4, R, C), BF16),
        compiler_params=_params(("arbitrary", "arbitrary")),
    )(cls, partial, from_sibling)


def _all_reduce_small(packed):
    shape = packed.shape

    def body(in_ref, out_ref, slots, send_sems, recv_sems):
        x, y, c = _place()
        my_slot = 4 * x + 2 * y + c

        def peer(k):
            return (x ^ ((k >> 2) & 1), y ^ ((k >> 1) & 1), c ^ (k & 1))

        def copy(k):
            p = peer(k)
            return pltpu.make_async_remote_copy(
                src_ref=in_ref, dst_ref=slots.at[my_slot],
                send_sem=send_sems.at[k - 1], recv_sem=recv_sems.at[k - 1],
                device_id=p, device_id_type=pl.DeviceIdType.MESH)

        def arrival(k):
            p = peer(k)
            return pltpu.make_async_remote_copy(
                src_ref=in_ref, dst_ref=slots.at[4 * p[0] + 2 * p[1] + p[2]],
                send_sem=send_sems.at[k - 1], recv_sem=recv_sems.at[k - 1],
                device_id=p, device_id_type=pl.DeviceIdType.MESH)

        sends = [copy(k) for k in range(1, N_DEV)]
        for cp in sends:
            cp.start()
        slots[my_slot] = in_ref[...]
        for k in range(1, N_DEV):
            arrival(k).wait_recv()
        for cp in sends:
            cp.wait_send()
        acc = slots[0]
        for s in range(1, N_DEV):
            acc = acc + slots[s]
        out_ref[...] = acc

    return pl.pallas_call(
        body, name="all_reduce_small",
        in_specs=[pl.BlockSpec(memory_space=pltpu.VMEM)],
        out_specs=pl.BlockSpec(memory_space=pltpu.VMEM),
        out_shape=jax.ShapeDtypeStruct(shape, F32),
        scratch_shapes=[pltpu.VMEM((N_DEV,) + shape, F32),
                        pltpu.SemaphoreType.DMA((7,)), pltpu.SemaphoreType.DMA((7,))],
    )(packed)


def _pack_small(norm_pre, norm_post, lb_logits, hgrn_norm, rel_bias, sinks, extra=None):
    tail = [hgrn_norm.reshape(1, 1024), rel_bias.reshape(1, 512), sinks.reshape(1, 16)]
    used = 1024 + 512 + 16
    if extra is not None:
        tail.append(extra.reshape(1, 1))
        used += 1
    tail.append(jnp.zeros((1, D_MODEL - used), F32))
    rows = [norm_pre.reshape(1, D_MODEL), norm_post.reshape(1, D_MODEL), lb_logits.reshape(1, D_MODEL),
            jnp.concatenate(tail, axis=1), jnp.zeros((4, D_MODEL), F32)]
    return jnp.concatenate(rows, axis=0)


def _unpack_small(p):
    return (p[0:1], p[3, 1024:1536].reshape(REL_BUCKETS, ATTN_HEADS), p[3:4, 1536:1552],
            p[2].reshape(2, 1024), p[3, 0:1024].reshape(1, HGRN_HEADS, HGRN_DIM), p[1:2])


def _local_step(nseq, S, x2, tgt2, proj, h, rel_bias, attn_sinks, lb_logits, hgrn_norm, norm_post, shards):
    nb = S // ATTN_BLOCK
    bucket = jnp.asarray(_t5_bucket_table())
    gain3 = hgrn_norm.reshape(HGRN_HEADS, 1, HGRN_DIM)

    bias = _bias_table(rel_bias, bucket)
    ya = _attn_fwd(proj, bias, attn_sinks, nseq, nb)
    o, yh, states, wout, wa_t, wh_t = _hgrn_fwd(proj, lb_logits, gain3, nseq, S, shards)
    (merged, dy, dua, duh, dga, dgh, dya, dyh, dout, loss_cols, d_gpost) = _mid(
        proj, ya, yh, x2, tgt2, norm_post, wa_t, wh_t, wout)

    dq, dkv, dg, dbias, d_sinks = _attn_bwd(proj, bias, attn_sinks, dya, nseq, nb)
    d_rel_bias = _bias_table_bwd(dbias, bucket)
    dhq, dhf, dhi, dhg, d_gain, d_lbl = _hgrn_bwd(proj, lb_logits, gain3, o, dyh, states, nseq, S)
    dproj = jnp.concatenate([dq, dkv, dg, dhq, dhf, dhi, dhg, dga, dgh], axis=1)

    p_in = _tn_matmul(dproj, h, 768, 1024, "dw_in")
    p_out = _tn_matmul(merged, dy, 256, 1024, "dw_out")
    p_a = _tn_matmul(dua, ya, 256, 1024, "dw_branch_attn")
    p_h = _tn_matmul(duh, yh, 256, 1024, "dw_branch_hgrn")
    return dproj, dout, p_in, p_out, p_a, p_h, d_gpost, d_lbl, d_gain, d_rel_bias, d_sinks, loss_cols


def kernel(x, norm_pre, w_in, rel_bias, attn_sinks, lb_logits, hgrn_norm, w_branch_attn, w_branch_hgrn, w_out, norm_post, loss_target, m_norm_pre, m_w_in, m_rel_bias, m_attn_sinks, m_lb_logits, m_hgrn_norm, m_w_branch_attn, m_w_branch_hgrn, m_w_out, m_norm_post, v_norm_pre, v_w_in, v_rel_bias, v_attn_sinks, v_lb_logits, v_hgrn_norm, v_w_branch_attn, v_w_branch_hgrn, v_w_out, v_norm_post):
    nseq, S, _ = x.shape
    T = nseq * S
    x2 = x.reshape(T, D_MODEL)
    tgt2 = loss_target.reshape(T, D_MODEL)

    h = _prenorm(x2, norm_pre)
    proj, wt_in = _gather_inproj(h, w_in[0].T.astype(BF16))
    shards = [w_out[0].astype(BF16), w_branch_attn[0].T.astype(BF16), w_branch_hgrn[0].T.astype(BF16)]

    (dproj, dout, p_in, p_out, p_a, p_h, d_gpost, d_lbl, d_gain, d_rel_bias, d_sinks, loss_cols) = _local_step(
        nseq, S, x2, tgt2, proj, h, rel_bias, attn_sinks, lb_logits, hgrn_norm, norm_post, shards)

    partials = [p.reshape(4, 2, p.shape[0] // N_DEV, p.shape[1]) for p in (p_in, p_out, p_a, p_h)]
    from_sib = _sibling_exchange(partials)
    chip_sums = [_chip_sum(p, f, br, "chip_sum_" + nm) for p, f, br, nm in zip(
        partials, from_sib, (192, 128, 128, 128), ("dw_in", "dw_out", "dw_branch_attn", "dw_branch_hgrn"))]
    grad_x2, d_gpre, r_in, r_out, r_a, r_h = _dh_prenorm_bwd(dproj, wt_in, x2, dout, norm_pre, chip_sums)
    g_in = _sum_slots(r_in, 192, "sum_dw_in").T
    g_out = _sum_slots(r_out, 128, "sum_dw_out")
    g_a = _sum_slots(r_a, 128, "sum_dw_branch_attn").T
    g_h = _sum_slots(r_h, 128, "sum_dw_branch_hgrn").T

    loss_part = 0.5 / D_MODEL * jnp.sum(loss_cols)
    packed = _pack_small(d_gpre, d_gpost, d_lbl, d_gain, d_rel_bias, d_sinks, extra=loss_part)
    total = _all_reduce_small(packed)
    loss = total[3, 1024 + 512 + 16]
    sm_w = _pack_small(norm_pre, norm_post, lb_logits, hgrn_norm, rel_bias, attn_sinks)
    sm_m = _pack_small(m_norm_pre, m_norm_post, m_lb_logits, m_hgrn_norm, m_rel_bias, m_attn_sinks)
    sm_v = _pack_small(v_norm_pre, v_norm_post, v_lb_logits, v_hgrn_norm, v_rel_bias, v_attn_sinks)
    sm_d, sm_nm, sm_nv = _adamw(sm_w, total, sm_m, sm_v, 8, "adamw_small")

    d_in, nm_in, nv_in = _adamw(w_in[0], g_in, m_w_in[0], v_w_in[0], 256, "adamw_w_in")
    d_out, nm_out, nv_out = _adamw(w_out[0], g_out, m_w_out[0], v_w_out[0], 128, "adamw_w_out")
    d_a, nm_a, nv_a = _adamw(w_branch_attn[0], g_a, m_w_branch_attn[0], v_w_branch_attn[0], 256, "adamw_w_branch_attn")
    d_h, nm_h, nv_h = _adamw(w_branch_hgrn[0], g_h, m_w_branch_hgrn[0], v_w_branch_hgrn[0], 256, "adamw_w_branch_hgrn")

    def group(small, big_in, big_a, big_h, big_out):
        npre, rb, sk, lbl, hn, npost = _unpack_small(small)
        return (npre, big_in[None], rb, sk, lbl, hn, big_a[None], big_h[None], big_out[None], npost)

    return (loss, grad_x2.reshape(nseq, S, D_MODEL),
            *group(total, g_in, g_a, g_h, g_out),
            *group(sm_d, d_in, d_a, d_h, d_out),
            *group(sm_nm, nm_in, nm_a, nm_h, nm_out),
            *group(sm_nv, nv_in, nv_a, nv_h, nv_out))
```

```python
import functools
import math

import numpy as np
import jax
import jax.numpy as jnp
from jax import lax
from jax.experimental import pallas as pl
from jax.experimental.pallas import tpu as pltpu

F32 = jnp.float32
BF16 = jnp.bfloat16

D_MODEL = 2048
ATTN_HEADS = 16
ATTN_HEAD_DIM = 64
ATTN_GROUP = 4
ATTN_BLOCK = 128
HGRN_HEADS = 8
HGRN_DIM = 128
HGRN_CHUNK = 64
HGRN_SUB = 16
HGRN_PAR = 4
REL_BUCKETS = 32
REL_MAX_DIST = 128
NORM_EPS = 1e-6
C_AQ, C_AK, C_AV, C_AG = 0, 1024, 1280, 1536
C_HQ, C_HF, C_HI, C_HG = 2560, 3584, 4608, 5632
C_GA, C_GH = 6656, 8704
IN_WIDTH = 10752
N_DEV = 8
assert all(c0 % (HGRN_PAR * HGRN_DIM) == 0 for c0 in (C_HQ, C_HF, C_HI, C_HG))

ADAM_LR = 0.001
ADAM_B1 = 0.9
ADAM_B2 = 0.999
ADAM_EPS = 1e-08
ADAM_WD = 0.01
ADAM_STEP = 10

VMEM_LIMIT_V7X = 56 * 1024 * 1024
NEG_BIG = -1e30

NT = (((1,), (1,)), ((), ()))
TN = (((0,), (0,)), ((), ()))
NN = (((1,), (0,)), ((), ()))


def _dot(a, b, dims=NN):
    return lax.dot_general(a, b, dims, preferred_element_type=F32)


def _params(sem=None):
    return pltpu.CompilerParams(dimension_semantics=sem, vmem_limit_bytes=VMEM_LIMIT_V7X)


def _sigmoid(x):
    return 1.0 / (1.0 + jnp.exp(-x))


def _t5_bucket_table():
    qi = np.arange(ATTN_BLOCK)[:, None]
    si = np.arange(2 * ATTN_BLOCK)[None, :]
    dist = qi + ATTN_BLOCK - si
    max_exact = REL_BUCKETS // 2
    d = np.maximum(dist, 0)
    df = np.maximum(d, 1).astype(np.float32)
    large = max_exact + (np.log(df / np.float32(max_exact)).astype(np.float32)
                         / np.float32(math.log(REL_MAX_DIST / max_exact))
                         * np.float32(REL_BUCKETS - max_exact)).astype(np.int32)
    large = np.minimum(large, REL_BUCKETS - 1)
    return np.where(d < max_exact, d, large).astype(np.int32)


def _place():
    return lax.axis_index("x"), lax.axis_index("y"), lax.axis_index("c")


class _GatherOps:
    def __init__(self, ins, outs, send_sems, recv_sems, local_sems):
        self.ins, self.outs = ins, outs
        self.send_sems, self.recv_sems, self.local_sems = send_sems, recv_sems, local_sems
        x, y, c = _place()
        self.c = c
        self.me, self.sibling = (x, y, c), (x, y, 1 - c)
        self.chips = [(1 - x, y), (x, 1 - y), (1 - x, 1 - y)]

    def _rows(self, a, dev):
        r = self.ins[a].shape[0]
        return self.outs[a].at[pl.ds((4 * dev[0] + 2 * dev[1] + dev[2]) * r, r), :]

    def _copy(self, a, k, block, to, src=None):
        return pltpu.make_async_remote_copy(
            src_ref=self._rows(a, block) if src is None else src, dst_ref=self._rows(a, block),
            send_sem=self.send_sems.at[a, k], recv_sem=self.recv_sems.at[a, k],
            device_id=to, device_id_type=pl.DeviceIdType.MESH)

    def local(self, a):
        return pltpu.make_async_copy(self.ins[a], self._rows(a, self.me), self.local_sems.at[a])

    def to_sibling(self, a):
        return self._copy(a, 0, self.me, self.sibling, src=self.ins[a])

    def to_chip(self, a, q):
        return self._copy(a, 1 + q, self.me, (*self.chips[q], self.c), src=self.ins[a])

    def forward(self, a, q):
        return self._copy(a, 4 + q, (*self.chips[q], self.c), self.sibling)

    def from_sibling(self, a):
        return self._copy(a, 0, self.sibling, self.me)

    def from_chip(self, a, q):
        return self._copy(a, 1 + q, (*self.chips[q], self.c), self.me)

    def forwarded(self, a, q):
        return self._copy(a, 4 + q, (*self.chips[q], 1 - self.c), self.me)

    def sends(self, a):
        return [self.to_sibling(a)] + [self.to_chip(a, q) for q in range(3)] + [self.forward(a, q) for q in range(3)]


_GATHER_SEMS = 7
LOCAL_DMA_THREAD = 1

INPROJ_TILE = 896


def _prenorm(x2, gpre):
    T = x2.shape[0]
    tm = min(512, T)

    def body(x_ref, g_ref, h_ref):
        x = x_ref[...]
        r = lax.rsqrt(jnp.mean(x * x, axis=-1, keepdims=True) + NORM_EPS)
        h_ref[...] = (x * r * g_ref[...]).astype(BF16)

    return pl.pallas_call(
        body, name="prenorm",
        grid=(T // tm,),
        in_specs=[pl.BlockSpec((tm, D_MODEL), lambda i: (i, 0)), pl.BlockSpec((1, D_MODEL), lambda i: (0, 0))],
        out_specs=pl.BlockSpec((tm, D_MODEL), lambda i: (i, 0)),
        out_shape=jax.ShapeDtypeStruct((T, D_MODEL), BF16),
        compiler_params=_params(("arbitrary",)),
    )(x2, gpre)


def _gather_inproj(h, wt_shard):
    T = h.shape[0]
    tm = min(1024, T)
    nm = T // tm
    tn = INPROJ_TILE
    ntile = IN_WIDTH // tn
    nstep = ntile * nm

    def body(h_hbm, w_in, proj_hbm, w_out, hbuf, wtile, obuf, send_sems, recv_sems, local_sems, h_sem, w_sems, o_sems):
        j, i = pl.program_id(0), pl.program_id(1)
        step = j * nm + i
        slot = step % 2
        ops = _GatherOps([w_in], [w_out], send_sems, recv_sems, local_sems)
        x, y, _ = _place()

        def tile_of(jj):
            k = jj // 3
            return 3 * ((2 * x + y) ^ (((k & 1) << 1) | (k >> 1))) + jj % 3

        tile = tile_of(j)

        def h_load():
            return pltpu.make_async_copy(h_hbm, hbuf, h_sem)

        def store(s, rows, cols):
            return pltpu.make_async_copy(obuf.at[s], proj_hbm.at[rows, cols], o_sems.at[s])

        def window(ii, t):
            return pl.ds(pl.multiple_of(ii * tm, tm), tm), pl.ds(pl.multiple_of(t * tn, tn), tn)

        @pl.when(step == 0)
        def _():
            h_load().start()
            ops.local(0).start()
            ops.to_sibling(0).start()
            ops.to_chip(0, 0).start()
            ops.to_chip(0, 1).start()
            h_load().wait()

        for kk in range(4):
            @pl.when((j == 3 * kk) & (i == 0))
            def _(kk=kk):
                if kk == 0:
                    ops.local(0).wait()
                    ops.from_sibling(0).wait_recv()
                else:
                    q = kk - 1
                    ops.from_chip(0, q).wait_recv()
                    ops.forward(0, q).start()
                    if q == 0:
                        ops.to_chip(0, 2).start()
                    ops.forwarded(0, q).wait_recv()

        wslot = j % 2

        def fetch(jj, sw):
            rows = pl.ds(pl.multiple_of(tile_of(jj) * tn, tn), tn)
            return pltpu.make_async_copy(w_out.at[rows, :], wtile.at[sw], w_sems.at[sw])

        @pl.when((i == 0) & (j % 3 == 0))
        def _():
            fetch(j, wslot).start(LOCAL_DMA_THREAD)

        @pl.when(i == 0)
        def _():
            fetch(j, wslot).wait()

        @pl.when((i == 0) & (j % 3 != 2))
        def _():
            fetch(j + 1, 1 - wslot).start(LOCAL_DMA_THREAD)

        @pl.when(step >= 2)
        def _():
            store(slot, *window(0, 0)).wait()

        hv = hbuf[pl.ds(pl.multiple_of(i * tm, tm), tm), :]
        obuf[slot] = _dot(hv, wtile[wslot], NT).astype(BF16)
        store(slot, *window(i, tile)).start(LOCAL_DMA_THREAD)

        @pl.when(step == nstep - 1)
        def _():
            for s in range(min(2, nstep)):
                store(s, *window(0, 0)).wait()
            for cp in ops.sends(0):
                cp.wait_send()

    anyspec = lambda: pl.BlockSpec(memory_space=pl.ANY)
    return pl.pallas_call(
        body, name="gather_inproj",
        grid=(ntile, nm),
        in_specs=[anyspec(), anyspec()],
        out_specs=[anyspec(), anyspec()],
        out_shape=[jax.ShapeDtypeStruct((T, IN_WIDTH), BF16),
                   jax.ShapeDtypeStruct((N_DEV * wt_shard.shape[0], D_MODEL), BF16)],
        scratch_shapes=[pltpu.VMEM((T, D_MODEL), BF16), pltpu.VMEM((2, tn, D_MODEL), BF16),
                        pltpu.VMEM((2, tm, tn), BF16),
                        pltpu.SemaphoreType.DMA((1, _GATHER_SEMS)), pltpu.SemaphoreType.DMA((1, _GATHER_SEMS)),
                        pltpu.SemaphoreType.DMA((1,)), pltpu.SemaphoreType.DMA, pltpu.SemaphoreType.DMA((2,)),
                        pltpu.SemaphoreType.DMA((2,))],
        compiler_params=_params(("arbitrary", "arbitrary")),
    )(h, wt_shard)


def _bias_table(rel_bias, bucket):
    def body(rb_ref, bk_ref, out_ref):
        h = pl.program_id(0)
        bk = bk_ref[...]
        acc = jnp.zeros(bk.shape, F32)
        for b in range(REL_BUCKETS):
            acc = jnp.where(bk == b, rb_ref[b, h], acc)
        out_ref[...] = acc

    return pl.pallas_call(
        body, name="bias_table",
        grid=(ATTN_HEADS,),
        in_specs=[pl.BlockSpec(memory_space=pltpu.SMEM),
                  pl.BlockSpec((ATTN_BLOCK, 2 * ATTN_BLOCK), lambda h: (0, 0))],
        out_specs=pl.BlockSpec((None, ATTN_BLOCK, 2 * ATTN_BLOCK), lambda h: (h, 0, 0)),
        out_shape=jax.ShapeDtypeStruct((ATTN_HEADS, ATTN_BLOCK, 2 * ATTN_BLOCK), F32),
        compiler_params=_params(("arbitrary",)),
    )(rel_bias, bucket)


def _bias_table_bwd(dbias, bucket):
    def body(db_ref, bk_ref, out_ref):
        h = pl.program_id(0)
        bk = bk_ref[...]
        db = db_ref[...]
        for b in range(REL_BUCKETS):
            out_ref[b, h] = jnp.sum(jnp.where(bk == b, db, 0.0))

    return pl.pallas_call(
        body, name="bias_table_bwd",
        grid=(ATTN_HEADS,),
        in_specs=[pl.BlockSpec((None, ATTN_BLOCK, 2 * ATTN_BLOCK), lambda h: (h, 0, 0)),
                  pl.BlockSpec((ATTN_BLOCK, 2 * ATTN_BLOCK), lambda h: (0, 0))],
        out_specs=pl.BlockSpec(memory_space=pltpu.SMEM),
        out_shape=jax.ShapeDtypeStruct((REL_BUCKETS, ATTN_HEADS), F32),
        compiler_params=_params(("arbitrary",)),
    )(dbias, bucket)


def _attn_common(qkvg, kv_prev, blk):
    lane = lax.broadcasted_iota(jnp.int32, (1, 128), 1)
    half = (lane < ATTN_HEAD_DIM, lane >= ATTN_HEAD_DIM)
    kv_cur = qkvg[:, C_AK:C_AG]
    win = jnp.concatenate([kv_prev, kv_cur], axis=0)
    k_slab, v_slab = [], []
    for r in range(2):
        ks = win[:, 128 * r:128 * r + 128]
        vs = win[:, 256 + 128 * r:256 + 128 * r + 128]
        k_slab.append((ks, pltpu.roll(ks, ATTN_HEAD_DIM, 1)))
        v_slab.append((vs, pltpu.roll(vs, ATTN_HEAD_DIM, 1)))
    rows4 = ATTN_GROUP * ATTN_BLOCK
    qi = lax.broadcasted_iota(jnp.int32, (rows4, 2 * ATTN_BLOCK), 0) & (ATTN_BLOCK - 1)
    si = lax.broadcasted_iota(jnp.int32, (rows4, 2 * ATTN_BLOCK), 1)
    valid = (si > qi) & (si <= qi + ATTN_BLOCK) & ((si >= ATTN_BLOCK) | (blk > 0))
    return half, k_slab, v_slab, valid


def _stack_heads(half, slab0, slab1):
    return jnp.concatenate([jnp.where(half[0], slab0, 0.0), jnp.where(half[1], slab0, 0.0),
                            jnp.where(half[0], slab1, 0.0), jnp.where(half[1], slab1, 0.0)], axis=0)


def _unstack_heads(half, x4):
    B = ATTN_BLOCK
    return (jnp.where(half[0], x4[0:B], x4[B:2 * B]), jnp.where(half[0], x4[2 * B:3 * B], x4[3 * B:4 * B]))


def _attn_group(j, qkvg, half, k_slab, v_slab, valid, bias_ref, sinks_ref):
    r, aj = j // 2, j % 2
    pick = (lambda a, b: jnp.where(half[0], a, b)) if aj == 0 else (lambda a, b: jnp.where(half[0], b, a))
    kb = pick(*k_slab[r]).astype(BF16)
    vb = pick(*v_slab[r]).astype(BF16)
    q4 = _stack_heads(half, qkvg[:, 256 * j:256 * j + 128], qkvg[:, 256 * j + 128:256 * j + 256]).astype(BF16)
    bias4 = bias_ref[ATTN_GROUP * j:ATTN_GROUP * (j + 1)].reshape(valid.shape)
    s = _dot(q4, kb, NT) * (ATTN_HEAD_DIM ** -0.5) + bias4
    s = jnp.where(valid, s, NEG_BIG)
    rowblk = lax.broadcasted_iota(jnp.int32, (valid.shape[0], 1), 0) // ATTN_BLOCK
    sink = jnp.full((valid.shape[0], 1), sinks_ref[0, ATTN_GROUP * j], F32)
    for b in range(1, ATTN_GROUP):
        sink = jnp.where(rowblk == b, sinks_ref[0, ATTN_GROUP * j + b], sink)
    m = jnp.maximum(jnp.max(s, axis=-1, keepdims=True), sink)
    e = jnp.exp(s - m)
    es = jnp.exp(sink - m)
    inv = 1.0 / (jnp.sum(e, axis=-1, keepdims=True) + es)
    pn = e * inv
    o4 = _dot(pn.astype(BF16), vb)
    return dict(r=r, aj=aj, kb=kb, vb=vb, q4=q4, pn=pn, psink=es * inv, o4=o4)


def _attn_specs(nb):
    row = lambda b, i: b * nb + i
    return [
        pl.BlockSpec((ATTN_BLOCK, C_HQ), lambda b, i: (row(b, i), 0)),
        pl.BlockSpec((ATTN_BLOCK, 512), lambda b, i: (row(b, jnp.maximum(i - 1, 0)), 2)),
        pl.BlockSpec((ATTN_HEADS, ATTN_BLOCK, 2 * ATTN_BLOCK), lambda b, i: (0, 0, 0)),
        pl.BlockSpec(memory_space=pltpu.SMEM),
    ]


def _attn_fwd(proj, bias, sinks, nseq, nb):
    T = proj.shape[0]

    def body(qkvg_ref, kvp_ref, bias_ref, sinks_ref, ya_ref):
        qkvg = qkvg_ref[...].astype(F32)
        half, k_slab, v_slab, valid = _attn_common(qkvg, kvp_ref[...].astype(F32), pl.program_id(1))
        slabs = []
        for j in range(ATTN_HEADS // ATTN_GROUP):
            grp = _attn_group(j, qkvg, half, k_slab, v_slab, valid, bias_ref, sinks_ref)
            slabs += _unstack_heads(half, grp["o4"])
        o_all = jnp.concatenate(slabs, axis=1)
        g = qkvg[:, C_AG:C_HQ]
        ya_ref[...] = (o_all * (g * _sigmoid(g))).astype(BF16)

    return pl.pallas_call(
        body, name="attn_fwd",
        grid=(nseq, nb),
        in_specs=_attn_specs(nb),
        out_specs=pl.BlockSpec((ATTN_BLOCK, 1024), lambda b, i: (b * nb + i, 0)),
        out_shape=jax.ShapeDtypeStruct((T, 1024), BF16),
        compiler_params=_params(("arbitrary", "arbitrary")),
    )(proj, proj, bias, sinks)


def _attn_bwd(proj, bias, sinks, d_ya, nseq, nb):
    T = proj.shape[0]
    S = nb * ATTN_BLOCK
    scale = ATTN_HEAD_DIM ** -0.5

    def body(qkvg_ref, kvp_ref, bias_ref, sinks_ref, dya_ref, dq_ref, dkv_ref, dg_ref, dbias_ref, dsinks_ref):
        b, i = pl.program_id(0), pl.program_id(1)
        first = (b == 0) & (i == 0)

        @pl.when(first)
        def _():
            dbias_ref[...] = jnp.zeros(dbias_ref.shape, F32)
            for h in range(ATTN_HEADS):
                dsinks_ref[0, h] = 0.0

        qkvg = qkvg_ref[...].astype(F32)
        half, k_slab, v_slab, valid = _attn_common(qkvg, kvp_ref[...].astype(F32), i)
        g = qkvg[:, C_AG:C_HQ]
        sg = _sigmoid(g)
        silu_g = g * sg
        dya = dya_ref[...].astype(F32)
        do_all = dya * silu_g
        dq_slabs, o_slabs = [], []
        dk_slab, dv_slab = [None, None], [None, None]
        B = ATTN_BLOCK

        def fold(x, aj):
            return jnp.where(half[aj], x + pltpu.roll(x, ATTN_HEAD_DIM, 1), 0.0)

        for j in range(ATTN_HEADS // ATTN_GROUP):
            grp = _attn_group(j, qkvg, half, k_slab, v_slab, valid, bias_ref, sinks_ref)
            r, aj, pn = grp["r"], grp["aj"], grp["pn"]
            do4 = _stack_heads(half, do_all[:, 256 * j:256 * j + 128], do_all[:, 256 * j + 128:256 * j + 256])
            do4b = do4.astype(BF16)
            dp = _dot(do4b, grp["vb"], NT)
            delta = jnp.sum(do4 * grp["o4"], axis=-1, keepdims=True)
            ds = pn * (dp - delta)
            sink_term = grp["psink"] * delta
            for b4 in range(ATTN_GROUP):
                dsinks_ref[0, ATTN_GROUP * j + b4] += -jnp.sum(sink_term[B * b4:B * (b4 + 1)])
            dbias_ref[ATTN_GROUP * j:ATTN_GROUP * (j + 1)] += ds.reshape(ATTN_GROUP, B, 2 * B)
            dsb = ds.astype(BF16)
            dq_slabs += _unstack_heads(half, _dot(dsb, grp["kb"]) * scale)
            o_slabs += _unstack_heads(half, grp["o4"])
            dk_j = fold(_dot(dsb, grp["q4"], TN) * scale, aj)
            dv_j = fold(_dot(pn.astype(BF16), do4b, TN), aj)
            dk_slab[r] = dk_j if dk_slab[r] is None else dk_slab[r] + dk_j
            dv_slab[r] = dv_j if dv_slab[r] is None else dv_slab[r] + dv_j

        dq_ref[...] = jnp.concatenate(dq_slabs, axis=1).astype(BF16)
        o_all = jnp.concatenate(o_slabs, axis=1)
        dg_ref[...] = (dya * o_all * (sg * (1.0 + g * (1.0 - sg)))).astype(BF16)

        dkv = jnp.concatenate(dk_slab + dv_slab, axis=1)
        cur = pl.multiple_of(i * ATTN_BLOCK, ATTN_BLOCK)
        dkv_ref[pl.ds(cur, ATTN_BLOCK), :] = dkv[ATTN_BLOCK:].astype(BF16)

        @pl.when(i > 0)
        def _():
            prev = pl.multiple_of((i - 1) * ATTN_BLOCK, ATTN_BLOCK)
            old = dkv_ref[pl.ds(prev, ATTN_BLOCK), :].astype(F32)
            dkv_ref[pl.ds(prev, ATTN_BLOCK), :] = (old + dkv[:ATTN_BLOCK]).astype(BF16)

    row_spec = lambda w: pl.BlockSpec((ATTN_BLOCK, w), lambda b, i: (b * nb + i, 0))
    return pl.pallas_call(
        body, name="attn_bwd",
        grid=(nseq, nb),
        in_specs=_attn_specs(nb) + [row_spec(1024)],
        out_specs=[row_spec(1024),
                   pl.BlockSpec((S, 512), lambda b, i: (b, 0)),
                   row_spec(1024),
                   pl.BlockSpec((ATTN_HEADS, ATTN_BLOCK, 2 * ATTN_BLOCK), lambda b, i: (0, 0, 0)),
                   pl.BlockSpec(memory_space=pltpu.SMEM)],
        out_shape=[jax.ShapeDtypeStruct((T, 1024), BF16),
                   jax.ShapeDtypeStruct((T, 512), BF16),
                   jax.ShapeDtypeStruct((T, 1024), BF16),
                   jax.ShapeDtypeStruct((ATTN_HEADS, ATTN_BLOCK, 2 * ATTN_BLOCK), F32),
                   jax.ShapeDtypeStruct((1, ATTN_HEADS), F32)],
        compiler_params=_params(("arbitrary", "arbitrary")),
    )(proj, proj, bias, sinks, d_ya)


def _split3(x):
    hi = x.astype(BF16)
    r1 = x - hi.astype(F32)
    mid = r1.astype(BF16)
    lo = (r1 - mid.astype(F32)).astype(BF16)
    return jnp.concatenate([hi, mid, lo], axis=1)


def _tri_sum(tri, x):
    y = _dot(tri, _split3(x))
    return y[:, :128] + y[:, 128:256] + y[:, 256:]


def _hgrn_chunk(hq, hf, hi, lb):
    C = HGRN_CHUNK
    t = lax.broadcasted_iota(jnp.int32, (C, C), 0)
    s = lax.broadcasted_iota(jnp.int32, (C, C), 1)
    causal = s <= t
    sf = _sigmoid(hf)
    f = lb + (1.0 - lb) * sf
    G = _tri_sum(causal.astype(BF16), jnp.log(f))
    sq = _sigmoid(hq)
    qs = hq * sq
    k = 1.0 - f
    rowblk = lax.broadcasted_iota(jnp.int32, (C, 1), 0) // HGRN_SUB
    qt, kt, eq, ek = [], [], [], []
    for i in range(C // HGRN_SUB):
        lo = HGRN_SUB * i
        ref = G[lo + HGRN_SUB // 2:lo + HGRN_SUB // 2 + 1, :]
        eq_i = jnp.exp(G[lo:lo + HGRN_SUB] - ref)
        ek_i = jnp.exp(jnp.where(rowblk <= i, ref - G, 0.0))
        eq.append(eq_i)
        ek.append(ek_i)
        qt.append((qs[lo:lo + HGRN_SUB] * eq_i).astype(BF16))
        kt.append((k * ek_i).astype(BF16))
    A = jnp.concatenate([_dot(qt[i], kt[i], NT) for i in range(C // HGRN_SUB)], axis=0)
    A = jnp.where(causal, A, 0.0)
    glast = G[C - 1:C, :]
    eG = jnp.exp(G)
    edec = jnp.exp(glast - G)
    return dict(causal=causal, sf=sf, f=f, G=G, sq=sq, qs=qs, k=k, qt=qt, kt=kt, eq=eq, ek=ek, A=A,
                glast=glast, eG=eG, edec=edec, qhat=qs * eG, kdec=k * edec, v=hi)


def _hgrn_specs(nseq, ng, rows, reverse):
    W = HGRN_PAR
    gi = (lambda g: ng - 1 - g) if reverse else (lambda g: g)
    col = lambda c0: pl.BlockSpec((rows, W * HGRN_DIM), lambda h, b, g: (b * ng + gi(g), c0 // (W * HGRN_DIM) + h))
    return gi, [col(C_HQ), col(C_HF), col(C_HI), col(C_HG),
                pl.BlockSpec((2, W * HGRN_DIM), lambda h, b, g: (0, h)),
                pl.BlockSpec((W, 1, HGRN_DIM), lambda h, b, g: (h, 0, 0))]


def _hgrn_fwd(proj, lb_logits, gain3, nseq, S, shards):
    T = proj.shape[0]
    W = HGRN_PAR
    nc = S // HGRN_CHUNK
    cg = min(8, nc)
    ng = nc // cg
    rows = cg * HGRN_CHUNK
    n = len(shards)
    nh = HGRN_HEADS // W

    def body(hq_ref, hf_ref, hi_ref, hg_ref, lbl_ref, gain_ref, *refs):
        ins = refs[:n]
        o_ref, yh_ref, st_ref = refs[n:n + 3]
        outs = refs[n + 3:2 * n + 3]
        state, send_sems, recv_sems, local_sems = refs[2 * n + 3:]
        step = (pl.program_id(0), pl.program_id(1), pl.program_id(2))
        ops = _GatherOps(ins, outs, send_sems, recv_sems, local_sems)

        @pl.when((step[0] == 0) & (step[1] == 0) & (step[2] == 0))
        def _():
            for a in range(n):
                ops.local(a).start()
                ops.to_sibling(a).start()
                for q in range(3):
                    ops.to_chip(a, q).start()

        @pl.when(pl.program_id(2) == 0)
        def _():
            state[...] = jnp.zeros(state.shape, F32)

        lb_all = _sigmoid(lbl_ref[0:1, :] - lbl_ref[1:2, :])

        def chunk(c, carry):
            rs = pl.ds(pl.multiple_of(c * HGRN_CHUNK, HGRN_CHUNK), HGRN_CHUNK)
            for w in range(W):
                ls = slice(HGRN_DIM * w, HGRN_DIM * (w + 1))
                ch = _hgrn_chunk(hq_ref[rs, ls].astype(F32), hf_ref[rs, ls].astype(F32),
                                 hi_ref[rs, ls].astype(F32), lb_all[:, ls])
                st = state[w]
                st_ref[w, c] = st
                vb = ch["v"].astype(BF16)
                o = _dot(ch["qhat"].astype(BF16), st.astype(BF16), NT) + _dot(ch["A"].astype(BF16), vb)
                state[w] = st * jnp.exp(ch["glast"]) + _dot(vb, ch["kdec"].astype(BF16), TN)
                o_ref[rs, ls] = o
                r = lax.rsqrt(jnp.mean(o * o, axis=-1, keepdims=True) + NORM_EPS)
                hg = hg_ref[rs, ls].astype(F32)
                yh_ref[rs, ls] = (o * r * gain_ref[w] * (hg * _sigmoid(hg))).astype(BF16)
            return carry

        lax.fori_loop(0, cg, chunk, 0)

        @pl.when((step[0] == nh - 1) & (step[1] == nseq - 1) & (step[2] == ng - 1))
        def _():
            for a in range(n):
                ops.local(a).wait()
                ops.from_sibling(a).wait_recv()
                for q in range(3):
                    ops.from_chip(a, q).wait_recv()
                    ops.forward(a, q).start()
            for a in range(n):
                for q in range(3):
                    ops.forwarded(a, q).wait_recv()
                for cp in ops.sends(a):
                    cp.wait_send()

    _, in_specs = _hgrn_specs(nseq, ng, rows, False)
    out_row = lambda: pl.BlockSpec((rows, W * HGRN_DIM), lambda h, b, g: (b * ng + g, h))
    anyspec = lambda: pl.BlockSpec(memory_space=pl.ANY)
    return pl.pallas_call(
        body, name="hgrn_fwd",
        grid=(nh, nseq, ng),
        in_specs=in_specs + [anyspec() for _ in shards],
        out_specs=[out_row(), out_row(),
                   pl.BlockSpec((None, W, cg, HGRN_DIM, HGRN_DIM), lambda h, b, g: (b, h, g, 0, 0))]
                  + [anyspec() for _ in shards],
        out_shape=[jax.ShapeDtypeStruct((T, 1024), F32),
                   jax.ShapeDtypeStruct((T, 1024), BF16),
                   jax.ShapeDtypeStruct((nseq, HGRN_HEADS, nc, HGRN_DIM, HGRN_DIM), F32)]
                  + [jax.ShapeDtypeStruct((N_DEV * s.shape[0], s.shape[1]), s.dtype) for s in shards],
        scratch_shapes=[pltpu.VMEM((W, HGRN_DIM, HGRN_DIM), F32),
                        pltpu.SemaphoreType.DMA((n, _GATHER_SEMS)), pltpu.SemaphoreType.DMA((n, _GATHER_SEMS)),
                        pltpu.SemaphoreType.DMA((n,))],
        compiler_params=_params(("arbitrary", "arbitrary", "arbitrary")),
    )(proj, proj, proj, proj, lb_logits, gain3, *shards)


def _hgrn_bwd(proj, lb_logits, gain3, o, d_yh, states, nseq, S):
    T = proj.shape[0]
    W = HGRN_PAR
    nc = S // HGRN_CHUNK
    cg = min(8, nc)
    ng = nc // cg
    rows = cg * HGRN_CHUNK
    C = HGRN_CHUNK
    nsub = C // HGRN_SUB

    def body(hq_ref, hf_ref, hi_ref, hg_ref, lbl_ref, gain_ref, o_ref, dyh_ref, st_ref,
             dhq_ref, dhf_ref, dhi_ref, dhg_ref, dgain_ref, dlbl_ref, dstate, dlb_acc):
        b, g = pl.program_id(1), pl.program_id(2)

        @pl.when(g == 0)
        def _():
            dstate[...] = jnp.zeros(dstate.shape, F32)

        @pl.when((b == 0) & (g == 0))
        def _():
            dgain_ref[...] = jnp.zeros(dgain_ref.shape, F32)
            dlb_acc[...] = jnp.zeros(dlb_acc.shape, F32)

        lb_all = _sigmoid(lbl_ref[0:1, :] - lbl_ref[1:2, :])

        def head_chunk(w, c):
            rs = pl.ds(pl.multiple_of(c * C, C), C)
            ls = slice(HGRN_DIM * w, HGRN_DIM * (w + 1))
            lb = lb_all[:, ls]
            gain = gain_ref[w]
            hq, hg = hq_ref[rs, ls].astype(F32), hg_ref[rs, ls].astype(F32)
            ch = _hgrn_chunk(hq, hf_ref[rs, ls].astype(F32), hi_ref[rs, ls].astype(F32), lb)
            ov = o_ref[rs, ls]
            dyh = dyh_ref[rs, ls].astype(F32)
            r = lax.rsqrt(jnp.mean(ov * ov, axis=-1, keepdims=True) + NORM_EPS)
            on = ov * r
            sg = _sigmoid(hg)
            doh = dyh * (hg * sg)
            dhg_ref[rs, ls] = (dyh * on * gain * (sg * (1.0 + hg * (1.0 - sg)))).astype(BF16)
            dgain_ref[w] += jnp.sum(doh * on, axis=0, keepdims=True)
            don = doh * gain
            do = r * (don - on * jnp.mean(don * on, axis=-1, keepdims=True))
            dob = do.astype(BF16)
            st = st_ref[w, c]
            dst = dstate[w]
            stb, dstb = st.astype(BF16), dst.astype(BF16)
            vb = ch["v"].astype(BF16)
            qhatb = ch["qhat"].astype(BF16)
            eglast = jnp.exp(ch["glast"])
            dqhat = _dot(dob, stb)
            dkdec = _dot(vb, dstb)
            dv = _dot(ch["kdec"].astype(BF16), dstb, NT)
            deg = jnp.sum(dst * st, axis=0, keepdims=True)
            dstate[w] = dst * eglast + _dot(dob, qhatb, TN)
            dA = jnp.where(ch["causal"], _dot(dob, vb, NT), 0.0)
            dv = dv + _dot(ch["A"].astype(BF16), dob, TN)
            dAb = dA.astype(BF16)
            dqs_parts, dgq_parts = [], []
            dk_intra, dgk = None, None
            for i in range(nsub):
                dA_i = dAb[HGRN_SUB * i:HGRN_SUB * (i + 1)]
                dqt = _dot(dA_i, ch["kt"][i])
                dkt = _dot(dA_i, ch["qt"][i], TN)
                dqs_parts.append(dqt * ch["eq"][i])
                dgq_parts.append(dqt * ch["qt"][i].astype(F32))
                dk_i = dkt * ch["ek"][i]
                dgk_i = dkt * ch["kt"][i].astype(F32)
                dk_intra = dk_i if dk_intra is None else dk_intra + dk_i
                dgk = dgk_i if dgk is None else dgk + dgk_i
            dqs_inter = dqhat * ch["eG"]
            dk_state = dkdec * ch["edec"]
            dqs = jnp.concatenate(dqs_parts, axis=0) + dqs_inter
            dk = dk_intra + dk_state
            dG = jnp.concatenate(dgq_parts, axis=0) - dgk + ch["qs"] * dqs_inter - ch["k"] * dk_state
            last_row = lax.broadcasted_iota(jnp.int32, (C, 1), 0) == C - 1
            tail = jnp.sum(dkdec * ch["kdec"], axis=0, keepdims=True) + deg * eglast
            dG = dG + jnp.where(last_row, tail, 0.0)
            anti = (lax.broadcasted_iota(jnp.int32, (C, C), 1)
                    >= lax.broadcasted_iota(jnp.int32, (C, C), 0)).astype(BF16)
            dlf = _tri_sum(anti, dG)
            df = dlf / ch["f"] - dk
            sf = ch["sf"]
            dhf_ref[rs, ls] = (df * (1.0 - lb) * sf * (1.0 - sf)).astype(BF16)
            dlb_acc[:, ls] += jnp.sum(df * (1.0 - sf), axis=0, keepdims=True)
            sq = ch["sq"]
            dhq_ref[rs, ls] = (dqs * (sq * (1.0 + hq * (1.0 - sq)))).astype(BF16)
            dhi_ref[rs, ls] = dv.astype(BF16)

        def chunk(cc, carry):
            for w in range(W):
                head_chunk(w, cg - 1 - cc)
            return carry

        lax.fori_loop(0, cg, chunk, 0)

        @pl.when((b == nseq - 1) & (g == ng - 1))
        def _():
            dl0 = dlb_acc[...] * lb_all * (1.0 - lb_all)
            dlbl_ref[0:1, :] = dl0
            dlbl_ref[1:2, :] = -dl0

    gi, in_specs = _hgrn_specs(nseq, ng, rows, True)
    row = lambda: pl.BlockSpec((rows, W * HGRN_DIM), lambda h, b, g: (b * ng + gi(g), h))
    return pl.pallas_call(
        body, name="hgrn_bwd",
        grid=(HGRN_HEADS // W, nseq, ng),
        in_specs=in_specs + [row(), row(),
                             pl.BlockSpec((None, W, cg, HGRN_DIM, HGRN_DIM), lambda h, b, g: (b, h, gi(g), 0, 0))],
        out_specs=[row(), row(), row(), row(),
                   pl.BlockSpec((W, 1, HGRN_DIM), lambda h, b, g: (h, 0, 0)),
                   pl.BlockSpec((2, W * HGRN_DIM), lambda h, b, g: (0, h))],
        out_shape=[jax.ShapeDtypeStruct((T, 1024), BF16)] * 4
                  + [jax.ShapeDtypeStruct((HGRN_HEADS, 1, HGRN_DIM), F32),
                     jax.ShapeDtypeStruct((2, HGRN_HEADS * HGRN_DIM), F32)],
        scratch_shapes=[pltpu.VMEM((W, HGRN_DIM, HGRN_DIM), F32), pltpu.VMEM((1, W * HGRN_DIM), F32)],
        compiler_params=_params(("arbitrary", "arbitrary", "arbitrary")),
    )(proj, proj, proj, proj, lb_logits, gain3, o, d_yh, states)


def _mid(proj, ya, yh, x2, tgt2, gpost, wa_t, wh_t, wout):
    T = proj.shape[0]
    tm = min(128, T)
    nt = T // tm

    def body(*refs):
        ga_refs, gh_refs = refs[0:4], refs[4:8]
        (ya_ref, yh_ref, x_ref, t_ref, gpost_ref, wa_hbm, wh_hbm, wo_hbm,
         merged_ref, dy_ref, dua_ref, duh_ref, dga_ref, dgh_ref, dya_ref, dyh_ref, dout_ref,
         loss_ref, dgpost_ref, wa, wh, wo) = refs[8:]
        i = pl.program_id(0)

        @pl.when(i == 0)
        def _():
            pltpu.sync_copy(wa_hbm, wa)
            pltpu.sync_copy(wh_hbm, wh)
            pltpu.sync_copy(wo_hbm, wo)
            loss_ref[...] = jnp.zeros(loss_ref.shape, F32)
            dgpost_ref[...] = jnp.zeros(dgpost_ref.shape, F32)

        ga = jnp.concatenate([r[...] for r in ga_refs], axis=1).astype(F32)
        gh = jnp.concatenate([r[...] for r in gh_refs], axis=1).astype(F32)
        sa, sh = _sigmoid(ga), _sigmoid(gh)
        ua = _dot(ya_ref[...], wa[...], NT)
        uh = _dot(yh_ref[...], wh[...], NT)
        merged = (sa * ua + sh * uh).astype(BF16)
        merged_ref[...] = merged
        y = _dot(merged, wo[...])
        r2 = lax.rsqrt(jnp.mean(y * y, axis=-1, keepdims=True) + NORM_EPS)
        yn = y * r2
        gpost = gpost_ref[...]
        err = x_ref[...] + yn * gpost - t_ref[...]
        loss_ref[...] += jnp.sum(err * err, axis=0, keepdims=True)
        dout = err * (1.0 / D_MODEL)
        dout_ref[...] = dout
        dgpost_ref[...] += jnp.sum(dout * yn, axis=0, keepdims=True)
        dyn = dout * gpost
        dy = (r2 * (dyn - yn * jnp.mean(dyn * yn, axis=-1, keepdims=True))).astype(BF16)
        dy_ref[...] = dy
        dm = _dot(dy, wo[...], NT)
        dua = (dm * sa).astype(BF16)
        duh = (dm * sh).astype(BF16)
        dua_ref[...] = dua
        duh_ref[...] = duh
        dga_ref[...] = (dm * ua * (sa * (1.0 - sa))).astype(BF16)
        dgh_ref[...] = (dm * uh * (sh * (1.0 - sh))).astype(BF16)
        dya_ref[...] = _dot(dua, wa[...]).astype(BF16)
        dyh_ref[...] = _dot(duh, wh[...]).astype(BF16)

    gate_spec = lambda c0, q: pl.BlockSpec((tm, 512), lambda i: (i, c0 // 512 + q))
    rowb = lambda w: pl.BlockSpec((tm, w), lambda i: (i, 0))
    vec = lambda: pl.BlockSpec((1, D_MODEL), lambda i: (0, 0))
    anyspec = lambda: pl.BlockSpec(memory_space=pl.ANY)
    out_shapes = ([jax.ShapeDtypeStruct((T, D_MODEL), BF16)] * 6
                  + [jax.ShapeDtypeStruct((T, 1024), BF16)] * 2
                  + [jax.ShapeDtypeStruct((T, D_MODEL), F32),
                     jax.ShapeDtypeStruct((1, D_MODEL), F32), jax.ShapeDtypeStruct((1, D_MODEL), F32)])
    return pl.pallas_call(
        body, name="mid",
        grid=(nt,),
        in_specs=[gate_spec(C_GA, q) for q in range(4)] + [gate_spec(C_GH, q) for q in range(4)]
                 + [rowb(1024), rowb(1024), rowb(D_MODEL), rowb(D_MODEL), vec(), anyspec(), anyspec(), anyspec()],
        out_specs=[rowb(D_MODEL)] * 6 + [rowb(1024)] * 2 + [rowb(D_MODEL), vec(), vec()],
        out_shape=out_shapes,
        scratch_shapes=[pltpu.VMEM((D_MODEL, 1024), BF16), pltpu.VMEM((D_MODEL, 1024), BF16),
                        pltpu.VMEM((D_MODEL, D_MODEL), BF16)],
        compiler_params=_params(("arbitrary",)),
    )(*([proj] * 8), ya, yh, x2, tgt2, gpost, wa_t, wh_t, wout)


def _tn_matmul(L, R, bm, bn, name):
    T, M = L.shape
    N = R.shape[1]

    def body(l_ref, r_ref, out_ref):
        out_ref[...] = _dot(l_ref[...], r_ref[...], TN).astype(BF16)

    return pl.pallas_call(
        body, name=name,
        grid=(N // bn, M // bm),
        in_specs=[pl.BlockSpec((T, bm), lambda j, i: (0, i)),
                  pl.BlockSpec((T, bn), lambda j, i: (0, j))],
        out_specs=pl.BlockSpec((bm, bn), lambda j, i: (i, j)),
        out_shape=jax.ShapeDtypeStruct((M, N), BF16),
        compiler_params=_params(("arbitrary", "arbitrary")),
    )(L, R)


def _dh_prenorm_bwd(dproj, wt_in, x2, dout, gpre, chip_sums):
    T = x2.shape[0]
    tm = min(1024, T)
    ne = 2
    te = tm // ne
    tk = 768
    nk = IN_WIDTH // tk
    nt = T // tm
    n = len(chip_sums)

    def body(dp_ref, w_ref, x_ref, dout_ref, g_ref, *refs):
        ins = refs[:n]
        gx_ref, dg_ref = refs[n], refs[n + 1]
        outs = refs[n + 2:2 * n + 2]
        acc, send_sems, recv_sems, local_sems = refs[2 * n + 2:]
        i, k = pl.program_id(0), pl.program_id(1)
        x, y, c = _place()
        my_chip = 2 * x + y

        def peer(q):
            return (x ^ (q >> 1), y ^ (q & 1))

        def send(a, q):
            px, py = peer(q)
            return pltpu.make_async_remote_copy(
                src_ref=ins[a].at[2 * px + py], dst_ref=outs[a].at[my_chip],
                send_sem=send_sems.at[a, q - 1], recv_sem=recv_sems.at[a, q - 1],
                device_id=(px, py, c), device_id_type=pl.DeviceIdType.MESH)

        def arrival(a, q):
            px, py = peer(q)
            return pltpu.make_async_remote_copy(
                src_ref=ins[a].at[my_chip], dst_ref=outs[a].at[2 * px + py],
                send_sem=send_sems.at[a, q - 1], recv_sem=recv_sems.at[a, q - 1],
                device_id=(px, py, c), device_id_type=pl.DeviceIdType.MESH)

        def local(a):
            return pltpu.make_async_copy(ins[a].at[my_chip], outs[a].at[my_chip], local_sems.at[a])

        @pl.when((i == 0) & (k == 0))
        def _():
            dg_ref[...] = jnp.zeros(dg_ref.shape, F32)
            for a in range(n):
                local(a).start()
                for q in range(1, 4):
                    send(a, q).start()

        @pl.when((i == nt - 1) & (k == nk + ne - 1))
        def _():
            for a in range(n):
                for q in range(1, 4):
                    arrival(a, q).wait_recv()
            for a in range(n):
                for q in range(1, 4):
                    send(a, q).wait_send()
                local(a).wait()

        @pl.when(k == 0)
        def _():
            acc[...] = _dot(dp_ref[...], w_ref[...])

        @pl.when((k > 0) & (k < nk))
        def _():
            acc[...] += _dot(dp_ref[...], w_ref[...])

        @pl.when(k >= nk)
        def _():
            dh = acc[pl.ds(pl.multiple_of((k - nk) * te, te), te), :]
            x = x_ref[...]
            r = lax.rsqrt(jnp.mean(x * x, axis=-1, keepdims=True) + NORM_EPS)
            xn = x * r
            dg_ref[...] += jnp.sum(dh * xn, axis=0, keepdims=True)
            dxn = dh * g_ref[...]
            gx_ref[...] = dout_ref[...] + r * (dxn - xn * jnp.mean(dxn * xn, axis=-1, keepdims=True))

    rowb = lambda: pl.BlockSpec((te, D_MODEL), lambda i, k: (ne * i + jnp.clip(k - nk, 0, ne - 1), 0))
    vec = lambda: pl.BlockSpec((1, D_MODEL), lambda i, k: (0, 0))
    anyspec = lambda: pl.BlockSpec(memory_space=pl.ANY)
    return pl.pallas_call(
        body, name="dh_prenorm_bwd",
        grid=(nt, nk + ne),
        in_specs=[pl.BlockSpec((tm, tk), lambda i, k: (i, jnp.minimum(k, nk - 1))),
                  pl.BlockSpec((tk, D_MODEL), lambda i, k: (jnp.minimum(k, nk - 1), 0)),
                  rowb(), rowb(), vec()] + [anyspec() for _ in chip_sums],
        out_specs=[rowb(), vec()] + [anyspec() for _ in chip_sums],
        out_shape=[jax.ShapeDtypeStruct((T, D_MODEL), F32), jax.ShapeDtypeStruct((1, D_MODEL), F32)]
                  + [jax.ShapeDtypeStruct(s.shape, s.dtype) for s in chip_sums],
        scratch_shapes=[pltpu.VMEM((tm, D_MODEL), F32),
                        pltpu.SemaphoreType.DMA((n, 3)), pltpu.SemaphoreType.DMA((n, 3)),
                        pltpu.SemaphoreType.DMA((n,))],
        compiler_params=_params(("arbitrary", "arbitrary")),
    )(dproj, wt_in, x2, dout, gpre, *chip_sums)


def _sum_slots(recv, br, name):
    nslot, R, C = recv.shape

    def body(r_ref, out_ref):
        acc = r_ref[0].astype(F32)
        for s in range(1, nslot):
            acc = acc + r_ref[s].astype(F32)
        out_ref[...] = acc

    return pl.pallas_call(
        body, name=name,
        grid=(R // br,),
        in_specs=[pl.BlockSpec((nslot, br, C), lambda i: (0, i, 0))],
        out_specs=pl.BlockSpec((br, C), lambda i: (i, 0)),
        out_shape=jax.ShapeDtypeStruct((R, C), F32),
        compiler_params=_params(("arbitrary",)),
    )(recv)


def _adamw_math(w, g, m, v):
    m = ADAM_B1 * m + (1.0 - ADAM_B1) * g
    v = ADAM_B2 * v + (1.0 - ADAM_B2) * (g * g)
    m_hat = m / (1.0 - ADAM_B1 ** ADAM_STEP)
    v_hat = v / (1.0 - ADAM_B2 ** ADAM_STEP)
    delta = -ADAM_LR * (m_hat / (jnp.sqrt(v_hat) + ADAM_EPS) + ADAM_WD * w)
    return delta, m, v


def _adamw(w, g, m, v, br, name):
    R, C = w.shape

    def body(w_ref, g_ref, m_ref, v_ref, d_ref, nm_ref, nv_ref):
        d_ref[...], nm_ref[...], nv_ref[...] = _adamw_math(w_ref[...], g_ref[...], m_ref[...], v_ref[...])

    spec = lambda: pl.BlockSpec((br, C), lambda i: (i, 0))
    return pl.pallas_call(
        body, name=name,
        grid=(R // br,),
        in_specs=[spec(), spec(), spec(), spec()],
        out_specs=[spec(), spec(), spec()],
        out_shape=[jax.ShapeDtypeStruct((R, C), F32)] * 3,
        compiler_params=_params(("arbitrary",)),
    )(w, g, m, v)


def _sibling_exchange(partials):
    n = len(partials)

    def body(*refs):
        ins, outs = refs[:n], refs[n:2 * n]
        send_sems, recv_sems = refs[2 * n:]
        x, y, c = _place()

        def copy(a, p):
            return pltpu.make_async_remote_copy(
                src_ref=ins[a].at[p, 1 - c], dst_ref=outs[a].at[p],
                send_sem=send_sems.at[a, p], recv_sem=recv_sems.at[a, p],
                device_id=(x, y, 1 - c), device_id_type=pl.DeviceIdType.MESH)

        copies = [copy(a, p) for p in range(4) for a in range(n)]
        for cp in copies:
            cp.start()
        for cp in copies:
            cp.wait()

    anyspec = lambda: pl.BlockSpec(memory_space=pl.ANY)
    return pl.pallas_call(
        body, name="sibling_exchange",
        in_specs=[anyspec() for _ in partials],
        out_specs=[anyspec() for _ in partials],
        out_shape=[jax.ShapeDtypeStruct((4,) + p.shape[2:], p.dtype) for p in partials],
        scratch_shapes=[pltpu.SemaphoreType.DMA((n, 4)), pltpu.SemaphoreType.DMA((n, 4))],
    )(*partials)


def _chip_sum(partial, from_sibling, br, name):
    _, _, R, C = partial.shape
    cls = lax.axis_index("c").astype(jnp.int32).reshape(1)

    def body(c_ref, mine_ref, sib_ref, out_ref):
        out_ref[...] = (mine_ref[...].astype(F32) + sib_ref[...].astype(F32)).astype(BF16)

    grid_spec = pltpu.PrefetchScalarGridSpec(
        num_scalar_prefetch=1,
        grid=(4, R // br),
        in_specs=[pl.BlockSpec((None, None, br, C), lambda p, i, c: (p, c[0], i, 0)),
                  pl.BlockSpec((None, br, C), lambda p, i, c: (p, i, 0))],
        out_specs=pl.BlockSpec((None, br, C), lambda p, i, c: (p, i, 0)),
    )
    return pl.pallas_call(
        body, name=name, grid_spec=grid_spec,
        out_shape=jax.ShapeDtypeStruct((4, R, C), BF16),
        compiler_params=_params(("arbitrary", "arbitrary")),
    )(cls, partial, from_sibling)


def _all_reduce_small(packed):
    shape = packed.shape

    def body(in_ref, out_ref, slots, send_sems, recv_sems):
        x, y, c = _place()
        my_slot = 4 * x + 2 * y + c

        def peer(k):
            return (x ^ ((k >> 2) & 1), y ^ ((k >> 1) & 1), c ^ (k & 1))

        def copy(k):
            p = peer(k)
            return pltpu.make_async_remote_copy(
                src_ref=in_ref, dst_ref=slots.at[my_slot],
                send_sem=send_sems.at[k - 1], recv_sem=recv_sems.at[k - 1],
                device_id=p, device_id_type=pl.DeviceIdType.MESH)

        def arrival(k):
            p = peer(k)
            return pltpu.make_async_remote_copy(
                src_ref=in_ref, dst_ref=slots.at[4 * p[0] + 2 * p[1] + p[2]],
                send_sem=send_sems.at[k - 1], recv_sem=recv_sems.at[k - 1],
                device_id=p, device_id_type=pl.DeviceIdType.MESH)

        sends = [copy(k) for k in range(1, N_DEV)]
        for cp in sends:
            cp.start()
        slots[my_slot] = in_ref[...]
        for k in range(1, N_DEV):
            arrival(k).wait_recv()
        for cp in sends:
            cp.wait_send()
        acc = slots[0]
        for s in range(1, N_DEV):
            acc = acc + slots[s]
        out_ref[...] = acc

    return pl.pallas_call(
        body, name="all_reduce_small",
        in_specs=[pl.BlockSpec(memory_space=pltpu.VMEM)],
        out_specs=pl.BlockSpec(memory_space=pltpu.VMEM),
        out_shape=jax.ShapeDtypeStruct(shape, F32),
        scratch_shapes=[pltpu.VMEM((N_DEV,) + shape, F32),
                        pltpu.SemaphoreType.DMA((7,)), pltpu.SemaphoreType.DMA((7,))],
    )(packed)


def _pack_small(norm_pre, norm_post, lb_logits, hgrn_norm, rel_bias, sinks, extra=None):
    tail = [hgrn_norm.reshape(1, 1024), rel_bias.reshape(1, 512), sinks.reshape(1, 16)]
    used = 1024 + 512 + 16
    if extra is not None:
        tail.append(extra.reshape(1, 1))
        used += 1
    tail.append(jnp.zeros((1, D_MODEL - used), F32))
    rows = [norm_pre.reshape(1, D_MODEL), norm_post.reshape(1, D_MODEL), lb_logits.reshape(1, D_MODEL),
            jnp.concatenate(tail, axis=1), jnp.zeros((4, D_MODEL), F32)]
    return jnp.concatenate(rows, axis=0)


def _unpack_small(p):
    return (p[0:1], p[3, 1024:1536].reshape(REL_BUCKETS, ATTN_HEADS), p[3:4, 1536:1552],
            p[2].reshape(2, 1024), p[3, 0:1024].reshape(1, HGRN_HEADS, HGRN_DIM), p[1:2])


def _local_step(nseq, S, x2, tgt2, proj, h, rel_bias, attn_sinks, lb_logits, hgrn_norm, norm_post, shards):
    nb = S // ATTN_BLOCK
    bucket = jnp.asarray(_t5_bucket_table())
    gain3 = hgrn_norm.reshape(HGRN_HEADS, 1, HGRN_DIM)

    bias = _bias_table(rel_bias, bucket)
    ya = _attn_fwd(proj, bias, attn_sinks, nseq, nb)
    o, yh, states, wout, wa_t, wh_t = _hgrn_fwd(proj, lb_logits, gain3, nseq, S, shards)
    (merged, dy, dua, duh, dga, dgh, dya, dyh, dout, loss_cols, d_gpost) = _mid(
        proj, ya, yh, x2, tgt2, norm_post, wa_t, wh_t, wout)

    dq, dkv, dg, dbias, d_sinks = _attn_bwd(proj, bias, attn_sinks, dya, nseq, nb)
    d_rel_bias = _bias_table_bwd(dbias, bucket)
    dhq, dhf, dhi, dhg, d_gain, d_lbl = _hgrn_bwd(proj, lb_logits, gain3, o, dyh, states, nseq, S)
    dproj = jnp.concatenate([dq, dkv, dg, dhq, dhf, dhi, dhg, dga, dgh], axis=1)

    p_in = _tn_matmul(dproj, h, 768, 1024, "dw_in")
    p_out = _tn_matmul(merged, dy, 256, 1024, "dw_out")
    p_a = _tn_matmul(dua, ya, 256, 1024, "dw_branch_attn")
    p_h = _tn_matmul(duh, yh, 256, 1024, "dw_branch_hgrn")
    return dproj, dout, p_in, p_out, p_a, p_h, d_gpost, d_lbl, d_gain, d_rel_bias, d_sinks, loss_cols


def kernel(x, norm_pre, w_in, rel_bias, attn_sinks, lb_logits, hgrn_norm, w_branch_attn, w_branch_hgrn, w_out, norm_post, loss_target, m_norm_pre, m_w_in, m_rel_bias, m_attn_sinks, m_lb_logits, m_hgrn_norm, m_w_branch_attn, m_w_branch_hgrn, m_w_out, m_norm_post, v_norm_pre, v_w_in, v_rel_bias, v_attn_sinks, v_lb_logits, v_hgrn_norm, v_w_branch_attn, v_w_branch_hgrn, v_w_out, v_norm_post):
    nseq, S, _ = x.shape
    T = nseq * S
    x2 = x.reshape(T, D_MODEL)
    tgt2 = loss_target.reshape(T, D_MODEL)

    h = _prenorm(x2, norm_pre)
    proj, wt_in = _gather_inproj(h, w_in[0].T.astype(BF16))
    shards = [w_out[0].astype(BF16), w_branch_attn[0].T.astype(BF16), w_branch_hgrn[0].T.astype(BF16)]

    (dproj, dout, p_in, p_out, p_a, p_h, d_gpost, d_lbl, d_gain, d_rel_bias, d_sinks, loss_cols) = _local_step(
        nseq, S, x2, tgt2, proj, h, rel_bias, attn_sinks, lb_logits, hgrn_norm, norm_post, shards)

    partials = [p.reshape(4, 2, p.shape[0] // N_DEV, p.shape[1]) for p in (p_in, p_out, p_a, p_h)]
    from_sib = _sibling_exchange(partials)
    chip_sums = [_chip_sum(p, f, br, "chip_sum_" + nm) for p, f, br, nm in zip(
        partials, from_sib, (192, 128, 128, 128), ("dw_in", "dw_out", "dw_branch_attn", "dw_branch_hgrn"))]
    grad_x2, d_gpre, r_in, r_out, r_a, r_h = _dh_prenorm_bwd(dproj, wt_in, x2, dout, norm_pre, chip_sums)
    g_in = _sum_slots(r_in, 192, "sum_dw_in").T
    g_out = _sum_slots(r_out, 128, "sum_dw_out")
    g_a = _sum_slots(r_a, 128, "sum_dw_branch_attn").T
    g_h = _sum_slots(r_h, 128, "sum_dw_branch_hgrn").T

    loss_part = 0.5 / D_MODEL * jnp.sum(loss_cols)
    packed = _pack_small(d_gpre, d_gpost, d_lbl, d_gain, d_rel_bias, d_sinks, extra=loss_part)
    total = _all_reduce_small(packed)
    loss = total[3, 1024 + 512 + 16]
    sm_w = _pack_small(norm_pre, norm_post, lb_logits, hgrn_norm, rel_bias, attn_sinks)
    sm_m = _pack_small(m_norm_pre, m_norm_post, m_lb_logits, m_hgrn_norm, m_rel_bias, m_attn_sinks)
    sm_v = _pack_small(v_norm_pre, v_norm_post, v_lb_logits, v_hgrn_norm, v_rel_bias, v_attn_sinks)
    sm_d, sm_nm, sm_nv = _adamw(sm_w, total, sm_m, sm_v, 8, "adamw_small")

    d_in, nm_in, nv_in = _adamw(w_in[0], g_in, m_w_in[0], v_w_in[0], 256, "adamw_w_in")
    d_out, nm_out, nv_out = _adamw(w_out[0], g_out, m_w_out[0], v_w_out[0], 128, "adamw_w_out")
    d_a, nm_a, nv_a = _adamw(w_branch_attn[0], g_a, m_w_branch_attn[0], v_w_branch_attn[0], 256, "adamw_w_branch_attn")
    d_h, nm_h, nv_h = _adamw(w_branch_hgrn[0], g_h, m_w_branch_hgrn[0], v_w_branch_hgrn[0], 256, "adamw_w_branch_hgrn")

    def group(small, big_in, big_a, big_h, big_out):
        npre, rb, sk, lbl, hn, npost = _unpack_small(small)
        return (npre, big_in[None], rb, sk, lbl, hn, big_a[None], big_h[None], big_out[None], npost)

    return (loss, grad_x2.reshape(nseq, S, D_MODEL),
            *group(total, g_in, g_a, g_h, g_out),
            *group(sm_d, d_in, d_a, d_h, d_out),
            *group(sm_nm, nm_in, nm_a, nm_h, nm_out),
            *group(sm_nv, nv_in, nv_a, nv_h, nv_out))
```

```python
import functools
import math

import numpy as np
import jax
import jax.numpy as jnp
from jax import lax
from jax.experimental import pallas as pl
from jax.experimental.pallas import tpu as pltpu

F32 = jnp.float32
BF16 = jnp.bfloat16

D_MODEL = 2048
ATTN_HEADS = 16
ATTN_HEAD_DIM = 64
ATTN_GROUP = 4
ATTN_BLOCK = 128
HGRN_HEADS = 8
HGRN_DIM = 128
HGRN_CHUNK = 64
HGRN_SUB = 16
HGRN_PAR = 8
HGRN_COLS = 512
REL_BUCKETS = 32
REL_MAX_DIST = 128
NORM_EPS = 1e-6
C_AQ, C_AK, C_AV, C_AG = 0, 1024, 1280, 1536
C_HQ, C_HF, C_HI, C_HG = 2560, 3584, 4608, 5632
C_GA, C_GH = 6656, 8704
IN_WIDTH = 10752
N_DEV = 8
assert all(c0 % HGRN_COLS == 0 for c0 in (C_HQ, C_HF, C_HI, C_HG)) and (HGRN_PAR * HGRN_DIM) % HGRN_COLS == 0

ADAM_LR = 0.001
ADAM_B1 = 0.9
ADAM_B2 = 0.999
ADAM_EPS = 1e-08
ADAM_WD = 0.01
ADAM_STEP = 10

VMEM_LIMIT_V7X = 56 * 1024 * 1024
NEG_BIG = -1e30

NT = (((1,), (1,)), ((), ()))
TN = (((0,), (0,)), ((), ()))
NN = (((1,), (0,)), ((), ()))


def _dot(a, b, dims=NN):
    return lax.dot_general(a, b, dims, preferred_element_type=F32)


def _params(sem=None):
    return pltpu.CompilerParams(dimension_semantics=sem, vmem_limit_bytes=VMEM_LIMIT_V7X)


def _sigmoid(x):
    return 1.0 / (1.0 + jnp.exp(-x))


def _t5_bucket_table():
    qi = np.arange(ATTN_BLOCK)[:, None]
    si = np.arange(2 * ATTN_BLOCK)[None, :]
    dist = qi + ATTN_BLOCK - si
    max_exact = REL_BUCKETS // 2
    d = np.maximum(dist, 0)
    df = np.maximum(d, 1).astype(np.float32)
    large = max_exact + (np.log(df / np.float32(max_exact)).astype(np.float32)
                         / np.float32(math.log(REL_MAX_DIST / max_exact))
                         * np.float32(REL_BUCKETS - max_exact)).astype(np.int32)
    large = np.minimum(large, REL_BUCKETS - 1)
    return np.where(d < max_exact, d, large).astype(np.int32)


def _place():
    return lax.axis_index("x"), lax.axis_index("y"), lax.axis_index("c")


class _GatherOps:
    def __init__(self, ins, outs, send_sems, recv_sems, local_sems):
        self.ins, self.outs = ins, outs
        self.send_sems, self.recv_sems, self.local_sems = send_sems, recv_sems, local_sems
        x, y, c = _place()
        self.c = c
        self.me, self.sibling = (x, y, c), (x, y, 1 - c)
        self.chips = [(1 - x, y), (x, 1 - y), (1 - x, 1 - y)]

    def _rows(self, a, dev):
        r = self.ins[a].shape[0]
        return self.outs[a].at[pl.ds((4 * dev[0] + 2 * dev[1] + dev[2]) * r, r), :]

    def _copy(self, a, k, block, to, src=None):
        return pltpu.make_async_remote_copy(
            src_ref=self._rows(a, block) if src is None else src, dst_ref=self._rows(a, block),
            send_sem=self.send_sems.at[a, k], recv_sem=self.recv_sems.at[a, k],
            device_id=to, device_id_type=pl.DeviceIdType.MESH)

    def local(self, a):
        return pltpu.make_async_copy(self.ins[a], self._rows(a, self.me), self.local_sems.at[a])

    def to_sibling(self, a):
        return self._copy(a, 0, self.me, self.sibling, src=self.ins[a])

    def to_chip(self, a, q):
        return self._copy(a, 1 + q, self.me, (*self.chips[q], self.c), src=self.ins[a])

    def forward(self, a, q):
        return self._copy(a, 4 + q, (*self.chips[q], self.c), self.sibling)

    def from_sibling(self, a):
        return self._copy(a, 0, self.sibling, self.me)

    def from_chip(self, a, q):
        return self._copy(a, 1 + q, (*self.chips[q], self.c), self.me)

    def forwarded(self, a, q):
        return self._copy(a, 4 + q, (*self.chips[q], 1 - self.c), self.me)

    def sends(self, a):
        return [self.to_sibling(a)] + [self.to_chip(a, q) for q in range(3)] + [self.forward(a, q) for q in range(3)]


_GATHER_SEMS = 7
LOCAL_DMA_THREAD = 1

INPROJ_TILE = 896


def _prenorm(x2, gpre):
    T = x2.shape[0]
    tm = min(512, T)

    def body(x_ref, g_ref, h_ref):
        x = x_ref[...]
        r = lax.rsqrt(jnp.mean(x * x, axis=-1, keepdims=True) + NORM_EPS)
        h_ref[...] = (x * r * g_ref[...]).astype(BF16)

    return pl.pallas_call(
        body, name="prenorm",
        grid=(T // tm,),
        in_specs=[pl.BlockSpec((tm, D_MODEL), lambda i: (i, 0)), pl.BlockSpec((1, D_MODEL), lambda i: (0, 0))],
        out_specs=pl.BlockSpec((tm, D_MODEL), lambda i: (i, 0)),
        out_shape=jax.ShapeDtypeStruct((T, D_MODEL), BF16),
        compiler_params=_params(("arbitrary",)),
    )(x2, gpre)


def _gather_inproj(h, wt_shard):
    T = h.shape[0]
    tm = min(1024, T)
    nm = T // tm
    tn = INPROJ_TILE
    ntile = IN_WIDTH // tn
    nstep = ntile * nm

    def body(h_hbm, w_in, proj_hbm, w_out, hbuf, wtile, obuf, send_sems, recv_sems, local_sems, h_sem, w_sems, o_sems):
        j, i = pl.program_id(0), pl.program_id(1)
        step = j * nm + i
        slot = step % 2
        ops = _GatherOps([w_in], [w_out], send_sems, recv_sems, local_sems)
        x, y, _ = _place()

        def tile_of(jj):
            k = jj // 3
            return 3 * ((2 * x + y) ^ (((k & 1) << 1) | (k >> 1))) + jj % 3

        tile = tile_of(j)

        def h_load():
            return pltpu.make_async_copy(h_hbm, hbuf, h_sem)

        def store(s, rows, cols):
            return pltpu.make_async_copy(obuf.at[s], proj_hbm.at[rows, cols], o_sems.at[s])

        def window(ii, t):
            return pl.ds(pl.multiple_of(ii * tm, tm), tm), pl.ds(pl.multiple_of(t * tn, tn), tn)

        @pl.when(step == 0)
        def _():
            h_load().start()
            ops.local(0).start()
            ops.to_sibling(0).start()
            ops.to_chip(0, 0).start()
            ops.to_chip(0, 1).start()
            h_load().wait()

        for kk in range(4):
            @pl.when((j == 3 * kk) & (i == 0))
            def _(kk=kk):
                if kk == 0:
                    ops.local(0).wait()
                    ops.from_sibling(0).wait_recv()
                else:
                    q = kk - 1
                    ops.from_chip(0, q).wait_recv()
                    ops.forward(0, q).start()
                    if q == 0:
                        ops.to_chip(0, 2).start()
                    ops.forwarded(0, q).wait_recv()

        wslot = j % 2

        def fetch(jj, sw):
            rows = pl.ds(pl.multiple_of(tile_of(jj) * tn, tn), tn)
            return pltpu.make_async_copy(w_out.at[rows, :], wtile.at[sw], w_sems.at[sw])

        @pl.when((i == 0) & (j % 3 == 0))
        def _():
            fetch(j, wslot).start(LOCAL_DMA_THREAD)

        @pl.when(i == 0)
        def _():
            fetch(j, wslot).wait()

        @pl.when((i == 0) & (j % 3 != 2))
        def _():
            fetch(j + 1, 1 - wslot).start(LOCAL_DMA_THREAD)

        @pl.when(step >= 2)
        def _():
            store(slot, *window(0, 0)).wait()

        hv = hbuf[pl.ds(pl.multiple_of(i * tm, tm), tm), :]
        obuf[slot] = _dot(hv, wtile[wslot], NT).astype(BF16)
        store(slot, *window(i, tile)).start(LOCAL_DMA_THREAD)

        @pl.when(step == nstep - 1)
        def _():
            for s in range(min(2, nstep)):
                store(s, *window(0, 0)).wait()
            for cp in ops.sends(0):
                cp.wait_send()

    anyspec = lambda: pl.BlockSpec(memory_space=pl.ANY)
    return pl.pallas_call(
        body, name="gather_inproj",
        grid=(ntile, nm),
        in_specs=[anyspec(), anyspec()],
        out_specs=[anyspec(), anyspec()],
        out_shape=[jax.ShapeDtypeStruct((T, IN_WIDTH), BF16),
                   jax.ShapeDtypeStruct((N_DEV * wt_shard.shape[0], D_MODEL), BF16)],
        scratch_shapes=[pltpu.VMEM((T, D_MODEL), BF16), pltpu.VMEM((2, tn, D_MODEL), BF16),
                        pltpu.VMEM((2, tm, tn), BF16),
                        pltpu.SemaphoreType.DMA((1, _GATHER_SEMS)), pltpu.SemaphoreType.DMA((1, _GATHER_SEMS)),
                        pltpu.SemaphoreType.DMA((1,)), pltpu.SemaphoreType.DMA, pltpu.SemaphoreType.DMA((2,)),
                        pltpu.SemaphoreType.DMA((2,))],
        compiler_params=_params(("arbitrary", "arbitrary")),
    )(h, wt_shard)


def _bias_table(rel_bias, bucket):
    def body(rb_ref, bk_ref, out_ref):
        h = pl.program_id(0)
        bk = bk_ref[...]
        acc = jnp.zeros(bk.shape, F32)
        for b in range(REL_BUCKETS):
            acc = jnp.where(bk == b, rb_ref[b, h], acc)
        out_ref[...] = acc

    return pl.pallas_call(
        body, name="bias_table",
        grid=(ATTN_HEADS,),
        in_specs=[pl.BlockSpec(memory_space=pltpu.SMEM),
                  pl.BlockSpec((ATTN_BLOCK, 2 * ATTN_BLOCK), lambda h: (0, 0))],
        out_specs=pl.BlockSpec((None, ATTN_BLOCK, 2 * ATTN_BLOCK), lambda h: (h, 0, 0)),
        out_shape=jax.ShapeDtypeStruct((ATTN_HEADS, ATTN_BLOCK, 2 * ATTN_BLOCK), F32),
        compiler_params=_params(("arbitrary",)),
    )(rel_bias, bucket)


def _bias_table_bwd(dbias, bucket):
    def body(db_ref, bk_ref, out_ref):
        h = pl.program_id(0)
        bk = bk_ref[...]
        db = db_ref[...]
        for b in range(REL_BUCKETS):
            out_ref[b, h] = jnp.sum(jnp.where(bk == b, db, 0.0))

    return pl.pallas_call(
        body, name="bias_table_bwd",
        grid=(ATTN_HEADS,),
        in_specs=[pl.BlockSpec((None, ATTN_BLOCK, 2 * ATTN_BLOCK), lambda h: (h, 0, 0)),
                  pl.BlockSpec((ATTN_BLOCK, 2 * ATTN_BLOCK), lambda h: (0, 0))],
        out_specs=pl.BlockSpec(memory_space=pltpu.SMEM),
        out_shape=jax.ShapeDtypeStruct((REL_BUCKETS, ATTN_HEADS), F32),
        compiler_params=_params(("arbitrary",)),
    )(dbias, bucket)


def _attn_common(qkvg, kv_prev, blk):
    lane = lax.broadcasted_iota(jnp.int32, (1, 128), 1)
    half = (lane < ATTN_HEAD_DIM, lane >= ATTN_HEAD_DIM)
    kv_cur = qkvg[:, C_AK:C_AG]
    win = jnp.concatenate([kv_prev, kv_cur], axis=0)
    k_slab, v_slab = [], []
    for r in range(2):
        ks = win[:, 128 * r:128 * r + 128]
        vs = win[:, 256 + 128 * r:256 + 128 * r + 128]
        k_slab.append((ks, pltpu.roll(ks, ATTN_HEAD_DIM, 1)))
        v_slab.append((vs, pltpu.roll(vs, ATTN_HEAD_DIM, 1)))
    rows4 = ATTN_GROUP * ATTN_BLOCK
    qi = lax.broadcasted_iota(jnp.int32, (rows4, 2 * ATTN_BLOCK), 0) & (ATTN_BLOCK - 1)
    si = lax.broadcasted_iota(jnp.int32, (rows4, 2 * ATTN_BLOCK), 1)
    valid = (si > qi) & (si <= qi + ATTN_BLOCK) & ((si >= ATTN_BLOCK) | (blk > 0))
    return half, k_slab, v_slab, valid


def _stack_heads(half, slab0, slab1):
    return jnp.concatenate([jnp.where(half[0], slab0, 0.0), jnp.where(half[1], slab0, 0.0),
                            jnp.where(half[0], slab1, 0.0), jnp.where(half[1], slab1, 0.0)], axis=0)


def _unstack_heads(half, x4):
    B = ATTN_BLOCK
    return (jnp.where(half[0], x4[0:B], x4[B:2 * B]), jnp.where(half[0], x4[2 * B:3 * B], x4[3 * B:4 * B]))


def _attn_group(j, qkvg, half, k_slab, v_slab, valid, bias_ref, sinks_ref):
    r, aj = j // 2, j % 2
    pick = (lambda a, b: jnp.where(half[0], a, b)) if aj == 0 else (lambda a, b: jnp.where(half[0], b, a))
    kb = pick(*k_slab[r]).astype(BF16)
    vb = pick(*v_slab[r]).astype(BF16)
    q4 = _stack_heads(half, qkvg[:, 256 * j:256 * j + 128], qkvg[:, 256 * j + 128:256 * j + 256]).astype(BF16)
    bias4 = bias_ref[ATTN_GROUP * j:ATTN_GROUP * (j + 1)].reshape(valid.shape)
    s = _dot(q4, kb, NT) * (ATTN_HEAD_DIM ** -0.5) + bias4
    s = jnp.where(valid, s, NEG_BIG)
    rowblk = lax.broadcasted_iota(jnp.int32, (valid.shape[0], 1), 0) // ATTN_BLOCK
    sink = jnp.full((valid.shape[0], 1), sinks_ref[0, ATTN_GROUP * j], F32)
    for b in range(1, ATTN_GROUP):
        sink = jnp.where(rowblk == b, sinks_ref[0, ATTN_GROUP * j + b], sink)
    m = jnp.maximum(jnp.max(s, axis=-1, keepdims=True), sink)
    e = jnp.exp(s - m)
    es = jnp.exp(sink - m)
    inv = 1.0 / (jnp.sum(e, axis=-1, keepdims=True) + es)
    pn = e * inv
    o4 = _dot(pn.astype(BF16), vb)
    return dict(r=r, aj=aj, kb=kb, vb=vb, q4=q4, pn=pn, psink=es * inv, o4=o4)


def _attn_specs(nb):
    row = lambda b, i: b * nb + i
    return [
        pl.BlockSpec((ATTN_BLOCK, C_HQ), lambda b, i: (row(b, i), 0)),
        pl.BlockSpec((ATTN_BLOCK, 512), lambda b, i: (row(b, jnp.maximum(i - 1, 0)), 2)),
        pl.BlockSpec((ATTN_HEADS, ATTN_BLOCK, 2 * ATTN_BLOCK), lambda b, i: (0, 0, 0)),
        pl.BlockSpec(memory_space=pltpu.SMEM),
    ]


def _attn_fwd(proj, bias, sinks, nseq, nb):
    T = proj.shape[0]

    def body(qkvg_ref, kvp_ref, bias_ref, sinks_ref, ya_ref):
        qkvg = qkvg_ref[...].astype(F32)
        half, k_slab, v_slab, valid = _attn_common(qkvg, kvp_ref[...].astype(F32), pl.program_id(1))
        slabs = []
        for j in range(ATTN_HEADS // ATTN_GROUP):
            grp = _attn_group(j, qkvg, half, k_slab, v_slab, valid, bias_ref, sinks_ref)
            slabs += _unstack_heads(half, grp["o4"])
        o_all = jnp.concatenate(slabs, axis=1)
        g = qkvg[:, C_AG:C_HQ]
        ya_ref[...] = (o_all * (g * _sigmoid(g))).astype(BF16)

    return pl.pallas_call(
        body, name="attn_fwd",
        grid=(nseq, nb),
        in_specs=_attn_specs(nb),
        out_specs=pl.BlockSpec((ATTN_BLOCK, 1024), lambda b, i: (b * nb + i, 0)),
        out_shape=jax.ShapeDtypeStruct((T, 1024), BF16),
        compiler_params=_params(("arbitrary", "arbitrary")),
    )(proj, proj, bias, sinks)


def _attn_bwd(proj, bias, sinks, d_ya, nseq, nb):
    T = proj.shape[0]
    S = nb * ATTN_BLOCK
    scale = ATTN_HEAD_DIM ** -0.5

    def body(qkvg_ref, kvp_ref, bias_ref, sinks_ref, dya_ref, dq_ref, dkv_ref, dg_ref, dbias_ref, dsinks_ref):
        b, i = pl.program_id(0), pl.program_id(1)
        first = (b == 0) & (i == 0)

        @pl.when(first)
        def _():
            dbias_ref[...] = jnp.zeros(dbias_ref.shape, F32)
            for h in range(ATTN_HEADS):
                dsinks_ref[0, h] = 0.0

        qkvg = qkvg_ref[...].astype(F32)
        half, k_slab, v_slab, valid = _attn_common(qkvg, kvp_ref[...].astype(F32), i)
        g = qkvg[:, C_AG:C_HQ]
        sg = _sigmoid(g)
        silu_g = g * sg
        dya = dya_ref[...].astype(F32)
        do_all = dya * silu_g
        dq_slabs, o_slabs = [], []
        dk_slab, dv_slab = [None, None], [None, None]
        B = ATTN_BLOCK

        def fold(x, aj):
            return jnp.where(half[aj], x + pltpu.roll(x, ATTN_HEAD_DIM, 1), 0.0)

        for j in range(ATTN_HEADS // ATTN_GROUP):
            grp = _attn_group(j, qkvg, half, k_slab, v_slab, valid, bias_ref, sinks_ref)
            r, aj, pn = grp["r"], grp["aj"], grp["pn"]
            do4 = _stack_heads(half, do_all[:, 256 * j:256 * j + 128], do_all[:, 256 * j + 128:256 * j + 256])
            do4b = do4.astype(BF16)
            dp = _dot(do4b, grp["vb"], NT)
            delta = jnp.sum(do4 * grp["o4"], axis=-1, keepdims=True)
            ds = pn * (dp - delta)
            sink_term = grp["psink"] * delta
            for b4 in range(ATTN_GROUP):
                dsinks_ref[0, ATTN_GROUP * j + b4] += -jnp.sum(sink_term[B * b4:B * (b4 + 1)])
            dbias_ref[ATTN_GROUP * j:ATTN_GROUP * (j + 1)] += ds.reshape(ATTN_GROUP, B, 2 * B)
            dsb = ds.astype(BF16)
            dq_slabs += _unstack_heads(half, _dot(dsb, grp["kb"]) * scale)
            o_slabs += _unstack_heads(half, grp["o4"])
            dk_j = fold(_dot(dsb, grp["q4"], TN) * scale, aj)
            dv_j = fold(_dot(pn.astype(BF16), do4b, TN), aj)
            dk_slab[r] = dk_j if dk_slab[r] is None else dk_slab[r] + dk_j
            dv_slab[r] = dv_j if dv_slab[r] is None else dv_slab[r] + dv_j

        dq_ref[...] = jnp.concatenate(dq_slabs, axis=1).astype(BF16)
        o_all = jnp.concatenate(o_slabs, axis=1)
        dg_ref[...] = (dya * o_all * (sg * (1.0 + g * (1.0 - sg)))).astype(BF16)

        dkv = jnp.concatenate(dk_slab + dv_slab, axis=1)
        cur = pl.multiple_of(i * ATTN_BLOCK, ATTN_BLOCK)
        dkv_ref[pl.ds(cur, ATTN_BLOCK), :] = dkv[ATTN_BLOCK:].astype(BF16)

        @pl.when(i > 0)
        def _():
            prev = pl.multiple_of((i - 1) * ATTN_BLOCK, ATTN_BLOCK)
            old = dkv_ref[pl.ds(prev, ATTN_BLOCK), :].astype(F32)
            dkv_ref[pl.ds(prev, ATTN_BLOCK), :] = (old + dkv[:ATTN_BLOCK]).astype(BF16)

    row_spec = lambda w: pl.BlockSpec((ATTN_BLOCK, w), lambda b, i: (b * nb + i, 0))
    return pl.pallas_call(
        body, name="attn_bwd",
        grid=(nseq, nb),
        in_specs=_attn_specs(nb) + [row_spec(1024)],
        out_specs=[row_spec(1024),
                   pl.BlockSpec((S, 512), lambda b, i: (b, 0)),
                   row_spec(1024),
                   pl.BlockSpec((ATTN_HEADS, ATTN_BLOCK, 2 * ATTN_BLOCK), lambda b, i: (0, 0, 0)),
                   pl.BlockSpec(memory_space=pltpu.SMEM)],
        out_shape=[jax.ShapeDtypeStruct((T, 1024), BF16),
                   jax.ShapeDtypeStruct((T, 512), BF16),
                   jax.ShapeDtypeStruct((T, 1024), BF16),
                   jax.ShapeDtypeStruct((ATTN_HEADS, ATTN_BLOCK, 2 * ATTN_BLOCK), F32),
                   jax.ShapeDtypeStruct((1, ATTN_HEADS), F32)],
        compiler_params=_params(("arbitrary", "arbitrary")),
    )(proj, proj, bias, sinks, d_ya)


def _split3(x):
    hi = x.astype(BF16)
    r1 = x - hi.astype(F32)
    mid = r1.astype(BF16)
    lo = (r1 - mid.astype(F32)).astype(BF16)
    return jnp.concatenate([hi, mid, lo], axis=1)


def _tri_sum(tri, x):
    y = _dot(tri, _split3(x))
    return y[:, :128] + y[:, 128:256] + y[:, 256:]


def _hgrn_chunk(hq, hf, hi, lb):
    C = HGRN_CHUNK
    t = lax.broadcasted_iota(jnp.int32, (C, C), 0)
    s = lax.broadcasted_iota(jnp.int32, (C, C), 1)
    causal = s <= t
    sf = _sigmoid(hf)
    f = lb + (1.0 - lb) * sf
    G = _tri_sum(causal.astype(BF16), jnp.log(f))
    sq = _sigmoid(hq)
    qs = hq * sq
    k = 1.0 - f
    rowblk = lax.broadcasted_iota(jnp.int32, (C, 1), 0) // HGRN_SUB
    qt, kt, eq, ek = [], [], [], []
    for i in range(C // HGRN_SUB):
        lo = HGRN_SUB * i
        ref = G[lo + HGRN_SUB // 2:lo + HGRN_SUB // 2 + 1, :]
        eq_i = jnp.exp(G[lo:lo + HGRN_SUB] - ref)
        ek_i = jnp.exp(jnp.where(rowblk <= i, ref - G, 0.0))
        eq.append(eq_i)
        ek.append(ek_i)
        qt.append((qs[lo:lo + HGRN_SUB] * eq_i).astype(BF16))
        kt.append((k * ek_i).astype(BF16))
    A = jnp.concatenate([_dot(qt[i], kt[i], NT) for i in range(C // HGRN_SUB)], axis=0)
    A = jnp.where(causal, A, 0.0)
    glast = G[C - 1:C, :]
    eG = jnp.exp(G)
    edec = jnp.exp(glast - G)
    return dict(causal=causal, sf=sf, f=f, G=G, sq=sq, qs=qs, k=k, qt=qt, kt=kt, eq=eq, ek=ek, A=A,
                glast=glast, eG=eG, edec=edec, qhat=qs * eG, kdec=k * edec, v=hi)


def _hgrn_specs(nseq, ng, rows, reverse):
    W = HGRN_PAR
    nblk = W * HGRN_DIM // HGRN_COLS
    gi = (lambda g: ng - 1 - g) if reverse else (lambda g: g)

    def cols(c0):
        return [pl.BlockSpec((rows, HGRN_COLS),
                             lambda h, b, g, q=q: (b * ng + gi(g), c0 // HGRN_COLS + h * nblk + q)) for q in range(nblk)]

    return gi, (cols(C_HQ) + cols(C_HF) + cols(C_HI) + cols(C_HG)
                + [pl.BlockSpec((2, W * HGRN_DIM), lambda h, b, g: (0, h)),
                   pl.BlockSpec((W, 1, HGRN_DIM), lambda h, b, g: (h, 0, 0))])


def _hgrn_operands(refs):
    nblk = HGRN_PAR * HGRN_DIM // HGRN_COLS
    per = HGRN_COLS // HGRN_DIM

    def reader(group):
        def read(rs, w):
            lo = HGRN_DIM * (w % per)
            return group[w // per][rs, lo:lo + HGRN_DIM].astype(F32)
        return read

    readers = [reader(refs[nblk * a:nblk * (a + 1)]) for a in range(4)]
    return readers, refs[4 * nblk], refs[4 * nblk + 1], refs[4 * nblk + 2:]


def _hgrn_fwd(proj, lb_logits, gain3, nseq, S, shards):
    T = proj.shape[0]
    W = HGRN_PAR
    nc = S // HGRN_CHUNK
    cg = min(8, nc)
    ng = nc // cg
    rows = cg * HGRN_CHUNK
    n = len(shards)
    nh = HGRN_HEADS // W

    def body(*all_refs):
        (hq, hf, hi, hg), lbl_ref, gain_ref, refs = _hgrn_operands(all_refs)
        ins = refs[:n]
        o_ref, yh_ref, st_ref = refs[n:n + 3]
        outs = refs[n + 3:2 * n + 3]
        state, send_sems, recv_sems, local_sems = refs[2 * n + 3:]
        step = (pl.program_id(0), pl.program_id(1), pl.program_id(2))
        ops = _GatherOps(ins, outs, send_sems, recv_sems, local_sems)

        @pl.when((step[0] == 0) & (step[1] == 0) & (step[2] == 0))
        def _():
            for a in range(n):
                ops.local(a).start()
                ops.to_sibling(a).start()
                for q in range(3):
                    ops.to_chip(a, q).start()

        @pl.when(pl.program_id(2) == 0)
        def _():
            state[...] = jnp.zeros(state.shape, F32)

        lb_all = _sigmoid(lbl_ref[0:1, :] - lbl_ref[1:2, :])

        def chunk(c, carry):
            rs = pl.ds(pl.multiple_of(c * HGRN_CHUNK, HGRN_CHUNK), HGRN_CHUNK)
            for w in range(W):
                ls = slice(HGRN_DIM * w, HGRN_DIM * (w + 1))
                ch = _hgrn_chunk(hq(rs, w), hf(rs, w), hi(rs, w), lb_all[:, ls])
                st = state[w]
                st_ref[w, c] = st
                vb = ch["v"].astype(BF16)
                o = _dot(ch["qhat"].astype(BF16), st.astype(BF16), NT) + _dot(ch["A"].astype(BF16), vb)
                state[w] = st * jnp.exp(ch["glast"]) + _dot(vb, ch["kdec"].astype(BF16), TN)
                o_ref[rs, ls] = o
                r = lax.rsqrt(jnp.mean(o * o, axis=-1, keepdims=True) + NORM_EPS)
                gate = hg(rs, w)
                yh_ref[rs, ls] = (o * r * gain_ref[w] * (gate * _sigmoid(gate))).astype(BF16)
            return carry

        lax.fori_loop(0, cg, chunk, 0)

        @pl.when((step[0] == nh - 1) & (step[1] == nseq - 1) & (step[2] == ng - 1))
        def _():
            for a in range(n):
                ops.local(a).wait()
                ops.from_sibling(a).wait_recv()
                for q in range(3):
                    ops.from_chip(a, q).wait_recv()
                    ops.forward(a, q).start()
            for a in range(n):
                for q in range(3):
                    ops.forwarded(a, q).wait_recv()
                for cp in ops.sends(a):
                    cp.wait_send()

    _, in_specs = _hgrn_specs(nseq, ng, rows, False)
    out_row = lambda: pl.BlockSpec((rows, W * HGRN_DIM), lambda h, b, g: (b * ng + g, h))
    anyspec = lambda: pl.BlockSpec(memory_space=pl.ANY)
    return pl.pallas_call(
        body, name="hgrn_fwd",
        grid=(nh, nseq, ng),
        in_specs=in_specs + [anyspec() for _ in shards],
        out_specs=[out_row(), out_row(),
                   pl.BlockSpec((None, W, cg, HGRN_DIM, HGRN_DIM), lambda h, b, g: (b, h, g, 0, 0))]
                  + [anyspec() for _ in shards],
        out_shape=[jax.ShapeDtypeStruct((T, 1024), F32),
                   jax.ShapeDtypeStruct((T, 1024), BF16),
                   jax.ShapeDtypeStruct((nseq, HGRN_HEADS, nc, HGRN_DIM, HGRN_DIM), F32)]
                  + [jax.ShapeDtypeStruct((N_DEV * s.shape[0], s.shape[1]), s.dtype) for s in shards],
        scratch_shapes=[pltpu.VMEM((W, HGRN_DIM, HGRN_DIM), F32),
                        pltpu.SemaphoreType.DMA((n, _GATHER_SEMS)), pltpu.SemaphoreType.DMA((n, _GATHER_SEMS)),
                        pltpu.SemaphoreType.DMA((n,))],
        compiler_params=_params(("arbitrary", "arbitrary", "arbitrary")),
    )(*([proj] * (4 * W * HGRN_DIM // HGRN_COLS)), lb_logits, gain3, *shards)


def _hgrn_bwd(proj, lb_logits, gain3, o, d_yh, states, nseq, S):
    T = proj.shape[0]
    W = HGRN_PAR
    nc = S // HGRN_CHUNK
    cg = min(8, nc)
    ng = nc // cg
    rows = cg * HGRN_CHUNK
    C = HGRN_CHUNK
    nsub = C // HGRN_SUB

    def body(*all_refs):
        (hq_of, hf_of, hi_of, hg_of), lbl_ref, gain_ref, refs = _hgrn_operands(all_refs)
        (o_ref, dyh_ref, st_ref, dhq_ref, dhf_ref, dhi_ref, dhg_ref, dgain_ref, dlbl_ref, dstate, dlb_acc) = refs
        b, g = pl.program_id(1), pl.program_id(2)

        @pl.when(g == 0)
        def _():
            dstate[...] = jnp.zeros(dstate.shape, F32)

        @pl.when((b == 0) & (g == 0))
        def _():
            dgain_ref[...] = jnp.zeros(dgain_ref.shape, F32)
            dlb_acc[...] = jnp.zeros(dlb_acc.shape, F32)

        lb_all = _sigmoid(lbl_ref[0:1, :] - lbl_ref[1:2, :])

        def head_chunk(w, c):
            rs = pl.ds(pl.multiple_of(c * C, C), C)
            ls = slice(HGRN_DIM * w, HGRN_DIM * (w + 1))
            lb = lb_all[:, ls]
            gain = gain_ref[w]
            hq, hg = hq_of(rs, w), hg_of(rs, w)
            ch = _hgrn_chunk(hq, hf_of(rs, w), hi_of(rs, w), lb)
            ov = o_ref[rs, ls]
            dyh = dyh_ref[rs, ls].astype(F32)
            r = lax.rsqrt(jnp.mean(ov * ov, axis=-1, keepdims=True) + NORM_EPS)
            on = ov * r
            sg = _sigmoid(hg)
            doh = dyh * (hg * sg)
            dhg_ref[rs, ls] = (dyh * on * gain * (sg * (1.0 + hg * (1.0 - sg)))).astype(BF16)
            dgain_ref[w] += jnp.sum(doh * on, axis=0, keepdims=True)
            don = doh * gain
            do = r * (don - on * jnp.mean(don * on, axis=-1, keepdims=True))
            dob = do.astype(BF16)
            st = st_ref[w, c]
            dst = dstate[w]
            stb, dstb = st.astype(BF16), dst.astype(BF16)
            vb = ch["v"].astype(BF16)
            qhatb = ch["qhat"].astype(BF16)
            eglast = jnp.exp(ch["glast"])
            dqhat = _dot(dob, stb)
            dkdec = _dot(vb, dstb)
            dv = _dot(ch["kdec"].astype(BF16), dstb, NT)
            deg = jnp.sum(dst * st, axis=0, keepdims=True)
            dstate[w] = dst * eglast + _dot(dob, qhatb, TN)
            dA = jnp.where(ch["causal"], _dot(dob, vb, NT), 0.0)
            dv = dv + _dot(ch["A"].astype(BF16), dob, TN)
            dAb = dA.astype(BF16)
            dqs_parts, dgq_parts = [], []
            dk_intra, dgk = None, None
            for i in range(nsub):
                dA_i = dAb[HGRN_SUB * i:HGRN_SUB * (i + 1)]
                dqt = _dot(dA_i, ch["kt"][i])
                dkt = _dot(dA_i, ch["qt"][i], TN)
                dqs_parts.append(dqt * ch["eq"][i])
                dgq_parts.append(dqt * ch["qt"][i].astype(F32))
                dk_i = dkt * ch["ek"][i]
                dgk_i = dkt * ch["kt"][i].astype(F32)
                dk_intra = dk_i if dk_intra is None else dk_intra + dk_i
                dgk = dgk_i if dgk is None else dgk + dgk_i
            dqs_inter = dqhat * ch["eG"]
            dk_state = dkdec * ch["edec"]
            dqs = jnp.concatenate(dqs_parts, axis=0) + dqs_inter
            dk = dk_intra + dk_state
            dG = jnp.concatenate(dgq_parts, axis=0) - dgk + ch["qs"] * dqs_inter - ch["k"] * dk_state
            last_row = lax.broadcasted_iota(jnp.int32, (C, 1), 0) == C - 1
            tail = jnp.sum(dkdec * ch["kdec"], axis=0, keepdims=True) + deg * eglast
            dG = dG + jnp.where(last_row, tail, 0.0)
            anti = (lax.broadcasted_iota(jnp.int32, (C, C), 1)
                    >= lax.broadcasted_iota(jnp.int32, (C, C), 0)).astype(BF16)
            dlf = _tri_sum(anti, dG)
            df = dlf / ch["f"] - dk
            sf = ch["sf"]
            dhf_ref[rs, ls] = (df * (1.0 - lb) * sf * (1.0 - sf)).astype(BF16)
            dlb_acc[:, ls] += jnp.sum(df * (1.0 - sf), axis=0, keepdims=True)
            sq = ch["sq"]
            dhq_ref[rs, ls] = (dqs * (sq * (1.0 + hq * (1.0 - sq)))).astype(BF16)
            dhi_ref[rs, ls] = dv.astype(BF16)

        def chunk(cc, carry):
            for w in range(W):
                head_chunk(w, cg - 1 - cc)
            return carry

        lax.fori_loop(0, cg, chunk, 0)

        @pl.when((b == nseq - 1) & (g == ng - 1))
        def _():
            dl0 = dlb_acc[...] * lb_all * (1.0 - lb_all)
            dlbl_ref[0:1, :] = dl0
            dlbl_ref[1:2, :] = -dl0

    gi, in_specs = _hgrn_specs(nseq, ng, rows, True)
    row = lambda: pl.BlockSpec((rows, W * HGRN_DIM), lambda h, b, g: (b * ng + gi(g), h))
    return pl.pallas_call(
        body, name="hgrn_bwd",
        grid=(HGRN_HEADS // W, nseq, ng),
        in_specs=in_specs + [row(), row(),
                             pl.BlockSpec((None, W, cg, HGRN_DIM, HGRN_DIM), lambda h, b, g: (b, h, gi(g), 0, 0))],
        out_specs=[row(), row(), row(), row(),
                   pl.BlockSpec((W, 1, HGRN_DIM), lambda h, b, g: (h, 0, 0)),
                   pl.BlockSpec((2, W * HGRN_DIM), lambda h, b, g: (0, h))],
        out_shape=[jax.ShapeDtypeStruct((T, 1024), BF16)] * 4
                  + [jax.ShapeDtypeStruct((HGRN_HEADS, 1, HGRN_DIM), F32),
                     jax.ShapeDtypeStruct((2, HGRN_HEADS * HGRN_DIM), F32)],
        scratch_shapes=[pltpu.VMEM((W, HGRN_DIM, HGRN_DIM), F32), pltpu.VMEM((1, W * HGRN_DIM), F32)],
        compiler_params=_params(("arbitrary", "arbitrary", "arbitrary")),
    )(*([proj] * (4 * W * HGRN_DIM // HGRN_COLS)), lb_logits, gain3, o, d_yh, states)


def _gate_specs(tm):
    spec = lambda c0, q: pl.BlockSpec((tm, 512), lambda i: (i, c0 // 512 + q))
    return [spec(C_GA, q) for q in range(4)] + [spec(C_GH, q) for q in range(4)]


def _gates(refs):
    ga = jnp.concatenate([r[...] for r in refs[0:4]], axis=1).astype(F32)
    gh = jnp.concatenate([r[...] for r in refs[4:8]], axis=1).astype(F32)
    return ga, gh


def _branch_merge(proj, ya, yh, wa_t, wh_t):
    T = proj.shape[0]
    tm = min(512, T)

    def body(*refs):
        ya_ref, yh_ref, wa_ref, wh_ref, merged_ref, ua_ref, uh_ref = refs[8:]
        ga, gh = _gates(refs)
        ua = _dot(ya_ref[...], wa_ref[...], NT)
        uh = _dot(yh_ref[...], wh_ref[...], NT)
        merged_ref[...] = (_sigmoid(ga) * ua + _sigmoid(gh) * uh).astype(BF16)
        ua_ref[...] = ua.astype(BF16)
        uh_ref[...] = uh.astype(BF16)

    rowb = lambda w: pl.BlockSpec((tm, w), lambda i: (i, 0))
    full = lambda a: pl.BlockSpec(a.shape, lambda i: (0, 0))
    return pl.pallas_call(
        body, name="branch_merge",
        grid=(T // tm,),
        in_specs=_gate_specs(tm) + [rowb(1024), rowb(1024), full(wa_t), full(wh_t)],
        out_specs=[rowb(D_MODEL)] * 3,
        out_shape=[jax.ShapeDtypeStruct((T, D_MODEL), BF16)] * 3,
        compiler_params=_params(("arbitrary",)),
    )(*([proj] * 8), ya, yh, wa_t, wh_t)


def _out_norm_loss(merged, x2, tgt2, gpost, wout):
    T = merged.shape[0]
    tm = min(256, T)

    def body(m_ref, x_ref, t_ref, gpost_ref, wo_ref, dy_ref, dm_ref, dout_ref, loss_ref, dgpost_ref):
        @pl.when(pl.program_id(0) == 0)
        def _():
            loss_ref[...] = jnp.zeros(loss_ref.shape, F32)
            dgpost_ref[...] = jnp.zeros(dgpost_ref.shape, F32)

        y = _dot(m_ref[...], wo_ref[...])
        r2 = lax.rsqrt(jnp.mean(y * y, axis=-1, keepdims=True) + NORM_EPS)
        yn = y * r2
        gpost = gpost_ref[...]
        err = x_ref[...] + yn * gpost - t_ref[...]
        loss_ref[...] += jnp.sum(err * err, axis=0, keepdims=True)
        dout = err * (1.0 / D_MODEL)
        dout_ref[...] = dout
        dgpost_ref[...] += jnp.sum(dout * yn, axis=0, keepdims=True)
        dyn = dout * gpost
        dy = (r2 * (dyn - yn * jnp.mean(dyn * yn, axis=-1, keepdims=True))).astype(BF16)
        dy_ref[...] = dy
        dm_ref[...] = _dot(dy, wo_ref[...], NT).astype(BF16)

    rowb = lambda: pl.BlockSpec((tm, D_MODEL), lambda i: (i, 0))
    vec = lambda: pl.BlockSpec((1, D_MODEL), lambda i: (0, 0))
    return pl.pallas_call(
        body, name="out_norm_loss",
        grid=(T // tm,),
        in_specs=[rowb(), rowb(), rowb(), vec(), pl.BlockSpec(wout.shape, lambda i: (0, 0))],
        out_specs=[rowb(), rowb(), rowb(), vec(), vec()],
        out_shape=[jax.ShapeDtypeStruct((T, D_MODEL), BF16)] * 2
                  + [jax.ShapeDtypeStruct((T, D_MODEL), F32)] + [jax.ShapeDtypeStruct((1, D_MODEL), F32)] * 2,
        compiler_params=_params(("arbitrary",)),
    )(merged, x2, tgt2, gpost, wout)


def _branch_bwd(proj, dm, ua, uh, wa_t, wh_t):
    T = proj.shape[0]
    tm = min(256, T)

    def body(*refs):
        (dm_ref, ua_ref, uh_ref, wa_ref, wh_ref,
         dua_ref, duh_ref, dga_ref, dgh_ref, dya_ref, dyh_ref) = refs[8:]
        ga, gh = _gates(refs)
        sa, sh = _sigmoid(ga), _sigmoid(gh)
        dm = dm_ref[...].astype(F32)
        dua = (dm * sa).astype(BF16)
        duh = (dm * sh).astype(BF16)
        dua_ref[...] = dua
        duh_ref[...] = duh
        dga_ref[...] = (dm * ua_ref[...].astype(F32) * (sa * (1.0 - sa))).astype(BF16)
        dgh_ref[...] = (dm * uh_ref[...].astype(F32) * (sh * (1.0 - sh))).astype(BF16)
        dya_ref[...] = _dot(dua, wa_ref[...]).astype(BF16)
        dyh_ref[...] = _dot(duh, wh_ref[...]).astype(BF16)

    rowb = lambda w: pl.BlockSpec((tm, w), lambda i: (i, 0))
    full = lambda a: pl.BlockSpec(a.shape, lambda i: (0, 0))
    return pl.pallas_call(
        body, name="branch_bwd",
        grid=(T // tm,),
        in_specs=_gate_specs(tm) + [rowb(D_MODEL)] * 3 + [full(wa_t), full(wh_t)],
        out_specs=[rowb(D_MODEL)] * 4 + [rowb(1024)] * 2,
        out_shape=[jax.ShapeDtypeStruct((T, D_MODEL), BF16)] * 4 + [jax.ShapeDtypeStruct((T, 1024), BF16)] * 2,
        compiler_params=_params(("arbitrary",)),
    )(*([proj] * 8), dm, ua, uh, wa_t, wh_t)


def _tn_matmul(L, R, bm, bn, name):
    T, M = L.shape
    N = R.shape[1]

    def body(l_ref, r_ref, out_ref):
        out_ref[...] = _dot(l_ref[...], r_ref[...], TN).astype(BF16)

    return pl.pallas_call(
        body, name=name,
        grid=(N // bn, M // bm),
        in_specs=[pl.BlockSpec((T, bm), lambda j, i: (0, i)),
                  pl.BlockSpec((T, bn), lambda j, i: (0, j))],
        out_specs=pl.BlockSpec((bm, bn), lambda j, i: (i, j)),
        out_shape=jax.ShapeDtypeStruct((M, N), BF16),
        compiler_params=_params(("arbitrary", "arbitrary")),
    )(L, R)


def _dh_prenorm_bwd(dproj, wt_in, x2, dout, gpre, chip_sums):
    T = x2.shape[0]
    tm = min(1024, T)
    ne = 2
    te = tm // ne
    tk = 768
    nk = IN_WIDTH // tk
    nt = T // tm
    n = len(chip_sums)

    def body(dp_ref, w_ref, x_ref, dout_ref, g_ref, *refs):
        ins = refs[:n]
        gx_ref, dg_ref = refs[n], refs[n + 1]
        outs = refs[n + 2:2 * n + 2]
        acc, send_sems, recv_sems, local_sems = refs[2 * n + 2:]
        i, k = pl.program_id(0), pl.program_id(1)
        x, y, c = _place()
        my_chip = 2 * x + y

        def peer(q):
            return (x ^ (q >> 1), y ^ (q & 1))

        def send(a, q):
            px, py = peer(q)
            return pltpu.make_async_remote_copy(
                src_ref=ins[a].at[2 * px + py], dst_ref=outs[a].at[my_chip],
                send_sem=send_sems.at[a, q - 1], recv_sem=recv_sems.at[a, q - 1],
                device_id=(px, py, c), device_id_type=pl.DeviceIdType.MESH)

        def arrival(a, q):
            px, py = peer(q)
            return pltpu.make_async_remote_copy(
                src_ref=ins[a].at[my_chip], dst_ref=outs[a].at[2 * px + py],
                send_sem=send_sems.at[a, q - 1], recv_sem=recv_sems.at[a, q - 1],
                device_id=(px, py, c), device_id_type=pl.DeviceIdType.MESH)

        def local(a):
            return pltpu.make_async_copy(ins[a].at[my_chip], outs[a].at[my_chip], local_sems.at[a])

        @pl.when((i == 0) & (k == 0))
        def _():
            dg_ref[...] = jnp.zeros(dg_ref.shape, F32)
            for a in range(n):
                local(a).start()
                for q in range(1, 4):
                    send(a, q).start()

        @pl.when((i == nt - 1) & (k == nk + ne - 1))
        def _():
            for a in range(n):
                for q in range(1, 4):
                    arrival(a, q).wait_recv()
            for a in range(n):
                for q in range(1, 4):
                    send(a, q).wait_send()
                local(a).wait()

        @pl.when(k == 0)
        def _():
            acc[...] = _dot(dp_ref[...], w_ref[...])

        @pl.when((k > 0) & (k < nk))
        def _():
            acc[...] += _dot(dp_ref[...], w_ref[...])

        @pl.when(k >= nk)
        def _():
            dh = acc[pl.ds(pl.multiple_of((k - nk) * te, te), te), :]
            x = x_ref[...]
            r = lax.rsqrt(jnp.mean(x * x, axis=-1, keepdims=True) + NORM_EPS)
            xn = x * r
            dg_ref[...] += jnp.sum(dh * xn, axis=0, keepdims=True)
            dxn = dh * g_ref[...]
            gx_ref[...] = dout_ref[...] + r * (dxn - xn * jnp.mean(dxn * xn, axis=-1, keepdims=True))

    rowb = lambda: pl.BlockSpec((te, D_MODEL), lambda i, k: (ne * i + jnp.clip(k - nk, 0, ne - 1), 0))
    vec = lambda: pl.BlockSpec((1, D_MODEL), lambda i, k: (0, 0))
    anyspec = lambda: pl.BlockSpec(memory_space=pl.ANY)
    return pl.pallas_call(
        body, name="dh_prenorm_bwd",
        grid=(nt, nk + ne),
        in_specs=[pl.BlockSpec((tm, tk), lambda i, k: (i, jnp.minimum(k, nk - 1))),
                  pl.BlockSpec((tk, D_MODEL), lambda i, k: (jnp.minimum(k, nk - 1), 0)),
                  rowb(), rowb(), vec()] + [anyspec() for _ in chip_sums],
        out_specs=[rowb(), vec()] + [anyspec() for _ in chip_sums],
        out_shape=[jax.ShapeDtypeStruct((T, D_MODEL), F32), jax.ShapeDtypeStruct((1, D_MODEL), F32)]
                  + [jax.ShapeDtypeStruct(s.shape, s.dtype) for s in chip_sums],
        scratch_shapes=[pltpu.VMEM((tm, D_MODEL), F32),
                        pltpu.SemaphoreType.DMA((n, 3)), pltpu.SemaphoreType.DMA((n, 3)),
                        pltpu.SemaphoreType.DMA((n,))],
        compiler_params=_params(("arbitrary", "arbitrary")),
    )(dproj, wt_in, x2, dout, gpre, *chip_sums)


def _sum_slots(recv, br, name):
    nslot, R, C = recv.shape

    def body(r_ref, out_ref):
        acc = r_ref[0].astype(F32)
        for s in range(1, nslot):
            acc = acc + r_ref[s].astype(F32)
        out_ref[...] = acc

    return pl.pallas_call(
        body, name=name,
        grid=(R // br,),
        in_specs=[pl.BlockSpec((nslot, br, C), lambda i: (0, i, 0))],
        out_specs=pl.BlockSpec((br, C), lambda i: (i, 0)),
        out_shape=jax.ShapeDtypeStruct((R, C), F32),
        compiler_params=_params(("arbitrary",)),
    )(recv)


def _adamw_math(w, g, m, v):
    m = ADAM_B1 * m + (1.0 - ADAM_B1) * g
    v = ADAM_B2 * v + (1.0 - ADAM_B2) * (g * g)
    m_hat = m / (1.0 - ADAM_B1 ** ADAM_STEP)
    v_hat = v / (1.0 - ADAM_B2 ** ADAM_STEP)
    delta = -ADAM_LR * (m_hat / (jnp.sqrt(v_hat) + ADAM_EPS) + ADAM_WD * w)
    return delta, m, v


def _adamw(w, g, m, v, br, name):
    R, C = w.shape

    def body(w_ref, g_ref, m_ref, v_ref, d_ref, nm_ref, nv_ref):
        d_ref[...], nm_ref[...], nv_ref[...] = _adamw_math(w_ref[...], g_ref[...], m_ref[...], v_ref[...])

    spec = lambda: pl.BlockSpec((br, C), lambda i: (i, 0))
    return pl.pallas_call(
        body, name=name,
        grid=(R // br,),
        in_specs=[spec(), spec(), spec(), spec()],
        out_specs=[spec(), spec(), spec()],
        out_shape=[jax.ShapeDtypeStruct((R, C), F32)] * 3,
        compiler_params=_params(("arbitrary",)),
    )(w, g, m, v)


def _sibling_exchange(partials):
    n = len(partials)

    def body(*refs):
        ins, outs = refs[:n], refs[n:2 * n]
        send_sems, recv_sems = refs[2 * n:]
        x, y, c = _place()

        def copy(a, p):
            return pltpu.make_async_remote_copy(
                src_ref=ins[a].at[p, 1 - c], dst_ref=outs[a].at[p],
                send_sem=send_sems.at[a, p], recv_sem=recv_sems.at[a, p],
                device_id=(x, y, 1 - c), device_id_type=pl.DeviceIdType.MESH)

        copies = [copy(a, p) for p in range(4) for a in range(n)]
        for cp in copies:
            cp.start()
        for cp in copies:
            cp.wait()

    anyspec = lambda: pl.BlockSpec(memory_space=pl.ANY)
    return pl.pallas_call(
        body, name="sibling_exchange",
        in_specs=[anyspec() for _ in partials],
        out_specs=[anyspec() for _ in partials],
        out_shape=[jax.ShapeDtypeStruct((4,) + p.shape[2:], p.dtype) for p in partials],
        scratch_shapes=[pltpu.SemaphoreType.DMA((n, 4)), pltpu.SemaphoreType.DMA((n, 4))],
    )(*partials)


def _chip_sum(partial, from_sibling, br, name):
    _, _, R, C = partial.shape
    cls = lax.axis_index("c").astype(jnp.int32).reshape(1)

    def body(c_ref, mine_ref, sib_ref, out_ref):
        out_ref[...] = (mine_ref[...].astype(F32) + sib_ref[...].astype(F32)).astype(BF16)

    grid_spec = pltpu.PrefetchScalarGridSpec(
        num_scalar_prefetch=1,
        grid=(4, R // br),
        in_specs=[pl.BlockSpec((None, None, br, C), lambda p, i, c: (p, c[0], i, 0)),
                  pl.BlockSpec((None, br, C), lambda p, i, c: (p, i, 0))],
        out_specs=pl.BlockSpec((None, br, C), lambda p, i, c: (p, i, 0)),
    )
    return pl.pallas_call(
        body, name=name, grid_spec=grid_spec,
        out_shape=jax.ShapeDtypeStruct((4, R, C), BF16),
        compiler_params=_params(("arbitrary", "arbitrary")),
    )(cls, partial, from_sibling)


def _all_reduce_small(packed):
    shape = packed.shape

    def body(in_ref, out_ref, slots, send_sems, recv_sems):
        x, y, c = _place()
        my_slot = 4 * x + 2 * y + c

        def peer(k):
            return (x ^ ((k >> 2) & 1), y ^ ((k >> 1) & 1), c ^ (k & 1))

        def copy(k):
            p = peer(k)
            return pltpu.make_async_remote_copy(
                src_ref=in_ref, dst_ref=slots.at[my_slot],
                send_sem=send_sems.at[k - 1], recv_sem=recv_sems.at[k - 1],
                device_id=p, device_id_type=pl.DeviceIdType.MESH)

        def arrival(k):
            p = peer(k)
            return pltpu.make_async_remote_copy(
                src_ref=in_ref, dst_ref=slots.at[4 * p[0] + 2 * p[1] + p[2]],
                send_sem=send_sems.at[k - 1], recv_sem=recv_sems.at[k - 1],
                device_id=p, device_id_type=pl.DeviceIdType.MESH)

        sends = [copy(k) for k in range(1, N_DEV)]
        for cp in sends:
            cp.start()
        slots[my_slot] = in_ref[...]
        for k in range(1, N_DEV):
            arrival(k).wait_recv()
        for cp in sends:
            cp.wait_send()
        acc = slots[0]
        for s in range(1, N_DEV):
            acc = acc + slots[s]
        out_ref[...] = acc

    return pl.pallas_call(
        body, name="all_reduce_small",
        in_specs=[pl.BlockSpec(memory_space=pltpu.VMEM)],
        out_specs=pl.BlockSpec(memory_space=pltpu.VMEM),
        out_shape=jax.ShapeDtypeStruct(shape, F32),
        scratch_shapes=[pltpu.VMEM((N_DEV,) + shape, F32),
                        pltpu.SemaphoreType.DMA((7,)), pltpu.SemaphoreType.DMA((7,))],
    )(packed)


def _pack_small(norm_pre, norm_post, lb_logits, hgrn_norm, rel_bias, sinks, extra=None):
    tail = [hgrn_norm.reshape(1, 1024), rel_bias.reshape(1, 512), sinks.reshape(1, 16)]
    used = 1024 + 512 + 16
    if extra is not None:
        tail.append(extra.reshape(1, 1))
        used += 1
    tail.append(jnp.zeros((1, D_MODEL - used), F32))
    rows = [norm_pre.reshape(1, D_MODEL), norm_post.reshape(1, D_MODEL), lb_logits.reshape(1, D_MODEL),
            jnp.concatenate(tail, axis=1), jnp.zeros((4, D_MODEL), F32)]
    return jnp.concatenate(rows, axis=0)


def _unpack_small(p):
    return (p[0:1], p[3, 1024:1536].reshape(REL_BUCKETS, ATTN_HEADS), p[3:4, 1536:1552],
            p[2].reshape(2, 1024), p[3, 0:1024].reshape(1, HGRN_HEADS, HGRN_DIM), p[1:2])


def _local_step(nseq, S, x2, tgt2, proj, h, rel_bias, attn_sinks, lb_logits, hgrn_norm, norm_post, shards):
    nb = S // ATTN_BLOCK
    bucket = jnp.asarray(_t5_bucket_table())
    gain3 = hgrn_norm.reshape(HGRN_HEADS, 1, HGRN_DIM)

    bias = _bias_table(rel_bias, bucket)
    ya = _attn_fwd(proj, bias, attn_sinks, nseq, nb)
    o, yh, states, wout, wa_t, wh_t = _hgrn_fwd(proj, lb_logits, gain3, nseq, S, shards)
    merged, ua, uh = _branch_merge(proj, ya, yh, wa_t, wh_t)
    dy, dm, dout, loss_cols, d_gpost = _out_norm_loss(merged, x2, tgt2, norm_post, wout)
    dua, duh, dga, dgh, dya, dyh = _branch_bwd(proj, dm, ua, uh, wa_t, wh_t)

    dq, dkv, dg, dbias, d_sinks = _attn_bwd(proj, bias, attn_sinks, dya, nseq, nb)
    d_rel_bias = _bias_table_bwd(dbias, bucket)
    dhq, dhf, dhi, dhg, d_gain, d_lbl = _hgrn_bwd(proj, lb_logits, gain3, o, dyh, states, nseq, S)
    dproj = jnp.concatenate([dq, dkv, dg, dhq, dhf, dhi, dhg, dga, dgh], axis=1)

    p_in = _tn_matmul(dproj, h, 768, 1024, "dw_in")
    p_out = _tn_matmul(merged, dy, 256, 1024, "dw_out")
    p_a = _tn_matmul(dua, ya, 256, 1024, "dw_branch_attn")
    p_h = _tn_matmul(duh, yh, 256, 1024, "dw_branch_hgrn")
    return dproj, dout, p_in, p_out, p_a, p_h, d_gpost, d_lbl, d_gain, d_rel_bias, d_sinks, loss_cols


def kernel(x, norm_pre, w_in, rel_bias, attn_sinks, lb_logits, hgrn_norm, w_branch_attn, w_branch_hgrn, w_out, norm_post, loss_target, m_norm_pre, m_w_in, m_rel_bias, m_attn_sinks, m_lb_logits, m_hgrn_norm, m_w_branch_attn, m_w_branch_hgrn, m_w_out, m_norm_post, v_norm_pre, v_w_in, v_rel_bias, v_attn_sinks, v_lb_logits, v_hgrn_norm, v_w_branch_attn, v_w_branch_hgrn, v_w_out, v_norm_post):
    nseq, S, _ = x.shape
    T = nseq * S
    x2 = x.reshape(T, D_MODEL)
    tgt2 = loss_target.reshape(T, D_MODEL)

    h = _prenorm(x2, norm_pre)
    proj, wt_in = _gather_inproj(h, w_in[0].T.astype(BF16))
    shards = [w_out[0].astype(BF16), w_branch_attn[0].T.astype(BF16), w_branch_hgrn[0].T.astype(BF16)]

    (dproj, dout, p_in, p_out, p_a, p_h, d_gpost, d_lbl, d_gain, d_rel_bias, d_sinks, loss_cols) = _local_step(
        nseq, S, x2, tgt2, proj, h, rel_bias, attn_sinks, lb_logits, hgrn_norm, norm_post, shards)

    partials = [p.reshape(4, 2, p.shape[0] // N_DEV, p.shape[1]) for p in (p_in, p_out, p_a, p_h)]
    from_sib = _sibling_exchange(partials)
    chip_sums = [_chip_sum(p, f, br, "chip_sum_" + nm) for p, f, br, nm in zip(
        partials, from_sib, (192, 128, 128, 128), ("dw_in", "dw_out", "dw_branch_attn", "dw_branch_hgrn"))]
    grad_x2, d_gpre, r_in, r_out, r_a, r_h = _dh_prenorm_bwd(dproj, wt_in, x2, dout, norm_pre, chip_sums)
    g_in = _sum_slots(r_in, 192, "sum_dw_in").T
    g_out = _sum_slots(r_out, 128, "sum_dw_out")
    g_a = _sum_slots(r_a, 128, "sum_dw_branch_attn").T
    g_h = _sum_slots(r_h, 128, "sum_dw_branch_hgrn").T

    loss_part = 0.5 / D_MODEL * jnp.sum(loss_cols)
    packed = _pack_small(d_gpre, d_gpost, d_lbl, d_gain, d_rel_bias, d_sinks, extra=loss_part)
    total = _all_reduce_small(packed)
    loss = total[3, 1024 + 512 + 16]
    sm_w = _pack_small(norm_pre, norm_post, lb_logits, hgrn_norm, rel_bias, attn_sinks)
    sm_m = _pack_small(m_norm_pre, m_norm_post, m_lb_logits, m_hgrn_norm, m_rel_bias, m_attn_sinks)
    sm_v = _pack_small(v_norm_pre, v_norm_post, v_lb_logits, v_hgrn_norm, v_rel_bias, v_attn_sinks)
    sm_d, sm_nm, sm_nv = _adamw(sm_w, total, sm_m, sm_v, 8, "adamw_small")

    d_in, nm_in, nv_in = _adamw(w_in[0], g_in, m_w_in[0], v_w_in[0], 256, "adamw_w_in")
    d_out, nm_out, nv_out = _adamw(w_out[0], g_out, m_w_out[0], v_w_out[0], 128, "adamw_w_out")
    d_a, nm_a, nv_a = _adamw(w_branch_attn[0], g_a, m_w_branch_attn[0], v_w_branch_attn[0], 256, "adamw_w_branch_attn")
    d_h, nm_h, nv_h = _adamw(w_branch_hgrn[0], g_h, m_w_branch_hgrn[0], v_w_branch_hgrn[0], 256, "adamw_w_branch_hgrn")

    def group(small, big_in, big_a, big_h, big_out):
        npre, rb, sk, lbl, hn, npost = _unpack_small(small)
        return (npre, big_in[None], rb, sk, lbl, hn, big_a[None], big_h[None], big_out[None], npost)

    return (loss, grad_x2.reshape(nseq, S, D_MODEL),
            *group(total, g_in, g_a, g_h, g_out),
            *group(sm_d, d_in, d_a, d_h, d_out),
            *group(sm_nm, nm_in, nm_a, nm_h, nm_out),
            *group(sm_nv, nv_in, nv_a, nv_h, nv_out))
```

```python
import functools
import math

import numpy as np
import jax
import jax.numpy as jnp
from jax import lax
from jax.experimental import pallas as pl
from jax.experimental.pallas import tpu as pltpu

F32 = jnp.float32
BF16 = jnp.bfloat16

D_MODEL = 2048
ATTN_HEADS = 16
ATTN_HEAD_DIM = 64
ATTN_GROUP = 4
ATTN_BLOCK = 128
HGRN_HEADS = 8
HGRN_DIM = 128
HGRN_CHUNK = 64
HGRN_SUB = 16
HGRN_PAR = 8
HGRN_COLS = 512
REL_BUCKETS = 32
REL_MAX_DIST = 128
NORM_EPS = 1e-6
C_AQ, C_AK, C_AV, C_AG = 0, 1024, 1280, 1536
C_HQ, C_HF, C_HI, C_HG = 2560, 3584, 4608, 5632
C_GA, C_GH = 6656, 8704
IN_WIDTH = 10752
N_DEV = 8
assert all(c0 % HGRN_COLS == 0 for c0 in (C_HQ, C_HF, C_HI, C_HG)) and (HGRN_PAR * HGRN_DIM) % HGRN_COLS == 0

ADAM_LR = 0.001
ADAM_B1 = 0.9
ADAM_B2 = 0.999
ADAM_EPS = 1e-08
ADAM_WD = 0.01
ADAM_STEP = 10

VMEM_LIMIT_V7X = 56 * 1024 * 1024
NEG_BIG = -1e30

NT = (((1,), (1,)), ((), ()))
TN = (((0,), (0,)), ((), ()))
NN = (((1,), (0,)), ((), ()))


def _dot(a, b, dims=NN):
    return lax.dot_general(a, b, dims, preferred_element_type=F32)


def _params(sem=None):
    return pltpu.CompilerParams(dimension_semantics=sem, vmem_limit_bytes=VMEM_LIMIT_V7X)


def _sigmoid(x):
    return 1.0 / (1.0 + jnp.exp(-x))


def _t5_bucket_table():
    qi = np.arange(ATTN_BLOCK)[:, None]
    si = np.arange(2 * ATTN_BLOCK)[None, :]
    dist = qi + ATTN_BLOCK - si
    max_exact = REL_BUCKETS // 2
    d = np.maximum(dist, 0)
    df = np.maximum(d, 1).astype(np.float32)
    large = max_exact + (np.log(df / np.float32(max_exact)).astype(np.float32)
                         / np.float32(math.log(REL_MAX_DIST / max_exact))
                         * np.float32(REL_BUCKETS - max_exact)).astype(np.int32)
    large = np.minimum(large, REL_BUCKETS - 1)
    return np.where(d < max_exact, d, large).astype(np.int32)


def _place():
    return lax.axis_index("x"), lax.axis_index("y"), lax.axis_index("c")


class _GatherOps:
    def __init__(self, ins, outs, send_sems, recv_sems, local_sems):
        self.ins, self.outs = ins, outs
        self.send_sems, self.recv_sems, self.local_sems = send_sems, recv_sems, local_sems
        x, y, c = _place()
        self.c = c
        self.me, self.sibling = (x, y, c), (x, y, 1 - c)
        self.chips = [(1 - x, y), (x, 1 - y), (1 - x, 1 - y)]

    def _rows(self, a, dev):
        r = self.ins[a].shape[0]
        return self.outs[a].at[pl.ds((4 * dev[0] + 2 * dev[1] + dev[2]) * r, r), :]

    def _copy(self, a, k, block, to, src=None):
        return pltpu.make_async_remote_copy(
            src_ref=self._rows(a, block) if src is None else src, dst_ref=self._rows(a, block),
            send_sem=self.send_sems.at[a, k], recv_sem=self.recv_sems.at[a, k],
            device_id=to, device_id_type=pl.DeviceIdType.MESH)

    def local(self, a):
        return pltpu.make_async_copy(self.ins[a], self._rows(a, self.me), self.local_sems.at[a])

    def to_sibling(self, a):
        return self._copy(a, 0, self.me, self.sibling, src=self.ins[a])

    def to_chip(self, a, q):
        return self._copy(a, 1 + q, self.me, (*self.chips[q], self.c), src=self.ins[a])

    def forward(self, a, q):
        return self._copy(a, 4 + q, (*self.chips[q], self.c), self.sibling)

    def from_sibling(self, a):
        return self._copy(a, 0, self.sibling, self.me)

    def from_chip(self, a, q):
        return self._copy(a, 1 + q, (*self.chips[q], self.c), self.me)

    def forwarded(self, a, q):
        return self._copy(a, 4 + q, (*self.chips[q], 1 - self.c), self.me)

    def sends(self, a):
        return [self.to_sibling(a)] + [self.to_chip(a, q) for q in range(3)] + [self.forward(a, q) for q in range(3)]


class _ChipExchange:
    def __init__(self, ins, outs, send_sems, recv_sems, local_sems):
        self.ins, self.outs = ins, outs
        self.send_sems, self.recv_sems, self.local_sems = send_sems, recv_sems, local_sems
        self.x, self.y, self.c = _place()
        self.my_chip = 2 * self.x + self.y

    def _peer(self, q):
        return (self.x ^ (q >> 1), self.y ^ (q & 1))

    def _copy(self, a, q, src_block, dst_slot):
        px, py = self._peer(q)
        return pltpu.make_async_remote_copy(
            src_ref=self.ins[a].at[src_block], dst_ref=self.outs[a].at[dst_slot],
            send_sem=self.send_sems.at[a, q - 1], recv_sem=self.recv_sems.at[a, q - 1],
            device_id=(px, py, self.c), device_id_type=pl.DeviceIdType.MESH)

    def _send(self, a, q):
        px, py = self._peer(q)
        return self._copy(a, q, 2 * px + py, self.my_chip)

    def _arrival(self, a, q):
        px, py = self._peer(q)
        return self._copy(a, q, self.my_chip, 2 * px + py)

    def _local(self, a):
        return pltpu.make_async_copy(self.ins[a].at[self.my_chip], self.outs[a].at[self.my_chip],
                                     self.local_sems.at[a])

    def start(self):
        for a in range(len(self.ins)):
            self._local(a).start()
            for q in range(1, 4):
                self._send(a, q).start()

    def wait(self):
        for a in range(len(self.ins)):
            for q in range(1, 4):
                self._arrival(a, q).wait_recv()
        for a in range(len(self.ins)):
            for q in range(1, 4):
                self._send(a, q).wait_send()
            self._local(a).wait()

    @staticmethod
    def scratch(n):
        return [pltpu.SemaphoreType.DMA((n, 3)), pltpu.SemaphoreType.DMA((n, 3)), pltpu.SemaphoreType.DMA((n,))]


_GATHER_SEMS = 7
LOCAL_DMA_THREAD = 1

INPROJ_TILE = 896


def _prenorm(x2, gpre):
    T = x2.shape[0]
    tm = min(512, T)

    def body(x_ref, g_ref, h_ref):
        x = x_ref[...]
        r = lax.rsqrt(jnp.mean(x * x, axis=-1, keepdims=True) + NORM_EPS)
        h_ref[...] = (x * r * g_ref[...]).astype(BF16)

    return pl.pallas_call(
        body, name="prenorm",
        grid=(T // tm,),
        in_specs=[pl.BlockSpec((tm, D_MODEL), lambda i: (i, 0)), pl.BlockSpec((1, D_MODEL), lambda i: (0, 0))],
        out_specs=pl.BlockSpec((tm, D_MODEL), lambda i: (i, 0)),
        out_shape=jax.ShapeDtypeStruct((T, D_MODEL), BF16),
        compiler_params=_params(("arbitrary",)),
    )(x2, gpre)


def _gather_inproj(h, wt_shard):
    T = h.shape[0]
    tm = min(1024, T)
    nm = T // tm
    tn = INPROJ_TILE
    ntile = IN_WIDTH // tn
    nstep = ntile * nm

    def body(h_hbm, w_in, proj_hbm, w_out, hbuf, wtile, obuf, send_sems, recv_sems, local_sems, h_sem, w_sems, o_sems):
        j, i = pl.program_id(0), pl.program_id(1)
        step = j * nm + i
        slot = step % 2
        ops = _GatherOps([w_in], [w_out], send_sems, recv_sems, local_sems)
        x, y, _ = _place()

        def tile_of(jj):
            k = jj // 3
            return 3 * ((2 * x + y) ^ (((k & 1) << 1) | (k >> 1))) + jj % 3

        tile = tile_of(j)

        def h_load():
            return pltpu.make_async_copy(h_hbm, hbuf, h_sem)

        def store(s, rows, cols):
            return pltpu.make_async_copy(obuf.at[s], proj_hbm.at[rows, cols], o_sems.at[s])

        def window(ii, t):
            return pl.ds(pl.multiple_of(ii * tm, tm), tm), pl.ds(pl.multiple_of(t * tn, tn), tn)

        @pl.when(step == 0)
        def _():
            h_load().start()
            ops.local(0).start()
            ops.to_sibling(0).start()
            ops.to_chip(0, 0).start()
            ops.to_chip(0, 1).start()
            h_load().wait()

        for kk in range(4):
            @pl.when((j == 3 * kk) & (i == 0))
            def _(kk=kk):
                if kk == 0:
                    ops.local(0).wait()
                    ops.from_sibling(0).wait_recv()
                else:
                    q = kk - 1
                    ops.from_chip(0, q).wait_recv()
                    ops.forward(0, q).start()
                    if q == 0:
                        ops.to_chip(0, 2).start()
                    ops.forwarded(0, q).wait_recv()

        wslot = j % 2

        def fetch(jj, sw):
            rows = pl.ds(pl.multiple_of(tile_of(jj) * tn, tn), tn)
            return pltpu.make_async_copy(w_out.at[rows, :], wtile.at[sw], w_sems.at[sw])

        @pl.when((i == 0) & (j % 3 == 0))
        def _():
            fetch(j, wslot).start(LOCAL_DMA_THREAD)

        @pl.when(i == 0)
        def _():
            fetch(j, wslot).wait()

        @pl.when((i == 0) & (j % 3 != 2))
        def _():
            fetch(j + 1, 1 - wslot).start(LOCAL_DMA_THREAD)

        @pl.when(step >= 2)
        def _():
            store(slot, *window(0, 0)).wait()

        hv = hbuf[pl.ds(pl.multiple_of(i * tm, tm), tm), :]
        obuf[slot] = _dot(hv, wtile[wslot], NT).astype(BF16)
        store(slot, *window(i, tile)).start(LOCAL_DMA_THREAD)

        @pl.when(step == nstep - 1)
        def _():
            for s in range(min(2, nstep)):
                store(s, *window(0, 0)).wait()
            for cp in ops.sends(0):
                cp.wait_send()

    anyspec = lambda: pl.BlockSpec(memory_space=pl.ANY)
    return pl.pallas_call(
        body, name="gather_inproj",
        grid=(ntile, nm),
        in_specs=[anyspec(), anyspec()],
        out_specs=[anyspec(), anyspec()],
        out_shape=[jax.ShapeDtypeStruct((T, IN_WIDTH), BF16),
                   jax.ShapeDtypeStruct((N_DEV * wt_shard.shape[0], D_MODEL), BF16)],
        scratch_shapes=[pltpu.VMEM((T, D_MODEL), BF16), pltpu.VMEM((2, tn, D_MODEL), BF16),
                        pltpu.VMEM((2, tm, tn), BF16),
                        pltpu.SemaphoreType.DMA((1, _GATHER_SEMS)), pltpu.SemaphoreType.DMA((1, _GATHER_SEMS)),
                        pltpu.SemaphoreType.DMA((1,)), pltpu.SemaphoreType.DMA, pltpu.SemaphoreType.DMA((2,)),
                        pltpu.SemaphoreType.DMA((2,))],
        compiler_params=_params(("arbitrary", "arbitrary")),
    )(h, wt_shard)


def _bias_table(rel_bias, bucket):
    def body(rb_ref, bk_ref, out_ref):
        h = pl.program_id(0)
        bk = bk_ref[...]
        acc = jnp.zeros(bk.shape, F32)
        for b in range(REL_BUCKETS):
            acc = jnp.where(bk == b, rb_ref[b, h], acc)
        out_ref[...] = acc

    return pl.pallas_call(
        body, name="bias_table",
        grid=(ATTN_HEADS,),
        in_specs=[pl.BlockSpec(memory_space=pltpu.SMEM),
                  pl.BlockSpec((ATTN_BLOCK, 2 * ATTN_BLOCK), lambda h: (0, 0))],
        out_specs=pl.BlockSpec((None, ATTN_BLOCK, 2 * ATTN_BLOCK), lambda h: (h, 0, 0)),
        out_shape=jax.ShapeDtypeStruct((ATTN_HEADS, ATTN_BLOCK, 2 * ATTN_BLOCK), F32),
        compiler_params=_params(("arbitrary",)),
    )(rel_bias, bucket)


def _bias_table_bwd(dbias, bucket):
    def body(db_ref, bk_ref, out_ref):
        h = pl.program_id(0)
        bk = bk_ref[...]
        db = db_ref[...]
        for b in range(REL_BUCKETS):
            out_ref[b, h] = jnp.sum(jnp.where(bk == b, db, 0.0))

    return pl.pallas_call(
        body, name="bias_table_bwd",
        grid=(ATTN_HEADS,),
        in_specs=[pl.BlockSpec((None, ATTN_BLOCK, 2 * ATTN_BLOCK), lambda h: (h, 0, 0)),
                  pl.BlockSpec((ATTN_BLOCK, 2 * ATTN_BLOCK), lambda h: (0, 0))],
        out_specs=pl.BlockSpec(memory_space=pltpu.SMEM),
        out_shape=jax.ShapeDtypeStruct((REL_BUCKETS, ATTN_HEADS), F32),
        compiler_params=_params(("arbitrary",)),
    )(dbias, bucket)


def _attn_common(qkvg, kv_prev, blk):
    lane = lax.broadcasted_iota(jnp.int32, (1, 128), 1)
    half = (lane < ATTN_HEAD_DIM, lane >= ATTN_HEAD_DIM)
    kv_cur = qkvg[:, C_AK:C_AG]
    win = jnp.concatenate([kv_prev, kv_cur], axis=0)
    k_slab, v_slab = [], []
    for r in range(2):
        ks = win[:, 128 * r:128 * r + 128]
        vs = win[:, 256 + 128 * r:256 + 128 * r + 128]
        k_slab.append((ks, pltpu.roll(ks, ATTN_HEAD_DIM, 1)))
        v_slab.append((vs, pltpu.roll(vs, ATTN_HEAD_DIM, 1)))
    rows4 = ATTN_GROUP * ATTN_BLOCK
    qi = lax.broadcasted_iota(jnp.int32, (rows4, 2 * ATTN_BLOCK), 0) & (ATTN_BLOCK - 1)
    si = lax.broadcasted_iota(jnp.int32, (rows4, 2 * ATTN_BLOCK), 1)
    valid = (si > qi) & (si <= qi + ATTN_BLOCK) & ((si >= ATTN_BLOCK) | (blk > 0))
    return half, k_slab, v_slab, valid


def _stack_heads(half, slab0, slab1):
    return jnp.concatenate([jnp.where(half[0], slab0, 0.0), jnp.where(half[1], slab0, 0.0),
                            jnp.where(half[0], slab1, 0.0), jnp.where(half[1], slab1, 0.0)], axis=0)


def _unstack_heads(half, x4):
    B = ATTN_BLOCK
    return (jnp.where(half[0], x4[0:B], x4[B:2 * B]), jnp.where(half[0], x4[2 * B:3 * B], x4[3 * B:4 * B]))


def _attn_group(j, qkvg, half, k_slab, v_slab, valid, bias_ref, sinks_ref):
    r, aj = j // 2, j % 2
    pick = (lambda a, b: jnp.where(half[0], a, b)) if aj == 0 else (lambda a, b: jnp.where(half[0], b, a))
    kb = pick(*k_slab[r]).astype(BF16)
    vb = pick(*v_slab[r]).astype(BF16)
    q4 = _stack_heads(half, qkvg[:, 256 * j:256 * j + 128], qkvg[:, 256 * j + 128:256 * j + 256]).astype(BF16)
    bias4 = bias_ref[ATTN_GROUP * j:ATTN_GROUP * (j + 1)].reshape(valid.shape)
    s = _dot(q4, kb, NT) * (ATTN_HEAD_DIM ** -0.5) + bias4
    s = jnp.where(valid, s, NEG_BIG)
    rowblk = lax.broadcasted_iota(jnp.int32, (valid.shape[0], 1), 0) // ATTN_BLOCK
    sink = jnp.full((valid.shape[0], 1), sinks_ref[0, ATTN_GROUP * j], F32)
    for b in range(1, ATTN_GROUP):
        sink = jnp.where(rowblk == b, sinks_ref[0, ATTN_GROUP * j + b], sink)
    m = jnp.maximum(jnp.max(s, axis=-1, keepdims=True), sink)
    e = jnp.exp(s - m)
    es = jnp.exp(sink - m)
    inv = 1.0 / (jnp.sum(e, axis=-1, keepdims=True) + es)
    pn = e * inv
    o4 = _dot(pn.astype(BF16), vb)
    return dict(r=r, aj=aj, kb=kb, vb=vb, q4=q4, pn=pn, psink=es * inv, o4=o4)


def _attn_specs(nb):
    row = lambda b, i: b * nb + i
    return [
        pl.BlockSpec((ATTN_BLOCK, C_HQ), lambda b, i: (row(b, i), 0)),
        pl.BlockSpec((ATTN_BLOCK, 512), lambda b, i: (row(b, jnp.maximum(i - 1, 0)), 2)),
        pl.BlockSpec((ATTN_HEADS, ATTN_BLOCK, 2 * ATTN_BLOCK), lambda b, i: (0, 0, 0)),
        pl.BlockSpec(memory_space=pltpu.SMEM),
    ]


def _attn_fwd(proj, bias, sinks, nseq, nb):
    T = proj.shape[0]

    def body(qkvg_ref, kvp_ref, bias_ref, sinks_ref, ya_ref):
        qkvg = qkvg_ref[...].astype(F32)
        half, k_slab, v_slab, valid = _attn_common(qkvg, kvp_ref[...].astype(F32), pl.program_id(1))
        slabs = []
        for j in range(ATTN_HEADS // ATTN_GROUP):
            grp = _attn_group(j, qkvg, half, k_slab, v_slab, valid, bias_ref, sinks_ref)
            slabs += _unstack_heads(half, grp["o4"])
        o_all = jnp.concatenate(slabs, axis=1)
        g = qkvg[:, C_AG:C_HQ]
        ya_ref[...] = (o_all * (g * _sigmoid(g))).astype(BF16)

    return pl.pallas_call(
        body, name="attn_fwd",
        grid=(nseq, nb),
        in_specs=_attn_specs(nb),
        out_specs=pl.BlockSpec((ATTN_BLOCK, 1024), lambda b, i: (b * nb + i, 0)),
        out_shape=jax.ShapeDtypeStruct((T, 1024), BF16),
        compiler_params=_params(("arbitrary", "arbitrary")),
    )(proj, proj, bias, sinks)


def _attn_bwd(proj, bias, sinks, d_ya, nseq, nb):
    T = proj.shape[0]
    S = nb * ATTN_BLOCK
    scale = ATTN_HEAD_DIM ** -0.5

    def body(qkvg_ref, kvp_ref, bias_ref, sinks_ref, dya_ref, dq_ref, dkv_ref, dg_ref, dbias_ref, dsinks_ref):
        b, i = pl.program_id(0), pl.program_id(1)
        first = (b == 0) & (i == 0)

        @pl.when(first)
        def _():
            dbias_ref[...] = jnp.zeros(dbias_ref.shape, F32)
            for h in range(ATTN_HEADS):
                dsinks_ref[0, h] = 0.0

        qkvg = qkvg_ref[...].astype(F32)
        half, k_slab, v_slab, valid = _attn_common(qkvg, kvp_ref[...].astype(F32), i)
        g = qkvg[:, C_AG:C_HQ]
        sg = _sigmoid(g)
        silu_g = g * sg
        dya = dya_ref[...].astype(F32)
        do_all = dya * silu_g
        dq_slabs, o_slabs = [], []
        dk_slab, dv_slab = [None, None], [None, None]
        B = ATTN_BLOCK

        def fold(x, aj):
            return jnp.where(half[aj], x + pltpu.roll(x, ATTN_HEAD_DIM, 1), 0.0)

        for j in range(ATTN_HEADS // ATTN_GROUP):
            grp = _attn_group(j, qkvg, half, k_slab, v_slab, valid, bias_ref, sinks_ref)
            r, aj, pn = grp["r"], grp["aj"], grp["pn"]
            do4 = _stack_heads(half, do_all[:, 256 * j:256 * j + 128], do_all[:, 256 * j + 128:256 * j + 256])
            do4b = do4.astype(BF16)
            dp = _dot(do4b, grp["vb"], NT)
            delta = jnp.sum(do4 * grp["o4"], axis=-1, keepdims=True)
            ds = pn * (dp - delta)
            sink_term = grp["psink"] * delta
            for b4 in range(ATTN_GROUP):
                dsinks_ref[0, ATTN_GROUP * j + b4] += -jnp.sum(sink_term[B * b4:B * (b4 + 1)])
            dbias_ref[ATTN_GROUP * j:ATTN_GROUP * (j + 1)] += ds.reshape(ATTN_GROUP, B, 2 * B)
            dsb = ds.astype(BF16)
            dq_slabs += _unstack_heads(half, _dot(dsb, grp["kb"]) * scale)
            o_slabs += _unstack_heads(half, grp["o4"])
            dk_j = fold(_dot(dsb, grp["q4"], TN) * scale, aj)
            dv_j = fold(_dot(pn.astype(BF16), do4b, TN), aj)
            dk_slab[r] = dk_j if dk_slab[r] is None else dk_slab[r] + dk_j
            dv_slab[r] = dv_j if dv_slab[r] is None else dv_slab[r] + dv_j

        dq_ref[...] = jnp.concatenate(dq_slabs, axis=1).astype(BF16)
        o_all = jnp.concatenate(o_slabs, axis=1)
        dg_ref[...] = (dya * o_all * (sg * (1.0 + g * (1.0 - sg)))).astype(BF16)

        dkv = jnp.concatenate(dk_slab + dv_slab, axis=1)
        cur = pl.multiple_of(i * ATTN_BLOCK, ATTN_BLOCK)
        dkv_ref[pl.ds(cur, ATTN_BLOCK), :] = dkv[ATTN_BLOCK:].astype(BF16)

        @pl.when(i > 0)
        def _():
            prev = pl.multiple_of((i - 1) * ATTN_BLOCK, ATTN_BLOCK)
            old = dkv_ref[pl.ds(prev, ATTN_BLOCK), :].astype(F32)
            dkv_ref[pl.ds(prev, ATTN_BLOCK), :] = (old + dkv[:ATTN_BLOCK]).astype(BF16)

    row_spec = lambda w: pl.BlockSpec((ATTN_BLOCK, w), lambda b, i: (b * nb + i, 0))
    return pl.pallas_call(
        body, name="attn_bwd",
        grid=(nseq, nb),
        in_specs=_attn_specs(nb) + [row_spec(1024)],
        out_specs=[row_spec(1024),
                   pl.BlockSpec((S, 512), lambda b, i: (b, 0)),
                   row_spec(1024),
                   pl.BlockSpec((ATTN_HEADS, ATTN_BLOCK, 2 * ATTN_BLOCK), lambda b, i: (0, 0, 0)),
                   pl.BlockSpec(memory_space=pltpu.SMEM)],
        out_shape=[jax.ShapeDtypeStruct((T, 1024), BF16),
                   jax.ShapeDtypeStruct((T, 512), BF16),
                   jax.ShapeDtypeStruct((T, 1024), BF16),
                   jax.ShapeDtypeStruct((ATTN_HEADS, ATTN_BLOCK, 2 * ATTN_BLOCK), F32),
                   jax.ShapeDtypeStruct((1, ATTN_HEADS), F32)],
        compiler_params=_params(("arbitrary", "arbitrary")),
    )(proj, proj, bias, sinks, d_ya)


def _split3(x):
    hi = x.astype(BF16)
    r1 = x - hi.astype(F32)
    mid = r1.astype(BF16)
    lo = (r1 - mid.astype(F32)).astype(BF16)
    return jnp.concatenate([hi, mid, lo], axis=1)


def _tri_sum(tri, x):
    y = _dot(tri, _split3(x))
    return y[:, :128] + y[:, 128:256] + y[:, 256:]


def _hgrn_chunk(hq, hf, hi, lb):
    C = HGRN_CHUNK
    t = lax.broadcasted_iota(jnp.int32, (C, C), 0)
    s = lax.broadcasted_iota(jnp.int32, (C, C), 1)
    causal = s <= t
    sf = _sigmoid(hf)
    f = lb + (1.0 - lb) * sf
    G = _tri_sum(causal.astype(BF16), jnp.log(f))
    sq = _sigmoid(hq)
    qs = hq * sq
    k = 1.0 - f
    rowblk = lax.broadcasted_iota(jnp.int32, (C, 1), 0) // HGRN_SUB
    qt, kt, eq, ek = [], [], [], []
    for i in range(C // HGRN_SUB):
        lo = HGRN_SUB * i
        ref = G[lo + HGRN_SUB // 2:lo + HGRN_SUB // 2 + 1, :]
        eq_i = jnp.exp(G[lo:lo + HGRN_SUB] - ref)
        ek_i = jnp.exp(jnp.where(rowblk <= i, ref - G, 0.0))
        eq.append(eq_i)
        ek.append(ek_i)
        qt.append((qs[lo:lo + HGRN_SUB] * eq_i).astype(BF16))
        kt.append((k * ek_i).astype(BF16))
    A = jnp.concatenate([_dot(qt[i], kt[i], NT) for i in range(C // HGRN_SUB)], axis=0)
    A = jnp.where(causal, A, 0.0)
    glast = G[C - 1:C, :]
    eG = jnp.exp(G)
    edec = jnp.exp(glast - G)
    return dict(causal=causal, sf=sf, f=f, G=G, sq=sq, qs=qs, k=k, qt=qt, kt=kt, eq=eq, ek=ek, A=A,
                glast=glast, eG=eG, edec=edec, qhat=qs * eG, kdec=k * edec, v=hi)


def _hgrn_specs(nseq, ng, rows, reverse):
    W = HGRN_PAR
    nblk = W * HGRN_DIM // HGRN_COLS
    gi = (lambda g: ng - 1 - g) if reverse else (lambda g: g)

    def cols(c0):
        return [pl.BlockSpec((rows, HGRN_COLS),
                             lambda h, b, g, q=q: (b * ng + gi(g), c0 // HGRN_COLS + h * nblk + q)) for q in range(nblk)]

    return gi, (cols(C_HQ) + cols(C_HF) + cols(C_HI) + cols(C_HG)
                + [pl.BlockSpec((2, W * HGRN_DIM), lambda h, b, g: (0, h)),
                   pl.BlockSpec((W, 1, HGRN_DIM), lambda h, b, g: (h, 0, 0))])


def _hgrn_operands(refs):
    nblk = HGRN_PAR * HGRN_DIM // HGRN_COLS
    per = HGRN_COLS // HGRN_DIM

    def reader(group):
        def read(rs, w):
            lo = HGRN_DIM * (w % per)
            return group[w // per][rs, lo:lo + HGRN_DIM].astype(F32)
        return read

    readers = [reader(refs[nblk * a:nblk * (a + 1)]) for a in range(4)]
    return readers, refs[4 * nblk], refs[4 * nblk + 1], refs[4 * nblk + 2:]


def _hgrn_fwd(proj, lb_logits, gain3, nseq, S, shards):
    T = proj.shape[0]
    W = HGRN_PAR
    nc = S // HGRN_CHUNK
    cg = min(8, nc)
    ng = nc // cg
    rows = cg * HGRN_CHUNK
    n = len(shards)
    nh = HGRN_HEADS // W

    def body(*all_refs):
        (hq, hf, hi, hg), lbl_ref, gain_ref, refs = _hgrn_operands(all_refs)
        ins = refs[:n]
        o_ref, yh_ref, st_ref = refs[n:n + 3]
        outs = refs[n + 3:2 * n + 3]
        state, send_sems, recv_sems, local_sems = refs[2 * n + 3:]
        step = (pl.program_id(0), pl.program_id(1), pl.program_id(2))
        ops = _GatherOps(ins, outs, send_sems, recv_sems, local_sems)

        @pl.when((step[0] == 0) & (step[1] == 0) & (step[2] == 0))
        def _():
            for a in range(n):
                ops.local(a).start()
                ops.to_sibling(a).start()
                for q in range(3):
                    ops.to_chip(a, q).start()

        @pl.when(pl.program_id(2) == 0)
        def _():
            state[...] = jnp.zeros(state.shape, F32)

        lb_all = _sigmoid(lbl_ref[0:1, :] - lbl_ref[1:2, :])

        def chunk(c, carry):
            rs = pl.ds(pl.multiple_of(c * HGRN_CHUNK, HGRN_CHUNK), HGRN_CHUNK)
            for w in range(W):
                ls = slice(HGRN_DIM * w, HGRN_DIM * (w + 1))
                ch = _hgrn_chunk(hq(rs, w), hf(rs, w), hi(rs, w), lb_all[:, ls])
                st = state[w]
                st_ref[w, c] = st
                vb = ch["v"].astype(BF16)
                o = _dot(ch["qhat"].astype(BF16), st.astype(BF16), NT) + _dot(ch["A"].astype(BF16), vb)
                state[w] = st * jnp.exp(ch["glast"]) + _dot(vb, ch["kdec"].astype(BF16), TN)
                o_ref[rs, ls] = o
                r = lax.rsqrt(jnp.mean(o * o, axis=-1, keepdims=True) + NORM_EPS)
                gate = hg(rs, w)
                yh_ref[rs, ls] = (o * r * gain_ref[w] * (gate * _sigmoid(gate))).astype(BF16)
            return carry

        lax.fori_loop(0, cg, chunk, 0)

        @pl.when((step[0] == nh - 1) & (step[1] == nseq - 1) & (step[2] == ng - 1))
        def _():
            for a in range(n):
                ops.local(a).wait()
                ops.from_sibling(a).wait_recv()
                for q in range(3):
                    ops.from_chip(a, q).wait_recv()
                    ops.forward(a, q).start()
            for a in range(n):
                for q in range(3):
                    ops.forwarded(a, q).wait_recv()
                for cp in ops.sends(a):
                    cp.wait_send()

    _, in_specs = _hgrn_specs(nseq, ng, rows, False)
    out_row = lambda: pl.BlockSpec((rows, W * HGRN_DIM), lambda h, b, g: (b * ng + g, h))
    anyspec = lambda: pl.BlockSpec(memory_space=pl.ANY)
    return pl.pallas_call(
        body, name="hgrn_fwd",
        grid=(nh, nseq, ng),
        in_specs=in_specs + [anyspec() for _ in shards],
        out_specs=[out_row(), out_row(),
                   pl.BlockSpec((None, W, cg, HGRN_DIM, HGRN_DIM), lambda h, b, g: (b, h, g, 0, 0))]
                  + [anyspec() for _ in shards],
        out_shape=[jax.ShapeDtypeStruct((T, 1024), F32),
                   jax.ShapeDtypeStruct((T, 1024), BF16),
                   jax.ShapeDtypeStruct((nseq, HGRN_HEADS, nc, HGRN_DIM, HGRN_DIM), F32)]
                  + [jax.ShapeDtypeStruct((N_DEV * s.shape[0], s.shape[1]), s.dtype) for s in shards],
        scratch_shapes=[pltpu.VMEM((W, HGRN_DIM, HGRN_DIM), F32),
                        pltpu.SemaphoreType.DMA((n, _GATHER_SEMS)), pltpu.SemaphoreType.DMA((n, _GATHER_SEMS)),
                        pltpu.SemaphoreType.DMA((n,))],
        compiler_params=_params(("arbitrary", "arbitrary", "arbitrary")),
    )(*([proj] * (4 * W * HGRN_DIM // HGRN_COLS)), lb_logits, gain3, *shards)


def _hgrn_bwd(proj, lb_logits, gain3, o, d_yh, states, nseq, S, chip_sums):
    T = proj.shape[0]
    W = HGRN_PAR
    n = len(chip_sums)
    nh = HGRN_HEADS // W
    nc = S // HGRN_CHUNK
    cg = min(8, nc)
    ng = nc // cg
    rows = cg * HGRN_CHUNK
    C = HGRN_CHUNK
    nsub = C // HGRN_SUB

    def body(*all_refs):
        (hq_of, hf_of, hi_of, hg_of), lbl_ref, gain_ref, refs = _hgrn_operands(all_refs)
        o_ref, dyh_ref, st_ref = refs[:3]
        sums_in = refs[3:3 + n]
        dhq_ref, dhf_ref, dhi_ref, dhg_ref, dgain_ref, dlbl_ref = refs[3 + n:9 + n]
        sums_out = refs[9 + n:9 + 2 * n]
        dstate, dlb_acc, send_sems, recv_sems, local_sems = refs[9 + 2 * n:]
        h, b, g = pl.program_id(0), pl.program_id(1), pl.program_id(2)
        exchange = _ChipExchange(sums_in, sums_out, send_sems, recv_sems, local_sems)

        @pl.when((h == 0) & (b == 0) & (g == 0))
        def _():
            exchange.start()

        @pl.when(g == 0)
        def _():
            dstate[...] = jnp.zeros(dstate.shape, F32)

        @pl.when((b == 0) & (g == 0))
        def _():
            dgain_ref[...] = jnp.zeros(dgain_ref.shape, F32)
            dlb_acc[...] = jnp.zeros(dlb_acc.shape, F32)

        lb_all = _sigmoid(lbl_ref[0:1, :] - lbl_ref[1:2, :])

        def head_chunk(w, c):
            rs = pl.ds(pl.multiple_of(c * C, C), C)
            ls = slice(HGRN_DIM * w, HGRN_DIM * (w + 1))
            lb = lb_all[:, ls]
            gain = gain_ref[w]
            hq, hg = hq_of(rs, w), hg_of(rs, w)
            ch = _hgrn_chunk(hq, hf_of(rs, w), hi_of(rs, w), lb)
            ov = o_ref[rs, ls]
            dyh = dyh_ref[rs, ls].astype(F32)
            r = lax.rsqrt(jnp.mean(ov * ov, axis=-1, keepdims=True) + NORM_EPS)
            on = ov * r
            sg = _sigmoid(hg)
            doh = dyh * (hg * sg)
            dhg_ref[rs, ls] = (dyh * on * gain * (sg * (1.0 + hg * (1.0 - sg)))).astype(BF16)
            dgain_ref[w] += jnp.sum(doh * on, axis=0, keepdims=True)
            don = doh * gain
            do = r * (don - on * jnp.mean(don * on, axis=-1, keepdims=True))
            dob = do.astype(BF16)
            st = st_ref[w, c]
            dst = dstate[w]
            stb, dstb = st.astype(BF16), dst.astype(BF16)
            vb = ch["v"].astype(BF16)
            qhatb = ch["qhat"].astype(BF16)
            eglast = jnp.exp(ch["glast"])
            dqhat = _dot(dob, stb)
            dkdec = _dot(vb, dstb)
            dv = _dot(ch["kdec"].astype(BF16), dstb, NT)
            deg = jnp.sum(dst * st, axis=0, keepdims=True)
            dstate[w] = dst * eglast + _dot(dob, qhatb, TN)
            dA = jnp.where(ch["causal"], _dot(dob, vb, NT), 0.0)
            dv = dv + _dot(ch["A"].astype(BF16), dob, TN)
            dAb = dA.astype(BF16)
            dqs_parts, dgq_parts = [], []
            dk_intra, dgk = None, None
            for i in range(nsub):
                dA_i = dAb[HGRN_SUB * i:HGRN_SUB * (i + 1)]
                dqt = _dot(dA_i, ch["kt"][i])
                dkt = _dot(dA_i, ch["qt"][i], TN)
                dqs_parts.append(dqt * ch["eq"][i])
                dgq_parts.append(dqt * ch["qt"][i].astype(F32))
                dk_i = dkt * ch["ek"][i]
                dgk_i = dkt * ch["kt"][i].astype(F32)
                dk_intra = dk_i if dk_intra is None else dk_intra + dk_i
                dgk = dgk_i if dgk is None else dgk + dgk_i
            dqs_inter = dqhat * ch["eG"]
            dk_state = dkdec * ch["edec"]
            dqs = jnp.concatenate(dqs_parts, axis=0) + dqs_inter
            dk = dk_intra + dk_state
            dG = jnp.concatenate(dgq_parts, axis=0) - dgk + ch["qs"] * dqs_inter - ch["k"] * dk_state
            last_row = lax.broadcasted_iota(jnp.int32, (C, 1), 0) == C - 1
            tail = jnp.sum(dkdec * ch["kdec"], axis=0, keepdims=True) + deg * eglast
            dG = dG + jnp.where(last_row, tail, 0.0)
            anti = (lax.broadcasted_iota(jnp.int32, (C, C), 1)
                    >= lax.broadcasted_iota(jnp.int32, (C, C), 0)).astype(BF16)
            dlf = _tri_sum(anti, dG)
            df = dlf / ch["f"] - dk
            sf = ch["sf"]
            dhf_ref[rs, ls] = (df * (1.0 - lb) * sf * (1.0 - sf)).astype(BF16)
            dlb_acc[:, ls] += jnp.sum(df * (1.0 - sf), axis=0, keepdims=True)
            sq = ch["sq"]
            dhq_ref[rs, ls] = (dqs * (sq * (1.0 + hq * (1.0 - sq)))).astype(BF16)
            dhi_ref[rs, ls] = dv.astype(BF16)

        def chunk(cc, carry):
            for w in range(W):
                head_chunk(w, cg - 1 - cc)
            return carry

        lax.fori_loop(0, cg, chunk, 0)

        @pl.when((b == nseq - 1) & (g == ng - 1))
        def _():
            dl0 = dlb_acc[...] * lb_all * (1.0 - lb_all)
            dlbl_ref[0:1, :] = dl0
            dlbl_ref[1:2, :] = -dl0

        @pl.when((h == nh - 1) & (b == nseq - 1) & (g == ng - 1))
        def _():
            exchange.wait()

    gi, in_specs = _hgrn_specs(nseq, ng, rows, True)
    row = lambda: pl.BlockSpec((rows, W * HGRN_DIM), lambda h, b, g: (b * ng + gi(g), h))
    anyspec = lambda: pl.BlockSpec(memory_space=pl.ANY)
    return pl.pallas_call(
        body, name="hgrn_bwd",
        grid=(nh, nseq, ng),
        in_specs=in_specs + [row(), row(),
                             pl.BlockSpec((None, W, cg, HGRN_DIM, HGRN_DIM), lambda h, b, g: (b, h, gi(g), 0, 0))]
                 + [anyspec() for _ in chip_sums],
        out_specs=[row(), row(), row(), row(),
                   pl.BlockSpec((W, 1, HGRN_DIM), lambda h, b, g: (h, 0, 0)),
                   pl.BlockSpec((2, W * HGRN_DIM), lambda h, b, g: (0, h))] + [anyspec() for _ in chip_sums],
        out_shape=[jax.ShapeDtypeStruct((T, 1024), BF16)] * 4
                  + [jax.ShapeDtypeStruct((HGRN_HEADS, 1, HGRN_DIM), F32),
                     jax.ShapeDtypeStruct((2, HGRN_HEADS * HGRN_DIM), F32)]
                  + [jax.ShapeDtypeStruct(s.shape, s.dtype) for s in chip_sums],
        scratch_shapes=[pltpu.VMEM((W, HGRN_DIM, HGRN_DIM), F32), pltpu.VMEM((1, W * HGRN_DIM), F32)]
                       + _ChipExchange.scratch(n),
        compiler_params=_params(("arbitrary", "arbitrary", "arbitrary")),
    )(*([proj] * (4 * W * HGRN_DIM // HGRN_COLS)), lb_logits, gain3, o, d_yh, states, *chip_sums)


def _gate_specs(tm):
    spec = lambda c0, q: pl.BlockSpec((tm, 512), lambda i: (i, c0 // 512 + q))
    return [spec(C_GA, q) for q in range(4)] + [spec(C_GH, q) for q in range(4)]


def _gates(refs):
    ga = jnp.concatenate([r[...] for r in refs[0:4]], axis=1).astype(F32)
    gh = jnp.concatenate([r[...] for r in refs[4:8]], axis=1).astype(F32)
    return ga, gh


def _branch_merge(proj, ya, yh, wa_t, wh_t):
    T = proj.shape[0]
    tm = min(512, T)

    def body(*refs):
        ya_ref, yh_ref, wa_ref, wh_ref, merged_ref, ua_ref, uh_ref = refs[8:]
        ga, gh = _gates(refs)
        ua = _dot(ya_ref[...], wa_ref[...], NT)
        uh = _dot(yh_ref[...], wh_ref[...], NT)
        merged_ref[...] = (_sigmoid(ga) * ua + _sigmoid(gh) * uh).astype(BF16)
        ua_ref[...] = ua.astype(BF16)
        uh_ref[...] = uh.astype(BF16)

    rowb = lambda w: pl.BlockSpec((tm, w), lambda i: (i, 0))
    full = lambda a: pl.BlockSpec(a.shape, lambda i: (0, 0))
    return pl.pallas_call(
        body, name="branch_merge",
        grid=(T // tm,),
        in_specs=_gate_specs(tm) + [rowb(1024), rowb(1024), full(wa_t), full(wh_t)],
        out_specs=[rowb(D_MODEL)] * 3,
        out_shape=[jax.ShapeDtypeStruct((T, D_MODEL), BF16)] * 3,
        compiler_params=_params(("arbitrary",)),
    )(*([proj] * 8), ya, yh, wa_t, wh_t)


def _out_norm_loss(merged, x2, tgt2, gpost, wout):
    T = merged.shape[0]
    tm = min(256, T)

    def body(m_ref, x_ref, t_ref, gpost_ref, wo_ref, dy_ref, dm_ref, dout_ref, loss_ref, dgpost_ref):
        @pl.when(pl.program_id(0) == 0)
        def _():
            loss_ref[...] = jnp.zeros(loss_ref.shape, F32)
            dgpost_ref[...] = jnp.zeros(dgpost_ref.shape, F32)

        y = _dot(m_ref[...], wo_ref[...])
        r2 = lax.rsqrt(jnp.mean(y * y, axis=-1, keepdims=True) + NORM_EPS)
        yn = y * r2
        gpost = gpost_ref[...]
        err = x_ref[...] + yn * gpost - t_ref[...]
        loss_ref[...] += jnp.sum(err * err, axis=0, keepdims=True)
        dout = err * (1.0 / D_MODEL)
        dout_ref[...] = dout
        dgpost_ref[...] += jnp.sum(dout * yn, axis=0, keepdims=True)
        dyn = dout * gpost
        dy = (r2 * (dyn - yn * jnp.mean(dyn * yn, axis=-1, keepdims=True))).astype(BF16)
        dy_ref[...] = dy
        dm_ref[...] = _dot(dy, wo_ref[...], NT).astype(BF16)

    rowb = lambda: pl.BlockSpec((tm, D_MODEL), lambda i: (i, 0))
    vec = lambda: pl.BlockSpec((1, D_MODEL), lambda i: (0, 0))
    return pl.pallas_call(
        body, name="out_norm_loss",
        grid=(T // tm,),
        in_specs=[rowb(), rowb(), rowb(), vec(), pl.BlockSpec(wout.shape, lambda i: (0, 0))],
        out_specs=[rowb(), rowb(), rowb(), vec(), vec()],
        out_shape=[jax.ShapeDtypeStruct((T, D_MODEL), BF16)] * 2
                  + [jax.ShapeDtypeStruct((T, D_MODEL), F32)] + [jax.ShapeDtypeStruct((1, D_MODEL), F32)] * 2,
        compiler_params=_params(("arbitrary",)),
    )(merged, x2, tgt2, gpost, wout)


def _branch_bwd(proj, dm, ua, uh, wa_t, wh_t):
    T = proj.shape[0]
    tm = min(256, T)

    def body(*refs):
        (dm_ref, ua_ref, uh_ref, wa_ref, wh_ref,
         dua_ref, duh_ref, dga_ref, dgh_ref, dya_ref, dyh_ref) = refs[8:]
        ga, gh = _gates(refs)
        sa, sh = _sigmoid(ga), _sigmoid(gh)
        dm = dm_ref[...].astype(F32)
        dua = (dm * sa).astype(BF16)
        duh = (dm * sh).astype(BF16)
        dua_ref[...] = dua
        duh_ref[...] = duh
        dga_ref[...] = (dm * ua_ref[...].astype(F32) * (sa * (1.0 - sa))).astype(BF16)
        dgh_ref[...] = (dm * uh_ref[...].astype(F32) * (sh * (1.0 - sh))).astype(BF16)
        dya_ref[...] = _dot(dua, wa_ref[...]).astype(BF16)
        dyh_ref[...] = _dot(duh, wh_ref[...]).astype(BF16)

    rowb = lambda w: pl.BlockSpec((tm, w), lambda i: (i, 0))
    full = lambda a: pl.BlockSpec(a.shape, lambda i: (0, 0))
    return pl.pallas_call(
        body, name="branch_bwd",
        grid=(T // tm,),
        in_specs=_gate_specs(tm) + [rowb(D_MODEL)] * 3 + [full(wa_t), full(wh_t)],
        out_specs=[rowb(D_MODEL)] * 4 + [rowb(1024)] * 2,
        out_shape=[jax.ShapeDtypeStruct((T, D_MODEL), BF16)] * 4 + [jax.ShapeDtypeStruct((T, 1024), BF16)] * 2,
        compiler_params=_params(("arbitrary",)),
    )(*([proj] * 8), dm, ua, uh, wa_t, wh_t)


def _tn_matmul(L, R, bm, bn, name):
    T, M = L.shape
    N = R.shape[1]

    def body(l_ref, r_ref, out_ref):
        out_ref[...] = _dot(l_ref[...], r_ref[...], TN).astype(BF16)

    return pl.pallas_call(
        body, name=name,
        grid=(N // bn, M // bm),
        in_specs=[pl.BlockSpec((T, bm), lambda j, i: (0, i)),
                  pl.BlockSpec((T, bn), lambda j, i: (0, j))],
        out_specs=pl.BlockSpec((bm, bn), lambda j, i: (i, j)),
        out_shape=jax.ShapeDtypeStruct((M, N), BF16),
        compiler_params=_params(("arbitrary", "arbitrary")),
    )(L, R)


def _dh_prenorm_bwd(dproj, wt_in, x2, dout, gpre, chip_sums):
    T = x2.shape[0]
    tm = min(1024, T)
    ne = 2
    te = tm // ne
    tk = 768
    nk = IN_WIDTH // tk
    nt = T // tm
    n = len(chip_sums)

    def body(dp_ref, w_ref, x_ref, dout_ref, g_ref, *refs):
        ins = refs[:n]
        gx_ref, dg_ref = refs[n], refs[n + 1]
        outs = refs[n + 2:2 * n + 2]
        acc, send_sems, recv_sems, local_sems = refs[2 * n + 2:]
        i, k = pl.program_id(0), pl.program_id(1)
        exchange = _ChipExchange(ins, outs, send_sems, recv_sems, local_sems)

        @pl.when((i == 0) & (k == 0))
        def _():
            dg_ref[...] = jnp.zeros(dg_ref.shape, F32)
            exchange.start()

        @pl.when((i == nt - 1) & (k == nk + ne - 1))
        def _():
            exchange.wait()

        @pl.when(k == 0)
        def _():
            acc[...] = _dot(dp_ref[...], w_ref[...])

        @pl.when((k > 0) & (k < nk))
        def _():
            acc[...] += _dot(dp_ref[...], w_ref[...])

        @pl.when(k >= nk)
        def _():
            dh = acc[pl.ds(pl.multiple_of((k - nk) * te, te), te), :]
            x = x_ref[...]
            r = lax.rsqrt(jnp.mean(x * x, axis=-1, keepdims=True) + NORM_EPS)
            xn = x * r
            dg_ref[...] += jnp.sum(dh * xn, axis=0, keepdims=True)
            dxn = dh * g_ref[...]
            gx_ref[...] = dout_ref[...] + r * (dxn - xn * jnp.mean(dxn * xn, axis=-1, keepdims=True))

    rowb = lambda: pl.BlockSpec((te, D_MODEL), lambda i, k: (ne * i + jnp.clip(k - nk, 0, ne - 1), 0))
    vec = lambda: pl.BlockSpec((1, D_MODEL), lambda i, k: (0, 0))
    anyspec = lambda: pl.BlockSpec(memory_space=pl.ANY)
    return pl.pallas_call(
        body, name="dh_prenorm_bwd",
        grid=(nt, nk + ne),
        in_specs=[pl.BlockSpec((tm, tk), lambda i, k: (i, jnp.minimum(k, nk - 1))),
                  pl.BlockSpec((tk, D_MODEL), lambda i, k: (jnp.minimum(k, nk - 1), 0)),
                  rowb(), rowb(), vec()] + [anyspec() for _ in chip_sums],
        out_specs=[rowb(), vec()] + [anyspec() for _ in chip_sums],
        out_shape=[jax.ShapeDtypeStruct((T, D_MODEL), F32), jax.ShapeDtypeStruct((1, D_MODEL), F32)]
                  + [jax.ShapeDtypeStruct(s.shape, s.dtype) for s in chip_sums],
        scratch_shapes=[pltpu.VMEM((tm, D_MODEL), F32)] + _ChipExchange.scratch(n),
        compiler_params=_params(("arbitrary", "arbitrary")),
    )(dproj, wt_in, x2, dout, gpre, *chip_sums)


def _sum_slots(recv, br, name):
    nslot, R, C = recv.shape

    def body(r_ref, out_ref):
        acc = r_ref[0].astype(F32)
        for s in range(1, nslot):
            acc = acc + r_ref[s].astype(F32)
        out_ref[...] = acc

    return pl.pallas_call(
        body, name=name,
        grid=(R // br,),
        in_specs=[pl.BlockSpec((nslot, br, C), lambda i: (0, i, 0))],
        out_specs=pl.BlockSpec((br, C), lambda i: (i, 0)),
        out_shape=jax.ShapeDtypeStruct((R, C), F32),
        compiler_params=_params(("arbitrary",)),
    )(recv)


def _adamw_math(w, g, m, v):
    m = ADAM_B1 * m + (1.0 - ADAM_B1) * g
    v = ADAM_B2 * v + (1.0 - ADAM_B2) * (g * g)
    m_hat = m / (1.0 - ADAM_B1 ** ADAM_STEP)
    v_hat = v / (1.0 - ADAM_B2 ** ADAM_STEP)
    delta = -ADAM_LR * (m_hat / (jnp.sqrt(v_hat) + ADAM_EPS) + ADAM_WD * w)
    return delta, m, v


def _adamw(w, g, m, v, br, name):
    R, C = w.shape

    def body(w_ref, g_ref, m_ref, v_ref, d_ref, nm_ref, nv_ref):
        d_ref[...], nm_ref[...], nv_ref[...] = _adamw_math(w_ref[...], g_ref[...], m_ref[...], v_ref[...])

    spec = lambda: pl.BlockSpec((br, C), lambda i: (i, 0))
    return pl.pallas_call(
        body, name=name,
        grid=(R // br,),
        in_specs=[spec(), spec(), spec(), spec()],
        out_specs=[spec(), spec(), spec()],
        out_shape=[jax.ShapeDtypeStruct((R, C), F32)] * 3,
        compiler_params=_params(("arbitrary",)),
    )(w, g, m, v)


def _sibling_exchange(partials, name):
    n = len(partials)

    def body(*refs):
        ins, outs = refs[:n], refs[n:2 * n]
        send_sems, recv_sems = refs[2 * n:]
        x, y, c = _place()

        def copy(a, p):
            return pltpu.make_async_remote_copy(
                src_ref=ins[a].at[p, 1 - c], dst_ref=outs[a].at[p],
                send_sem=send_sems.at[a, p], recv_sem=recv_sems.at[a, p],
                device_id=(x, y, 1 - c), device_id_type=pl.DeviceIdType.MESH)

        copies = [copy(a, p) for p in range(4) for a in range(n)]
        for cp in copies:
            cp.start()
        for cp in copies:
            cp.wait()

    anyspec = lambda: pl.BlockSpec(memory_space=pl.ANY)
    return pl.pallas_call(
        body, name=name,
        in_specs=[anyspec() for _ in partials],
        out_specs=[anyspec() for _ in partials],
        out_shape=[jax.ShapeDtypeStruct((4,) + p.shape[2:], p.dtype) for p in partials],
        scratch_shapes=[pltpu.SemaphoreType.DMA((n, 4)), pltpu.SemaphoreType.DMA((n, 4))],
    )(*partials)


def _chip_sum(partial, from_sibling, br, name):
    _, _, R, C = partial.shape
    cls = lax.axis_index("c").astype(jnp.int32).reshape(1)

    def body(c_ref, mine_ref, sib_ref, out_ref):
        out_ref[...] = (mine_ref[...].astype(F32) + sib_ref[...].astype(F32)).astype(BF16)

    grid_spec = pltpu.PrefetchScalarGridSpec(
        num_scalar_prefetch=1,
        grid=(4, R // br),
        in_specs=[pl.BlockSpec((None, None, br, C), lambda p, i, c: (p, c[0], i, 0)),
                  pl.BlockSpec((None, br, C), lambda p, i, c: (p, i, 0))],
        out_specs=pl.BlockSpec((None, br, C), lambda p, i, c: (p, i, 0)),
    )
    return pl.pallas_call(
        body, name=name, grid_spec=grid_spec,
        out_shape=jax.ShapeDtypeStruct((4, R, C), BF16),
        compiler_params=_params(("arbitrary", "arbitrary")),
    )(cls, partial, from_sibling)


def _all_reduce_small(packed):
    shape = packed.shape

    def body(in_ref, out_ref, slots, send_sems, recv_sems):
        x, y, c = _place()
        my_slot = 4 * x + 2 * y + c

        def peer(k):
            return (x ^ ((k >> 2) & 1), y ^ ((k >> 1) & 1), c ^ (k & 1))

        def copy(k):
            p = peer(k)
            return pltpu.make_async_remote_copy(
                src_ref=in_ref, dst_ref=slots.at[my_slot],
                send_sem=send_sems.at[k - 1], recv_sem=recv_sems.at[k - 1],
                device_id=p, device_id_type=pl.DeviceIdType.MESH)

        def arrival(k):
            p = peer(k)
            return pltpu.make_async_remote_copy(
                src_ref=in_ref, dst_ref=slots.at[4 * p[0] + 2 * p[1] + p[2]],
                send_sem=send_sems.at[k - 1], recv_sem=recv_sems.at[k - 1],
                device_id=p, device_id_type=pl.DeviceIdType.MESH)

        sends = [copy(k) for k in range(1, N_DEV)]
        for cp in sends:
            cp.start()
        slots[my_slot] = in_ref[...]
        for k in range(1, N_DEV):
            arrival(k).wait_recv()
        for cp in sends:
            cp.wait_send()
        acc = slots[0]
        for s in range(1, N_DEV):
            acc = acc + slots[s]
        out_ref[...] = acc

    return pl.pallas_call(
        body, name="all_reduce_small",
        in_specs=[pl.BlockSpec(memory_space=pltpu.VMEM)],
        out_specs=pl.BlockSpec(memory_space=pltpu.VMEM),
        out_shape=jax.ShapeDtypeStruct(shape, F32),
        scratch_shapes=[pltpu.VMEM((N_DEV,) + shape, F32),
                        pltpu.SemaphoreType.DMA((7,)), pltpu.SemaphoreType.DMA((7,))],
    )(packed)


def _pack_small(norm_pre, norm_post, lb_logits, hgrn_norm, rel_bias, sinks, extra=None):
    tail = [hgrn_norm.reshape(1, 1024), rel_bias.reshape(1, 512), sinks.reshape(1, 16)]
    used = 1024 + 512 + 16
    if extra is not None:
        tail.append(extra.reshape(1, 1))
        used += 1
    tail.append(jnp.zeros((1, D_MODEL - used), F32))
    rows = [norm_pre.reshape(1, D_MODEL), norm_post.reshape(1, D_MODEL), lb_logits.reshape(1, D_MODEL),
            jnp.concatenate(tail, axis=1), jnp.zeros((4, D_MODEL), F32)]
    return jnp.concatenate(rows, axis=0)


def _unpack_small(p):
    return (p[0:1], p[3, 1024:1536].reshape(REL_BUCKETS, ATTN_HEADS), p[3:4, 1536:1552],
            p[2].reshape(2, 1024), p[3, 0:1024].reshape(1, HGRN_HEADS, HGRN_DIM), p[1:2])


def _local_step(nseq, S, x2, tgt2, proj, h, rel_bias, attn_sinks, lb_logits, hgrn_norm, norm_post, shards):
    nb = S // ATTN_BLOCK
    bucket = jnp.asarray(_t5_bucket_table())
    gain3 = hgrn_norm.reshape(HGRN_HEADS, 1, HGRN_DIM)

    bias = _bias_table(rel_bias, bucket)
    ya = _attn_fwd(proj, bias, attn_sinks, nseq, nb)
    o, yh, states, wout, wa_t, wh_t = _hgrn_fwd(proj, lb_logits, gain3, nseq, S, shards)
    merged, ua, uh = _branch_merge(proj, ya, yh, wa_t, wh_t)
    dy, dm, dout, loss_cols, d_gpost = _out_norm_loss(merged, x2, tgt2, norm_post, wout)
    dua, duh, dga, dgh, dya, dyh = _branch_bwd(proj, dm, ua, uh, wa_t, wh_t)

    p_out = _tn_matmul(merged, dy, 256, 1024, "dw_out")
    p_a = _tn_matmul(dua, ya, 256, 1024, "dw_branch_attn")
    p_h = _tn_matmul(duh, yh, 256, 1024, "dw_branch_hgrn")
    small_sums = _chip_sums([p_out, p_a, p_h], (128, 128, 128), ("dw_out", "dw_branch_attn", "dw_branch_hgrn"),
                            "sibling_exchange_small")

    dq, dkv, dg, dbias, d_sinks = _attn_bwd(proj, bias, attn_sinks, dya, nseq, nb)
    d_rel_bias = _bias_table_bwd(dbias, bucket)
    dhq, dhf, dhi, dhg, d_gain, d_lbl, r_out, r_a, r_h = _hgrn_bwd(
        proj, lb_logits, gain3, o, dyh, states, nseq, S, small_sums)
    dproj = jnp.concatenate([dq, dkv, dg, dhq, dhf, dhi, dhg, dga, dgh], axis=1)
    p_in = _tn_matmul(dproj, h, 768, 1024, "dw_in")
    return dproj, dout, p_in, r_out, r_a, r_h, d_gpost, d_lbl, d_gain, d_rel_bias, d_sinks, loss_cols


def _chip_sums(partials, block_rows, names, exchange_name):
    split = [p.reshape(4, 2, p.shape[0] // N_DEV, p.shape[1]) for p in partials]
    from_sibling = _sibling_exchange(split, exchange_name)
    return [_chip_sum(p, f, br, "chip_sum_" + nm) for p, f, br, nm in zip(split, from_sibling, block_rows, names)]


def kernel(x, norm_pre, w_in, rel_bias, attn_sinks, lb_logits, hgrn_norm, w_branch_attn, w_branch_hgrn, w_out, norm_post, loss_target, m_norm_pre, m_w_in, m_rel_bias, m_attn_sinks, m_lb_logits, m_hgrn_norm, m_w_branch_attn, m_w_branch_hgrn, m_w_out, m_norm_post, v_norm_pre, v_w_in, v_rel_bias, v_attn_sinks, v_lb_logits, v_hgrn_norm, v_w_branch_attn, v_w_branch_hgrn, v_w_out, v_norm_post):
    nseq, S, _ = x.shape
    T = nseq * S
    x2 = x.reshape(T, D_MODEL)
    tgt2 = loss_target.reshape(T, D_MODEL)

    h = _prenorm(x2, norm_pre)
    proj, wt_in = _gather_inproj(h, w_in[0].T.astype(BF16))
    shards = [w_out[0].astype(BF16), w_branch_attn[0].T.astype(BF16), w_branch_hgrn[0].T.astype(BF16)]

    (dproj, dout, p_in, r_out, r_a, r_h, d_gpost, d_lbl, d_gain, d_rel_bias, d_sinks, loss_cols) = _local_step(
        nseq, S, x2, tgt2, proj, h, rel_bias, attn_sinks, lb_logits, hgrn_norm, norm_post, shards)

    in_sums = _chip_sums([p_in], (192,), ("dw_in",), "sibling_exchange_w_in")
    grad_x2, d_gpre, r_in = _dh_prenorm_bwd(dproj, wt_in, x2, dout, norm_pre, in_sums)
    g_in = _sum_slots(r_in, 192, "sum_dw_in").T
    g_out = _sum_slots(r_out, 128, "sum_dw_out")
    g_a = _sum_slots(r_a, 128, "sum_dw_branch_attn").T
    g_h = _sum_slots(r_h, 128, "sum_dw_branch_hgrn").T

    loss_part = 0.5 / D_MODEL * jnp.sum(loss_cols)
    packed = _pack_small(d_gpre, d_gpost, d_lbl, d_gain, d_rel_bias, d_sinks, extra=loss_part)
    total = _all_reduce_small(packed)
    loss = total[3, 1024 + 512 + 16]
    sm_w = _pack_small(norm_pre, norm_post, lb_logits, hgrn_norm, rel_bias, attn_sinks)
    sm_m = _pack_small(m_norm_pre, m_norm_post, m_lb_logits, m_hgrn_norm, m_rel_bias, m_attn_sinks)
    sm_v = _pack_small(v_norm_pre, v_norm_post, v_lb_logits, v_hgrn_norm, v_rel_bias, v_attn_sinks)
    sm_d, sm_nm, sm_nv = _adamw(sm_w, total, sm_m, sm_v, 8, "adamw_small")

    d_in, nm_in, nv_in = _adamw(w_in[0], g_in, m_w_in[0], v_w_in[0], 256, "adamw_w_in")
    d_out, nm_out, nv_out = _adamw(w_out[0], g_out, m_w_out[0], v_w_out[0], 128, "adamw_w_out")
    d_a, nm_a, nv_a = _adamw(w_branch_attn[0], g_a, m_w_branch_attn[0], v_w_branch_attn[0], 256, "adamw_w_branch_attn")
    d_h, nm_h, nv_h = _adamw(w_branch_hgrn[0], g_h, m_w_branch_hgrn[0], v_w_branch_hgrn[0], 256, "adamw_w_branch_hgrn")

    def group(small, big_in, big_a, big_h, big_out):
        npre, rb, sk, lbl, hn, npost = _unpack_small(small)
        return (npre, big_in[None], rb, sk, lbl, hn, big_a[None], big_h[None], big_out[None], npost)

    return (loss, grad_x2.reshape(nseq, S, D_MODEL),
            *group(total, g_in, g_a, g_h, g_out),
            *group(sm_d, d_in, d_a, d_h, d_out),
            *group(sm_nm, nm_in, nm_a, nm_h, nm_out),
            *group(sm_nv, nv_in, nv_a, nv_h, nv_out))
```

```python
import functools
import math

import numpy as np
import jax
import jax.numpy as jnp
from jax import lax
from jax.experimental import pallas as pl
from jax.experimental.pallas import tpu as pltpu

F32 = jnp.float32
BF16 = jnp.bfloat16

D_MODEL = 2048
ATTN_HEADS = 16
ATTN_HEAD_DIM = 64
ATTN_GROUP = 4
ATTN_BLOCK = 128
HGRN_HEADS = 8
HGRN_DIM = 128
HGRN_CHUNK = 64
HGRN_SUB = 16
HGRN_PAR = 8
HGRN_COLS = 512
REL_BUCKETS = 32
REL_MAX_DIST = 128
NORM_EPS = 1e-6
C_AQ, C_AK, C_AV, C_AG = 0, 1024, 1280, 1536
C_HQ, C_HF, C_HI, C_HG = 2560, 3584, 4608, 5632
C_GA, C_GH = 6656, 8704
IN_WIDTH = 10752
N_DEV = 8
assert all(c0 % HGRN_COLS == 0 for c0 in (C_HQ, C_HF, C_HI, C_HG)) and (HGRN_PAR * HGRN_DIM) % HGRN_COLS == 0

ADAM_LR = 0.001
ADAM_B1 = 0.9
ADAM_B2 = 0.999
ADAM_EPS = 1e-08
ADAM_WD = 0.01
ADAM_STEP = 10

VMEM_LIMIT_V7X = 56 * 1024 * 1024
NEG_BIG = -1e30

NT = (((1,), (1,)), ((), ()))
TN = (((0,), (0,)), ((), ()))
NN = (((1,), (0,)), ((), ()))


def _dot(a, b, dims=NN):
    return lax.dot_general(a, b, dims, preferred_element_type=F32)


def _params(sem=None):
    return pltpu.CompilerParams(dimension_semantics=sem, vmem_limit_bytes=VMEM_LIMIT_V7X)


def _sigmoid(x):
    return 1.0 / (1.0 + jnp.exp(-x))


def _t5_bucket_table():
    qi = np.arange(ATTN_BLOCK)[:, None]
    si = np.arange(2 * ATTN_BLOCK)[None, :]
    dist = qi + ATTN_BLOCK - si
    max_exact = REL_BUCKETS // 2
    d = np.maximum(dist, 0)
    df = np.maximum(d, 1).astype(np.float32)
    large = max_exact + (np.log(df / np.float32(max_exact)).astype(np.float32)
                         / np.float32(math.log(REL_MAX_DIST / max_exact))
                         * np.float32(REL_BUCKETS - max_exact)).astype(np.int32)
    large = np.minimum(large, REL_BUCKETS - 1)
    return np.where(d < max_exact, d, large).astype(np.int32)


def _place():
    return lax.axis_index("x"), lax.axis_index("y"), lax.axis_index("c")


class _GatherOps:
    def __init__(self, ins, outs, send_sems, recv_sems, local_sems):
        self.ins, self.outs = ins, outs
        self.send_sems, self.recv_sems, self.local_sems = send_sems, recv_sems, local_sems
        x, y, c = _place()
        self.c = c
        self.me, self.sibling = (x, y, c), (x, y, 1 - c)
        self.chips = [(1 - x, y), (x, 1 - y), (1 - x, 1 - y)]

    def _rows(self, a, dev):
        r = self.ins[a].shape[0]
        return self.outs[a].at[pl.ds((4 * dev[0] + 2 * dev[1] + dev[2]) * r, r), :]

    def _copy(self, a, k, block, to, src=None):
        return pltpu.make_async_remote_copy(
            src_ref=self._rows(a, block) if src is None else src, dst_ref=self._rows(a, block),
            send_sem=self.send_sems.at[a, k], recv_sem=self.recv_sems.at[a, k],
            device_id=to, device_id_type=pl.DeviceIdType.MESH)

    def local(self, a):
        return pltpu.make_async_copy(self.ins[a], self._rows(a, self.me), self.local_sems.at[a])

    def to_sibling(self, a):
        return self._copy(a, 0, self.me, self.sibling, src=self.ins[a])

    def to_chip(self, a, q):
        return self._copy(a, 1 + q, self.me, (*self.chips[q], self.c), src=self.ins[a])

    def forward(self, a, q):
        return self._copy(a, 4 + q, (*self.chips[q], self.c), self.sibling)

    def from_sibling(self, a):
        return self._copy(a, 0, self.sibling, self.me)

    def from_chip(self, a, q):
        return self._copy(a, 1 + q, (*self.chips[q], self.c), self.me)

    def forwarded(self, a, q):
        return self._copy(a, 4 + q, (*self.chips[q], 1 - self.c), self.me)

    def sends(self, a):
        return [self.to_sibling(a)] + [self.to_chip(a, q) for q in range(3)] + [self.forward(a, q) for q in range(3)]


class _ChipExchange:
    def __init__(self, ins, outs, send_sems, recv_sems, local_sems):
        self.ins, self.outs = ins, outs
        self.send_sems, self.recv_sems, self.local_sems = send_sems, recv_sems, local_sems
        self.x, self.y, self.c = _place()
        self.my_chip = 2 * self.x + self.y

    def _peer(self, q):
        return (self.x ^ (q >> 1), self.y ^ (q & 1))

    def _copy(self, a, q, src_block, dst_slot):
        px, py = self._peer(q)
        return pltpu.make_async_remote_copy(
            src_ref=self.ins[a].at[src_block], dst_ref=self.outs[a].at[dst_slot],
            send_sem=self.send_sems.at[a, q - 1], recv_sem=self.recv_sems.at[a, q - 1],
            device_id=(px, py, self.c), device_id_type=pl.DeviceIdType.MESH)

    def _send(self, a, q):
        px, py = self._peer(q)
        return self._copy(a, q, 2 * px + py, self.my_chip)

    def _arrival(self, a, q):
        px, py = self._peer(q)
        return self._copy(a, q, self.my_chip, 2 * px + py)

    def _local(self, a):
        return pltpu.make_async_copy(self.ins[a].at[self.my_chip], self.outs[a].at[self.my_chip],
                                     self.local_sems.at[a])

    def start(self):
        for a in range(len(self.ins)):
            self._local(a).start()
            for q in range(1, 4):
                self._send(a, q).start()

    def wait(self):
        for a in range(len(self.ins)):
            for q in range(1, 4):
                self._arrival(a, q).wait_recv()
        for a in range(len(self.ins)):
            for q in range(1, 4):
                self._send(a, q).wait_send()
            self._local(a).wait()

    @staticmethod
    def scratch(n):
        return [pltpu.SemaphoreType.DMA((n, 3)), pltpu.SemaphoreType.DMA((n, 3)), pltpu.SemaphoreType.DMA((n,))]


_GATHER_SEMS = 7
LOCAL_DMA_THREAD = 1

INPROJ_TILE = 896


def _prenorm(x2, gpre):
    T = x2.shape[0]
    tm = min(512, T)

    def body(x_ref, g_ref, h_ref):
        x = x_ref[...]
        r = lax.rsqrt(jnp.mean(x * x, axis=-1, keepdims=True) + NORM_EPS)
        h_ref[...] = (x * r * g_ref[...]).astype(BF16)

    return pl.pallas_call(
        body, name="prenorm",
        grid=(T // tm,),
        in_specs=[pl.BlockSpec((tm, D_MODEL), lambda i: (i, 0)), pl.BlockSpec((1, D_MODEL), lambda i: (0, 0))],
        out_specs=pl.BlockSpec((tm, D_MODEL), lambda i: (i, 0)),
        out_shape=jax.ShapeDtypeStruct((T, D_MODEL), BF16),
        compiler_params=_params(("arbitrary",)),
    )(x2, gpre)


def _gather_inproj(h, wt_shard):
    T = h.shape[0]
    tm = min(1024, T)
    nm = T // tm
    tn = INPROJ_TILE
    ntile = IN_WIDTH // tn
    nstep = ntile * nm

    def body(h_hbm, w_in, proj_hbm, w_out, hbuf, wtile, obuf, send_sems, recv_sems, local_sems, h_sem, w_sems, o_sems):
        j, i = pl.program_id(0), pl.program_id(1)
        step = j * nm + i
        slot = step % 2
        ops = _GatherOps([w_in], [w_out], send_sems, recv_sems, local_sems)
        x, y, _ = _place()

        def tile_of(jj):
            k = jj // 3
            return 3 * ((2 * x + y) ^ (((k & 1) << 1) | (k >> 1))) + jj % 3

        tile = tile_of(j)

        def h_load():
            return pltpu.make_async_copy(h_hbm, hbuf, h_sem)

        def store(s, rows, cols):
            return pltpu.make_async_copy(obuf.at[s], proj_hbm.at[rows, cols], o_sems.at[s])

        def window(ii, t):
            return pl.ds(pl.multiple_of(ii * tm, tm), tm), pl.ds(pl.multiple_of(t * tn, tn), tn)

        @pl.when(step == 0)
        def _():
            h_load().start()
            ops.local(0).start()
            ops.to_sibling(0).start()
            ops.to_chip(0, 0).start()
            ops.to_chip(0, 1).start()
            h_load().wait()

        for kk in range(4):
            @pl.when((j == 3 * kk) & (i == 0))
            def _(kk=kk):
                if kk == 0:
                    ops.local(0).wait()
                    ops.from_sibling(0).wait_recv()
                else:
                    q = kk - 1
                    ops.from_chip(0, q).wait_recv()
                    ops.forward(0, q).start()
                    if q == 0:
                        ops.to_chip(0, 2).start()
                    ops.forwarded(0, q).wait_recv()

        wslot = j % 2

        def fetch(jj, sw):
            rows = pl.ds(pl.multiple_of(tile_of(jj) * tn, tn), tn)
            return pltpu.make_async_copy(w_out.at[rows, :], wtile.at[sw], w_sems.at[sw])

        @pl.when((i == 0) & (j % 3 == 0))
        def _():
            fetch(j, wslot).start(LOCAL_DMA_THREAD)

        @pl.when(i == 0)
        def _():
            fetch(j, wslot).wait()

        @pl.when((i == 0) & (j % 3 != 2))
        def _():
            fetch(j + 1, 1 - wslot).start(LOCAL_DMA_THREAD)

        @pl.when(step >= 2)
        def _():
            store(slot, *window(0, 0)).wait()

        hv = hbuf[pl.ds(pl.multiple_of(i * tm, tm), tm), :]
        obuf[slot] = _dot(hv, wtile[wslot], NT).astype(BF16)
        store(slot, *window(i, tile)).start(LOCAL_DMA_THREAD)

        @pl.when(step == nstep - 1)
        def _():
            for s in range(min(2, nstep)):
                store(s, *window(0, 0)).wait()
            for cp in ops.sends(0):
                cp.wait_send()

    anyspec = lambda: pl.BlockSpec(memory_space=pl.ANY)
    return pl.pallas_call(
        body, name="gather_inproj",
        grid=(ntile, nm),
        in_specs=[anyspec(), anyspec()],
        out_specs=[anyspec(), anyspec()],
        out_shape=[jax.ShapeDtypeStruct((T, IN_WIDTH), BF16),
                   jax.ShapeDtypeStruct((N_DEV * wt_shard.shape[0], D_MODEL), BF16)],
        scratch_shapes=[pltpu.VMEM((T, D_MODEL), BF16), pltpu.VMEM((2, tn, D_MODEL), BF16),
                        pltpu.VMEM((2, tm, tn), BF16),
                        pltpu.SemaphoreType.DMA((1, _GATHER_SEMS)), pltpu.SemaphoreType.DMA((1, _GATHER_SEMS)),
                        pltpu.SemaphoreType.DMA((1,)), pltpu.SemaphoreType.DMA, pltpu.SemaphoreType.DMA((2,)),
                        pltpu.SemaphoreType.DMA((2,))],
        compiler_params=_params(("arbitrary", "arbitrary")),
    )(h, wt_shard)


def _bias_table(rel_bias, bucket):
    def body(rb_ref, bk_ref, out_ref):
        h = pl.program_id(0)
        bk = bk_ref[...]
        acc = jnp.zeros(bk.shape, F32)
        for b in range(REL_BUCKETS):
            acc = jnp.where(bk == b, rb_ref[b, h], acc)
        out_ref[...] = acc

    return pl.pallas_call(
        body, name="bias_table",
        grid=(ATTN_HEADS,),
        in_specs=[pl.BlockSpec(memory_space=pltpu.SMEM),
                  pl.BlockSpec((ATTN_BLOCK, 2 * ATTN_BLOCK), lambda h: (0, 0))],
        out_specs=pl.BlockSpec((None, ATTN_BLOCK, 2 * ATTN_BLOCK), lambda h: (h, 0, 0)),
        out_shape=jax.ShapeDtypeStruct((ATTN_HEADS, ATTN_BLOCK, 2 * ATTN_BLOCK), F32),
        compiler_params=_params(("arbitrary",)),
    )(rel_bias, bucket)


def _bias_table_bwd(dbias, bucket):
    def body(db_ref, bk_ref, out_ref):
        h = pl.program_id(0)
        bk = bk_ref[...]
        db = db_ref[...]
        for b in range(REL_BUCKETS):
            out_ref[b, h] = jnp.sum(jnp.where(bk == b, db, 0.0))

    return pl.pallas_call(
        body, name="bias_table_bwd",
        grid=(ATTN_HEADS,),
        in_specs=[pl.BlockSpec((None, ATTN_BLOCK, 2 * ATTN_BLOCK), lambda h: (h, 0, 0)),
                  pl.BlockSpec((ATTN_BLOCK, 2 * ATTN_BLOCK), lambda h: (0, 0))],
        out_specs=pl.BlockSpec(memory_space=pltpu.SMEM),
        out_shape=jax.ShapeDtypeStruct((REL_BUCKETS, ATTN_HEADS), F32),
        compiler_params=_params(("arbitrary",)),
    )(dbias, bucket)


def _attn_common(qkvg, kv_prev, blk):
    lane = lax.broadcasted_iota(jnp.int32, (1, 128), 1)
    half = (lane < ATTN_HEAD_DIM, lane >= ATTN_HEAD_DIM)
    kv_cur = qkvg[:, C_AK:C_AG]
    win = jnp.concatenate([kv_prev, kv_cur], axis=0)
    k_slab, v_slab = [], []
    for r in range(2):
        ks = win[:, 128 * r:128 * r + 128]
        vs = win[:, 256 + 128 * r:256 + 128 * r + 128]
        k_slab.append((ks, pltpu.roll(ks, ATTN_HEAD_DIM, 1)))
        v_slab.append((vs, pltpu.roll(vs, ATTN_HEAD_DIM, 1)))
    rows4 = ATTN_GROUP * ATTN_BLOCK
    qi = lax.broadcasted_iota(jnp.int32, (rows4, 2 * ATTN_BLOCK), 0) & (ATTN_BLOCK - 1)
    si = lax.broadcasted_iota(jnp.int32, (rows4, 2 * ATTN_BLOCK), 1)
    valid = (si > qi) & (si <= qi + ATTN_BLOCK) & ((si >= ATTN_BLOCK) | (blk > 0))
    return half, k_slab, v_slab, valid


def _stack_heads(half, slab0, slab1):
    return jnp.concatenate([jnp.where(half[0], slab0, 0.0), jnp.where(half[1], slab0, 0.0),
                            jnp.where(half[0], slab1, 0.0), jnp.where(half[1], slab1, 0.0)], axis=0)


def _unstack_heads(half, x4):
    B = ATTN_BLOCK
    return (jnp.where(half[0], x4[0:B], x4[B:2 * B]), jnp.where(half[0], x4[2 * B:3 * B], x4[3 * B:4 * B]))


def _attn_group(j, qkvg, half, k_slab, v_slab, valid, bias_ref, sinks_ref):
    r, aj = j // 2, j % 2
    pick = (lambda a, b: jnp.where(half[0], a, b)) if aj == 0 else (lambda a, b: jnp.where(half[0], b, a))
    kb = pick(*k_slab[r]).astype(BF16)
    vb = pick(*v_slab[r]).astype(BF16)
    q4 = _stack_heads(half, qkvg[:, 256 * j:256 * j + 128], qkvg[:, 256 * j + 128:256 * j + 256]).astype(BF16)
    bias4 = bias_ref[ATTN_GROUP * j:ATTN_GROUP * (j + 1)].reshape(valid.shape)
    yield
    s = _dot(q4, kb, NT) * (ATTN_HEAD_DIM ** -0.5) + bias4
    s = jnp.where(valid, s, NEG_BIG)
    rowblk = lax.broadcasted_iota(jnp.int32, (valid.shape[0], 1), 0) // ATTN_BLOCK
    sink = jnp.full((valid.shape[0], 1), sinks_ref[0, ATTN_GROUP * j], F32)
    for b in range(1, ATTN_GROUP):
        sink = jnp.where(rowblk == b, sinks_ref[0, ATTN_GROUP * j + b], sink)
    m = jnp.maximum(jnp.max(s, axis=-1, keepdims=True), sink)
    e = jnp.exp(s - m)
    es = jnp.exp(sink - m)
    inv = 1.0 / (jnp.sum(e, axis=-1, keepdims=True) + es)
    pn = e * inv
    yield
    o4 = _dot(pn.astype(BF16), vb)
    return dict(r=r, aj=aj, kb=kb, vb=vb, q4=q4, pn=pn, psink=es * inv, o4=o4)


def _attn_specs(nb):
    row = lambda b, i: b * nb + i
    return [
        pl.BlockSpec((ATTN_BLOCK, C_HQ), lambda b, i: (row(b, i), 0)),
        pl.BlockSpec((ATTN_BLOCK, 512), lambda b, i: (row(b, jnp.maximum(i - 1, 0)), 2)),
        pl.BlockSpec((ATTN_HEADS, ATTN_BLOCK, 2 * ATTN_BLOCK), lambda b, i: (0, 0, 0)),
        pl.BlockSpec(memory_space=pltpu.SMEM),
    ]


def _attn_fwd(proj, bias, sinks, nseq, nb):
    T = proj.shape[0]

    def body(qkvg_ref, kvp_ref, bias_ref, sinks_ref, ya_ref):
        qkvg = qkvg_ref[...].astype(F32)
        half, k_slab, v_slab, valid = _attn_common(qkvg, kvp_ref[...].astype(F32), pl.program_id(1))
        groups = _interleave([_attn_group(j, qkvg, half, k_slab, v_slab, valid, bias_ref, sinks_ref)
                              for j in range(ATTN_HEADS // ATTN_GROUP)])
        slabs = []
        for grp in groups:
            slabs += _unstack_heads(half, grp["o4"])
        o_all = jnp.concatenate(slabs, axis=1)
        g = qkvg[:, C_AG:C_HQ]
        ya_ref[...] = (o_all * (g * _sigmoid(g))).astype(BF16)

    return pl.pallas_call(
        body, name="attn_fwd",
        grid=(nseq, nb),
        in_specs=_attn_specs(nb),
        out_specs=pl.BlockSpec((ATTN_BLOCK, 1024), lambda b, i: (b * nb + i, 0)),
        out_shape=jax.ShapeDtypeStruct((T, 1024), BF16),
        compiler_params=_params(("arbitrary", "arbitrary")),
    )(proj, proj, bias, sinks)


def _attn_bwd(proj, bias, sinks, d_ya, nseq, nb):
    T = proj.shape[0]
    S = nb * ATTN_BLOCK
    scale = ATTN_HEAD_DIM ** -0.5

    def body(qkvg_ref, kvp_ref, bias_ref, sinks_ref, dya_ref, dq_ref, dkv_ref, dg_ref, dbias_ref, dsinks_ref):
        b, i = pl.program_id(0), pl.program_id(1)
        first = (b == 0) & (i == 0)

        @pl.when(first)
        def _():
            dbias_ref[...] = jnp.zeros(dbias_ref.shape, F32)
            for h in range(ATTN_HEADS):
                dsinks_ref[0, h] = 0.0

        qkvg = qkvg_ref[...].astype(F32)
        half, k_slab, v_slab, valid = _attn_common(qkvg, kvp_ref[...].astype(F32), i)
        g = qkvg[:, C_AG:C_HQ]
        sg = _sigmoid(g)
        silu_g = g * sg
        dya = dya_ref[...].astype(F32)
        do_all = dya * silu_g
        dq_slabs, o_slabs = [], []
        dk_slab, dv_slab = [None, None], [None, None]
        B = ATTN_BLOCK

        def fold(x, aj):
            return jnp.where(half[aj], x + pltpu.roll(x, ATTN_HEAD_DIM, 1), 0.0)

        def group_bwd(j):
            grp = yield from _attn_group(j, qkvg, half, k_slab, v_slab, valid, bias_ref, sinks_ref)
            pn = grp["pn"]
            do4 = _stack_heads(half, do_all[:, 256 * j:256 * j + 128], do_all[:, 256 * j + 128:256 * j + 256])
            do4b = do4.astype(BF16)
            yield
            dp = _dot(do4b, grp["vb"], NT)
            delta = jnp.sum(do4 * grp["o4"], axis=-1, keepdims=True)
            ds = pn * (dp - delta)
            sink_term = grp["psink"] * delta
            for b4 in range(ATTN_GROUP):
                dsinks_ref[0, ATTN_GROUP * j + b4] += -jnp.sum(sink_term[B * b4:B * (b4 + 1)])
            dbias_ref[ATTN_GROUP * j:ATTN_GROUP * (j + 1)] += ds.reshape(ATTN_GROUP, B, 2 * B)
            dsb = ds.astype(BF16)
            yield
            dq4 = _dot(dsb, grp["kb"]) * scale
            dk_j = fold(_dot(dsb, grp["q4"], TN) * scale, grp["aj"])
            dv_j = fold(_dot(pn.astype(BF16), do4b, TN), grp["aj"])
            return dict(r=grp["r"], dq=_unstack_heads(half, dq4), o=_unstack_heads(half, grp["o4"]), dk=dk_j, dv=dv_j)

        for res in _interleave([group_bwd(j) for j in range(ATTN_HEADS // ATTN_GROUP)]):
            r = res["r"]
            dq_slabs += res["dq"]
            o_slabs += res["o"]
            dk_slab[r] = res["dk"] if dk_slab[r] is None else dk_slab[r] + res["dk"]
            dv_slab[r] = res["dv"] if dv_slab[r] is None else dv_slab[r] + res["dv"]

        dq_ref[...] = jnp.concatenate(dq_slabs, axis=1).astype(BF16)
        o_all = jnp.concatenate(o_slabs, axis=1)
        dg_ref[...] = (dya * o_all * (sg * (1.0 + g * (1.0 - sg)))).astype(BF16)

        dkv = jnp.concatenate(dk_slab + dv_slab, axis=1)
        cur = pl.multiple_of(i * ATTN_BLOCK, ATTN_BLOCK)
        dkv_ref[pl.ds(cur, ATTN_BLOCK), :] = dkv[ATTN_BLOCK:].astype(BF16)

        @pl.when(i > 0)
        def _():
            prev = pl.multiple_of((i - 1) * ATTN_BLOCK, ATTN_BLOCK)
            old = dkv_ref[pl.ds(prev, ATTN_BLOCK), :].astype(F32)
            dkv_ref[pl.ds(prev, ATTN_BLOCK), :] = (old + dkv[:ATTN_BLOCK]).astype(BF16)

    row_spec = lambda w: pl.BlockSpec((ATTN_BLOCK, w), lambda b, i: (b * nb + i, 0))
    return pl.pallas_call(
        body, name="attn_bwd",
        grid=(nseq, nb),
        in_specs=_attn_specs(nb) + [row_spec(1024)],
        out_specs=[row_spec(1024),
                   pl.BlockSpec((S, 512), lambda b, i: (b, 0)),
                   row_spec(1024),
                   pl.BlockSpec((ATTN_HEADS, ATTN_BLOCK, 2 * ATTN_BLOCK), lambda b, i: (0, 0, 0)),
                   pl.BlockSpec(memory_space=pltpu.SMEM)],
        out_shape=[jax.ShapeDtypeStruct((T, 1024), BF16),
                   jax.ShapeDtypeStruct((T, 512), BF16),
                   jax.ShapeDtypeStruct((T, 1024), BF16),
                   jax.ShapeDtypeStruct((ATTN_HEADS, ATTN_BLOCK, 2 * ATTN_BLOCK), F32),
                   jax.ShapeDtypeStruct((1, ATTN_HEADS), F32)],
        compiler_params=_params(("arbitrary", "arbitrary")),
    )(proj, proj, bias, sinks, d_ya)


def _split3(x):
    hi = x.astype(BF16)
    r1 = x - hi.astype(F32)
    mid = r1.astype(BF16)
    lo = (r1 - mid.astype(F32)).astype(BF16)
    return jnp.concatenate([hi, mid, lo], axis=1)


def _tri_sum(tri, x):
    y = _dot(tri, _split3(x))
    return y[:, :128] + y[:, 128:256] + y[:, 256:]


def _interleave(stages):
    results = [None] * len(stages)
    live = list(range(len(stages)))
    while live:
        still = []
        for idx in live:
            try:
                next(stages[idx])
                still.append(idx)
            except StopIteration as done:
                results[idx] = done.value
        live = still
    return results


def _hgrn_chunk(hq, hf, hi, lb):
    C = HGRN_CHUNK
    t = lax.broadcasted_iota(jnp.int32, (C, C), 0)
    s = lax.broadcasted_iota(jnp.int32, (C, C), 1)
    causal = s <= t
    sf = _sigmoid(hf)
    f = lb + (1.0 - lb) * sf
    lf = jnp.log(f)
    yield
    G = _tri_sum(causal.astype(BF16), lf)
    sq = _sigmoid(hq)
    qs = hq * sq
    k = 1.0 - f
    rowblk = lax.broadcasted_iota(jnp.int32, (C, 1), 0) // HGRN_SUB
    qt, kt, eq, ek = [], [], [], []
    for i in range(C // HGRN_SUB):
        lo = HGRN_SUB * i
        ref = G[lo + HGRN_SUB // 2:lo + HGRN_SUB // 2 + 1, :]
        eq_i = jnp.exp(G[lo:lo + HGRN_SUB] - ref)
        ek_i = jnp.exp(jnp.where(rowblk <= i, ref - G, 0.0))
        eq.append(eq_i)
        ek.append(ek_i)
        qt.append((qs[lo:lo + HGRN_SUB] * eq_i).astype(BF16))
        kt.append((k * ek_i).astype(BF16))
    yield
    A = jnp.concatenate([_dot(qt[i], kt[i], NT) for i in range(C // HGRN_SUB)], axis=0)
    A = jnp.where(causal, A, 0.0)
    glast = G[C - 1:C, :]
    eG = jnp.exp(G)
    edec = jnp.exp(glast - G)
    return dict(causal=causal, sf=sf, f=f, G=G, sq=sq, qs=qs, k=k, qt=qt, kt=kt, eq=eq, ek=ek, A=A,
                glast=glast, eG=eG, edec=edec, qhat=qs * eG, kdec=k * edec, v=hi)


def _hgrn_specs(nseq, ng, rows, reverse):
    W = HGRN_PAR
    nblk = W * HGRN_DIM // HGRN_COLS
    gi = (lambda g: ng - 1 - g) if reverse else (lambda g: g)

    def cols(c0):
        return [pl.BlockSpec((rows, HGRN_COLS),
                             lambda h, b, g, q=q: (b * ng + gi(g), c0 // HGRN_COLS + h * nblk + q)) for q in range(nblk)]

    return gi, (cols(C_HQ) + cols(C_HF) + cols(C_HI) + cols(C_HG)
                + [pl.BlockSpec((2, W * HGRN_DIM), lambda h, b, g: (0, h)),
                   pl.BlockSpec((W, 1, HGRN_DIM), lambda h, b, g: (h, 0, 0))])


def _hgrn_operands(refs):
    nblk = HGRN_PAR * HGRN_DIM // HGRN_COLS
    per = HGRN_COLS // HGRN_DIM

    def reader(group):
        def read(rs, w):
            lo = HGRN_DIM * (w % per)
            return group[w // per][rs, lo:lo + HGRN_DIM].astype(F32)
        return read

    readers = [reader(refs[nblk * a:nblk * (a + 1)]) for a in range(4)]
    return readers, refs[4 * nblk], refs[4 * nblk + 1], refs[4 * nblk + 2:]


def _hgrn_fwd(proj, lb_logits, gain3, nseq, S, shards):
    T = proj.shape[0]
    W = HGRN_PAR
    nc = S // HGRN_CHUNK
    cg = min(8, nc)
    ng = nc // cg
    rows = cg * HGRN_CHUNK
    n = len(shards)
    nh = HGRN_HEADS // W

    def body(*all_refs):
        (hq, hf, hi, hg), lbl_ref, gain_ref, refs = _hgrn_operands(all_refs)
        ins = refs[:n]
        o_ref, yh_ref, st_ref = refs[n:n + 3]
        outs = refs[n + 3:2 * n + 3]
        state, send_sems, recv_sems, local_sems = refs[2 * n + 3:]
        step = (pl.program_id(0), pl.program_id(1), pl.program_id(2))
        ops = _GatherOps(ins, outs, send_sems, recv_sems, local_sems)

        @pl.when((step[0] == 0) & (step[1] == 0) & (step[2] == 0))
        def _():
            for a in range(n):
                ops.local(a).start()
                ops.to_sibling(a).start()
                for q in range(3):
                    ops.to_chip(a, q).start()

        @pl.when(pl.program_id(2) == 0)
        def _():
            state[...] = jnp.zeros(state.shape, F32)

        lb_all = _sigmoid(lbl_ref[0:1, :] - lbl_ref[1:2, :])

        def head(w, c):
            rs = pl.ds(pl.multiple_of(c * HGRN_CHUNK, HGRN_CHUNK), HGRN_CHUNK)
            ls = slice(HGRN_DIM * w, HGRN_DIM * (w + 1))
            ch = yield from _hgrn_chunk(hq(rs, w), hf(rs, w), hi(rs, w), lb_all[:, ls])
            st = state[w]
            st_ref[w, c] = st
            vb = ch["v"].astype(BF16)
            yield
            o = _dot(ch["qhat"].astype(BF16), st.astype(BF16), NT) + _dot(ch["A"].astype(BF16), vb)
            state[w] = st * jnp.exp(ch["glast"]) + _dot(vb, ch["kdec"].astype(BF16), TN)
            o_ref[rs, ls] = o
            r = lax.rsqrt(jnp.mean(o * o, axis=-1, keepdims=True) + NORM_EPS)
            gate = hg(rs, w)
            yh_ref[rs, ls] = (o * r * gain_ref[w] * (gate * _sigmoid(gate))).astype(BF16)

        def chunk(c, carry):
            _interleave([head(w, c) for w in range(W)])
            return carry

        lax.fori_loop(0, cg, chunk, 0)

        @pl.when((step[0] == nh - 1) & (step[1] == nseq - 1) & (step[2] == ng - 1))
        def _():
            for a in range(n):
                ops.local(a).wait()
                ops.from_sibling(a).wait_recv()
                for q in range(3):
                    ops.from_chip(a, q).wait_recv()
                    ops.forward(a, q).start()
            for a in range(n):
                for q in range(3):
                    ops.forwarded(a, q).wait_recv()
                for cp in ops.sends(a):
                    cp.wait_send()

    _, in_specs = _hgrn_specs(nseq, ng, rows, False)
    out_row = lambda: pl.BlockSpec((rows, W * HGRN_DIM), lambda h, b, g: (b * ng + g, h))
    anyspec = lambda: pl.BlockSpec(memory_space=pl.ANY)
    return pl.pallas_call(
        body, name="hgrn_fwd",
        grid=(nh, nseq, ng),
        in_specs=in_specs + [anyspec() for _ in shards],
        out_specs=[out_row(), out_row(),
                   pl.BlockSpec((None, W, cg, HGRN_DIM, HGRN_DIM), lambda h, b, g: (b, h, g, 0, 0))]
                  + [anyspec() for _ in shards],
        out_shape=[jax.ShapeDtypeStruct((T, 1024), F32),
                   jax.ShapeDtypeStruct((T, 1024), BF16),
                   jax.ShapeDtypeStruct((nseq, HGRN_HEADS, nc, HGRN_DIM, HGRN_DIM), F32)]
                  + [jax.ShapeDtypeStruct((N_DEV * s.shape[0], s.shape[1]), s.dtype) for s in shards],
        scratch_shapes=[pltpu.VMEM((W, HGRN_DIM, HGRN_DIM), F32),
                        pltpu.SemaphoreType.DMA((n, _GATHER_SEMS)), pltpu.SemaphoreType.DMA((n, _GATHER_SEMS)),
                        pltpu.SemaphoreType.DMA((n,))],
        compiler_params=_params(("arbitrary", "arbitrary", "arbitrary")),
    )(*([proj] * (4 * W * HGRN_DIM // HGRN_COLS)), lb_logits, gain3, *shards)


def _hgrn_bwd(proj, lb_logits, gain3, o, d_yh, states, nseq, S, chip_sums):
    T = proj.shape[0]
    W = HGRN_PAR
    n = len(chip_sums)
    nh = HGRN_HEADS // W
    nc = S // HGRN_CHUNK
    cg = min(8, nc)
    ng = nc // cg
    rows = cg * HGRN_CHUNK
    C = HGRN_CHUNK
    nsub = C // HGRN_SUB

    def body(*all_refs):
        (hq_of, hf_of, hi_of, hg_of), lbl_ref, gain_ref, refs = _hgrn_operands(all_refs)
        o_ref, dyh_ref, st_ref = refs[:3]
        sums_in = refs[3:3 + n]
        dhq_ref, dhf_ref, dhi_ref, dhg_ref, dgain_ref, dlbl_ref = refs[3 + n:9 + n]
        sums_out = refs[9 + n:9 + 2 * n]
        dstate, dlb_acc, send_sems, recv_sems, local_sems = refs[9 + 2 * n:]
        h, b, g = pl.program_id(0), pl.program_id(1), pl.program_id(2)
        exchange = _ChipExchange(sums_in, sums_out, send_sems, recv_sems, local_sems)

        @pl.when((h == 0) & (b == 0) & (g == 0))
        def _():
            exchange.start()

        @pl.when(g == 0)
        def _():
            dstate[...] = jnp.zeros(dstate.shape, F32)

        @pl.when((b == 0) & (g == 0))
        def _():
            dgain_ref[...] = jnp.zeros(dgain_ref.shape, F32)
            dlb_acc[...] = jnp.zeros(dlb_acc.shape, F32)

        lb_all = _sigmoid(lbl_ref[0:1, :] - lbl_ref[1:2, :])

        anti = (lax.broadcasted_iota(jnp.int32, (C, C), 1) >= lax.broadcasted_iota(jnp.int32, (C, C), 0)).astype(BF16)
        last_row = lax.broadcasted_iota(jnp.int32, (C, 1), 0) == C - 1

        def head_load(w, c):
            rs = pl.ds(pl.multiple_of(c * C, C), C)
            ls = slice(HGRN_DIM * w, HGRN_DIM * (w + 1))
            return dict(hq=hq_of(rs, w), hf=hf_of(rs, w), hi=hi_of(rs, w), hg=hg_of(rs, w), lb=lb_all[:, ls],
                        gain=gain_ref[w], ov=o_ref[rs, ls], dyh=dyh_ref[rs, ls].astype(F32),
                        st=st_ref[w, c], dst=dstate[w])

        def head_math(v):
            lb, gain, hq, hg = v["lb"], v["gain"], v["hq"], v["hg"]
            ch = yield from _hgrn_chunk(hq, v["hf"], v["hi"], lb)
            ov, dyh = v["ov"], v["dyh"]
            r = lax.rsqrt(jnp.mean(ov * ov, axis=-1, keepdims=True) + NORM_EPS)
            on = ov * r
            sg = _sigmoid(hg)
            doh = dyh * (hg * sg)
            out = dict(dhg=(dyh * on * gain * (sg * (1.0 + hg * (1.0 - sg)))).astype(BF16),
                       dgain=jnp.sum(doh * on, axis=0, keepdims=True))
            don = doh * gain
            do = r * (don - on * jnp.mean(don * on, axis=-1, keepdims=True))
            dob = do.astype(BF16)
            st, dst = v["st"], v["dst"]
            stb, dstb = st.astype(BF16), dst.astype(BF16)
            vb = ch["v"].astype(BF16)
            qhatb = ch["qhat"].astype(BF16)
            eglast = jnp.exp(ch["glast"])
            yield
            dqhat = _dot(dob, stb)
            dkdec = _dot(vb, dstb)
            dv = _dot(ch["kdec"].astype(BF16), dstb, NT)
            deg = jnp.sum(dst * st, axis=0, keepdims=True)
            out["dstate"] = dst * eglast + _dot(dob, qhatb, TN)
            dA = jnp.where(ch["causal"], _dot(dob, vb, NT), 0.0)
            dv = dv + _dot(ch["A"].astype(BF16), dob, TN)
            dAb = dA.astype(BF16)
            yield
            dqs_parts, dgq_parts = [], []
            dk_intra, dgk = None, None
            for i in range(nsub):
                dA_i = dAb[HGRN_SUB * i:HGRN_SUB * (i + 1)]
                dqt = _dot(dA_i, ch["kt"][i])
                dkt = _dot(dA_i, ch["qt"][i], TN)
                dqs_parts.append(dqt * ch["eq"][i])
                dgq_parts.append(dqt * ch["qt"][i].astype(F32))
                dk_i = dkt * ch["ek"][i]
                dgk_i = dkt * ch["kt"][i].astype(F32)
                dk_intra = dk_i if dk_intra is None else dk_intra + dk_i
                dgk = dgk_i if dgk is None else dgk + dgk_i
            dqs_inter = dqhat * ch["eG"]
            dk_state = dkdec * ch["edec"]
            dqs = jnp.concatenate(dqs_parts, axis=0) + dqs_inter
            dk = dk_intra + dk_state
            dG = jnp.concatenate(dgq_parts, axis=0) - dgk + ch["qs"] * dqs_inter - ch["k"] * dk_state
            tail = jnp.sum(dkdec * ch["kdec"], axis=0, keepdims=True) + deg * eglast
            dG = dG + jnp.where(last_row, tail, 0.0)
            yield
            dlf = _tri_sum(anti, dG)
            df = dlf / ch["f"] - dk
            sf, sq = ch["sf"], ch["sq"]
            out["dhf"] = (df * (1.0 - lb) * sf * (1.0 - sf)).astype(BF16)
            out["dlb"] = jnp.sum(df * (1.0 - sf), axis=0, keepdims=True)
            out["dhq"] = (dqs * (sq * (1.0 + hq * (1.0 - sq)))).astype(BF16)
            out["dhi"] = dv.astype(BF16)
            return out

        def head_store(w, c, out):
            rs = pl.ds(pl.multiple_of(c * C, C), C)
            ls = slice(HGRN_DIM * w, HGRN_DIM * (w + 1))
            dhg_ref[rs, ls] = out["dhg"]
            dhf_ref[rs, ls] = out["dhf"]
            dhq_ref[rs, ls] = out["dhq"]
            dhi_ref[rs, ls] = out["dhi"]
            dstate[w] = out["dstate"]
            dgain_ref[w] += out["dgain"]
            dlb_acc[:, ls] += out["dlb"]

        def chunk(cc, carry):
            c = cg - 1 - cc
            outs = _interleave([head_math(v) for v in [head_load(w, c) for w in range(W)]])
            for w in range(W):
                head_store(w, c, outs[w])
            return carry

        lax.fori_loop(0, cg, chunk, 0)

        @pl.when((b == nseq - 1) & (g == ng - 1))
        def _():
            dl0 = dlb_acc[...] * lb_all * (1.0 - lb_all)
            dlbl_ref[0:1, :] = dl0
            dlbl_ref[1:2, :] = -dl0

        @pl.when((h == nh - 1) & (b == nseq - 1) & (g == ng - 1))
        def _():
            exchange.wait()

    gi, in_specs = _hgrn_specs(nseq, ng, rows, True)
    row = lambda: pl.BlockSpec((rows, W * HGRN_DIM), lambda h, b, g: (b * ng + gi(g), h))
    anyspec = lambda: pl.BlockSpec(memory_space=pl.ANY)
    return pl.pallas_call(
        body, name="hgrn_bwd",
        grid=(nh, nseq, ng),
        in_specs=in_specs + [row(), row(),
                             pl.BlockSpec((None, W, cg, HGRN_DIM, HGRN_DIM), lambda h, b, g: (b, h, gi(g), 0, 0))]
                 + [anyspec() for _ in chip_sums],
        out_specs=[row(), row(), row(), row(),
                   pl.BlockSpec((W, 1, HGRN_DIM), lambda h, b, g: (h, 0, 0)),
                   pl.BlockSpec((2, W * HGRN_DIM), lambda h, b, g: (0, h))] + [anyspec() for _ in chip_sums],
        out_shape=[jax.ShapeDtypeStruct((T, 1024), BF16)] * 4
                  + [jax.ShapeDtypeStruct((HGRN_HEADS, 1, HGRN_DIM), F32),
                     jax.ShapeDtypeStruct((2, HGRN_HEADS * HGRN_DIM), F32)]
                  + [jax.ShapeDtypeStruct(s.shape, s.dtype) for s in chip_sums],
        scratch_shapes=[pltpu.VMEM((W, HGRN_DIM, HGRN_DIM), F32), pltpu.VMEM((1, W * HGRN_DIM), F32)]
                       + _ChipExchange.scratch(n),
        compiler_params=_params(("arbitrary", "arbitrary", "arbitrary")),
    )(*([proj] * (4 * W * HGRN_DIM // HGRN_COLS)), lb_logits, gain3, o, d_yh, states, *chip_sums)


def _gate_specs(tm):
    spec = lambda c0, q: pl.BlockSpec((tm, 512), lambda i: (i, c0 // 512 + q))
    return [spec(C_GA, q) for q in range(4)] + [spec(C_GH, q) for q in range(4)]


def _gates(refs):
    ga = jnp.concatenate([r[...] for r in refs[0:4]], axis=1).astype(F32)
    gh = jnp.concatenate([r[...] for r in refs[4:8]], axis=1).astype(F32)
    return ga, gh


def _branch_merge(proj, ya, yh, wa_t, wh_t):
    T = proj.shape[0]
    tm = min(512, T)

    def body(*refs):
        ya_ref, yh_ref, wa_ref, wh_ref, merged_ref, ua_ref, uh_ref = refs[8:]
        ga, gh = _gates(refs)
        ua = _dot(ya_ref[...], wa_ref[...], NT)
        uh = _dot(yh_ref[...], wh_ref[...], NT)
        merged_ref[...] = (_sigmoid(ga) * ua + _sigmoid(gh) * uh).astype(BF16)
        ua_ref[...] = ua.astype(BF16)
        uh_ref[...] = uh.astype(BF16)

    rowb = lambda w: pl.BlockSpec((tm, w), lambda i: (i, 0))
    full = lambda a: pl.BlockSpec(a.shape, lambda i: (0, 0))
    return pl.pallas_call(
        body, name="branch_merge",
        grid=(T // tm,),
        in_specs=_gate_specs(tm) + [rowb(1024), rowb(1024), full(wa_t), full(wh_t)],
        out_specs=[rowb(D_MODEL)] * 3,
        out_shape=[jax.ShapeDtypeStruct((T, D_MODEL), BF16)] * 3,
        compiler_params=_params(("arbitrary",)),
    )(*([proj] * 8), ya, yh, wa_t, wh_t)


def _out_norm_loss(merged, x2, tgt2, gpost, wout):
    T = merged.shape[0]
    tm = min(256, T)

    def body(m_ref, x_ref, t_ref, gpost_ref, wo_ref, dy_ref, dm_ref, dout_ref, loss_ref, dgpost_ref):
        @pl.when(pl.program_id(0) == 0)
        def _():
            loss_ref[...] = jnp.zeros(loss_ref.shape, F32)
            dgpost_ref[...] = jnp.zeros(dgpost_ref.shape, F32)

        y = _dot(m_ref[...], wo_ref[...])
        r2 = lax.rsqrt(jnp.mean(y * y, axis=-1, keepdims=True) + NORM_EPS)
        yn = y * r2
        gpost = gpost_ref[...]
        err = x_ref[...] + yn * gpost - t_ref[...]
        loss_ref[...] += jnp.sum(err * err, axis=0, keepdims=True)
        dout = err * (1.0 / D_MODEL)
        dout_ref[...] = dout
        dgpost_ref[...] += jnp.sum(dout * yn, axis=0, keepdims=True)
        dyn = dout * gpost
        dy = (r2 * (dyn - yn * jnp.mean(dyn * yn, axis=-1, keepdims=True))).astype(BF16)
        dy_ref[...] = dy
        dm_ref[...] = _dot(dy, wo_ref[...], NT).astype(BF16)

    rowb = lambda: pl.BlockSpec((tm, D_MODEL), lambda i: (i, 0))
    vec = lambda: pl.BlockSpec((1, D_MODEL), lambda i: (0, 0))
    return pl.pallas_call(
        body, name="out_norm_loss",
        grid=(T // tm,),
        in_specs=[rowb(), rowb(), rowb(), vec(), pl.BlockSpec(wout.shape, lambda i: (0, 0))],
        out_specs=[rowb(), rowb(), rowb(), vec(), vec()],
        out_shape=[jax.ShapeDtypeStruct((T, D_MODEL), BF16)] * 2
                  + [jax.ShapeDtypeStruct((T, D_MODEL), F32)] + [jax.ShapeDtypeStruct((1, D_MODEL), F32)] * 2,
        compiler_params=_params(("arbitrary",)),
    )(merged, x2, tgt2, gpost, wout)


def _branch_bwd(proj, dm, ua, uh, wa_t, wh_t):
    T = proj.shape[0]
    tm = min(256, T)

    def body(*refs):
        (dm_ref, ua_ref, uh_ref, wa_ref, wh_ref,
         dua_ref, duh_ref, dga_ref, dgh_ref, dya_ref, dyh_ref) = refs[8:]
        ga, gh = _gates(refs)
        sa, sh = _sigmoid(ga), _sigmoid(gh)
        dm = dm_ref[...].astype(F32)
        dua = (dm * sa).astype(BF16)
        duh = (dm * sh).astype(BF16)
        dua_ref[...] = dua
        duh_ref[...] = duh
        dga_ref[...] = (dm * ua_ref[...].astype(F32) * (sa * (1.0 - sa))).astype(BF16)
        dgh_ref[...] = (dm * uh_ref[...].astype(F32) * (sh * (1.0 - sh))).astype(BF16)
        dya_ref[...] = _dot(dua, wa_ref[...]).astype(BF16)
        dyh_ref[...] = _dot(duh, wh_ref[...]).astype(BF16)

    rowb = lambda w: pl.BlockSpec((tm, w), lambda i: (i, 0))
    full = lambda a: pl.BlockSpec(a.shape, lambda i: (0, 0))
    return pl.pallas_call(
        body, name="branch_bwd",
        grid=(T // tm,),
        in_specs=_gate_specs(tm) + [rowb(D_MODEL)] * 3 + [full(wa_t), full(wh_t)],
        out_specs=[rowb(D_MODEL)] * 4 + [rowb(1024)] * 2,
        out_shape=[jax.ShapeDtypeStruct((T, D_MODEL), BF16)] * 4 + [jax.ShapeDtypeStruct((T, 1024), BF16)] * 2,
        compiler_params=_params(("arbitrary",)),
    )(*([proj] * 8), dm, ua, uh, wa_t, wh_t)


def _tn_matmul(L, R, bm, bn, name):
    T, M = L.shape
    N = R.shape[1]

    def body(l_ref, r_ref, out_ref):
        out_ref[...] = _dot(l_ref[...], r_ref[...], TN).astype(BF16)

    return pl.pallas_call(
        body, name=name,
        grid=(N // bn, M // bm),
        in_specs=[pl.BlockSpec((T, bm), lambda j, i: (0, i)),
                  pl.BlockSpec((T, bn), lambda j, i: (0, j))],
        out_specs=pl.BlockSpec((bm, bn), lambda j, i: (i, j)),
        out_shape=jax.ShapeDtypeStruct((M, N), BF16),
        compiler_params=_params(("arbitrary", "arbitrary")),
    )(L, R)


def _dh_prenorm_bwd(dproj, wt_in, x2, dout, gpre, chip_sums):
    T = x2.shape[0]
    tm = min(1024, T)
    ne = 2
    te = tm // ne
    tk = 768
    nk = IN_WIDTH // tk
    nt = T // tm
    n = len(chip_sums)

    def body(dp_ref, w_ref, x_ref, dout_ref, g_ref, *refs):
        ins = refs[:n]
        gx_ref, dg_ref = refs[n], refs[n + 1]
        outs = refs[n + 2:2 * n + 2]
        acc, send_sems, recv_sems, local_sems = refs[2 * n + 2:]
        i, k = pl.program_id(0), pl.program_id(1)
        exchange = _ChipExchange(ins, outs, send_sems, recv_sems, local_sems)

        @pl.when((i == 0) & (k == 0))
        def _():
            dg_ref[...] = jnp.zeros(dg_ref.shape, F32)
            exchange.start()

        @pl.when((i == nt - 1) & (k == nk + ne - 1))
        def _():
            exchange.wait()

        @pl.when(k == 0)
        def _():
            acc[...] = _dot(dp_ref[...], w_ref[...])

        @pl.when((k > 0) & (k < nk))
        def _():
            acc[...] += _dot(dp_ref[...], w_ref[...])

        @pl.when(k >= nk)
        def _():
            dh = acc[pl.ds(pl.multiple_of((k - nk) * te, te), te), :]
            x = x_ref[...]
            r = lax.rsqrt(jnp.mean(x * x, axis=-1, keepdims=True) + NORM_EPS)
            xn = x * r
            dg_ref[...] += jnp.sum(dh * xn, axis=0, keepdims=True)
            dxn = dh * g_ref[...]
            gx_ref[...] = dout_ref[...] + r * (dxn - xn * jnp.mean(dxn * xn, axis=-1, keepdims=True))

    rowb = lambda: pl.BlockSpec((te, D_MODEL), lambda i, k: (ne * i + jnp.clip(k - nk, 0, ne - 1), 0))
    vec = lambda: pl.BlockSpec((1, D_MODEL), lambda i, k: (0, 0))
    anyspec = lambda: pl.BlockSpec(memory_space=pl.ANY)
    return pl.pallas_call(
        body, name="dh_prenorm_bwd",
        grid=(nt, nk + ne),
        in_specs=[pl.BlockSpec((tm, tk), lambda i, k: (i, jnp.minimum(k, nk - 1))),
                  pl.BlockSpec((tk, D_MODEL), lambda i, k: (jnp.minimum(k, nk - 1), 0)),
                  rowb(), rowb(), vec()] + [anyspec() for _ in chip_sums],
        out_specs=[rowb(), vec()] + [anyspec() for _ in chip_sums],
        out_shape=[jax.ShapeDtypeStruct((T, D_MODEL), F32), jax.ShapeDtypeStruct((1, D_MODEL), F32)]
                  + [jax.ShapeDtypeStruct(s.shape, s.dtype) for s in chip_sums],
        scratch_shapes=[pltpu.VMEM((tm, D_MODEL), F32)] + _ChipExchange.scratch(n),
        compiler_params=_params(("arbitrary", "arbitrary")),
    )(dproj, wt_in, x2, dout, gpre, *chip_sums)


def _sum_slots(recv, br, name):
    nslot, R, C = recv.shape

    def body(r_ref, out_ref):
        acc = r_ref[0].astype(F32)
        for s in range(1, nslot):
            acc = acc + r_ref[s].astype(F32)
        out_ref[...] = acc

    return pl.pallas_call(
        body, name=name,
        grid=(R // br,),
        in_specs=[pl.BlockSpec((nslot, br, C), lambda i: (0, i, 0))],
        out_specs=pl.BlockSpec((br, C), lambda i: (i, 0)),
        out_shape=jax.ShapeDtypeStruct((R, C), F32),
        compiler_params=_params(("arbitrary",)),
    )(recv)


def _adamw_math(w, g, m, v):
    m = ADAM_B1 * m + (1.0 - ADAM_B1) * g
    v = ADAM_B2 * v + (1.0 - ADAM_B2) * (g * g)
    m_hat = m / (1.0 - ADAM_B1 ** ADAM_STEP)
    v_hat = v / (1.0 - ADAM_B2 ** ADAM_STEP)
    delta = -ADAM_LR * (m_hat / (jnp.sqrt(v_hat) + ADAM_EPS) + ADAM_WD * w)
    return delta, m, v


def _adamw(w, g, m, v, br, name):
    R, C = w.shape

    def body(w_ref, g_ref, m_ref, v_ref, d_ref, nm_ref, nv_ref):
        d_ref[...], nm_ref[...], nv_ref[...] = _adamw_math(w_ref[...], g_ref[...], m_ref[...], v_ref[...])

    spec = lambda: pl.BlockSpec((br, C), lambda i: (i, 0))
    return pl.pallas_call(
        body, name=name,
        grid=(R // br,),
        in_specs=[spec(), spec(), spec(), spec()],
        out_specs=[spec(), spec(), spec()],
        out_shape=[jax.ShapeDtypeStruct((R, C), F32)] * 3,
        compiler_params=_params(("arbitrary",)),
    )(w, g, m, v)


def _sibling_exchange(partials, name):
    n = len(partials)

    def body(*refs):
        ins, outs = refs[:n], refs[n:2 * n]
        send_sems, recv_sems = refs[2 * n:]
        x, y, c = _place()

        def copy(a, p):
            return pltpu.make_async_remote_copy(
                src_ref=ins[a].at[p, 1 - c], dst_ref=outs[a].at[p],
                send_sem=send_sems.at[a, p], recv_sem=recv_sems.at[a, p],
                device_id=(x, y, 1 - c), device_id_type=pl.DeviceIdType.MESH)

        copies = [copy(a, p) for p in range(4) for a in range(n)]
        for cp in copies:
            cp.start()
        for cp in copies:
            cp.wait()

    anyspec = lambda: pl.BlockSpec(memory_space=pl.ANY)
    return pl.pallas_call(
        body, name=name,
        in_specs=[anyspec() for _ in partials],
        out_specs=[anyspec() for _ in partials],
        out_shape=[jax.ShapeDtypeStruct((4,) + p.shape[2:], p.dtype) for p in partials],
        scratch_shapes=[pltpu.SemaphoreType.DMA((n, 4)), pltpu.SemaphoreType.DMA((n, 4))],
    )(*partials)


def _chip_sum(partial, from_sibling, br, name):
    _, _, R, C = partial.shape
    cls = lax.axis_index("c").astype(jnp.int32).reshape(1)

    def body(c_ref, mine_ref, sib_ref, out_ref):
        out_ref[...] = (mine_ref[...].astype(F32) + sib_ref[...].astype(F32)).astype(BF16)

    grid_spec = pltpu.PrefetchScalarGridSpec(
        num_scalar_prefetch=1,
        grid=(4, R // br),
        in_specs=[pl.BlockSpec((None, None, br, C), lambda p, i, c: (p, c[0], i, 0)),
                  pl.BlockSpec((None, br, C), lambda p, i, c: (p, i, 0))],
        out_specs=pl.BlockSpec((None, br, C), lambda p, i, c: (p, i, 0)),
    )
    return pl.pallas_call(
        body, name=name, grid_spec=grid_spec,
        out_shape=jax.ShapeDtypeStruct((4, R, C), BF16),
        compiler_params=_params(("arbitrary", "arbitrary")),
    )(cls, partial, from_sibling)


def _all_reduce_small(packed):
    shape = packed.shape

    def body(in_ref, out_ref, slots, send_sems, recv_sems):
        x, y, c = _place()
        my_slot = 4 * x + 2 * y + c

        def peer(k):
            return (x ^ ((k >> 2) & 1), y ^ ((k >> 1) & 1), c ^ (k & 1))

        def copy(k):
            p = peer(k)
            return pltpu.make_async_remote_copy(
                src_ref=in_ref, dst_ref=slots.at[my_slot],
                send_sem=send_sems.at[k - 1], recv_sem=recv_sems.at[k - 1],
                device_id=p, device_id_type=pl.DeviceIdType.MESH)

        def arrival(k):
            p = peer(k)
            return pltpu.make_async_remote_copy(
                src_ref=in_ref, dst_ref=slots.at[4 * p[0] + 2 * p[1] + p[2]],
                send_sem=send_sems.at[k - 1], recv_sem=recv_sems.at[k - 1],
                device_id=p, device_id_type=pl.DeviceIdType.MESH)

        sends = [copy(k) for k in range(1, N_DEV)]
        for cp in sends:
            cp.start()
        slots[my_slot] = in_ref[...]
        for k in range(1, N_DEV):
            arrival(k).wait_recv()
        for cp in sends:
            cp.wait_send()
        acc = slots[0]
        for s in range(1, N_DEV):
            acc = acc + slots[s]
        out_ref[...] = acc

    return pl.pallas_call(
        body, name="all_reduce_small",
        in_specs=[pl.BlockSpec(memory_space=pltpu.VMEM)],
        out_specs=pl.BlockSpec(memory_space=pltpu.VMEM),
        out_shape=jax.ShapeDtypeStruct(shape, F32),
        scratch_shapes=[pltpu.VMEM((N_DEV,) + shape, F32),
                        pltpu.SemaphoreType.DMA((7,)), pltpu.SemaphoreType.DMA((7,))],
    )(packed)


def _pack_small(norm_pre, norm_post, lb_logits, hgrn_norm, rel_bias, sinks, extra=None):
    tail = [hgrn_norm.reshape(1, 1024), rel_bias.reshape(1, 512), sinks.reshape(1, 16)]
    used = 1024 + 512 + 16
    if extra is not None:
        tail.append(extra.reshape(1, 1))
        used += 1
    tail.append(jnp.zeros((1, D_MODEL - used), F32))
    rows = [norm_pre.reshape(1, D_MODEL), norm_post.reshape(1, D_MODEL), lb_logits.reshape(1, D_MODEL),
            jnp.concatenate(tail, axis=1), jnp.zeros((4, D_MODEL), F32)]
    return jnp.concatenate(rows, axis=0)


def _unpack_small(p):
    return (p[0:1], p[3, 1024:1536].reshape(REL_BUCKETS, ATTN_HEADS), p[3:4, 1536:1552],
            p[2].reshape(2, 1024), p[3, 0:1024].reshape(1, HGRN_HEADS, HGRN_DIM), p[1:2])


def _local_step(nseq, S, x2, tgt2, proj, h, rel_bias, attn_sinks, lb_logits, hgrn_norm, norm_post, shards):
    nb = S // ATTN_BLOCK
    bucket = jnp.asarray(_t5_bucket_table())
    gain3 = hgrn_norm.reshape(HGRN_HEADS, 1, HGRN_DIM)

    bias = _bias_table(rel_bias, bucket)
    ya = _attn_fwd(proj, bias, attn_sinks, nseq, nb)
    o, yh, states, wout, wa_t, wh_t = _hgrn_fwd(proj, lb_logits, gain3, nseq, S, shards)
    merged, ua, uh = _branch_merge(proj, ya, yh, wa_t, wh_t)
    dy, dm, dout, loss_cols, d_gpost = _out_norm_loss(merged, x2, tgt2, norm_post, wout)
    dua, duh, dga, dgh, dya, dyh = _branch_bwd(proj, dm, ua, uh, wa_t, wh_t)

    p_out = _tn_matmul(merged, dy, 256, 1024, "dw_out")
    p_a = _tn_matmul(dua, ya, 256, 1024, "dw_branch_attn")
    p_h = _tn_matmul(duh, yh, 256, 1024, "dw_branch_hgrn")
    small_sums = _chip_sums([p_out, p_a, p_h], (128, 128, 128), ("dw_out", "dw_branch_attn", "dw_branch_hgrn"),
                            "sibling_exchange_small")

    dq, dkv, dg, dbias, d_sinks = _attn_bwd(proj, bias, attn_sinks, dya, nseq, nb)
    d_rel_bias = _bias_table_bwd(dbias, bucket)
    dhq, dhf, dhi, dhg, d_gain, d_lbl, r_out, r_a, r_h = _hgrn_bwd(
        proj, lb_logits, gain3, o, dyh, states, nseq, S, small_sums)
    dproj = jnp.concatenate([dq, dkv, dg, dhq, dhf, dhi, dhg, dga, dgh], axis=1)
    p_in = _tn_matmul(dproj, h, 768, 1024, "dw_in")
    return dproj, dout, p_in, r_out, r_a, r_h, d_gpost, d_lbl, d_gain, d_rel_bias, d_sinks, loss_cols


def _chip_sums(partials, block_rows, names, exchange_name):
    split = [p.reshape(4, 2, p.shape[0] // N_DEV, p.shape[1]) for p in partials]
    from_sibling = _sibling_exchange(split, exchange_name)
    return [_chip_sum(p, f, br, "chip_sum_" + nm) for p, f, br, nm in zip(split, from_sibling, block_rows, names)]


def kernel(x, norm_pre, w_in, rel_bias, attn_sinks, lb_logits, hgrn_norm, w_branch_attn, w_branch_hgrn, w_out, norm_post, loss_target, m_norm_pre, m_w_in, m_rel_bias, m_attn_sinks, m_lb_logits, m_hgrn_norm, m_w_branch_attn, m_w_branch_hgrn, m_w_out, m_norm_post, v_norm_pre, v_w_in, v_rel_bias, v_attn_sinks, v_lb_logits, v_hgrn_norm, v_w_branch_attn, v_w_branch_hgrn, v_w_out, v_norm_post):
    nseq, S, _ = x.shape
    T = nseq * S
    x2 = x.reshape(T, D_MODEL)
    tgt2 = loss_target.reshape(T, D_MODEL)

    h = _prenorm(x2, norm_pre)
    proj, wt_in = _gather_inproj(h, w_in[0].T.astype(BF16))
    shards = [w_out[0].astype(BF16), w_branch_attn[0].T.astype(BF16), w_branch_hgrn[0].T.astype(BF16)]

    (dproj, dout, p_in, r_out, r_a, r_h, d_gpost, d_lbl, d_gain, d_rel_bias, d_sinks, loss_cols) = _local_step(
        nseq, S, x2, tgt2, proj, h, rel_bias, attn_sinks, lb_logits, hgrn_norm, norm_post, shards)

    in_sums = _chip_sums([p_in], (192,), ("dw_in",), "sibling_exchange_w_in")
    grad_x2, d_gpre, r_in = _dh_prenorm_bwd(dproj, wt_in, x2, dout, norm_pre, in_sums)
    g_in = _sum_slots(r_in, 192, "sum_dw_in").T
    g_out = _sum_slots(r_out, 128, "sum_dw_out")
    g_a = _sum_slots(r_a, 128, "sum_dw_branch_attn").T
    g_h = _sum_slots(r_h, 128, "sum_dw_branch_hgrn").T

    loss_part = 0.5 / D_MODEL * jnp.sum(loss_cols)
    packed = _pack_small(d_gpre, d_gpost, d_lbl, d_gain, d_rel_bias, d_sinks, extra=loss_part)
    total = _all_reduce_small(packed)
    loss = total[3, 1024 + 512 + 16]
    sm_w = _pack_small(norm_pre, norm_post, lb_logits, hgrn_norm, rel_bias, attn_sinks)
    sm_m = _pack_small(m_norm_pre, m_norm_post, m_lb_logits, m_hgrn_norm, m_rel_bias, m_attn_sinks)
    sm_v = _pack_small(v_norm_pre, v_norm_post, v_lb_logits, v_hgrn_norm, v_rel_bias, v_attn_sinks)
    sm_d, sm_nm, sm_nv = _adamw(sm_w, total, sm_m, sm_v, 8, "adamw_small")

    d_in, nm_in, nv_in = _adamw(w_in[0], g_in, m_w_in[0], v_w_in[0], 256, "adamw_w_in")
    d_out, nm_out, nv_out = _adamw(w_out[0], g_out, m_w_out[0], v_w_out[0], 128, "adamw_w_out")
    d_a, nm_a, nv_a = _adamw(w_branch_attn[0], g_a, m_w_branch_attn[0], v_w_branch_attn[0], 256, "adamw_w_branch_attn")
    d_h, nm_h, nv_h = _adamw(w_branch_hgrn[0], g_h, m_w_branch_hgrn[0], v_w_branch_hgrn[0], 256, "adamw_w_branch_hgrn")

    def group(small, big_in, big_a, big_h, big_out):
        npre, rb, sk, lbl, hn, npost = _unpack_small(small)
        return (npre, big_in[None], rb, sk, lbl, hn, big_a[None], big_h[None], big_out[None], npost)

    return (loss, grad_x2.reshape(nseq, S, D_MODEL),
            *group(total, g_in, g_a, g_h, g_out),
            *group(sm_d, d_in, d_a, d_h, d_out),
            *group(sm_nm, nm_in, nm_a, nm_h, nm_out),
            *group(sm_nv, nv_in, nv_a, nv_h, nv_out))
```

```python
import functools
import math

import numpy as np
import jax
import jax.numpy as jnp
from jax import lax
from jax.experimental import pallas as pl
from jax.experimental.pallas import tpu as pltpu

F32 = jnp.float32
BF16 = jnp.bfloat16

D_MODEL = 2048
ATTN_HEADS = 16
ATTN_HEAD_DIM = 64
ATTN_GROUP = 4
ATTN_BLOCK = 128
HGRN_HEADS = 8
HGRN_DIM = 128
HGRN_CHUNK = 64
HGRN_SUB = 16
HGRN_PAR = 8
HGRN_COLS = 512
REL_BUCKETS = 32
REL_MAX_DIST = 128
NORM_EPS = 1e-6
C_AQ, C_AK, C_AV, C_AG = 0, 1024, 1280, 1536
C_HQ, C_HF, C_HI, C_HG = 2560, 3584, 4608, 5632
C_GA, C_GH = 6656, 8704
IN_WIDTH = 10752
N_DEV = 8
assert all(c0 % HGRN_COLS == 0 for c0 in (C_HQ, C_HF, C_HI, C_HG)) and (HGRN_PAR * HGRN_DIM) % HGRN_COLS == 0

ADAM_LR = 0.001
ADAM_B1 = 0.9
ADAM_B2 = 0.999
ADAM_EPS = 1e-08
ADAM_WD = 0.01
ADAM_STEP = 10

VMEM_LIMIT_V7X = 56 * 1024 * 1024
NEG_BIG = -1e30

NT = (((1,), (1,)), ((), ()))
TN = (((0,), (0,)), ((), ()))
NN = (((1,), (0,)), ((), ()))


def _dot(a, b, dims=NN):
    return lax.dot_general(a, b, dims, preferred_element_type=F32)


def _params(sem=None):
    return pltpu.CompilerParams(dimension_semantics=sem, vmem_limit_bytes=VMEM_LIMIT_V7X)


def _sigmoid(x):
    return 1.0 / (1.0 + jnp.exp(-x))


def _t5_bucket_table():
    qi = np.arange(ATTN_BLOCK)[:, None]
    si = np.arange(2 * ATTN_BLOCK)[None, :]
    dist = qi + ATTN_BLOCK - si
    max_exact = REL_BUCKETS // 2
    d = np.maximum(dist, 0)
    df = np.maximum(d, 1).astype(np.float32)
    large = max_exact + (np.log(df / np.float32(max_exact)).astype(np.float32)
                         / np.float32(math.log(REL_MAX_DIST / max_exact))
                         * np.float32(REL_BUCKETS - max_exact)).astype(np.int32)
    large = np.minimum(large, REL_BUCKETS - 1)
    return np.where(d < max_exact, d, large).astype(np.int32)


def _place():
    return lax.axis_index("x"), lax.axis_index("y"), lax.axis_index("c")


class _GatherOps:
    def __init__(self, ins, outs, send_sems, recv_sems, local_sems):
        self.ins, self.outs = ins, outs
        self.send_sems, self.recv_sems, self.local_sems = send_sems, recv_sems, local_sems
        x, y, c = _place()
        self.c = c
        self.me, self.sibling = (x, y, c), (x, y, 1 - c)
        self.chips = [(1 - x, y), (x, 1 - y), (1 - x, 1 - y)]

    def _rows(self, a, dev):
        r = self.ins[a].shape[0]
        return self.outs[a].at[pl.ds((4 * dev[0] + 2 * dev[1] + dev[2]) * r, r), :]

    def _copy(self, a, k, block, to, src=None):
        return pltpu.make_async_remote_copy(
            src_ref=self._rows(a, block) if src is None else src, dst_ref=self._rows(a, block),
            send_sem=self.send_sems.at[a, k], recv_sem=self.recv_sems.at[a, k],
            device_id=to, device_id_type=pl.DeviceIdType.MESH)

    def local(self, a):
        return pltpu.make_async_copy(self.ins[a], self._rows(a, self.me), self.local_sems.at[a])

    def to_sibling(self, a):
        return self._copy(a, 0, self.me, self.sibling, src=self.ins[a])

    def to_chip(self, a, q):
        return self._copy(a, 1 + q, self.me, (*self.chips[q], self.c), src=self.ins[a])

    def forward(self, a, q):
        return self._copy(a, 4 + q, (*self.chips[q], self.c), self.sibling)

    def from_sibling(self, a):
        return self._copy(a, 0, self.sibling, self.me)

    def from_chip(self, a, q):
        return self._copy(a, 1 + q, (*self.chips[q], self.c), self.me)

    def forwarded(self, a, q):
        return self._copy(a, 4 + q, (*self.chips[q], 1 - self.c), self.me)

    def sends(self, a):
        return [self.to_sibling(a)] + [self.to_chip(a, q) for q in range(3)] + [self.forward(a, q) for q in range(3)]


class _ChipExchange:
    def __init__(self, ins, outs, send_sems, recv_sems, local_sems):
        self.ins, self.outs = ins, outs
        self.send_sems, self.recv_sems, self.local_sems = send_sems, recv_sems, local_sems
        self.x, self.y, self.c = _place()
        self.my_chip = 2 * self.x + self.y

    def _peer(self, q):
        return (self.x ^ (q >> 1), self.y ^ (q & 1))

    def _copy(self, a, q, src_block, dst_slot):
        px, py = self._peer(q)
        return pltpu.make_async_remote_copy(
            src_ref=self.ins[a].at[src_block], dst_ref=self.outs[a].at[dst_slot],
            send_sem=self.send_sems.at[a, q - 1], recv_sem=self.recv_sems.at[a, q - 1],
            device_id=(px, py, self.c), device_id_type=pl.DeviceIdType.MESH)

    def _send(self, a, q):
        px, py = self._peer(q)
        return self._copy(a, q, 2 * px + py, self.my_chip)

    def _arrival(self, a, q):
        px, py = self._peer(q)
        return self._copy(a, q, self.my_chip, 2 * px + py)

    def _local(self, a):
        return pltpu.make_async_copy(self.ins[a].at[self.my_chip], self.outs[a].at[self.my_chip],
                                     self.local_sems.at[a])

    def start(self):
        for a in range(len(self.ins)):
            self._local(a).start()
            for q in range(1, 4):
                self._send(a, q).start()

    def wait(self):
        for a in range(len(self.ins)):
            for q in range(1, 4):
                self._arrival(a, q).wait_recv()
        for a in range(len(self.ins)):
            for q in range(1, 4):
                self._send(a, q).wait_send()
            self._local(a).wait()

    @staticmethod
    def scratch(n):
        return [pltpu.SemaphoreType.DMA((n, 3)), pltpu.SemaphoreType.DMA((n, 3)), pltpu.SemaphoreType.DMA((n,))]


_GATHER_SEMS = 7
LOCAL_DMA_THREAD = 1

INPROJ_TILE = 896


def _prenorm(x2, gpre):
    T = x2.shape[0]
    tm = min(512, T)

    def body(x_ref, g_ref, h_ref):
        x = x_ref[...]
        r = lax.rsqrt(jnp.mean(x * x, axis=-1, keepdims=True) + NORM_EPS)
        h_ref[...] = (x * r * g_ref[...]).astype(BF16)

    return pl.pallas_call(
        body, name="prenorm",
        grid=(T // tm,),
        in_specs=[pl.BlockSpec((tm, D_MODEL), lambda i: (i, 0)), pl.BlockSpec((1, D_MODEL), lambda i: (0, 0))],
        out_specs=pl.BlockSpec((tm, D_MODEL), lambda i: (i, 0)),
        out_shape=jax.ShapeDtypeStruct((T, D_MODEL), BF16),
        compiler_params=_params(("arbitrary",)),
    )(x2, gpre)


def _gather_inproj(h, wt_shard):
    T = h.shape[0]
    tm = min(1024, T)
    nm = T // tm
    tn = INPROJ_TILE
    ntile = IN_WIDTH // tn
    nstep = ntile * nm

    def body(h_hbm, w_in, proj_hbm, w_out, hbuf, wtile, obuf, send_sems, recv_sems, local_sems, h_sem, w_sems, o_sems):
        j, i = pl.program_id(0), pl.program_id(1)
        step = j * nm + i
        slot = step % 2
        ops = _GatherOps([w_in], [w_out], send_sems, recv_sems, local_sems)
        x, y, _ = _place()

        def tile_of(jj):
            k = jj // 3
            return 3 * ((2 * x + y) ^ (((k & 1) << 1) | (k >> 1))) + jj % 3

        tile = tile_of(j)

        def h_load():
            return pltpu.make_async_copy(h_hbm, hbuf, h_sem)

        def store(s, rows, cols):
            return pltpu.make_async_copy(obuf.at[s], proj_hbm.at[rows, cols], o_sems.at[s])

        def window(ii, t):
            return pl.ds(pl.multiple_of(ii * tm, tm), tm), pl.ds(pl.multiple_of(t * tn, tn), tn)

        @pl.when(step == 0)
        def _():
            h_load().start()
            ops.local(0).start()
            ops.to_sibling(0).start()
            ops.to_chip(0, 0).start()
            ops.to_chip(0, 1).start()
            h_load().wait()

        for kk in range(4):
            @pl.when((j == 3 * kk) & (i == 0))
            def _(kk=kk):
                if kk == 0:
                    ops.local(0).wait()
                    ops.from_sibling(0).wait_recv()
                else:
                    q = kk - 1
                    ops.from_chip(0, q).wait_recv()
                    ops.forward(0, q).start()
                    if q == 0:
                        ops.to_chip(0, 2).start()
                    ops.forwarded(0, q).wait_recv()

        wslot = j % 2

        def fetch(jj, sw):
            rows = pl.ds(pl.multiple_of(tile_of(jj) * tn, tn), tn)
            return pltpu.make_async_copy(w_out.at[rows, :], wtile.at[sw], w_sems.at[sw])

        @pl.when((i == 0) & (j % 3 == 0))
        def _():
            fetch(j, wslot).start(LOCAL_DMA_THREAD)

        @pl.when(i == 0)
        def _():
            fetch(j, wslot).wait()

        @pl.when((i == 0) & (j % 3 != 2))
        def _():
            fetch(j + 1, 1 - wslot).start(LOCAL_DMA_THREAD)

        @pl.when(step >= 2)
        def _():
            store(slot, *window(0, 0)).wait()

        hv = hbuf[pl.ds(pl.multiple_of(i * tm, tm), tm), :]
        obuf[slot] = _dot(hv, wtile[wslot], NT).astype(BF16)
        store(slot, *window(i, tile)).start(LOCAL_DMA_THREAD)

        @pl.when(step == nstep - 1)
        def _():
            for s in range(min(2, nstep)):
                store(s, *window(0, 0)).wait()
            for cp in ops.sends(0):
                cp.wait_send()

    anyspec = lambda: pl.BlockSpec(memory_space=pl.ANY)
    return pl.pallas_call(
        body, name="gather_inproj",
        grid=(ntile, nm),
        in_specs=[anyspec(), anyspec()],
        out_specs=[anyspec(), anyspec()],
        out_shape=[jax.ShapeDtypeStruct((T, IN_WIDTH), BF16),
                   jax.ShapeDtypeStruct((N_DEV * wt_shard.shape[0], D_MODEL), BF16)],
        scratch_shapes=[pltpu.VMEM((T, D_MODEL), BF16), pltpu.VMEM((2, tn, D_MODEL), BF16),
                        pltpu.VMEM((2, tm, tn), BF16),
                        pltpu.SemaphoreType.DMA((1, _GATHER_SEMS)), pltpu.SemaphoreType.DMA((1, _GATHER_SEMS)),
                        pltpu.SemaphoreType.DMA((1,)), pltpu.SemaphoreType.DMA, pltpu.SemaphoreType.DMA((2,)),
                        pltpu.SemaphoreType.DMA((2,))],
        compiler_params=_params(("arbitrary", "arbitrary")),
    )(h, wt_shard)


def _bias_table(rel_bias, bucket):
    def body(rb_ref, bk_ref, out_ref):
        h = pl.program_id(0)
        bk = bk_ref[...]
        acc = jnp.zeros(bk.shape, F32)
        for b in range(REL_BUCKETS):
            acc = jnp.where(bk == b, rb_ref[b, h], acc)
        out_ref[...] = acc

    return pl.pallas_call(
        body, name="bias_table",
        grid=(ATTN_HEADS,),
        in_specs=[pl.BlockSpec(memory_space=pltpu.SMEM),
                  pl.BlockSpec((ATTN_BLOCK, 2 * ATTN_BLOCK), lambda h: (0, 0))],
        out_specs=pl.BlockSpec((None, ATTN_BLOCK, 2 * ATTN_BLOCK), lambda h: (h, 0, 0)),
        out_shape=jax.ShapeDtypeStruct((ATTN_HEADS, ATTN_BLOCK, 2 * ATTN_BLOCK), F32),
        compiler_params=_params(("arbitrary",)),
    )(rel_bias, bucket)


def _bias_table_bwd(dbias, bucket):
    def body(db_ref, bk_ref, out_ref):
        h = pl.program_id(0)
        bk = bk_ref[...]
        db = db_ref[...]
        for b in range(REL_BUCKETS):
            out_ref[b, h] = jnp.sum(jnp.where(bk == b, db, 0.0))

    return pl.pallas_call(
        body, name="bias_table_bwd",
        grid=(ATTN_HEADS,),
        in_specs=[pl.BlockSpec((None, ATTN_BLOCK, 2 * ATTN_BLOCK), lambda h: (h, 0, 0)),
                  pl.BlockSpec((ATTN_BLOCK, 2 * ATTN_BLOCK), lambda h: (0, 0))],
        out_specs=pl.BlockSpec(memory_space=pltpu.SMEM),
        out_shape=jax.ShapeDtypeStruct((REL_BUCKETS, ATTN_HEADS), F32),
        compiler_params=_params(("arbitrary",)),
    )(dbias, bucket)


def _attn_common(qkvg, kv_prev, blk):
    lane = lax.broadcasted_iota(jnp.int32, (1, 128), 1)
    half = (lane < ATTN_HEAD_DIM, lane >= ATTN_HEAD_DIM)
    kv_cur = qkvg[:, C_AK:C_AG]
    win = jnp.concatenate([kv_prev, kv_cur], axis=0)
    k_slab, v_slab = [], []
    for r in range(2):
        ks = win[:, 128 * r:128 * r + 128]
        vs = win[:, 256 + 128 * r:256 + 128 * r + 128]
        k_slab.append((ks, pltpu.roll(ks, ATTN_HEAD_DIM, 1)))
        v_slab.append((vs, pltpu.roll(vs, ATTN_HEAD_DIM, 1)))
    rows4 = ATTN_GROUP * ATTN_BLOCK
    qi = lax.broadcasted_iota(jnp.int32, (rows4, 2 * ATTN_BLOCK), 0) & (ATTN_BLOCK - 1)
    si = lax.broadcasted_iota(jnp.int32, (rows4, 2 * ATTN_BLOCK), 1)
    valid = (si > qi) & (si <= qi + ATTN_BLOCK) & ((si >= ATTN_BLOCK) | (blk > 0))
    return half, k_slab, v_slab, valid


def _stack_heads(half, slab0, slab1):
    return jnp.concatenate([jnp.where(half[0], slab0, 0.0), jnp.where(half[1], slab0, 0.0),
                            jnp.where(half[0], slab1, 0.0), jnp.where(half[1], slab1, 0.0)], axis=0)


def _unstack_heads(half, x4):
    B = ATTN_BLOCK
    return (jnp.where(half[0], x4[0:B], x4[B:2 * B]), jnp.where(half[0], x4[2 * B:3 * B], x4[3 * B:4 * B]))


def _attn_group(j, qkvg, half, k_slab, v_slab, valid, bias_ref, sinks_ref):
    r, aj = j // 2, j % 2
    pick = (lambda a, b: jnp.where(half[0], a, b)) if aj == 0 else (lambda a, b: jnp.where(half[0], b, a))
    kb = pick(*k_slab[r]).astype(BF16)
    vb = pick(*v_slab[r]).astype(BF16)
    q4 = _stack_heads(half, qkvg[:, 256 * j:256 * j + 128], qkvg[:, 256 * j + 128:256 * j + 256]).astype(BF16)
    bias4 = bias_ref[ATTN_GROUP * j:ATTN_GROUP * (j + 1)].reshape(valid.shape)
    yield
    s = _dot(q4, kb, NT) * (ATTN_HEAD_DIM ** -0.5) + bias4
    s = jnp.where(valid, s, NEG_BIG)
    rowblk = lax.broadcasted_iota(jnp.int32, (valid.shape[0], 1), 0) // ATTN_BLOCK
    sink = jnp.full((valid.shape[0], 1), sinks_ref[0, ATTN_GROUP * j], F32)
    for b in range(1, ATTN_GROUP):
        sink = jnp.where(rowblk == b, sinks_ref[0, ATTN_GROUP * j + b], sink)
    m = jnp.maximum(jnp.max(s, axis=-1, keepdims=True), sink)
    e = jnp.exp(s - m)
    es = jnp.exp(sink - m)
    inv = 1.0 / (jnp.sum(e, axis=-1, keepdims=True) + es)
    pn = e * inv
    yield
    o4 = _dot(pn.astype(BF16), vb)
    return dict(r=r, aj=aj, kb=kb, vb=vb, q4=q4, pn=pn, psink=es * inv, o4=o4)


def _attn_specs(nb):
    row = lambda b, i: b * nb + i
    return [
        pl.BlockSpec((ATTN_BLOCK, C_HQ), lambda b, i: (row(b, i), 0)),
        pl.BlockSpec((ATTN_BLOCK, 512), lambda b, i: (row(b, jnp.maximum(i - 1, 0)), 2)),
        pl.BlockSpec((ATTN_HEADS, ATTN_BLOCK, 2 * ATTN_BLOCK), lambda b, i: (0, 0, 0)),
        pl.BlockSpec(memory_space=pltpu.SMEM),
    ]


def _attn_fwd(proj, bias, sinks, nseq, nb):
    T = proj.shape[0]

    def body(qkvg_ref, kvp_ref, bias_ref, sinks_ref, ya_ref):
        qkvg = qkvg_ref[...].astype(F32)
        half, k_slab, v_slab, valid = _attn_common(qkvg, kvp_ref[...].astype(F32), pl.program_id(1))
        groups = _interleave([_attn_group(j, qkvg, half, k_slab, v_slab, valid, bias_ref, sinks_ref)
                              for j in range(ATTN_HEADS // ATTN_GROUP)])
        slabs = []
        for grp in groups:
            slabs += _unstack_heads(half, grp["o4"])
        o_all = jnp.concatenate(slabs, axis=1)
        g = qkvg[:, C_AG:C_HQ]
        ya_ref[...] = (o_all * (g * _sigmoid(g))).astype(BF16)

    return pl.pallas_call(
        body, name="attn_fwd",
        grid=(nseq, nb),
        in_specs=_attn_specs(nb),
        out_specs=pl.BlockSpec((ATTN_BLOCK, 1024), lambda b, i: (b * nb + i, 0)),
        out_shape=jax.ShapeDtypeStruct((T, 1024), BF16),
        compiler_params=_params(("arbitrary", "arbitrary")),
    )(proj, proj, bias, sinks)


def _attn_bwd(proj, bias, sinks, d_ya, nseq, nb):
    T = proj.shape[0]
    S = nb * ATTN_BLOCK
    scale = ATTN_HEAD_DIM ** -0.5

    def body(qkvg_ref, kvp_ref, bias_ref, sinks_ref, dya_ref, dq_ref, dkv_ref, dg_ref, dbias_ref, dsinks_ref):
        b, i = pl.program_id(0), pl.program_id(1)
        first = (b == 0) & (i == 0)

        @pl.when(first)
        def _():
            dbias_ref[...] = jnp.zeros(dbias_ref.shape, F32)
            for h in range(ATTN_HEADS):
                dsinks_ref[0, h] = 0.0

        qkvg = qkvg_ref[...].astype(F32)
        half, k_slab, v_slab, valid = _attn_common(qkvg, kvp_ref[...].astype(F32), i)
        g = qkvg[:, C_AG:C_HQ]
        sg = _sigmoid(g)
        silu_g = g * sg
        dya = dya_ref[...].astype(F32)
        do_all = dya * silu_g
        dq_slabs, o_slabs = [], []
        dk_slab, dv_slab = [None, None], [None, None]
        B = ATTN_BLOCK

        def fold(x, aj):
            return jnp.where(half[aj], x + pltpu.roll(x, ATTN_HEAD_DIM, 1), 0.0)

        def group_bwd(j):
            grp = yield from _attn_group(j, qkvg, half, k_slab, v_slab, valid, bias_ref, sinks_ref)
            pn = grp["pn"]
            do4 = _stack_heads(half, do_all[:, 256 * j:256 * j + 128], do_all[:, 256 * j + 128:256 * j + 256])
            do4b = do4.astype(BF16)
            yield
            dp = _dot(do4b, grp["vb"], NT)
            delta = jnp.sum(do4 * grp["o4"], axis=-1, keepdims=True)
            ds = pn * (dp - delta)
            sink_term = grp["psink"] * delta
            for b4 in range(ATTN_GROUP):
                dsinks_ref[0, ATTN_GROUP * j + b4] += -jnp.sum(sink_term[B * b4:B * (b4 + 1)])
            dbias_ref[ATTN_GROUP * j:ATTN_GROUP * (j + 1)] += ds.reshape(ATTN_GROUP, B, 2 * B)
            dsb = ds.astype(BF16)
            yield
            dq4 = _dot(dsb, grp["kb"]) * scale
            dk_j = fold(_dot(dsb, grp["q4"], TN) * scale, grp["aj"])
            dv_j = fold(_dot(pn.astype(BF16), do4b, TN), grp["aj"])
            return dict(r=grp["r"], dq=_unstack_heads(half, dq4), o=_unstack_heads(half, grp["o4"]), dk=dk_j, dv=dv_j)

        for res in _interleave([group_bwd(j) for j in range(ATTN_HEADS // ATTN_GROUP)]):
            r = res["r"]
            dq_slabs += res["dq"]
            o_slabs += res["o"]
            dk_slab[r] = res["dk"] if dk_slab[r] is None else dk_slab[r] + res["dk"]
            dv_slab[r] = res["dv"] if dv_slab[r] is None else dv_slab[r] + res["dv"]

        dq_ref[...] = jnp.concatenate(dq_slabs, axis=1).astype(BF16)
        o_all = jnp.concatenate(o_slabs, axis=1)
        dg_ref[...] = (dya * o_all * (sg * (1.0 + g * (1.0 - sg)))).astype(BF16)

        dkv = jnp.concatenate(dk_slab + dv_slab, axis=1)
        cur = pl.multiple_of(i * ATTN_BLOCK, ATTN_BLOCK)
        dkv_ref[pl.ds(cur, ATTN_BLOCK), :] = dkv[ATTN_BLOCK:].astype(BF16)

        @pl.when(i > 0)
        def _():
            prev = pl.multiple_of((i - 1) * ATTN_BLOCK, ATTN_BLOCK)
            old = dkv_ref[pl.ds(prev, ATTN_BLOCK), :].astype(F32)
            dkv_ref[pl.ds(prev, ATTN_BLOCK), :] = (old + dkv[:ATTN_BLOCK]).astype(BF16)

    row_spec = lambda w: pl.BlockSpec((ATTN_BLOCK, w), lambda b, i: (b * nb + i, 0))
    return pl.pallas_call(
        body, name="attn_bwd",
        grid=(nseq, nb),
        in_specs=_attn_specs(nb) + [row_spec(1024)],
        out_specs=[row_spec(1024),
                   pl.BlockSpec((S, 512), lambda b, i: (b, 0)),
                   row_spec(1024),
                   pl.BlockSpec((ATTN_HEADS, ATTN_BLOCK, 2 * ATTN_BLOCK), lambda b, i: (0, 0, 0)),
                   pl.BlockSpec(memory_space=pltpu.SMEM)],
        out_shape=[jax.ShapeDtypeStruct((T, 1024), BF16),
                   jax.ShapeDtypeStruct((T, 512), BF16),
                   jax.ShapeDtypeStruct((T, 1024), BF16),
                   jax.ShapeDtypeStruct((ATTN_HEADS, ATTN_BLOCK, 2 * ATTN_BLOCK), F32),
                   jax.ShapeDtypeStruct((1, ATTN_HEADS), F32)],
        compiler_params=_params(("arbitrary", "arbitrary")),
    )(proj, proj, bias, sinks, d_ya)


def _split3(x):
    hi = x.astype(BF16)
    r1 = x - hi.astype(F32)
    mid = r1.astype(BF16)
    lo = (r1 - mid.astype(F32)).astype(BF16)
    return jnp.concatenate([hi, mid, lo], axis=1)


def _tri_sum(tri, x):
    y = _dot(tri, _split3(x))
    return y[:, :128] + y[:, 128:256] + y[:, 256:]


def _interleave(stages):
    results = [None] * len(stages)
    live = list(range(len(stages)))
    while live:
        still = []
        for idx in live:
            try:
                next(stages[idx])
                still.append(idx)
            except StopIteration as done:
                results[idx] = done.value
        live = still
    return results


def _hgrn_chunk(hq, hf, hi, lb):
    C = HGRN_CHUNK
    t = lax.broadcasted_iota(jnp.int32, (C, C), 0)
    s = lax.broadcasted_iota(jnp.int32, (C, C), 1)
    causal = s <= t
    sf = _sigmoid(hf)
    f = lb + (1.0 - lb) * sf
    lf = jnp.log(f)
    yield
    G = _tri_sum(causal.astype(BF16), lf)
    sq = _sigmoid(hq)
    qs = hq * sq
    k = 1.0 - f
    rowblk = lax.broadcasted_iota(jnp.int32, (C, 1), 0) // HGRN_SUB
    qt, kt, eq, ek = [], [], [], []
    for i in range(C // HGRN_SUB):
        lo = HGRN_SUB * i
        ref = G[lo + HGRN_SUB // 2:lo + HGRN_SUB // 2 + 1, :]
        eq_i = jnp.exp(G[lo:lo + HGRN_SUB] - ref)
        ek_i = jnp.exp(jnp.where(rowblk <= i, ref - G, 0.0))
        eq.append(eq_i)
        ek.append(ek_i)
        qt.append((qs[lo:lo + HGRN_SUB] * eq_i).astype(BF16))
        kt.append((k * ek_i).astype(BF16))
    yield
    A = jnp.concatenate([_dot(qt[i], kt[i], NT) for i in range(C // HGRN_SUB)], axis=0)
    A = jnp.where(causal, A, 0.0)
    glast = G[C - 1:C, :]
    eG = jnp.exp(G)
    edec = jnp.exp(glast - G)
    return dict(causal=causal, sf=sf, f=f, G=G, sq=sq, qs=qs, k=k, qt=qt, kt=kt, eq=eq, ek=ek, A=A,
                glast=glast, eG=eG, edec=edec, qhat=qs * eG, kdec=k * edec, v=hi)


def _hgrn_specs(nseq, ng, rows, reverse):
    W = HGRN_PAR
    nblk = W * HGRN_DIM // HGRN_COLS
    gi = (lambda g: ng - 1 - g) if reverse else (lambda g: g)

    def cols(c0):
        return [pl.BlockSpec((rows, HGRN_COLS),
                             lambda h, b, g, q=q: (b * ng + gi(g), c0 // HGRN_COLS + h * nblk + q)) for q in range(nblk)]

    return gi, (cols(C_HQ) + cols(C_HF) + cols(C_HI) + cols(C_HG)
                + [pl.BlockSpec((2, W * HGRN_DIM), lambda h, b, g: (0, h)),
                   pl.BlockSpec((W, 1, HGRN_DIM), lambda h, b, g: (h, 0, 0))])


def _hgrn_operands(refs):
    nblk = HGRN_PAR * HGRN_DIM // HGRN_COLS
    per = HGRN_COLS // HGRN_DIM

    def reader(group):
        def read(rs, w):
            lo = HGRN_DIM * (w % per)
            return group[w // per][rs, lo:lo + HGRN_DIM].astype(F32)
        return read

    readers = [reader(refs[nblk * a:nblk * (a + 1)]) for a in range(4)]
    return readers, refs[4 * nblk], refs[4 * nblk + 1], refs[4 * nblk + 2:]


def _hgrn_fwd(proj, lb_logits, gain3, nseq, S, shards):
    T = proj.shape[0]
    W = HGRN_PAR
    nc = S // HGRN_CHUNK
    cg = min(8, nc)
    ng = nc // cg
    rows = cg * HGRN_CHUNK
    n = len(shards)
    nh = HGRN_HEADS // W

    def body(*all_refs):
        (hq, hf, hi, hg), lbl_ref, gain_ref, refs = _hgrn_operands(all_refs)
        ins = refs[:n]
        o_ref, yh_ref, st_ref = refs[n:n + 3]
        outs = refs[n + 3:2 * n + 3]
        state, send_sems, recv_sems, local_sems = refs[2 * n + 3:]
        step = (pl.program_id(0), pl.program_id(1), pl.program_id(2))
        ops = _GatherOps(ins, outs, send_sems, recv_sems, local_sems)

        @pl.when((step[0] == 0) & (step[1] == 0) & (step[2] == 0))
        def _():
            for a in range(n):
                ops.local(a).start()
                ops.to_sibling(a).start()
                for q in range(3):
                    ops.to_chip(a, q).start()

        @pl.when(pl.program_id(2) == 0)
        def _():
            state[...] = jnp.zeros(state.shape, F32)

        lb_all = _sigmoid(lbl_ref[0:1, :] - lbl_ref[1:2, :])

        def head(w, c):
            rs = pl.ds(pl.multiple_of(c * HGRN_CHUNK, HGRN_CHUNK), HGRN_CHUNK)
            ls = slice(HGRN_DIM * w, HGRN_DIM * (w + 1))
            ch = yield from _hgrn_chunk(hq(rs, w), hf(rs, w), hi(rs, w), lb_all[:, ls])
            st = state[w]
            st_ref[w, c] = st
            vb = ch["v"].astype(BF16)
            yield
            o = _dot(ch["qhat"].astype(BF16), st.astype(BF16), NT) + _dot(ch["A"].astype(BF16), vb)
            state[w] = st * jnp.exp(ch["glast"]) + _dot(vb, ch["kdec"].astype(BF16), TN)
            o_ref[rs, ls] = o
            r = lax.rsqrt(jnp.mean(o * o, axis=-1, keepdims=True) + NORM_EPS)
            gate = hg(rs, w)
            yh_ref[rs, ls] = (o * r * gain_ref[w] * (gate * _sigmoid(gate))).astype(BF16)

        def chunk(c, carry):
            _interleave([head(w, c) for w in range(W)])
            return carry

        lax.fori_loop(0, cg, chunk, 0)

        @pl.when((step[0] == nh - 1) & (step[1] == nseq - 1) & (step[2] == ng - 1))
        def _():
            for a in range(n):
                ops.local(a).wait()
                ops.from_sibling(a).wait_recv()
                for q in range(3):
                    ops.from_chip(a, q).wait_recv()
                    ops.forward(a, q).start()
            for a in range(n):
                for q in range(3):
                    ops.forwarded(a, q).wait_recv()
                for cp in ops.sends(a):
                    cp.wait_send()

    _, in_specs = _hgrn_specs(nseq, ng, rows, False)
    out_row = lambda: pl.BlockSpec((rows, W * HGRN_DIM), lambda h, b, g: (b * ng + g, h))
    anyspec = lambda: pl.BlockSpec(memory_space=pl.ANY)
    return pl.pallas_call(
        body, name="hgrn_fwd",
        grid=(nh, nseq, ng),
        in_specs=in_specs + [anyspec() for _ in shards],
        out_specs=[out_row(), out_row(),
                   pl.BlockSpec((None, W, cg, HGRN_DIM, HGRN_DIM), lambda h, b, g: (b, h, g, 0, 0))]
                  + [anyspec() for _ in shards],
        out_shape=[jax.ShapeDtypeStruct((T, 1024), F32),
                   jax.ShapeDtypeStruct((T, 1024), BF16),
                   jax.ShapeDtypeStruct((nseq, HGRN_HEADS, nc, HGRN_DIM, HGRN_DIM), F32)]
                  + [jax.ShapeDtypeStruct((N_DEV * s.shape[0], s.shape[1]), s.dtype) for s in shards],
        scratch_shapes=[pltpu.VMEM((W, HGRN_DIM, HGRN_DIM), F32),
                        pltpu.SemaphoreType.DMA((n, _GATHER_SEMS)), pltpu.SemaphoreType.DMA((n, _GATHER_SEMS)),
                        pltpu.SemaphoreType.DMA((n,))],
        compiler_params=_params(("arbitrary", "arbitrary", "arbitrary")),
    )(*([proj] * (4 * W * HGRN_DIM // HGRN_COLS)), lb_logits, gain3, *shards)


def _hgrn_bwd(proj, lb_logits, gain3, o, d_yh, states, nseq, S, chip_sums):
    T = proj.shape[0]
    W = HGRN_PAR
    n = len(chip_sums)
    nh = HGRN_HEADS // W
    nc = S // HGRN_CHUNK
    cg = min(8, nc)
    ng = nc // cg
    rows = cg * HGRN_CHUNK
    C = HGRN_CHUNK
    nsub = C // HGRN_SUB

    def body(*all_refs):
        (hq_of, hf_of, hi_of, hg_of), lbl_ref, gain_ref, refs = _hgrn_operands(all_refs)
        o_ref, dyh_ref, st_ref = refs[:3]
        sums_in = refs[3:3 + n]
        dhq_ref, dhf_ref, dhi_ref, dhg_ref, dgain_ref, dlbl_ref = refs[3 + n:9 + n]
        sums_out = refs[9 + n:9 + 2 * n]
        dstate, dlb_acc, send_sems, recv_sems, local_sems = refs[9 + 2 * n:]
        h, b, g = pl.program_id(0), pl.program_id(1), pl.program_id(2)
        exchange = _ChipExchange(sums_in, sums_out, send_sems, recv_sems, local_sems)

        @pl.when((h == 0) & (b == 0) & (g == 0))
        def _():
            exchange.start()

        @pl.when(g == 0)
        def _():
            dstate[...] = jnp.zeros(dstate.shape, F32)

        @pl.when((b == 0) & (g == 0))
        def _():
            dgain_ref[...] = jnp.zeros(dgain_ref.shape, F32)
            dlb_acc[...] = jnp.zeros(dlb_acc.shape, F32)

        lb_all = _sigmoid(lbl_ref[0:1, :] - lbl_ref[1:2, :])

        anti = (lax.broadcasted_iota(jnp.int32, (C, C), 1) >= lax.broadcasted_iota(jnp.int32, (C, C), 0)).astype(BF16)
        last_row = lax.broadcasted_iota(jnp.int32, (C, 1), 0) == C - 1

        def head_load(w, c):
            rs = pl.ds(pl.multiple_of(c * C, C), C)
            ls = slice(HGRN_DIM * w, HGRN_DIM * (w + 1))
            return dict(hq=hq_of(rs, w), hf=hf_of(rs, w), hi=hi_of(rs, w), hg=hg_of(rs, w), lb=lb_all[:, ls],
                        gain=gain_ref[w], ov=o_ref[rs, ls], dyh=dyh_ref[rs, ls].astype(F32),
                        st=st_ref[w, c], dst=dstate[w])

        def head_math(v):
            lb, gain, hq, hg = v["lb"], v["gain"], v["hq"], v["hg"]
            ch = yield from _hgrn_chunk(hq, v["hf"], v["hi"], lb)
            ov, dyh = v["ov"], v["dyh"]
            r = lax.rsqrt(jnp.mean(ov * ov, axis=-1, keepdims=True) + NORM_EPS)
            on = ov * r
            sg = _sigmoid(hg)
            doh = dyh * (hg * sg)
            out = dict(dhg=(dyh * on * gain * (sg * (1.0 + hg * (1.0 - sg)))).astype(BF16),
                       dgain=jnp.sum(doh * on, axis=0, keepdims=True))
            don = doh * gain
            do = r * (don - on * jnp.mean(don * on, axis=-1, keepdims=True))
            dob = do.astype(BF16)
            st, dst = v["st"], v["dst"]
            stb, dstb = st.astype(BF16), dst.astype(BF16)
            vb = ch["v"].astype(BF16)
            qhatb = ch["qhat"].astype(BF16)
            eglast = jnp.exp(ch["glast"])
            yield
            dqhat = _dot(dob, stb)
            dkdec = _dot(vb, dstb)
            dv = _dot(ch["kdec"].astype(BF16), dstb, NT)
            deg = jnp.sum(dst * st, axis=0, keepdims=True)
            out["dstate"] = dst * eglast + _dot(dob, qhatb, TN)
            dA = jnp.where(ch["causal"], _dot(dob, vb, NT), 0.0)
            dv = dv + _dot(ch["A"].astype(BF16), dob, TN)
            dAb = dA.astype(BF16)
            yield
            dqs_parts, dgq_parts = [], []
            dk_intra, dgk = None, None
            for i in range(nsub):
                dA_i = dAb[HGRN_SUB * i:HGRN_SUB * (i + 1)]
                dqt = _dot(dA_i, ch["kt"][i])
                dkt = _dot(dA_i, ch["qt"][i], TN)
                dqs_parts.append(dqt * ch["eq"][i])
                dgq_parts.append(dqt * ch["qt"][i].astype(F32))
                dk_i = dkt * ch["ek"][i]
                dgk_i = dkt * ch["kt"][i].astype(F32)
                dk_intra = dk_i if dk_intra is None else dk_intra + dk_i
                dgk = dgk_i if dgk is None else dgk + dgk_i
            dqs_inter = dqhat * ch["eG"]
            dk_state = dkdec * ch["edec"]
            dqs = jnp.concatenate(dqs_parts, axis=0) + dqs_inter
            dk = dk_intra + dk_state
            dG = jnp.concatenate(dgq_parts, axis=0) - dgk + ch["qs"] * dqs_inter - ch["k"] * dk_state
            tail = jnp.sum(dkdec * ch["kdec"], axis=0, keepdims=True) + deg * eglast
            dG = dG + jnp.where(last_row, tail, 0.0)
            yield
            dlf = _tri_sum(anti, dG)
            df = dlf / ch["f"] - dk
            sf, sq = ch["sf"], ch["sq"]
            out["dhf"] = (df * (1.0 - lb) * sf * (1.0 - sf)).astype(BF16)
            out["dlb"] = jnp.sum(df * (1.0 - sf), axis=0, keepdims=True)
            out["dhq"] = (dqs * (sq * (1.0 + hq * (1.0 - sq)))).astype(BF16)
            out["dhi"] = dv.astype(BF16)
            return out

        def head_store(w, c, out):
            rs = pl.ds(pl.multiple_of(c * C, C), C)
            ls = slice(HGRN_DIM * w, HGRN_DIM * (w + 1))
            dhg_ref[rs, ls] = out["dhg"]
            dhf_ref[rs, ls] = out["dhf"]
            dhq_ref[rs, ls] = out["dhq"]
            dhi_ref[rs, ls] = out["dhi"]
            dstate[w] = out["dstate"]
            dgain_ref[w] += out["dgain"]
            dlb_acc[:, ls] += out["dlb"]

        def chunk(cc, carry):
            c = cg - 1 - cc
            outs = _interleave([head_math(v) for v in [head_load(w, c) for w in range(W)]])
            for w in range(W):
                head_store(w, c, outs[w])
            return carry

        lax.fori_loop(0, cg, chunk, 0)

        @pl.when((b == nseq - 1) & (g == ng - 1))
        def _():
            dl0 = dlb_acc[...] * lb_all * (1.0 - lb_all)
            dlbl_ref[0:1, :] = dl0
            dlbl_ref[1:2, :] = -dl0

        @pl.when((h == nh - 1) & (b == nseq - 1) & (g == ng - 1))
        def _():
            exchange.wait()

    gi, in_specs = _hgrn_specs(nseq, ng, rows, True)
    row = lambda: pl.BlockSpec((rows, W * HGRN_DIM), lambda h, b, g: (b * ng + gi(g), h))
    anyspec = lambda: pl.BlockSpec(memory_space=pl.ANY)
    return pl.pallas_call(
        body, name="hgrn_bwd",
        grid=(nh, nseq, ng),
        in_specs=in_specs + [row(), row(),
                             pl.BlockSpec((None, W, cg, HGRN_DIM, HGRN_DIM), lambda h, b, g: (b, h, gi(g), 0, 0))]
                 + [anyspec() for _ in chip_sums],
        out_specs=[row(), row(), row(), row(),
                   pl.BlockSpec((W, 1, HGRN_DIM), lambda h, b, g: (h, 0, 0)),
                   pl.BlockSpec((2, W * HGRN_DIM), lambda h, b, g: (0, h))] + [anyspec() for _ in chip_sums],
        out_shape=[jax.ShapeDtypeStruct((T, 1024), BF16)] * 4
                  + [jax.ShapeDtypeStruct((HGRN_HEADS, 1, HGRN_DIM), F32),
                     jax.ShapeDtypeStruct((2, HGRN_HEADS * HGRN_DIM), F32)]
                  + [jax.ShapeDtypeStruct(s.shape, s.dtype) for s in chip_sums],
        scratch_shapes=[pltpu.VMEM((W, HGRN_DIM, HGRN_DIM), F32), pltpu.VMEM((1, W * HGRN_DIM), F32)]
                       + _ChipExchange.scratch(n),
        compiler_params=_params(("arbitrary", "arbitrary", "arbitrary")),
    )(*([proj] * (4 * W * HGRN_DIM // HGRN_COLS)), lb_logits, gain3, o, d_yh, states, *chip_sums)


def _gate_specs(tm):
    spec = lambda c0, q: pl.BlockSpec((tm, 512), lambda i: (i, c0 // 512 + q))
    return [spec(C_GA, q) for q in range(4)] + [spec(C_GH, q) for q in range(4)]


def _gates(refs):
    ga = jnp.concatenate([r[...] for r in refs[0:4]], axis=1).astype(F32)
    gh = jnp.concatenate([r[...] for r in refs[4:8]], axis=1).astype(F32)
    return ga, gh


def _branch_merge(proj, ya, yh, wa_t, wh_t):
    T = proj.shape[0]
    tm = min(512, T)

    def body(*refs):
        ya_ref, yh_ref, wa_ref, wh_ref, merged_ref, ua_ref, uh_ref = refs[8:]
        ga, gh = _gates(refs)
        ua = _dot(ya_ref[...], wa_ref[...], NT)
        uh = _dot(yh_ref[...], wh_ref[...], NT)
        merged_ref[...] = (_sigmoid(ga) * ua + _sigmoid(gh) * uh).astype(BF16)
        ua_ref[...] = ua.astype(BF16)
        uh_ref[...] = uh.astype(BF16)

    rowb = lambda w: pl.BlockSpec((tm, w), lambda i: (i, 0))
    full = lambda a: pl.BlockSpec(a.shape, lambda i: (0, 0))
    return pl.pallas_call(
        body, name="branch_merge",
        grid=(T // tm,),
        in_specs=_gate_specs(tm) + [rowb(1024), rowb(1024), full(wa_t), full(wh_t)],
        out_specs=[rowb(D_MODEL)] * 3,
        out_shape=[jax.ShapeDtypeStruct((T, D_MODEL), BF16)] * 3,
        compiler_params=_params(("arbitrary",)),
    )(*([proj] * 8), ya, yh, wa_t, wh_t)


def _out_norm_loss(merged, x2, tgt2, gpost, wout):
    T = merged.shape[0]
    tm = min(256, T)

    def body(m_ref, x_ref, t_ref, gpost_ref, wo_ref, dy_ref, dm_ref, dout_ref, loss_ref, dgpost_ref):
        @pl.when(pl.program_id(0) == 0)
        def _():
            loss_ref[...] = jnp.zeros(loss_ref.shape, F32)
            dgpost_ref[...] = jnp.zeros(dgpost_ref.shape, F32)

        y = _dot(m_ref[...], wo_ref[...])
        r2 = lax.rsqrt(jnp.mean(y * y, axis=-1, keepdims=True) + NORM_EPS)
        yn = y * r2
        gpost = gpost_ref[...]
        err = x_ref[...] + yn * gpost - t_ref[...]
        loss_ref[...] += jnp.sum(err * err, axis=0, keepdims=True)
        dout = err * (1.0 / D_MODEL)
        dout_ref[...] = dout
        dgpost_ref[...] += jnp.sum(dout * yn, axis=0, keepdims=True)
        dyn = dout * gpost
        dy = (r2 * (dyn - yn * jnp.mean(dyn * yn, axis=-1, keepdims=True))).astype(BF16)
        dy_ref[...] = dy
        dm_ref[...] = _dot(dy, wo_ref[...], NT).astype(BF16)

    rowb = lambda: pl.BlockSpec((tm, D_MODEL), lambda i: (i, 0))
    vec = lambda: pl.BlockSpec((1, D_MODEL), lambda i: (0, 0))
    return pl.pallas_call(
        body, name="out_norm_loss",
        grid=(T // tm,),
        in_specs=[rowb(), rowb(), rowb(), vec(), pl.BlockSpec(wout.shape, lambda i: (0, 0))],
        out_specs=[rowb(), rowb(), rowb(), vec(), vec()],
        out_shape=[jax.ShapeDtypeStruct((T, D_MODEL), BF16)] * 2
                  + [jax.ShapeDtypeStruct((T, D_MODEL), F32)] + [jax.ShapeDtypeStruct((1, D_MODEL), F32)] * 2,
        compiler_params=_params(("arbitrary",)),
    )(merged, x2, tgt2, gpost, wout)


def _branch_bwd(proj, dm, ua, uh, wa_t, wh_t):
    T = proj.shape[0]
    tm = min(256, T)

    def body(*refs):
        (dm_ref, ua_ref, uh_ref, wa_ref, wh_ref,
         dua_ref, duh_ref, dga_ref, dgh_ref, dya_ref, dyh_ref) = refs[8:]
        ga, gh = _gates(refs)
        sa, sh = _sigmoid(ga), _sigmoid(gh)
        dm = dm_ref[...].astype(F32)
        dua = (dm * sa).astype(BF16)
        duh = (dm * sh).astype(BF16)
        dua_ref[...] = dua
        duh_ref[...] = duh
        dga_ref[...] = (dm * ua_ref[...].astype(F32) * (sa * (1.0 - sa))).astype(BF16)
        dgh_ref[...] = (dm * uh_ref[...].astype(F32) * (sh * (1.0 - sh))).astype(BF16)
        dya_ref[...] = _dot(dua, wa_ref[...]).astype(BF16)
        dyh_ref[...] = _dot(duh, wh_ref[...]).astype(BF16)

    rowb = lambda w: pl.BlockSpec((tm, w), lambda i: (i, 0))
    full = lambda a: pl.BlockSpec(a.shape, lambda i: (0, 0))
    return pl.pallas_call(
        body, name="branch_bwd",
        grid=(T // tm,),
        in_specs=_gate_specs(tm) + [rowb(D_MODEL)] * 3 + [full(wa_t), full(wh_t)],
        out_specs=[rowb(D_MODEL)] * 4 + [rowb(1024)] * 2,
        out_shape=[jax.ShapeDtypeStruct((T, D_MODEL), BF16)] * 4 + [jax.ShapeDtypeStruct((T, 1024), BF16)] * 2,
        compiler_params=_params(("arbitrary",)),
    )(*([proj] * 8), dm, ua, uh, wa_t, wh_t)


def _tn_matmul(L, R, bm, bn, name):
    T, M = L.shape
    N = R.shape[1]

    def body(l_ref, r_ref, out_ref):
        out_ref[...] = _dot(l_ref[...], r_ref[...], TN).astype(BF16)

    return pl.pallas_call(
        body, name=name,
        grid=(N // bn, M // bm),
        in_specs=[pl.BlockSpec((T, bm), lambda j, i: (0, i)),
                  pl.BlockSpec((T, bn), lambda j, i: (0, j))],
        out_specs=pl.BlockSpec((bm, bn), lambda j, i: (i, j)),
        out_shape=jax.ShapeDtypeStruct((M, N), BF16),
        compiler_params=_params(("arbitrary", "arbitrary")),
    )(L, R)


def _dh_prenorm_bwd(dproj, wt_in, x2, dout, gpre, chip_sums):
    T = x2.shape[0]
    tm = min(1024, T)
    ne = 2
    te = tm // ne
    tk = 768
    nk = IN_WIDTH // tk
    nt = T // tm
    n = len(chip_sums)

    def body(dp_ref, w_ref, x_ref, dout_ref, g_ref, *refs):
        ins = refs[:n]
        gx_ref, dg_ref = refs[n], refs[n + 1]
        outs = refs[n + 2:2 * n + 2]
        acc, send_sems, recv_sems, local_sems = refs[2 * n + 2:]
        i, k = pl.program_id(0), pl.program_id(1)
        exchange = _ChipExchange(ins, outs, send_sems, recv_sems, local_sems)

        @pl.when((i == 0) & (k == 0))
        def _():
            dg_ref[...] = jnp.zeros(dg_ref.shape, F32)
            exchange.start()

        @pl.when((i == nt - 1) & (k == nk + ne - 1))
        def _():
            exchange.wait()

        @pl.when(k == 0)
        def _():
            acc[...] = _dot(dp_ref[...], w_ref[...])

        @pl.when((k > 0) & (k < nk))
        def _():
            acc[...] += _dot(dp_ref[...], w_ref[...])

        @pl.when(k >= nk)
        def _():
            dh = acc[pl.ds(pl.multiple_of((k - nk) * te, te), te), :]
            x = x_ref[...]
            r = lax.rsqrt(jnp.mean(x * x, axis=-1, keepdims=True) + NORM_EPS)
            xn = x * r
            dg_ref[...] += jnp.sum(dh * xn, axis=0, keepdims=True)
            dxn = dh * g_ref[...]
            gx_ref[...] = dout_ref[...] + r * (dxn - xn * jnp.mean(dxn * xn, axis=-1, keepdims=True))

    rowb = lambda: pl.BlockSpec((te, D_MODEL), lambda i, k: (ne * i + jnp.clip(k - nk, 0, ne - 1), 0))
    vec = lambda: pl.BlockSpec((1, D_MODEL), lambda i, k: (0, 0))
    anyspec = lambda: pl.BlockSpec(memory_space=pl.ANY)
    return pl.pallas_call(
        body, name="dh_prenorm_bwd",
        grid=(nt, nk + ne),
        in_specs=[pl.BlockSpec((tm, tk), lambda i, k: (i, jnp.minimum(k, nk - 1))),
                  pl.BlockSpec((tk, D_MODEL), lambda i, k: (jnp.minimum(k, nk - 1), 0)),
                  rowb(), rowb(), vec()] + [anyspec() for _ in chip_sums],
        out_specs=[rowb(), vec()] + [anyspec() for _ in chip_sums],
        out_shape=[jax.ShapeDtypeStruct((T, D_MODEL), F32), jax.ShapeDtypeStruct((1, D_MODEL), F32)]
                  + [jax.ShapeDtypeStruct(s.shape, s.dtype) for s in chip_sums],
        scratch_shapes=[pltpu.VMEM((tm, D_MODEL), F32)] + _ChipExchange.scratch(n),
        compiler_params=_params(("arbitrary", "arbitrary")),
    )(dproj, wt_in, x2, dout, gpre, *chip_sums)


def _sum_slots(recv, br, name):
    nslot, R, C = recv.shape

    def body(r_ref, out_ref):
        acc = r_ref[0].astype(F32)
        for s in range(1, nslot):
            acc = acc + r_ref[s].astype(F32)
        out_ref[...] = acc

    return pl.pallas_call(
        body, name=name,
        grid=(R // br,),
        in_specs=[pl.BlockSpec((nslot, br, C), lambda i: (0, i, 0))],
        out_specs=pl.BlockSpec((br, C), lambda i: (i, 0)),
        out_shape=jax.ShapeDtypeStruct((R, C), F32),
        compiler_params=_params(("arbitrary",)),
    )(recv)


def _adamw_math(w, g, m, v):
    m = ADAM_B1 * m + (1.0 - ADAM_B1) * g
    v = ADAM_B2 * v + (1.0 - ADAM_B2) * (g * g)
    m_hat = m / (1.0 - ADAM_B1 ** ADAM_STEP)
    v_hat = v / (1.0 - ADAM_B2 ** ADAM_STEP)
    delta = -ADAM_LR * (m_hat / (jnp.sqrt(v_hat) + ADAM_EPS) + ADAM_WD * w)
    return delta, m, v


def _adamw(w, g, m, v, br, name):
    R, C = g.shape
    lead = (None,) * (w.ndim - 2)

    def body(w_ref, g_ref, m_ref, v_ref, d_ref, nm_ref, nv_ref):
        d_ref[...], nm_ref[...], nv_ref[...] = _adamw_math(w_ref[...], g_ref[...], m_ref[...], v_ref[...])

    spec = lambda: pl.BlockSpec(lead + (br, C), lambda i: (0,) * len(lead) + (i, 0))
    return pl.pallas_call(
        body, name=name,
        grid=(R // br,),
        in_specs=[spec(), pl.BlockSpec((br, C), lambda i: (i, 0)), spec(), spec()],
        out_specs=[spec(), spec(), spec()],
        out_shape=[jax.ShapeDtypeStruct(w.shape, F32)] * 3,
        compiler_params=_params(("arbitrary",)),
    )(w, g, m, v)


def _sibling_exchange(partials, name):
    n = len(partials)

    def body(*refs):
        ins, outs = refs[:n], refs[n:2 * n]
        send_sems, recv_sems = refs[2 * n:]
        x, y, c = _place()

        def copy(a, p):
            return pltpu.make_async_remote_copy(
                src_ref=ins[a].at[p, 1 - c], dst_ref=outs[a].at[p],
                send_sem=send_sems.at[a, p], recv_sem=recv_sems.at[a, p],
                device_id=(x, y, 1 - c), device_id_type=pl.DeviceIdType.MESH)

        copies = [copy(a, p) for p in range(4) for a in range(n)]
        for cp in copies:
            cp.start()
        for cp in copies:
            cp.wait()

    anyspec = lambda: pl.BlockSpec(memory_space=pl.ANY)
    return pl.pallas_call(
        body, name=name,
        in_specs=[anyspec() for _ in partials],
        out_specs=[anyspec() for _ in partials],
        out_shape=[jax.ShapeDtypeStruct((4,) + p.shape[2:], p.dtype) for p in partials],
        scratch_shapes=[pltpu.SemaphoreType.DMA((n, 4)), pltpu.SemaphoreType.DMA((n, 4))],
    )(*partials)


def _chip_sum(partial, from_sibling, br, name):
    _, _, R, C = partial.shape
    cls = lax.axis_index("c").astype(jnp.int32).reshape(1)

    def body(c_ref, mine_ref, sib_ref, out_ref):
        out_ref[...] = (mine_ref[...].astype(F32) + sib_ref[...].astype(F32)).astype(BF16)

    grid_spec = pltpu.PrefetchScalarGridSpec(
        num_scalar_prefetch=1,
        grid=(4, R // br),
        in_specs=[pl.BlockSpec((None, None, br, C), lambda p, i, c: (p, c[0], i, 0)),
                  pl.BlockSpec((None, br, C), lambda p, i, c: (p, i, 0))],
        out_specs=pl.BlockSpec((None, br, C), lambda p, i, c: (p, i, 0)),
    )
    return pl.pallas_call(
        body, name=name, grid_spec=grid_spec,
        out_shape=jax.ShapeDtypeStruct((4, R, C), BF16),
        compiler_params=_params(("arbitrary", "arbitrary")),
    )(cls, partial, from_sibling)


def _all_reduce_small(packed):
    shape = packed.shape

    def body(in_ref, out_ref, slots, send_sems, recv_sems):
        x, y, c = _place()
        my_slot = 4 * x + 2 * y + c

        def peer(k):
            return (x ^ ((k >> 2) & 1), y ^ ((k >> 1) & 1), c ^ (k & 1))

        def copy(k):
            p = peer(k)
            return pltpu.make_async_remote_copy(
                src_ref=in_ref, dst_ref=slots.at[my_slot],
                send_sem=send_sems.at[k - 1], recv_sem=recv_sems.at[k - 1],
                device_id=p, device_id_type=pl.DeviceIdType.MESH)

        def arrival(k):
            p = peer(k)
            return pltpu.make_async_remote_copy(
                src_ref=in_ref, dst_ref=slots.at[4 * p[0] + 2 * p[1] + p[2]],
                send_sem=send_sems.at[k - 1], recv_sem=recv_sems.at[k - 1],
                device_id=p, device_id_type=pl.DeviceIdType.MESH)

        sends = [copy(k) for k in range(1, N_DEV)]
        for cp in sends:
            cp.start()
        slots[my_slot] = in_ref[...]
        for k in range(1, N_DEV):
            arrival(k).wait_recv()
        for cp in sends:
            cp.wait_send()
        acc = slots[0]
        for s in range(1, N_DEV):
            acc = acc + slots[s]
        out_ref[...] = acc

    return pl.pallas_call(
        body, name="all_reduce_small",
        in_specs=[pl.BlockSpec(memory_space=pltpu.VMEM)],
        out_specs=pl.BlockSpec(memory_space=pltpu.VMEM),
        out_shape=jax.ShapeDtypeStruct(shape, F32),
        scratch_shapes=[pltpu.VMEM((N_DEV,) + shape, F32),
                        pltpu.SemaphoreType.DMA((7,)), pltpu.SemaphoreType.DMA((7,))],
    )(packed)


def _pack_small(norm_pre, norm_post, lb_logits, hgrn_norm, rel_bias, sinks, extra=None):
    tail = [hgrn_norm.reshape(1, 1024), rel_bias.reshape(1, 512), sinks.reshape(1, 16)]
    used = 1024 + 512 + 16
    if extra is not None:
        tail.append(extra.reshape(1, 1))
        used += 1
    tail.append(jnp.zeros((1, D_MODEL - used), F32))
    rows = [norm_pre.reshape(1, D_MODEL), norm_post.reshape(1, D_MODEL), lb_logits.reshape(1, D_MODEL),
            jnp.concatenate(tail, axis=1), jnp.zeros((4, D_MODEL), F32)]
    return jnp.concatenate(rows, axis=0)


def _unpack_small(p):
    return (p[0:1], p[3, 1024:1536].reshape(REL_BUCKETS, ATTN_HEADS), p[3:4, 1536:1552],
            p[2].reshape(2, 1024), p[3, 0:1024].reshape(1, HGRN_HEADS, HGRN_DIM), p[1:2])


def _local_step(nseq, S, x2, tgt2, proj, h, rel_bias, attn_sinks, lb_logits, hgrn_norm, norm_post, shards):
    nb = S // ATTN_BLOCK
    bucket = jnp.asarray(_t5_bucket_table())
    gain3 = hgrn_norm.reshape(HGRN_HEADS, 1, HGRN_DIM)

    bias = _bias_table(rel_bias, bucket)
    ya = _attn_fwd(proj, bias, attn_sinks, nseq, nb)
    o, yh, states, wout, wa_t, wh_t = _hgrn_fwd(proj, lb_logits, gain3, nseq, S, shards)
    merged, ua, uh = _branch_merge(proj, ya, yh, wa_t, wh_t)
    dy, dm, dout, loss_cols, d_gpost = _out_norm_loss(merged, x2, tgt2, norm_post, wout)
    dua, duh, dga, dgh, dya, dyh = _branch_bwd(proj, dm, ua, uh, wa_t, wh_t)

    p_out = _tn_matmul(merged, dy, 256, 1024, "dw_out")
    p_a = _tn_matmul(dua, ya, 256, 1024, "dw_branch_attn")
    p_h = _tn_matmul(duh, yh, 256, 1024, "dw_branch_hgrn")
    small_sums = _chip_sums([p_out, p_a, p_h], (128, 128, 128), ("dw_out", "dw_branch_attn", "dw_branch_hgrn"),
                            "sibling_exchange_small")

    dq, dkv, dg, dbias, d_sinks = _attn_bwd(proj, bias, attn_sinks, dya, nseq, nb)
    d_rel_bias = _bias_table_bwd(dbias, bucket)
    dhq, dhf, dhi, dhg, d_gain, d_lbl, r_out, r_a, r_h = _hgrn_bwd(
        proj, lb_logits, gain3, o, dyh, states, nseq, S, small_sums)
    dproj = jnp.concatenate([dq, dkv, dg, dhq, dhf, dhi, dhg, dga, dgh], axis=1)
    p_in = _tn_matmul(dproj, h, 768, 1024, "dw_in")
    return dproj, dout, p_in, r_out, r_a, r_h, d_gpost, d_lbl, d_gain, d_rel_bias, d_sinks, loss_cols


def _chip_sums(partials, block_rows, names, exchange_name):
    split = [p.reshape(4, 2, p.shape[0] // N_DEV, p.shape[1]) for p in partials]
    from_sibling = _sibling_exchange(split, exchange_name)
    return [_chip_sum(p, f, br, "chip_sum_" + nm) for p, f, br, nm in zip(split, from_sibling, block_rows, names)]


def kernel(x, norm_pre, w_in, rel_bias, attn_sinks, lb_logits, hgrn_norm, w_branch_attn, w_branch_hgrn, w_out, norm_post, loss_target, m_norm_pre, m_w_in, m_rel_bias, m_attn_sinks, m_lb_logits, m_hgrn_norm, m_w_branch_attn, m_w_branch_hgrn, m_w_out, m_norm_post, v_norm_pre, v_w_in, v_rel_bias, v_attn_sinks, v_lb_logits, v_hgrn_norm, v_w_branch_attn, v_w_branch_hgrn, v_w_out, v_norm_post):
    nseq, S, _ = x.shape
    T = nseq * S
    x2 = x.reshape(T, D_MODEL)
    tgt2 = loss_target.reshape(T, D_MODEL)

    h = _prenorm(x2, norm_pre)
    proj, wt_in = _gather_inproj(h, w_in[0].T.astype(BF16))
    shards = [w_out[0].astype(BF16), w_branch_attn[0].T.astype(BF16), w_branch_hgrn[0].T.astype(BF16)]

    (dproj, dout, p_in, r_out, r_a, r_h, d_gpost, d_lbl, d_gain, d_rel_bias, d_sinks, loss_cols) = _local_step(
        nseq, S, x2, tgt2, proj, h, rel_bias, attn_sinks, lb_logits, hgrn_norm, norm_post, shards)

    in_sums = _chip_sums([p_in], (192,), ("dw_in",), "sibling_exchange_w_in")
    grad_x2, d_gpre, r_in = _dh_prenorm_bwd(dproj, wt_in, x2, dout, norm_pre, in_sums)
    g_in_t = _sum_slots(r_in, 192, "sum_dw_in")
    g_out = _sum_slots(r_out, 128, "sum_dw_out")
    g_a = _sum_slots(r_a, 128, "sum_dw_branch_attn").T
    g_h = _sum_slots(r_h, 128, "sum_dw_branch_hgrn").T

    loss_part = 0.5 / D_MODEL * jnp.sum(loss_cols)
    packed = _pack_small(d_gpre, d_gpost, d_lbl, d_gain, d_rel_bias, d_sinks, extra=loss_part)
    total = _all_reduce_small(packed)
    loss = total[3, 1024 + 512 + 16]
    sm_w = _pack_small(norm_pre, norm_post, lb_logits, hgrn_norm, rel_bias, attn_sinks)
    sm_m = _pack_small(m_norm_pre, m_norm_post, m_lb_logits, m_hgrn_norm, m_rel_bias, m_attn_sinks)
    sm_v = _pack_small(v_norm_pre, v_norm_post, v_lb_logits, v_hgrn_norm, v_rel_bias, v_attn_sinks)
    sm_d, sm_nm, sm_nv = _adamw(sm_w, total, sm_m, sm_v, 8, "adamw_small")

    t = lambda a: jnp.swapaxes(a, 1, 2)
    d_in, nm_in, nv_in = map(t, _adamw(t(w_in), g_in_t, t(m_w_in), t(v_w_in), 192, "adamw_w_in"))
    d_out, nm_out, nv_out = _adamw(w_out, g_out, m_w_out, v_w_out, 128, "adamw_w_out")
    d_a, nm_a, nv_a = _adamw(w_branch_attn, g_a, m_w_branch_attn, v_w_branch_attn, 256, "adamw_w_branch_attn")
    d_h, nm_h, nv_h = _adamw(w_branch_hgrn, g_h, m_w_branch_hgrn, v_w_branch_hgrn, 256, "adamw_w_branch_hgrn")

    def group(small, big_in, big_a, big_h, big_out):
        npre, rb, sk, lbl, hn, npost = _unpack_small(small)
        return (npre, big_in, rb, sk, lbl, hn, big_a, big_h, big_out, npost)

    return (loss, grad_x2.reshape(nseq, S, D_MODEL),
            *group(total, t(g_in_t[None]), g_a[None], g_h[None], g_out[None]),
            *group(sm_d, d_in, d_a, d_h, d_out),
            *group(sm_nm, nm_in, nm_a, nm_h, nm_out),
            *group(sm_nv, nv_in, nv_a, nv_h, nv_out))
```

```python
import functools
import math

import numpy as np
import jax
import jax.numpy as jnp
from jax import lax
from jax.experimental import pallas as pl
from jax.experimental.pallas import tpu as pltpu

F32 = jnp.float32
BF16 = jnp.bfloat16

D_MODEL = 2048
ATTN_HEADS = 16
ATTN_HEAD_DIM = 64
ATTN_GROUP = 4
ATTN_BLOCK = 128
HGRN_HEADS = 8
HGRN_DIM = 128
HGRN_CHUNK = 64
HGRN_SUB = 16
HGRN_PAR = 8
HGRN_COLS = 512
REL_BUCKETS = 32
REL_MAX_DIST = 128
NORM_EPS = 1e-6
C_AQ, C_AK, C_AV, C_AG = 0, 1024, 1280, 1536
C_HQ, C_HF, C_HI, C_HG = 2560, 3584, 4608, 5632
C_GA, C_GH = 6656, 8704
IN_WIDTH = 10752
N_DEV = 8
assert all(c0 % HGRN_COLS == 0 for c0 in (C_HQ, C_HF, C_HI, C_HG)) and (HGRN_PAR * HGRN_DIM) % HGRN_COLS == 0

ADAM_LR = 0.001
ADAM_B1 = 0.9
ADAM_B2 = 0.999
ADAM_EPS = 1e-08
ADAM_WD = 0.01
ADAM_STEP = 10

VMEM_LIMIT_V7X = 56 * 1024 * 1024
NEG_BIG = -1e30

NT = (((1,), (1,)), ((), ()))
TN = (((0,), (0,)), ((), ()))
NN = (((1,), (0,)), ((), ()))


def _dot(a, b, dims=NN):
    return lax.dot_general(a, b, dims, preferred_element_type=F32)


def _params(sem=None):
    return pltpu.CompilerParams(dimension_semantics=sem, vmem_limit_bytes=VMEM_LIMIT_V7X)


def _sigmoid(x):
    return 1.0 / (1.0 + jnp.exp(-x))


def _t5_bucket_table():
    qi = np.arange(ATTN_BLOCK)[:, None]
    si = np.arange(2 * ATTN_BLOCK)[None, :]
    dist = qi + ATTN_BLOCK - si
    max_exact = REL_BUCKETS // 2
    d = np.maximum(dist, 0)
    df = np.maximum(d, 1).astype(np.float32)
    large = max_exact + (np.log(df / np.float32(max_exact)).astype(np.float32)
                         / np.float32(math.log(REL_MAX_DIST / max_exact))
                         * np.float32(REL_BUCKETS - max_exact)).astype(np.int32)
    large = np.minimum(large, REL_BUCKETS - 1)
    return np.where(d < max_exact, d, large).astype(np.int32)


def _place():
    return lax.axis_index("x"), lax.axis_index("y"), lax.axis_index("c")


class _GatherOps:
    def __init__(self, ins, outs, send_sems, recv_sems, local_sems):
        self.ins, self.outs = ins, outs
        self.send_sems, self.recv_sems, self.local_sems = send_sems, recv_sems, local_sems
        x, y, c = _place()
        self.c = c
        self.me, self.sibling = (x, y, c), (x, y, 1 - c)
        self.chips = [(1 - x, y), (x, 1 - y), (1 - x, 1 - y)]

    def _rows(self, a, dev):
        r = self.ins[a].shape[0]
        return self.outs[a].at[pl.ds((4 * dev[0] + 2 * dev[1] + dev[2]) * r, r), :]

    def _copy(self, a, k, block, to, src=None):
        return pltpu.make_async_remote_copy(
            src_ref=self._rows(a, block) if src is None else src, dst_ref=self._rows(a, block),
            send_sem=self.send_sems.at[a, k], recv_sem=self.recv_sems.at[a, k],
            device_id=to, device_id_type=pl.DeviceIdType.MESH)

    def local(self, a):
        return pltpu.make_async_copy(self.ins[a], self._rows(a, self.me), self.local_sems.at[a])

    def to_sibling(self, a):
        return self._copy(a, 0, self.me, self.sibling, src=self.ins[a])

    def to_chip(self, a, q):
        return self._copy(a, 1 + q, self.me, (*self.chips[q], self.c), src=self.ins[a])

    def forward(self, a, q):
        return self._copy(a, 4 + q, (*self.chips[q], self.c), self.sibling)

    def from_sibling(self, a):
        return self._copy(a, 0, self.sibling, self.me)

    def from_chip(self, a, q):
        return self._copy(a, 1 + q, (*self.chips[q], self.c), self.me)

    def forwarded(self, a, q):
        return self._copy(a, 4 + q, (*self.chips[q], 1 - self.c), self.me)

    def sends(self, a):
        return [self.to_sibling(a)] + [self.to_chip(a, q) for q in range(3)] + [self.forward(a, q) for q in range(3)]


class _ChipExchange:
    def __init__(self, ins, outs, send_sems, recv_sems, local_sems):
        self.ins, self.outs = ins, outs
        self.send_sems, self.recv_sems, self.local_sems = send_sems, recv_sems, local_sems
        self.x, self.y, self.c = _place()
        self.my_chip = 2 * self.x + self.y

    def _peer(self, q):
        return (self.x ^ (q >> 1), self.y ^ (q & 1))

    def _copy(self, a, q, src_block, dst_slot):
        px, py = self._peer(q)
        return pltpu.make_async_remote_copy(
            src_ref=self.ins[a].at[src_block], dst_ref=self.outs[a].at[dst_slot],
            send_sem=self.send_sems.at[a, q - 1], recv_sem=self.recv_sems.at[a, q - 1],
            device_id=(px, py, self.c), device_id_type=pl.DeviceIdType.MESH)

    def _send(self, a, q):
        px, py = self._peer(q)
        return self._copy(a, q, 2 * px + py, self.my_chip)

    def _arrival(self, a, q):
        px, py = self._peer(q)
        return self._copy(a, q, self.my_chip, 2 * px + py)

    def _local(self, a):
        return pltpu.make_async_copy(self.ins[a].at[self.my_chip], self.outs[a].at[self.my_chip],
                                     self.local_sems.at[a])

    def start(self):
        for a in range(len(self.ins)):
            self._local(a).start()
            for q in range(1, 4):
                self._send(a, q).start()

    def wait(self):
        for a in range(len(self.ins)):
            for q in range(1, 4):
                self._arrival(a, q).wait_recv()
        for a in range(len(self.ins)):
            for q in range(1, 4):
                self._send(a, q).wait_send()
            self._local(a).wait()

    @staticmethod
    def scratch(n):
        return [pltpu.SemaphoreType.DMA((n, 3)), pltpu.SemaphoreType.DMA((n, 3)), pltpu.SemaphoreType.DMA((n,))]


_GATHER_SEMS = 7
LOCAL_DMA_THREAD = 1

INPROJ_TILE = 896


def _prenorm(x2, gpre):
    T = x2.shape[0]
    tm = min(512, T)

    def body(x_ref, g_ref, h_ref):
        x = x_ref[...]
        r = lax.rsqrt(jnp.mean(x * x, axis=-1, keepdims=True) + NORM_EPS)
        h_ref[...] = (x * r * g_ref[...]).astype(BF16)

    return pl.pallas_call(
        body, name="prenorm",
        grid=(T // tm,),
        in_specs=[pl.BlockSpec((tm, D_MODEL), lambda i: (i, 0)), pl.BlockSpec((1, D_MODEL), lambda i: (0, 0))],
        out_specs=pl.BlockSpec((tm, D_MODEL), lambda i: (i, 0)),
        out_shape=jax.ShapeDtypeStruct((T, D_MODEL), BF16),
        compiler_params=_params(("arbitrary",)),
    )(x2, gpre)


def _gather_inproj(h, wt_shard):
    T = h.shape[0]
    tm = min(1024, T)
    nm = T // tm
    tn = INPROJ_TILE
    ntile = IN_WIDTH // tn
    nstep = ntile * nm

    def body(h_hbm, w_in, proj_hbm, w_out, hbuf, wtile, obuf, send_sems, recv_sems, local_sems, h_sem, w_sems, o_sems):
        j, i = pl.program_id(0), pl.program_id(1)
        step = j * nm + i
        slot = step % 2
        ops = _GatherOps([w_in], [w_out], send_sems, recv_sems, local_sems)
        x, y, _ = _place()

        def tile_of(jj):
            k = jj // 3
            return 3 * ((2 * x + y) ^ (((k & 1) << 1) | (k >> 1))) + jj % 3

        tile = tile_of(j)

        def h_load():
            return pltpu.make_async_copy(h_hbm, hbuf, h_sem)

        def store(s, rows, cols):
            return pltpu.make_async_copy(obuf.at[s], proj_hbm.at[rows, cols], o_sems.at[s])

        def window(ii, t):
            return pl.ds(pl.multiple_of(ii * tm, tm), tm), pl.ds(pl.multiple_of(t * tn, tn), tn)

        @pl.when(step == 0)
        def _():
            h_load().start()
            ops.local(0).start()
            ops.to_sibling(0).start()
            ops.to_chip(0, 0).start()
            ops.to_chip(0, 1).start()
            h_load().wait()

        for kk in range(4):
            @pl.when((j == 3 * kk) & (i == 0))
            def _(kk=kk):
                if kk == 0:
                    ops.local(0).wait()
                    ops.from_sibling(0).wait_recv()
                else:
                    q = kk - 1
                    ops.from_chip(0, q).wait_recv()
                    ops.forward(0, q).start()
                    if q == 0:
                        ops.to_chip(0, 2).start()
                    ops.forwarded(0, q).wait_recv()

        wslot = j % 2

        def fetch(jj, sw):
            rows = pl.ds(pl.multiple_of(tile_of(jj) * tn, tn), tn)
            return pltpu.make_async_copy(w_out.at[rows, :], wtile.at[sw], w_sems.at[sw])

        @pl.when((i == 0) & (j % 3 == 0))
        def _():
            fetch(j, wslot).start(LOCAL_DMA_THREAD)

        @pl.when(i == 0)
        def _():
            fetch(j, wslot).wait()

        @pl.when((i == 0) & (j % 3 != 2))
        def _():
            fetch(j + 1, 1 - wslot).start(LOCAL_DMA_THREAD)

        @pl.when(step >= 2)
        def _():
            store(slot, *window(0, 0)).wait()

        hv = hbuf[pl.ds(pl.multiple_of(i * tm, tm), tm), :]
        obuf[slot] = _dot(hv, wtile[wslot], NT).astype(BF16)
        store(slot, *window(i, tile)).start(LOCAL_DMA_THREAD)

        @pl.when(step == nstep - 1)
        def _():
            for s in range(min(2, nstep)):
                store(s, *window(0, 0)).wait()
            for cp in ops.sends(0):
                cp.wait_send()

    anyspec = lambda: pl.BlockSpec(memory_space=pl.ANY)
    return pl.pallas_call(
        body, name="gather_inproj",
        grid=(ntile, nm),
        in_specs=[anyspec(), anyspec()],
        out_specs=[anyspec(), anyspec()],
        out_shape=[jax.ShapeDtypeStruct((T, IN_WIDTH), BF16),
                   jax.ShapeDtypeStruct((N_DEV * wt_shard.shape[0], D_MODEL), BF16)],
        scratch_shapes=[pltpu.VMEM((T, D_MODEL), BF16), pltpu.VMEM((2, tn, D_MODEL), BF16),
                        pltpu.VMEM((2, tm, tn), BF16),
                        pltpu.SemaphoreType.DMA((1, _GATHER_SEMS)), pltpu.SemaphoreType.DMA((1, _GATHER_SEMS)),
                        pltpu.SemaphoreType.DMA((1,)), pltpu.SemaphoreType.DMA, pltpu.SemaphoreType.DMA((2,)),
                        pltpu.SemaphoreType.DMA((2,))],
        compiler_params=_params(("arbitrary", "arbitrary")),
    )(h, wt_shard)


def _bias_table(rel_bias, bucket):
    def body(rb_ref, bk_ref, out_ref):
        h = pl.program_id(0)
        bk = bk_ref[...]
        acc = jnp.zeros(bk.shape, F32)
        for b in range(REL_BUCKETS):
            acc = jnp.where(bk == b, rb_ref[b, h], acc)
        out_ref[...] = acc

    return pl.pallas_call(
        body, name="bias_table",
        grid=(ATTN_HEADS,),
        in_specs=[pl.BlockSpec(memory_space=pltpu.SMEM),
                  pl.BlockSpec((ATTN_BLOCK, 2 * ATTN_BLOCK), lambda h: (0, 0))],
        out_specs=pl.BlockSpec((None, ATTN_BLOCK, 2 * ATTN_BLOCK), lambda h: (h, 0, 0)),
        out_shape=jax.ShapeDtypeStruct((ATTN_HEADS, ATTN_BLOCK, 2 * ATTN_BLOCK), F32),
        compiler_params=_params(("arbitrary",)),
    )(rel_bias, bucket)


def _bias_table_bwd(dbias, bucket):
    def body(db_ref, bk_ref, out_ref):
        h = pl.program_id(0)
        bk = bk_ref[...]
        db = db_ref[...]
        for b in range(REL_BUCKETS):
            out_ref[b, h] = jnp.sum(jnp.where(bk == b, db, 0.0))

    return pl.pallas_call(
        body, name="bias_table_bwd",
        grid=(ATTN_HEADS,),
        in_specs=[pl.BlockSpec((None, ATTN_BLOCK, 2 * ATTN_BLOCK), lambda h: (h, 0, 0)),
                  pl.BlockSpec((ATTN_BLOCK, 2 * ATTN_BLOCK), lambda h: (0, 0))],
        out_specs=pl.BlockSpec(memory_space=pltpu.SMEM),
        out_shape=jax.ShapeDtypeStruct((REL_BUCKETS, ATTN_HEADS), F32),
        compiler_params=_params(("arbitrary",)),
    )(dbias, bucket)


def _attn_common(qkvg, kv_prev, blk):
    lane = lax.broadcasted_iota(jnp.int32, (1, 128), 1)
    half = (lane < ATTN_HEAD_DIM, lane >= ATTN_HEAD_DIM)
    kv_cur = qkvg[:, C_AK:C_AG]
    win = jnp.concatenate([kv_prev, kv_cur], axis=0)
    k_slab, v_slab = [], []
    for r in range(2):
        ks = win[:, 128 * r:128 * r + 128]
        vs = win[:, 256 + 128 * r:256 + 128 * r + 128]
        k_slab.append((ks, pltpu.roll(ks, ATTN_HEAD_DIM, 1)))
        v_slab.append((vs, pltpu.roll(vs, ATTN_HEAD_DIM, 1)))
    rows4 = ATTN_GROUP * ATTN_BLOCK
    qi = lax.broadcasted_iota(jnp.int32, (rows4, 2 * ATTN_BLOCK), 0) & (ATTN_BLOCK - 1)
    si = lax.broadcasted_iota(jnp.int32, (rows4, 2 * ATTN_BLOCK), 1)
    valid = (si > qi) & (si <= qi + ATTN_BLOCK) & ((si >= ATTN_BLOCK) | (blk > 0))
    return half, k_slab, v_slab, valid


def _stack_heads(half, slab0, slab1):
    return jnp.concatenate([jnp.where(half[0], slab0, 0.0), jnp.where(half[1], slab0, 0.0),
                            jnp.where(half[0], slab1, 0.0), jnp.where(half[1], slab1, 0.0)], axis=0)


def _unstack_heads(half, x4):
    B = ATTN_BLOCK
    return (jnp.where(half[0], x4[0:B], x4[B:2 * B]), jnp.where(half[0], x4[2 * B:3 * B], x4[3 * B:4 * B]))


def _attn_group(j, qkvg, half, k_slab, v_slab, valid, bias_ref, sinks_ref):
    r, aj = j // 2, j % 2
    pick = (lambda a, b: jnp.where(half[0], a, b)) if aj == 0 else (lambda a, b: jnp.where(half[0], b, a))
    kb = pick(*k_slab[r]).astype(BF16)
    vb = pick(*v_slab[r]).astype(BF16)
    q4 = _stack_heads(half, qkvg[:, 256 * j:256 * j + 128], qkvg[:, 256 * j + 128:256 * j + 256]).astype(BF16)
    bias4 = bias_ref[ATTN_GROUP * j:ATTN_GROUP * (j + 1)].reshape(valid.shape)
    yield
    s = _dot(q4, kb, NT) * (ATTN_HEAD_DIM ** -0.5) + bias4
    s = jnp.where(valid, s, NEG_BIG)
    rowblk = lax.broadcasted_iota(jnp.int32, (valid.shape[0], 1), 0) // ATTN_BLOCK
    sink = jnp.full((valid.shape[0], 1), sinks_ref[0, ATTN_GROUP * j], F32)
    for b in range(1, ATTN_GROUP):
        sink = jnp.where(rowblk == b, sinks_ref[0, ATTN_GROUP * j + b], sink)
    m = jnp.maximum(jnp.max(s, axis=-1, keepdims=True), sink)
    e = jnp.exp(s - m)
    es = jnp.exp(sink - m)
    inv = 1.0 / (jnp.sum(e, axis=-1, keepdims=True) + es)
    pn = e * inv
    yield
    o4 = _dot(pn.astype(BF16), vb)
    return dict(r=r, aj=aj, kb=kb, vb=vb, q4=q4, pn=pn, psink=es * inv, o4=o4)


def _attn_specs(nb):
    row = lambda b, i: b * nb + i
    return [
        pl.BlockSpec((ATTN_BLOCK, C_HQ), lambda b, i: (row(b, i), 0)),
        pl.BlockSpec((ATTN_BLOCK, 512), lambda b, i: (row(b, jnp.maximum(i - 1, 0)), 2)),
        pl.BlockSpec((ATTN_HEADS, ATTN_BLOCK, 2 * ATTN_BLOCK), lambda b, i: (0, 0, 0)),
        pl.BlockSpec(memory_space=pltpu.SMEM),
    ]


def _attn_fwd(proj, bias, sinks, nseq, nb):
    T = proj.shape[0]

    def body(qkvg_ref, kvp_ref, bias_ref, sinks_ref, ya_ref):
        qkvg = qkvg_ref[...].astype(F32)
        half, k_slab, v_slab, valid = _attn_common(qkvg, kvp_ref[...].astype(F32), pl.program_id(1))
        groups = _interleave([_attn_group(j, qkvg, half, k_slab, v_slab, valid, bias_ref, sinks_ref)
                              for j in range(ATTN_HEADS // ATTN_GROUP)])
        slabs = []
        for grp in groups:
            slabs += _unstack_heads(half, grp["o4"])
        o_all = jnp.concatenate(slabs, axis=1)
        g = qkvg[:, C_AG:C_HQ]
        ya_ref[...] = (o_all * (g * _sigmoid(g))).astype(BF16)

    return pl.pallas_call(
        body, name="attn_fwd",
        grid=(nseq, nb),
        in_specs=_attn_specs(nb),
        out_specs=pl.BlockSpec((ATTN_BLOCK, 1024), lambda b, i: (b * nb + i, 0)),
        out_shape=jax.ShapeDtypeStruct((T, 1024), BF16),
        compiler_params=_params(("arbitrary", "arbitrary")),
    )(proj, proj, bias, sinks)


def _attn_bwd(proj, bias, sinks, d_ya, nseq, nb):
    T = proj.shape[0]
    S = nb * ATTN_BLOCK
    scale = ATTN_HEAD_DIM ** -0.5

    def body(qkvg_ref, kvp_ref, bias_ref, sinks_ref, dya_ref, dq_ref, dkv_ref, dg_ref, dbias_ref, dsinks_ref):
        b, i = pl.program_id(0), pl.program_id(1)
        first = (b == 0) & (i == 0)

        @pl.when(first)
        def _():
            dbias_ref[...] = jnp.zeros(dbias_ref.shape, F32)
            for h in range(ATTN_HEADS):
                dsinks_ref[0, h] = 0.0

        qkvg = qkvg_ref[...].astype(F32)
        half, k_slab, v_slab, valid = _attn_common(qkvg, kvp_ref[...].astype(F32), i)
        g = qkvg[:, C_AG:C_HQ]
        sg = _sigmoid(g)
        silu_g = g * sg
        dya = dya_ref[...].astype(F32)
        do_all = dya * silu_g
        dq_slabs, o_slabs = [], []
        dk_slab, dv_slab = [None, None], [None, None]
        B = ATTN_BLOCK

        def fold(x, aj):
            return jnp.where(half[aj], x + pltpu.roll(x, ATTN_HEAD_DIM, 1), 0.0)

        def group_bwd(j):
            grp = yield from _attn_group(j, qkvg, half, k_slab, v_slab, valid, bias_ref, sinks_ref)
            pn = grp["pn"]
            do4 = _stack_heads(half, do_all[:, 256 * j:256 * j + 128], do_all[:, 256 * j + 128:256 * j + 256])
            do4b = do4.astype(BF16)
            yield
            dp = _dot(do4b, grp["vb"], NT)
            delta = jnp.sum(do4 * grp["o4"], axis=-1, keepdims=True)
            ds = pn * (dp - delta)
            sink_term = grp["psink"] * delta
            for b4 in range(ATTN_GROUP):
                dsinks_ref[0, ATTN_GROUP * j + b4] += -jnp.sum(sink_term[B * b4:B * (b4 + 1)])
            dbias_ref[ATTN_GROUP * j:ATTN_GROUP * (j + 1)] += ds.reshape(ATTN_GROUP, B, 2 * B)
            dsb = ds.astype(BF16)
            yield
            dq4 = _dot(dsb, grp["kb"]) * scale
            dk_j = fold(_dot(dsb, grp["q4"], TN) * scale, grp["aj"])
            dv_j = fold(_dot(pn.astype(BF16), do4b, TN), grp["aj"])
            return dict(r=grp["r"], dq=_unstack_heads(half, dq4), o=_unstack_heads(half, grp["o4"]), dk=dk_j, dv=dv_j)

        for res in _interleave([group_bwd(j) for j in range(ATTN_HEADS // ATTN_GROUP)]):
            r = res["r"]
            dq_slabs += res["dq"]
            o_slabs += res["o"]
            dk_slab[r] = res["dk"] if dk_slab[r] is None else dk_slab[r] + res["dk"]
            dv_slab[r] = res["dv"] if dv_slab[r] is None else dv_slab[r] + res["dv"]

        dq_ref[...] = jnp.concatenate(dq_slabs, axis=1).astype(BF16)
        o_all = jnp.concatenate(o_slabs, axis=1)
        dg_ref[...] = (dya * o_all * (sg * (1.0 + g * (1.0 - sg)))).astype(BF16)

        dkv = jnp.concatenate(dk_slab + dv_slab, axis=1)
        cur = pl.multiple_of(i * ATTN_BLOCK, ATTN_BLOCK)
        dkv_ref[pl.ds(cur, ATTN_BLOCK), :] = dkv[ATTN_BLOCK:].astype(BF16)

        @pl.when(i > 0)
        def _():
            prev = pl.multiple_of((i - 1) * ATTN_BLOCK, ATTN_BLOCK)
            old = dkv_ref[pl.ds(prev, ATTN_BLOCK), :].astype(F32)
            dkv_ref[pl.ds(prev, ATTN_BLOCK), :] = (old + dkv[:ATTN_BLOCK]).astype(BF16)

    row_spec = lambda w: pl.BlockSpec((ATTN_BLOCK, w), lambda b, i: (b * nb + i, 0))
    return pl.pallas_call(
        body, name="attn_bwd",
        grid=(nseq, nb),
        in_specs=_attn_specs(nb) + [row_spec(1024)],
        out_specs=[row_spec(1024),
                   pl.BlockSpec((S, 512), lambda b, i: (b, 0)),
                   row_spec(1024),
                   pl.BlockSpec((ATTN_HEADS, ATTN_BLOCK, 2 * ATTN_BLOCK), lambda b, i: (0, 0, 0)),
                   pl.BlockSpec(memory_space=pltpu.SMEM)],
        out_shape=[jax.ShapeDtypeStruct((T, 1024), BF16),
                   jax.ShapeDtypeStruct((T, 512), BF16),
                   jax.ShapeDtypeStruct((T, 1024), BF16),
                   jax.ShapeDtypeStruct((ATTN_HEADS, ATTN_BLOCK, 2 * ATTN_BLOCK), F32),
                   jax.ShapeDtypeStruct((1, ATTN_HEADS), F32)],
        compiler_params=_params(("arbitrary", "arbitrary")),
    )(proj, proj, bias, sinks, d_ya)


def _split3(x):
    hi = x.astype(BF16)
    r1 = x - hi.astype(F32)
    mid = r1.astype(BF16)
    lo = (r1 - mid.astype(F32)).astype(BF16)
    return jnp.concatenate([hi, mid, lo], axis=1)


def _tri_sum(tri, x):
    y = _dot(tri, _split3(x))
    return y[:, :128] + y[:, 128:256] + y[:, 256:]


def _interleave(stages):
    results = [None] * len(stages)
    live = list(range(len(stages)))
    while live:
        still = []
        for idx in live:
            try:
                next(stages[idx])
                still.append(idx)
            except StopIteration as done:
                results[idx] = done.value
        live = still
    return results


def _hgrn_chunk(hq, hf, hi, lb):
    C = HGRN_CHUNK
    t = lax.broadcasted_iota(jnp.int32, (C, C), 0)
    s = lax.broadcasted_iota(jnp.int32, (C, C), 1)
    causal = s <= t
    sf = _sigmoid(hf)
    f = lb + (1.0 - lb) * sf
    lf = jnp.log(f)
    yield
    G = _tri_sum(causal.astype(BF16), lf)
    sq = _sigmoid(hq)
    qs = hq * sq
    k = 1.0 - f
    rowblk = lax.broadcasted_iota(jnp.int32, (C, 1), 0) // HGRN_SUB
    qt, kt, eq, ek = [], [], [], []
    for i in range(C // HGRN_SUB):
        lo = HGRN_SUB * i
        ref = G[lo + HGRN_SUB // 2:lo + HGRN_SUB // 2 + 1, :]
        eq_i = jnp.exp(G[lo:lo + HGRN_SUB] - ref)
        ek_i = jnp.exp(jnp.where(rowblk <= i, ref - G, 0.0))
        eq.append(eq_i)
        ek.append(ek_i)
        qt.append((qs[lo:lo + HGRN_SUB] * eq_i).astype(BF16))
        kt.append((k * ek_i).astype(BF16))
    yield
    A = jnp.concatenate([_dot(qt[i], kt[i], NT) for i in range(C // HGRN_SUB)], axis=0)
    A = jnp.where(causal, A, 0.0)
    glast = G[C - 1:C, :]
    eG = jnp.exp(G)
    edec = jnp.exp(glast - G)
    return dict(causal=causal, sf=sf, f=f, G=G, sq=sq, qs=qs, k=k, qt=qt, kt=kt, eq=eq, ek=ek, A=A,
                glast=glast, eG=eG, edec=edec, qhat=qs * eG, kdec=k * edec, v=hi)


def _hgrn_specs(nseq, ng, rows, reverse):
    W = HGRN_PAR
    nblk = W * HGRN_DIM // HGRN_COLS
    gi = (lambda g: ng - 1 - g) if reverse else (lambda g: g)

    def cols(c0):
        return [pl.BlockSpec((rows, HGRN_COLS),
                             lambda h, b, g, q=q: (b * ng + gi(g), c0 // HGRN_COLS + h * nblk + q)) for q in range(nblk)]

    return gi, (cols(C_HQ) + cols(C_HF) + cols(C_HI) + cols(C_HG)
                + [pl.BlockSpec((2, W * HGRN_DIM), lambda h, b, g: (0, h)),
                   pl.BlockSpec((W, 1, HGRN_DIM), lambda h, b, g: (h, 0, 0))])


def _hgrn_operands(refs):
    nblk = HGRN_PAR * HGRN_DIM // HGRN_COLS
    per = HGRN_COLS // HGRN_DIM

    def reader(group):
        def read(rs, w):
            lo = HGRN_DIM * (w % per)
            return group[w // per][rs, lo:lo + HGRN_DIM].astype(F32)
        return read

    readers = [reader(refs[nblk * a:nblk * (a + 1)]) for a in range(4)]
    return readers, refs[4 * nblk], refs[4 * nblk + 1], refs[4 * nblk + 2:]


def _hgrn_fwd(proj, lb_logits, gain3, nseq, S, shards):
    T = proj.shape[0]
    W = HGRN_PAR
    nc = S // HGRN_CHUNK
    cg = min(8, nc)
    ng = nc // cg
    rows = cg * HGRN_CHUNK
    n = len(shards)
    nh = HGRN_HEADS // W

    def body(*all_refs):
        (hq, hf, hi, hg), lbl_ref, gain_ref, refs = _hgrn_operands(all_refs)
        ins = refs[:n]
        o_ref, yh_ref, st_ref = refs[n:n + 3]
        outs = refs[n + 3:2 * n + 3]
        state, send_sems, recv_sems, local_sems = refs[2 * n + 3:]
        step = (pl.program_id(0), pl.program_id(1), pl.program_id(2))
        ops = _GatherOps(ins, outs, send_sems, recv_sems, local_sems)

        @pl.when((step[0] == 0) & (step[1] == 0) & (step[2] == 0))
        def _():
            for a in range(n):
                ops.local(a).start()
                ops.to_sibling(a).start()
                for q in range(3):
                    ops.to_chip(a, q).start()

        @pl.when(pl.program_id(2) == 0)
        def _():
            state[...] = jnp.zeros(state.shape, F32)

        lb_all = _sigmoid(lbl_ref[0:1, :] - lbl_ref[1:2, :])

        def head(w, c):
            rs = pl.ds(pl.multiple_of(c * HGRN_CHUNK, HGRN_CHUNK), HGRN_CHUNK)
            ls = slice(HGRN_DIM * w, HGRN_DIM * (w + 1))
            ch = yield from _hgrn_chunk(hq(rs, w), hf(rs, w), hi(rs, w), lb_all[:, ls])
            st = state[w]
            st_ref[w, c] = st
            vb = ch["v"].astype(BF16)
            yield
            o = _dot(ch["qhat"].astype(BF16), st.astype(BF16), NT) + _dot(ch["A"].astype(BF16), vb)
            state[w] = st * jnp.exp(ch["glast"]) + _dot(vb, ch["kdec"].astype(BF16), TN)
            o_ref[rs, ls] = o
            r = lax.rsqrt(jnp.mean(o * o, axis=-1, keepdims=True) + NORM_EPS)
            gate = hg(rs, w)
            yh_ref[rs, ls] = (o * r * gain_ref[w] * (gate * _sigmoid(gate))).astype(BF16)

        def chunk(c, carry):
            _interleave([head(w, c) for w in range(W)])
            return carry

        lax.fori_loop(0, cg, chunk, 0)

        @pl.when((step[0] == nh - 1) & (step[1] == nseq - 1) & (step[2] == ng - 1))
        def _():
            for a in range(n):
                ops.local(a).wait()
                ops.from_sibling(a).wait_recv()
                for q in range(3):
                    ops.from_chip(a, q).wait_recv()
                    ops.forward(a, q).start()
            for a in range(n):
                for q in range(3):
                    ops.forwarded(a, q).wait_recv()
                for cp in ops.sends(a):
                    cp.wait_send()

    _, in_specs = _hgrn_specs(nseq, ng, rows, False)
    out_row = lambda: pl.BlockSpec((rows, W * HGRN_DIM), lambda h, b, g: (b * ng + g, h))
    anyspec = lambda: pl.BlockSpec(memory_space=pl.ANY)
    return pl.pallas_call(
        body, name="hgrn_fwd",
        grid=(nh, nseq, ng),
        in_specs=in_specs + [anyspec() for _ in shards],
        out_specs=[out_row(), out_row(),
                   pl.BlockSpec((None, W, cg, HGRN_DIM, HGRN_DIM), lambda h, b, g: (b, h, g, 0, 0))]
                  + [anyspec() for _ in shards],
        out_shape=[jax.ShapeDtypeStruct((T, 1024), F32),
                   jax.ShapeDtypeStruct((T, 1024), BF16),
                   jax.ShapeDtypeStruct((nseq, HGRN_HEADS, nc, HGRN_DIM, HGRN_DIM), F32)]
                  + [jax.ShapeDtypeStruct((N_DEV * s.shape[0], s.shape[1]), s.dtype) for s in shards],
        scratch_shapes=[pltpu.VMEM((W, HGRN_DIM, HGRN_DIM), F32),
                        pltpu.SemaphoreType.DMA((n, _GATHER_SEMS)), pltpu.SemaphoreType.DMA((n, _GATHER_SEMS)),
                        pltpu.SemaphoreType.DMA((n,))],
        compiler_params=_params(("arbitrary", "arbitrary", "arbitrary")),
    )(*([proj] * (4 * W * HGRN_DIM // HGRN_COLS)), lb_logits, gain3, *shards)


def _hgrn_bwd(proj, lb_logits, gain3, o, d_yh, states, nseq, S, chip_sums):
    T = proj.shape[0]
    W = HGRN_PAR
    n = len(chip_sums)
    nh = HGRN_HEADS // W
    assert nh == 1
    nc = S // HGRN_CHUNK
    cg = min(8, nc)
    ng = nc // cg
    rows = cg * HGRN_CHUNK
    C = HGRN_CHUNK
    nsub = C // HGRN_SUB

    def body(*all_refs):
        (hq_of, hf_of, hi_of, hg_of), lbl_ref, gain_ref, refs = _hgrn_operands(all_refs)
        o_ref, dyh_ref, st_ref = refs[:3]
        sums_in = refs[3:3 + n]
        dh4_ref, dgain_ref, dlbl_ref = refs[3 + n:6 + n]
        sums_out = refs[6 + n:6 + 2 * n]
        dstate, dlb_acc, send_sems, recv_sems, local_sems = refs[6 + 2 * n:]
        h, b, g = pl.program_id(0), pl.program_id(1), pl.program_id(2)
        exchange = _ChipExchange(sums_in, sums_out, send_sems, recv_sems, local_sems)

        @pl.when((h == 0) & (b == 0) & (g == 0))
        def _():
            exchange.start()

        @pl.when(g == 0)
        def _():
            dstate[...] = jnp.zeros(dstate.shape, F32)

        @pl.when((b == 0) & (g == 0))
        def _():
            dgain_ref[...] = jnp.zeros(dgain_ref.shape, F32)
            dlb_acc[...] = jnp.zeros(dlb_acc.shape, F32)

        lb_all = _sigmoid(lbl_ref[0:1, :] - lbl_ref[1:2, :])

        anti = (lax.broadcasted_iota(jnp.int32, (C, C), 1) >= lax.broadcasted_iota(jnp.int32, (C, C), 0)).astype(BF16)
        last_row = lax.broadcasted_iota(jnp.int32, (C, 1), 0) == C - 1

        def head_load(w, c):
            rs = pl.ds(pl.multiple_of(c * C, C), C)
            ls = slice(HGRN_DIM * w, HGRN_DIM * (w + 1))
            return dict(hq=hq_of(rs, w), hf=hf_of(rs, w), hi=hi_of(rs, w), hg=hg_of(rs, w), lb=lb_all[:, ls],
                        gain=gain_ref[w], ov=o_ref[rs, ls], dyh=dyh_ref[rs, ls].astype(F32),
                        st=st_ref[w, c], dst=dstate[w])

        def head_math(v):
            lb, gain, hq, hg = v["lb"], v["gain"], v["hq"], v["hg"]
            ch = yield from _hgrn_chunk(hq, v["hf"], v["hi"], lb)
            ov, dyh = v["ov"], v["dyh"]
            r = lax.rsqrt(jnp.mean(ov * ov, axis=-1, keepdims=True) + NORM_EPS)
            on = ov * r
            sg = _sigmoid(hg)
            doh = dyh * (hg * sg)
            out = dict(dhg=(dyh * on * gain * (sg * (1.0 + hg * (1.0 - sg)))).astype(BF16),
                       dgain=jnp.sum(doh * on, axis=0, keepdims=True))
            don = doh * gain
            do = r * (don - on * jnp.mean(don * on, axis=-1, keepdims=True))
            dob = do.astype(BF16)
            st, dst = v["st"], v["dst"]
            stb, dstb = st.astype(BF16), dst.astype(BF16)
            vb = ch["v"].astype(BF16)
            qhatb = ch["qhat"].astype(BF16)
            eglast = jnp.exp(ch["glast"])
            yield
            dqhat = _dot(dob, stb)
            dkdec = _dot(vb, dstb)
            dv = _dot(ch["kdec"].astype(BF16), dstb, NT)
            deg = jnp.sum(dst * st, axis=0, keepdims=True)
            out["dstate"] = dst * eglast + _dot(dob, qhatb, TN)
            dA = jnp.where(ch["causal"], _dot(dob, vb, NT), 0.0)
            dv = dv + _dot(ch["A"].astype(BF16), dob, TN)
            dAb = dA.astype(BF16)
            yield
            dqs_parts, dgq_parts = [], []
            dk_intra, dgk = None, None
            for i in range(nsub):
                dA_i = dAb[HGRN_SUB * i:HGRN_SUB * (i + 1)]
                dqt = _dot(dA_i, ch["kt"][i])
                dkt = _dot(dA_i, ch["qt"][i], TN)
                dqs_parts.append(dqt * ch["eq"][i])
                dgq_parts.append(dqt * ch["qt"][i].astype(F32))
                dk_i = dkt * ch["ek"][i]
                dgk_i = dkt * ch["kt"][i].astype(F32)
                dk_intra = dk_i if dk_intra is None else dk_intra + dk_i
                dgk = dgk_i if dgk is None else dgk + dgk_i
            dqs_inter = dqhat * ch["eG"]
            dk_state = dkdec * ch["edec"]
            dqs = jnp.concatenate(dqs_parts, axis=0) + dqs_inter
            dk = dk_intra + dk_state
            dG = jnp.concatenate(dgq_parts, axis=0) - dgk + ch["qs"] * dqs_inter - ch["k"] * dk_state
            tail = jnp.sum(dkdec * ch["kdec"], axis=0, keepdims=True) + deg * eglast
            dG = dG + jnp.where(last_row, tail, 0.0)
            yield
            dlf = _tri_sum(anti, dG)
            df = dlf / ch["f"] - dk
            sf, sq = ch["sf"], ch["sq"]
            out["dhf"] = (df * (1.0 - lb) * sf * (1.0 - sf)).astype(BF16)
            out["dlb"] = jnp.sum(df * (1.0 - sf), axis=0, keepdims=True)
            out["dhq"] = (dqs * (sq * (1.0 + hq * (1.0 - sq)))).astype(BF16)
            out["dhi"] = dv.astype(BF16)
            return out

        def head_store(w, c, out):
            rs = pl.ds(pl.multiple_of(c * C, C), C)
            ls = slice(HGRN_DIM * w, HGRN_DIM * (w + 1))
            for a, key in enumerate(("dhq", "dhf", "dhi", "dhg")):
                lo = HGRN_HEADS * HGRN_DIM * a + HGRN_DIM * w
                dh4_ref[rs, lo:lo + HGRN_DIM] = out[key]
            dstate[w] = out["dstate"]
            dgain_ref[w] += out["dgain"]
            dlb_acc[:, ls] += out["dlb"]

        def chunk(cc, carry):
            c = cg - 1 - cc
            outs = _interleave([head_math(v) for v in [head_load(w, c) for w in range(W)]])
            for w in range(W):
                head_store(w, c, outs[w])
            return carry

        lax.fori_loop(0, cg, chunk, 0)

        @pl.when((b == nseq - 1) & (g == ng - 1))
        def _():
            dl0 = dlb_acc[...] * lb_all * (1.0 - lb_all)
            dlbl_ref[0:1, :] = dl0
            dlbl_ref[1:2, :] = -dl0

        @pl.when((h == nh - 1) & (b == nseq - 1) & (g == ng - 1))
        def _():
            exchange.wait()

    gi, in_specs = _hgrn_specs(nseq, ng, rows, True)
    row = lambda: pl.BlockSpec((rows, W * HGRN_DIM), lambda h, b, g: (b * ng + gi(g), h))
    anyspec = lambda: pl.BlockSpec(memory_space=pl.ANY)
    return pl.pallas_call(
        body, name="hgrn_bwd",
        grid=(nh, nseq, ng),
        in_specs=in_specs + [row(), row(),
                             pl.BlockSpec((None, W, cg, HGRN_DIM, HGRN_DIM), lambda h, b, g: (b, h, gi(g), 0, 0))]
                 + [anyspec() for _ in chip_sums],
        out_specs=[pl.BlockSpec((rows, 4 * W * HGRN_DIM), lambda h, b, g: (b * ng + gi(g), h)),
                   pl.BlockSpec((W, 1, HGRN_DIM), lambda h, b, g: (h, 0, 0)),
                   pl.BlockSpec((2, W * HGRN_DIM), lambda h, b, g: (0, h))] + [anyspec() for _ in chip_sums],
        out_shape=[jax.ShapeDtypeStruct((T, 4 * HGRN_HEADS * HGRN_DIM), BF16),
                   jax.ShapeDtypeStruct((HGRN_HEADS, 1, HGRN_DIM), F32),
                   jax.ShapeDtypeStruct((2, HGRN_HEADS * HGRN_DIM), F32)]
                  + [jax.ShapeDtypeStruct(s.shape, s.dtype) for s in chip_sums],
        scratch_shapes=[pltpu.VMEM((W, HGRN_DIM, HGRN_DIM), F32), pltpu.VMEM((1, W * HGRN_DIM), F32)]
                       + _ChipExchange.scratch(n),
        compiler_params=_params(("arbitrary", "arbitrary", "arbitrary")),
    )(*([proj] * (4 * W * HGRN_DIM // HGRN_COLS)), lb_logits, gain3, o, d_yh, states, *chip_sums)


def _gate_specs(tm):
    spec = lambda c0, q: pl.BlockSpec((tm, 512), lambda i: (i, c0 // 512 + q))
    return [spec(C_GA, q) for q in range(4)] + [spec(C_GH, q) for q in range(4)]


def _gates(refs):
    ga = jnp.concatenate([r[...] for r in refs[0:4]], axis=1).astype(F32)
    gh = jnp.concatenate([r[...] for r in refs[4:8]], axis=1).astype(F32)
    return ga, gh


def _branch_merge(proj, ya, yh, wa_t, wh_t):
    T = proj.shape[0]
    tm = min(512, T)

    def body(*refs):
        ya_ref, yh_ref, wa_ref, wh_ref, merged_ref, ua_ref, uh_ref = refs[8:]
        ga, gh = _gates(refs)
        ua = _dot(ya_ref[...], wa_ref[...], NT)
        uh = _dot(yh_ref[...], wh_ref[...], NT)
        merged_ref[...] = (_sigmoid(ga) * ua + _sigmoid(gh) * uh).astype(BF16)
        ua_ref[...] = ua.astype(BF16)
        uh_ref[...] = uh.astype(BF16)

    rowb = lambda w: pl.BlockSpec((tm, w), lambda i: (i, 0))
    full = lambda a: pl.BlockSpec(a.shape, lambda i: (0, 0))
    return pl.pallas_call(
        body, name="branch_merge",
        grid=(T // tm,),
        in_specs=_gate_specs(tm) + [rowb(1024), rowb(1024), full(wa_t), full(wh_t)],
        out_specs=[rowb(D_MODEL)] * 3,
        out_shape=[jax.ShapeDtypeStruct((T, D_MODEL), BF16)] * 3,
        compiler_params=_params(("arbitrary",)),
    )(*([proj] * 8), ya, yh, wa_t, wh_t)


def _out_norm_loss(merged, x2, tgt2, gpost, wout):
    T = merged.shape[0]
    tm = min(256, T)

    def body(m_ref, x_ref, t_ref, gpost_ref, wo_ref, dy_ref, dm_ref, dout_ref, loss_ref, dgpost_ref):
        @pl.when(pl.program_id(0) == 0)
        def _():
            loss_ref[...] = jnp.zeros(loss_ref.shape, F32)
            dgpost_ref[...] = jnp.zeros(dgpost_ref.shape, F32)

        y = _dot(m_ref[...], wo_ref[...])
        r2 = lax.rsqrt(jnp.mean(y * y, axis=-1, keepdims=True) + NORM_EPS)
        yn = y * r2
        gpost = gpost_ref[...]
        err = x_ref[...] + yn * gpost - t_ref[...]
        loss_ref[...] += jnp.sum(err * err, axis=0, keepdims=True)
        dout = err * (1.0 / D_MODEL)
        dout_ref[...] = dout
        dgpost_ref[...] += jnp.sum(dout * yn, axis=0, keepdims=True)
        dyn = dout * gpost
        dy = (r2 * (dyn - yn * jnp.mean(dyn * yn, axis=-1, keepdims=True))).astype(BF16)
        dy_ref[...] = dy
        dm_ref[...] = _dot(dy, wo_ref[...], NT).astype(BF16)

    rowb = lambda: pl.BlockSpec((tm, D_MODEL), lambda i: (i, 0))
    vec = lambda: pl.BlockSpec((1, D_MODEL), lambda i: (0, 0))
    return pl.pallas_call(
        body, name="out_norm_loss",
        grid=(T // tm,),
        in_specs=[rowb(), rowb(), rowb(), vec(), pl.BlockSpec(wout.shape, lambda i: (0, 0))],
        out_specs=[rowb(), rowb(), rowb(), vec(), vec()],
        out_shape=[jax.ShapeDtypeStruct((T, D_MODEL), BF16)] * 2
                  + [jax.ShapeDtypeStruct((T, D_MODEL), F32)] + [jax.ShapeDtypeStruct((1, D_MODEL), F32)] * 2,
        compiler_params=_params(("arbitrary",)),
    )(merged, x2, tgt2, gpost, wout)


def _branch_bwd(proj, dm, ua, uh, wa_t, wh_t):
    T = proj.shape[0]
    tm = min(256, T)

    def body(*refs):
        (dm_ref, ua_ref, uh_ref, wa_ref, wh_ref,
         dua_ref, duh_ref, dgg_ref, dya_ref, dyh_ref) = refs[8:]
        ga, gh = _gates(refs)
        sa, sh = _sigmoid(ga), _sigmoid(gh)
        dm = dm_ref[...].astype(F32)
        dua = (dm * sa).astype(BF16)
        duh = (dm * sh).astype(BF16)
        dua_ref[...] = dua
        duh_ref[...] = duh
        dgg_ref[:, :D_MODEL] = (dm * ua_ref[...].astype(F32) * (sa * (1.0 - sa))).astype(BF16)
        dgg_ref[:, D_MODEL:] = (dm * uh_ref[...].astype(F32) * (sh * (1.0 - sh))).astype(BF16)
        dya_ref[...] = _dot(dua, wa_ref[...]).astype(BF16)
        dyh_ref[...] = _dot(duh, wh_ref[...]).astype(BF16)

    rowb = lambda w: pl.BlockSpec((tm, w), lambda i: (i, 0))
    full = lambda a: pl.BlockSpec(a.shape, lambda i: (0, 0))
    return pl.pallas_call(
        body, name="branch_bwd",
        grid=(T // tm,),
        in_specs=_gate_specs(tm) + [rowb(D_MODEL)] * 3 + [full(wa_t), full(wh_t)],
        out_specs=[rowb(D_MODEL)] * 2 + [rowb(2 * D_MODEL)] + [rowb(1024)] * 2,
        out_shape=[jax.ShapeDtypeStruct((T, D_MODEL), BF16)] * 2 + [jax.ShapeDtypeStruct((T, 2 * D_MODEL), BF16)]
                  + [jax.ShapeDtypeStruct((T, 1024), BF16)] * 2,
        compiler_params=_params(("arbitrary",)),
    )(*([proj] * 8), dm, ua, uh, wa_t, wh_t)


def _tn_matmul(L, R, bm, bn, name):
    T, M = L.shape
    N = R.shape[1]

    def body(l_ref, r_ref, out_ref):
        out_ref[...] = _dot(l_ref[...], r_ref[...], TN).astype(BF16)

    return pl.pallas_call(
        body, name=name,
        grid=(N // bn, M // bm),
        in_specs=[pl.BlockSpec((T, bm), lambda j, i: (0, i)),
                  pl.BlockSpec((T, bn), lambda j, i: (0, j))],
        out_specs=pl.BlockSpec((bm, bn), lambda j, i: (i, j)),
        out_shape=jax.ShapeDtypeStruct((M, N), BF16),
        compiler_params=_params(("arbitrary", "arbitrary")),
    )(L, R)


def _tn_matmul_pieces(pieces, R, bm, bn, name):
    T, N = R.shape
    M = sum(p.shape[1] for p in pieces)
    out, row0 = None, 0
    for q, L in enumerate(pieces):
        off = row0 // bm

        def body(*refs):
            refs[-1][...] = _dot(refs[0][...], refs[1][...], TN).astype(BF16)

        prev = [] if out is None else [out]
        out = pl.pallas_call(
            body, name="%s_%d" % (name, q),
            grid=(N // bn, L.shape[1] // bm),
            in_specs=[pl.BlockSpec((T, bm), lambda j, i: (0, i)), pl.BlockSpec((T, bn), lambda j, i: (0, j))]
                     + [pl.BlockSpec(memory_space=pl.ANY) for _ in prev],
            out_specs=pl.BlockSpec((bm, bn), lambda j, i, off=off: (off + i, j)),
            out_shape=jax.ShapeDtypeStruct((M, N), BF16),
            input_output_aliases={2: 0} if prev else {},
            compiler_params=_params(("arbitrary", "arbitrary")),
        )(L, R, *prev)
        row0 += L.shape[1]
    return out


def _dh_prenorm_bwd(pieces, wt_in, x2, dout, gpre, chip_sums):
    T = x2.shape[0]
    tm = min(1024, T)
    ne = 4
    te = tm // ne
    tk = 512
    nk = IN_WIDTH // tk
    nt = T // tm
    n = len(chip_sums)
    npiece = len(pieces)
    counts = [p.shape[1] // tk for p in pieces]
    starts = [sum(counts[:q]) for q in range(npiece)]
    assert sum(counts) == nk and all(p.shape[1] % tk == 0 for p in pieces)

    def body(*all_refs):
        piece_refs = all_refs[:npiece]
        w_ref, x_ref, dout_ref, g_ref = all_refs[npiece:npiece + 4]
        refs = all_refs[npiece + 4:]
        ins = refs[:n]
        gx_ref, dg_ref = refs[n], refs[n + 1]
        outs = refs[n + 2:2 * n + 2]
        acc, send_sems, recv_sems, local_sems = refs[2 * n + 2:]
        i, k = pl.program_id(0), pl.program_id(1)
        exchange = _ChipExchange(ins, outs, send_sems, recv_sems, local_sems)

        @pl.when((i == 0) & (k == 0))
        def _():
            dg_ref[...] = jnp.zeros(dg_ref.shape, F32)
            exchange.start()

        @pl.when((i == nt - 1) & (k == nk + ne - 1))
        def _():
            exchange.wait()

        @pl.when(k == 0)
        def _():
            acc[...] = jnp.zeros(acc.shape, F32)

        for q in range(npiece):
            @pl.when((k >= starts[q]) & (k < starts[q] + counts[q]))
            def _(q=q):
                acc[...] += _dot(piece_refs[q][...], w_ref[...])

        @pl.when(k >= nk)
        def _():
            dh = acc[pl.ds(pl.multiple_of((k - nk) * te, te), te), :]
            x = x_ref[...]
            r = lax.rsqrt(jnp.mean(x * x, axis=-1, keepdims=True) + NORM_EPS)
            xn = x * r
            dg_ref[...] += jnp.sum(dh * xn, axis=0, keepdims=True)
            dxn = dh * g_ref[...]
            gx_ref[...] = dout_ref[...] + r * (dxn - xn * jnp.mean(dxn * xn, axis=-1, keepdims=True))

    piece_spec = lambda q: pl.BlockSpec((tm, tk), lambda i, k: (i, jnp.clip(k - starts[q], 0, counts[q] - 1)))
    rowb = lambda: pl.BlockSpec((te, D_MODEL), lambda i, k: (ne * i + jnp.clip(k - nk, 0, ne - 1), 0))
    vec = lambda: pl.BlockSpec((1, D_MODEL), lambda i, k: (0, 0))
    anyspec = lambda: pl.BlockSpec(memory_space=pl.ANY)
    return pl.pallas_call(
        body, name="dh_prenorm_bwd",
        grid=(nt, nk + ne),
        in_specs=[piece_spec(q) for q in range(npiece)]
                 + [pl.BlockSpec((tk, D_MODEL), lambda i, k: (jnp.minimum(k, nk - 1), 0)),
                    rowb(), rowb(), vec()] + [anyspec() for _ in chip_sums],
        out_specs=[rowb(), vec()] + [anyspec() for _ in chip_sums],
        out_shape=[jax.ShapeDtypeStruct((T, D_MODEL), F32), jax.ShapeDtypeStruct((1, D_MODEL), F32)]
                  + [jax.ShapeDtypeStruct(s.shape, s.dtype) for s in chip_sums],
        scratch_shapes=[pltpu.VMEM((tm, D_MODEL), F32)] + _ChipExchange.scratch(n),
        compiler_params=_params(("arbitrary", "arbitrary")),
    )(*pieces, wt_in, x2, dout, gpre, *chip_sums)


def _sum_slots(recv, br, name):
    nslot, R, C = recv.shape

    def body(r_ref, out_ref):
        acc = r_ref[0].astype(F32)
        for s in range(1, nslot):
            acc = acc + r_ref[s].astype(F32)
        out_ref[...] = acc

    return pl.pallas_call(
        body, name=name,
        grid=(R // br,),
        in_specs=[pl.BlockSpec((nslot, br, C), lambda i: (0, i, 0))],
        out_specs=pl.BlockSpec((br, C), lambda i: (i, 0)),
        out_shape=jax.ShapeDtypeStruct((R, C), F32),
        compiler_params=_params(("arbitrary",)),
    )(recv)


def _adamw_math(w, g, m, v):
    m = ADAM_B1 * m + (1.0 - ADAM_B1) * g
    v = ADAM_B2 * v + (1.0 - ADAM_B2) * (g * g)
    m_hat = m / (1.0 - ADAM_B1 ** ADAM_STEP)
    v_hat = v / (1.0 - ADAM_B2 ** ADAM_STEP)
    delta = -ADAM_LR * (m_hat / (jnp.sqrt(v_hat) + ADAM_EPS) + ADAM_WD * w)
    return delta, m, v


def _adamw(w, g, m, v, br, name):
    R, C = g.shape
    lead = (None,) * (w.ndim - 2)

    def body(w_ref, g_ref, m_ref, v_ref, d_ref, nm_ref, nv_ref):
        d_ref[...], nm_ref[...], nv_ref[...] = _adamw_math(w_ref[...], g_ref[...], m_ref[...], v_ref[...])

    spec = lambda: pl.BlockSpec(lead + (br, C), lambda i: (0,) * len(lead) + (i, 0))
    return pl.pallas_call(
        body, name=name,
        grid=(R // br,),
        in_specs=[spec(), pl.BlockSpec((br, C), lambda i: (i, 0)), spec(), spec()],
        out_specs=[spec(), spec(), spec()],
        out_shape=[jax.ShapeDtypeStruct(w.shape, F32)] * 3,
        compiler_params=_params(("arbitrary",)),
    )(w, g, m, v)


def _sibling_exchange(partials, name):
    n = len(partials)

    def body(*refs):
        ins, outs = refs[:n], refs[n:2 * n]
        send_sems, recv_sems = refs[2 * n:]
        x, y, c = _place()

        def copy(a, p):
            return pltpu.make_async_remote_copy(
                src_ref=ins[a].at[p, 1 - c], dst_ref=outs[a].at[p],
                send_sem=send_sems.at[a, p], recv_sem=recv_sems.at[a, p],
                device_id=(x, y, 1 - c), device_id_type=pl.DeviceIdType.MESH)

        copies = [copy(a, p) for p in range(4) for a in range(n)]
        for cp in copies:
            cp.start()
        for cp in copies:
            cp.wait()

    anyspec = lambda: pl.BlockSpec(memory_space=pl.ANY)
    return pl.pallas_call(
        body, name=name,
        in_specs=[anyspec() for _ in partials],
        out_specs=[anyspec() for _ in partials],
        out_shape=[jax.ShapeDtypeStruct((4,) + p.shape[2:], p.dtype) for p in partials],
        scratch_shapes=[pltpu.SemaphoreType.DMA((n, 4)), pltpu.SemaphoreType.DMA((n, 4))],
    )(*partials)


def _chip_sum(partial, from_sibling, br, name):
    _, _, R, C = partial.shape
    cls = lax.axis_index("c").astype(jnp.int32).reshape(1)

    def body(c_ref, mine_ref, sib_ref, out_ref):
        out_ref[...] = (mine_ref[...].astype(F32) + sib_ref[...].astype(F32)).astype(BF16)

    grid_spec = pltpu.PrefetchScalarGridSpec(
        num_scalar_prefetch=1,
        grid=(4, R // br),
        in_specs=[pl.BlockSpec((None, None, br, C), lambda p, i, c: (p, c[0], i, 0)),
                  pl.BlockSpec((None, br, C), lambda p, i, c: (p, i, 0))],
        out_specs=pl.BlockSpec((None, br, C), lambda p, i, c: (p, i, 0)),
    )
    return pl.pallas_call(
        body, name=name, grid_spec=grid_spec,
        out_shape=jax.ShapeDtypeStruct((4, R, C), BF16),
        compiler_params=_params(("arbitrary", "arbitrary")),
    )(cls, partial, from_sibling)


def _all_reduce_small(packed):
    shape = packed.shape

    def body(in_ref, out_ref, slots, send_sems, recv_sems):
        x, y, c = _place()
        my_slot = 4 * x + 2 * y + c

        def peer(k):
            return (x ^ ((k >> 2) & 1), y ^ ((k >> 1) & 1), c ^ (k & 1))

        def copy(k):
            p = peer(k)
            return pltpu.make_async_remote_copy(
                src_ref=in_ref, dst_ref=slots.at[my_slot],
                send_sem=send_sems.at[k - 1], recv_sem=recv_sems.at[k - 1],
                device_id=p, device_id_type=pl.DeviceIdType.MESH)

        def arrival(k):
            p = peer(k)
            return pltpu.make_async_remote_copy(
                src_ref=in_ref, dst_ref=slots.at[4 * p[0] + 2 * p[1] + p[2]],
                send_sem=send_sems.at[k - 1], recv_sem=recv_sems.at[k - 1],
                device_id=p, device_id_type=pl.DeviceIdType.MESH)

        sends = [copy(k) for k in range(1, N_DEV)]
        for cp in sends:
            cp.start()
        slots[my_slot] = in_ref[...]
        for k in range(1, N_DEV):
            arrival(k).wait_recv()
        for cp in sends:
            cp.wait_send()
        acc = slots[0]
        for s in range(1, N_DEV):
            acc = acc + slots[s]
        out_ref[...] = acc

    return pl.pallas_call(
        body, name="all_reduce_small",
        in_specs=[pl.BlockSpec(memory_space=pltpu.VMEM)],
        out_specs=pl.BlockSpec(memory_space=pltpu.VMEM),
        out_shape=jax.ShapeDtypeStruct(shape, F32),
        scratch_shapes=[pltpu.VMEM((N_DEV,) + shape, F32),
                        pltpu.SemaphoreType.DMA((7,)), pltpu.SemaphoreType.DMA((7,))],
    )(packed)


def _pack_small(norm_pre, norm_post, lb_logits, hgrn_norm, rel_bias, sinks, extra=None):
    tail = [hgrn_norm.reshape(1, 1024), rel_bias.reshape(1, 512), sinks.reshape(1, 16)]
    used = 1024 + 512 + 16
    if extra is not None:
        tail.append(extra.reshape(1, 1))
        used += 1
    tail.append(jnp.zeros((1, D_MODEL - used), F32))
    rows = [norm_pre.reshape(1, D_MODEL), norm_post.reshape(1, D_MODEL), lb_logits.reshape(1, D_MODEL),
            jnp.concatenate(tail, axis=1), jnp.zeros((4, D_MODEL), F32)]
    return jnp.concatenate(rows, axis=0)


def _unpack_small(p):
    return (p[0:1], p[3, 1024:1536].reshape(REL_BUCKETS, ATTN_HEADS), p[3:4, 1536:1552],
            p[2].reshape(2, 1024), p[3, 0:1024].reshape(1, HGRN_HEADS, HGRN_DIM), p[1:2])


def _local_step(nseq, S, x2, tgt2, proj, h, rel_bias, attn_sinks, lb_logits, hgrn_norm, norm_post, shards):
    nb = S // ATTN_BLOCK
    bucket = jnp.asarray(_t5_bucket_table())
    gain3 = hgrn_norm.reshape(HGRN_HEADS, 1, HGRN_DIM)

    bias = _bias_table(rel_bias, bucket)
    ya = _attn_fwd(proj, bias, attn_sinks, nseq, nb)
    o, yh, states, wout, wa_t, wh_t = _hgrn_fwd(proj, lb_logits, gain3, nseq, S, shards)
    merged, ua, uh = _branch_merge(proj, ya, yh, wa_t, wh_t)
    dy, dm, dout, loss_cols, d_gpost = _out_norm_loss(merged, x2, tgt2, norm_post, wout)
    dua, duh, dgg, dya, dyh = _branch_bwd(proj, dm, ua, uh, wa_t, wh_t)

    p_out = _tn_matmul(merged, dy, 256, 1024, "dw_out")
    p_a = _tn_matmul(dua, ya, 256, 1024, "dw_branch_attn")
    p_h = _tn_matmul(duh, yh, 256, 1024, "dw_branch_hgrn")
    small_sums = _chip_sums([p_out, p_a, p_h], (128, 128, 128), ("dw_out", "dw_branch_attn", "dw_branch_hgrn"),
                            "sibling_exchange_small")

    dq, dkv, dg, dbias, d_sinks = _attn_bwd(proj, bias, attn_sinks, dya, nseq, nb)
    d_rel_bias = _bias_table_bwd(dbias, bucket)
    dh4, d_gain, d_lbl, r_out, r_a, r_h = _hgrn_bwd(proj, lb_logits, gain3, o, dyh, states, nseq, S, small_sums)
    dproj = [dq, dkv, dg, dh4, dgg]
    p_in = _tn_matmul_pieces(dproj, h, 512, 1024, "dw_in")
    return dproj, dout, p_in, r_out, r_a, r_h, d_gpost, d_lbl, d_gain, d_rel_bias, d_sinks, loss_cols


def _chip_sums(partials, block_rows, names, exchange_name):
    split = [p.reshape(4, 2, p.shape[0] // N_DEV, p.shape[1]) for p in partials]
    from_sibling = _sibling_exchange(split, exchange_name)
    return [_chip_sum(p, f, br, "chip_sum_" + nm) for p, f, br, nm in zip(split, from_sibling, block_rows, names)]


def kernel(x, norm_pre, w_in, rel_bias, attn_sinks, lb_logits, hgrn_norm, w_branch_attn, w_branch_hgrn, w_out, norm_post, loss_target, m_norm_pre, m_w_in, m_rel_bias, m_attn_sinks, m_lb_logits, m_hgrn_norm, m_w_branch_attn, m_w_branch_hgrn, m_w_out, m_norm_post, v_norm_pre, v_w_in, v_rel_bias, v_attn_sinks, v_lb_logits, v_hgrn_norm, v_w_branch_attn, v_w_branch_hgrn, v_w_out, v_norm_post):
    nseq, S, _ = x.shape
    T = nseq * S
    x2 = x.reshape(T, D_MODEL)
    tgt2 = loss_target.reshape(T, D_MODEL)

    h = _prenorm(x2, norm_pre)
    proj, wt_in = _gather_inproj(h, w_in[0].T.astype(BF16))
    shards = [w_out[0].astype(BF16), w_branch_attn[0].T.astype(BF16), w_branch_hgrn[0].T.astype(BF16)]

    (dproj, dout, p_in, r_out, r_a, r_h, d_gpost, d_lbl, d_gain, d_rel_bias, d_sinks, loss_cols) = _local_step(
        nseq, S, x2, tgt2, proj, h, rel_bias, attn_sinks, lb_logits, hgrn_norm, norm_post, shards)

    in_sums = _chip_sums([p_in], (192,), ("dw_in",), "sibling_exchange_w_in")
    grad_x2, d_gpre, r_in = _dh_prenorm_bwd(dproj, wt_in, x2, dout, norm_pre, in_sums)
    g_in_t = _sum_slots(r_in, 192, "sum_dw_in")
    g_out = _sum_slots(r_out, 128, "sum_dw_out")
    g_a = _sum_slots(r_a, 128, "sum_dw_branch_attn").T
    g_h = _sum_slots(r_h, 128, "sum_dw_branch_hgrn").T

    loss_part = 0.5 / D_MODEL * jnp.sum(loss_cols)
    packed = _pack_small(d_gpre, d_gpost, d_lbl, d_gain, d_rel_bias, d_sinks, extra=loss_part)
    total = _all_reduce_small(packed)
    loss = total[3, 1024 + 512 + 16]
    sm_w = _pack_small(norm_pre, norm_post, lb_logits, hgrn_norm, rel_bias, attn_sinks)
    sm_m = _pack_small(m_norm_pre, m_norm_post, m_lb_logits, m_hgrn_norm, m_rel_bias, m_attn_sinks)
    sm_v = _pack_small(v_norm_pre, v_norm_post, v_lb_logits, v_hgrn_norm, v_rel_bias, v_attn_sinks)
    sm_d, sm_nm, sm_nv = _adamw(sm_w, total, sm_m, sm_v, 8, "adamw_small")

    t = lambda a: jnp.swapaxes(a, 1, 2)
    d_in, nm_in, nv_in = map(t, _adamw(t(w_in), g_in_t, t(m_w_in), t(v_w_in), 192, "adamw_w_in"))
    d_out, nm_out, nv_out = _adamw(w_out, g_out, m_w_out, v_w_out, 128, "adamw_w_out")
    d_a, nm_a, nv_a = _adamw(w_branch_attn, g_a, m_w_branch_attn, v_w_branch_attn, 256, "adamw_w_branch_attn")
    d_h, nm_h, nv_h = _adamw(w_branch_hgrn, g_h, m_w_branch_hgrn, v_w_branch_hgrn, 256, "adamw_w_branch_hgrn")

    def group(small, big_in, big_a, big_h, big_out):
        npre, rb, sk, lbl, hn, npost = _unpack_small(small)
        return (npre, big_in, rb, sk, lbl, hn, big_a, big_h, big_out, npost)

    return (loss, grad_x2.reshape(nseq, S, D_MODEL),
            *group(total, t(g_in_t[None]), g_a[None], g_h[None], g_out[None]),
            *group(sm_d, d_in, d_a, d_h, d_out),
            *group(sm_nm, nm_in, nm_a, nm_h, nm_out),
            *group(sm_nv, nv_in, nv_a, nv_h, nv_out))
```

```python
import functools
import math

import numpy as np
import jax
import jax.numpy as jnp
from jax import lax
from jax.experimental import pallas as pl
from jax.experimental.pallas import tpu as pltpu

F32 = jnp.float32
BF16 = jnp.bfloat16

D_MODEL = 2048
ATTN_HEADS = 16
ATTN_HEAD_DIM = 64
ATTN_GROUP = 4
ATTN_BLOCK = 128
HGRN_HEADS = 8
HGRN_DIM = 128
HGRN_CHUNK = 64
HGRN_SUB = 16
HGRN_PAR = 8
HGRN_COLS = 512
REL_BUCKETS = 32
REL_MAX_DIST = 128
NORM_EPS = 1e-6
C_AQ, C_AK, C_AV, C_AG = 0, 1024, 1280, 1536
C_HQ, C_HF, C_HI, C_HG = 2560, 3584, 4608, 5632
C_GA, C_GH = 6656, 8704
IN_WIDTH = 10752
N_DEV = 8
assert all(c0 % HGRN_COLS == 0 for c0 in (C_HQ, C_HF, C_HI, C_HG)) and (HGRN_PAR * HGRN_DIM) % HGRN_COLS == 0

ADAM_LR = 0.001
ADAM_B1 = 0.9
ADAM_B2 = 0.999
ADAM_EPS = 1e-08
ADAM_WD = 0.01
ADAM_STEP = 10

VMEM_LIMIT_V7X = 56 * 1024 * 1024
NEG_BIG = -1e30

NT = (((1,), (1,)), ((), ()))
TN = (((0,), (0,)), ((), ()))
NN = (((1,), (0,)), ((), ()))


def _dot(a, b, dims=NN):
    return lax.dot_general(a, b, dims, preferred_element_type=F32)


def _params(sem=None):
    return pltpu.CompilerParams(dimension_semantics=sem, vmem_limit_bytes=VMEM_LIMIT_V7X)


def _sigmoid(x):
    return 1.0 / (1.0 + jnp.exp(-x))


def _t5_bucket_table():
    qi = np.arange(ATTN_BLOCK)[:, None]
    si = np.arange(2 * ATTN_BLOCK)[None, :]
    dist = qi + ATTN_BLOCK - si
    max_exact = REL_BUCKETS // 2
    d = np.maximum(dist, 0)
    df = np.maximum(d, 1).astype(np.float32)
    large = max_exact + (np.log(df / np.float32(max_exact)).astype(np.float32)
                         / np.float32(math.log(REL_MAX_DIST / max_exact))
                         * np.float32(REL_BUCKETS - max_exact)).astype(np.int32)
    large = np.minimum(large, REL_BUCKETS - 1)
    return np.where(d < max_exact, d, large).astype(np.int32)


def _place():
    return lax.axis_index("x"), lax.axis_index("y"), lax.axis_index("c")


class _GatherOps:
    def __init__(self, ins, outs, send_sems, recv_sems, local_sems):
        self.ins, self.outs = ins, outs
        self.send_sems, self.recv_sems, self.local_sems = send_sems, recv_sems, local_sems
        x, y, c = _place()
        self.c = c
        self.me, self.sibling = (x, y, c), (x, y, 1 - c)
        self.chips = [(1 - x, y), (x, 1 - y), (1 - x, 1 - y)]

    def _rows(self, a, dev):
        r = self.ins[a].shape[0]
        return self.outs[a].at[pl.ds((4 * dev[0] + 2 * dev[1] + dev[2]) * r, r), :]

    def _copy(self, a, k, block, to, src=None):
        return pltpu.make_async_remote_copy(
            src_ref=self._rows(a, block) if src is None else src, dst_ref=self._rows(a, block),
            send_sem=self.send_sems.at[a, k], recv_sem=self.recv_sems.at[a, k],
            device_id=to, device_id_type=pl.DeviceIdType.MESH)

    def local(self, a):
        return pltpu.make_async_copy(self.ins[a], self._rows(a, self.me), self.local_sems.at[a])

    def to_sibling(self, a):
        return self._copy(a, 0, self.me, self.sibling, src=self.ins[a])

    def to_chip(self, a, q):
        return self._copy(a, 1 + q, self.me, (*self.chips[q], self.c), src=self.ins[a])

    def forward(self, a, q):
        return self._copy(a, 4 + q, (*self.chips[q], self.c), self.sibling)

    def from_sibling(self, a):
        return self._copy(a, 0, self.sibling, self.me)

    def from_chip(self, a, q):
        return self._copy(a, 1 + q, (*self.chips[q], self.c), self.me)

    def forwarded(self, a, q):
        return self._copy(a, 4 + q, (*self.chips[q], 1 - self.c), self.me)

    def sends(self, a):
        return [self.to_sibling(a)] + [self.to_chip(a, q) for q in range(3)] + [self.forward(a, q) for q in range(3)]


class _ChipExchange:
    def __init__(self, ins, outs, send_sems, recv_sems, local_sems):
        self.ins, self.outs = ins, outs
        self.send_sems, self.recv_sems, self.local_sems = send_sems, recv_sems, local_sems
        self.x, self.y, self.c = _place()
        self.my_chip = 2 * self.x + self.y

    def _peer(self, q):
        return (self.x ^ (q >> 1), self.y ^ (q & 1))

    def _copy(self, a, q, src_block, dst_slot):
        px, py = self._peer(q)
        return pltpu.make_async_remote_copy(
            src_ref=self.ins[a].at[src_block], dst_ref=self.outs[a].at[dst_slot],
            send_sem=self.send_sems.at[a, q - 1], recv_sem=self.recv_sems.at[a, q - 1],
            device_id=(px, py, self.c), device_id_type=pl.DeviceIdType.MESH)

    def _send(self, a, q):
        px, py = self._peer(q)
        return self._copy(a, q, 2 * px + py, self.my_chip)

    def _arrival(self, a, q):
        px, py = self._peer(q)
        return self._copy(a, q, self.my_chip, 2 * px + py)

    def _local(self, a):
        return pltpu.make_async_copy(self.ins[a].at[self.my_chip], self.outs[a].at[self.my_chip],
                                     self.local_sems.at[a])

    def start(self):
        for a in range(len(self.ins)):
            self._local(a).start()
            for q in range(1, 4):
                self._send(a, q).start()

    def wait(self):
        for a in range(len(self.ins)):
            for q in range(1, 4):
                self._arrival(a, q).wait_recv()
        for a in range(len(self.ins)):
            for q in range(1, 4):
                self._send(a, q).wait_send()
            self._local(a).wait()

    @staticmethod
    def scratch(n):
        return [pltpu.SemaphoreType.DMA((n, 3)), pltpu.SemaphoreType.DMA((n, 3)), pltpu.SemaphoreType.DMA((n,))]


_GATHER_SEMS = 7

INPROJ_TILE = 896


def _prenorm(x2, gpre):
    T = x2.shape[0]
    tm = min(512, T)

    def body(x_ref, g_ref, h_ref):
        x = x_ref[...]
        r = lax.rsqrt(jnp.mean(x * x, axis=-1, keepdims=True) + NORM_EPS)
        h_ref[...] = (x * r * g_ref[...]).astype(BF16)

    return pl.pallas_call(
        body, name="prenorm",
        grid=(T // tm,),
        in_specs=[pl.BlockSpec((tm, D_MODEL), lambda i: (i, 0)), pl.BlockSpec((1, D_MODEL), lambda i: (0, 0))],
        out_specs=pl.BlockSpec((tm, D_MODEL), lambda i: (i, 0)),
        out_shape=jax.ShapeDtypeStruct((T, D_MODEL), BF16),
        compiler_params=_params(("arbitrary",)),
    )(x2, gpre)


def _gather_own_pair(wt_shard):
    R = wt_shard.shape[0]

    def body(w_in, wt_full, pair, send_sems, recv_sems, local_sems):
        x, y, c = _place()
        sibling = (x, y, 1 - c)

        def places(cc):
            return (wt_full.at[pl.ds((2 * (2 * x + y) + cc) * R, R), :], pair.at[0, pl.ds(cc * R, R), :])

        def remote(k, dst):
            return pltpu.make_async_remote_copy(src_ref=w_in, dst_ref=dst, send_sem=send_sems.at[k],
                                                recv_sem=recv_sems.at[k], device_id=sibling,
                                                device_id_type=pl.DeviceIdType.MESH)

        mine = [pltpu.make_async_copy(w_in, dst, local_sems.at[k]) for k, dst in enumerate(places(c))]
        sends = [remote(k, dst) for k, dst in enumerate(places(c))]
        for cp in mine + sends:
            cp.start()
        for k, dst in enumerate(places(1 - c)):
            remote(k, dst).wait_recv()
        for cp in sends:
            cp.wait_send()
        for cp in mine:
            cp.wait()

    anyspec = lambda: pl.BlockSpec(memory_space=pl.ANY)
    return pl.pallas_call(
        body, name="gather_own_pair",
        in_specs=[anyspec()], out_specs=[anyspec(), anyspec()],
        out_shape=[jax.ShapeDtypeStruct((N_DEV * R, D_MODEL), BF16), jax.ShapeDtypeStruct((1, 2 * R, D_MODEL), BF16)],
        scratch_shapes=[pltpu.SemaphoreType.DMA((2,)), pltpu.SemaphoreType.DMA((2,)), pltpu.SemaphoreType.DMA((2,))],
    )(wt_shard)


def _inproj_phase(name, h, pairs, masks, proj_prev, wt_prev, wt_shard, relations):
    T = h.shape[0]
    tm = min(1024, T)
    nm = T // tm
    tn = INPROJ_TILE
    npair, rows2 = pairs.shape[0], pairs.shape[1]
    per = rows2 // tn
    nrel = len(relations)
    chip0 = 2 * lax.axis_index("x") + lax.axis_index("y")
    chips = jnp.stack([chip0 ^ m for m in masks]).astype(jnp.int32)

    def body(chips_ref, h_ref, w_ref, *refs):
        refs = refs[1:] if proj_prev is not None else refs
        if nrel:
            w_in, proj_ref, wt_full, nxt, send_sems, recv_sems, local_sems = refs[1:]
        else:
            (proj_ref,) = refs
        i, j = pl.program_id(0), pl.program_id(1)
        if nrel:
            ops = _GatherOps([w_in], [wt_full], send_sems, recv_sems, local_sems)

            @pl.when((i == 0) & (j == 0))
            def _():
                for q in relations:
                    ops.to_chip(0, q).start()

        proj_ref[...] = _dot(h_ref[...], w_ref[...], NT).astype(BF16)

        if nrel:
            @pl.when((i == nm - 1) & (j == npair * per - 1))
            def _():
                for q in relations:
                    ops.from_chip(0, q).wait_recv()
                    ops.forward(0, q).start()
                for q in relations:
                    ops.forwarded(0, q).wait_recv()
                outs = []
                for k, q in enumerate(relations):
                    cx, cy = ops.chips[q]
                    src = wt_full.at[pl.ds((2 * cx + cy) * rows2, rows2), :]
                    outs.append(pltpu.make_async_copy(src, nxt.at[k], local_sems.at[k]))
                for cp in outs:
                    cp.start()
                for q in relations:
                    ops.to_chip(0, q).wait_send()
                    ops.forward(0, q).wait_send()
                for cp in outs:
                    cp.wait()

    anyspec = lambda: pl.BlockSpec(memory_space=pl.ANY)
    prev = [proj_prev] if proj_prev is not None else []
    comm_in = [wt_prev, wt_shard] if nrel else []
    aliases = {}
    if prev:
        aliases[3] = 0
    if nrel:
        aliases[3 + len(prev)] = 1
    grid_spec = pltpu.PrefetchScalarGridSpec(
        num_scalar_prefetch=1,
        grid=(nm, npair * per),
        in_specs=[pl.BlockSpec((tm, D_MODEL), lambda i, j, ch: (i, 0)),
                  pl.BlockSpec((None, tn, D_MODEL), lambda i, j, ch: (j // per, j % per, 0))]
                 + [anyspec() for _ in prev + comm_in],
        out_specs=[pl.BlockSpec((tm, tn), lambda i, j, ch: (i, per * ch[j // per] + j % per))]
                  + ([anyspec(), anyspec()] if nrel else []),
        scratch_shapes=([pltpu.SemaphoreType.DMA((1, _GATHER_SEMS)), pltpu.SemaphoreType.DMA((1, _GATHER_SEMS)),
                         pltpu.SemaphoreType.DMA((nrel,))] if nrel else []),
    )
    out_shape = [jax.ShapeDtypeStruct((T, IN_WIDTH), BF16)]
    if nrel:
        out_shape += [jax.ShapeDtypeStruct(wt_prev.shape, BF16), jax.ShapeDtypeStruct((nrel, rows2, D_MODEL), BF16)]
    return pl.pallas_call(
        body, name=name, grid_spec=grid_spec, out_shape=out_shape, input_output_aliases=aliases,
        compiler_params=_params(("arbitrary", "arbitrary")),
    )(chips, h, pairs, *prev, *comm_in)


def _bias_table(rel_bias, bucket):
    def body(rb_ref, bk_ref, out_ref):
        h = pl.program_id(0)
        bk = bk_ref[...]
        acc = jnp.zeros(bk.shape, F32)
        for b in range(REL_BUCKETS):
            acc = jnp.where(bk == b, rb_ref[b, h], acc)
        out_ref[...] = acc

    return pl.pallas_call(
        body, name="bias_table",
        grid=(ATTN_HEADS,),
        in_specs=[pl.BlockSpec(memory_space=pltpu.SMEM),
                  pl.BlockSpec((ATTN_BLOCK, 2 * ATTN_BLOCK), lambda h: (0, 0))],
        out_specs=pl.BlockSpec((None, ATTN_BLOCK, 2 * ATTN_BLOCK), lambda h: (h, 0, 0)),
        out_shape=jax.ShapeDtypeStruct((ATTN_HEADS, ATTN_BLOCK, 2 * ATTN_BLOCK), F32),
        compiler_params=_params(("arbitrary",)),
    )(rel_bias, bucket)


def _bias_table_bwd(dbias, bucket):
    def body(db_ref, bk_ref, out_ref):
        h = pl.program_id(0)
        bk = bk_ref[...]
        db = db_ref[...]
        for b in range(REL_BUCKETS):
            out_ref[b, h] = jnp.sum(jnp.where(bk == b, db, 0.0))

    return pl.pallas_call(
        body, name="bias_table_bwd",
        grid=(ATTN_HEADS,),
        in_specs=[pl.BlockSpec((None, ATTN_BLOCK, 2 * ATTN_BLOCK), lambda h: (h, 0, 0)),
                  pl.BlockSpec((ATTN_BLOCK, 2 * ATTN_BLOCK), lambda h: (0, 0))],
        out_specs=pl.BlockSpec(memory_space=pltpu.SMEM),
        out_shape=jax.ShapeDtypeStruct((REL_BUCKETS, ATTN_HEADS), F32),
        compiler_params=_params(("arbitrary",)),
    )(dbias, bucket)


def _attn_common(qkvg, kv_prev, blk):
    lane = lax.broadcasted_iota(jnp.int32, (1, 128), 1)
    half = (lane < ATTN_HEAD_DIM, lane >= ATTN_HEAD_DIM)
    kv_cur = qkvg[:, C_AK:C_AG]
    win = jnp.concatenate([kv_prev, kv_cur], axis=0)
    k_slab, v_slab = [], []
    for r in range(2):
        ks = win[:, 128 * r:128 * r + 128]
        vs = win[:, 256 + 128 * r:256 + 128 * r + 128]
        k_slab.append((ks, pltpu.roll(ks, ATTN_HEAD_DIM, 1)))
        v_slab.append((vs, pltpu.roll(vs, ATTN_HEAD_DIM, 1)))
    rows4 = ATTN_GROUP * ATTN_BLOCK
    qi = lax.broadcasted_iota(jnp.int32, (rows4, 2 * ATTN_BLOCK), 0) & (ATTN_BLOCK - 1)
    si = lax.broadcasted_iota(jnp.int32, (rows4, 2 * ATTN_BLOCK), 1)
    valid = (si > qi) & (si <= qi + ATTN_BLOCK) & ((si >= ATTN_BLOCK) | (blk > 0))
    return half, k_slab, v_slab, valid


def _stack_heads(half, slab0, slab1):
    return jnp.concatenate([jnp.where(half[0], slab0, 0.0), jnp.where(half[1], slab0, 0.0),
                            jnp.where(half[0], slab1, 0.0), jnp.where(half[1], slab1, 0.0)], axis=0)


def _unstack_heads(half, x4):
    B = ATTN_BLOCK
    return (jnp.where(half[0], x4[0:B], x4[B:2 * B]), jnp.where(half[0], x4[2 * B:3 * B], x4[3 * B:4 * B]))


def _attn_group(j, qkvg, half, k_slab, v_slab, valid, bias_ref, sinks_ref):
    r, aj = j // 2, j % 2
    pick = (lambda a, b: jnp.where(half[0], a, b)) if aj == 0 else (lambda a, b: jnp.where(half[0], b, a))
    kb = pick(*k_slab[r]).astype(BF16)
    vb = pick(*v_slab[r]).astype(BF16)
    q4 = _stack_heads(half, qkvg[:, 256 * j:256 * j + 128], qkvg[:, 256 * j + 128:256 * j + 256]).astype(BF16)
    bias4 = bias_ref[ATTN_GROUP * j:ATTN_GROUP * (j + 1)].reshape(valid.shape)
    yield
    s = _dot(q4, kb, NT) * (ATTN_HEAD_DIM ** -0.5) + bias4
    s = jnp.where(valid, s, NEG_BIG)
    rowblk = lax.broadcasted_iota(jnp.int32, (valid.shape[0], 1), 0) // ATTN_BLOCK
    sink = jnp.full((valid.shape[0], 1), sinks_ref[0, ATTN_GROUP * j], F32)
    for b in range(1, ATTN_GROUP):
        sink = jnp.where(rowblk == b, sinks_ref[0, ATTN_GROUP * j + b], sink)
    m = jnp.maximum(jnp.max(s, axis=-1, keepdims=True), sink)
    e = jnp.exp(s - m)
    es = jnp.exp(sink - m)
    inv = 1.0 / (jnp.sum(e, axis=-1, keepdims=True) + es)
    pn = e * inv
    yield
    o4 = _dot(pn.astype(BF16), vb)
    return dict(r=r, aj=aj, kb=kb, vb=vb, q4=q4, pn=pn, psink=es * inv, o4=o4)


def _attn_specs(nb):
    row = lambda b, i: b * nb + i
    return [
        pl.BlockSpec((ATTN_BLOCK, C_HQ), lambda b, i: (row(b, i), 0)),
        pl.BlockSpec((ATTN_BLOCK, 512), lambda b, i: (row(b, jnp.maximum(i - 1, 0)), 2)),
        pl.BlockSpec((ATTN_HEADS, ATTN_BLOCK, 2 * ATTN_BLOCK), lambda b, i: (0, 0, 0)),
        pl.BlockSpec(memory_space=pltpu.SMEM),
    ]


def _attn_fwd(proj, bias, sinks, nseq, nb):
    T = proj.shape[0]

    def body(qkvg_ref, kvp_ref, bias_ref, sinks_ref, ya_ref):
        qkvg = qkvg_ref[...].astype(F32)
        half, k_slab, v_slab, valid = _attn_common(qkvg, kvp_ref[...].astype(F32), pl.program_id(1))
        groups = _interleave([_attn_group(j, qkvg, half, k_slab, v_slab, valid, bias_ref, sinks_ref)
                              for j in range(ATTN_HEADS // ATTN_GROUP)])
        slabs = []
        for grp in groups:
            slabs += _unstack_heads(half, grp["o4"])
        o_all = jnp.concatenate(slabs, axis=1)
        g = qkvg[:, C_AG:C_HQ]
        ya_ref[...] = (o_all * (g * _sigmoid(g))).astype(BF16)

    return pl.pallas_call(
        body, name="attn_fwd",
        grid=(nseq, nb),
        in_specs=_attn_specs(nb),
        out_specs=pl.BlockSpec((ATTN_BLOCK, 1024), lambda b, i: (b * nb + i, 0)),
        out_shape=jax.ShapeDtypeStruct((T, 1024), BF16),
        compiler_params=_params(("arbitrary", "arbitrary")),
    )(proj, proj, bias, sinks)


def _attn_bwd(proj, bias, sinks, d_ya, nseq, nb):
    T = proj.shape[0]
    S = nb * ATTN_BLOCK
    scale = ATTN_HEAD_DIM ** -0.5

    def body(qkvg_ref, kvp_ref, bias_ref, sinks_ref, dya_ref, dq_ref, dkv_ref, dg_ref, dbias_ref, dsinks_ref):
        b, i = pl.program_id(0), pl.program_id(1)
        first = (b == 0) & (i == 0)

        @pl.when(first)
        def _():
            dbias_ref[...] = jnp.zeros(dbias_ref.shape, F32)
            for h in range(ATTN_HEADS):
                dsinks_ref[0, h] = 0.0

        qkvg = qkvg_ref[...].astype(F32)
        half, k_slab, v_slab, valid = _attn_common(qkvg, kvp_ref[...].astype(F32), i)
        g = qkvg[:, C_AG:C_HQ]
        sg = _sigmoid(g)
        silu_g = g * sg
        dya = dya_ref[...].astype(F32)
        do_all = dya * silu_g
        dq_slabs, o_slabs = [], []
        dk_slab, dv_slab = [None, None], [None, None]
        B = ATTN_BLOCK

        def fold(x, aj):
            return jnp.where(half[aj], x + pltpu.roll(x, ATTN_HEAD_DIM, 1), 0.0)

        def group_bwd(j):
            grp = yield from _attn_group(j, qkvg, half, k_slab, v_slab, valid, bias_ref, sinks_ref)
            pn = grp["pn"]
            do4 = _stack_heads(half, do_all[:, 256 * j:256 * j + 128], do_all[:, 256 * j + 128:256 * j + 256])
            do4b = do4.astype(BF16)
            yield
            dp = _dot(do4b, grp["vb"], NT)
            delta = jnp.sum(do4 * grp["o4"], axis=-1, keepdims=True)
            ds = pn * (dp - delta)
            sink_term = grp["psink"] * delta
            for b4 in range(ATTN_GROUP):
                dsinks_ref[0, ATTN_GROUP * j + b4] += -jnp.sum(sink_term[B * b4:B * (b4 + 1)])
            dbias_ref[ATTN_GROUP * j:ATTN_GROUP * (j + 1)] += ds.reshape(ATTN_GROUP, B, 2 * B)
            dsb = ds.astype(BF16)
            yield
            dq4 = _dot(dsb, grp["kb"]) * scale
            dk_j = fold(_dot(dsb, grp["q4"], TN) * scale, grp["aj"])
            dv_j = fold(_dot(pn.astype(BF16), do4b, TN), grp["aj"])
            return dict(r=grp["r"], dq=_unstack_heads(half, dq4), o=_unstack_heads(half, grp["o4"]), dk=dk_j, dv=dv_j)

        for res in _interleave([group_bwd(j) for j in range(ATTN_HEADS // ATTN_GROUP)]):
            r = res["r"]
            dq_slabs += res["dq"]
            o_slabs += res["o"]
            dk_slab[r] = res["dk"] if dk_slab[r] is None else dk_slab[r] + res["dk"]
            dv_slab[r] = res["dv"] if dv_slab[r] is None else dv_slab[r] + res["dv"]

        dq_ref[...] = jnp.concatenate(dq_slabs, axis=1).astype(BF16)
        o_all = jnp.concatenate(o_slabs, axis=1)
        dg_ref[...] = (dya * o_all * (sg * (1.0 + g * (1.0 - sg)))).astype(BF16)

        dkv = jnp.concatenate(dk_slab + dv_slab, axis=1)
        cur = pl.multiple_of(i * ATTN_BLOCK, ATTN_BLOCK)
        dkv_ref[pl.ds(cur, ATTN_BLOCK), :] = dkv[ATTN_BLOCK:].astype(BF16)

        @pl.when(i > 0)
        def _():
            prev = pl.multiple_of((i - 1) * ATTN_BLOCK, ATTN_BLOCK)
            old = dkv_ref[pl.ds(prev, ATTN_BLOCK), :].astype(F32)
            dkv_ref[pl.ds(prev, ATTN_BLOCK), :] = (old + dkv[:ATTN_BLOCK]).astype(BF16)

    row_spec = lambda w: pl.BlockSpec((ATTN_BLOCK, w), lambda b, i: (b * nb + i, 0))
    return pl.pallas_call(
        body, name="attn_bwd",
        grid=(nseq, nb),
        in_specs=_attn_specs(nb) + [row_spec(1024)],
        out_specs=[row_spec(1024),
                   pl.BlockSpec((S, 512), lambda b, i: (b, 0)),
                   row_spec(1024),
                   pl.BlockSpec((ATTN_HEADS, ATTN_BLOCK, 2 * ATTN_BLOCK), lambda b, i: (0, 0, 0)),
                   pl.BlockSpec(memory_space=pltpu.SMEM)],
        out_shape=[jax.ShapeDtypeStruct((T, 1024), BF16),
                   jax.ShapeDtypeStruct((T, 512), BF16),
                   jax.ShapeDtypeStruct((T, 1024), BF16),
                   jax.ShapeDtypeStruct((ATTN_HEADS, ATTN_BLOCK, 2 * ATTN_BLOCK), F32),
                   jax.ShapeDtypeStruct((1, ATTN_HEADS), F32)],
        compiler_params=_params(("arbitrary", "arbitrary")),
    )(proj, proj, bias, sinks, d_ya)


def _split3(x):
    hi = x.astype(BF16)
    r1 = x - hi.astype(F32)
    mid = r1.astype(BF16)
    lo = (r1 - mid.astype(F32)).astype(BF16)
    return jnp.concatenate([hi, mid, lo], axis=1)


def _tri_sum(tri, x):
    y = _dot(tri, _split3(x))
    return y[:, :128] + y[:, 128:256] + y[:, 256:]


def _interleave(stages):
    results = [None] * len(stages)
    live = list(range(len(stages)))
    while live:
        still = []
        for idx in live:
            try:
                next(stages[idx])
                still.append(idx)
            except StopIteration as done:
                results[idx] = done.value
        live = still
    return results


def _hgrn_chunk(hq, hf, hi, lb):
    C = HGRN_CHUNK
    t = lax.broadcasted_iota(jnp.int32, (C, C), 0)
    s = lax.broadcasted_iota(jnp.int32, (C, C), 1)
    causal = s <= t
    sf = _sigmoid(hf)
    f = lb + (1.0 - lb) * sf
    lf = jnp.log(f)
    yield
    G = _tri_sum(causal.astype(BF16), lf)
    sq = _sigmoid(hq)
    qs = hq * sq
    k = 1.0 - f
    rowblk = lax.broadcasted_iota(jnp.int32, (C, 1), 0) // HGRN_SUB
    qt, kt, eq, ek = [], [], [], []
    for i in range(C // HGRN_SUB):
        lo = HGRN_SUB * i
        ref = G[lo + HGRN_SUB // 2:lo + HGRN_SUB // 2 + 1, :]
        eq_i = jnp.exp(G[lo:lo + HGRN_SUB] - ref)
        ek_i = jnp.exp(jnp.where(rowblk <= i, ref - G, 0.0))
        eq.append(eq_i)
        ek.append(ek_i)
        qt.append((qs[lo:lo + HGRN_SUB] * eq_i).astype(BF16))
        kt.append((k * ek_i).astype(BF16))
    yield
    A = jnp.concatenate([_dot(qt[i], kt[i], NT) for i in range(C // HGRN_SUB)], axis=0)
    A = jnp.where(causal, A, 0.0)
    glast = G[C - 1:C, :]
    eG = jnp.exp(G)
    edec = jnp.exp(glast - G)
    return dict(causal=causal, sf=sf, f=f, G=G, sq=sq, qs=qs, k=k, qt=qt, kt=kt, eq=eq, ek=ek, A=A,
                glast=glast, eG=eG, edec=edec, qhat=qs * eG, kdec=k * edec, v=hi)


def _hgrn_specs(nseq, ng, rows, reverse):
    W = HGRN_PAR
    nblk = W * HGRN_DIM // HGRN_COLS
    gi = (lambda g: ng - 1 - g) if reverse else (lambda g: g)

    def cols(c0):
        return [pl.BlockSpec((rows, HGRN_COLS),
                             lambda h, b, g, q=q: (b * ng + gi(g), c0 // HGRN_COLS + h * nblk + q)) for q in range(nblk)]

    return gi, (cols(C_HQ) + cols(C_HF) + cols(C_HI) + cols(C_HG)
                + [pl.BlockSpec((2, W * HGRN_DIM), lambda h, b, g: (0, h)),
                   pl.BlockSpec((W, 1, HGRN_DIM), lambda h, b, g: (h, 0, 0))])


def _hgrn_operands(refs):
    nblk = HGRN_PAR * HGRN_DIM // HGRN_COLS
    per = HGRN_COLS // HGRN_DIM

    def reader(group):
        def read(rs, w):
            lo = HGRN_DIM * (w % per)
            return group[w // per][rs, lo:lo + HGRN_DIM].astype(F32)
        return read

    readers = [reader(refs[nblk * a:nblk * (a + 1)]) for a in range(4)]
    return readers, refs[4 * nblk], refs[4 * nblk + 1], refs[4 * nblk + 2:]


def _hgrn_fwd(proj, lb_logits, gain3, nseq, S, shards):
    T = proj.shape[0]
    W = HGRN_PAR
    nc = S // HGRN_CHUNK
    cg = min(8, nc)
    ng = nc // cg
    rows = cg * HGRN_CHUNK
    n = len(shards)
    nh = HGRN_HEADS // W

    def body(*all_refs):
        (hq, hf, hi, hg), lbl_ref, gain_ref, refs = _hgrn_operands(all_refs)
        ins = refs[:n]
        o_ref, yh_ref, st_ref = refs[n:n + 3]
        outs = refs[n + 3:2 * n + 3]
        state, send_sems, recv_sems, local_sems = refs[2 * n + 3:]
        step = (pl.program_id(0), pl.program_id(1), pl.program_id(2))
        ops = _GatherOps(ins, outs, send_sems, recv_sems, local_sems)

        @pl.when((step[0] == 0) & (step[1] == 0) & (step[2] == 0))
        def _():
            for a in range(n):
                ops.local(a).start()
                ops.to_sibling(a).start()
                for q in range(3):
                    ops.to_chip(a, q).start()

        @pl.when(pl.program_id(2) == 0)
        def _():
            state[...] = jnp.zeros(state.shape, F32)

        lb_all = _sigmoid(lbl_ref[0:1, :] - lbl_ref[1:2, :])

        def head(w, c):
            rs = pl.ds(pl.multiple_of(c * HGRN_CHUNK, HGRN_CHUNK), HGRN_CHUNK)
            ls = slice(HGRN_DIM * w, HGRN_DIM * (w + 1))
            ch = yield from _hgrn_chunk(hq(rs, w), hf(rs, w), hi(rs, w), lb_all[:, ls])
            st = state[w]
            st_ref[w, c] = st
            vb = ch["v"].astype(BF16)
            yield
            o = _dot(ch["qhat"].astype(BF16), st.astype(BF16), NT) + _dot(ch["A"].astype(BF16), vb)
            state[w] = st * jnp.exp(ch["glast"]) + _dot(vb, ch["kdec"].astype(BF16), TN)
            o_ref[rs, ls] = o
            r = lax.rsqrt(jnp.mean(o * o, axis=-1, keepdims=True) + NORM_EPS)
            gate = hg(rs, w)
            yh_ref[rs, ls] = (o * r * gain_ref[w] * (gate * _sigmoid(gate))).astype(BF16)

        def chunk(c, carry):
            _interleave([head(w, c) for w in range(W)])
            return carry

        lax.fori_loop(0, cg, chunk, 0)

        @pl.when((step[0] == nh - 1) & (step[1] == nseq - 1) & (step[2] == ng - 1))
        def _():
            for a in range(n):
                ops.local(a).wait()
                ops.from_sibling(a).wait_recv()
                for q in range(3):
                    ops.from_chip(a, q).wait_recv()
                    ops.forward(a, q).start()
            for a in range(n):
                for q in range(3):
                    ops.forwarded(a, q).wait_recv()
                for cp in ops.sends(a):
                    cp.wait_send()

    _, in_specs = _hgrn_specs(nseq, ng, rows, False)
    out_row = lambda: pl.BlockSpec((rows, W * HGRN_DIM), lambda h, b, g: (b * ng + g, h))
    anyspec = lambda: pl.BlockSpec(memory_space=pl.ANY)
    return pl.pallas_call(
        body, name="hgrn_fwd",
        grid=(nh, nseq, ng),
        in_specs=in_specs + [anyspec() for _ in shards],
        out_specs=[out_row(), out_row(),
                   pl.BlockSpec((None, W, cg, HGRN_DIM, HGRN_DIM), lambda h, b, g: (b, h, g, 0, 0))]
                  + [anyspec() for _ in shards],
        out_shape=[jax.ShapeDtypeStruct((T, 1024), F32),
                   jax.ShapeDtypeStruct((T, 1024), BF16),
                   jax.ShapeDtypeStruct((nseq, HGRN_HEADS, nc, HGRN_DIM, HGRN_DIM), F32)]
                  + [jax.ShapeDtypeStruct((N_DEV * s.shape[0], s.shape[1]), s.dtype) for s in shards],
        scratch_shapes=[pltpu.VMEM((W, HGRN_DIM, HGRN_DIM), F32),
                        pltpu.SemaphoreType.DMA((n, _GATHER_SEMS)), pltpu.SemaphoreType.DMA((n, _GATHER_SEMS)),
                        pltpu.SemaphoreType.DMA((n,))],
        compiler_params=_params(("arbitrary", "arbitrary", "arbitrary")),
    )(*([proj] * (4 * W * HGRN_DIM // HGRN_COLS)), lb_logits, gain3, *shards)


def _hgrn_bwd(proj, lb_logits, gain3, o, d_yh, states, nseq, S, chip_sums):
    T = proj.shape[0]
    W = HGRN_PAR
    n = len(chip_sums)
    nh = HGRN_HEADS // W
    assert nh == 1
    nc = S // HGRN_CHUNK
    cg = min(8, nc)
    ng = nc // cg
    rows = cg * HGRN_CHUNK
    C = HGRN_CHUNK
    nsub = C // HGRN_SUB

    def body(*all_refs):
        (hq_of, hf_of, hi_of, hg_of), lbl_ref, gain_ref, refs = _hgrn_operands(all_refs)
        o_ref, dyh_ref, st_ref = refs[:3]
        sums_in = refs[3:3 + n]
        dh4_ref, dgain_ref, dlbl_ref = refs[3 + n:6 + n]
        sums_out = refs[6 + n:6 + 2 * n]
        dstate, dlb_acc, send_sems, recv_sems, local_sems = refs[6 + 2 * n:]
        h, b, g = pl.program_id(0), pl.program_id(1), pl.program_id(2)
        exchange = _ChipExchange(sums_in, sums_out, send_sems, recv_sems, local_sems)

        @pl.when((h == 0) & (b == 0) & (g == 0))
        def _():
            exchange.start()

        @pl.when(g == 0)
        def _():
            dstate[...] = jnp.zeros(dstate.shape, F32)

        @pl.when((b == 0) & (g == 0))
        def _():
            dgain_ref[...] = jnp.zeros(dgain_ref.shape, F32)
            dlb_acc[...] = jnp.zeros(dlb_acc.shape, F32)

        lb_all = _sigmoid(lbl_ref[0:1, :] - lbl_ref[1:2, :])

        anti = (lax.broadcasted_iota(jnp.int32, (C, C), 1) >= lax.broadcasted_iota(jnp.int32, (C, C), 0)).astype(BF16)
        last_row = lax.broadcasted_iota(jnp.int32, (C, 1), 0) == C - 1

        def head_load(w, c):
            rs = pl.ds(pl.multiple_of(c * C, C), C)
            ls = slice(HGRN_DIM * w, HGRN_DIM * (w + 1))
            return dict(hq=hq_of(rs, w), hf=hf_of(rs, w), hi=hi_of(rs, w), hg=hg_of(rs, w), lb=lb_all[:, ls],
                        gain=gain_ref[w], ov=o_ref[rs, ls], dyh=dyh_ref[rs, ls].astype(F32),
                        st=st_ref[w, c], dst=dstate[w])

        def head_math(v):
            lb, gain, hq, hg = v["lb"], v["gain"], v["hq"], v["hg"]
            ch = yield from _hgrn_chunk(hq, v["hf"], v["hi"], lb)
            ov, dyh = v["ov"], v["dyh"]
            r = lax.rsqrt(jnp.mean(ov * ov, axis=-1, keepdims=True) + NORM_EPS)
            on = ov * r
            sg = _sigmoid(hg)
            doh = dyh * (hg * sg)
            out = dict(dhg=(dyh * on * gain * (sg * (1.0 + hg * (1.0 - sg)))).astype(BF16),
                       dgain=jnp.sum(doh * on, axis=0, keepdims=True))
            don = doh * gain
            do = r * (don - on * jnp.mean(don * on, axis=-1, keepdims=True))
            dob = do.astype(BF16)
            st, dst = v["st"], v["dst"]
            stb, dstb = st.astype(BF16), dst.astype(BF16)
            vb = ch["v"].astype(BF16)
            qhatb = ch["qhat"].astype(BF16)
            eglast = jnp.exp(ch["glast"])
            yield
            dqhat = _dot(dob, stb)
            dkdec = _dot(vb, dstb)
            dv = _dot(ch["kdec"].astype(BF16), dstb, NT)
            deg = jnp.sum(dst * st, axis=0, keepdims=True)
            out["dstate"] = dst * eglast + _dot(dob, qhatb, TN)
            dA = jnp.where(ch["causal"], _dot(dob, vb, NT), 0.0)
            dv = dv + _dot(ch["A"].astype(BF16), dob, TN)
            dAb = dA.astype(BF16)
            yield
            dqs_parts, dgq_parts = [], []
            dk_intra, dgk = None, None
            for i in range(nsub):
                dA_i = dAb[HGRN_SUB * i:HGRN_SUB * (i + 1)]
                dqt = _dot(dA_i, ch["kt"][i])
                dkt = _dot(dA_i, ch["qt"][i], TN)
                dqs_parts.append(dqt * ch["eq"][i])
                dgq_parts.append(dqt * ch["qt"][i].astype(F32))
                dk_i = dkt * ch["ek"][i]
                dgk_i = dkt * ch["kt"][i].astype(F32)
                dk_intra = dk_i if dk_intra is None else dk_intra + dk_i
                dgk = dgk_i if dgk is None else dgk + dgk_i
            dqs_inter = dqhat * ch["eG"]
            dk_state = dkdec * ch["edec"]
            dqs = jnp.concatenate(dqs_parts, axis=0) + dqs_inter
            dk = dk_intra + dk_state
            dG = jnp.concatenate(dgq_parts, axis=0) - dgk + ch["qs"] * dqs_inter - ch["k"] * dk_state
            tail = jnp.sum(dkdec * ch["kdec"], axis=0, keepdims=True) + deg * eglast
            dG = dG + jnp.where(last_row, tail, 0.0)
            yield
            dlf = _tri_sum(anti, dG)
            df = dlf / ch["f"] - dk
            sf, sq = ch["sf"], ch["sq"]
            out["dhf"] = (df * (1.0 - lb) * sf * (1.0 - sf)).astype(BF16)
            out["dlb"] = jnp.sum(df * (1.0 - sf), axis=0, keepdims=True)
            out["dhq"] = (dqs * (sq * (1.0 + hq * (1.0 - sq)))).astype(BF16)
            out["dhi"] = dv.astype(BF16)
            return out

        def head_store(w, c, out):
            rs = pl.ds(pl.multiple_of(c * C, C), C)
            ls = slice(HGRN_DIM * w, HGRN_DIM * (w + 1))
            for a, key in enumerate(("dhq", "dhf", "dhi", "dhg")):
                lo = HGRN_HEADS * HGRN_DIM * a + HGRN_DIM * w
                dh4_ref[rs, lo:lo + HGRN_DIM] = out[key]
            dstate[w] = out["dstate"]
            dgain_ref[w] += out["dgain"]
            dlb_acc[:, ls] += out["dlb"]

        def chunk(cc, carry):
            c = cg - 1 - cc
            outs = _interleave([head_math(v) for v in [head_load(w, c) for w in range(W)]])
            for w in range(W):
                head_store(w, c, outs[w])
            return carry

        lax.fori_loop(0, cg, chunk, 0)

        @pl.when((b == nseq - 1) & (g == ng - 1))
        def _():
            dl0 = dlb_acc[...] * lb_all * (1.0 - lb_all)
            dlbl_ref[0:1, :] = dl0
            dlbl_ref[1:2, :] = -dl0

        @pl.when((h == nh - 1) & (b == nseq - 1) & (g == ng - 1))
        def _():
            exchange.wait()

    gi, in_specs = _hgrn_specs(nseq, ng, rows, True)
    row = lambda: pl.BlockSpec((rows, W * HGRN_DIM), lambda h, b, g: (b * ng + gi(g), h))
    anyspec = lambda: pl.BlockSpec(memory_space=pl.ANY)
    return pl.pallas_call(
        body, name="hgrn_bwd",
        grid=(nh, nseq, ng),
        in_specs=in_specs + [row(), row(),
                             pl.BlockSpec((None, W, cg, HGRN_DIM, HGRN_DIM), lambda h, b, g: (b, h, gi(g), 0, 0))]
                 + [anyspec() for _ in chip_sums],
        out_specs=[pl.BlockSpec((rows, 4 * W * HGRN_DIM), lambda h, b, g: (b * ng + gi(g), h)),
                   pl.BlockSpec((W, 1, HGRN_DIM), lambda h, b, g: (h, 0, 0)),
                   pl.BlockSpec((2, W * HGRN_DIM), lambda h, b, g: (0, h))] + [anyspec() for _ in chip_sums],
        out_shape=[jax.ShapeDtypeStruct((T, 4 * HGRN_HEADS * HGRN_DIM), BF16),
                   jax.ShapeDtypeStruct((HGRN_HEADS, 1, HGRN_DIM), F32),
                   jax.ShapeDtypeStruct((2, HGRN_HEADS * HGRN_DIM), F32)]
                  + [jax.ShapeDtypeStruct(s.shape, s.dtype) for s in chip_sums],
        scratch_shapes=[pltpu.VMEM((W, HGRN_DIM, HGRN_DIM), F32), pltpu.VMEM((1, W * HGRN_DIM), F32)]
                       + _ChipExchange.scratch(n),
        compiler_params=_params(("arbitrary", "arbitrary", "arbitrary")),
    )(*([proj] * (4 * W * HGRN_DIM // HGRN_COLS)), lb_logits, gain3, o, d_yh, states, *chip_sums)


def _gate_specs(tm):
    spec = lambda c0, q: pl.BlockSpec((tm, 512), lambda i: (i, c0 // 512 + q))
    return [spec(C_GA, q) for q in range(4)] + [spec(C_GH, q) for q in range(4)]


def _gates(refs):
    ga = jnp.concatenate([r[...] for r in refs[0:4]], axis=1).astype(F32)
    gh = jnp.concatenate([r[...] for r in refs[4:8]], axis=1).astype(F32)
    return ga, gh


def _branch_merge(proj, ya, yh, wa_t, wh_t):
    T = proj.shape[0]
    tm = min(512, T)

    def body(*refs):
        ya_ref, yh_ref, wa_ref, wh_ref, merged_ref, ua_ref, uh_ref = refs[8:]
        ga, gh = _gates(refs)
        ua = _dot(ya_ref[...], wa_ref[...], NT)
        uh = _dot(yh_ref[...], wh_ref[...], NT)
        merged_ref[...] = (_sigmoid(ga) * ua + _sigmoid(gh) * uh).astype(BF16)
        ua_ref[...] = ua.astype(BF16)
        uh_ref[...] = uh.astype(BF16)

    rowb = lambda w: pl.BlockSpec((tm, w), lambda i: (i, 0))
    full = lambda a: pl.BlockSpec(a.shape, lambda i: (0, 0))
    return pl.pallas_call(
        body, name="branch_merge",
        grid=(T // tm,),
        in_specs=_gate_specs(tm) + [rowb(1024), rowb(1024), full(wa_t), full(wh_t)],
        out_specs=[rowb(D_MODEL)] * 3,
        out_shape=[jax.ShapeDtypeStruct((T, D_MODEL), BF16)] * 3,
        compiler_params=_params(("arbitrary",)),
    )(*([proj] * 8), ya, yh, wa_t, wh_t)


def _out_norm_loss(merged, x2, tgt2, gpost, wout):
    T = merged.shape[0]
    tm = min(256, T)

    def body(m_ref, x_ref, t_ref, gpost_ref, wo_ref, dy_ref, dm_ref, dout_ref, loss_ref, dgpost_ref):
        @pl.when(pl.program_id(0) == 0)
        def _():
            loss_ref[...] = jnp.zeros(loss_ref.shape, F32)
            dgpost_ref[...] = jnp.zeros(dgpost_ref.shape, F32)

        y = _dot(m_ref[...], wo_ref[...])
        r2 = lax.rsqrt(jnp.mean(y * y, axis=-1, keepdims=True) + NORM_EPS)
        yn = y * r2
        gpost = gpost_ref[...]
        err = x_ref[...] + yn * gpost - t_ref[...]
        loss_ref[...] += jnp.sum(err * err, axis=0, keepdims=True)
        dout = err * (1.0 / D_MODEL)
        dout_ref[...] = dout
        dgpost_ref[...] += jnp.sum(dout * yn, axis=0, keepdims=True)
        dyn = dout * gpost
        dy = (r2 * (dyn - yn * jnp.mean(dyn * yn, axis=-1, keepdims=True))).astype(BF16)
        dy_ref[...] = dy
        dm_ref[...] = _dot(dy, wo_ref[...], NT).astype(BF16)

    rowb = lambda: pl.BlockSpec((tm, D_MODEL), lambda i: (i, 0))
    vec = lambda: pl.BlockSpec((1, D_MODEL), lambda i: (0, 0))
    return pl.pallas_call(
        body, name="out_norm_loss",
        grid=(T // tm,),
        in_specs=[rowb(), rowb(), rowb(), vec(), pl.BlockSpec(wout.shape, lambda i: (0, 0))],
        out_specs=[rowb(), rowb(), rowb(), vec(), vec()],
        out_shape=[jax.ShapeDtypeStruct((T, D_MODEL), BF16)] * 2
                  + [jax.ShapeDtypeStruct((T, D_MODEL), F32)] + [jax.ShapeDtypeStruct((1, D_MODEL), F32)] * 2,
        compiler_params=_params(("arbitrary",)),
    )(merged, x2, tgt2, gpost, wout)


def _branch_bwd(proj, dm, ua, uh, wa_t, wh_t):
    T = proj.shape[0]
    tm = min(256, T)

    def body(*refs):
        (dm_ref, ua_ref, uh_ref, wa_ref, wh_ref,
         dua_ref, duh_ref, dgg_ref, dya_ref, dyh_ref) = refs[8:]
        ga, gh = _gates(refs)
        sa, sh = _sigmoid(ga), _sigmoid(gh)
        dm = dm_ref[...].astype(F32)
        dua = (dm * sa).astype(BF16)
        duh = (dm * sh).astype(BF16)
        dua_ref[...] = dua
        duh_ref[...] = duh
        dgg_ref[:, :D_MODEL] = (dm * ua_ref[...].astype(F32) * (sa * (1.0 - sa))).astype(BF16)
        dgg_ref[:, D_MODEL:] = (dm * uh_ref[...].astype(F32) * (sh * (1.0 - sh))).astype(BF16)
        dya_ref[...] = _dot(dua, wa_ref[...]).astype(BF16)
        dyh_ref[...] = _dot(duh, wh_ref[...]).astype(BF16)

    rowb = lambda w: pl.BlockSpec((tm, w), lambda i: (i, 0))
    full = lambda a: pl.BlockSpec(a.shape, lambda i: (0, 0))
    return pl.pallas_call(
        body, name="branch_bwd",
        grid=(T // tm,),
        in_specs=_gate_specs(tm) + [rowb(D_MODEL)] * 3 + [full(wa_t), full(wh_t)],
        out_specs=[rowb(D_MODEL)] * 2 + [rowb(2 * D_MODEL)] + [rowb(1024)] * 2,
        out_shape=[jax.ShapeDtypeStruct((T, D_MODEL), BF16)] * 2 + [jax.ShapeDtypeStruct((T, 2 * D_MODEL), BF16)]
                  + [jax.ShapeDtypeStruct((T, 1024), BF16)] * 2,
        compiler_params=_params(("arbitrary",)),
    )(*([proj] * 8), dm, ua, uh, wa_t, wh_t)


def _tn_matmul(L, R, bm, bn, name):
    T, M = L.shape
    N = R.shape[1]

    def body(l_ref, r_ref, out_ref):
        out_ref[...] = _dot(l_ref[...], r_ref[...], TN).astype(BF16)

    return pl.pallas_call(
        body, name=name,
        grid=(N // bn, M // bm),
        in_specs=[pl.BlockSpec((T, bm), lambda j, i: (0, i)),
                  pl.BlockSpec((T, bn), lambda j, i: (0, j))],
        out_specs=pl.BlockSpec((bm, bn), lambda j, i: (i, j)),
        out_shape=jax.ShapeDtypeStruct((M, N), BF16),
        compiler_params=_params(("arbitrary", "arbitrary")),
    )(L, R)


def _tn_matmul_pieces(pieces, R, bm, bn, name):
    T, N = R.shape
    M = sum(p.shape[1] for p in pieces)
    out, row0 = None, 0
    for q, L in enumerate(pieces):
        off = row0 // bm

        def body(*refs):
            refs[-1][...] = _dot(refs[0][...], refs[1][...], TN).astype(BF16)

        prev = [] if out is None else [out]
        out = pl.pallas_call(
            body, name="%s_%d" % (name, q),
            grid=(N // bn, L.shape[1] // bm),
            in_specs=[pl.BlockSpec((T, bm), lambda j, i: (0, i)), pl.BlockSpec((T, bn), lambda j, i: (0, j))]
                     + [pl.BlockSpec(memory_space=pl.ANY) for _ in prev],
            out_specs=pl.BlockSpec((bm, bn), lambda j, i, off=off: (off + i, j)),
            out_shape=jax.ShapeDtypeStruct((M, N), BF16),
            input_output_aliases={2: 0} if prev else {},
            compiler_params=_params(("arbitrary", "arbitrary")),
        )(L, R, *prev)
        row0 += L.shape[1]
    return out


def _dh_prenorm_bwd(pieces, wt_in, x2, dout, gpre, chip_sums):
    T = x2.shape[0]
    tm = min(1024, T)
    ne = 4
    te = tm // ne
    tk = 512
    nk = IN_WIDTH // tk
    nt = T // tm
    n = len(chip_sums)
    npiece = len(pieces)
    counts = [p.shape[1] // tk for p in pieces]
    starts = [sum(counts[:q]) for q in range(npiece)]
    assert sum(counts) == nk and all(p.shape[1] % tk == 0 for p in pieces)

    def body(*all_refs):
        piece_refs = all_refs[:npiece]
        w_ref, x_ref, dout_ref, g_ref = all_refs[npiece:npiece + 4]
        refs = all_refs[npiece + 4:]
        ins = refs[:n]
        gx_ref, dg_ref = refs[n], refs[n + 1]
        outs = refs[n + 2:2 * n + 2]
        acc, send_sems, recv_sems, local_sems = refs[2 * n + 2:]
        i, k = pl.program_id(0), pl.program_id(1)
        exchange = _ChipExchange(ins, outs, send_sems, recv_sems, local_sems)

        @pl.when((i == 0) & (k == 0))
        def _():
            dg_ref[...] = jnp.zeros(dg_ref.shape, F32)
            exchange.start()

        @pl.when((i == nt - 1) & (k == nk + ne - 1))
        def _():
            exchange.wait()

        @pl.when(k == 0)
        def _():
            acc[...] = jnp.zeros(acc.shape, F32)

        for q in range(npiece):
            @pl.when((k >= starts[q]) & (k < starts[q] + counts[q]))
            def _(q=q):
                acc[...] += _dot(piece_refs[q][...], w_ref[...])

        @pl.when(k >= nk)
        def _():
            dh = acc[pl.ds(pl.multiple_of((k - nk) * te, te), te), :]
            x = x_ref[...]
            r = lax.rsqrt(jnp.mean(x * x, axis=-1, keepdims=True) + NORM_EPS)
            xn = x * r
            dg_ref[...] += jnp.sum(dh * xn, axis=0, keepdims=True)
            dxn = dh * g_ref[...]
            gx_ref[...] = dout_ref[...] + r * (dxn - xn * jnp.mean(dxn * xn, axis=-1, keepdims=True))

    piece_spec = lambda q: pl.BlockSpec((tm, tk), lambda i, k: (i, jnp.clip(k - starts[q], 0, counts[q] - 1)))
    rowb = lambda: pl.BlockSpec((te, D_MODEL), lambda i, k: (ne * i + jnp.clip(k - nk, 0, ne - 1), 0))
    vec = lambda: pl.BlockSpec((1, D_MODEL), lambda i, k: (0, 0))
    anyspec = lambda: pl.BlockSpec(memory_space=pl.ANY)
    return pl.pallas_call(
        body, name="dh_prenorm_bwd",
        grid=(nt, nk + ne),
        in_specs=[piece_spec(q) for q in range(npiece)]
                 + [pl.BlockSpec((tk, D_MODEL), lambda i, k: (jnp.minimum(k, nk - 1), 0)),
                    rowb(), rowb(), vec()] + [anyspec() for _ in chip_sums],
        out_specs=[rowb(), vec()] + [anyspec() for _ in chip_sums],
        out_shape=[jax.ShapeDtypeStruct((T, D_MODEL), F32), jax.ShapeDtypeStruct((1, D_MODEL), F32)]
                  + [jax.ShapeDtypeStruct(s.shape, s.dtype) for s in chip_sums],
        scratch_shapes=[pltpu.VMEM((tm, D_MODEL), F32)] + _ChipExchange.scratch(n),
        compiler_params=_params(("arbitrary", "arbitrary")),
    )(*pieces, wt_in, x2, dout, gpre, *chip_sums)


def _sum_slots(recv, br, name):
    nslot, R, C = recv.shape

    def body(r_ref, out_ref):
        acc = r_ref[0].astype(F32)
        for s in range(1, nslot):
            acc = acc + r_ref[s].astype(F32)
        out_ref[...] = acc

    return pl.pallas_call(
        body, name=name,
        grid=(R // br,),
        in_specs=[pl.BlockSpec((nslot, br, C), lambda i: (0, i, 0))],
        out_specs=pl.BlockSpec((br, C), lambda i: (i, 0)),
        out_shape=jax.ShapeDtypeStruct((R, C), F32),
        compiler_params=_params(("arbitrary",)),
    )(recv)


def _adamw_math(w, g, m, v):
    m = ADAM_B1 * m + (1.0 - ADAM_B1) * g
    v = ADAM_B2 * v + (1.0 - ADAM_B2) * (g * g)
    m_hat = m / (1.0 - ADAM_B1 ** ADAM_STEP)
    v_hat = v / (1.0 - ADAM_B2 ** ADAM_STEP)
    delta = -ADAM_LR * (m_hat / (jnp.sqrt(v_hat) + ADAM_EPS) + ADAM_WD * w)
    return delta, m, v


def _adamw(w, g, m, v, br, name):
    R, C = g.shape
    lead = (None,) * (w.ndim - 2)

    def body(w_ref, g_ref, m_ref, v_ref, d_ref, nm_ref, nv_ref):
        d_ref[...], nm_ref[...], nv_ref[...] = _adamw_math(w_ref[...], g_ref[...], m_ref[...], v_ref[...])

    spec = lambda: pl.BlockSpec(lead + (br, C), lambda i: (0,) * len(lead) + (i, 0))
    return pl.pallas_call(
        body, name=name,
        grid=(R // br,),
        in_specs=[spec(), pl.BlockSpec((br, C), lambda i: (i, 0)), spec(), spec()],
        out_specs=[spec(), spec(), spec()],
        out_shape=[jax.ShapeDtypeStruct(w.shape, F32)] * 3,
        compiler_params=_params(("arbitrary",)),
    )(w, g, m, v)


def _sibling_exchange(partials, name):
    n = len(partials)

    def body(*refs):
        ins, outs = refs[:n], refs[n:2 * n]
        send_sems, recv_sems = refs[2 * n:]
        x, y, c = _place()

        def copy(a, p):
            return pltpu.make_async_remote_copy(
                src_ref=ins[a].at[p, 1 - c], dst_ref=outs[a].at[p],
                send_sem=send_sems.at[a, p], recv_sem=recv_sems.at[a, p],
                device_id=(x, y, 1 - c), device_id_type=pl.DeviceIdType.MESH)

        copies = [copy(a, p) for p in range(4) for a in range(n)]
        for cp in copies:
            cp.start()
        for cp in copies:
            cp.wait()

    anyspec = lambda: pl.BlockSpec(memory_space=pl.ANY)
    return pl.pallas_call(
        body, name=name,
        in_specs=[anyspec() for _ in partials],
        out_specs=[anyspec() for _ in partials],
        out_shape=[jax.ShapeDtypeStruct((4,) + p.shape[2:], p.dtype) for p in partials],
        scratch_shapes=[pltpu.SemaphoreType.DMA((n, 4)), pltpu.SemaphoreType.DMA((n, 4))],
    )(*partials)


def _chip_sum(partial, from_sibling, br, name):
    _, _, R, C = partial.shape
    cls = lax.axis_index("c").astype(jnp.int32).reshape(1)

    def body(c_ref, mine_ref, sib_ref, out_ref):
        out_ref[...] = (mine_ref[...].astype(F32) + sib_ref[...].astype(F32)).astype(BF16)

    grid_spec = pltpu.PrefetchScalarGridSpec(
        num_scalar_prefetch=1,
        grid=(4, R // br),
        in_specs=[pl.BlockSpec((None, None, br, C), lambda p, i, c: (p, c[0], i, 0)),
                  pl.BlockSpec((None, br, C), lambda p, i, c: (p, i, 0))],
        out_specs=pl.BlockSpec((None, br, C), lambda p, i, c: (p, i, 0)),
    )
    return pl.pallas_call(
        body, name=name, grid_spec=grid_spec,
        out_shape=jax.ShapeDtypeStruct((4, R, C), BF16),
        compiler_params=_params(("arbitrary", "arbitrary")),
    )(cls, partial, from_sibling)


def _all_reduce_small(packed):
    shape = packed.shape

    def body(in_ref, out_ref, slots, send_sems, recv_sems):
        x, y, c = _place()
        my_slot = 4 * x + 2 * y + c

        def peer(k):
            return (x ^ ((k >> 2) & 1), y ^ ((k >> 1) & 1), c ^ (k & 1))

        def copy(k):
            p = peer(k)
            return pltpu.make_async_remote_copy(
                src_ref=in_ref, dst_ref=slots.at[my_slot],
                send_sem=send_sems.at[k - 1], recv_sem=recv_sems.at[k - 1],
                device_id=p, device_id_type=pl.DeviceIdType.MESH)

        def arrival(k):
            p = peer(k)
            return pltpu.make_async_remote_copy(
                src_ref=in_ref, dst_ref=slots.at[4 * p[0] + 2 * p[1] + p[2]],
                send_sem=send_sems.at[k - 1], recv_sem=recv_sems.at[k - 1],
                device_id=p, device_id_type=pl.DeviceIdType.MESH)

        sends = [copy(k) for k in range(1, N_DEV)]
        for cp in sends:
            cp.start()
        slots[my_slot] = in_ref[...]
        for k in range(1, N_DEV):
            arrival(k).wait_recv()
        for cp in sends:
            cp.wait_send()
        acc = slots[0]
        for s in range(1, N_DEV):
            acc = acc + slots[s]
        out_ref[...] = acc

    return pl.pallas_call(
        body, name="all_reduce_small",
        in_specs=[pl.BlockSpec(memory_space=pltpu.VMEM)],
        out_specs=pl.BlockSpec(memory_space=pltpu.VMEM),
        out_shape=jax.ShapeDtypeStruct(shape, F32),
        scratch_shapes=[pltpu.VMEM((N_DEV,) + shape, F32),
                        pltpu.SemaphoreType.DMA((7,)), pltpu.SemaphoreType.DMA((7,))],
    )(packed)


def _pack_small(norm_pre, norm_post, lb_logits, hgrn_norm, rel_bias, sinks, extra=None):
    tail = [hgrn_norm.reshape(1, 1024), rel_bias.reshape(1, 512), sinks.reshape(1, 16)]
    used = 1024 + 512 + 16
    if extra is not None:
        tail.append(extra.reshape(1, 1))
        used += 1
    tail.append(jnp.zeros((1, D_MODEL - used), F32))
    rows = [norm_pre.reshape(1, D_MODEL), norm_post.reshape(1, D_MODEL), lb_logits.reshape(1, D_MODEL),
            jnp.concatenate(tail, axis=1), jnp.zeros((4, D_MODEL), F32)]
    return jnp.concatenate(rows, axis=0)


def _unpack_small(p):
    return (p[0:1], p[3, 1024:1536].reshape(REL_BUCKETS, ATTN_HEADS), p[3:4, 1536:1552],
            p[2].reshape(2, 1024), p[3, 0:1024].reshape(1, HGRN_HEADS, HGRN_DIM), p[1:2])


def _local_step(nseq, S, x2, tgt2, proj, h, rel_bias, attn_sinks, lb_logits, hgrn_norm, norm_post, shards):
    nb = S // ATTN_BLOCK
    bucket = jnp.asarray(_t5_bucket_table())
    gain3 = hgrn_norm.reshape(HGRN_HEADS, 1, HGRN_DIM)

    bias = _bias_table(rel_bias, bucket)
    ya = _attn_fwd(proj, bias, attn_sinks, nseq, nb)
    o, yh, states, wout, wa_t, wh_t = _hgrn_fwd(proj, lb_logits, gain3, nseq, S, shards)
    merged, ua, uh = _branch_merge(proj, ya, yh, wa_t, wh_t)
    dy, dm, dout, loss_cols, d_gpost = _out_norm_loss(merged, x2, tgt2, norm_post, wout)
    dua, duh, dgg, dya, dyh = _branch_bwd(proj, dm, ua, uh, wa_t, wh_t)

    p_out = _tn_matmul(merged, dy, 256, 1024, "dw_out")
    p_a = _tn_matmul(dua, ya, 256, 1024, "dw_branch_attn")
    p_h = _tn_matmul(duh, yh, 256, 1024, "dw_branch_hgrn")
    small_sums = _chip_sums([p_out, p_a, p_h], (128, 128, 128), ("dw_out", "dw_branch_attn", "dw_branch_hgrn"),
                            "sibling_exchange_small")

    dq, dkv, dg, dbias, d_sinks = _attn_bwd(proj, bias, attn_sinks, dya, nseq, nb)
    d_rel_bias = _bias_table_bwd(dbias, bucket)
    dh4, d_gain, d_lbl, r_out, r_a, r_h = _hgrn_bwd(proj, lb_logits, gain3, o, dyh, states, nseq, S, small_sums)
    dproj = [dq, dkv, dg, dh4, dgg]
    p_in = _tn_matmul_pieces(dproj, h, 512, 1024, "dw_in")
    return dproj, dout, p_in, r_out, r_a, r_h, d_gpost, d_lbl, d_gain, d_rel_bias, d_sinks, loss_cols


def _chip_sums(partials, block_rows, names, exchange_name):
    split = [p.reshape(4, 2, p.shape[0] // N_DEV, p.shape[1]) for p in partials]
    from_sibling = _sibling_exchange(split, exchange_name)
    return [_chip_sum(p, f, br, "chip_sum_" + nm) for p, f, br, nm in zip(split, from_sibling, block_rows, names)]


def kernel(x, norm_pre, w_in, rel_bias, attn_sinks, lb_logits, hgrn_norm, w_branch_attn, w_branch_hgrn, w_out, norm_post, loss_target, m_norm_pre, m_w_in, m_rel_bias, m_attn_sinks, m_lb_logits, m_hgrn_norm, m_w_branch_attn, m_w_branch_hgrn, m_w_out, m_norm_post, v_norm_pre, v_w_in, v_rel_bias, v_attn_sinks, v_lb_logits, v_hgrn_norm, v_w_branch_attn, v_w_branch_hgrn, v_w_out, v_norm_post):
    nseq, S, _ = x.shape
    T = nseq * S
    x2 = x.reshape(T, D_MODEL)
    tgt2 = loss_target.reshape(T, D_MODEL)

    h = _prenorm(x2, norm_pre)
    wt_shard = w_in[0].T.astype(BF16)
    wt_own, own_pair = _gather_own_pair(wt_shard)
    proj, wt_near, near_pairs = _inproj_phase("inproj_own", h, own_pair, (0,), None, wt_own, wt_shard, (0, 1))
    proj, wt_in, far_pair = _inproj_phase("inproj_neighbours", h, near_pairs, (2, 1), proj, wt_near, wt_shard, (2,))
    (proj,) = _inproj_phase("inproj_diagonal", h, far_pair, (3,), proj, None, None, ())
    shards = [w_out[0].astype(BF16), w_branch_attn[0].T.astype(BF16), w_branch_hgrn[0].T.astype(BF16)]

    (dproj, dout, p_in, r_out, r_a, r_h, d_gpost, d_lbl, d_gain, d_rel_bias, d_sinks, loss_cols) = _local_step(
        nseq, S, x2, tgt2, proj, h, rel_bias, attn_sinks, lb_logits, hgrn_norm, norm_post, shards)

    in_sums = _chip_sums([p_in], (192,), ("dw_in",), "sibling_exchange_w_in")
    grad_x2, d_gpre, r_in = _dh_prenorm_bwd(dproj, wt_in, x2, dout, norm_pre, in_sums)
    g_in_t = _sum_slots(r_in, 192, "sum_dw_in")
    g_out = _sum_slots(r_out, 128, "sum_dw_out")
    g_a = _sum_slots(r_a, 128, "sum_dw_branch_attn").T
    g_h = _sum_slots(r_h, 128, "sum_dw_branch_hgrn").T

    loss_part = 0.5 / D_MODEL * jnp.sum(loss_cols)
    packed = _pack_small(d_gpre, d_gpost, d_lbl, d_gain, d_rel_bias, d_sinks, extra=loss_part)
    total = _all_reduce_small(packed)
    loss = total[3, 1024 + 512 + 16]
    sm_w = _pack_small(norm_pre, norm_post, lb_logits, hgrn_norm, rel_bias, attn_sinks)
    sm_m = _pack_small(m_norm_pre, m_norm_post, m_lb_logits, m_hgrn_norm, m_rel_bias, m_attn_sinks)
    sm_v = _pack_small(v_norm_pre, v_norm_post, v_lb_logits, v_hgrn_norm, v_rel_bias, v_attn_sinks)
    sm_d, sm_nm, sm_nv = _adamw(sm_w, total, sm_m, sm_v, 8, "adamw_small")

    t = lambda a: jnp.swapaxes(a, 1, 2)
    d_in, nm_in, nv_in = map(t, _adamw(t(w_in), g_in_t, t(m_w_in), t(v_w_in), 192, "adamw_w_in"))
    d_out, nm_out, nv_out = _adamw(w_out, g_out, m_w_out, v_w_out, 128, "adamw_w_out")
    d_a, nm_a, nv_a = _adamw(w_branch_attn, g_a, m_w_branch_attn, v_w_branch_attn, 256, "adamw_w_branch_attn")
    d_h, nm_h, nv_h = _adamw(w_branch_hgrn, g_h, m_w_branch_hgrn, v_w_branch_hgrn, 256, "adamw_w_branch_hgrn")

    def group(small, big_in, big_a, big_h, big_out):
        npre, rb, sk, lbl, hn, npost = _unpack_small(small)
        return (npre, big_in, rb, sk, lbl, hn, big_a, big_h, big_out, npost)

    return (loss, grad_x2.reshape(nseq, S, D_MODEL),
            *group(total, t(g_in_t[None]), g_a[None], g_h[None], g_out[None]),
            *group(sm_d, d_in, d_a, d_h, d_out),
            *group(sm_nm, nm_in, nm_a, nm_h, nm_out),
            *group(sm_nv, nv_in, nv_a, nv_h, nv_out))
```

```python
import functools
import math

import numpy as np
import jax
import jax.numpy as jnp
from jax import lax
from jax.experimental import pallas as pl
from jax.experimental.pallas import tpu as pltpu

F32 = jnp.float32
BF16 = jnp.bfloat16

D_MODEL = 2048
ATTN_HEADS = 16
ATTN_HEAD_DIM = 64
ATTN_GROUP = 4
ATTN_BLOCK = 128
HGRN_HEADS = 8
HGRN_DIM = 128
HGRN_CHUNK = 64
HGRN_SUB = 16
HGRN_PAR = 8
HGRN_COLS = 512
REL_BUCKETS = 32
REL_MAX_DIST = 128
NORM_EPS = 1e-6
C_AQ, C_AK, C_AV, C_AG = 0, 1024, 1280, 1536
C_HQ, C_HF, C_HI, C_HG = 2560, 3584, 4608, 5632
C_GA, C_GH = 6656, 8704
IN_WIDTH = 10752
N_DEV = 8
assert all(c0 % HGRN_COLS == 0 for c0 in (C_HQ, C_HF, C_HI, C_HG)) and (HGRN_PAR * HGRN_DIM) % HGRN_COLS == 0

ADAM_LR = 0.001
ADAM_B1 = 0.9
ADAM_B2 = 0.999
ADAM_EPS = 1e-08
ADAM_WD = 0.01
ADAM_STEP = 10

VMEM_LIMIT_V7X = 56 * 1024 * 1024
NEG_BIG = -1e30

NT = (((1,), (1,)), ((), ()))
TN = (((0,), (0,)), ((), ()))
NN = (((1,), (0,)), ((), ()))


def _dot(a, b, dims=NN):
    return lax.dot_general(a, b, dims, preferred_element_type=F32)


def _params(sem=None):
    return pltpu.CompilerParams(dimension_semantics=sem, vmem_limit_bytes=VMEM_LIMIT_V7X)


def _sigmoid(x):
    return 1.0 / (1.0 + jnp.exp(-x))


def _t5_bucket_table():
    qi = np.arange(ATTN_BLOCK)[:, None]
    si = np.arange(2 * ATTN_BLOCK)[None, :]
    dist = qi + ATTN_BLOCK - si
    max_exact = REL_BUCKETS // 2
    d = np.maximum(dist, 0)
    df = np.maximum(d, 1).astype(np.float32)
    large = max_exact + (np.log(df / np.float32(max_exact)).astype(np.float32)
                         / np.float32(math.log(REL_MAX_DIST / max_exact))
                         * np.float32(REL_BUCKETS - max_exact)).astype(np.int32)
    large = np.minimum(large, REL_BUCKETS - 1)
    return np.where(d < max_exact, d, large).astype(np.int32)


def _place():
    return lax.axis_index("x"), lax.axis_index("y"), lax.axis_index("c")


class _GatherOps:
    def __init__(self, ins, outs, send_sems, recv_sems, local_sems):
        self.ins, self.outs = ins, outs
        self.send_sems, self.recv_sems, self.local_sems = send_sems, recv_sems, local_sems
        x, y, c = _place()
        self.c = c
        self.me, self.sibling = (x, y, c), (x, y, 1 - c)
        self.chips = [(1 - x, y), (x, 1 - y), (1 - x, 1 - y)]

    def _rows(self, a, dev):
        r = self.ins[a].shape[0]
        return self.outs[a].at[pl.ds((4 * dev[0] + 2 * dev[1] + dev[2]) * r, r), :]

    def _copy(self, a, k, block, to, src=None):
        return pltpu.make_async_remote_copy(
            src_ref=self._rows(a, block) if src is None else src, dst_ref=self._rows(a, block),
            send_sem=self.send_sems.at[a, k], recv_sem=self.recv_sems.at[a, k],
            device_id=to, device_id_type=pl.DeviceIdType.MESH)

    def local(self, a):
        return pltpu.make_async_copy(self.ins[a], self._rows(a, self.me), self.local_sems.at[a])

    def to_sibling(self, a):
        return self._copy(a, 0, self.me, self.sibling, src=self.ins[a])

    def to_chip(self, a, q):
        return self._copy(a, 1 + q, self.me, (*self.chips[q], self.c), src=self.ins[a])

    def forward(self, a, q):
        return self._copy(a, 4 + q, (*self.chips[q], self.c), self.sibling)

    def from_sibling(self, a):
        return self._copy(a, 0, self.sibling, self.me)

    def from_chip(self, a, q):
        return self._copy(a, 1 + q, (*self.chips[q], self.c), self.me)

    def forwarded(self, a, q):
        return self._copy(a, 4 + q, (*self.chips[q], 1 - self.c), self.me)

    def sends(self, a):
        return [self.to_sibling(a)] + [self.to_chip(a, q) for q in range(3)] + [self.forward(a, q) for q in range(3)]


class _ChipExchange:
    def __init__(self, ins, outs, send_sems, recv_sems, local_sems):
        self.ins, self.outs = ins, outs
        self.send_sems, self.recv_sems, self.local_sems = send_sems, recv_sems, local_sems
        self.x, self.y, self.c = _place()
        self.my_chip = 2 * self.x + self.y

    def _peer(self, q):
        return (self.x ^ (q >> 1), self.y ^ (q & 1))

    def _copy(self, a, q, src_block, dst_slot):
        px, py = self._peer(q)
        return pltpu.make_async_remote_copy(
            src_ref=self.ins[a].at[src_block], dst_ref=self.outs[a].at[dst_slot],
            send_sem=self.send_sems.at[a, q - 1], recv_sem=self.recv_sems.at[a, q - 1],
            device_id=(px, py, self.c), device_id_type=pl.DeviceIdType.MESH)

    def _send(self, a, q):
        px, py = self._peer(q)
        return self._copy(a, q, 2 * px + py, self.my_chip)

    def _arrival(self, a, q):
        px, py = self._peer(q)
        return self._copy(a, q, self.my_chip, 2 * px + py)

    def _local(self, a):
        return pltpu.make_async_copy(self.ins[a].at[self.my_chip], self.outs[a].at[self.my_chip],
                                     self.local_sems.at[a])

    def start(self):
        for a in range(len(self.ins)):
            self._local(a).start()
            for q in range(1, 4):
                self._send(a, q).start()

    def wait(self):
        for a in range(len(self.ins)):
            for q in range(1, 4):
                self._arrival(a, q).wait_recv()
        for a in range(len(self.ins)):
            for q in range(1, 4):
                self._send(a, q).wait_send()
            self._local(a).wait()

    @staticmethod
    def scratch(n):
        return [pltpu.SemaphoreType.DMA((n, 3)), pltpu.SemaphoreType.DMA((n, 3)), pltpu.SemaphoreType.DMA((n,))]


_GATHER_SEMS = 7
LOCAL_DMA_THREAD = 1

INPROJ_TILE = 896


def _prenorm(x2, gpre):
    T = x2.shape[0]
    tm = min(512, T)

    def body(x_ref, g_ref, h_ref):
        x = x_ref[...]
        r = lax.rsqrt(jnp.mean(x * x, axis=-1, keepdims=True) + NORM_EPS)
        h_ref[...] = (x * r * g_ref[...]).astype(BF16)

    return pl.pallas_call(
        body, name="prenorm",
        grid=(T // tm,),
        in_specs=[pl.BlockSpec((tm, D_MODEL), lambda i: (i, 0)), pl.BlockSpec((1, D_MODEL), lambda i: (0, 0))],
        out_specs=pl.BlockSpec((tm, D_MODEL), lambda i: (i, 0)),
        out_shape=jax.ShapeDtypeStruct((T, D_MODEL), BF16),
        compiler_params=_params(("arbitrary",)),
    )(x2, gpre)


def _gather_inproj(h, wt_shard):
    T = h.shape[0]
    tm = min(1024, T)
    nm = T // tm
    tn = INPROJ_TILE
    ntile = IN_WIDTH // tn
    nstep = ntile * nm

    def body(h_hbm, w_in, proj_hbm, w_out, hbuf, wtile, obuf, own_buf,
             send_sems, recv_sems, local_sems, h_sem, w_sems, o_sems):
        j, i = pl.program_id(0), pl.program_id(1)
        step = j * nm + i
        slot = step % 2
        ops = _GatherOps([w_in], [w_out], send_sems, recv_sems, local_sems)
        x, y, _ = _place()

        def tile_of(jj):
            k = jj // 3
            return 3 * ((2 * x + y) ^ (((k & 1) << 1) | (k >> 1))) + jj % 3

        tile = tile_of(j)

        def h_load():
            return pltpu.make_async_copy(h_hbm, hbuf, h_sem)

        own_in = pltpu.make_async_copy(w_in, own_buf, local_sems.at[0])
        own_out = pltpu.make_async_copy(own_buf, ops._rows(0, ops.me), local_sems.at[1])

        def store(s, rows, cols):
            return pltpu.make_async_copy(obuf.at[s], proj_hbm.at[rows, cols], o_sems.at[s])

        def window(ii, t):
            return pl.ds(pl.multiple_of(ii * tm, tm), tm), pl.ds(pl.multiple_of(t * tn, tn), tn)

        @pl.when(step == 0)
        def _():
            h_load().start()
            own_in.start()
            ops.to_sibling(0).start()
            ops.to_chip(0, 0).start()
            ops.to_chip(0, 1).start()
            h_load().wait()

        for kk in range(4):
            @pl.when((j == 3 * kk) & (i == 0))
            def _(kk=kk):
                if kk == 0:
                    own_in.wait()
                    own_out.start()
                    own_out.wait()
                    ops.from_sibling(0).wait_recv()
                else:
                    q = kk - 1
                    ops.from_chip(0, q).wait_recv()
                    ops.forward(0, q).start()
                    if q == 0:
                        ops.to_chip(0, 2).start()
                    ops.forwarded(0, q).wait_recv()

        wslot = j % 2

        def fetch(jj, sw):
            rows = pl.ds(pl.multiple_of(tile_of(jj) * tn, tn), tn)
            return pltpu.make_async_copy(w_out.at[rows, :], wtile.at[sw], w_sems.at[sw])

        @pl.when((i == 0) & (j % 3 == 0))
        def _():
            fetch(j, wslot).start(LOCAL_DMA_THREAD)

        @pl.when(i == 0)
        def _():
            fetch(j, wslot).wait()

        @pl.when((i == 0) & (j % 3 != 2))
        def _():
            fetch(j + 1, 1 - wslot).start(LOCAL_DMA_THREAD)

        @pl.when(step >= 2)
        def _():
            store(slot, *window(0, 0)).wait()

        hv = hbuf[pl.ds(pl.multiple_of(i * tm, tm), tm), :]
        obuf[slot] = _dot(hv, wtile[wslot], NT).astype(BF16)
        store(slot, *window(i, tile)).start(LOCAL_DMA_THREAD)

        @pl.when(step == nstep - 1)
        def _():
            for s in range(min(2, nstep)):
                store(s, *window(0, 0)).wait()
            for cp in ops.sends(0):
                cp.wait_send()

    anyspec = lambda: pl.BlockSpec(memory_space=pl.ANY)
    return pl.pallas_call(
        body, name="gather_inproj",
        grid=(ntile, nm),
        in_specs=[anyspec(), anyspec()],
        out_specs=[anyspec(), anyspec()],
        out_shape=[jax.ShapeDtypeStruct((T, IN_WIDTH), BF16),
                   jax.ShapeDtypeStruct((N_DEV * wt_shard.shape[0], D_MODEL), BF16)],
        scratch_shapes=[pltpu.VMEM((T, D_MODEL), BF16), pltpu.VMEM((2, tn, D_MODEL), BF16),
                        pltpu.VMEM((2, tm, tn), BF16), pltpu.VMEM(wt_shard.shape, BF16),
                        pltpu.SemaphoreType.DMA((1, _GATHER_SEMS)), pltpu.SemaphoreType.DMA((1, _GATHER_SEMS)),
                        pltpu.SemaphoreType.DMA((2,)), pltpu.SemaphoreType.DMA, pltpu.SemaphoreType.DMA((2,)),
                        pltpu.SemaphoreType.DMA((2,))],
        compiler_params=_params(("arbitrary", "arbitrary")),
    )(h, wt_shard)


def _bias_table(rel_bias, bucket):
    def body(rb_ref, bk_ref, out_ref):
        h = pl.program_id(0)
        bk = bk_ref[...]
        acc = jnp.zeros(bk.shape, F32)
        for b in range(REL_BUCKETS):
            acc = jnp.where(bk == b, rb_ref[b, h], acc)
        out_ref[...] = acc

    return pl.pallas_call(
        body, name="bias_table",
        grid=(ATTN_HEADS,),
        in_specs=[pl.BlockSpec(memory_space=pltpu.SMEM),
                  pl.BlockSpec((ATTN_BLOCK, 2 * ATTN_BLOCK), lambda h: (0, 0))],
        out_specs=pl.BlockSpec((None, ATTN_BLOCK, 2 * ATTN_BLOCK), lambda h: (h, 0, 0)),
        out_shape=jax.ShapeDtypeStruct((ATTN_HEADS, ATTN_BLOCK, 2 * ATTN_BLOCK), F32),
        compiler_params=_params(("arbitrary",)),
    )(rel_bias, bucket)


def _bias_table_bwd(dbias, bucket):
    def body(db_ref, bk_ref, out_ref):
        h = pl.program_id(0)
        bk = bk_ref[...]
        db = db_ref[...]
        for b in range(REL_BUCKETS):
            out_ref[b, h] = jnp.sum(jnp.where(bk == b, db, 0.0))

    return pl.pallas_call(
        body, name="bias_table_bwd",
        grid=(ATTN_HEADS,),
        in_specs=[pl.BlockSpec((None, ATTN_BLOCK, 2 * ATTN_BLOCK), lambda h: (h, 0, 0)),
                  pl.BlockSpec((ATTN_BLOCK, 2 * ATTN_BLOCK), lambda h: (0, 0))],
        out_specs=pl.BlockSpec(memory_space=pltpu.SMEM),
        out_shape=jax.ShapeDtypeStruct((REL_BUCKETS, ATTN_HEADS), F32),
        compiler_params=_params(("arbitrary",)),
    )(dbias, bucket)


def _attn_common(qkvg, kv_prev, blk):
    lane = lax.broadcasted_iota(jnp.int32, (1, 128), 1)
    half = (lane < ATTN_HEAD_DIM, lane >= ATTN_HEAD_DIM)
    kv_cur = qkvg[:, C_AK:C_AG]
    win = jnp.concatenate([kv_prev, kv_cur], axis=0)
    k_slab, v_slab = [], []
    for r in range(2):
        ks = win[:, 128 * r:128 * r + 128]
        vs = win[:, 256 + 128 * r:256 + 128 * r + 128]
        k_slab.append((ks, pltpu.roll(ks, ATTN_HEAD_DIM, 1)))
        v_slab.append((vs, pltpu.roll(vs, ATTN_HEAD_DIM, 1)))
    rows4 = ATTN_GROUP * ATTN_BLOCK
    qi = lax.broadcasted_iota(jnp.int32, (rows4, 2 * ATTN_BLOCK), 0) & (ATTN_BLOCK - 1)
    si = lax.broadcasted_iota(jnp.int32, (rows4, 2 * ATTN_BLOCK), 1)
    valid = (si > qi) & (si <= qi + ATTN_BLOCK) & ((si >= ATTN_BLOCK) | (blk > 0))
    return half, k_slab, v_slab, valid


def _stack_heads(half, slab0, slab1):
    return jnp.concatenate([jnp.where(half[0], slab0, 0.0), jnp.where(half[1], slab0, 0.0),
                            jnp.where(half[0], slab1, 0.0), jnp.where(half[1], slab1, 0.0)], axis=0)


def _unstack_heads(half, x4):
    B = ATTN_BLOCK
    return (jnp.where(half[0], x4[0:B], x4[B:2 * B]), jnp.where(half[0], x4[2 * B:3 * B], x4[3 * B:4 * B]))


def _attn_group(j, qkvg, half, k_slab, v_slab, valid, bias_ref, sinks_ref):
    r, aj = j // 2, j % 2
    pick = (lambda a, b: jnp.where(half[0], a, b)) if aj == 0 else (lambda a, b: jnp.where(half[0], b, a))
    kb = pick(*k_slab[r]).astype(BF16)
    vb = pick(*v_slab[r]).astype(BF16)
    q4 = _stack_heads(half, qkvg[:, 256 * j:256 * j + 128], qkvg[:, 256 * j + 128:256 * j + 256]).astype(BF16)
    bias4 = bias_ref[ATTN_GROUP * j:ATTN_GROUP * (j + 1)].reshape(valid.shape)
    yield
    s = _dot(q4, kb, NT) * (ATTN_HEAD_DIM ** -0.5) + bias4
    s = jnp.where(valid, s, NEG_BIG)
    rowblk = lax.broadcasted_iota(jnp.int32, (valid.shape[0], 1), 0) // ATTN_BLOCK
    sink = jnp.full((valid.shape[0], 1), sinks_ref[0, ATTN_GROUP * j], F32)
    for b in range(1, ATTN_GROUP):
        sink = jnp.where(rowblk == b, sinks_ref[0, ATTN_GROUP * j + b], sink)
    m = jnp.maximum(jnp.max(s, axis=-1, keepdims=True), sink)
    e = jnp.exp(s - m)
    es = jnp.exp(sink - m)
    inv = 1.0 / (jnp.sum(e, axis=-1, keepdims=True) + es)
    pn = e * inv
    yield
    o4 = _dot(pn.astype(BF16), vb)
    return dict(r=r, aj=aj, kb=kb, vb=vb, q4=q4, pn=pn, psink=es * inv, o4=o4)


def _attn_specs(nb):
    row = lambda b, i: b * nb + i
    return [
        pl.BlockSpec((ATTN_BLOCK, C_HQ), lambda b, i: (row(b, i), 0)),
        pl.BlockSpec((ATTN_BLOCK, 512), lambda b, i: (row(b, jnp.maximum(i - 1, 0)), 2)),
        pl.BlockSpec((ATTN_HEADS, ATTN_BLOCK, 2 * ATTN_BLOCK), lambda b, i: (0, 0, 0)),
        pl.BlockSpec(memory_space=pltpu.SMEM),
    ]


def _attn_fwd(proj, bias, sinks, nseq, nb):
    T = proj.shape[0]

    def body(qkvg_ref, kvp_ref, bias_ref, sinks_ref, ya_ref):
        qkvg = qkvg_ref[...].astype(F32)
        half, k_slab, v_slab, valid = _attn_common(qkvg, kvp_ref[...].astype(F32), pl.program_id(1))
        groups = _interleave([_attn_group(j, qkvg, half, k_slab, v_slab, valid, bias_ref, sinks_ref)
                              for j in range(ATTN_HEADS // ATTN_GROUP)])
        slabs = []
        for grp in groups:
            slabs += _unstack_heads(half, grp["o4"])
        o_all = jnp.concatenate(slabs, axis=1)
        g = qkvg[:, C_AG:C_HQ]
        ya_ref[...] = (o_all * (g * _sigmoid(g))).astype(BF16)

    return pl.pallas_call(
        body, name="attn_fwd",
        grid=(nseq, nb),
        in_specs=_attn_specs(nb),
        out_specs=pl.BlockSpec((ATTN_BLOCK, 1024), lambda b, i: (b * nb + i, 0)),
        out_shape=jax.ShapeDtypeStruct((T, 1024), BF16),
        compiler_params=_params(("arbitrary", "arbitrary")),
    )(proj, proj, bias, sinks)


def _attn_bwd(proj, bias, sinks, d_ya, nseq, nb):
    T = proj.shape[0]
    S = nb * ATTN_BLOCK
    scale = ATTN_HEAD_DIM ** -0.5

    def body(qkvg_ref, kvp_ref, bias_ref, sinks_ref, dya_ref, dq_ref, dkv_ref, dg_ref, dbias_ref, dsinks_ref):
        b, i = pl.program_id(0), pl.program_id(1)
        first = (b == 0) & (i == 0)

        @pl.when(first)
        def _():
            dbias_ref[...] = jnp.zeros(dbias_ref.shape, F32)
            for h in range(ATTN_HEADS):
                dsinks_ref[0, h] = 0.0

        qkvg = qkvg_ref[...].astype(F32)
        half, k_slab, v_slab, valid = _attn_common(qkvg, kvp_ref[...].astype(F32), i)
        g = qkvg[:, C_AG:C_HQ]
        sg = _sigmoid(g)
        silu_g = g * sg
        dya = dya_ref[...].astype(F32)
        do_all = dya * silu_g
        dq_slabs, o_slabs = [], []
        dk_slab, dv_slab = [None, None], [None, None]
        B = ATTN_BLOCK

        def fold(x, aj):
            return jnp.where(half[aj], x + pltpu.roll(x, ATTN_HEAD_DIM, 1), 0.0)

        def group_bwd(j):
            grp = yield from _attn_group(j, qkvg, half, k_slab, v_slab, valid, bias_ref, sinks_ref)
            pn = grp["pn"]
            do4 = _stack_heads(half, do_all[:, 256 * j:256 * j + 128], do_all[:, 256 * j + 128:256 * j + 256])
            do4b = do4.astype(BF16)
            yield
            dp = _dot(do4b, grp["vb"], NT)
            delta = jnp.sum(do4 * grp["o4"], axis=-1, keepdims=True)
            ds = pn * (dp - delta)
            sink_term = grp["psink"] * delta
            for b4 in range(ATTN_GROUP):
                dsinks_ref[0, ATTN_GROUP * j + b4] += -jnp.sum(sink_term[B * b4:B * (b4 + 1)])
            dbias_ref[ATTN_GROUP * j:ATTN_GROUP * (j + 1)] += ds.reshape(ATTN_GROUP, B, 2 * B)
            dsb = ds.astype(BF16)
            yield
            dq4 = _dot(dsb, grp["kb"]) * scale
            dk_j = fold(_dot(dsb, grp["q4"], TN) * scale, grp["aj"])
            dv_j = fold(_dot(pn.astype(BF16), do4b, TN), grp["aj"])
            return dict(r=grp["r"], dq=_unstack_heads(half, dq4), o=_unstack_heads(half, grp["o4"]), dk=dk_j, dv=dv_j)

        for res in _interleave([group_bwd(j) for j in range(ATTN_HEADS // ATTN_GROUP)]):
            r = res["r"]
            dq_slabs += res["dq"]
            o_slabs += res["o"]
            dk_slab[r] = res["dk"] if dk_slab[r] is None else dk_slab[r] + res["dk"]
            dv_slab[r] = res["dv"] if dv_slab[r] is None else dv_slab[r] + res["dv"]

        dq_ref[...] = jnp.concatenate(dq_slabs, axis=1).astype(BF16)
        o_all = jnp.concatenate(o_slabs, axis=1)
        dg_ref[...] = (dya * o_all * (sg * (1.0 + g * (1.0 - sg)))).astype(BF16)

        dkv = jnp.concatenate(dk_slab + dv_slab, axis=1)
        cur = pl.multiple_of(i * ATTN_BLOCK, ATTN_BLOCK)
        dkv_ref[pl.ds(cur, ATTN_BLOCK), :] = dkv[ATTN_BLOCK:].astype(BF16)

        @pl.when(i > 0)
        def _():
            prev = pl.multiple_of((i - 1) * ATTN_BLOCK, ATTN_BLOCK)
            old = dkv_ref[pl.ds(prev, ATTN_BLOCK), :].astype(F32)
            dkv_ref[pl.ds(prev, ATTN_BLOCK), :] = (old + dkv[:ATTN_BLOCK]).astype(BF16)

    row_spec = lambda w: pl.BlockSpec((ATTN_BLOCK, w), lambda b, i: (b * nb + i, 0))
    return pl.pallas_call(
        body, name="attn_bwd",
        grid=(nseq, nb),
        in_specs=_attn_specs(nb) + [row_spec(1024)],
        out_specs=[row_spec(1024),
                   pl.BlockSpec((S, 512), lambda b, i: (b, 0)),
                   row_spec(1024),
                   pl.BlockSpec((ATTN_HEADS, ATTN_BLOCK, 2 * ATTN_BLOCK), lambda b, i: (0, 0, 0)),
                   pl.BlockSpec(memory_space=pltpu.SMEM)],
        out_shape=[jax.ShapeDtypeStruct((T, 1024), BF16),
                   jax.ShapeDtypeStruct((T, 512), BF16),
                   jax.ShapeDtypeStruct((T, 1024), BF16),
                   jax.ShapeDtypeStruct((ATTN_HEADS, ATTN_BLOCK, 2 * ATTN_BLOCK), F32),
                   jax.ShapeDtypeStruct((1, ATTN_HEADS), F32)],
        compiler_params=_params(("arbitrary", "arbitrary")),
    )(proj, proj, bias, sinks, d_ya)


def _split3(x):
    hi = x.astype(BF16)
    r1 = x - hi.astype(F32)
    mid = r1.astype(BF16)
    lo = (r1 - mid.astype(F32)).astype(BF16)
    return jnp.concatenate([hi, mid, lo], axis=1)


def _tri_sum(tri, x):
    y = _dot(tri, _split3(x))
    return y[:, :128] + y[:, 128:256] + y[:, 256:]


def _interleave(stages):
    results = [None] * len(stages)
    live = list(range(len(stages)))
    while live:
        still = []
        for idx in live:
            try:
                next(stages[idx])
                still.append(idx)
            except StopIteration as done:
                results[idx] = done.value
        live = still
    return results


def _hgrn_chunk(hq, hf, hi, lb):
    C = HGRN_CHUNK
    t = lax.broadcasted_iota(jnp.int32, (C, C), 0)
    s = lax.broadcasted_iota(jnp.int32, (C, C), 1)
    causal = s <= t
    sf = _sigmoid(hf)
    f = lb + (1.0 - lb) * sf
    lf = jnp.log(f)
    yield
    G = _tri_sum(causal.astype(BF16), lf)
    sq = _sigmoid(hq)
    qs = hq * sq
    k = 1.0 - f
    rowblk = lax.broadcasted_iota(jnp.int32, (C, 1), 0) // HGRN_SUB
    qt, kt, eq, ek = [], [], [], []
    for i in range(C // HGRN_SUB):
        lo = HGRN_SUB * i
        ref = G[lo + HGRN_SUB // 2:lo + HGRN_SUB // 2 + 1, :]
        eq_i = jnp.exp(G[lo:lo + HGRN_SUB] - ref)
        ek_i = jnp.exp(jnp.where(rowblk <= i, ref - G, 0.0))
        eq.append(eq_i)
        ek.append(ek_i)
        qt.append((qs[lo:lo + HGRN_SUB] * eq_i).astype(BF16))
        kt.append((k * ek_i).astype(BF16))
    yield
    A = jnp.concatenate([_dot(qt[i], kt[i], NT) for i in range(C // HGRN_SUB)], axis=0)
    A = jnp.where(causal, A, 0.0)
    glast = G[C - 1:C, :]
    eG = jnp.exp(G)
    edec = jnp.exp(glast - G)
    return dict(causal=causal, sf=sf, f=f, G=G, sq=sq, qs=qs, k=k, qt=qt, kt=kt, eq=eq, ek=ek, A=A,
                glast=glast, eG=eG, edec=edec, qhat=qs * eG, kdec=k * edec, v=hi)


def _hgrn_specs(nseq, ng, rows, reverse):
    W = HGRN_PAR
    nblk = W * HGRN_DIM // HGRN_COLS
    gi = (lambda g: ng - 1 - g) if reverse else (lambda g: g)

    def cols(c0):
        return [pl.BlockSpec((rows, HGRN_COLS),
                             lambda h, b, g, q=q: (b * ng + gi(g), c0 // HGRN_COLS + h * nblk + q)) for q in range(nblk)]

    return gi, (cols(C_HQ) + cols(C_HF) + cols(C_HI) + cols(C_HG)
                + [pl.BlockSpec((2, W * HGRN_DIM), lambda h, b, g: (0, h)),
                   pl.BlockSpec((W, 1, HGRN_DIM), lambda h, b, g: (h, 0, 0))])


def _hgrn_operands(refs):
    nblk = HGRN_PAR * HGRN_DIM // HGRN_COLS
    per = HGRN_COLS // HGRN_DIM

    def reader(group):
        def read(rs, w):
            lo = HGRN_DIM * (w % per)
            return group[w // per][rs, lo:lo + HGRN_DIM].astype(F32)
        return read

    readers = [reader(refs[nblk * a:nblk * (a + 1)]) for a in range(4)]
    return readers, refs[4 * nblk], refs[4 * nblk + 1], refs[4 * nblk + 2:]


def _hgrn_fwd(proj, lb_logits, gain3, nseq, S, shards):
    T = proj.shape[0]
    W = HGRN_PAR
    nc = S // HGRN_CHUNK
    cg = min(8, nc)
    ng = nc // cg
    rows = cg * HGRN_CHUNK
    n = len(shards)
    nh = HGRN_HEADS // W

    def body(*all_refs):
        (hq, hf, hi, hg), lbl_ref, gain_ref, refs = _hgrn_operands(all_refs)
        ins = refs[:n]
        o_ref, yh_ref, st_ref = refs[n:n + 3]
        outs = refs[n + 3:2 * n + 3]
        state, send_sems, recv_sems, local_sems = refs[2 * n + 3:]
        step = (pl.program_id(0), pl.program_id(1), pl.program_id(2))
        ops = _GatherOps(ins, outs, send_sems, recv_sems, local_sems)

        @pl.when((step[0] == 0) & (step[1] == 0) & (step[2] == 0))
        def _():
            for a in range(n):
                ops.local(a).start()
                ops.to_sibling(a).start()
                for q in range(3):
                    ops.to_chip(a, q).start()

        @pl.when(pl.program_id(2) == 0)
        def _():
            state[...] = jnp.zeros(state.shape, F32)

        lb_all = _sigmoid(lbl_ref[0:1, :] - lbl_ref[1:2, :])

        def head(w, c):
            rs = pl.ds(pl.multiple_of(c * HGRN_CHUNK, HGRN_CHUNK), HGRN_CHUNK)
            ls = slice(HGRN_DIM * w, HGRN_DIM * (w + 1))
            ch = yield from _hgrn_chunk(hq(rs, w), hf(rs, w), hi(rs, w), lb_all[:, ls])
            st = state[w]
            st_ref[w, c] = st
            vb = ch["v"].astype(BF16)
            yield
            o = _dot(ch["qhat"].astype(BF16), st.astype(BF16), NT) + _dot(ch["A"].astype(BF16), vb)
            state[w] = st * jnp.exp(ch["glast"]) + _dot(vb, ch["kdec"].astype(BF16), TN)
            o_ref[rs, ls] = o
            r = lax.rsqrt(jnp.mean(o * o, axis=-1, keepdims=True) + NORM_EPS)
            gate = hg(rs, w)
            yh_ref[rs, ls] = (o * r * gain_ref[w] * (gate * _sigmoid(gate))).astype(BF16)

        def chunk(c, carry):
            _interleave([head(w, c) for w in range(W)])
            return carry

        lax.fori_loop(0, cg, chunk, 0)

        @pl.when((step[0] == nh - 1) & (step[1] == nseq - 1) & (step[2] == ng - 1))
        def _():
            for a in range(n):
                ops.local(a).wait()
                ops.from_sibling(a).wait_recv()
                for q in range(3):
                    ops.from_chip(a, q).wait_recv()
                    ops.forward(a, q).start()
            for a in range(n):
                for q in range(3):
                    ops.forwarded(a, q).wait_recv()
                for cp in ops.sends(a):
                    cp.wait_send()

    _, in_specs = _hgrn_specs(nseq, ng, rows, False)
    out_row = lambda: pl.BlockSpec((rows, W * HGRN_DIM), lambda h, b, g: (b * ng + g, h))
    anyspec = lambda: pl.BlockSpec(memory_space=pl.ANY)
    return pl.pallas_call(
        body, name="hgrn_fwd",
        grid=(nh, nseq, ng),
        in_specs=in_specs + [anyspec() for _ in shards],
        out_specs=[out_row(), out_row(),
                   pl.BlockSpec((None, W, cg, HGRN_DIM, HGRN_DIM), lambda h, b, g: (b, h, g, 0, 0))]
                  + [anyspec() for _ in shards],
        out_shape=[jax.ShapeDtypeStruct((T, 1024), F32),
                   jax.ShapeDtypeStruct((T, 1024), BF16),
                   jax.ShapeDtypeStruct((nseq, HGRN_HEADS, nc, HGRN_DIM, HGRN_DIM), F32)]
                  + [jax.ShapeDtypeStruct((N_DEV * s.shape[0], s.shape[1]), s.dtype) for s in shards],
        scratch_shapes=[pltpu.VMEM((W, HGRN_DIM, HGRN_DIM), F32),
                        pltpu.SemaphoreType.DMA((n, _GATHER_SEMS)), pltpu.SemaphoreType.DMA((n, _GATHER_SEMS)),
                        pltpu.SemaphoreType.DMA((n,))],
        compiler_params=_params(("arbitrary", "arbitrary", "arbitrary")),
    )(*([proj] * (4 * W * HGRN_DIM // HGRN_COLS)), lb_logits, gain3, *shards)


def _hgrn_bwd(proj, lb_logits, gain3, o, d_yh, states, nseq, S, chip_sums):
    T = proj.shape[0]
    W = HGRN_PAR
    n = len(chip_sums)
    nh = HGRN_HEADS // W
    assert nh == 1
    nc = S // HGRN_CHUNK
    cg = min(8, nc)
    ng = nc // cg
    rows = cg * HGRN_CHUNK
    C = HGRN_CHUNK
    nsub = C // HGRN_SUB

    def body(*all_refs):
        (hq_of, hf_of, hi_of, hg_of), lbl_ref, gain_ref, refs = _hgrn_operands(all_refs)
        o_ref, dyh_ref, st_ref = refs[:3]
        sums_in = refs[3:3 + n]
        dh4_ref, dgain_ref, dlbl_ref = refs[3 + n:6 + n]
        sums_out = refs[6 + n:6 + 2 * n]
        dstate, dlb_acc, send_sems, recv_sems, local_sems = refs[6 + 2 * n:]
        h, b, g = pl.program_id(0), pl.program_id(1), pl.program_id(2)
        exchange = _ChipExchange(sums_in, sums_out, send_sems, recv_sems, local_sems)

        @pl.when((h == 0) & (b == 0) & (g == 0))
        def _():
            exchange.start()

        @pl.when(g == 0)
        def _():
            dstate[...] = jnp.zeros(dstate.shape, F32)

        @pl.when((b == 0) & (g == 0))
        def _():
            dgain_ref[...] = jnp.zeros(dgain_ref.shape, F32)
            dlb_acc[...] = jnp.zeros(dlb_acc.shape, F32)

        lb_all = _sigmoid(lbl_ref[0:1, :] - lbl_ref[1:2, :])

        anti = (lax.broadcasted_iota(jnp.int32, (C, C), 1) >= lax.broadcasted_iota(jnp.int32, (C, C), 0)).astype(BF16)
        last_row = lax.broadcasted_iota(jnp.int32, (C, 1), 0) == C - 1

        def head_load(w, c):
            rs = pl.ds(pl.multiple_of(c * C, C), C)
            ls = slice(HGRN_DIM * w, HGRN_DIM * (w + 1))
            return dict(hq=hq_of(rs, w), hf=hf_of(rs, w), hi=hi_of(rs, w), hg=hg_of(rs, w), lb=lb_all[:, ls],
                        gain=gain_ref[w], ov=o_ref[rs, ls], dyh=dyh_ref[rs, ls].astype(F32),
                        st=st_ref[w, c], dst=dstate[w])

        def head_math(v):
            lb, gain, hq, hg = v["lb"], v["gain"], v["hq"], v["hg"]
            ch = yield from _hgrn_chunk(hq, v["hf"], v["hi"], lb)
            ov, dyh = v["ov"], v["dyh"]
            r = lax.rsqrt(jnp.mean(ov * ov, axis=-1, keepdims=True) + NORM_EPS)
            on = ov * r
            sg = _sigmoid(hg)
            doh = dyh * (hg * sg)
            out = dict(dhg=(dyh * on * gain * (sg * (1.0 + hg * (1.0 - sg)))).astype(BF16),
                       dgain=jnp.sum(doh * on, axis=0, keepdims=True))
            don = doh * gain
            do = r * (don - on * jnp.mean(don * on, axis=-1, keepdims=True))
            dob = do.astype(BF16)
            st, dst = v["st"], v["dst"]
            stb, dstb = st.astype(BF16), dst.astype(BF16)
            vb = ch["v"].astype(BF16)
            qhatb = ch["qhat"].astype(BF16)
            eglast = jnp.exp(ch["glast"])
            yield
            dqhat = _dot(dob, stb)
            dkdec = _dot(vb, dstb)
            dv = _dot(ch["kdec"].astype(BF16), dstb, NT)
            deg = jnp.sum(dst * st, axis=0, keepdims=True)
            out["dstate"] = dst * eglast + _dot(dob, qhatb, TN)
            dA = jnp.where(ch["causal"], _dot(dob, vb, NT), 0.0)
            dv = dv + _dot(ch["A"].astype(BF16), dob, TN)
            dAb = dA.astype(BF16)
            yield
            dqs_parts, dgq_parts = [], []
            dk_intra, dgk = None, None
            for i in range(nsub):
                dA_i = dAb[HGRN_SUB * i:HGRN_SUB * (i + 1)]
                dqt = _dot(dA_i, ch["kt"][i])
                dkt = _dot(dA_i, ch["qt"][i], TN)
                dqs_parts.append(dqt * ch["eq"][i])
                dgq_parts.append(dqt * ch["qt"][i].astype(F32))
                dk_i = dkt * ch["ek"][i]
                dgk_i = dkt * ch["kt"][i].astype(F32)
                dk_intra = dk_i if dk_intra is None else dk_intra + dk_i
                dgk = dgk_i if dgk is None else dgk + dgk_i
            dqs_inter = dqhat * ch["eG"]
            dk_state = dkdec * ch["edec"]
            dqs = jnp.concatenate(dqs_parts, axis=0) + dqs_inter
            dk = dk_intra + dk_state
            dG = jnp.concatenate(dgq_parts, axis=0) - dgk + ch["qs"] * dqs_inter - ch["k"] * dk_state
            tail = jnp.sum(dkdec * ch["kdec"], axis=0, keepdims=True) + deg * eglast
            dG = dG + jnp.where(last_row, tail, 0.0)
            yield
            dlf = _tri_sum(anti, dG)
            df = dlf / ch["f"] - dk
            sf, sq = ch["sf"], ch["sq"]
            out["dhf"] = (df * (1.0 - lb) * sf * (1.0 - sf)).astype(BF16)
            out["dlb"] = jnp.sum(df * (1.0 - sf), axis=0, keepdims=True)
            out["dhq"] = (dqs * (sq * (1.0 + hq * (1.0 - sq)))).astype(BF16)
            out["dhi"] = dv.astype(BF16)
            return out

        def head_store(w, c, out):
            rs = pl.ds(pl.multiple_of(c * C, C), C)
            ls = slice(HGRN_DIM * w, HGRN_DIM * (w + 1))
            for a, key in enumerate(("dhq", "dhf", "dhi", "dhg")):
                lo = HGRN_HEADS * HGRN_DIM * a + HGRN_DIM * w
                dh4_ref[rs, lo:lo + HGRN_DIM] = out[key]
            dstate[w] = out["dstate"]
            dgain_ref[w] += out["dgain"]
            dlb_acc[:, ls] += out["dlb"]

        def chunk(cc, carry):
            c = cg - 1 - cc
            outs = _interleave([head_math(v) for v in [head_load(w, c) for w in range(W)]])
            for w in range(W):
                head_store(w, c, outs[w])
            return carry

        lax.fori_loop(0, cg, chunk, 0)

        @pl.when((b == nseq - 1) & (g == ng - 1))
        def _():
            dl0 = dlb_acc[...] * lb_all * (1.0 - lb_all)
            dlbl_ref[0:1, :] = dl0
            dlbl_ref[1:2, :] = -dl0

        @pl.when((h == nh - 1) & (b == nseq - 1) & (g == ng - 1))
        def _():
            exchange.wait()

    gi, in_specs = _hgrn_specs(nseq, ng, rows, True)
    row = lambda: pl.BlockSpec((rows, W * HGRN_DIM), lambda h, b, g: (b * ng + gi(g), h))
    anyspec = lambda: pl.BlockSpec(memory_space=pl.ANY)
    return pl.pallas_call(
        body, name="hgrn_bwd",
        grid=(nh, nseq, ng),
        in_specs=in_specs + [row(), row(),
                             pl.BlockSpec((None, W, cg, HGRN_DIM, HGRN_DIM), lambda h, b, g: (b, h, gi(g), 0, 0))]
                 + [anyspec() for _ in chip_sums],
        out_specs=[pl.BlockSpec((rows, 4 * W * HGRN_DIM), lambda h, b, g: (b * ng + gi(g), h)),
                   pl.BlockSpec((W, 1, HGRN_DIM), lambda h, b, g: (h, 0, 0)),
                   pl.BlockSpec((2, W * HGRN_DIM), lambda h, b, g: (0, h))] + [anyspec() for _ in chip_sums],
        out_shape=[jax.ShapeDtypeStruct((T, 4 * HGRN_HEADS * HGRN_DIM), BF16),
                   jax.ShapeDtypeStruct((HGRN_HEADS, 1, HGRN_DIM), F32),
                   jax.ShapeDtypeStruct((2, HGRN_HEADS * HGRN_DIM), F32)]
                  + [jax.ShapeDtypeStruct(s.shape, s.dtype) for s in chip_sums],
        scratch_shapes=[pltpu.VMEM((W, HGRN_DIM, HGRN_DIM), F32), pltpu.VMEM((1, W * HGRN_DIM), F32)]
                       + _ChipExchange.scratch(n),
        compiler_params=_params(("arbitrary", "arbitrary", "arbitrary")),
    )(*([proj] * (4 * W * HGRN_DIM // HGRN_COLS)), lb_logits, gain3, o, d_yh, states, *chip_sums)


def _gate_specs(tm):
    spec = lambda c0, q: pl.BlockSpec((tm, 512), lambda i: (i, c0 // 512 + q))
    return [spec(C_GA, q) for q in range(4)] + [spec(C_GH, q) for q in range(4)]


def _gates(refs):
    ga = jnp.concatenate([r[...] for r in refs[0:4]], axis=1).astype(F32)
    gh = jnp.concatenate([r[...] for r in refs[4:8]], axis=1).astype(F32)
    return ga, gh


def _branch_merge(proj, ya, yh, wa_t, wh_t):
    T = proj.shape[0]
    tm = min(512, T)

    def body(*refs):
        ya_ref, yh_ref, wa_ref, wh_ref, merged_ref, ua_ref, uh_ref = refs[8:]
        ga, gh = _gates(refs)
        ua = _dot(ya_ref[...], wa_ref[...], NT)
        uh = _dot(yh_ref[...], wh_ref[...], NT)
        merged_ref[...] = (_sigmoid(ga) * ua + _sigmoid(gh) * uh).astype(BF16)
        ua_ref[...] = ua.astype(BF16)
        uh_ref[...] = uh.astype(BF16)

    rowb = lambda w: pl.BlockSpec((tm, w), lambda i: (i, 0))
    full = lambda a: pl.BlockSpec(a.shape, lambda i: (0, 0))
    return pl.pallas_call(
        body, name="branch_merge",
        grid=(T // tm,),
        in_specs=_gate_specs(tm) + [rowb(1024), rowb(1024), full(wa_t), full(wh_t)],
        out_specs=[rowb(D_MODEL)] * 3,
        out_shape=[jax.ShapeDtypeStruct((T, D_MODEL), BF16)] * 3,
        compiler_params=_params(("arbitrary",)),
    )(*([proj] * 8), ya, yh, wa_t, wh_t)


def _out_norm_loss(merged, x2, tgt2, gpost, wout):
    T = merged.shape[0]
    tm = min(256, T)

    def body(m_ref, x_ref, t_ref, gpost_ref, wo_ref, dy_ref, dm_ref, dout_ref, loss_ref, dgpost_ref):
        @pl.when(pl.program_id(0) == 0)
        def _():
            loss_ref[...] = jnp.zeros(loss_ref.shape, F32)
            dgpost_ref[...] = jnp.zeros(dgpost_ref.shape, F32)

        y = _dot(m_ref[...], wo_ref[...])
        r2 = lax.rsqrt(jnp.mean(y * y, axis=-1, keepdims=True) + NORM_EPS)
        yn = y * r2
        gpost = gpost_ref[...]
        err = x_ref[...] + yn * gpost - t_ref[...]
        loss_ref[...] += jnp.sum(err * err, axis=0, keepdims=True)
        dout = err * (1.0 / D_MODEL)
        dout_ref[...] = dout
        dgpost_ref[...] += jnp.sum(dout * yn, axis=0, keepdims=True)
        dyn = dout * gpost
        dy = (r2 * (dyn - yn * jnp.mean(dyn * yn, axis=-1, keepdims=True))).astype(BF16)
        dy_ref[...] = dy
        dm_ref[...] = _dot(dy, wo_ref[...], NT).astype(BF16)

    rowb = lambda: pl.BlockSpec((tm, D_MODEL), lambda i: (i, 0))
    vec = lambda: pl.BlockSpec((1, D_MODEL), lambda i: (0, 0))
    return pl.pallas_call(
        body, name="out_norm_loss",
        grid=(T // tm,),
        in_specs=[rowb(), rowb(), rowb(), vec(), pl.BlockSpec(wout.shape, lambda i: (0, 0))],
        out_specs=[rowb(), rowb(), rowb(), vec(), vec()],
        out_shape=[jax.ShapeDtypeStruct((T, D_MODEL), BF16)] * 2
                  + [jax.ShapeDtypeStruct((T, D_MODEL), F32)] + [jax.ShapeDtypeStruct((1, D_MODEL), F32)] * 2,
        compiler_params=_params(("arbitrary",)),
    )(merged, x2, tgt2, gpost, wout)


def _branch_bwd(proj, dm, ua, uh, wa_t, wh_t):
    T = proj.shape[0]
    tm = min(256, T)

    def body(*refs):
        (dm_ref, ua_ref, uh_ref, wa_ref, wh_ref,
         dua_ref, duh_ref, dgg_ref, dya_ref, dyh_ref) = refs[8:]
        ga, gh = _gates(refs)
        sa, sh = _sigmoid(ga), _sigmoid(gh)
        dm = dm_ref[...].astype(F32)
        dua = (dm * sa).astype(BF16)
        duh = (dm * sh).astype(BF16)
        dua_ref[...] = dua
        duh_ref[...] = duh
        dgg_ref[:, :D_MODEL] = (dm * ua_ref[...].astype(F32) * (sa * (1.0 - sa))).astype(BF16)
        dgg_ref[:, D_MODEL:] = (dm * uh_ref[...].astype(F32) * (sh * (1.0 - sh))).astype(BF16)
        dya_ref[...] = _dot(dua, wa_ref[...]).astype(BF16)
        dyh_ref[...] = _dot(duh, wh_ref[...]).astype(BF16)

    rowb = lambda w: pl.BlockSpec((tm, w), lambda i: (i, 0))
    full = lambda a: pl.BlockSpec(a.shape, lambda i: (0, 0))
    return pl.pallas_call(
        body, name="branch_bwd",
        grid=(T // tm,),
        in_specs=_gate_specs(tm) + [rowb(D_MODEL)] * 3 + [full(wa_t), full(wh_t)],
        out_specs=[rowb(D_MODEL)] * 2 + [rowb(2 * D_MODEL)] + [rowb(1024)] * 2,
        out_shape=[jax.ShapeDtypeStruct((T, D_MODEL), BF16)] * 2 + [jax.ShapeDtypeStruct((T, 2 * D_MODEL), BF16)]
                  + [jax.ShapeDtypeStruct((T, 1024), BF16)] * 2,
        compiler_params=_params(("arbitrary",)),
    )(*([proj] * 8), dm, ua, uh, wa_t, wh_t)


def _tn_matmul(L, R, bm, bn, name):
    T, M = L.shape
    N = R.shape[1]

    def body(l_ref, r_ref, out_ref):
        out_ref[...] = _dot(l_ref[...], r_ref[...], TN).astype(BF16)

    return pl.pallas_call(
        body, name=name,
        grid=(N // bn, M // bm),
        in_specs=[pl.BlockSpec((T, bm), lambda j, i: (0, i)),
                  pl.BlockSpec((T, bn), lambda j, i: (0, j))],
        out_specs=pl.BlockSpec((bm, bn), lambda j, i: (i, j)),
        out_shape=jax.ShapeDtypeStruct((M, N), BF16),
        compiler_params=_params(("arbitrary", "arbitrary")),
    )(L, R)


def _tn_matmul_pieces(pieces, R, bm, bn, name):
    T, N = R.shape
    M = sum(p.shape[1] for p in pieces)
    out, row0 = None, 0
    for q, L in enumerate(pieces):
        off = row0 // bm

        def body(*refs):
            refs[-1][...] = _dot(refs[0][...], refs[1][...], TN).astype(BF16)

        prev = [] if out is None else [out]
        out = pl.pallas_call(
            body, name="%s_%d" % (name, q),
            grid=(N // bn, L.shape[1] // bm),
            in_specs=[pl.BlockSpec((T, bm), lambda j, i: (0, i)), pl.BlockSpec((T, bn), lambda j, i: (0, j))]
                     + [pl.BlockSpec(memory_space=pl.ANY) for _ in prev],
            out_specs=pl.BlockSpec((bm, bn), lambda j, i, off=off: (off + i, j)),
            out_shape=jax.ShapeDtypeStruct((M, N), BF16),
            input_output_aliases={2: 0} if prev else {},
            compiler_params=_params(("arbitrary", "arbitrary")),
        )(L, R, *prev)
        row0 += L.shape[1]
    return out


def _dh_prenorm_bwd(pieces, wt_in, x2, dout, gpre, chip_sums):
    T = x2.shape[0]
    tm = min(1024, T)
    ne = 4
    te = tm // ne
    tk = 512
    nk = IN_WIDTH // tk
    nt = T // tm
    n = len(chip_sums)
    npiece = len(pieces)
    counts = [p.shape[1] // tk for p in pieces]
    starts = [sum(counts[:q]) for q in range(npiece)]
    assert sum(counts) == nk and all(p.shape[1] % tk == 0 for p in pieces)

    def body(*all_refs):
        piece_refs = all_refs[:npiece]
        w_ref, x_ref, dout_ref, g_ref = all_refs[npiece:npiece + 4]
        refs = all_refs[npiece + 4:]
        ins = refs[:n]
        gx_ref, dg_ref = refs[n], refs[n + 1]
        outs = refs[n + 2:2 * n + 2]
        acc, send_sems, recv_sems, local_sems = refs[2 * n + 2:]
        i, k = pl.program_id(0), pl.program_id(1)
        exchange = _ChipExchange(ins, outs, send_sems, recv_sems, local_sems)

        @pl.when((i == 0) & (k == 0))
        def _():
            dg_ref[...] = jnp.zeros(dg_ref.shape, F32)
            exchange.start()

        @pl.when((i == nt - 1) & (k == nk + ne - 1))
        def _():
            exchange.wait()

        @pl.when(k == 0)
        def _():
            acc[...] = jnp.zeros(acc.shape, F32)

        for q in range(npiece):
            @pl.when((k >= starts[q]) & (k < starts[q] + counts[q]))
            def _(q=q):
                acc[...] += _dot(piece_refs[q][...], w_ref[...])

        @pl.when(k >= nk)
        def _():
            dh = acc[pl.ds(pl.multiple_of((k - nk) * te, te), te), :]
            x = x_ref[...]
            r = lax.rsqrt(jnp.mean(x * x, axis=-1, keepdims=True) + NORM_EPS)
            xn = x * r
            dg_ref[...] += jnp.sum(dh * xn, axis=0, keepdims=True)
            dxn = dh * g_ref[...]
            gx_ref[...] = dout_ref[...] + r * (dxn - xn * jnp.mean(dxn * xn, axis=-1, keepdims=True))

    piece_spec = lambda q: pl.BlockSpec((tm, tk), lambda i, k: (i, jnp.clip(k - starts[q], 0, counts[q] - 1)))
    rowb = lambda: pl.BlockSpec((te, D_MODEL), lambda i, k: (ne * i + jnp.clip(k - nk, 0, ne - 1), 0))
    vec = lambda: pl.BlockSpec((1, D_MODEL), lambda i, k: (0, 0))
    anyspec = lambda: pl.BlockSpec(memory_space=pl.ANY)
    return pl.pallas_call(
        body, name="dh_prenorm_bwd",
        grid=(nt, nk + ne),
        in_specs=[piece_spec(q) for q in range(npiece)]
                 + [pl.BlockSpec((tk, D_MODEL), lambda i, k: (jnp.minimum(k, nk - 1), 0)),
                    rowb(), rowb(), vec()] + [anyspec() for _ in chip_sums],
        out_specs=[rowb(), vec()] + [anyspec() for _ in chip_sums],
        out_shape=[jax.ShapeDtypeStruct((T, D_MODEL), F32), jax.ShapeDtypeStruct((1, D_MODEL), F32)]
                  + [jax.ShapeDtypeStruct(s.shape, s.dtype) for s in chip_sums],
        scratch_shapes=[pltpu.VMEM((tm, D_MODEL), F32)] + _ChipExchange.scratch(n),
        compiler_params=_params(("arbitrary", "arbitrary")),
    )(*pieces, wt_in, x2, dout, gpre, *chip_sums)


def _sum_slots(recv, br, name):
    nslot, R, C = recv.shape

    def body(r_ref, out_ref):
        acc = r_ref[0].astype(F32)
        for s in range(1, nslot):
            acc = acc + r_ref[s].astype(F32)
        out_ref[...] = acc

    return pl.pallas_call(
        body, name=name,
        grid=(R // br,),
        in_specs=[pl.BlockSpec((nslot, br, C), lambda i: (0, i, 0))],
        out_specs=pl.BlockSpec((br, C), lambda i: (i, 0)),
        out_shape=jax.ShapeDtypeStruct((R, C), F32),
        compiler_params=_params(("arbitrary",)),
    )(recv)


def _adamw_math(w, g, m, v):
    m = ADAM_B1 * m + (1.0 - ADAM_B1) * g
    v = ADAM_B2 * v + (1.0 - ADAM_B2) * (g * g)
    m_hat = m / (1.0 - ADAM_B1 ** ADAM_STEP)
    v_hat = v / (1.0 - ADAM_B2 ** ADAM_STEP)
    delta = -ADAM_LR * (m_hat / (jnp.sqrt(v_hat) + ADAM_EPS) + ADAM_WD * w)
    return delta, m, v


def _adamw(w, g, m, v, br, name):
    R, C = g.shape
    lead = (None,) * (w.ndim - 2)

    def body(w_ref, g_ref, m_ref, v_ref, d_ref, nm_ref, nv_ref):
        d_ref[...], nm_ref[...], nv_ref[...] = _adamw_math(w_ref[...], g_ref[...], m_ref[...], v_ref[...])

    spec = lambda: pl.BlockSpec(lead + (br, C), lambda i: (0,) * len(lead) + (i, 0))
    return pl.pallas_call(
        body, name=name,
        grid=(R // br,),
        in_specs=[spec(), pl.BlockSpec((br, C), lambda i: (i, 0)), spec(), spec()],
        out_specs=[spec(), spec(), spec()],
        out_shape=[jax.ShapeDtypeStruct(w.shape, F32)] * 3,
        compiler_params=_params(("arbitrary",)),
    )(w, g, m, v)


def _sibling_exchange(partials, name):
    n = len(partials)

    def body(*refs):
        ins, outs = refs[:n], refs[n:2 * n]
        send_sems, recv_sems = refs[2 * n:]
        x, y, c = _place()

        def copy(a, p):
            return pltpu.make_async_remote_copy(
                src_ref=ins[a].at[p, 1 - c], dst_ref=outs[a].at[p],
                send_sem=send_sems.at[a, p], recv_sem=recv_sems.at[a, p],
                device_id=(x, y, 1 - c), device_id_type=pl.DeviceIdType.MESH)

        copies = [copy(a, p) for p in range(4) for a in range(n)]
        for cp in copies:
            cp.start()
        for cp in copies:
            cp.wait()

    anyspec = lambda: pl.BlockSpec(memory_space=pl.ANY)
    return pl.pallas_call(
        body, name=name,
        in_specs=[anyspec() for _ in partials],
        out_specs=[anyspec() for _ in partials],
        out_shape=[jax.ShapeDtypeStruct((4,) + p.shape[2:], p.dtype) for p in partials],
        scratch_shapes=[pltpu.SemaphoreType.DMA((n, 4)), pltpu.SemaphoreType.DMA((n, 4))],
    )(*partials)


def _chip_sum(partial, from_sibling, br, name):
    _, _, R, C = partial.shape
    cls = lax.axis_index("c").astype(jnp.int32).reshape(1)

    def body(c_ref, mine_ref, sib_ref, out_ref):
        out_ref[...] = (mine_ref[...].astype(F32) + sib_ref[...].astype(F32)).astype(BF16)

    grid_spec = pltpu.PrefetchScalarGridSpec(
        num_scalar_prefetch=1,
        grid=(4, R // br),
        in_specs=[pl.BlockSpec((None, None, br, C), lambda p, i, c: (p, c[0], i, 0)),
                  pl.BlockSpec((None, br, C), lambda p, i, c: (p, i, 0))],
        out_specs=pl.BlockSpec((None, br, C), lambda p, i, c: (p, i, 0)),
    )
    return pl.pallas_call(
        body, name=name, grid_spec=grid_spec,
        out_shape=jax.ShapeDtypeStruct((4, R, C), BF16),
        compiler_params=_params(("arbitrary", "arbitrary")),
    )(cls, partial, from_sibling)


def _all_reduce_small(packed):
    shape = packed.shape

    def body(in_ref, out_ref, slots, send_sems, recv_sems):
        x, y, c = _place()
        my_slot = 4 * x + 2 * y + c

        def peer(k):
            return (x ^ ((k >> 2) & 1), y ^ ((k >> 1) & 1), c ^ (k & 1))

        def copy(k):
            p = peer(k)
            return pltpu.make_async_remote_copy(
                src_ref=in_ref, dst_ref=slots.at[my_slot],
                send_sem=send_sems.at[k - 1], recv_sem=recv_sems.at[k - 1],
                device_id=p, device_id_type=pl.DeviceIdType.MESH)

        def arrival(k):
            p = peer(k)
            return pltpu.make_async_remote_copy(
                src_ref=in_ref, dst_ref=slots.at[4 * p[0] + 2 * p[1] + p[2]],
                send_sem=send_sems.at[k - 1], recv_sem=recv_sems.at[k - 1],
                device_id=p, device_id_type=pl.DeviceIdType.MESH)

        sends = [copy(k) for k in range(1, N_DEV)]
        for cp in sends:
            cp.start()
        slots[my_slot] = in_ref[...]
        for k in range(1, N_DEV):
            arrival(k).wait_recv()
        for cp in sends:
            cp.wait_send()
        acc = slots[0]
        for s in range(1, N_DEV):
            acc = acc + slots[s]
        out_ref[...] = acc

    return pl.pallas_call(
        body, name="all_reduce_small",
        in_specs=[pl.BlockSpec(memory_space=pltpu.VMEM)],
        out_specs=pl.BlockSpec(memory_space=pltpu.VMEM),
        out_shape=jax.ShapeDtypeStruct(shape, F32),
        scratch_shapes=[pltpu.VMEM((N_DEV,) + shape, F32),
                        pltpu.SemaphoreType.DMA((7,)), pltpu.SemaphoreType.DMA((7,))],
    )(packed)


def _pack_small(norm_pre, norm_post, lb_logits, hgrn_norm, rel_bias, sinks, extra=None):
    tail = [hgrn_norm.reshape(1, 1024), rel_bias.reshape(1, 512), sinks.reshape(1, 16)]
    used = 1024 + 512 + 16
    if extra is not None:
        tail.append(extra.reshape(1, 1))
        used += 1
    tail.append(jnp.zeros((1, D_MODEL - used), F32))
    rows = [norm_pre.reshape(1, D_MODEL), norm_post.reshape(1, D_MODEL), lb_logits.reshape(1, D_MODEL),
            jnp.concatenate(tail, axis=1), jnp.zeros((4, D_MODEL), F32)]
    return jnp.concatenate(rows, axis=0)


def _unpack_small(p):
    return (p[0:1], p[3, 1024:1536].reshape(REL_BUCKETS, ATTN_HEADS), p[3:4, 1536:1552],
            p[2].reshape(2, 1024), p[3, 0:1024].reshape(1, HGRN_HEADS, HGRN_DIM), p[1:2])


def _local_step(nseq, S, x2, tgt2, proj, h, rel_bias, attn_sinks, lb_logits, hgrn_norm, norm_post, shards):
    nb = S // ATTN_BLOCK
    bucket = jnp.asarray(_t5_bucket_table())
    gain3 = hgrn_norm.reshape(HGRN_HEADS, 1, HGRN_DIM)

    bias = _bias_table(rel_bias, bucket)
    ya = _attn_fwd(proj, bias, attn_sinks, nseq, nb)
    o, yh, states, wout, wa_t, wh_t = _hgrn_fwd(proj, lb_logits, gain3, nseq, S, shards)
    merged, ua, uh = _branch_merge(proj, ya, yh, wa_t, wh_t)
    dy, dm, dout, loss_cols, d_gpost = _out_norm_loss(merged, x2, tgt2, norm_post, wout)
    dua, duh, dgg, dya, dyh = _branch_bwd(proj, dm, ua, uh, wa_t, wh_t)

    p_out = _tn_matmul(merged, dy, 256, 1024, "dw_out")
    p_a = _tn_matmul(dua, ya, 256, 1024, "dw_branch_attn")
    p_h = _tn_matmul(duh, yh, 256, 1024, "dw_branch_hgrn")
    small_sums = _chip_sums([p_out, p_a, p_h], (128, 128, 128), ("dw_out", "dw_branch_attn", "dw_branch_hgrn"),
                            "sibling_exchange_small")

    dq, dkv, dg, dbias, d_sinks = _attn_bwd(proj, bias, attn_sinks, dya, nseq, nb)
    d_rel_bias = _bias_table_bwd(dbias, bucket)
    dh4, d_gain, d_lbl, r_out, r_a, r_h = _hgrn_bwd(proj, lb_logits, gain3, o, dyh, states, nseq, S, small_sums)
    dproj = [dq, dkv, dg, dh4, dgg]
    p_in = _tn_matmul_pieces(dproj, h, 512, 1024, "dw_in")
    return dproj, dout, p_in, r_out, r_a, r_h, d_gpost, d_lbl, d_gain, d_rel_bias, d_sinks, loss_cols


def _chip_sums(partials, block_rows, names, exchange_name):
    split = [p.reshape(4, 2, p.shape[0] // N_DEV, p.shape[1]) for p in partials]
    from_sibling = _sibling_exchange(split, exchange_name)
    return [_chip_sum(p, f, br, "chip_sum_" + nm) for p, f, br, nm in zip(split, from_sibling, block_rows, names)]


def kernel(x, norm_pre, w_in, rel_bias, attn_sinks, lb_logits, hgrn_norm, w_branch_attn, w_branch_hgrn, w_out, norm_post, loss_target, m_norm_pre, m_w_in, m_rel_bias, m_attn_sinks, m_lb_logits, m_hgrn_norm, m_w_branch_attn, m_w_branch_hgrn, m_w_out, m_norm_post, v_norm_pre, v_w_in, v_rel_bias, v_attn_sinks, v_lb_logits, v_hgrn_norm, v_w_branch_attn, v_w_branch_hgrn, v_w_out, v_norm_post):
    nseq, S, _ = x.shape
    T = nseq * S
    x2 = x.reshape(T, D_MODEL)
    tgt2 = loss_target.reshape(T, D_MODEL)

    h = _prenorm(x2, norm_pre)
    proj, wt_in = _gather_inproj(h, w_in[0].T.astype(BF16))
    shards = [w_out[0].astype(BF16), w_branch_attn[0].T.astype(BF16), w_branch_hgrn[0].T.astype(BF16)]

    (dproj, dout, p_in, r_out, r_a, r_h, d_gpost, d_lbl, d_gain, d_rel_bias, d_sinks, loss_cols) = _local_step(
        nseq, S, x2, tgt2, proj, h, rel_bias, attn_sinks, lb_logits, hgrn_norm, norm_post, shards)

    in_sums = _chip_sums([p_in], (192,), ("dw_in",), "sibling_exchange_w_in")
    grad_x2, d_gpre, r_in = _dh_prenorm_bwd(dproj, wt_in, x2, dout, norm_pre, in_sums)
    g_in_t = _sum_slots(r_in, 192, "sum_dw_in")
    g_out = _sum_slots(r_out, 128, "sum_dw_out")
    g_a = _sum_slots(r_a, 128, "sum_dw_branch_attn").T
    g_h = _sum_slots(r_h, 128, "sum_dw_branch_hgrn").T

    loss_part = 0.5 / D_MODEL * jnp.sum(loss_cols)
    packed = _pack_small(d_gpre, d_gpost, d_lbl, d_gain, d_rel_bias, d_sinks, extra=loss_part)
    total = _all_reduce_small(packed)
    loss = total[3, 1024 + 512 + 16]
    sm_w = _pack_small(norm_pre, norm_post, lb_logits, hgrn_norm, rel_bias, attn_sinks)
    sm_m = _pack_small(m_norm_pre, m_norm_post, m_lb_logits, m_hgrn_norm, m_rel_bias, m_attn_sinks)
    sm_v = _pack_small(v_norm_pre, v_norm_post, v_lb_logits, v_hgrn_norm, v_rel_bias, v_attn_sinks)
    sm_d, sm_nm, sm_nv = _adamw(sm_w, total, sm_m, sm_v, 8, "adamw_small")

    t = lambda a: jnp.swapaxes(a, 1, 2)
    d_in, nm_in, nv_in = map(t, _adamw(t(w_in), g_in_t, t(m_w_in), t(v_w_in), 192, "adamw_w_in"))
    d_out, nm_out, nv_out = _adamw(w_out, g_out, m_w_out, v_w_out, 128, "adamw_w_out")
    d_a, nm_a, nv_a = _adamw(w_branch_attn, g_a, m_w_branch_attn, v_w_branch_attn, 256, "adamw_w_branch_attn")
    d_h, nm_h, nv_h = _adamw(w_branch_hgrn, g_h, m_w_branch_hgrn, v_w_branch_hgrn, 256, "adamw_w_branch_hgrn")

    def group(small, big_in, big_a, big_h, big_out):
        npre, rb, sk, lbl, hn, npost = _unpack_small(small)
        return (npre, big_in, rb, sk, lbl, hn, big_a, big_h, big_out, npost)

    return (loss, grad_x2.reshape(nseq, S, D_MODEL),
            *group(total, t(g_in_t[None]), g_a[None], g_h[None], g_out[None]),
            *group(sm_d, d_in, d_a, d_h, d_out),
            *group(sm_nm, nm_in, nm_a, nm_h, nm_out),
            *group(sm_nv, nv_in, nv_a, nv_h, nv_out))
```

```python
import functools
import math

import numpy as np
import jax
import jax.numpy as jnp
from jax import lax
from jax.experimental import pallas as pl
from jax.experimental.pallas import tpu as pltpu

F32 = jnp.float32
BF16 = jnp.bfloat16

D_MODEL = 2048
ATTN_HEADS = 16
ATTN_HEAD_DIM = 64
ATTN_GROUP = 4
ATTN_BLOCK = 128
HGRN_HEADS = 8
HGRN_DIM = 128
HGRN_CHUNK = 64
HGRN_SUB = 16
HGRN_PAR = 8
HGRN_COLS = 512
REL_BUCKETS = 32
REL_MAX_DIST = 128
NORM_EPS = 1e-6
C_AQ, C_AK, C_AV, C_AG = 0, 1024, 1280, 1536
C_HQ, C_HF, C_HI, C_HG = 2560, 3584, 4608, 5632
C_GA, C_GH = 6656, 8704
IN_WIDTH = 10752
N_DEV = 8
assert all(c0 % HGRN_COLS == 0 for c0 in (C_HQ, C_HF, C_HI, C_HG)) and (HGRN_PAR * HGRN_DIM) % HGRN_COLS == 0

ADAM_LR = 0.001
ADAM_B1 = 0.9
ADAM_B2 = 0.999
ADAM_EPS = 1e-08
ADAM_WD = 0.01
ADAM_STEP = 10

VMEM_LIMIT_V7X = 56 * 1024 * 1024
NEG_BIG = -1e30

NT = (((1,), (1,)), ((), ()))
TN = (((0,), (0,)), ((), ()))
NN = (((1,), (0,)), ((), ()))


def _dot(a, b, dims=NN):
    return lax.dot_general(a, b, dims, preferred_element_type=F32)


def _params(sem=None):
    return pltpu.CompilerParams(dimension_semantics=sem, vmem_limit_bytes=VMEM_LIMIT_V7X)


def _sigmoid(x):
    return 1.0 / (1.0 + jnp.exp(-x))


def _t5_bucket_table():
    qi = np.arange(ATTN_BLOCK)[:, None]
    si = np.arange(2 * ATTN_BLOCK)[None, :]
    dist = qi + ATTN_BLOCK - si
    max_exact = REL_BUCKETS // 2
    d = np.maximum(dist, 0)
    df = np.maximum(d, 1).astype(np.float32)
    large = max_exact + (np.log(df / np.float32(max_exact)).astype(np.float32)
                         / np.float32(math.log(REL_MAX_DIST / max_exact))
                         * np.float32(REL_BUCKETS - max_exact)).astype(np.int32)
    large = np.minimum(large, REL_BUCKETS - 1)
    return np.where(d < max_exact, d, large).astype(np.int32)


def _place():
    return lax.axis_index("x"), lax.axis_index("y"), lax.axis_index("c")


class _GatherOps:
    def __init__(self, ins, outs, send_sems, recv_sems, local_sems):
        self.ins, self.outs = ins, outs
        self.send_sems, self.recv_sems, self.local_sems = send_sems, recv_sems, local_sems
        x, y, c = _place()
        self.c = c
        self.me, self.sibling = (x, y, c), (x, y, 1 - c)
        self.chips = [(1 - x, y), (x, 1 - y), (1 - x, 1 - y)]

    def _rows(self, a, dev):
        r = self.ins[a].shape[0]
        return self.outs[a].at[pl.ds((4 * dev[0] + 2 * dev[1] + dev[2]) * r, r), :]

    def _copy(self, a, k, block, to, src=None):
        return pltpu.make_async_remote_copy(
            src_ref=self._rows(a, block) if src is None else src, dst_ref=self._rows(a, block),
            send_sem=self.send_sems.at[a, k], recv_sem=self.recv_sems.at[a, k],
            device_id=to, device_id_type=pl.DeviceIdType.MESH)

    def local(self, a, stage):
        return (pltpu.make_async_copy(self.ins[a], stage, self.local_sems.at[a, 0]),
                pltpu.make_async_copy(stage, self._rows(a, self.me), self.local_sems.at[a, 1]))

    def to_sibling(self, a):
        return self._copy(a, 0, self.me, self.sibling, src=self.ins[a])

    def to_chip(self, a, q):
        return self._copy(a, 1 + q, self.me, (*self.chips[q], self.c), src=self.ins[a])

    def forward(self, a, q):
        return self._copy(a, 4 + q, (*self.chips[q], self.c), self.sibling)

    def from_sibling(self, a):
        return self._copy(a, 0, self.sibling, self.me)

    def from_chip(self, a, q):
        return self._copy(a, 1 + q, (*self.chips[q], self.c), self.me)

    def forwarded(self, a, q):
        return self._copy(a, 4 + q, (*self.chips[q], 1 - self.c), self.me)

    def sends(self, a):
        return [self.to_sibling(a)] + [self.to_chip(a, q) for q in range(3)] + [self.forward(a, q) for q in range(3)]


class _ChipExchange:
    def __init__(self, ins, outs, send_sems, recv_sems, local_sems, stages):
        self.ins, self.outs, self.stages = ins, outs, stages
        self.send_sems, self.recv_sems, self.local_sems = send_sems, recv_sems, local_sems
        self.x, self.y, self.c = _place()
        self.my_chip = 2 * self.x + self.y

    def _peer(self, q):
        return (self.x ^ (q >> 1), self.y ^ (q & 1))

    def _copy(self, a, q, src_block, dst_slot):
        px, py = self._peer(q)
        return pltpu.make_async_remote_copy(
            src_ref=self.ins[a].at[src_block], dst_ref=self.outs[a].at[dst_slot],
            send_sem=self.send_sems.at[a, q - 1], recv_sem=self.recv_sems.at[a, q - 1],
            device_id=(px, py, self.c), device_id_type=pl.DeviceIdType.MESH)

    def _send(self, a, q):
        px, py = self._peer(q)
        return self._copy(a, q, 2 * px + py, self.my_chip)

    def _arrival(self, a, q):
        px, py = self._peer(q)
        return self._copy(a, q, self.my_chip, 2 * px + py)

    def _local(self, a):
        return (pltpu.make_async_copy(self.ins[a].at[self.my_chip], self.stages[a], self.local_sems.at[a, 0]),
                pltpu.make_async_copy(self.stages[a], self.outs[a].at[self.my_chip], self.local_sems.at[a, 1]))

    def start(self):
        for a in range(len(self.ins)):
            self._local(a)[0].start()
            for q in range(1, 4):
                self._send(a, q).start()

    def wait(self):
        for a in range(len(self.ins)):
            own_in, own_out = self._local(a)
            own_in.wait()
            own_out.start()
            own_out.wait()
        for a in range(len(self.ins)):
            for q in range(1, 4):
                self._arrival(a, q).wait_recv()
        for a in range(len(self.ins)):
            for q in range(1, 4):
                self._send(a, q).wait_send()

    @staticmethod
    def scratch(sums):
        n = len(sums)
        return ([pltpu.SemaphoreType.DMA((n, 3)), pltpu.SemaphoreType.DMA((n, 3)), pltpu.SemaphoreType.DMA((n, 2))]
                + [pltpu.VMEM(s.shape[1:], s.dtype) for s in sums])


_GATHER_SEMS = 7
LOCAL_DMA_THREAD = 1

INPROJ_TILE = 896


def _prenorm(x2, gpre):
    T = x2.shape[0]
    tm = min(512, T)

    def body(x_ref, g_ref, h_ref):
        x = x_ref[...]
        r = lax.rsqrt(jnp.mean(x * x, axis=-1, keepdims=True) + NORM_EPS)
        h_ref[...] = (x * r * g_ref[...]).astype(BF16)

    return pl.pallas_call(
        body, name="prenorm",
        grid=(T // tm,),
        in_specs=[pl.BlockSpec((tm, D_MODEL), lambda i: (i, 0)), pl.BlockSpec((1, D_MODEL), lambda i: (0, 0))],
        out_specs=pl.BlockSpec((tm, D_MODEL), lambda i: (i, 0)),
        out_shape=jax.ShapeDtypeStruct((T, D_MODEL), BF16),
        compiler_params=_params(("arbitrary",)),
    )(x2, gpre)


def _gather_inproj(h, wt_shard):
    T = h.shape[0]
    tm = min(1024, T)
    nm = T // tm
    tn = INPROJ_TILE
    ntile = IN_WIDTH // tn
    nstep = ntile * nm

    def body(h_hbm, w_in, proj_hbm, w_out, hbuf, wtile, obuf, own_buf,
             send_sems, recv_sems, local_sems, h_sem, w_sems, o_sems):
        j, i = pl.program_id(0), pl.program_id(1)
        step = j * nm + i
        slot = step % 2
        ops = _GatherOps([w_in], [w_out], send_sems, recv_sems, local_sems)
        x, y, _ = _place()

        def tile_of(jj):
            k = jj // 3
            return 3 * ((2 * x + y) ^ (((k & 1) << 1) | (k >> 1))) + jj % 3

        tile = tile_of(j)

        def h_load():
            return pltpu.make_async_copy(h_hbm, hbuf, h_sem)

        own_in, own_out = ops.local(0, own_buf)

        def store(s, rows, cols):
            return pltpu.make_async_copy(obuf.at[s], proj_hbm.at[rows, cols], o_sems.at[s])

        def window(ii, t):
            return pl.ds(pl.multiple_of(ii * tm, tm), tm), pl.ds(pl.multiple_of(t * tn, tn), tn)

        @pl.when(step == 0)
        def _():
            h_load().start()
            own_in.start()
            ops.to_sibling(0).start()
            ops.to_chip(0, 0).start()
            ops.to_chip(0, 1).start()
            h_load().wait()

        for kk in range(4):
            @pl.when((j == 3 * kk) & (i == 0))
            def _(kk=kk):
                if kk == 0:
                    own_in.wait()
                    own_out.start()
                    own_out.wait()
                    ops.from_sibling(0).wait_recv()
                else:
                    q = kk - 1
                    ops.from_chip(0, q).wait_recv()
                    ops.forward(0, q).start()
                    if q == 0:
                        ops.to_chip(0, 2).start()
                    ops.forwarded(0, q).wait_recv()

        wslot = j % 2

        def fetch(jj, sw):
            rows = pl.ds(pl.multiple_of(tile_of(jj) * tn, tn), tn)
            return pltpu.make_async_copy(w_out.at[rows, :], wtile.at[sw], w_sems.at[sw])

        @pl.when((i == 0) & (j % 3 == 0))
        def _():
            fetch(j, wslot).start(LOCAL_DMA_THREAD)

        @pl.when(i == 0)
        def _():
            fetch(j, wslot).wait()

        @pl.when((i == 0) & (j % 3 != 2))
        def _():
            fetch(j + 1, 1 - wslot).start(LOCAL_DMA_THREAD)

        @pl.when(step >= 2)
        def _():
            store(slot, *window(0, 0)).wait()

        hv = hbuf[pl.ds(pl.multiple_of(i * tm, tm), tm), :]
        obuf[slot] = _dot(hv, wtile[wslot], NT).astype(BF16)
        store(slot, *window(i, tile)).start(LOCAL_DMA_THREAD)

        @pl.when(step == nstep - 1)
        def _():
            for s in range(min(2, nstep)):
                store(s, *window(0, 0)).wait()
            for cp in ops.sends(0):
                cp.wait_send()

    anyspec = lambda: pl.BlockSpec(memory_space=pl.ANY)
    return pl.pallas_call(
        body, name="gather_inproj",
        grid=(ntile, nm),
        in_specs=[anyspec(), anyspec()],
        out_specs=[anyspec(), anyspec()],
        out_shape=[jax.ShapeDtypeStruct((T, IN_WIDTH), BF16),
                   jax.ShapeDtypeStruct((N_DEV * wt_shard.shape[0], D_MODEL), BF16)],
        scratch_shapes=[pltpu.VMEM((T, D_MODEL), BF16), pltpu.VMEM((2, tn, D_MODEL), BF16),
                        pltpu.VMEM((2, tm, tn), BF16), pltpu.VMEM(wt_shard.shape, BF16),
                        pltpu.SemaphoreType.DMA((1, _GATHER_SEMS)), pltpu.SemaphoreType.DMA((1, _GATHER_SEMS)),
                        pltpu.SemaphoreType.DMA((1, 2)), pltpu.SemaphoreType.DMA, pltpu.SemaphoreType.DMA((2,)),
                        pltpu.SemaphoreType.DMA((2,))],
        compiler_params=_params(("arbitrary", "arbitrary")),
    )(h, wt_shard)


def _bias_table(rel_bias, bucket):
    def body(rb_ref, bk_ref, out_ref):
        h = pl.program_id(0)
        bk = bk_ref[...]
        acc = jnp.zeros(bk.shape, F32)
        for b in range(REL_BUCKETS):
            acc = jnp.where(bk == b, rb_ref[b, h], acc)
        out_ref[...] = acc

    return pl.pallas_call(
        body, name="bias_table",
        grid=(ATTN_HEADS,),
        in_specs=[pl.BlockSpec(memory_space=pltpu.SMEM),
                  pl.BlockSpec((ATTN_BLOCK, 2 * ATTN_BLOCK), lambda h: (0, 0))],
        out_specs=pl.BlockSpec((None, ATTN_BLOCK, 2 * ATTN_BLOCK), lambda h: (h, 0, 0)),
        out_shape=jax.ShapeDtypeStruct((ATTN_HEADS, ATTN_BLOCK, 2 * ATTN_BLOCK), F32),
        compiler_params=_params(("arbitrary",)),
    )(rel_bias, bucket)


def _bias_table_bwd(dbias, bucket):
    def body(db_ref, bk_ref, out_ref):
        h = pl.program_id(0)
        bk = bk_ref[...]
        db = db_ref[...]
        for b in range(REL_BUCKETS):
            out_ref[b, h] = jnp.sum(jnp.where(bk == b, db, 0.0))

    return pl.pallas_call(
        body, name="bias_table_bwd",
        grid=(ATTN_HEADS,),
        in_specs=[pl.BlockSpec((None, ATTN_BLOCK, 2 * ATTN_BLOCK), lambda h: (h, 0, 0)),
                  pl.BlockSpec((ATTN_BLOCK, 2 * ATTN_BLOCK), lambda h: (0, 0))],
        out_specs=pl.BlockSpec(memory_space=pltpu.SMEM),
        out_shape=jax.ShapeDtypeStruct((REL_BUCKETS, ATTN_HEADS), F32),
        compiler_params=_params(("arbitrary",)),
    )(dbias, bucket)


def _attn_common(qkvg, kv_prev, blk):
    lane = lax.broadcasted_iota(jnp.int32, (1, 128), 1)
    half = (lane < ATTN_HEAD_DIM, lane >= ATTN_HEAD_DIM)
    kv_cur = qkvg[:, C_AK:C_AG]
    win = jnp.concatenate([kv_prev, kv_cur], axis=0)
    k_slab, v_slab = [], []
    for r in range(2):
        ks = win[:, 128 * r:128 * r + 128]
        vs = win[:, 256 + 128 * r:256 + 128 * r + 128]
        k_slab.append((ks, pltpu.roll(ks, ATTN_HEAD_DIM, 1)))
        v_slab.append((vs, pltpu.roll(vs, ATTN_HEAD_DIM, 1)))
    rows4 = ATTN_GROUP * ATTN_BLOCK
    qi = lax.broadcasted_iota(jnp.int32, (rows4, 2 * ATTN_BLOCK), 0) & (ATTN_BLOCK - 1)
    si = lax.broadcasted_iota(jnp.int32, (rows4, 2 * ATTN_BLOCK), 1)
    valid = (si > qi) & (si <= qi + ATTN_BLOCK) & ((si >= ATTN_BLOCK) | (blk > 0))
    return half, k_slab, v_slab, valid


def _stack_heads(half, slab0, slab1):
    return jnp.concatenate([jnp.where(half[0], slab0, 0.0), jnp.where(half[1], slab0, 0.0),
                            jnp.where(half[0], slab1, 0.0), jnp.where(half[1], slab1, 0.0)], axis=0)


def _unstack_heads(half, x4):
    B = ATTN_BLOCK
    return (jnp.where(half[0], x4[0:B], x4[B:2 * B]), jnp.where(half[0], x4[2 * B:3 * B], x4[3 * B:4 * B]))


def _attn_group(j, qkvg, half, k_slab, v_slab, valid, bias_ref, sinks_ref):
    r, aj = j // 2, j % 2
    pick = (lambda a, b: jnp.where(half[0], a, b)) if aj == 0 else (lambda a, b: jnp.where(half[0], b, a))
    kb = pick(*k_slab[r]).astype(BF16)
    vb = pick(*v_slab[r]).astype(BF16)
    q4 = _stack_heads(half, qkvg[:, 256 * j:256 * j + 128], qkvg[:, 256 * j + 128:256 * j + 256]).astype(BF16)
    bias4 = bias_ref[ATTN_GROUP * j:ATTN_GROUP * (j + 1)].reshape(valid.shape)
    yield
    s = _dot(q4, kb, NT) * (ATTN_HEAD_DIM ** -0.5) + bias4
    s = jnp.where(valid, s, NEG_BIG)
    rowblk = lax.broadcasted_iota(jnp.int32, (valid.shape[0], 1), 0) // ATTN_BLOCK
    sink = jnp.full((valid.shape[0], 1), sinks_ref[0, ATTN_GROUP * j], F32)
    for b in range(1, ATTN_GROUP):
        sink = jnp.where(rowblk == b, sinks_ref[0, ATTN_GROUP * j + b], sink)
    m = jnp.maximum(jnp.max(s, axis=-1, keepdims=True), sink)
    e = jnp.exp(s - m)
    es = jnp.exp(sink - m)
    inv = 1.0 / (jnp.sum(e, axis=-1, keepdims=True) + es)
    pn = e * inv
    yield
    o4 = _dot(pn.astype(BF16), vb)
    return dict(r=r, aj=aj, kb=kb, vb=vb, q4=q4, pn=pn, psink=es * inv, o4=o4)


def _attn_specs(nb):
    row = lambda b, i: b * nb + i
    return [
        pl.BlockSpec((ATTN_BLOCK, C_HQ), lambda b, i: (row(b, i), 0)),
        pl.BlockSpec((ATTN_BLOCK, 512), lambda b, i: (row(b, jnp.maximum(i - 1, 0)), 2)),
        pl.BlockSpec((ATTN_HEADS, ATTN_BLOCK, 2 * ATTN_BLOCK), lambda b, i: (0, 0, 0)),
        pl.BlockSpec(memory_space=pltpu.SMEM),
    ]


def _attn_fwd(proj, bias, sinks, nseq, nb):
    T = proj.shape[0]

    def body(qkvg_ref, kvp_ref, bias_ref, sinks_ref, ya_ref):
        qkvg = qkvg_ref[...].astype(F32)
        half, k_slab, v_slab, valid = _attn_common(qkvg, kvp_ref[...].astype(F32), pl.program_id(1))
        groups = _interleave([_attn_group(j, qkvg, half, k_slab, v_slab, valid, bias_ref, sinks_ref)
                              for j in range(ATTN_HEADS // ATTN_GROUP)])
        slabs = []
        for grp in groups:
            slabs += _unstack_heads(half, grp["o4"])
        o_all = jnp.concatenate(slabs, axis=1)
        g = qkvg[:, C_AG:C_HQ]
        ya_ref[...] = (o_all * (g * _sigmoid(g))).astype(BF16)

    return pl.pallas_call(
        body, name="attn_fwd",
        grid=(nseq, nb),
        in_specs=_attn_specs(nb),
        out_specs=pl.BlockSpec((ATTN_BLOCK, 1024), lambda b, i: (b * nb + i, 0)),
        out_shape=jax.ShapeDtypeStruct((T, 1024), BF16),
        compiler_params=_params(("arbitrary", "arbitrary")),
    )(proj, proj, bias, sinks)


def _attn_bwd(proj, bias, sinks, d_ya, nseq, nb):
    T = proj.shape[0]
    S = nb * ATTN_BLOCK
    scale = ATTN_HEAD_DIM ** -0.5

    def body(qkvg_ref, kvp_ref, bias_ref, sinks_ref, dya_ref, dq_ref, dkv_ref, dg_ref, dbias_ref, dsinks_ref):
        b, i = pl.program_id(0), pl.program_id(1)
        first = (b == 0) & (i == 0)

        @pl.when(first)
        def _():
            dbias_ref[...] = jnp.zeros(dbias_ref.shape, F32)
            for h in range(ATTN_HEADS):
                dsinks_ref[0, h] = 0.0

        qkvg = qkvg_ref[...].astype(F32)
        half, k_slab, v_slab, valid = _attn_common(qkvg, kvp_ref[...].astype(F32), i)
        g = qkvg[:, C_AG:C_HQ]
        sg = _sigmoid(g)
        silu_g = g * sg
        dya = dya_ref[...].astype(F32)
        do_all = dya * silu_g
        dq_slabs, o_slabs = [], []
        dk_slab, dv_slab = [None, None], [None, None]
        B = ATTN_BLOCK

        def fold(x, aj):
            return jnp.where(half[aj], x + pltpu.roll(x, ATTN_HEAD_DIM, 1), 0.0)

        def group_bwd(j):
            grp = yield from _attn_group(j, qkvg, half, k_slab, v_slab, valid, bias_ref, sinks_ref)
            pn = grp["pn"]
            do4 = _stack_heads(half, do_all[:, 256 * j:256 * j + 128], do_all[:, 256 * j + 128:256 * j + 256])
            do4b = do4.astype(BF16)
            yield
            dp = _dot(do4b, grp["vb"], NT)
            delta = jnp.sum(do4 * grp["o4"], axis=-1, keepdims=True)
            ds = pn * (dp - delta)
            sink_term = grp["psink"] * delta
            for b4 in range(ATTN_GROUP):
                dsinks_ref[0, ATTN_GROUP * j + b4] += -jnp.sum(sink_term[B * b4:B * (b4 + 1)])
            dbias_ref[ATTN_GROUP * j:ATTN_GROUP * (j + 1)] += ds.reshape(ATTN_GROUP, B, 2 * B)
            dsb = ds.astype(BF16)
            yield
            dq4 = _dot(dsb, grp["kb"]) * scale
            dk_j = fold(_dot(dsb, grp["q4"], TN) * scale, grp["aj"])
            dv_j = fold(_dot(pn.astype(BF16), do4b, TN), grp["aj"])
            return dict(r=grp["r"], dq=_unstack_heads(half, dq4), o=_unstack_heads(half, grp["o4"]), dk=dk_j, dv=dv_j)

        for res in _interleave([group_bwd(j) for j in range(ATTN_HEADS // ATTN_GROUP)]):
            r = res["r"]
            dq_slabs += res["dq"]
            o_slabs += res["o"]
            dk_slab[r] = res["dk"] if dk_slab[r] is None else dk_slab[r] + res["dk"]
            dv_slab[r] = res["dv"] if dv_slab[r] is None else dv_slab[r] + res["dv"]

        dq_ref[...] = jnp.concatenate(dq_slabs, axis=1).astype(BF16)
        o_all = jnp.concatenate(o_slabs, axis=1)
        dg_ref[...] = (dya * o_all * (sg * (1.0 + g * (1.0 - sg)))).astype(BF16)

        dkv = jnp.concatenate(dk_slab + dv_slab, axis=1)
        cur = pl.multiple_of(i * ATTN_BLOCK, ATTN_BLOCK)
        dkv_ref[pl.ds(cur, ATTN_BLOCK), :] = dkv[ATTN_BLOCK:].astype(BF16)

        @pl.when(i > 0)
        def _():
            prev = pl.multiple_of((i - 1) * ATTN_BLOCK, ATTN_BLOCK)
            old = dkv_ref[pl.ds(prev, ATTN_BLOCK), :].astype(F32)
            dkv_ref[pl.ds(prev, ATTN_BLOCK), :] = (old + dkv[:ATTN_BLOCK]).astype(BF16)

    row_spec = lambda w: pl.BlockSpec((ATTN_BLOCK, w), lambda b, i: (b * nb + i, 0))
    return pl.pallas_call(
        body, name="attn_bwd",
        grid=(nseq, nb),
        in_specs=_attn_specs(nb) + [row_spec(1024)],
        out_specs=[row_spec(1024),
                   pl.BlockSpec((S, 512), lambda b, i: (b, 0)),
                   row_spec(1024),
                   pl.BlockSpec((ATTN_HEADS, ATTN_BLOCK, 2 * ATTN_BLOCK), lambda b, i: (0, 0, 0)),
                   pl.BlockSpec(memory_space=pltpu.SMEM)],
        out_shape=[jax.ShapeDtypeStruct((T, 1024), BF16),
                   jax.ShapeDtypeStruct((T, 512), BF16),
                   jax.ShapeDtypeStruct((T, 1024), BF16),
                   jax.ShapeDtypeStruct((ATTN_HEADS, ATTN_BLOCK, 2 * ATTN_BLOCK), F32),
                   jax.ShapeDtypeStruct((1, ATTN_HEADS), F32)],
        compiler_params=_params(("arbitrary", "arbitrary")),
    )(proj, proj, bias, sinks, d_ya)


def _split3(x):
    hi = x.astype(BF16)
    r1 = x - hi.astype(F32)
    mid = r1.astype(BF16)
    lo = (r1 - mid.astype(F32)).astype(BF16)
    return jnp.concatenate([hi, mid, lo], axis=1)


def _tri_sum(tri, x):
    y = _dot(tri, _split3(x))
    return y[:, :128] + y[:, 128:256] + y[:, 256:]


def _interleave(stages):
    results = [None] * len(stages)
    live = list(range(len(stages)))
    while live:
        still = []
        for idx in live:
            try:
                next(stages[idx])
                still.append(idx)
            except StopIteration as done:
                results[idx] = done.value
        live = still
    return results


def _hgrn_chunk(hq, hf, hi, lb):
    C = HGRN_CHUNK
    t = lax.broadcasted_iota(jnp.int32, (C, C), 0)
    s = lax.broadcasted_iota(jnp.int32, (C, C), 1)
    causal = s <= t
    sf = _sigmoid(hf)
    f = lb + (1.0 - lb) * sf
    lf = jnp.log(f)
    yield
    G = _tri_sum(causal.astype(BF16), lf)
    sq = _sigmoid(hq)
    qs = hq * sq
    k = 1.0 - f
    rowblk = lax.broadcasted_iota(jnp.int32, (C, 1), 0) // HGRN_SUB
    qt, kt, eq, ek = [], [], [], []
    for i in range(C // HGRN_SUB):
        lo = HGRN_SUB * i
        ref = G[lo + HGRN_SUB // 2:lo + HGRN_SUB // 2 + 1, :]
        eq_i = jnp.exp(G[lo:lo + HGRN_SUB] - ref)
        ek_i = jnp.exp(jnp.where(rowblk <= i, ref - G, 0.0))
        eq.append(eq_i)
        ek.append(ek_i)
        qt.append((qs[lo:lo + HGRN_SUB] * eq_i).astype(BF16))
        kt.append((k * ek_i).astype(BF16))
    yield
    A = jnp.concatenate([_dot(qt[i], kt[i], NT) for i in range(C // HGRN_SUB)], axis=0)
    A = jnp.where(causal, A, 0.0)
    glast = G[C - 1:C, :]
    eG = jnp.exp(G)
    edec = jnp.exp(glast - G)
    return dict(causal=causal, sf=sf, f=f, G=G, sq=sq, qs=qs, k=k, qt=qt, kt=kt, eq=eq, ek=ek, A=A,
                glast=glast, eG=eG, edec=edec, qhat=qs * eG, kdec=k * edec, v=hi)


def _hgrn_specs(nseq, ng, rows, reverse):
    W = HGRN_PAR
    nblk = W * HGRN_DIM // HGRN_COLS
    gi = (lambda g: ng - 1 - g) if reverse else (lambda g: g)

    def cols(c0):
        return [pl.BlockSpec((rows, HGRN_COLS),
                             lambda h, b, g, q=q: (b * ng + gi(g), c0 // HGRN_COLS + h * nblk + q)) for q in range(nblk)]

    return gi, (cols(C_HQ) + cols(C_HF) + cols(C_HI) + cols(C_HG)
                + [pl.BlockSpec((2, W * HGRN_DIM), lambda h, b, g: (0, h)),
                   pl.BlockSpec((W, 1, HGRN_DIM), lambda h, b, g: (h, 0, 0))])


def _hgrn_operands(refs):
    nblk = HGRN_PAR * HGRN_DIM // HGRN_COLS
    per = HGRN_COLS // HGRN_DIM

    def reader(group):
        def read(rs, w):
            lo = HGRN_DIM * (w % per)
            return group[w // per][rs, lo:lo + HGRN_DIM].astype(F32)
        return read

    readers = [reader(refs[nblk * a:nblk * (a + 1)]) for a in range(4)]
    return readers, refs[4 * nblk], refs[4 * nblk + 1], refs[4 * nblk + 2:]


def _hgrn_fwd(proj, lb_logits, gain3, nseq, S, shards):
    T = proj.shape[0]
    W = HGRN_PAR
    nc = S // HGRN_CHUNK
    cg = min(8, nc)
    ng = nc // cg
    rows = cg * HGRN_CHUNK
    n = len(shards)
    nh = HGRN_HEADS // W

    def body(*all_refs):
        (hq, hf, hi, hg), lbl_ref, gain_ref, refs = _hgrn_operands(all_refs)
        ins = refs[:n]
        o_ref, yh_ref, st_ref = refs[n:n + 3]
        outs = refs[n + 3:2 * n + 3]
        state, send_sems, recv_sems, local_sems = refs[2 * n + 3:2 * n + 7]
        stages = refs[2 * n + 7:]
        step = (pl.program_id(0), pl.program_id(1), pl.program_id(2))
        ops = _GatherOps(ins, outs, send_sems, recv_sems, local_sems)

        @pl.when((step[0] == 0) & (step[1] == 0) & (step[2] == 0))
        def _():
            for a in range(n):
                ops.local(a, stages[a])[0].start()
                ops.to_sibling(a).start()
                for q in range(3):
                    ops.to_chip(a, q).start()

        @pl.when(pl.program_id(2) == 0)
        def _():
            state[...] = jnp.zeros(state.shape, F32)

        lb_all = _sigmoid(lbl_ref[0:1, :] - lbl_ref[1:2, :])

        def head(w, c):
            rs = pl.ds(pl.multiple_of(c * HGRN_CHUNK, HGRN_CHUNK), HGRN_CHUNK)
            ls = slice(HGRN_DIM * w, HGRN_DIM * (w + 1))
            ch = yield from _hgrn_chunk(hq(rs, w), hf(rs, w), hi(rs, w), lb_all[:, ls])
            st = state[w]
            st_ref[w, c] = st
            vb = ch["v"].astype(BF16)
            yield
            o = _dot(ch["qhat"].astype(BF16), st.astype(BF16), NT) + _dot(ch["A"].astype(BF16), vb)
            state[w] = st * jnp.exp(ch["glast"]) + _dot(vb, ch["kdec"].astype(BF16), TN)
            o_ref[rs, ls] = o
            r = lax.rsqrt(jnp.mean(o * o, axis=-1, keepdims=True) + NORM_EPS)
            gate = hg(rs, w)
            yh_ref[rs, ls] = (o * r * gain_ref[w] * (gate * _sigmoid(gate))).astype(BF16)

        def chunk(c, carry):
            _interleave([head(w, c) for w in range(W)])
            return carry

        lax.fori_loop(0, cg, chunk, 0)

        @pl.when((step[0] == nh - 1) & (step[1] == nseq - 1) & (step[2] == ng - 1))
        def _():
            for a in range(n):
                own_in, own_out = ops.local(a, stages[a])
                own_in.wait()
                own_out.start()
                own_out.wait()
                ops.from_sibling(a).wait_recv()
                for q in range(3):
                    ops.from_chip(a, q).wait_recv()
                    ops.forward(a, q).start()
            for a in range(n):
                for q in range(3):
                    ops.forwarded(a, q).wait_recv()
                for cp in ops.sends(a):
                    cp.wait_send()

    _, in_specs = _hgrn_specs(nseq, ng, rows, False)
    out_row = lambda: pl.BlockSpec((rows, W * HGRN_DIM), lambda h, b, g: (b * ng + g, h))
    anyspec = lambda: pl.BlockSpec(memory_space=pl.ANY)
    return pl.pallas_call(
        body, name="hgrn_fwd",
        grid=(nh, nseq, ng),
        in_specs=in_specs + [anyspec() for _ in shards],
        out_specs=[out_row(), out_row(),
                   pl.BlockSpec((None, W, cg, HGRN_DIM, HGRN_DIM), lambda h, b, g: (b, h, g, 0, 0))]
                  + [anyspec() for _ in shards],
        out_shape=[jax.ShapeDtypeStruct((T, 1024), F32),
                   jax.ShapeDtypeStruct((T, 1024), BF16),
                   jax.ShapeDtypeStruct((nseq, HGRN_HEADS, nc, HGRN_DIM, HGRN_DIM), F32)]
                  + [jax.ShapeDtypeStruct((N_DEV * s.shape[0], s.shape[1]), s.dtype) for s in shards],
        scratch_shapes=[pltpu.VMEM((W, HGRN_DIM, HGRN_DIM), F32),
                        pltpu.SemaphoreType.DMA((n, _GATHER_SEMS)), pltpu.SemaphoreType.DMA((n, _GATHER_SEMS)),
                        pltpu.SemaphoreType.DMA((n, 2))] + [pltpu.VMEM(s.shape, s.dtype) for s in shards],
        compiler_params=_params(("arbitrary", "arbitrary", "arbitrary")),
    )(*([proj] * (4 * W * HGRN_DIM // HGRN_COLS)), lb_logits, gain3, *shards)


def _hgrn_bwd(proj, lb_logits, gain3, o, d_yh, states, nseq, S, chip_sums):
    T = proj.shape[0]
    W = HGRN_PAR
    n = len(chip_sums)
    nh = HGRN_HEADS // W
    assert nh == 1
    nc = S // HGRN_CHUNK
    cg = min(8, nc)
    ng = nc // cg
    rows = cg * HGRN_CHUNK
    C = HGRN_CHUNK
    nsub = C // HGRN_SUB

    def body(*all_refs):
        (hq_of, hf_of, hi_of, hg_of), lbl_ref, gain_ref, refs = _hgrn_operands(all_refs)
        o_ref, dyh_ref, st_ref = refs[:3]
        sums_in = refs[3:3 + n]
        dh4_ref, dgain_ref, dlbl_ref = refs[3 + n:6 + n]
        sums_out = refs[6 + n:6 + 2 * n]
        dstate, dlb_acc, send_sems, recv_sems, local_sems = refs[6 + 2 * n:11 + 2 * n]
        h, b, g = pl.program_id(0), pl.program_id(1), pl.program_id(2)
        exchange = _ChipExchange(sums_in, sums_out, send_sems, recv_sems, local_sems, refs[11 + 2 * n:])

        @pl.when((h == 0) & (b == 0) & (g == 0))
        def _():
            exchange.start()

        @pl.when(g == 0)
        def _():
            dstate[...] = jnp.zeros(dstate.shape, F32)

        @pl.when((b == 0) & (g == 0))
        def _():
            dgain_ref[...] = jnp.zeros(dgain_ref.shape, F32)
            dlb_acc[...] = jnp.zeros(dlb_acc.shape, F32)

        lb_all = _sigmoid(lbl_ref[0:1, :] - lbl_ref[1:2, :])

        anti = (lax.broadcasted_iota(jnp.int32, (C, C), 1) >= lax.broadcasted_iota(jnp.int32, (C, C), 0)).astype(BF16)
        last_row = lax.broadcasted_iota(jnp.int32, (C, 1), 0) == C - 1

        def head_load(w, c):
            rs = pl.ds(pl.multiple_of(c * C, C), C)
            ls = slice(HGRN_DIM * w, HGRN_DIM * (w + 1))
            return dict(hq=hq_of(rs, w), hf=hf_of(rs, w), hi=hi_of(rs, w), hg=hg_of(rs, w), lb=lb_all[:, ls],
                        gain=gain_ref[w], ov=o_ref[rs, ls], dyh=dyh_ref[rs, ls].astype(F32),
                        st=st_ref[w, c], dst=dstate[w])

        def head_math(v):
            lb, gain, hq, hg = v["lb"], v["gain"], v["hq"], v["hg"]
            ch = yield from _hgrn_chunk(hq, v["hf"], v["hi"], lb)
            ov, dyh = v["ov"], v["dyh"]
            r = lax.rsqrt(jnp.mean(ov * ov, axis=-1, keepdims=True) + NORM_EPS)
            on = ov * r
            sg = _sigmoid(hg)
            doh = dyh * (hg * sg)
            out = dict(dhg=(dyh * on * gain * (sg * (1.0 + hg * (1.0 - sg)))).astype(BF16),
                       dgain=jnp.sum(doh * on, axis=0, keepdims=True))
            don = doh * gain
            do = r * (don - on * jnp.mean(don * on, axis=-1, keepdims=True))
            dob = do.astype(BF16)
            st, dst = v["st"], v["dst"]
            stb, dstb = st.astype(BF16), dst.astype(BF16)
            vb = ch["v"].astype(BF16)
            qhatb = ch["qhat"].astype(BF16)
            eglast = jnp.exp(ch["glast"])
            yield
            dqhat = _dot(dob, stb)
            dkdec = _dot(vb, dstb)
            dv = _dot(ch["kdec"].astype(BF16), dstb, NT)
            deg = jnp.sum(dst * st, axis=0, keepdims=True)
            out["dstate"] = dst * eglast + _dot(dob, qhatb, TN)
            dA = jnp.where(ch["causal"], _dot(dob, vb, NT), 0.0)
            dv = dv + _dot(ch["A"].astype(BF16), dob, TN)
            dAb = dA.astype(BF16)
            yield
            dqs_parts, dgq_parts = [], []
            dk_intra, dgk = None, None
            for i in range(nsub):
                dA_i = dAb[HGRN_SUB * i:HGRN_SUB * (i + 1)]
                dqt = _dot(dA_i, ch["kt"][i])
                dkt = _dot(dA_i, ch["qt"][i], TN)
                dqs_parts.append(dqt * ch["eq"][i])
                dgq_parts.append(dqt * ch["qt"][i].astype(F32))
                dk_i = dkt * ch["ek"][i]
                dgk_i = dkt * ch["kt"][i].astype(F32)
                dk_intra = dk_i if dk_intra is None else dk_intra + dk_i
                dgk = dgk_i if dgk is None else dgk + dgk_i
            dqs_inter = dqhat * ch["eG"]
            dk_state = dkdec * ch["edec"]
            dqs = jnp.concatenate(dqs_parts, axis=0) + dqs_inter
            dk = dk_intra + dk_state
            dG = jnp.concatenate(dgq_parts, axis=0) - dgk + ch["qs"] * dqs_inter - ch["k"] * dk_state
            tail = jnp.sum(dkdec * ch["kdec"], axis=0, keepdims=True) + deg * eglast
            dG = dG + jnp.where(last_row, tail, 0.0)
            yield
            dlf = _tri_sum(anti, dG)
            df = dlf / ch["f"] - dk
            sf, sq = ch["sf"], ch["sq"]
            out["dhf"] = (df * (1.0 - lb) * sf * (1.0 - sf)).astype(BF16)
            out["dlb"] = jnp.sum(df * (1.0 - sf), axis=0, keepdims=True)
            out["dhq"] = (dqs * (sq * (1.0 + hq * (1.0 - sq)))).astype(BF16)
            out["dhi"] = dv.astype(BF16)
            return out

        def head_store(w, c, out):
            rs = pl.ds(pl.multiple_of(c * C, C), C)
            ls = slice(HGRN_DIM * w, HGRN_DIM * (w + 1))
            for a, key in enumerate(("dhq", "dhf", "dhi", "dhg")):
                lo = HGRN_HEADS * HGRN_DIM * a + HGRN_DIM * w
                dh4_ref[rs, lo:lo + HGRN_DIM] = out[key]
            dstate[w] = out["dstate"]
            dgain_ref[w] += out["dgain"]
            dlb_acc[:, ls] += out["dlb"]

        def chunk(cc, carry):
            c = cg - 1 - cc
            outs = _interleave([head_math(v) for v in [head_load(w, c) for w in range(W)]])
            for w in range(W):
                head_store(w, c, outs[w])
            return carry

        lax.fori_loop(0, cg, chunk, 0)

        @pl.when((b == nseq - 1) & (g == ng - 1))
        def _():
            dl0 = dlb_acc[...] * lb_all * (1.0 - lb_all)
            dlbl_ref[0:1, :] = dl0
            dlbl_ref[1:2, :] = -dl0

        @pl.when((h == nh - 1) & (b == nseq - 1) & (g == ng - 1))
        def _():
            exchange.wait()

    gi, in_specs = _hgrn_specs(nseq, ng, rows, True)
    row = lambda: pl.BlockSpec((rows, W * HGRN_DIM), lambda h, b, g: (b * ng + gi(g), h))
    anyspec = lambda: pl.BlockSpec(memory_space=pl.ANY)
    return pl.pallas_call(
        body, name="hgrn_bwd",
        grid=(nh, nseq, ng),
        in_specs=in_specs + [row(), row(),
                             pl.BlockSpec((None, W, cg, HGRN_DIM, HGRN_DIM), lambda h, b, g: (b, h, gi(g), 0, 0))]
                 + [anyspec() for _ in chip_sums],
        out_specs=[pl.BlockSpec((rows, 4 * W * HGRN_DIM), lambda h, b, g: (b * ng + gi(g), h)),
                   pl.BlockSpec((W, 1, HGRN_DIM), lambda h, b, g: (h, 0, 0)),
                   pl.BlockSpec((2, W * HGRN_DIM), lambda h, b, g: (0, h))] + [anyspec() for _ in chip_sums],
        out_shape=[jax.ShapeDtypeStruct((T, 4 * HGRN_HEADS * HGRN_DIM), BF16),
                   jax.ShapeDtypeStruct((HGRN_HEADS, 1, HGRN_DIM), F32),
                   jax.ShapeDtypeStruct((2, HGRN_HEADS * HGRN_DIM), F32)]
                  + [jax.ShapeDtypeStruct(s.shape, s.dtype) for s in chip_sums],
        scratch_shapes=[pltpu.VMEM((W, HGRN_DIM, HGRN_DIM), F32), pltpu.VMEM((1, W * HGRN_DIM), F32)]
                       + _ChipExchange.scratch(chip_sums),
        compiler_params=_params(("arbitrary", "arbitrary", "arbitrary")),
    )(*([proj] * (4 * W * HGRN_DIM // HGRN_COLS)), lb_logits, gain3, o, d_yh, states, *chip_sums)


def _gate_specs(tm):
    spec = lambda c0, q: pl.BlockSpec((tm, 512), lambda i: (i, c0 // 512 + q))
    return [spec(C_GA, q) for q in range(4)] + [spec(C_GH, q) for q in range(4)]


def _gates(refs):
    ga = jnp.concatenate([r[...] for r in refs[0:4]], axis=1).astype(F32)
    gh = jnp.concatenate([r[...] for r in refs[4:8]], axis=1).astype(F32)
    return ga, gh


def _branch_merge(proj, ya, yh, wa_t, wh_t):
    T = proj.shape[0]
    tm = min(512, T)

    def body(*refs):
        ya_ref, yh_ref, wa_ref, wh_ref, merged_ref, ua_ref, uh_ref = refs[8:]
        ga, gh = _gates(refs)
        ua = _dot(ya_ref[...], wa_ref[...], NT)
        uh = _dot(yh_ref[...], wh_ref[...], NT)
        merged_ref[...] = (_sigmoid(ga) * ua + _sigmoid(gh) * uh).astype(BF16)
        ua_ref[...] = ua.astype(BF16)
        uh_ref[...] = uh.astype(BF16)

    rowb = lambda w: pl.BlockSpec((tm, w), lambda i: (i, 0))
    full = lambda a: pl.BlockSpec(a.shape, lambda i: (0, 0))
    return pl.pallas_call(
        body, name="branch_merge",
        grid=(T // tm,),
        in_specs=_gate_specs(tm) + [rowb(1024), rowb(1024), full(wa_t), full(wh_t)],
        out_specs=[rowb(D_MODEL)] * 3,
        out_shape=[jax.ShapeDtypeStruct((T, D_MODEL), BF16)] * 3,
        compiler_params=_params(("arbitrary",)),
    )(*([proj] * 8), ya, yh, wa_t, wh_t)


def _out_norm_loss(merged, x2, tgt2, gpost, wout):
    T = merged.shape[0]
    tm = min(256, T)

    def body(m_ref, x_ref, t_ref, gpost_ref, wo_ref, dy_ref, dm_ref, dout_ref, loss_ref, dgpost_ref):
        @pl.when(pl.program_id(0) == 0)
        def _():
            loss_ref[...] = jnp.zeros(loss_ref.shape, F32)
            dgpost_ref[...] = jnp.zeros(dgpost_ref.shape, F32)

        y = _dot(m_ref[...], wo_ref[...])
        r2 = lax.rsqrt(jnp.mean(y * y, axis=-1, keepdims=True) + NORM_EPS)
        yn = y * r2
        gpost = gpost_ref[...]
        err = x_ref[...] + yn * gpost - t_ref[...]
        loss_ref[...] += jnp.sum(err * err, axis=0, keepdims=True)
        dout = err * (1.0 / D_MODEL)
        dout_ref[...] = dout
        dgpost_ref[...] += jnp.sum(dout * yn, axis=0, keepdims=True)
        dyn = dout * gpost
        dy = (r2 * (dyn - yn * jnp.mean(dyn * yn, axis=-1, keepdims=True))).astype(BF16)
        dy_ref[...] = dy
        dm_ref[...] = _dot(dy, wo_ref[...], NT).astype(BF16)

    rowb = lambda: pl.BlockSpec((tm, D_MODEL), lambda i: (i, 0))
    vec = lambda: pl.BlockSpec((1, D_MODEL), lambda i: (0, 0))
    return pl.pallas_call(
        body, name="out_norm_loss",
        grid=(T // tm,),
        in_specs=[rowb(), rowb(), rowb(), vec(), pl.BlockSpec(wout.shape, lambda i: (0, 0))],
        out_specs=[rowb(), rowb(), rowb(), vec(), vec()],
        out_shape=[jax.ShapeDtypeStruct((T, D_MODEL), BF16)] * 2
                  + [jax.ShapeDtypeStruct((T, D_MODEL), F32)] + [jax.ShapeDtypeStruct((1, D_MODEL), F32)] * 2,
        compiler_params=_params(("arbitrary",)),
    )(merged, x2, tgt2, gpost, wout)


def _branch_bwd(proj, dm, ua, uh, wa_t, wh_t):
    T = proj.shape[0]
    tm = min(256, T)

    def body(*refs):
        (dm_ref, ua_ref, uh_ref, wa_ref, wh_ref,
         dua_ref, duh_ref, dgg_ref, dya_ref, dyh_ref) = refs[8:]
        ga, gh = _gates(refs)
        sa, sh = _sigmoid(ga), _sigmoid(gh)
        dm = dm_ref[...].astype(F32)
        dua = (dm * sa).astype(BF16)
        duh = (dm * sh).astype(BF16)
        dua_ref[...] = dua
        duh_ref[...] = duh
        dgg_ref[:, :D_MODEL] = (dm * ua_ref[...].astype(F32) * (sa * (1.0 - sa))).astype(BF16)
        dgg_ref[:, D_MODEL:] = (dm * uh_ref[...].astype(F32) * (sh * (1.0 - sh))).astype(BF16)
        dya_ref[...] = _dot(dua, wa_ref[...]).astype(BF16)
        dyh_ref[...] = _dot(duh, wh_ref[...]).astype(BF16)

    rowb = lambda w: pl.BlockSpec((tm, w), lambda i: (i, 0))
    full = lambda a: pl.BlockSpec(a.shape, lambda i: (0, 0))
    return pl.pallas_call(
        body, name="branch_bwd",
        grid=(T // tm,),
        in_specs=_gate_specs(tm) + [rowb(D_MODEL)] * 3 + [full(wa_t), full(wh_t)],
        out_specs=[rowb(D_MODEL)] * 2 + [rowb(2 * D_MODEL)] + [rowb(1024)] * 2,
        out_shape=[jax.ShapeDtypeStruct((T, D_MODEL), BF16)] * 2 + [jax.ShapeDtypeStruct((T, 2 * D_MODEL), BF16)]
                  + [jax.ShapeDtypeStruct((T, 1024), BF16)] * 2,
        compiler_params=_params(("arbitrary",)),
    )(*([proj] * 8), dm, ua, uh, wa_t, wh_t)


def _tn_matmul(L, R, bm, bn, name):
    T, M = L.shape
    N = R.shape[1]

    def body(l_ref, r_ref, out_ref):
        out_ref[...] = _dot(l_ref[...], r_ref[...], TN).astype(BF16)

    return pl.pallas_call(
        body, name=name,
        grid=(N // bn, M // bm),
        in_specs=[pl.BlockSpec((T, bm), lambda j, i: (0, i)),
                  pl.BlockSpec((T, bn), lambda j, i: (0, j))],
        out_specs=pl.BlockSpec((bm, bn), lambda j, i: (i, j)),
        out_shape=jax.ShapeDtypeStruct((M, N), BF16),
        compiler_params=_params(("arbitrary", "arbitrary")),
    )(L, R)


def _tn_matmul_pieces(pieces, R, bm, bn, name):
    T, N = R.shape
    M = sum(p.shape[1] for p in pieces)
    out, row0 = None, 0
    for q, L in enumerate(pieces):
        off = row0 // bm

        def body(*refs):
            refs[-1][...] = _dot(refs[0][...], refs[1][...], TN).astype(BF16)

        prev = [] if out is None else [out]
        out = pl.pallas_call(
            body, name="%s_%d" % (name, q),
            grid=(N // bn, L.shape[1] // bm),
            in_specs=[pl.BlockSpec((T, bm), lambda j, i: (0, i)), pl.BlockSpec((T, bn), lambda j, i: (0, j))]
                     + [pl.BlockSpec(memory_space=pl.ANY) for _ in prev],
            out_specs=pl.BlockSpec((bm, bn), lambda j, i, off=off: (off + i, j)),
            out_shape=jax.ShapeDtypeStruct((M, N), BF16),
            input_output_aliases={2: 0} if prev else {},
            compiler_params=_params(("arbitrary", "arbitrary")),
        )(L, R, *prev)
        row0 += L.shape[1]
    return out


def _dh_prenorm_bwd(pieces, wt_in, x2, dout, gpre, chip_sums):
    T = x2.shape[0]
    tm = min(1024, T)
    ne = 4
    te = tm // ne
    tk = 512
    nk = IN_WIDTH // tk
    nt = T // tm
    n = len(chip_sums)
    npiece = len(pieces)
    counts = [p.shape[1] // tk for p in pieces]
    starts = [sum(counts[:q]) for q in range(npiece)]
    assert sum(counts) == nk and all(p.shape[1] % tk == 0 for p in pieces)

    def body(*all_refs):
        piece_refs = all_refs[:npiece]
        w_ref, x_ref, dout_ref, g_ref = all_refs[npiece:npiece + 4]
        refs = all_refs[npiece + 4:]
        ins = refs[:n]
        gx_ref, dg_ref = refs[n], refs[n + 1]
        outs = refs[n + 2:2 * n + 2]
        acc, send_sems, recv_sems, local_sems = refs[2 * n + 2:2 * n + 6]
        i, k = pl.program_id(0), pl.program_id(1)
        exchange = _ChipExchange(ins, outs, send_sems, recv_sems, local_sems, refs[2 * n + 6:])

        @pl.when((i == 0) & (k == 0))
        def _():
            dg_ref[...] = jnp.zeros(dg_ref.shape, F32)
            exchange.start()

        @pl.when((i == nt - 1) & (k == nk + ne - 1))
        def _():
            exchange.wait()

        @pl.when(k == 0)
        def _():
            acc[...] = jnp.zeros(acc.shape, F32)

        for q in range(npiece):
            @pl.when((k >= starts[q]) & (k < starts[q] + counts[q]))
            def _(q=q):
                acc[...] += _dot(piece_refs[q][...], w_ref[...])

        @pl.when(k >= nk)
        def _():
            dh = acc[pl.ds(pl.multiple_of((k - nk) * te, te), te), :]
            x = x_ref[...]
            r = lax.rsqrt(jnp.mean(x * x, axis=-1, keepdims=True) + NORM_EPS)
            xn = x * r
            dg_ref[...] += jnp.sum(dh * xn, axis=0, keepdims=True)
            dxn = dh * g_ref[...]
            gx_ref[...] = dout_ref[...] + r * (dxn - xn * jnp.mean(dxn * xn, axis=-1, keepdims=True))

    piece_spec = lambda q: pl.BlockSpec((tm, tk), lambda i, k: (i, jnp.clip(k - starts[q], 0, counts[q] - 1)))
    rowb = lambda: pl.BlockSpec((te, D_MODEL), lambda i, k: (ne * i + jnp.clip(k - nk, 0, ne - 1), 0))
    vec = lambda: pl.BlockSpec((1, D_MODEL), lambda i, k: (0, 0))
    anyspec = lambda: pl.BlockSpec(memory_space=pl.ANY)
    return pl.pallas_call(
        body, name="dh_prenorm_bwd",
        grid=(nt, nk + ne),
        in_specs=[piece_spec(q) for q in range(npiece)]
                 + [pl.BlockSpec((tk, D_MODEL), lambda i, k: (jnp.minimum(k, nk - 1), 0)),
                    rowb(), rowb(), vec()] + [anyspec() for _ in chip_sums],
        out_specs=[rowb(), vec()] + [anyspec() for _ in chip_sums],
        out_shape=[jax.ShapeDtypeStruct((T, D_MODEL), F32), jax.ShapeDtypeStruct((1, D_MODEL), F32)]
                  + [jax.ShapeDtypeStruct(s.shape, s.dtype) for s in chip_sums],
        scratch_shapes=[pltpu.VMEM((tm, D_MODEL), F32)] + _ChipExchange.scratch(chip_sums),
        compiler_params=_params(("arbitrary", "arbitrary")),
    )(*pieces, wt_in, x2, dout, gpre, *chip_sums)


def _sum_slots(recv, br, name):
    nslot, R, C = recv.shape

    def body(r_ref, out_ref):
        acc = r_ref[0].astype(F32)
        for s in range(1, nslot):
            acc = acc + r_ref[s].astype(F32)
        out_ref[...] = acc

    return pl.pallas_call(
        body, name=name,
        grid=(R // br,),
        in_specs=[pl.BlockSpec((nslot, br, C), lambda i: (0, i, 0))],
        out_specs=pl.BlockSpec((br, C), lambda i: (i, 0)),
        out_shape=jax.ShapeDtypeStruct((R, C), F32),
        compiler_params=_params(("arbitrary",)),
    )(recv)


def _adamw_math(w, g, m, v):
    m = ADAM_B1 * m + (1.0 - ADAM_B1) * g
    v = ADAM_B2 * v + (1.0 - ADAM_B2) * (g * g)
    m_hat = m / (1.0 - ADAM_B1 ** ADAM_STEP)
    v_hat = v / (1.0 - ADAM_B2 ** ADAM_STEP)
    delta = -ADAM_LR * (m_hat / (jnp.sqrt(v_hat) + ADAM_EPS) + ADAM_WD * w)
    return delta, m, v


def _adamw(w, g, m, v, br, name):
    R, C = g.shape
    lead = (None,) * (w.ndim - 2)

    def body(w_ref, g_ref, m_ref, v_ref, d_ref, nm_ref, nv_ref):
        d_ref[...], nm_ref[...], nv_ref[...] = _adamw_math(w_ref[...], g_ref[...], m_ref[...], v_ref[...])

    spec = lambda: pl.BlockSpec(lead + (br, C), lambda i: (0,) * len(lead) + (i, 0))
    return pl.pallas_call(
        body, name=name,
        grid=(R // br,),
        in_specs=[spec(), pl.BlockSpec((br, C), lambda i: (i, 0)), spec(), spec()],
        out_specs=[spec(), spec(), spec()],
        out_shape=[jax.ShapeDtypeStruct(w.shape, F32)] * 3,
        compiler_params=_params(("arbitrary",)),
    )(w, g, m, v)


def _sibling_exchange(partials, name):
    n = len(partials)

    def body(*refs):
        ins, outs = refs[:n], refs[n:2 * n]
        send_sems, recv_sems = refs[2 * n:]
        x, y, c = _place()

        def copy(a, p):
            return pltpu.make_async_remote_copy(
                src_ref=ins[a].at[p, 1 - c], dst_ref=outs[a].at[p],
                send_sem=send_sems.at[a, p], recv_sem=recv_sems.at[a, p],
                device_id=(x, y, 1 - c), device_id_type=pl.DeviceIdType.MESH)

        copies = [copy(a, p) for p in range(4) for a in range(n)]
        for cp in copies:
            cp.start()
        for cp in copies:
            cp.wait()

    anyspec = lambda: pl.BlockSpec(memory_space=pl.ANY)
    return pl.pallas_call(
        body, name=name,
        in_specs=[anyspec() for _ in partials],
        out_specs=[anyspec() for _ in partials],
        out_shape=[jax.ShapeDtypeStruct((4,) + p.shape[2:], p.dtype) for p in partials],
        scratch_shapes=[pltpu.SemaphoreType.DMA((n, 4)), pltpu.SemaphoreType.DMA((n, 4))],
    )(*partials)


def _chip_sum(partial, from_sibling, br, name):
    _, _, R, C = partial.shape
    cls = lax.axis_index("c").astype(jnp.int32).reshape(1)

    def body(c_ref, mine_ref, sib_ref, out_ref):
        out_ref[...] = (mine_ref[...].astype(F32) + sib_ref[...].astype(F32)).astype(BF16)

    grid_spec = pltpu.PrefetchScalarGridSpec(
        num_scalar_prefetch=1,
        grid=(4, R // br),
        in_specs=[pl.BlockSpec((None, None, br, C), lambda p, i, c: (p, c[0], i, 0)),
                  pl.BlockSpec((None, br, C), lambda p, i, c: (p, i, 0))],
        out_specs=pl.BlockSpec((None, br, C), lambda p, i, c: (p, i, 0)),
    )
    return pl.pallas_call(
        body, name=name, grid_spec=grid_spec,
        out_shape=jax.ShapeDtypeStruct((4, R, C), BF16),
        compiler_params=_params(("arbitrary", "arbitrary")),
    )(cls, partial, from_sibling)


def _all_reduce_small(packed):
    shape = packed.shape

    def body(in_ref, out_ref, slots, send_sems, recv_sems):
        x, y, c = _place()
        my_slot = 4 * x + 2 * y + c

        def peer(k):
            return (x ^ ((k >> 2) & 1), y ^ ((k >> 1) & 1), c ^ (k & 1))

        def copy(k):
            p = peer(k)
            return pltpu.make_async_remote_copy(
                src_ref=in_ref, dst_ref=slots.at[my_slot],
                send_sem=send_sems.at[k - 1], recv_sem=recv_sems.at[k - 1],
                device_id=p, device_id_type=pl.DeviceIdType.MESH)

        def arrival(k):
            p = peer(k)
            return pltpu.make_async_remote_copy(
                src_ref=in_ref, dst_ref=slots.at[4 * p[0] + 2 * p[1] + p[2]],
                send_sem=send_sems.at[k - 1], recv_sem=recv_sems.at[k - 1],
                device_id=p, device_id_type=pl.DeviceIdType.MESH)

        sends = [copy(k) for k in range(1, N_DEV)]
        for cp in sends:
            cp.start()
        slots[my_slot] = in_ref[...]
        for k in range(1, N_DEV):
            arrival(k).wait_recv()
        for cp in sends:
            cp.wait_send()
        acc = slots[0]
        for s in range(1, N_DEV):
            acc = acc + slots[s]
        out_ref[...] = acc

    return pl.pallas_call(
        body, name="all_reduce_small",
        in_specs=[pl.BlockSpec(memory_space=pltpu.VMEM)],
        out_specs=pl.BlockSpec(memory_space=pltpu.VMEM),
        out_shape=jax.ShapeDtypeStruct(shape, F32),
        scratch_shapes=[pltpu.VMEM((N_DEV,) + shape, F32),
                        pltpu.SemaphoreType.DMA((7,)), pltpu.SemaphoreType.DMA((7,))],
    )(packed)


def _pack_small(norm_pre, norm_post, lb_logits, hgrn_norm, rel_bias, sinks, extra=None):
    tail = [hgrn_norm.reshape(1, 1024), rel_bias.reshape(1, 512), sinks.reshape(1, 16)]
    used = 1024 + 512 + 16
    if extra is not None:
        tail.append(extra.reshape(1, 1))
        used += 1
    tail.append(jnp.zeros((1, D_MODEL - used), F32))
    rows = [norm_pre.reshape(1, D_MODEL), norm_post.reshape(1, D_MODEL), lb_logits.reshape(1, D_MODEL),
            jnp.concatenate(tail, axis=1), jnp.zeros((4, D_MODEL), F32)]
    return jnp.concatenate(rows, axis=0)


def _unpack_small(p):
    return (p[0:1], p[3, 1024:1536].reshape(REL_BUCKETS, ATTN_HEADS), p[3:4, 1536:1552],
            p[2].reshape(2, 1024), p[3, 0:1024].reshape(1, HGRN_HEADS, HGRN_DIM), p[1:2])


def _local_step(nseq, S, x2, tgt2, proj, h, rel_bias, attn_sinks, lb_logits, hgrn_norm, norm_post, shards):
    nb = S // ATTN_BLOCK
    bucket = jnp.asarray(_t5_bucket_table())
    gain3 = hgrn_norm.reshape(HGRN_HEADS, 1, HGRN_DIM)

    bias = _bias_table(rel_bias, bucket)
    ya = _attn_fwd(proj, bias, attn_sinks, nseq, nb)
    o, yh, states, wout, wa_t, wh_t = _hgrn_fwd(proj, lb_logits, gain3, nseq, S, shards)
    merged, ua, uh = _branch_merge(proj, ya, yh, wa_t, wh_t)
    dy, dm, dout, loss_cols, d_gpost = _out_norm_loss(merged, x2, tgt2, norm_post, wout)
    dua, duh, dgg, dya, dyh = _branch_bwd(proj, dm, ua, uh, wa_t, wh_t)

    p_out = _tn_matmul(merged, dy, 256, 1024, "dw_out")
    p_a = _tn_matmul(dua, ya, 256, 1024, "dw_branch_attn")
    p_h = _tn_matmul(duh, yh, 256, 1024, "dw_branch_hgrn")
    small_sums = _chip_sums([p_out, p_a, p_h], (128, 128, 128), ("dw_out", "dw_branch_attn", "dw_branch_hgrn"),
                            "sibling_exchange_small")

    dq, dkv, dg, dbias, d_sinks = _attn_bwd(proj, bias, attn_sinks, dya, nseq, nb)
    d_rel_bias = _bias_table_bwd(dbias, bucket)
    dh4, d_gain, d_lbl, r_out, r_a, r_h = _hgrn_bwd(proj, lb_logits, gain3, o, dyh, states, nseq, S, small_sums)
    dproj = [dq, dkv, dg, dh4, dgg]
    p_in = _tn_matmul_pieces(dproj, h, 512, 1024, "dw_in")
    return dproj, dout, p_in, r_out, r_a, r_h, d_gpost, d_lbl, d_gain, d_rel_bias, d_sinks, loss_cols


def _chip_sums(partials, block_rows, names, exchange_name):
    split = [p.reshape(4, 2, p.shape[0] // N_DEV, p.shape[1]) for p in partials]
    from_sibling = _sibling_exchange(split, exchange_name)
    return [_chip_sum(p, f, br, "chip_sum_" + nm) for p, f, br, nm in zip(split, from_sibling, block_rows, names)]


def kernel(x, norm_pre, w_in, rel_bias, attn_sinks, lb_logits, hgrn_norm, w_branch_attn, w_branch_hgrn, w_out, norm_post, loss_target, m_norm_pre, m_w_in, m_rel_bias, m_attn_sinks, m_lb_logits, m_hgrn_norm, m_w_branch_attn, m_w_branch_hgrn, m_w_out, m_norm_post, v_norm_pre, v_w_in, v_rel_bias, v_attn_sinks, v_lb_logits, v_hgrn_norm, v_w_branch_attn, v_w_branch_hgrn, v_w_out, v_norm_post):
    nseq, S, _ = x.shape
    T = nseq * S
    x2 = x.reshape(T, D_MODEL)
    tgt2 = loss_target.reshape(T, D_MODEL)

    h = _prenorm(x2, norm_pre)
    proj, wt_in = _gather_inproj(h, w_in[0].T.astype(BF16))
    shards = [w_out[0].astype(BF16), w_branch_attn[0].T.astype(BF16), w_branch_hgrn[0].T.astype(BF16)]

    (dproj, dout, p_in, r_out, r_a, r_h, d_gpost, d_lbl, d_gain, d_rel_bias, d_sinks, loss_cols) = _local_step(
        nseq, S, x2, tgt2, proj, h, rel_bias, attn_sinks, lb_logits, hgrn_norm, norm_post, shards)

    in_sums = _chip_sums([p_in], (192,), ("dw_in",), "sibling_exchange_w_in")
    grad_x2, d_gpre, r_in = _dh_prenorm_bwd(dproj, wt_in, x2, dout, norm_pre, in_sums)
    g_in_t = _sum_slots(r_in, 192, "sum_dw_in")
    g_out = _sum_slots(r_out, 128, "sum_dw_out")
    g_a = _sum_slots(r_a, 128, "sum_dw_branch_attn").T
    g_h = _sum_slots(r_h, 128, "sum_dw_branch_hgrn").T

    loss_part = 0.5 / D_MODEL * jnp.sum(loss_cols)
    packed = _pack_small(d_gpre, d_gpost, d_lbl, d_gain, d_rel_bias, d_sinks, extra=loss_part)
    total = _all_reduce_small(packed)
    loss = total[3, 1024 + 512 + 16]
    sm_w = _pack_small(norm_pre, norm_post, lb_logits, hgrn_norm, rel_bias, attn_sinks)
    sm_m = _pack_small(m_norm_pre, m_norm_post, m_lb_logits, m_hgrn_norm, m_rel_bias, m_attn_sinks)
    sm_v = _pack_small(v_norm_pre, v_norm_post, v_lb_logits, v_hgrn_norm, v_rel_bias, v_attn_sinks)
    sm_d, sm_nm, sm_nv = _adamw(sm_w, total, sm_m, sm_v, 8, "adamw_small")

    t = lambda a: jnp.swapaxes(a, 1, 2)
    d_in, nm_in, nv_in = map(t, _adamw(t(w_in), g_in_t, t(m_w_in), t(v_w_in), 192, "adamw_w_in"))
    d_out, nm_out, nv_out = _adamw(w_out, g_out, m_w_out, v_w_out, 128, "adamw_w_out")
    d_a, nm_a, nv_a = _adamw(w_branch_attn, g_a, m_w_branch_attn, v_w_branch_attn, 256, "adamw_w_branch_attn")
    d_h, nm_h, nv_h = _adamw(w_branch_hgrn, g_h, m_w_branch_hgrn, v_w_branch_hgrn, 256, "adamw_w_branch_hgrn")

    def group(small, big_in, big_a, big_h, big_out):
        npre, rb, sk, lbl, hn, npost = _unpack_small(small)
        return (npre, big_in, rb, sk, lbl, hn, big_a, big_h, big_out, npost)

    return (loss, grad_x2.reshape(nseq, S, D_MODEL),
            *group(total, t(g_in_t[None]), g_a[None], g_h[None], g_out[None]),
            *group(sm_d, d_in, d_a, d_h, d_out),
            *group(sm_nm, nm_in, nm_a, nm_h, nm_out),
            *group(sm_nv, nv_in, nv_a, nv_h, nv_out))
```

```python
import functools
import math

import numpy as np
import jax
import jax.numpy as jnp
from jax import lax
from jax.experimental import pallas as pl
from jax.experimental.pallas import tpu as pltpu

F32 = jnp.float32
BF16 = jnp.bfloat16

D_MODEL = 2048
ATTN_HEADS = 16
ATTN_HEAD_DIM = 64
ATTN_GROUP = 4
ATTN_BLOCK = 128
HGRN_HEADS = 8
HGRN_DIM = 128
HGRN_CHUNK = 64
HGRN_SUB = 16
HGRN_PAR = 8
HGRN_COLS = 512
REL_BUCKETS = 32
REL_MAX_DIST = 128
NORM_EPS = 1e-6
C_AQ, C_AK, C_AV, C_AG = 0, 1024, 1280, 1536
C_HQ, C_HF, C_HI, C_HG = 2560, 3584, 4608, 5632
C_GA, C_GH = 6656, 8704
IN_WIDTH = 10752
N_DEV = 8
assert all(c0 % HGRN_COLS == 0 for c0 in (C_HQ, C_HF, C_HI, C_HG)) and (HGRN_PAR * HGRN_DIM) % HGRN_COLS == 0

ADAM_LR = 0.001
ADAM_B1 = 0.9
ADAM_B2 = 0.999
ADAM_EPS = 1e-08
ADAM_WD = 0.01
ADAM_STEP = 10

VMEM_LIMIT_V7X = 56 * 1024 * 1024
NEG_BIG = -1e30

NT = (((1,), (1,)), ((), ()))
TN = (((0,), (0,)), ((), ()))
NN = (((1,), (0,)), ((), ()))


def _dot(a, b, dims=NN):
    return lax.dot_general(a, b, dims, preferred_element_type=F32)


def _params(sem=None):
    return pltpu.CompilerParams(dimension_semantics=sem, vmem_limit_bytes=VMEM_LIMIT_V7X)


def _sigmoid(x):
    return 1.0 / (1.0 + jnp.exp(-x))


def _t5_bucket_table():
    qi = np.arange(ATTN_BLOCK)[:, None]
    si = np.arange(2 * ATTN_BLOCK)[None, :]
    dist = qi + ATTN_BLOCK - si
    max_exact = REL_BUCKETS // 2
    d = np.maximum(dist, 0)
    df = np.maximum(d, 1).astype(np.float32)
    large = max_exact + (np.log(df / np.float32(max_exact)).astype(np.float32)
                         / np.float32(math.log(REL_MAX_DIST / max_exact))
                         * np.float32(REL_BUCKETS - max_exact)).astype(np.int32)
    large = np.minimum(large, REL_BUCKETS - 1)
    return np.where(d < max_exact, d, large).astype(np.int32)


def _place():
    return lax.axis_index("x"), lax.axis_index("y"), lax.axis_index("c")


class _GatherOps:
    def __init__(self, ins, outs, send_sems, recv_sems, local_sems):
        self.ins, self.outs = ins, outs
        self.send_sems, self.recv_sems, self.local_sems = send_sems, recv_sems, local_sems
        x, y, c = _place()
        self.c = c
        self.me, self.sibling = (x, y, c), (x, y, 1 - c)
        self.chips = [(1 - x, y), (x, 1 - y), (1 - x, 1 - y)]

    def _rows(self, a, dev):
        r = self.ins[a].shape[0]
        return self.outs[a].at[pl.ds((4 * dev[0] + 2 * dev[1] + dev[2]) * r, r), :]

    def _copy(self, a, k, block, to, src=None):
        return pltpu.make_async_remote_copy(
            src_ref=self._rows(a, block) if src is None else src, dst_ref=self._rows(a, block),
            send_sem=self.send_sems.at[a, k], recv_sem=self.recv_sems.at[a, k],
            device_id=to, device_id_type=pl.DeviceIdType.MESH)

    def local(self, a):
        return pltpu.make_async_copy(self.ins[a], self._rows(a, self.me), self.local_sems.at[a])

    def to_sibling(self, a):
        return self._copy(a, 0, self.me, self.sibling, src=self.ins[a])

    def to_chip(self, a, q):
        return self._copy(a, 1 + q, self.me, (*self.chips[q], self.c), src=self.ins[a])

    def forward(self, a, q):
        return self._copy(a, 4 + q, (*self.chips[q], self.c), self.sibling)

    def from_sibling(self, a):
        return self._copy(a, 0, self.sibling, self.me)

    def from_chip(self, a, q):
        return self._copy(a, 1 + q, (*self.chips[q], self.c), self.me)

    def forwarded(self, a, q):
        return self._copy(a, 4 + q, (*self.chips[q], 1 - self.c), self.me)

    def sends(self, a):
        return [self.to_sibling(a)] + [self.to_chip(a, q) for q in range(3)] + [self.forward(a, q) for q in range(3)]

    def start(self):
        for a in range(len(self.ins)):
            self.local(a).start()
            self.to_sibling(a).start()
            for q in range(3):
                self.to_chip(a, q).start()

    def finish(self):
        for a in range(len(self.ins)):
            self.local(a).wait()
            self.from_sibling(a).wait_recv()
            for q in range(3):
                self.from_chip(a, q).wait_recv()
                self.forward(a, q).start()
        for a in range(len(self.ins)):
            for q in range(3):
                self.forwarded(a, q).wait_recv()
            for cp in self.sends(a):
                cp.wait_send()

    @staticmethod
    def scratch(n):
        return [pltpu.SemaphoreType.DMA((n, _GATHER_SEMS)), pltpu.SemaphoreType.DMA((n, _GATHER_SEMS)),
                pltpu.SemaphoreType.DMA((n,))]


class _ChipExchange:
    def __init__(self, ins, outs, send_sems, recv_sems, local_sems):
        self.ins, self.outs = ins, outs
        self.send_sems, self.recv_sems, self.local_sems = send_sems, recv_sems, local_sems
        self.x, self.y, self.c = _place()
        self.my_chip = 2 * self.x + self.y

    def _peer(self, q):
        return (self.x ^ (q >> 1), self.y ^ (q & 1))

    def _copy(self, a, q, src_block, dst_slot):
        px, py = self._peer(q)
        return pltpu.make_async_remote_copy(
            src_ref=self.ins[a].at[src_block], dst_ref=self.outs[a].at[dst_slot],
            send_sem=self.send_sems.at[a, q - 1], recv_sem=self.recv_sems.at[a, q - 1],
            device_id=(px, py, self.c), device_id_type=pl.DeviceIdType.MESH)

    def _send(self, a, q):
        px, py = self._peer(q)
        return self._copy(a, q, 2 * px + py, self.my_chip)

    def _arrival(self, a, q):
        px, py = self._peer(q)
        return self._copy(a, q, self.my_chip, 2 * px + py)

    def _local(self, a):
        return pltpu.make_async_copy(self.ins[a].at[self.my_chip], self.outs[a].at[self.my_chip],
                                     self.local_sems.at[a])

    def start(self):
        for a in range(len(self.ins)):
            self._local(a).start()
            for q in range(1, 4):
                self._send(a, q).start()

    def wait(self):
        for a in range(len(self.ins)):
            for q in range(1, 4):
                self._arrival(a, q).wait_recv()
        for a in range(len(self.ins)):
            for q in range(1, 4):
                self._send(a, q).wait_send()
            self._local(a).wait()

    @staticmethod
    def scratch(n):
        return [pltpu.SemaphoreType.DMA((n, 3)), pltpu.SemaphoreType.DMA((n, 3)), pltpu.SemaphoreType.DMA((n,))]


_GATHER_SEMS = 7
LOCAL_DMA_THREAD = 1

INPROJ_TILE = 896


def _prenorm(x2, gpre):
    T = x2.shape[0]
    tm = min(512, T)

    def body(x_ref, g_ref, h_ref):
        x = x_ref[...]
        r = lax.rsqrt(jnp.mean(x * x, axis=-1, keepdims=True) + NORM_EPS)
        h_ref[...] = (x * r * g_ref[...]).astype(BF16)

    return pl.pallas_call(
        body, name="prenorm",
        grid=(T // tm,),
        in_specs=[pl.BlockSpec((tm, D_MODEL), lambda i: (i, 0)), pl.BlockSpec((1, D_MODEL), lambda i: (0, 0))],
        out_specs=pl.BlockSpec((tm, D_MODEL), lambda i: (i, 0)),
        out_shape=jax.ShapeDtypeStruct((T, D_MODEL), BF16),
        compiler_params=_params(("arbitrary",)),
    )(x2, gpre)


def _gather_inproj(h, wt_shard):
    T = h.shape[0]
    tm = min(1024, T)
    nm = T // tm
    tn = INPROJ_TILE
    ntile = IN_WIDTH // tn
    nstep = ntile * nm

    def body(h_hbm, w_in, proj_hbm, w_out, hbuf, wtile, obuf, own_buf,
             send_sems, recv_sems, local_sems, h_sem, w_sems, o_sems):
        j, i = pl.program_id(0), pl.program_id(1)
        step = j * nm + i
        slot = step % 2
        ops = _GatherOps([w_in], [w_out], send_sems, recv_sems, local_sems)
        x, y, _ = _place()

        def tile_of(jj):
            k = jj // 3
            return 3 * ((2 * x + y) ^ (((k & 1) << 1) | (k >> 1))) + jj % 3

        tile = tile_of(j)

        def h_load():
            return pltpu.make_async_copy(h_hbm, hbuf, h_sem)

        own_in = pltpu.make_async_copy(w_in, own_buf, local_sems.at[0])
        own_out = pltpu.make_async_copy(own_buf, ops._rows(0, ops.me), local_sems.at[1])

        def store(s, rows, cols):
            return pltpu.make_async_copy(obuf.at[s], proj_hbm.at[rows, cols], o_sems.at[s])

        def window(ii, t):
            return pl.ds(pl.multiple_of(ii * tm, tm), tm), pl.ds(pl.multiple_of(t * tn, tn), tn)

        @pl.when(step == 0)
        def _():
            h_load().start()
            own_in.start()
            ops.to_sibling(0).start()
            ops.to_chip(0, 0).start()
            ops.to_chip(0, 1).start()
            h_load().wait()

        for kk in range(4):
            @pl.when((j == 3 * kk) & (i == 0))
            def _(kk=kk):
                if kk == 0:
                    own_in.wait()
                    own_out.start()
                    own_out.wait()
                    ops.from_sibling(0).wait_recv()
                else:
                    q = kk - 1
                    ops.from_chip(0, q).wait_recv()
                    ops.forward(0, q).start()
                    if q == 0:
                        ops.to_chip(0, 2).start()
                    ops.forwarded(0, q).wait_recv()

        wslot = j % 2

        def fetch(jj, sw):
            rows = pl.ds(pl.multiple_of(tile_of(jj) * tn, tn), tn)
            return pltpu.make_async_copy(w_out.at[rows, :], wtile.at[sw], w_sems.at[sw])

        @pl.when((i == 0) & (j % 3 == 0))
        def _():
            fetch(j, wslot).start(LOCAL_DMA_THREAD)

        @pl.when(i == 0)
        def _():
            fetch(j, wslot).wait()

        @pl.when((i == 0) & (j % 3 != 2))
        def _():
            fetch(j + 1, 1 - wslot).start(LOCAL_DMA_THREAD)

        @pl.when(step >= 2)
        def _():
            store(slot, *window(0, 0)).wait()

        hv = hbuf[pl.ds(pl.multiple_of(i * tm, tm), tm), :]
        obuf[slot] = _dot(hv, wtile[wslot], NT).astype(BF16)
        store(slot, *window(i, tile)).start(LOCAL_DMA_THREAD)

        @pl.when(step == nstep - 1)
        def _():
            for s in range(min(2, nstep)):
                store(s, *window(0, 0)).wait()
            for cp in ops.sends(0):
                cp.wait_send()

    anyspec = lambda: pl.BlockSpec(memory_space=pl.ANY)
    return pl.pallas_call(
        body, name="gather_inproj",
        grid=(ntile, nm),
        in_specs=[anyspec(), anyspec()],
        out_specs=[anyspec(), anyspec()],
        out_shape=[jax.ShapeDtypeStruct((T, IN_WIDTH), BF16),
                   jax.ShapeDtypeStruct((N_DEV * wt_shard.shape[0], D_MODEL), BF16)],
        scratch_shapes=[pltpu.VMEM((T, D_MODEL), BF16), pltpu.VMEM((2, tn, D_MODEL), BF16),
                        pltpu.VMEM((2, tm, tn), BF16), pltpu.VMEM(wt_shard.shape, BF16),
                        pltpu.SemaphoreType.DMA((1, _GATHER_SEMS)), pltpu.SemaphoreType.DMA((1, _GATHER_SEMS)),
                        pltpu.SemaphoreType.DMA((2,)), pltpu.SemaphoreType.DMA, pltpu.SemaphoreType.DMA((2,)),
                        pltpu.SemaphoreType.DMA((2,))],
        compiler_params=_params(("arbitrary", "arbitrary")),
    )(h, wt_shard)


def _bias_table(rel_bias, bucket):
    def body(rb_ref, bk_ref, out_ref):
        h = pl.program_id(0)
        bk = bk_ref[...]
        acc = jnp.zeros(bk.shape, F32)
        for b in range(REL_BUCKETS):
            acc = jnp.where(bk == b, rb_ref[b, h], acc)
        out_ref[...] = acc

    return pl.pallas_call(
        body, name="bias_table",
        grid=(ATTN_HEADS,),
        in_specs=[pl.BlockSpec(memory_space=pltpu.SMEM),
                  pl.BlockSpec((ATTN_BLOCK, 2 * ATTN_BLOCK), lambda h: (0, 0))],
        out_specs=pl.BlockSpec((None, ATTN_BLOCK, 2 * ATTN_BLOCK), lambda h: (h, 0, 0)),
        out_shape=jax.ShapeDtypeStruct((ATTN_HEADS, ATTN_BLOCK, 2 * ATTN_BLOCK), F32),
        compiler_params=_params(("arbitrary",)),
    )(rel_bias, bucket)


def _bias_table_bwd(dbias, bucket):
    def body(db_ref, bk_ref, out_ref):
        h = pl.program_id(0)
        bk = bk_ref[...]
        db = db_ref[...]
        for b in range(REL_BUCKETS):
            out_ref[b, h] = jnp.sum(jnp.where(bk == b, db, 0.0))

    return pl.pallas_call(
        body, name="bias_table_bwd",
        grid=(ATTN_HEADS,),
        in_specs=[pl.BlockSpec((None, ATTN_BLOCK, 2 * ATTN_BLOCK), lambda h: (h, 0, 0)),
                  pl.BlockSpec((ATTN_BLOCK, 2 * ATTN_BLOCK), lambda h: (0, 0))],
        out_specs=pl.BlockSpec(memory_space=pltpu.SMEM),
        out_shape=jax.ShapeDtypeStruct((REL_BUCKETS, ATTN_HEADS), F32),
        compiler_params=_params(("arbitrary",)),
    )(dbias, bucket)


def _attn_common(qkvg, kv_prev, blk):
    lane = lax.broadcasted_iota(jnp.int32, (1, 128), 1)
    half = (lane < ATTN_HEAD_DIM, lane >= ATTN_HEAD_DIM)
    kv_cur = qkvg[:, C_AK:C_AG]
    win = jnp.concatenate([kv_prev, kv_cur], axis=0)
    k_slab, v_slab = [], []
    for r in range(2):
        ks = win[:, 128 * r:128 * r + 128]
        vs = win[:, 256 + 128 * r:256 + 128 * r + 128]
        k_slab.append((ks, pltpu.roll(ks, ATTN_HEAD_DIM, 1)))
        v_slab.append((vs, pltpu.roll(vs, ATTN_HEAD_DIM, 1)))
    rows4 = ATTN_GROUP * ATTN_BLOCK
    qi = lax.broadcasted_iota(jnp.int32, (rows4, 2 * ATTN_BLOCK), 0) & (ATTN_BLOCK - 1)
    si = lax.broadcasted_iota(jnp.int32, (rows4, 2 * ATTN_BLOCK), 1)
    valid = (si > qi) & (si <= qi + ATTN_BLOCK) & ((si >= ATTN_BLOCK) | (blk > 0))
    return half, k_slab, v_slab, valid


def _stack_heads(half, slab0, slab1):
    return jnp.concatenate([jnp.where(half[0], slab0, 0.0), jnp.where(half[1], slab0, 0.0),
                            jnp.where(half[0], slab1, 0.0), jnp.where(half[1], slab1, 0.0)], axis=0)


def _unstack_heads(half, x4):
    B = ATTN_BLOCK
    return (jnp.where(half[0], x4[0:B], x4[B:2 * B]), jnp.where(half[0], x4[2 * B:3 * B], x4[3 * B:4 * B]))


def _attn_group(j, qkvg, half, k_slab, v_slab, valid, bias_ref, sinks_ref):
    r, aj = j // 2, j % 2
    pick = (lambda a, b: jnp.where(half[0], a, b)) if aj == 0 else (lambda a, b: jnp.where(half[0], b, a))
    kb = pick(*k_slab[r]).astype(BF16)
    vb = pick(*v_slab[r]).astype(BF16)
    q4 = _stack_heads(half, qkvg[:, 256 * j:256 * j + 128], qkvg[:, 256 * j + 128:256 * j + 256]).astype(BF16)
    bias4 = bias_ref[ATTN_GROUP * j:ATTN_GROUP * (j + 1)].reshape(valid.shape)
    yield
    s = _dot(q4, kb, NT) * (ATTN_HEAD_DIM ** -0.5) + bias4
    s = jnp.where(valid, s, NEG_BIG)
    rowblk = lax.broadcasted_iota(jnp.int32, (valid.shape[0], 1), 0) // ATTN_BLOCK
    sink = jnp.full((valid.shape[0], 1), sinks_ref[0, ATTN_GROUP * j], F32)
    for b in range(1, ATTN_GROUP):
        sink = jnp.where(rowblk == b, sinks_ref[0, ATTN_GROUP * j + b], sink)
    m = jnp.maximum(jnp.max(s, axis=-1, keepdims=True), sink)
    e = jnp.exp(s - m)
    es = jnp.exp(sink - m)
    inv = 1.0 / (jnp.sum(e, axis=-1, keepdims=True) + es)
    pn = e * inv
    yield
    o4 = _dot(pn.astype(BF16), vb)
    return dict(r=r, aj=aj, kb=kb, vb=vb, q4=q4, pn=pn, psink=es * inv, o4=o4)


def _attn_specs(nb):
    row = lambda b, i: b * nb + i
    return [
        pl.BlockSpec((ATTN_BLOCK, C_HQ), lambda b, i: (row(b, i), 0)),
        pl.BlockSpec((ATTN_BLOCK, 512), lambda b, i: (row(b, jnp.maximum(i - 1, 0)), 2)),
        pl.BlockSpec((ATTN_HEADS, ATTN_BLOCK, 2 * ATTN_BLOCK), lambda b, i: (0, 0, 0)),
        pl.BlockSpec(memory_space=pltpu.SMEM),
    ]


def _attn_fwd(proj, bias, sinks, nseq, nb, shards):
    T = proj.shape[0]
    n = len(shards)

    def body(qkvg_ref, kvp_ref, bias_ref, sinks_ref, *refs):
        ins, ya_ref, outs = refs[:n], refs[n], refs[n + 1:2 * n + 1]
        ops = _GatherOps(ins, outs, *refs[2 * n + 1:])

        @pl.when((pl.program_id(0) == 0) & (pl.program_id(1) == 0))
        def _():
            ops.start()

        qkvg = qkvg_ref[...].astype(F32)
        half, k_slab, v_slab, valid = _attn_common(qkvg, kvp_ref[...].astype(F32), pl.program_id(1))
        groups = _interleave([_attn_group(j, qkvg, half, k_slab, v_slab, valid, bias_ref, sinks_ref)
                              for j in range(ATTN_HEADS // ATTN_GROUP)])
        slabs = []
        for grp in groups:
            slabs += _unstack_heads(half, grp["o4"])
        o_all = jnp.concatenate(slabs, axis=1)
        g = qkvg[:, C_AG:C_HQ]
        ya_ref[...] = (o_all * (g * _sigmoid(g))).astype(BF16)

        @pl.when((pl.program_id(0) == nseq - 1) & (pl.program_id(1) == nb - 1))
        def _():
            ops.finish()

    anyspec = lambda: pl.BlockSpec(memory_space=pl.ANY)
    return pl.pallas_call(
        body, name="attn_fwd",
        grid=(nseq, nb),
        in_specs=_attn_specs(nb) + [anyspec() for _ in shards],
        out_specs=[pl.BlockSpec((ATTN_BLOCK, 1024), lambda b, i: (b * nb + i, 0))] + [anyspec() for _ in shards],
        out_shape=[jax.ShapeDtypeStruct((T, 1024), BF16)]
                  + [jax.ShapeDtypeStruct((N_DEV * s.shape[0], s.shape[1]), s.dtype) for s in shards],
        scratch_shapes=_GatherOps.scratch(n),
        compiler_params=_params(("arbitrary", "arbitrary")),
    )(proj, proj, bias, sinks, *shards)


def _attn_bwd(proj, bias, sinks, d_ya, nseq, nb):
    T = proj.shape[0]
    S = nb * ATTN_BLOCK
    scale = ATTN_HEAD_DIM ** -0.5

    def body(qkvg_ref, kvp_ref, bias_ref, sinks_ref, dya_ref, dq_ref, dkv_ref, dg_ref, dbias_ref, dsinks_ref):
        b, i = pl.program_id(0), pl.program_id(1)
        first = (b == 0) & (i == 0)

        @pl.when(first)
        def _():
            dbias_ref[...] = jnp.zeros(dbias_ref.shape, F32)
            for h in range(ATTN_HEADS):
                dsinks_ref[0, h] = 0.0

        qkvg = qkvg_ref[...].astype(F32)
        half, k_slab, v_slab, valid = _attn_common(qkvg, kvp_ref[...].astype(F32), i)
        g = qkvg[:, C_AG:C_HQ]
        sg = _sigmoid(g)
        silu_g = g * sg
        dya = dya_ref[...].astype(F32)
        do_all = dya * silu_g
        dq_slabs, o_slabs = [], []
        dk_slab, dv_slab = [None, None], [None, None]
        B = ATTN_BLOCK

        def fold(x, aj):
            return jnp.where(half[aj], x + pltpu.roll(x, ATTN_HEAD_DIM, 1), 0.0)

        def group_bwd(j):
            grp = yield from _attn_group(j, qkvg, half, k_slab, v_slab, valid, bias_ref, sinks_ref)
            pn = grp["pn"]
            do4 = _stack_heads(half, do_all[:, 256 * j:256 * j + 128], do_all[:, 256 * j + 128:256 * j + 256])
            do4b = do4.astype(BF16)
            yield
            dp = _dot(do4b, grp["vb"], NT)
            delta = jnp.sum(do4 * grp["o4"], axis=-1, keepdims=True)
            ds = pn * (dp - delta)
            sink_term = grp["psink"] * delta
            for b4 in range(ATTN_GROUP):
                dsinks_ref[0, ATTN_GROUP * j + b4] += -jnp.sum(sink_term[B * b4:B * (b4 + 1)])
            dbias_ref[ATTN_GROUP * j:ATTN_GROUP * (j + 1)] += ds.reshape(ATTN_GROUP, B, 2 * B)
            dsb = ds.astype(BF16)
            yield
            dq4 = _dot(dsb, grp["kb"]) * scale
            dk_j = fold(_dot(dsb, grp["q4"], TN) * scale, grp["aj"])
            dv_j = fold(_dot(pn.astype(BF16), do4b, TN), grp["aj"])
            return dict(r=grp["r"], dq=_unstack_heads(half, dq4), o=_unstack_heads(half, grp["o4"]), dk=dk_j, dv=dv_j)

        for res in _interleave([group_bwd(j) for j in range(ATTN_HEADS // ATTN_GROUP)]):
            r = res["r"]
            dq_slabs += res["dq"]
            o_slabs += res["o"]
            dk_slab[r] = res["dk"] if dk_slab[r] is None else dk_slab[r] + res["dk"]
            dv_slab[r] = res["dv"] if dv_slab[r] is None else dv_slab[r] + res["dv"]

        dq_ref[...] = jnp.concatenate(dq_slabs, axis=1).astype(BF16)
        o_all = jnp.concatenate(o_slabs, axis=1)
        dg_ref[...] = (dya * o_all * (sg * (1.0 + g * (1.0 - sg)))).astype(BF16)

        dkv = jnp.concatenate(dk_slab + dv_slab, axis=1)
        cur = pl.multiple_of(i * ATTN_BLOCK, ATTN_BLOCK)
        dkv_ref[pl.ds(cur, ATTN_BLOCK), :] = dkv[ATTN_BLOCK:].astype(BF16)

        @pl.when(i > 0)
        def _():
            prev = pl.multiple_of((i - 1) * ATTN_BLOCK, ATTN_BLOCK)
            old = dkv_ref[pl.ds(prev, ATTN_BLOCK), :].astype(F32)
            dkv_ref[pl.ds(prev, ATTN_BLOCK), :] = (old + dkv[:ATTN_BLOCK]).astype(BF16)

    row_spec = lambda w: pl.BlockSpec((ATTN_BLOCK, w), lambda b, i: (b * nb + i, 0))
    return pl.pallas_call(
        body, name="attn_bwd",
        grid=(nseq, nb),
        in_specs=_attn_specs(nb) + [row_spec(1024)],
        out_specs=[row_spec(1024),
                   pl.BlockSpec((S, 512), lambda b, i: (b, 0)),
                   row_spec(1024),
                   pl.BlockSpec((ATTN_HEADS, ATTN_BLOCK, 2 * ATTN_BLOCK), lambda b, i: (0, 0, 0)),
                   pl.BlockSpec(memory_space=pltpu.SMEM)],
        out_shape=[jax.ShapeDtypeStruct((T, 1024), BF16),
                   jax.ShapeDtypeStruct((T, 512), BF16),
                   jax.ShapeDtypeStruct((T, 1024), BF16),
                   jax.ShapeDtypeStruct((ATTN_HEADS, ATTN_BLOCK, 2 * ATTN_BLOCK), F32),
                   jax.ShapeDtypeStruct((1, ATTN_HEADS), F32)],
        compiler_params=_params(("arbitrary", "arbitrary")),
    )(proj, proj, bias, sinks, d_ya)


def _split3(x):
    hi = x.astype(BF16)
    r1 = x - hi.astype(F32)
    mid = r1.astype(BF16)
    lo = (r1 - mid.astype(F32)).astype(BF16)
    return jnp.concatenate([hi, mid, lo], axis=1)


def _tri_sum(tri, x):
    y = _dot(tri, _split3(x))
    return y[:, :128] + y[:, 128:256] + y[:, 256:]


def _interleave(stages):
    results = [None] * len(stages)
    live = list(range(len(stages)))
    while live:
        still = []
        for idx in live:
            try:
                next(stages[idx])
                still.append(idx)
            except StopIteration as done:
                results[idx] = done.value
        live = still
    return results


def _hgrn_chunk(hq, hf, hi, lb):
    C = HGRN_CHUNK
    t = lax.broadcasted_iota(jnp.int32, (C, C), 0)
    s = lax.broadcasted_iota(jnp.int32, (C, C), 1)
    causal = s <= t
    sf = _sigmoid(hf)
    f = lb + (1.0 - lb) * sf
    lf = jnp.log(f)
    yield
    G = _tri_sum(causal.astype(BF16), lf)
    sq = _sigmoid(hq)
    qs = hq * sq
    k = 1.0 - f
    rowblk = lax.broadcasted_iota(jnp.int32, (C, 1), 0) // HGRN_SUB
    qt, kt, eq, ek = [], [], [], []
    for i in range(C // HGRN_SUB):
        lo = HGRN_SUB * i
        ref = G[lo + HGRN_SUB // 2:lo + HGRN_SUB // 2 + 1, :]
        eq_i = jnp.exp(G[lo:lo + HGRN_SUB] - ref)
        ek_i = jnp.exp(jnp.where(rowblk <= i, ref - G, 0.0))
        eq.append(eq_i)
        ek.append(ek_i)
        qt.append((qs[lo:lo + HGRN_SUB] * eq_i).astype(BF16))
        kt.append((k * ek_i).astype(BF16))
    yield
    A = jnp.concatenate([_dot(qt[i], kt[i], NT) for i in range(C // HGRN_SUB)], axis=0)
    A = jnp.where(causal, A, 0.0)
    glast = G[C - 1:C, :]
    eG = jnp.exp(G)
    edec = jnp.exp(glast - G)
    return dict(causal=causal, sf=sf, f=f, G=G, sq=sq, qs=qs, k=k, qt=qt, kt=kt, eq=eq, ek=ek, A=A,
                glast=glast, eG=eG, edec=edec, qhat=qs * eG, kdec=k * edec, v=hi)


def _hgrn_specs(nseq, ng, rows, reverse):
    W = HGRN_PAR
    nblk = W * HGRN_DIM // HGRN_COLS
    gi = (lambda g: ng - 1 - g) if reverse else (lambda g: g)

    def cols(c0):
        return [pl.BlockSpec((rows, HGRN_COLS),
                             lambda h, b, g, q=q: (b * ng + gi(g), c0 // HGRN_COLS + h * nblk + q)) for q in range(nblk)]

    return gi, (cols(C_HQ) + cols(C_HF) + cols(C_HI) + cols(C_HG)
                + [pl.BlockSpec((2, W * HGRN_DIM), lambda h, b, g: (0, h)),
                   pl.BlockSpec((W, 1, HGRN_DIM), lambda h, b, g: (h, 0, 0))])


def _hgrn_operands(refs):
    nblk = HGRN_PAR * HGRN_DIM // HGRN_COLS
    per = HGRN_COLS // HGRN_DIM

    def reader(group):
        def read(rs, w):
            lo = HGRN_DIM * (w % per)
            return group[w // per][rs, lo:lo + HGRN_DIM].astype(F32)
        return read

    readers = [reader(refs[nblk * a:nblk * (a + 1)]) for a in range(4)]
    return readers, refs[4 * nblk], refs[4 * nblk + 1], refs[4 * nblk + 2:]


def _hgrn_fwd(proj, lb_logits, gain3, nseq, S, shards):
    T = proj.shape[0]
    W = HGRN_PAR
    nc = S // HGRN_CHUNK
    cg = min(8, nc)
    ng = nc // cg
    rows = cg * HGRN_CHUNK
    n = len(shards)
    nh = HGRN_HEADS // W

    def body(*all_refs):
        (hq, hf, hi, hg), lbl_ref, gain_ref, refs = _hgrn_operands(all_refs)
        ins = refs[:n]
        o_ref, yh_ref, st_ref = refs[n:n + 3]
        outs = refs[n + 3:2 * n + 3]
        state, send_sems, recv_sems, local_sems = refs[2 * n + 3:]
        step = (pl.program_id(0), pl.program_id(1), pl.program_id(2))
        ops = _GatherOps(ins, outs, send_sems, recv_sems, local_sems)

        @pl.when((step[0] == 0) & (step[1] == 0) & (step[2] == 0))
        def _():
            ops.start()

        @pl.when(pl.program_id(2) == 0)
        def _():
            state[...] = jnp.zeros(state.shape, F32)

        lb_all = _sigmoid(lbl_ref[0:1, :] - lbl_ref[1:2, :])

        def head(w, c):
            rs = pl.ds(pl.multiple_of(c * HGRN_CHUNK, HGRN_CHUNK), HGRN_CHUNK)
            ls = slice(HGRN_DIM * w, HGRN_DIM * (w + 1))
            ch = yield from _hgrn_chunk(hq(rs, w), hf(rs, w), hi(rs, w), lb_all[:, ls])
            st = state[w]
            st_ref[w, c] = st
            vb = ch["v"].astype(BF16)
            yield
            o = _dot(ch["qhat"].astype(BF16), st.astype(BF16), NT) + _dot(ch["A"].astype(BF16), vb)
            state[w] = st * jnp.exp(ch["glast"]) + _dot(vb, ch["kdec"].astype(BF16), TN)
            o_ref[rs, ls] = o
            r = lax.rsqrt(jnp.mean(o * o, axis=-1, keepdims=True) + NORM_EPS)
            gate = hg(rs, w)
            yh_ref[rs, ls] = (o * r * gain_ref[w] * (gate * _sigmoid(gate))).astype(BF16)

        def chunk(c, carry):
            _interleave([head(w, c) for w in range(W)])
            return carry

        lax.fori_loop(0, cg, chunk, 0)

        @pl.when((step[0] == nh - 1) & (step[1] == nseq - 1) & (step[2] == ng - 1))
        def _():
            ops.finish()

    _, in_specs = _hgrn_specs(nseq, ng, rows, False)
    out_row = lambda: pl.BlockSpec((rows, W * HGRN_DIM), lambda h, b, g: (b * ng + g, h))
    anyspec = lambda: pl.BlockSpec(memory_space=pl.ANY)
    return pl.pallas_call(
        body, name="hgrn_fwd",
        grid=(nh, nseq, ng),
        in_specs=in_specs + [anyspec() for _ in shards],
        out_specs=[out_row(), out_row(),
                   pl.BlockSpec((None, W, cg, HGRN_DIM, HGRN_DIM), lambda h, b, g: (b, h, g, 0, 0))]
                  + [anyspec() for _ in shards],
        out_shape=[jax.ShapeDtypeStruct((T, 1024), F32),
                   jax.ShapeDtypeStruct((T, 1024), BF16),
                   jax.ShapeDtypeStruct((nseq, HGRN_HEADS, nc, HGRN_DIM, HGRN_DIM), F32)]
                  + [jax.ShapeDtypeStruct((N_DEV * s.shape[0], s.shape[1]), s.dtype) for s in shards],
        scratch_shapes=[pltpu.VMEM((W, HGRN_DIM, HGRN_DIM), F32)] + _GatherOps.scratch(n),
        compiler_params=_params(("arbitrary", "arbitrary", "arbitrary")),
    )(*([proj] * (4 * W * HGRN_DIM // HGRN_COLS)), lb_logits, gain3, *shards)


def _hgrn_bwd(proj, lb_logits, gain3, o, d_yh, states, nseq, S, chip_sums):
    T = proj.shape[0]
    W = HGRN_PAR
    n = len(chip_sums)
    nh = HGRN_HEADS // W
    assert nh == 1
    nc = S // HGRN_CHUNK
    cg = min(8, nc)
    ng = nc // cg
    rows = cg * HGRN_CHUNK
    C = HGRN_CHUNK
    nsub = C // HGRN_SUB

    def body(*all_refs):
        (hq_of, hf_of, hi_of, hg_of), lbl_ref, gain_ref, refs = _hgrn_operands(all_refs)
        o_ref, dyh_ref, st_ref = refs[:3]
        sums_in = refs[3:3 + n]
        dh4_ref, dgain_ref, dlbl_ref = refs[3 + n:6 + n]
        sums_out = refs[6 + n:6 + 2 * n]
        dstate, dlb_acc, send_sems, recv_sems, local_sems = refs[6 + 2 * n:]
        h, b, g = pl.program_id(0), pl.program_id(1), pl.program_id(2)
        exchange = _ChipExchange(sums_in, sums_out, send_sems, recv_sems, local_sems)

        @pl.when((h == 0) & (b == 0) & (g == 0))
        def _():
            exchange.start()

        @pl.when(g == 0)
        def _():
            dstate[...] = jnp.zeros(dstate.shape, F32)

        @pl.when((b == 0) & (g == 0))
        def _():
            dgain_ref[...] = jnp.zeros(dgain_ref.shape, F32)
            dlb_acc[...] = jnp.zeros(dlb_acc.shape, F32)

        lb_all = _sigmoid(lbl_ref[0:1, :] - lbl_ref[1:2, :])

        anti = (lax.broadcasted_iota(jnp.int32, (C, C), 1) >= lax.broadcasted_iota(jnp.int32, (C, C), 0)).astype(BF16)
        last_row = lax.broadcasted_iota(jnp.int32, (C, 1), 0) == C - 1

        def head_load(w, c):
            rs = pl.ds(pl.multiple_of(c * C, C), C)
            ls = slice(HGRN_DIM * w, HGRN_DIM * (w + 1))
            return dict(hq=hq_of(rs, w), hf=hf_of(rs, w), hi=hi_of(rs, w), hg=hg_of(rs, w), lb=lb_all[:, ls],
                        gain=gain_ref[w], ov=o_ref[rs, ls], dyh=dyh_ref[rs, ls].astype(F32),
                        st=st_ref[w, c], dst=dstate[w])

        def head_math(v):
            lb, gain, hq, hg = v["lb"], v["gain"], v["hq"], v["hg"]
            ch = yield from _hgrn_chunk(hq, v["hf"], v["hi"], lb)
            ov, dyh = v["ov"], v["dyh"]
            r = lax.rsqrt(jnp.mean(ov * ov, axis=-1, keepdims=True) + NORM_EPS)
            on = ov * r
            sg = _sigmoid(hg)
            doh = dyh * (hg * sg)
            out = dict(dhg=(dyh * on * gain * (sg * (1.0 + hg * (1.0 - sg)))).astype(BF16),
                       dgain=jnp.sum(doh * on, axis=0, keepdims=True))
            don = doh * gain
            do = r * (don - on * jnp.mean(don * on, axis=-1, keepdims=True))
            dob = do.astype(BF16)
            st, dst = v["st"], v["dst"]
            stb, dstb = st.astype(BF16), dst.astype(BF16)
            vb = ch["v"].astype(BF16)
            qhatb = ch["qhat"].astype(BF16)
            eglast = jnp.exp(ch["glast"])
            yield
            dqhat = _dot(dob, stb)
            dkdec = _dot(vb, dstb)
            dv = _dot(ch["kdec"].astype(BF16), dstb, NT)
            deg = jnp.sum(dst * st, axis=0, keepdims=True)
            out["dstate"] = dst * eglast + _dot(dob, qhatb, TN)
            dA = jnp.where(ch["causal"], _dot(dob, vb, NT), 0.0)
            dv = dv + _dot(ch["A"].astype(BF16), dob, TN)
            dAb = dA.astype(BF16)
            yield
            dqs_parts, dgq_parts = [], []
            dk_intra, dgk = None, None
            for i in range(nsub):
                dA_i = dAb[HGRN_SUB * i:HGRN_SUB * (i + 1)]
                dqt = _dot(dA_i, ch["kt"][i])
                dkt = _dot(dA_i, ch["qt"][i], TN)
                dqs_parts.append(dqt * ch["eq"][i])
                dgq_parts.append(dqt * ch["qt"][i].astype(F32))
                dk_i = dkt * ch["ek"][i]
                dgk_i = dkt * ch["kt"][i].astype(F32)
                dk_intra = dk_i if dk_intra is None else dk_intra + dk_i
                dgk = dgk_i if dgk is None else dgk + dgk_i
            dqs_inter = dqhat * ch["eG"]
            dk_state = dkdec * ch["edec"]
            dqs = jnp.concatenate(dqs_parts, axis=0) + dqs_inter
            dk = dk_intra + dk_state
            dG = jnp.concatenate(dgq_parts, axis=0) - dgk + ch["qs"] * dqs_inter - ch["k"] * dk_state
            tail = jnp.sum(dkdec * ch["kdec"], axis=0, keepdims=True) + deg * eglast
            dG = dG + jnp.where(last_row, tail, 0.0)
            yield
            dlf = _tri_sum(anti, dG)
            df = dlf / ch["f"] - dk
            sf, sq = ch["sf"], ch["sq"]
            out["dhf"] = (df * (1.0 - lb) * sf * (1.0 - sf)).astype(BF16)
            out["dlb"] = jnp.sum(df * (1.0 - sf), axis=0, keepdims=True)
            out["dhq"] = (dqs * (sq * (1.0 + hq * (1.0 - sq)))).astype(BF16)
            out["dhi"] = dv.astype(BF16)
            return out

        def head_store(w, c, out):
            rs = pl.ds(pl.multiple_of(c * C, C), C)
            ls = slice(HGRN_DIM * w, HGRN_DIM * (w + 1))
            for a, key in enumerate(("dhq", "dhf", "dhi", "dhg")):
                lo = HGRN_HEADS * HGRN_DIM * a + HGRN_DIM * w
                dh4_ref[rs, lo:lo + HGRN_DIM] = out[key]
            dstate[w] = out["dstate"]
            dgain_ref[w] += out["dgain"]
            dlb_acc[:, ls] += out["dlb"]

        def chunk(cc, carry):
            c = cg - 1 - cc
            outs = _interleave([head_math(v) for v in [head_load(w, c) for w in range(W)]])
            for w in range(W):
                head_store(w, c, outs[w])
            return carry

        lax.fori_loop(0, cg, chunk, 0)

        @pl.when((b == nseq - 1) & (g == ng - 1))
        def _():
            dl0 = dlb_acc[...] * lb_all * (1.0 - lb_all)
            dlbl_ref[0:1, :] = dl0
            dlbl_ref[1:2, :] = -dl0

        @pl.when((h == nh - 1) & (b == nseq - 1) & (g == ng - 1))
        def _():
            exchange.wait()

    gi, in_specs = _hgrn_specs(nseq, ng, rows, True)
    row = lambda: pl.BlockSpec((rows, W * HGRN_DIM), lambda h, b, g: (b * ng + gi(g), h))
    anyspec = lambda: pl.BlockSpec(memory_space=pl.ANY)
    return pl.pallas_call(
        body, name="hgrn_bwd",
        grid=(nh, nseq, ng),
        in_specs=in_specs + [row(), row(),
                             pl.BlockSpec((None, W, cg, HGRN_DIM, HGRN_DIM), lambda h, b, g: (b, h, gi(g), 0, 0))]
                 + [anyspec() for _ in chip_sums],
        out_specs=[pl.BlockSpec((rows, 4 * W * HGRN_DIM), lambda h, b, g: (b * ng + gi(g), h)),
                   pl.BlockSpec((W, 1, HGRN_DIM), lambda h, b, g: (h, 0, 0)),
                   pl.BlockSpec((2, W * HGRN_DIM), lambda h, b, g: (0, h))] + [anyspec() for _ in chip_sums],
        out_shape=[jax.ShapeDtypeStruct((T, 4 * HGRN_HEADS * HGRN_DIM), BF16),
                   jax.ShapeDtypeStruct((HGRN_HEADS, 1, HGRN_DIM), F32),
                   jax.ShapeDtypeStruct((2, HGRN_HEADS * HGRN_DIM), F32)]
                  + [jax.ShapeDtypeStruct(s.shape, s.dtype) for s in chip_sums],
        scratch_shapes=[pltpu.VMEM((W, HGRN_DIM, HGRN_DIM), F32), pltpu.VMEM((1, W * HGRN_DIM), F32)]
                       + _ChipExchange.scratch(n),
        compiler_params=_params(("arbitrary", "arbitrary", "arbitrary")),
    )(*([proj] * (4 * W * HGRN_DIM // HGRN_COLS)), lb_logits, gain3, o, d_yh, states, *chip_sums)


def _gate_specs(tm):
    spec = lambda c0, q: pl.BlockSpec((tm, 512), lambda i: (i, c0 // 512 + q))
    return [spec(C_GA, q) for q in range(4)] + [spec(C_GH, q) for q in range(4)]


def _gates(refs):
    ga = jnp.concatenate([r[...] for r in refs[0:4]], axis=1).astype(F32)
    gh = jnp.concatenate([r[...] for r in refs[4:8]], axis=1).astype(F32)
    return ga, gh


def _branch_merge(proj, ya, yh, wa_t, wh_t):
    T = proj.shape[0]
    tm = min(512, T)

    def body(*refs):
        ya_ref, yh_ref, wa_ref, wh_ref, merged_ref, ua_ref, uh_ref = refs[8:]
        ga, gh = _gates(refs)
        ua = _dot(ya_ref[...], wa_ref[...], NT)
        uh = _dot(yh_ref[...], wh_ref[...], NT)
        merged_ref[...] = (_sigmoid(ga) * ua + _sigmoid(gh) * uh).astype(BF16)
        ua_ref[...] = ua.astype(BF16)
        uh_ref[...] = uh.astype(BF16)

    rowb = lambda w: pl.BlockSpec((tm, w), lambda i: (i, 0))
    full = lambda a: pl.BlockSpec(a.shape, lambda i: (0, 0))
    return pl.pallas_call(
        body, name="branch_merge",
        grid=(T // tm,),
        in_specs=_gate_specs(tm) + [rowb(1024), rowb(1024), full(wa_t), full(wh_t)],
        out_specs=[rowb(D_MODEL)] * 3,
        out_shape=[jax.ShapeDtypeStruct((T, D_MODEL), BF16)] * 3,
        compiler_params=_params(("arbitrary",)),
    )(*([proj] * 8), ya, yh, wa_t, wh_t)


def _out_norm_loss(merged, x2, tgt2, gpost, wout):
    T = merged.shape[0]
    tm = min(256, T)

    def body(m_ref, x_ref, t_ref, gpost_ref, wo_ref, dy_ref, dm_ref, dout_ref, loss_ref, dgpost_ref):
        @pl.when(pl.program_id(0) == 0)
        def _():
            loss_ref[...] = jnp.zeros(loss_ref.shape, F32)
            dgpost_ref[...] = jnp.zeros(dgpost_ref.shape, F32)

        y = _dot(m_ref[...], wo_ref[...])
        r2 = lax.rsqrt(jnp.mean(y * y, axis=-1, keepdims=True) + NORM_EPS)
        yn = y * r2
        gpost = gpost_ref[...]
        err = x_ref[...] + yn * gpost - t_ref[...]
        loss_ref[...] += jnp.sum(err * err, axis=0, keepdims=True)
        dout = err * (1.0 / D_MODEL)
        dout_ref[...] = dout
        dgpost_ref[...] += jnp.sum(dout * yn, axis=0, keepdims=True)
        dyn = dout * gpost
        dy = (r2 * (dyn - yn * jnp.mean(dyn * yn, axis=-1, keepdims=True))).astype(BF16)
        dy_ref[...] = dy
        dm_ref[...] = _dot(dy, wo_ref[...], NT).astype(BF16)

    rowb = lambda: pl.BlockSpec((tm, D_MODEL), lambda i: (i, 0))
    vec = lambda: pl.BlockSpec((1, D_MODEL), lambda i: (0, 0))
    return pl.pallas_call(
        body, name="out_norm_loss",
        grid=(T // tm,),
        in_specs=[rowb(), rowb(), rowb(), vec(), pl.BlockSpec(wout.shape, lambda i: (0, 0))],
        out_specs=[rowb(), rowb(), rowb(), vec(), vec()],
        out_shape=[jax.ShapeDtypeStruct((T, D_MODEL), BF16)] * 2
                  + [jax.ShapeDtypeStruct((T, D_MODEL), F32)] + [jax.ShapeDtypeStruct((1, D_MODEL), F32)] * 2,
        compiler_params=_params(("arbitrary",)),
    )(merged, x2, tgt2, gpost, wout)


def _branch_bwd(proj, dm, ua, uh, wa_t, wh_t):
    T = proj.shape[0]
    tm = min(256, T)

    def body(*refs):
        (dm_ref, ua_ref, uh_ref, wa_ref, wh_ref,
         dua_ref, duh_ref, dgg_ref, dya_ref, dyh_ref) = refs[8:]
        ga, gh = _gates(refs)
        sa, sh = _sigmoid(ga), _sigmoid(gh)
        dm = dm_ref[...].astype(F32)
        dua = (dm * sa).astype(BF16)
        duh = (dm * sh).astype(BF16)
        dua_ref[...] = dua
        duh_ref[...] = duh
        dgg_ref[:, :D_MODEL] = (dm * ua_ref[...].astype(F32) * (sa * (1.0 - sa))).astype(BF16)
        dgg_ref[:, D_MODEL:] = (dm * uh_ref[...].astype(F32) * (sh * (1.0 - sh))).astype(BF16)
        dya_ref[...] = _dot(dua, wa_ref[...]).astype(BF16)
        dyh_ref[...] = _dot(duh, wh_ref[...]).astype(BF16)

    rowb = lambda w: pl.BlockSpec((tm, w), lambda i: (i, 0))
    full = lambda a: pl.BlockSpec(a.shape, lambda i: (0, 0))
    return pl.pallas_call(
        body, name="branch_bwd",
        grid=(T // tm,),
        in_specs=_gate_specs(tm) + [rowb(D_MODEL)] * 3 + [full(wa_t), full(wh_t)],
        out_specs=[rowb(D_MODEL)] * 2 + [rowb(2 * D_MODEL)] + [rowb(1024)] * 2,
        out_shape=[jax.ShapeDtypeStruct((T, D_MODEL), BF16)] * 2 + [jax.ShapeDtypeStruct((T, 2 * D_MODEL), BF16)]
                  + [jax.ShapeDtypeStruct((T, 1024), BF16)] * 2,
        compiler_params=_params(("arbitrary",)),
    )(*([proj] * 8), dm, ua, uh, wa_t, wh_t)


def _tn_matmul(L, R, bm, bn, name):
    T, M = L.shape
    N = R.shape[1]

    def body(l_ref, r_ref, out_ref):
        out_ref[...] = _dot(l_ref[...], r_ref[...], TN).astype(BF16)

    return pl.pallas_call(
        body, name=name,
        grid=(N // bn, M // bm),
        in_specs=[pl.BlockSpec((T, bm), lambda j, i: (0, i)),
                  pl.BlockSpec((T, bn), lambda j, i: (0, j))],
        out_specs=pl.BlockSpec((bm, bn), lambda j, i: (i, j)),
        out_shape=jax.ShapeDtypeStruct((M, N), BF16),
        compiler_params=_params(("arbitrary", "arbitrary")),
    )(L, R)


def _tn_matmul_pieces(pieces, R, bm, bn, name):
    T, N = R.shape
    M = sum(p.shape[1] for p in pieces)
    out, row0 = None, 0
    for q, L in enumerate(pieces):
        off = row0 // bm

        def body(*refs):
            refs[-1][...] = _dot(refs[0][...], refs[1][...], TN).astype(BF16)

        prev = [] if out is None else [out]
        out = pl.pallas_call(
            body, name="%s_%d" % (name, q),
            grid=(N // bn, L.shape[1] // bm),
            in_specs=[pl.BlockSpec((T, bm), lambda j, i: (0, i)), pl.BlockSpec((T, bn), lambda j, i: (0, j))]
                     + [pl.BlockSpec(memory_space=pl.ANY) for _ in prev],
            out_specs=pl.BlockSpec((bm, bn), lambda j, i, off=off: (off + i, j)),
            out_shape=jax.ShapeDtypeStruct((M, N), BF16),
            input_output_aliases={2: 0} if prev else {},
            compiler_params=_params(("arbitrary", "arbitrary")),
        )(L, R, *prev)
        row0 += L.shape[1]
    return out


def _dh_prenorm_bwd(pieces, wt_in, x2, dout, gpre, chip_sums):
    T = x2.shape[0]
    tm = min(1024, T)
    ne = 4
    te = tm // ne
    tk = 512
    nk = IN_WIDTH // tk
    nt = T // tm
    n = len(chip_sums)
    npiece = len(pieces)
    counts = [p.shape[1] // tk for p in pieces]
    starts = [sum(counts[:q]) for q in range(npiece)]
    assert sum(counts) == nk and all(p.shape[1] % tk == 0 for p in pieces)

    def body(*all_refs):
        piece_refs = all_refs[:npiece]
        w_ref, x_ref, dout_ref, g_ref = all_refs[npiece:npiece + 4]
        refs = all_refs[npiece + 4:]
        ins = refs[:n]
        gx_ref, dg_ref = refs[n], refs[n + 1]
        outs = refs[n + 2:2 * n + 2]
        acc, send_sems, recv_sems, local_sems = refs[2 * n + 2:]
        i, k = pl.program_id(0), pl.program_id(1)
        exchange = _ChipExchange(ins, outs, send_sems, recv_sems, local_sems)

        @pl.when((i == 0) & (k == 0))
        def _():
            dg_ref[...] = jnp.zeros(dg_ref.shape, F32)
            exchange.start()

        @pl.when((i == nt - 1) & (k == nk + ne - 1))
        def _():
            exchange.wait()

        @pl.when(k == 0)
        def _():
            acc[...] = jnp.zeros(acc.shape, F32)

        for q in range(npiece):
            @pl.when((k >= starts[q]) & (k < starts[q] + counts[q]))
            def _(q=q):
                acc[...] += _dot(piece_refs[q][...], w_ref[...])

        @pl.when(k >= nk)
        def _():
            dh = acc[pl.ds(pl.multiple_of((k - nk) * te, te), te), :]
            x = x_ref[...]
            r = lax.rsqrt(jnp.mean(x * x, axis=-1, keepdims=True) + NORM_EPS)
            xn = x * r
            dg_ref[...] += jnp.sum(dh * xn, axis=0, keepdims=True)
            dxn = dh * g_ref[...]
            gx_ref[...] = dout_ref[...] + r * (dxn - xn * jnp.mean(dxn * xn, axis=-1, keepdims=True))

    piece_spec = lambda q: pl.BlockSpec((tm, tk), lambda i, k: (i, jnp.clip(k - starts[q], 0, counts[q] - 1)))
    rowb = lambda: pl.BlockSpec((te, D_MODEL), lambda i, k: (ne * i + jnp.clip(k - nk, 0, ne - 1), 0))
    vec = lambda: pl.BlockSpec((1, D_MODEL), lambda i, k: (0, 0))
    anyspec = lambda: pl.BlockSpec(memory_space=pl.ANY)
    return pl.pallas_call(
        body, name="dh_prenorm_bwd",
        grid=(nt, nk + ne),
        in_specs=[piece_spec(q) for q in range(npiece)]
                 + [pl.BlockSpec((tk, D_MODEL), lambda i, k: (jnp.minimum(k, nk - 1), 0)),
                    rowb(), rowb(), vec()] + [anyspec() for _ in chip_sums],
        out_specs=[rowb(), vec()] + [anyspec() for _ in chip_sums],
        out_shape=[jax.ShapeDtypeStruct((T, D_MODEL), F32), jax.ShapeDtypeStruct((1, D_MODEL), F32)]
                  + [jax.ShapeDtypeStruct(s.shape, s.dtype) for s in chip_sums],
        scratch_shapes=[pltpu.VMEM((tm, D_MODEL), F32)] + _ChipExchange.scratch(n),
        compiler_params=_params(("arbitrary", "arbitrary")),
    )(*pieces, wt_in, x2, dout, gpre, *chip_sums)


def _sum_slots(recv, br, name):
    nslot, R, C = recv.shape

    def body(r_ref, out_ref):
        acc = r_ref[0].astype(F32)
        for s in range(1, nslot):
            acc = acc + r_ref[s].astype(F32)
        out_ref[...] = acc

    return pl.pallas_call(
        body, name=name,
        grid=(R // br,),
        in_specs=[pl.BlockSpec((nslot, br, C), lambda i: (0, i, 0))],
        out_specs=pl.BlockSpec((br, C), lambda i: (i, 0)),
        out_shape=jax.ShapeDtypeStruct((R, C), F32),
        compiler_params=_params(("arbitrary",)),
    )(recv)


def _adamw_math(w, g, m, v):
    m = ADAM_B1 * m + (1.0 - ADAM_B1) * g
    v = ADAM_B2 * v + (1.0 - ADAM_B2) * (g * g)
    m_hat = m / (1.0 - ADAM_B1 ** ADAM_STEP)
    v_hat = v / (1.0 - ADAM_B2 ** ADAM_STEP)
    delta = -ADAM_LR * (m_hat / (jnp.sqrt(v_hat) + ADAM_EPS) + ADAM_WD * w)
    return delta, m, v


def _adamw(w, g, m, v, br, name):
    R, C = g.shape
    lead = (None,) * (w.ndim - 2)

    def body(w_ref, g_ref, m_ref, v_ref, d_ref, nm_ref, nv_ref):
        d_ref[...], nm_ref[...], nv_ref[...] = _adamw_math(w_ref[...], g_ref[...], m_ref[...], v_ref[...])

    spec = lambda: pl.BlockSpec(lead + (br, C), lambda i: (0,) * len(lead) + (i, 0))
    return pl.pallas_call(
        body, name=name,
        grid=(R // br,),
        in_specs=[spec(), pl.BlockSpec((br, C), lambda i: (i, 0)), spec(), spec()],
        out_specs=[spec(), spec(), spec()],
        out_shape=[jax.ShapeDtypeStruct(w.shape, F32)] * 3,
        compiler_params=_params(("arbitrary",)),
    )(w, g, m, v)


def _sibling_exchange(partials, name):
    n = len(partials)

    def body(*refs):
        ins, outs = refs[:n], refs[n:2 * n]
        send_sems, recv_sems = refs[2 * n:]
        x, y, c = _place()

        def copy(a, p):
            return pltpu.make_async_remote_copy(
                src_ref=ins[a].at[p, 1 - c], dst_ref=outs[a].at[p],
                send_sem=send_sems.at[a, p], recv_sem=recv_sems.at[a, p],
                device_id=(x, y, 1 - c), device_id_type=pl.DeviceIdType.MESH)

        copies = [copy(a, p) for p in range(4) for a in range(n)]
        for cp in copies:
            cp.start()
        for cp in copies:
            cp.wait()

    anyspec = lambda: pl.BlockSpec(memory_space=pl.ANY)
    return pl.pallas_call(
        body, name=name,
        in_specs=[anyspec() for _ in partials],
        out_specs=[anyspec() for _ in partials],
        out_shape=[jax.ShapeDtypeStruct((4,) + p.shape[2:], p.dtype) for p in partials],
        scratch_shapes=[pltpu.SemaphoreType.DMA((n, 4)), pltpu.SemaphoreType.DMA((n, 4))],
    )(*partials)


def _chip_sum(partial, from_sibling, br, name):
    _, _, R, C = partial.shape
    cls = lax.axis_index("c").astype(jnp.int32).reshape(1)

    def body(c_ref, mine_ref, sib_ref, out_ref):
        out_ref[...] = (mine_ref[...].astype(F32) + sib_ref[...].astype(F32)).astype(BF16)

    grid_spec = pltpu.PrefetchScalarGridSpec(
        num_scalar_prefetch=1,
        grid=(4, R // br),
        in_specs=[pl.BlockSpec((None, None, br, C), lambda p, i, c: (p, c[0], i, 0)),
                  pl.BlockSpec((None, br, C), lambda p, i, c: (p, i, 0))],
        out_specs=pl.BlockSpec((None, br, C), lambda p, i, c: (p, i, 0)),
    )
    return pl.pallas_call(
        body, name=name, grid_spec=grid_spec,
        out_shape=jax.ShapeDtypeStruct((4, R, C), BF16),
        compiler_params=_params(("arbitrary", "arbitrary")),
    )(cls, partial, from_sibling)


def _all_reduce_small(packed):
    shape = packed.shape

    def body(in_ref, out_ref, slots, send_sems, recv_sems):
        x, y, c = _place()
        my_slot = 4 * x + 2 * y + c

        def peer(k):
            return (x ^ ((k >> 2) & 1), y ^ ((k >> 1) & 1), c ^ (k & 1))

        def copy(k):
            p = peer(k)
            return pltpu.make_async_remote_copy(
                src_ref=in_ref, dst_ref=slots.at[my_slot],
                send_sem=send_sems.at[k - 1], recv_sem=recv_sems.at[k - 1],
                device_id=p, device_id_type=pl.DeviceIdType.MESH)

        def arrival(k):
            p = peer(k)
            return pltpu.make_async_remote_copy(
                src_ref=in_ref, dst_ref=slots.at[4 * p[0] + 2 * p[1] + p[2]],
                send_sem=send_sems.at[k - 1], recv_sem=recv_sems.at[k - 1],
                device_id=p, device_id_type=pl.DeviceIdType.MESH)

        sends = [copy(k) for k in range(1, N_DEV)]
        for cp in sends:
            cp.start()
        slots[my_slot] = in_ref[...]
        for k in range(1, N_DEV):
            arrival(k).wait_recv()
        for cp in sends:
            cp.wait_send()
        acc = slots[0]
        for s in range(1, N_DEV):
            acc = acc + slots[s]
        out_ref[...] = acc

    return pl.pallas_call(
        body, name="all_reduce_small",
        in_specs=[pl.BlockSpec(memory_space=pltpu.VMEM)],
        out_specs=pl.BlockSpec(memory_space=pltpu.VMEM),
        out_shape=jax.ShapeDtypeStruct(shape, F32),
        scratch_shapes=[pltpu.VMEM((N_DEV,) + shape, F32),
                        pltpu.SemaphoreType.DMA((7,)), pltpu.SemaphoreType.DMA((7,))],
    )(packed)


def _pack_small(norm_pre, norm_post, lb_logits, hgrn_norm, rel_bias, sinks, extra=None):
    tail = [hgrn_norm.reshape(1, 1024), rel_bias.reshape(1, 512), sinks.reshape(1, 16)]
    used = 1024 + 512 + 16
    if extra is not None:
        tail.append(extra.reshape(1, 1))
        used += 1
    tail.append(jnp.zeros((1, D_MODEL - used), F32))
    rows = [norm_pre.reshape(1, D_MODEL), norm_post.reshape(1, D_MODEL), lb_logits.reshape(1, D_MODEL),
            jnp.concatenate(tail, axis=1), jnp.zeros((4, D_MODEL), F32)]
    return jnp.concatenate(rows, axis=0)


def _unpack_small(p):
    return (p[0:1], p[3, 1024:1536].reshape(REL_BUCKETS, ATTN_HEADS), p[3:4, 1536:1552],
            p[2].reshape(2, 1024), p[3, 0:1024].reshape(1, HGRN_HEADS, HGRN_DIM), p[1:2])


def _local_step(nseq, S, x2, tgt2, proj, h, rel_bias, attn_sinks, lb_logits, hgrn_norm, norm_post, shards):
    nb = S // ATTN_BLOCK
    bucket = jnp.asarray(_t5_bucket_table())
    gain3 = hgrn_norm.reshape(HGRN_HEADS, 1, HGRN_DIM)

    bias = _bias_table(rel_bias, bucket)
    ya, wout = _attn_fwd(proj, bias, attn_sinks, nseq, nb, shards[:1])
    o, yh, states, wa_t, wh_t = _hgrn_fwd(proj, lb_logits, gain3, nseq, S, shards[1:])
    merged, ua, uh = _branch_merge(proj, ya, yh, wa_t, wh_t)
    dy, dm, dout, loss_cols, d_gpost = _out_norm_loss(merged, x2, tgt2, norm_post, wout)
    dua, duh, dgg, dya, dyh = _branch_bwd(proj, dm, ua, uh, wa_t, wh_t)

    p_out = _tn_matmul(merged, dy, 256, 1024, "dw_out")
    p_a = _tn_matmul(dua, ya, 256, 1024, "dw_branch_attn")
    p_h = _tn_matmul(duh, yh, 256, 1024, "dw_branch_hgrn")
    small_sums = _chip_sums([p_out, p_a, p_h], (128, 128, 128), ("dw_out", "dw_branch_attn", "dw_branch_hgrn"),
                            "sibling_exchange_small")

    dq, dkv, dg, dbias, d_sinks = _attn_bwd(proj, bias, attn_sinks, dya, nseq, nb)
    d_rel_bias = _bias_table_bwd(dbias, bucket)
    dh4, d_gain, d_lbl, r_out, r_a, r_h = _hgrn_bwd(proj, lb_logits, gain3, o, dyh, states, nseq, S, small_sums)
    dproj = [dq, dkv, dg, dh4, dgg]
    p_in = _tn_matmul_pieces(dproj, h, 512, 1024, "dw_in")
    return dproj, dout, p_in, r_out, r_a, r_h, d_gpost, d_lbl, d_gain, d_rel_bias, d_sinks, loss_cols


def _chip_sums(partials, block_rows, names, exchange_name):
    split = [p.reshape(4, 2, p.shape[0] // N_DEV, p.shape[1]) for p in partials]
    from_sibling = _sibling_exchange(split, exchange_name)
    return [_chip_sum(p, f, br, "chip_sum_" + nm) for p, f, br, nm in zip(split, from_sibling, block_rows, names)]


def kernel(x, norm_pre, w_in, rel_bias, attn_sinks, lb_logits, hgrn_norm, w_branch_attn, w_branch_hgrn, w_out, norm_post, loss_target, m_norm_pre, m_w_in, m_rel_bias, m_attn_sinks, m_lb_logits, m_hgrn_norm, m_w_branch_attn, m_w_branch_hgrn, m_w_out, m_norm_post, v_norm_pre, v_w_in, v_rel_bias, v_attn_sinks, v_lb_logits, v_hgrn_norm, v_w_branch_attn, v_w_branch_hgrn, v_w_out, v_norm_post):
    nseq, S, _ = x.shape
    T = nseq * S
    x2 = x.reshape(T, D_MODEL)
    tgt2 = loss_target.reshape(T, D_MODEL)

    h = _prenorm(x2, norm_pre)
    proj, wt_in = _gather_inproj(h, w_in[0].T.astype(BF16))
    shards = [w_out[0].astype(BF16), w_branch_attn[0].T.astype(BF16), w_branch_hgrn[0].T.astype(BF16)]

    (dproj, dout, p_in, r_out, r_a, r_h, d_gpost, d_lbl, d_gain, d_rel_bias, d_sinks, loss_cols) = _local_step(
        nseq, S, x2, tgt2, proj, h, rel_bias, attn_sinks, lb_logits, hgrn_norm, norm_post, shards)

    in_sums = _chip_sums([p_in], (192,), ("dw_in",), "sibling_exchange_w_in")
    grad_x2, d_gpre, r_in = _dh_prenorm_bwd(dproj, wt_in, x2, dout, norm_pre, in_sums)
    g_in_t = _sum_slots(r_in, 192, "sum_dw_in")
    g_out = _sum_slots(r_out, 128, "sum_dw_out")
    g_a = _sum_slots(r_a, 128, "sum_dw_branch_attn").T
    g_h = _sum_slots(r_h, 128, "sum_dw_branch_hgrn").T

    loss_part = 0.5 / D_MODEL * jnp.sum(loss_cols)
    packed = _pack_small(d_gpre, d_gpost, d_lbl, d_gain, d_rel_bias, d_sinks, extra=loss_part)
    total = _all_reduce_small(packed)
    loss = total[3, 1024 + 512 + 16]
    sm_w = _pack_small(norm_pre, norm_post, lb_logits, hgrn_norm, rel_bias, attn_sinks)
    sm_m = _pack_small(m_norm_pre, m_norm_post, m_lb_logits, m_hgrn_norm, m_rel_bias, m_attn_sinks)
    sm_v = _pack_small(v_norm_pre, v_norm_post, v_lb_logits, v_hgrn_norm, v_rel_bias, v_attn_sinks)
    sm_d, sm_nm, sm_nv = _adamw(sm_w, total, sm_m, sm_v, 8, "adamw_small")

    t = lambda a: jnp.swapaxes(a, 1, 2)
    d_in, nm_in, nv_in = map(t, _adamw(t(w_in), g_in_t, t(m_w_in), t(v_w_in), 192, "adamw_w_in"))
    d_out, nm_out, nv_out = _adamw(w_out, g_out, m_w_out, v_w_out, 128, "adamw_w_out")
    d_a, nm_a, nv_a = _adamw(w_branch_attn, g_a, m_w_branch_attn, v_w_branch_attn, 256, "adamw_w_branch_attn")
    d_h, nm_h, nv_h = _adamw(w_branch_hgrn, g_h, m_w_branch_hgrn, v_w_branch_hgrn, 256, "adamw_w_branch_hgrn")

    def group(small, big_in, big_a, big_h, big_out):
        npre, rb, sk, lbl, hn, npost = _unpack_small(small)
        return (npre, big_in, rb, sk, lbl, hn, big_a, big_h, big_out, npost)

    return (loss, grad_x2.reshape(nseq, S, D_MODEL),
            *group(total, t(g_in_t[None]), g_a[None], g_h[None], g_out[None]),
            *group(sm_d, d_in, d_a, d_h, d_out),
            *group(sm_nm, nm_in, nm_a, nm_h, nm_out),
            *group(sm_nv, nv_in, nv_a, nv_h, nv_out))
```

```python
import functools
import math

import numpy as np
import jax
import jax.numpy as jnp
from jax import lax
from jax.experimental import pallas as pl
from jax.experimental.pallas import tpu as pltpu

F32 = jnp.float32
BF16 = jnp.bfloat16

D_MODEL = 2048
ATTN_HEADS = 16
ATTN_HEAD_DIM = 64
ATTN_GROUP = 4
ATTN_BLOCK = 128
HGRN_HEADS = 8
HGRN_DIM = 128
HGRN_CHUNK = 64
HGRN_SUB = 16
HGRN_PAR = 8
HGRN_COLS = 512
REL_BUCKETS = 32
REL_MAX_DIST = 128
NORM_EPS = 1e-6
C_AQ, C_AK, C_AV, C_AG = 0, 1024, 1280, 1536
C_HQ, C_HF, C_HI, C_HG = 2560, 3584, 4608, 5632
C_GA, C_GH = 6656, 8704
IN_WIDTH = 10752
N_DEV = 8
assert all(c0 % HGRN_COLS == 0 for c0 in (C_HQ, C_HF, C_HI, C_HG)) and (HGRN_PAR * HGRN_DIM) % HGRN_COLS == 0

ADAM_LR = 0.001
ADAM_B1 = 0.9
ADAM_B2 = 0.999
ADAM_EPS = 1e-08
ADAM_WD = 0.01
ADAM_STEP = 10

VMEM_LIMIT_V7X = 56 * 1024 * 1024
NEG_BIG = -1e30

NT = (((1,), (1,)), ((), ()))
TN = (((0,), (0,)), ((), ()))
NN = (((1,), (0,)), ((), ()))


def _dot(a, b, dims=NN):
    return lax.dot_general(a, b, dims, preferred_element_type=F32)


def _params(sem=None):
    return pltpu.CompilerParams(dimension_semantics=sem, vmem_limit_bytes=VMEM_LIMIT_V7X)


def _sigmoid(x):
    return 1.0 / (1.0 + jnp.exp(-x))


def _t5_bucket_table():
    qi = np.arange(ATTN_BLOCK)[:, None]
    si = np.arange(2 * ATTN_BLOCK)[None, :]
    dist = qi + ATTN_BLOCK - si
    max_exact = REL_BUCKETS // 2
    d = np.maximum(dist, 0)
    df = np.maximum(d, 1).astype(np.float32)
    large = max_exact + (np.log(df / np.float32(max_exact)).astype(np.float32)
                         / np.float32(math.log(REL_MAX_DIST / max_exact))
                         * np.float32(REL_BUCKETS - max_exact)).astype(np.int32)
    large = np.minimum(large, REL_BUCKETS - 1)
    return np.where(d < max_exact, d, large).astype(np.int32)


def _place():
    return lax.axis_index("x"), lax.axis_index("y"), lax.axis_index("c")


class _GatherOps:
    def __init__(self, ins, outs, send_sems, recv_sems, local_sems):
        self.ins, self.outs = ins, outs
        self.send_sems, self.recv_sems, self.local_sems = send_sems, recv_sems, local_sems
        x, y, c = _place()
        self.c = c
        self.me, self.sibling = (x, y, c), (x, y, 1 - c)
        self.chips = [(1 - x, y), (x, 1 - y), (1 - x, 1 - y)]

    def _rows(self, a, dev):
        r = self.ins[a].shape[0]
        return self.outs[a].at[pl.ds((4 * dev[0] + 2 * dev[1] + dev[2]) * r, r), :]

    def _copy(self, a, k, block, to, src=None):
        return pltpu.make_async_remote_copy(
            src_ref=self._rows(a, block) if src is None else src, dst_ref=self._rows(a, block),
            send_sem=self.send_sems.at[a, k], recv_sem=self.recv_sems.at[a, k],
            device_id=to, device_id_type=pl.DeviceIdType.MESH)

    def local(self, a):
        return pltpu.make_async_copy(self.ins[a], self._rows(a, self.me), self.local_sems.at[a])

    def to_sibling(self, a):
        return self._copy(a, 0, self.me, self.sibling, src=self.ins[a])

    def to_chip(self, a, q):
        return self._copy(a, 1 + q, self.me, (*self.chips[q], self.c), src=self.ins[a])

    def forward(self, a, q):
        return self._copy(a, 4 + q, (*self.chips[q], self.c), self.sibling)

    def from_sibling(self, a):
        return self._copy(a, 0, self.sibling, self.me)

    def from_chip(self, a, q):
        return self._copy(a, 1 + q, (*self.chips[q], self.c), self.me)

    def forwarded(self, a, q):
        return self._copy(a, 4 + q, (*self.chips[q], 1 - self.c), self.me)

    def sends(self, a):
        return [self.to_sibling(a)] + [self.to_chip(a, q) for q in range(3)] + [self.forward(a, q) for q in range(3)]

    def start(self):
        for a in range(len(self.ins)):
            self.local(a).start()
            self.to_sibling(a).start()
            for q in range(3):
                self.to_chip(a, q).start()

    def finish(self):
        for a in range(len(self.ins)):
            self.local(a).wait()
            self.from_sibling(a).wait_recv()
            for q in range(3):
                self.from_chip(a, q).wait_recv()
                self.forward(a, q).start()
        for a in range(len(self.ins)):
            for q in range(3):
                self.forwarded(a, q).wait_recv()
            for cp in self.sends(a):
                cp.wait_send()

    @staticmethod
    def scratch(n):
        return [pltpu.SemaphoreType.DMA((n, _GATHER_SEMS)), pltpu.SemaphoreType.DMA((n, _GATHER_SEMS)),
                pltpu.SemaphoreType.DMA((n,))]


class _ChipExchange:
    def __init__(self, ins, outs, send_sems, recv_sems, local_sems):
        self.ins, self.outs = ins, outs
        self.send_sems, self.recv_sems, self.local_sems = send_sems, recv_sems, local_sems
        self.x, self.y, self.c = _place()
        self.my_chip = 2 * self.x + self.y

    def _peer(self, q):
        return (self.x ^ (q >> 1), self.y ^ (q & 1))

    def _copy(self, a, q, src_block, dst_slot):
        px, py = self._peer(q)
        return pltpu.make_async_remote_copy(
            src_ref=self.ins[a].at[src_block], dst_ref=self.outs[a].at[dst_slot],
            send_sem=self.send_sems.at[a, q - 1], recv_sem=self.recv_sems.at[a, q - 1],
            device_id=(px, py, self.c), device_id_type=pl.DeviceIdType.MESH)

    def _send(self, a, q):
        px, py = self._peer(q)
        return self._copy(a, q, 2 * px + py, self.my_chip)

    def _arrival(self, a, q):
        px, py = self._peer(q)
        return self._copy(a, q, self.my_chip, 2 * px + py)

    def _local(self, a):
        return pltpu.make_async_copy(self.ins[a].at[self.my_chip], self.outs[a].at[self.my_chip],
                                     self.local_sems.at[a])

    def start(self):
        for a in range(len(self.ins)):
            self._local(a).start()
            for q in range(1, 4):
                self._send(a, q).start()

    def wait(self):
        for a in range(len(self.ins)):
            for q in range(1, 4):
                self._arrival(a, q).wait_recv()
        for a in range(len(self.ins)):
            for q in range(1, 4):
                self._send(a, q).wait_send()
            self._local(a).wait()

    @staticmethod
    def scratch(n):
        return [pltpu.SemaphoreType.DMA((n, 3)), pltpu.SemaphoreType.DMA((n, 3)), pltpu.SemaphoreType.DMA((n,))]


_GATHER_SEMS = 7
LOCAL_DMA_THREAD = 1

INPROJ_TILE = 896


def _prenorm(x2, gpre):
    T = x2.shape[0]
    tm = min(512, T)

    def body(x_ref, g_ref, h_ref):
        x = x_ref[...]
        r = lax.rsqrt(jnp.mean(x * x, axis=-1, keepdims=True) + NORM_EPS)
        h_ref[...] = (x * r * g_ref[...]).astype(BF16)

    return pl.pallas_call(
        body, name="prenorm",
        grid=(T // tm,),
        in_specs=[pl.BlockSpec((tm, D_MODEL), lambda i: (i, 0)), pl.BlockSpec((1, D_MODEL), lambda i: (0, 0))],
        out_specs=pl.BlockSpec((tm, D_MODEL), lambda i: (i, 0)),
        out_shape=jax.ShapeDtypeStruct((T, D_MODEL), BF16),
        compiler_params=_params(("arbitrary",)),
    )(x2, gpre)


def _gather_inproj(h, wt_shard):
    T = h.shape[0]
    tm = min(1024, T)
    nm = T // tm
    tn = INPROJ_TILE
    ntile = IN_WIDTH // tn
    nstep = ntile * nm

    def body(h_hbm, w_in, proj_hbm, w_out, hbuf, wtile, obuf, own_buf,
             send_sems, recv_sems, local_sems, h_sem, w_sems, o_sems):
        j, i = pl.program_id(0), pl.program_id(1)
        step = j * nm + i
        slot = step % 2
        ops = _GatherOps([w_in], [w_out], send_sems, recv_sems, local_sems)
        x, y, _ = _place()

        def tile_of(jj):
            k = jj // 3
            return 3 * ((2 * x + y) ^ (((k & 1) << 1) | (k >> 1))) + jj % 3

        tile = tile_of(j)

        def h_load():
            return pltpu.make_async_copy(h_hbm, hbuf, h_sem)

        own_in = pltpu.make_async_copy(w_in, own_buf, local_sems.at[0])
        own_out = pltpu.make_async_copy(own_buf, ops._rows(0, ops.me), local_sems.at[1])

        def store(s, rows, cols):
            return pltpu.make_async_copy(obuf.at[s], proj_hbm.at[rows, cols], o_sems.at[s])

        def window(ii, t):
            return pl.ds(pl.multiple_of(ii * tm, tm), tm), pl.ds(pl.multiple_of(t * tn, tn), tn)

        @pl.when(step == 0)
        def _():
            h_load().start()
            own_in.start()
            ops.to_sibling(0).start()
            ops.to_chip(0, 0).start()
            ops.to_chip(0, 1).start()
            h_load().wait()

        for kk in range(4):
            @pl.when((j == 3 * kk) & (i == 0))
            def _(kk=kk):
                if kk == 0:
                    own_in.wait()
                    own_out.start()
                    own_out.wait()
                    ops.from_sibling(0).wait_recv()
                else:
                    q = kk - 1
                    ops.from_chip(0, q).wait_recv()
                    ops.forward(0, q).start()
                    if q == 0:
                        ops.to_chip(0, 2).start()
                    ops.forwarded(0, q).wait_recv()

        wslot = j % 2

        def fetch(jj, sw):
            rows = pl.ds(pl.multiple_of(tile_of(jj) * tn, tn), tn)
            return pltpu.make_async_copy(w_out.at[rows, :], wtile.at[sw], w_sems.at[sw])

        @pl.when((i == 0) & (j % 3 == 0))
        def _():
            fetch(j, wslot).start(LOCAL_DMA_THREAD)

        @pl.when(i == 0)
        def _():
            fetch(j, wslot).wait()

        @pl.when((i == 0) & (j % 3 != 2))
        def _():
            fetch(j + 1, 1 - wslot).start(LOCAL_DMA_THREAD)

        @pl.when(step >= 2)
        def _():
            store(slot, *window(0, 0)).wait()

        hv = hbuf[pl.ds(pl.multiple_of(i * tm, tm), tm), :]
        obuf[slot] = _dot(hv, wtile[wslot], NT).astype(BF16)
        store(slot, *window(i, tile)).start(LOCAL_DMA_THREAD)

        @pl.when(step == nstep - 1)
        def _():
            for s in range(min(2, nstep)):
                store(s, *window(0, 0)).wait()
            for cp in ops.sends(0):
                cp.wait_send()

    anyspec = lambda: pl.BlockSpec(memory_space=pl.ANY)
    return pl.pallas_call(
        body, name="gather_inproj",
        grid=(ntile, nm),
        in_specs=[anyspec(), anyspec()],
        out_specs=[anyspec(), anyspec()],
        out_shape=[jax.ShapeDtypeStruct((T, IN_WIDTH), BF16),
                   jax.ShapeDtypeStruct((N_DEV * wt_shard.shape[0], D_MODEL), BF16)],
        scratch_shapes=[pltpu.VMEM((T, D_MODEL), BF16), pltpu.VMEM((2, tn, D_MODEL), BF16),
                        pltpu.VMEM((2, tm, tn), BF16), pltpu.VMEM(wt_shard.shape, BF16),
                        pltpu.SemaphoreType.DMA((1, _GATHER_SEMS)), pltpu.SemaphoreType.DMA((1, _GATHER_SEMS)),
                        pltpu.SemaphoreType.DMA((2,)), pltpu.SemaphoreType.DMA, pltpu.SemaphoreType.DMA((2,)),
                        pltpu.SemaphoreType.DMA((2,))],
        compiler_params=_params(("arbitrary", "arbitrary")),
    )(h, wt_shard)


def _bias_table(rel_bias, bucket):
    def body(rb_ref, bk_ref, out_ref):
        h = pl.program_id(0)
        bk = bk_ref[...]
        acc = jnp.zeros(bk.shape, F32)
        for b in range(REL_BUCKETS):
            acc = jnp.where(bk == b, rb_ref[b, h], acc)
        out_ref[...] = acc

    return pl.pallas_call(
        body, name="bias_table",
        grid=(ATTN_HEADS,),
        in_specs=[pl.BlockSpec(memory_space=pltpu.SMEM),
                  pl.BlockSpec((ATTN_BLOCK, 2 * ATTN_BLOCK), lambda h: (0, 0))],
        out_specs=pl.BlockSpec((None, ATTN_BLOCK, 2 * ATTN_BLOCK), lambda h: (h, 0, 0)),
        out_shape=jax.ShapeDtypeStruct((ATTN_HEADS, ATTN_BLOCK, 2 * ATTN_BLOCK), F32),
        compiler_params=_params(("arbitrary",)),
    )(rel_bias, bucket)


def _bias_table_bwd(dbias, bucket):
    def body(db_ref, bk_ref, out_ref):
        h = pl.program_id(0)
        bk = bk_ref[...]
        db = db_ref[...]
        for b in range(REL_BUCKETS):
            out_ref[b, h] = jnp.sum(jnp.where(bk == b, db, 0.0))

    return pl.pallas_call(
        body, name="bias_table_bwd",
        grid=(ATTN_HEADS,),
        in_specs=[pl.BlockSpec((None, ATTN_BLOCK, 2 * ATTN_BLOCK), lambda h: (h, 0, 0)),
                  pl.BlockSpec((ATTN_BLOCK, 2 * ATTN_BLOCK), lambda h: (0, 0))],
        out_specs=pl.BlockSpec(memory_space=pltpu.SMEM),
        out_shape=jax.ShapeDtypeStruct((REL_BUCKETS, ATTN_HEADS), F32),
        compiler_params=_params(("arbitrary",)),
    )(dbias, bucket)


def _attn_common(qkvg, kv_prev, blk):
    lane = lax.broadcasted_iota(jnp.int32, (1, 128), 1)
    half = (lane < ATTN_HEAD_DIM, lane >= ATTN_HEAD_DIM)
    kv_cur = qkvg[:, C_AK:C_AG]
    win = jnp.concatenate([kv_prev, kv_cur], axis=0)
    k_slab, v_slab = [], []
    for r in range(2):
        ks = win[:, 128 * r:128 * r + 128]
        vs = win[:, 256 + 128 * r:256 + 128 * r + 128]
        k_slab.append((ks, pltpu.roll(ks, ATTN_HEAD_DIM, 1)))
        v_slab.append((vs, pltpu.roll(vs, ATTN_HEAD_DIM, 1)))
    rows4 = ATTN_GROUP * ATTN_BLOCK
    qi = lax.broadcasted_iota(jnp.int32, (rows4, 2 * ATTN_BLOCK), 0) & (ATTN_BLOCK - 1)
    si = lax.broadcasted_iota(jnp.int32, (rows4, 2 * ATTN_BLOCK), 1)
    valid = (si > qi) & (si <= qi + ATTN_BLOCK) & ((si >= ATTN_BLOCK) | (blk > 0))
    return half, k_slab, v_slab, valid


def _stack_heads(half, slab0, slab1):
    return jnp.concatenate([jnp.where(half[0], slab0, 0.0), jnp.where(half[1], slab0, 0.0),
                            jnp.where(half[0], slab1, 0.0), jnp.where(half[1], slab1, 0.0)], axis=0)


def _unstack_heads(half, x4):
    B = ATTN_BLOCK
    return (jnp.where(half[0], x4[0:B], x4[B:2 * B]), jnp.where(half[0], x4[2 * B:3 * B], x4[3 * B:4 * B]))


def _attn_group(j, qkvg, half, k_slab, v_slab, valid, bias_ref, sinks_ref):
    r, aj = j // 2, j % 2
    pick = (lambda a, b: jnp.where(half[0], a, b)) if aj == 0 else (lambda a, b: jnp.where(half[0], b, a))
    kb = pick(*k_slab[r]).astype(BF16)
    vb = pick(*v_slab[r]).astype(BF16)
    q4 = _stack_heads(half, qkvg[:, 256 * j:256 * j + 128], qkvg[:, 256 * j + 128:256 * j + 256]).astype(BF16)
    bias4 = bias_ref[ATTN_GROUP * j:ATTN_GROUP * (j + 1)].reshape(valid.shape)
    yield
    s = _dot(q4, kb, NT) * (ATTN_HEAD_DIM ** -0.5) + bias4
    s = jnp.where(valid, s, NEG_BIG)
    rowblk = lax.broadcasted_iota(jnp.int32, (valid.shape[0], 1), 0) // ATTN_BLOCK
    sink = jnp.full((valid.shape[0], 1), sinks_ref[0, ATTN_GROUP * j], F32)
    for b in range(1, ATTN_GROUP):
        sink = jnp.where(rowblk == b, sinks_ref[0, ATTN_GROUP * j + b], sink)
    m = jnp.maximum(jnp.max(s, axis=-1, keepdims=True), sink)
    e = jnp.exp(s - m)
    es = jnp.exp(sink - m)
    inv = 1.0 / (jnp.sum(e, axis=-1, keepdims=True) + es)
    pn = e * inv
    yield
    o4 = _dot(pn.astype(BF16), vb)
    return dict(r=r, aj=aj, kb=kb, vb=vb, q4=q4, pn=pn, psink=es * inv, o4=o4)


def _attn_specs(nb):
    row = lambda b, i: b * nb + i
    return [
        pl.BlockSpec((ATTN_BLOCK, C_HQ), lambda b, i: (row(b, i), 0)),
        pl.BlockSpec((ATTN_BLOCK, 512), lambda b, i: (row(b, jnp.maximum(i - 1, 0)), 2)),
        pl.BlockSpec((ATTN_HEADS, ATTN_BLOCK, 2 * ATTN_BLOCK), lambda b, i: (0, 0, 0)),
        pl.BlockSpec(memory_space=pltpu.SMEM),
    ]


def _attn_fwd(proj, bias, sinks, nseq, nb, shards):
    T = proj.shape[0]
    n = len(shards)

    def body(qkvg_ref, kvp_ref, bias_ref, sinks_ref, *refs):
        ins, ya_ref, outs = refs[:n], refs[n], refs[n + 1:2 * n + 1]
        ops = _GatherOps(ins, outs, *refs[2 * n + 1:])

        @pl.when((pl.program_id(0) == 0) & (pl.program_id(1) == 0))
        def _():
            ops.start()

        qkvg = qkvg_ref[...].astype(F32)
        half, k_slab, v_slab, valid = _attn_common(qkvg, kvp_ref[...].astype(F32), pl.program_id(1))
        groups = _interleave([_attn_group(j, qkvg, half, k_slab, v_slab, valid, bias_ref, sinks_ref)
                              for j in range(ATTN_HEADS // ATTN_GROUP)])
        slabs = []
        for grp in groups:
            slabs += _unstack_heads(half, grp["o4"])
        o_all = jnp.concatenate(slabs, axis=1)
        g = qkvg[:, C_AG:C_HQ]
        ya_ref[...] = (o_all * (g * _sigmoid(g))).astype(BF16)

        @pl.when((pl.program_id(0) == nseq - 1) & (pl.program_id(1) == nb - 1))
        def _():
            ops.finish()

    anyspec = lambda: pl.BlockSpec(memory_space=pl.ANY)
    return pl.pallas_call(
        body, name="attn_fwd",
        grid=(nseq, nb),
        in_specs=_attn_specs(nb) + [anyspec() for _ in shards],
        out_specs=[pl.BlockSpec((ATTN_BLOCK, 1024), lambda b, i: (b * nb + i, 0))] + [anyspec() for _ in shards],
        out_shape=[jax.ShapeDtypeStruct((T, 1024), BF16)]
                  + [jax.ShapeDtypeStruct((N_DEV * s.shape[0], s.shape[1]), s.dtype) for s in shards],
        scratch_shapes=_GatherOps.scratch(n),
        compiler_params=_params(("arbitrary", "arbitrary")),
    )(proj, proj, bias, sinks, *shards)


def _attn_bwd(proj, bias, sinks, d_ya, nseq, nb):
    T = proj.shape[0]
    S = nb * ATTN_BLOCK
    scale = ATTN_HEAD_DIM ** -0.5

    def body(qkvg_ref, kvp_ref, bias_ref, sinks_ref, dya_ref, dq_ref, dkv_ref, dg_ref, dbias_ref, dsinks_ref):
        b, i = pl.program_id(0), pl.program_id(1)
        first = (b == 0) & (i == 0)

        @pl.when(first)
        def _():
            dbias_ref[...] = jnp.zeros(dbias_ref.shape, F32)
            for h in range(ATTN_HEADS):
                dsinks_ref[0, h] = 0.0

        qkvg = qkvg_ref[...].astype(F32)
        half, k_slab, v_slab, valid = _attn_common(qkvg, kvp_ref[...].astype(F32), i)
        g = qkvg[:, C_AG:C_HQ]
        sg = _sigmoid(g)
        silu_g = g * sg
        dya = dya_ref[...].astype(F32)
        do_all = dya * silu_g
        dq_slabs, o_slabs = [], []
        dk_slab, dv_slab = [None, None], [None, None]
        B = ATTN_BLOCK

        def fold(x, aj):
            return jnp.where(half[aj], x + pltpu.roll(x, ATTN_HEAD_DIM, 1), 0.0)

        def group_bwd(j):
            grp = yield from _attn_group(j, qkvg, half, k_slab, v_slab, valid, bias_ref, sinks_ref)
            pn = grp["pn"]
            do4 = _stack_heads(half, do_all[:, 256 * j:256 * j + 128], do_all[:, 256 * j + 128:256 * j + 256])
            do4b = do4.astype(BF16)
            yield
            dp = _dot(do4b, grp["vb"], NT)
            delta = jnp.sum(do4 * grp["o4"], axis=-1, keepdims=True)
            ds = pn * (dp - delta)
            sink_term = grp["psink"] * delta
            for b4 in range(ATTN_GROUP):
                dsinks_ref[0, ATTN_GROUP * j + b4] += -jnp.sum(sink_term[B * b4:B * (b4 + 1)])
            dbias_ref[ATTN_GROUP * j:ATTN_GROUP * (j + 1)] += ds.reshape(ATTN_GROUP, B, 2 * B)
            dsb = ds.astype(BF16)
            yield
            dq4 = _dot(dsb, grp["kb"]) * scale
            dk_j = fold(_dot(dsb, grp["q4"], TN) * scale, grp["aj"])
            dv_j = fold(_dot(pn.astype(BF16), do4b, TN), grp["aj"])
            return dict(r=grp["r"], dq=_unstack_heads(half, dq4), o=_unstack_heads(half, grp["o4"]), dk=dk_j, dv=dv_j)

        for res in _interleave([group_bwd(j) for j in range(ATTN_HEADS // ATTN_GROUP)]):
            r = res["r"]
            dq_slabs += res["dq"]
            o_slabs += res["o"]
            dk_slab[r] = res["dk"] if dk_slab[r] is None else dk_slab[r] + res["dk"]
            dv_slab[r] = res["dv"] if dv_slab[r] is None else dv_slab[r] + res["dv"]

        dq_ref[...] = jnp.concatenate(dq_slabs, axis=1).astype(BF16)
        o_all = jnp.concatenate(o_slabs, axis=1)
        dg_ref[...] = (dya * o_all * (sg * (1.0 + g * (1.0 - sg)))).astype(BF16)

        dkv = jnp.concatenate(dk_slab + dv_slab, axis=1)
        cur = pl.multiple_of(i * ATTN_BLOCK, ATTN_BLOCK)
        dkv_ref[pl.ds(cur, ATTN_BLOCK), :] = dkv[ATTN_BLOCK:].astype(BF16)

        @pl.when(i > 0)
        def _():
            prev = pl.multiple_of((i - 1) * ATTN_BLOCK, ATTN_BLOCK)
            old = dkv_ref[pl.ds(prev, ATTN_BLOCK), :].astype(F32)
            dkv_ref[pl.ds(prev, ATTN_BLOCK), :] = (old + dkv[:ATTN_BLOCK]).astype(BF16)

    row_spec = lambda w: pl.BlockSpec((ATTN_BLOCK, w), lambda b, i: (b * nb + i, 0))
    return pl.pallas_call(
        body, name="attn_bwd",
        grid=(nseq, nb),
        in_specs=_attn_specs(nb) + [row_spec(1024)],
        out_specs=[row_spec(1024),
                   pl.BlockSpec((S, 512), lambda b, i: (b, 0)),
                   row_spec(1024),
                   pl.BlockSpec((ATTN_HEADS, ATTN_BLOCK, 2 * ATTN_BLOCK), lambda b, i: (0, 0, 0)),
                   pl.BlockSpec(memory_space=pltpu.SMEM)],
        out_shape=[jax.ShapeDtypeStruct((T, 1024), BF16),
                   jax.ShapeDtypeStruct((T, 512), BF16),
                   jax.ShapeDtypeStruct((T, 1024), BF16),
                   jax.ShapeDtypeStruct((ATTN_HEADS, ATTN_BLOCK, 2 * ATTN_BLOCK), F32),
                   jax.ShapeDtypeStruct((1, ATTN_HEADS), F32)],
        compiler_params=_params(("arbitrary", "arbitrary")),
    )(proj, proj, bias, sinks, d_ya)


def _split3(x):
    hi = x.astype(BF16)
    r1 = x - hi.astype(F32)
    mid = r1.astype(BF16)
    lo = (r1 - mid.astype(F32)).astype(BF16)
    return jnp.concatenate([hi, mid, lo], axis=1)


def _tri_sum(tri, x):
    y = _dot(tri, _split3(x))
    return y[:, :128] + y[:, 128:256] + y[:, 256:]


def _interleave(stages):
    results = [None] * len(stages)
    live = list(range(len(stages)))
    while live:
        still = []
        for idx in live:
            try:
                next(stages[idx])
                still.append(idx)
            except StopIteration as done:
                results[idx] = done.value
        live = still
    return results


def _hgrn_chunk(hq, hf, hi, lb):
    C = HGRN_CHUNK
    t = lax.broadcasted_iota(jnp.int32, (C, C), 0)
    s = lax.broadcasted_iota(jnp.int32, (C, C), 1)
    causal = s <= t
    sf = _sigmoid(hf)
    f = lb + (1.0 - lb) * sf
    lf = jnp.log(f)
    yield
    G = _tri_sum(causal.astype(BF16), lf)
    sq = _sigmoid(hq)
    qs = hq * sq
    k = 1.0 - f
    rowblk = lax.broadcasted_iota(jnp.int32, (C, 1), 0) // HGRN_SUB
    qt, kt, eq, ek = [], [], [], []
    for i in range(C // HGRN_SUB):
        lo = HGRN_SUB * i
        ref = G[lo + HGRN_SUB // 2:lo + HGRN_SUB // 2 + 1, :]
        eq_i = jnp.exp(G[lo:lo + HGRN_SUB] - ref)
        ek_i = jnp.exp(jnp.where(rowblk <= i, ref - G, 0.0))
        eq.append(eq_i)
        ek.append(ek_i)
        qt.append((qs[lo:lo + HGRN_SUB] * eq_i).astype(BF16))
        kt.append((k * ek_i).astype(BF16))
    yield
    A = jnp.concatenate([_dot(qt[i], kt[i], NT) for i in range(C // HGRN_SUB)], axis=0)
    A = jnp.where(causal, A, 0.0)
    glast = G[C - 1:C, :]
    eG = jnp.exp(G)
    edec = jnp.exp(glast - G)
    return dict(causal=causal, sf=sf, f=f, G=G, sq=sq, qs=qs, k=k, qt=qt, kt=kt, eq=eq, ek=ek, A=A,
                glast=glast, eG=eG, edec=edec, qhat=qs * eG, kdec=k * edec, v=hi)


def _hgrn_specs(nseq, ng, rows, reverse):
    W = HGRN_PAR
    nblk = W * HGRN_DIM // HGRN_COLS
    gi = (lambda g: ng - 1 - g) if reverse else (lambda g: g)

    def cols(c0):
        return [pl.BlockSpec((rows, HGRN_COLS),
                             lambda h, b, g, q=q: (b * ng + gi(g), c0 // HGRN_COLS + h * nblk + q)) for q in range(nblk)]

    return gi, (cols(C_HQ) + cols(C_HF) + cols(C_HI) + cols(C_HG)
                + [pl.BlockSpec((2, W * HGRN_DIM), lambda h, b, g: (0, h)),
                   pl.BlockSpec((W, 1, HGRN_DIM), lambda h, b, g: (h, 0, 0))])


def _hgrn_operands(refs):
    nblk = HGRN_PAR * HGRN_DIM // HGRN_COLS
    per = HGRN_COLS // HGRN_DIM

    def reader(group):
        def read(rs, w):
            lo = HGRN_DIM * (w % per)
            return group[w // per][rs, lo:lo + HGRN_DIM].astype(F32)
        return read

    readers = [reader(refs[nblk * a:nblk * (a + 1)]) for a in range(4)]
    return readers, refs[4 * nblk], refs[4 * nblk + 1], refs[4 * nblk + 2:]


def _hgrn_fwd(proj, lb_logits, gain3, nseq, S, shards):
    T = proj.shape[0]
    W = HGRN_PAR
    nc = S // HGRN_CHUNK
    cg = min(8, nc)
    ng = nc // cg
    rows = cg * HGRN_CHUNK
    n = len(shards)
    nh = HGRN_HEADS // W

    def body(*all_refs):
        (hq, hf, hi, hg), lbl_ref, gain_ref, refs = _hgrn_operands(all_refs)
        ins = refs[:n]
        o_ref, yh_ref, st_ref = refs[n:n + 3]
        outs = refs[n + 3:2 * n + 3]
        state, send_sems, recv_sems, local_sems = refs[2 * n + 3:]
        step = (pl.program_id(0), pl.program_id(1), pl.program_id(2))
        ops = _GatherOps(ins, outs, send_sems, recv_sems, local_sems)

        @pl.when((step[0] == 0) & (step[1] == 0) & (step[2] == 0))
        def _():
            ops.start()

        @pl.when(pl.program_id(2) == 0)
        def _():
            state[...] = jnp.zeros(state.shape, F32)

        lb_all = _sigmoid(lbl_ref[0:1, :] - lbl_ref[1:2, :])

        def head(w, c):
            rs = pl.ds(pl.multiple_of(c * HGRN_CHUNK, HGRN_CHUNK), HGRN_CHUNK)
            ls = slice(HGRN_DIM * w, HGRN_DIM * (w + 1))
            ch = yield from _hgrn_chunk(hq(rs, w), hf(rs, w), hi(rs, w), lb_all[:, ls])
            st = state[w]
            st_ref[w, c] = st
            vb = ch["v"].astype(BF16)
            yield
            o = _dot(ch["qhat"].astype(BF16), st.astype(BF16), NT) + _dot(ch["A"].astype(BF16), vb)
            state[w] = st * jnp.exp(ch["glast"]) + _dot(vb, ch["kdec"].astype(BF16), TN)
            o_ref[rs, ls] = o
            r = lax.rsqrt(jnp.mean(o * o, axis=-1, keepdims=True) + NORM_EPS)
            gate = hg(rs, w)
            yh_ref[rs, ls] = (o * r * gain_ref[w] * (gate * _sigmoid(gate))).astype(BF16)

        def chunk(c, carry):
            _interleave([head(w, c) for w in range(W)])
            return carry

        lax.fori_loop(0, cg, chunk, 0)

        @pl.when((step[0] == nh - 1) & (step[1] == nseq - 1) & (step[2] == ng - 1))
        def _():
            ops.finish()

    _, in_specs = _hgrn_specs(nseq, ng, rows, False)
    out_row = lambda: pl.BlockSpec((rows, W * HGRN_DIM), lambda h, b, g: (b * ng + g, h))
    anyspec = lambda: pl.BlockSpec(memory_space=pl.ANY)
    return pl.pallas_call(
        body, name="hgrn_fwd",
        grid=(nh, nseq, ng),
        in_specs=in_specs + [anyspec() for _ in shards],
        out_specs=[out_row(), out_row(),
                   pl.BlockSpec((None, W, cg, HGRN_DIM, HGRN_DIM), lambda h, b, g: (b, h, g, 0, 0))]
                  + [anyspec() for _ in shards],
        out_shape=[jax.ShapeDtypeStruct((T, 1024), F32),
                   jax.ShapeDtypeStruct((T, 1024), BF16),
                   jax.ShapeDtypeStruct((nseq, HGRN_HEADS, nc, HGRN_DIM, HGRN_DIM), F32)]
                  + [jax.ShapeDtypeStruct((N_DEV * s.shape[0], s.shape[1]), s.dtype) for s in shards],
        scratch_shapes=[pltpu.VMEM((W, HGRN_DIM, HGRN_DIM), F32)] + _GatherOps.scratch(n),
        compiler_params=_params(("arbitrary", "arbitrary", "arbitrary")),
    )(*([proj] * (4 * W * HGRN_DIM // HGRN_COLS)), lb_logits, gain3, *shards)


def _hgrn_bwd(proj, lb_logits, gain3, o, d_yh, states, nseq, S, chip_sums):
    T = proj.shape[0]
    W = HGRN_PAR
    n = len(chip_sums)
    nh = HGRN_HEADS // W
    assert nh == 1
    nc = S // HGRN_CHUNK
    cg = min(8, nc)
    ng = nc // cg
    rows = cg * HGRN_CHUNK
    C = HGRN_CHUNK
    nsub = C // HGRN_SUB

    def body(*all_refs):
        (hq_of, hf_of, hi_of, hg_of), lbl_ref, gain_ref, refs = _hgrn_operands(all_refs)
        o_ref, dyh_ref, st_ref = refs[:3]
        sums_in = refs[3:3 + n]
        dh4_ref, dgain_ref, dlbl_ref = refs[3 + n:6 + n]
        sums_out = refs[6 + n:6 + 2 * n]
        dstate, dlb_acc, send_sems, recv_sems, local_sems = refs[6 + 2 * n:]
        h, b, g = pl.program_id(0), pl.program_id(1), pl.program_id(2)
        exchange = _ChipExchange(sums_in, sums_out, send_sems, recv_sems, local_sems)

        @pl.when((h == 0) & (b == 0) & (g == 0))
        def _():
            exchange.start()

        @pl.when(g == 0)
        def _():
            dstate[...] = jnp.zeros(dstate.shape, F32)

        @pl.when((b == 0) & (g == 0))
        def _():
            dgain_ref[...] = jnp.zeros(dgain_ref.shape, F32)
            dlb_acc[...] = jnp.zeros(dlb_acc.shape, F32)

        lb_all = _sigmoid(lbl_ref[0:1, :] - lbl_ref[1:2, :])

        anti = (lax.broadcasted_iota(jnp.int32, (C, C), 1) >= lax.broadcasted_iota(jnp.int32, (C, C), 0)).astype(BF16)
        last_row = lax.broadcasted_iota(jnp.int32, (C, 1), 0) == C - 1

        def head_load(w, c):
            rs = pl.ds(pl.multiple_of(c * C, C), C)
            ls = slice(HGRN_DIM * w, HGRN_DIM * (w + 1))
            return dict(hq=hq_of(rs, w), hf=hf_of(rs, w), hi=hi_of(rs, w), hg=hg_of(rs, w), lb=lb_all[:, ls],
                        gain=gain_ref[w], ov=o_ref[rs, ls], dyh=dyh_ref[rs, ls].astype(F32),
                        st=st_ref[w, c], dst=dstate[w])

        def head_math(v):
            lb, gain, hq, hg = v["lb"], v["gain"], v["hq"], v["hg"]
            ch = yield from _hgrn_chunk(hq, v["hf"], v["hi"], lb)
            ov, dyh = v["ov"], v["dyh"]
            r = lax.rsqrt(jnp.mean(ov * ov, axis=-1, keepdims=True) + NORM_EPS)
            on = ov * r
            sg = _sigmoid(hg)
            doh = dyh * (hg * sg)
            out = dict(dhg=(dyh * on * gain * (sg * (1.0 + hg * (1.0 - sg)))).astype(BF16),
                       dgain=jnp.sum(doh * on, axis=0, keepdims=True))
            don = doh * gain
            do = r * (don - on * jnp.mean(don * on, axis=-1, keepdims=True))
            dob = do.astype(BF16)
            st, dst = v["st"], v["dst"]
            stb, dstb = st.astype(BF16), dst.astype(BF16)
            vb = ch["v"].astype(BF16)
            qhatb = ch["qhat"].astype(BF16)
            eglast = jnp.exp(ch["glast"])
            yield
            dqhat = _dot(dob, stb)
            dkdec = _dot(vb, dstb)
            dv = _dot(ch["kdec"].astype(BF16), dstb, NT)
            deg = jnp.sum(dst * st, axis=0, keepdims=True)
            out["dstate"] = dst * eglast + _dot(dob, qhatb, TN)
            dA = jnp.where(ch["causal"], _dot(dob, vb, NT), 0.0)
            dv = dv + _dot(ch["A"].astype(BF16), dob, TN)
            dAb = dA.astype(BF16)
            yield
            dqs_parts, dgq_parts = [], []
            dk_intra, dgk = None, None
            for i in range(nsub):
                dA_i = dAb[HGRN_SUB * i:HGRN_SUB * (i + 1)]
                dqt = _dot(dA_i, ch["kt"][i])
                dkt = _dot(dA_i, ch["qt"][i], TN)
                dqs_parts.append(dqt * ch["eq"][i])
                dgq_parts.append(dqt * ch["qt"][i].astype(F32))
                dk_i = dkt * ch["ek"][i]
                dgk_i = dkt * ch["kt"][i].astype(F32)
                dk_intra = dk_i if dk_intra is None else dk_intra + dk_i
                dgk = dgk_i if dgk is None else dgk + dgk_i
            dqs_inter = dqhat * ch["eG"]
            dk_state = dkdec * ch["edec"]
            dqs = jnp.concatenate(dqs_parts, axis=0) + dqs_inter
            dk = dk_intra + dk_state
            dG = jnp.concatenate(dgq_parts, axis=0) - dgk + ch["qs"] * dqs_inter - ch["k"] * dk_state
            tail = jnp.sum(dkdec * ch["kdec"], axis=0, keepdims=True) + deg * eglast
            dG = dG + jnp.where(last_row, tail, 0.0)
            yield
            dlf = _tri_sum(anti, dG)
            df = dlf / ch["f"] - dk
            sf, sq = ch["sf"], ch["sq"]
            out["dhf"] = (df * (1.0 - lb) * sf * (1.0 - sf)).astype(BF16)
            out["dlb"] = jnp.sum(df * (1.0 - sf), axis=0, keepdims=True)
            out["dhq"] = (dqs * (sq * (1.0 + hq * (1.0 - sq)))).astype(BF16)
            out["dhi"] = dv.astype(BF16)
            return out

        def head_store(w, c, out):
            rs = pl.ds(pl.multiple_of(c * C, C), C)
            ls = slice(HGRN_DIM * w, HGRN_DIM * (w + 1))
            for a, key in enumerate(("dhq", "dhf", "dhi", "dhg")):
                lo = HGRN_HEADS * HGRN_DIM * a + HGRN_DIM * w
                dh4_ref[rs, lo:lo + HGRN_DIM] = out[key]
            dstate[w] = out["dstate"]
            dgain_ref[w] += out["dgain"]
            dlb_acc[:, ls] += out["dlb"]

        def chunk(cc, carry):
            c = cg - 1 - cc
            outs = _interleave([head_math(v) for v in [head_load(w, c) for w in range(W)]])
            for w in range(W):
                head_store(w, c, outs[w])
            return carry

        lax.fori_loop(0, cg, chunk, 0)

        @pl.when((b == nseq - 1) & (g == ng - 1))
        def _():
            dl0 = dlb_acc[...] * lb_all * (1.0 - lb_all)
            dlbl_ref[0:1, :] = dl0
            dlbl_ref[1:2, :] = -dl0

        @pl.when((h == nh - 1) & (b == nseq - 1) & (g == ng - 1))
        def _():
            exchange.wait()

    gi, in_specs = _hgrn_specs(nseq, ng, rows, True)
    row = lambda: pl.BlockSpec((rows, W * HGRN_DIM), lambda h, b, g: (b * ng + gi(g), h))
    anyspec = lambda: pl.BlockSpec(memory_space=pl.ANY)
    return pl.pallas_call(
        body, name="hgrn_bwd",
        grid=(nh, nseq, ng),
        in_specs=in_specs + [row(), row(),
                             pl.BlockSpec((None, W, cg, HGRN_DIM, HGRN_DIM), lambda h, b, g: (b, h, gi(g), 0, 0))]
                 + [anyspec() for _ in chip_sums],
        out_specs=[pl.BlockSpec((rows, 4 * W * HGRN_DIM), lambda h, b, g: (b * ng + gi(g), h)),
                   pl.BlockSpec((W, 1, HGRN_DIM), lambda h, b, g: (h, 0, 0)),
                   pl.BlockSpec((2, W * HGRN_DIM), lambda h, b, g: (0, h))] + [anyspec() for _ in chip_sums],
        out_shape=[jax.ShapeDtypeStruct((T, 4 * HGRN_HEADS * HGRN_DIM), BF16),
                   jax.ShapeDtypeStruct((HGRN_HEADS, 1, HGRN_DIM), F32),
                   jax.ShapeDtypeStruct((2, HGRN_HEADS * HGRN_DIM), F32)]
                  + [jax.ShapeDtypeStruct(s.shape, s.dtype) for s in chip_sums],
        scratch_shapes=[pltpu.VMEM((W, HGRN_DIM, HGRN_DIM), F32), pltpu.VMEM((1, W * HGRN_DIM), F32)]
                       + _ChipExchange.scratch(n),
        compiler_params=_params(("arbitrary", "arbitrary", "arbitrary")),
    )(*([proj] * (4 * W * HGRN_DIM // HGRN_COLS)), lb_logits, gain3, o, d_yh, states, *chip_sums)


def _gate_specs(tm):
    spec = lambda c0, q: pl.BlockSpec((tm, 512), lambda i: (i, c0 // 512 + q))
    return [spec(C_GA, q) for q in range(4)] + [spec(C_GH, q) for q in range(4)]


def _gates(refs):
    ga = jnp.concatenate([r[...] for r in refs[0:4]], axis=1).astype(F32)
    gh = jnp.concatenate([r[...] for r in refs[4:8]], axis=1).astype(F32)
    return ga, gh


def _branch_merge(proj, ya, yh, wa_t, wh_t):
    T = proj.shape[0]
    tm = min(512, T)

    def body(*refs):
        ya_ref, yh_ref, wa_ref, wh_ref, merged_ref, ua_ref, uh_ref = refs[8:]
        ga, gh = _gates(refs)
        ua = _dot(ya_ref[...], wa_ref[...], NT)
        uh = _dot(yh_ref[...], wh_ref[...], NT)
        merged_ref[...] = (_sigmoid(ga) * ua + _sigmoid(gh) * uh).astype(BF16)
        ua_ref[...] = ua.astype(BF16)
        uh_ref[...] = uh.astype(BF16)

    rowb = lambda w: pl.BlockSpec((tm, w), lambda i: (i, 0))
    full = lambda a: pl.BlockSpec(a.shape, lambda i: (0, 0))
    return pl.pallas_call(
        body, name="branch_merge",
        grid=(T // tm,),
        in_specs=_gate_specs(tm) + [rowb(1024), rowb(1024), full(wa_t), full(wh_t)],
        out_specs=[rowb(D_MODEL)] * 3,
        out_shape=[jax.ShapeDtypeStruct((T, D_MODEL), BF16)] * 3,
        compiler_params=_params(("arbitrary",)),
    )(*([proj] * 8), ya, yh, wa_t, wh_t)


def _out_norm_loss(merged, x2, tgt2, gpost, wout):
    T = merged.shape[0]
    tm = min(256, T)

    def body(m_ref, x_ref, t_ref, gpost_ref, wo_ref, dy_ref, dm_ref, dout_ref, loss_ref, dgpost_ref):
        @pl.when(pl.program_id(0) == 0)
        def _():
            loss_ref[...] = jnp.zeros(loss_ref.shape, F32)
            dgpost_ref[...] = jnp.zeros(dgpost_ref.shape, F32)

        y = _dot(m_ref[...], wo_ref[...])
        r2 = lax.rsqrt(jnp.mean(y * y, axis=-1, keepdims=True) + NORM_EPS)
        yn = y * r2
        gpost = gpost_ref[...]
        err = x_ref[...] + yn * gpost - t_ref[...]
        loss_ref[...] += jnp.sum(err * err, axis=0, keepdims=True)
        dout = err * (1.0 / D_MODEL)
        dout_ref[...] = dout
        dgpost_ref[...] += jnp.sum(dout * yn, axis=0, keepdims=True)
        dyn = dout * gpost
        dy = (r2 * (dyn - yn * jnp.mean(dyn * yn, axis=-1, keepdims=True))).astype(BF16)
        dy_ref[...] = dy
        dm_ref[...] = _dot(dy, wo_ref[...], NT).astype(BF16)

    rowb = lambda: pl.BlockSpec((tm, D_MODEL), lambda i: (i, 0))
    vec = lambda: pl.BlockSpec((1, D_MODEL), lambda i: (0, 0))
    return pl.pallas_call(
        body, name="out_norm_loss",
        grid=(T // tm,),
        in_specs=[rowb(), rowb(), rowb(), vec(), pl.BlockSpec(wout.shape, lambda i: (0, 0))],
        out_specs=[rowb(), rowb(), rowb(), vec(), vec()],
        out_shape=[jax.ShapeDtypeStruct((T, D_MODEL), BF16)] * 2
                  + [jax.ShapeDtypeStruct((T, D_MODEL), F32)] + [jax.ShapeDtypeStruct((1, D_MODEL), F32)] * 2,
        compiler_params=_params(("arbitrary",)),
    )(merged, x2, tgt2, gpost, wout)


def _branch_bwd(proj, dm, ua, uh, wa_t, wh_t):
    T = proj.shape[0]
    tm = min(256, T)

    def body(*refs):
        (dm_ref, ua_ref, uh_ref, wa_ref, wh_ref,
         dua_ref, duh_ref, dgg_ref, dya_ref, dyh_ref) = refs[8:]
        ga, gh = _gates(refs)
        sa, sh = _sigmoid(ga), _sigmoid(gh)
        dm = dm_ref[...].astype(F32)
        dua = (dm * sa).astype(BF16)
        duh = (dm * sh).astype(BF16)
        dua_ref[...] = dua
        duh_ref[...] = duh
        dgg_ref[:, :D_MODEL] = (dm * ua_ref[...].astype(F32) * (sa * (1.0 - sa))).astype(BF16)
        dgg_ref[:, D_MODEL:] = (dm * uh_ref[...].astype(F32) * (sh * (1.0 - sh))).astype(BF16)
        dya_ref[...] = _dot(dua, wa_ref[...]).astype(BF16)
        dyh_ref[...] = _dot(duh, wh_ref[...]).astype(BF16)

    rowb = lambda w: pl.BlockSpec((tm, w), lambda i: (i, 0))
    full = lambda a: pl.BlockSpec(a.shape, lambda i: (0, 0))
    return pl.pallas_call(
        body, name="branch_bwd",
        grid=(T // tm,),
        in_specs=_gate_specs(tm) + [rowb(D_MODEL)] * 3 + [full(wa_t), full(wh_t)],
        out_specs=[rowb(D_MODEL)] * 2 + [rowb(2 * D_MODEL)] + [rowb(1024)] * 2,
        out_shape=[jax.ShapeDtypeStruct((T, D_MODEL), BF16)] * 2 + [jax.ShapeDtypeStruct((T, 2 * D_MODEL), BF16)]
                  + [jax.ShapeDtypeStruct((T, 1024), BF16)] * 2,
        compiler_params=_params(("arbitrary",)),
    )(*([proj] * 8), dm, ua, uh, wa_t, wh_t)


def _tn_matmul(L, R, bm, bn, name):
    T, M = L.shape
    N = R.shape[1]

    def body(l_ref, r_ref, out_ref):
        out_ref[...] = _dot(l_ref[...], r_ref[...], TN).astype(BF16)

    return pl.pallas_call(
        body, name=name,
        grid=(N // bn, M // bm),
        in_specs=[pl.BlockSpec((T, bm), lambda j, i: (0, i)),
                  pl.BlockSpec((T, bn), lambda j, i: (0, j))],
        out_specs=pl.BlockSpec((bm, bn), lambda j, i: (i, j)),
        out_shape=jax.ShapeDtypeStruct((M, N), BF16),
        compiler_params=_params(("arbitrary", "arbitrary")),
    )(L, R)


def _tn_matmul_pieces(pieces, R, bm, bn, name):
    T, N = R.shape
    M = sum(p.shape[1] for p in pieces)
    out, row0 = None, 0
    for q, L in enumerate(pieces):
        off = row0 // bm

        def body(*refs):
            refs[-1][...] = _dot(refs[0][...], refs[1][...], TN).astype(BF16)

        prev = [] if out is None else [out]
        out = pl.pallas_call(
            body, name="%s_%d" % (name, q),
            grid=(N // bn, L.shape[1] // bm),
            in_specs=[pl.BlockSpec((T, bm), lambda j, i: (0, i)), pl.BlockSpec((T, bn), lambda j, i: (0, j))]
                     + [pl.BlockSpec(memory_space=pl.ANY) for _ in prev],
            out_specs=pl.BlockSpec((bm, bn), lambda j, i, off=off: (off + i, j)),
            out_shape=jax.ShapeDtypeStruct((M, N), BF16),
            input_output_aliases={2: 0} if prev else {},
            compiler_params=_params(("arbitrary", "arbitrary")),
        )(L, R, *prev)
        row0 += L.shape[1]
    return out


def _dh_prenorm_bwd(pieces, wt_in, x2, dout, gpre, chip_sums):
    T = x2.shape[0]
    tm = min(1024, T)
    ne = 4
    te = tm // ne
    tk = 512
    nk = IN_WIDTH // tk
    nt = T // tm
    n = len(chip_sums)
    npiece = len(pieces)
    counts = [p.shape[1] // tk for p in pieces]
    starts = [sum(counts[:q]) for q in range(npiece)]
    assert sum(counts) == nk and all(p.shape[1] % tk == 0 for p in pieces)

    def body(*all_refs):
        piece_refs = all_refs[:npiece]
        w_ref, x_ref, dout_ref, g_ref = all_refs[npiece:npiece + 4]
        refs = all_refs[npiece + 4:]
        ins = refs[:n]
        gx_ref, dg_ref = refs[n], refs[n + 1]
        outs = refs[n + 2:2 * n + 2]
        acc, send_sems, recv_sems, local_sems = refs[2 * n + 2:]
        i, k = pl.program_id(0), pl.program_id(1)
        exchange = _ChipExchange(ins, outs, send_sems, recv_sems, local_sems)

        @pl.when((i == 0) & (k == 0))
        def _():
            dg_ref[...] = jnp.zeros(dg_ref.shape, F32)
            exchange.start()

        @pl.when((i == nt - 1) & (k == nk + ne - 1))
        def _():
            exchange.wait()

        @pl.when(k == 0)
        def _():
            acc[...] = jnp.zeros(acc.shape, F32)

        for q in range(npiece):
            @pl.when((k >= starts[q]) & (k < starts[q] + counts[q]))
            def _(q=q):
                acc[...] += _dot(piece_refs[q][...], w_ref[...])

        @pl.when(k >= nk)
        def _():
            dh = acc[pl.ds(pl.multiple_of((k - nk) * te, te), te), :]
            x = x_ref[...]
            r = lax.rsqrt(jnp.mean(x * x, axis=-1, keepdims=True) + NORM_EPS)
            xn = x * r
            dg_ref[...] += jnp.sum(dh * xn, axis=0, keepdims=True)
            dxn = dh * g_ref[...]
            gx_ref[...] = dout_ref[...] + r * (dxn - xn * jnp.mean(dxn * xn, axis=-1, keepdims=True))

    piece_spec = lambda q: pl.BlockSpec((tm, tk), lambda i, k: (i, jnp.clip(k - starts[q], 0, counts[q] - 1)))
    rowb = lambda: pl.BlockSpec((te, D_MODEL), lambda i, k: (ne * i + jnp.clip(k - nk, 0, ne - 1), 0))
    vec = lambda: pl.BlockSpec((1, D_MODEL), lambda i, k: (0, 0))
    anyspec = lambda: pl.BlockSpec(memory_space=pl.ANY)
    return pl.pallas_call(
        body, name="dh_prenorm_bwd",
        grid=(nt, nk + ne),
        in_specs=[piece_spec(q) for q in range(npiece)]
                 + [pl.BlockSpec((tk, D_MODEL), lambda i, k: (jnp.minimum(k, nk - 1), 0)),
                    rowb(), rowb(), vec()] + [anyspec() for _ in chip_sums],
        out_specs=[rowb(), vec()] + [anyspec() for _ in chip_sums],
        out_shape=[jax.ShapeDtypeStruct((T, D_MODEL), F32), jax.ShapeDtypeStruct((1, D_MODEL), F32)]
                  + [jax.ShapeDtypeStruct(s.shape, s.dtype) for s in chip_sums],
        scratch_shapes=[pltpu.VMEM((tm, D_MODEL), F32)] + _ChipExchange.scratch(n),
        compiler_params=_params(("arbitrary", "arbitrary")),
    )(*pieces, wt_in, x2, dout, gpre, *chip_sums)


def _sum_slots(recv, br, name):
    nslot, R, C = recv.shape

    def body(r_ref, out_ref):
        acc = r_ref[0].astype(F32)
        for s in range(1, nslot):
            acc = acc + r_ref[s].astype(F32)
        out_ref[...] = acc

    return pl.pallas_call(
        body, name=name,
        grid=(R // br,),
        in_specs=[pl.BlockSpec((nslot, br, C), lambda i: (0, i, 0))],
        out_specs=pl.BlockSpec((br, C), lambda i: (i, 0)),
        out_shape=jax.ShapeDtypeStruct((R, C), F32),
        compiler_params=_params(("arbitrary",)),
    )(recv)


def _adamw_math(w, g, m, v):
    m = ADAM_B1 * m + (1.0 - ADAM_B1) * g
    v = ADAM_B2 * v + (1.0 - ADAM_B2) * (g * g)
    m_hat = m / (1.0 - ADAM_B1 ** ADAM_STEP)
    v_hat = v / (1.0 - ADAM_B2 ** ADAM_STEP)
    delta = -ADAM_LR * (m_hat / (jnp.sqrt(v_hat) + ADAM_EPS) + ADAM_WD * w)
    return delta, m, v


def _adamw_slots(w, recv, m, v, br, name):
    nslot, R, C = recv.shape

    def body(w_ref, r_ref, m_ref, v_ref, g_ref, d_ref, nm_ref, nv_ref):
        g = r_ref[0].astype(F32)
        for s in range(1, nslot):
            g = g + r_ref[s].astype(F32)
        g_ref[...] = g
        d_ref[...], nm_ref[...], nv_ref[...] = _adamw_math(w_ref[...], g, m_ref[...], v_ref[...])

    spec = lambda: pl.BlockSpec((None, br, C), lambda i: (0, i, 0))
    return pl.pallas_call(
        body, name=name,
        grid=(R // br,),
        in_specs=[spec(), pl.BlockSpec((nslot, br, C), lambda i: (0, i, 0)), spec(), spec()],
        out_specs=[spec(), spec(), spec(), spec()],
        out_shape=[jax.ShapeDtypeStruct((1, R, C), F32)] * 4,
        compiler_params=_params(("arbitrary",)),
    )(w, recv, m, v)


def _adamw(w, g, m, v, br, name):
    R, C = g.shape
    lead = (None,) * (w.ndim - 2)

    def body(w_ref, g_ref, m_ref, v_ref, d_ref, nm_ref, nv_ref):
        d_ref[...], nm_ref[...], nv_ref[...] = _adamw_math(w_ref[...], g_ref[...], m_ref[...], v_ref[...])

    spec = lambda: pl.BlockSpec(lead + (br, C), lambda i: (0,) * len(lead) + (i, 0))
    return pl.pallas_call(
        body, name=name,
        grid=(R // br,),
        in_specs=[spec(), pl.BlockSpec((br, C), lambda i: (i, 0)), spec(), spec()],
        out_specs=[spec(), spec(), spec()],
        out_shape=[jax.ShapeDtypeStruct(w.shape, F32)] * 3,
        compiler_params=_params(("arbitrary",)),
    )(w, g, m, v)


def _sibling_exchange(partials, name):
    n = len(partials)

    def body(*refs):
        ins, outs = refs[:n], refs[n:2 * n]
        send_sems, recv_sems = refs[2 * n:]
        x, y, c = _place()

        def copy(a, p):
            return pltpu.make_async_remote_copy(
                src_ref=ins[a].at[p, 1 - c], dst_ref=outs[a].at[p],
                send_sem=send_sems.at[a, p], recv_sem=recv_sems.at[a, p],
                device_id=(x, y, 1 - c), device_id_type=pl.DeviceIdType.MESH)

        copies = [copy(a, p) for p in range(4) for a in range(n)]
        for cp in copies:
            cp.start()
        for cp in copies:
            cp.wait()

    anyspec = lambda: pl.BlockSpec(memory_space=pl.ANY)
    return pl.pallas_call(
        body, name=name,
        in_specs=[anyspec() for _ in partials],
        out_specs=[anyspec() for _ in partials],
        out_shape=[jax.ShapeDtypeStruct((4,) + p.shape[2:], p.dtype) for p in partials],
        scratch_shapes=[pltpu.SemaphoreType.DMA((n, 4)), pltpu.SemaphoreType.DMA((n, 4))],
    )(*partials)


def _chip_sum(partial, from_sibling, br, name):
    _, _, R, C = partial.shape
    cls = lax.axis_index("c").astype(jnp.int32).reshape(1)

    def body(c_ref, mine_ref, sib_ref, out_ref):
        out_ref[...] = (mine_ref[...].astype(F32) + sib_ref[...].astype(F32)).astype(BF16)

    grid_spec = pltpu.PrefetchScalarGridSpec(
        num_scalar_prefetch=1,
        grid=(4, R // br),
        in_specs=[pl.BlockSpec((None, None, br, C), lambda p, i, c: (p, c[0], i, 0)),
                  pl.BlockSpec((None, br, C), lambda p, i, c: (p, i, 0))],
        out_specs=pl.BlockSpec((None, br, C), lambda p, i, c: (p, i, 0)),
    )
    return pl.pallas_call(
        body, name=name, grid_spec=grid_spec,
        out_shape=jax.ShapeDtypeStruct((4, R, C), BF16),
        compiler_params=_params(("arbitrary", "arbitrary")),
    )(cls, partial, from_sibling)


def _all_reduce_small(packed):
    shape = packed.shape

    def body(in_ref, out_ref, slots, send_sems, recv_sems):
        x, y, c = _place()
        my_slot = 4 * x + 2 * y + c

        def peer(k):
            return (x ^ ((k >> 2) & 1), y ^ ((k >> 1) & 1), c ^ (k & 1))

        def copy(k):
            p = peer(k)
            return pltpu.make_async_remote_copy(
                src_ref=in_ref, dst_ref=slots.at[my_slot],
                send_sem=send_sems.at[k - 1], recv_sem=recv_sems.at[k - 1],
                device_id=p, device_id_type=pl.DeviceIdType.MESH)

        def arrival(k):
            p = peer(k)
            return pltpu.make_async_remote_copy(
                src_ref=in_ref, dst_ref=slots.at[4 * p[0] + 2 * p[1] + p[2]],
                send_sem=send_sems.at[k - 1], recv_sem=recv_sems.at[k - 1],
                device_id=p, device_id_type=pl.DeviceIdType.MESH)

        sends = [copy(k) for k in range(1, N_DEV)]
        for cp in sends:
            cp.start()
        slots[my_slot] = in_ref[...]
        for k in range(1, N_DEV):
            arrival(k).wait_recv()
        for cp in sends:
            cp.wait_send()
        acc = slots[0]
        for s in range(1, N_DEV):
            acc = acc + slots[s]
        out_ref[...] = acc

    return pl.pallas_call(
        body, name="all_reduce_small",
        in_specs=[pl.BlockSpec(memory_space=pltpu.VMEM)],
        out_specs=pl.BlockSpec(memory_space=pltpu.VMEM),
        out_shape=jax.ShapeDtypeStruct(shape, F32),
        scratch_shapes=[pltpu.VMEM((N_DEV,) + shape, F32),
                        pltpu.SemaphoreType.DMA((7,)), pltpu.SemaphoreType.DMA((7,))],
    )(packed)


def _pack_small(norm_pre, norm_post, lb_logits, hgrn_norm, rel_bias, sinks, extra=None):
    tail = [hgrn_norm.reshape(1, 1024), rel_bias.reshape(1, 512), sinks.reshape(1, 16)]
    used = 1024 + 512 + 16
    if extra is not None:
        tail.append(extra.reshape(1, 1))
        used += 1
    tail.append(jnp.zeros((1, D_MODEL - used), F32))
    rows = [norm_pre.reshape(1, D_MODEL), norm_post.reshape(1, D_MODEL), lb_logits.reshape(1, D_MODEL),
            jnp.concatenate(tail, axis=1), jnp.zeros((4, D_MODEL), F32)]
    return jnp.concatenate(rows, axis=0)


def _unpack_small(p):
    return (p[0:1], p[3, 1024:1536].reshape(REL_BUCKETS, ATTN_HEADS), p[3:4, 1536:1552],
            p[2].reshape(2, 1024), p[3, 0:1024].reshape(1, HGRN_HEADS, HGRN_DIM), p[1:2])


def _local_step(nseq, S, x2, tgt2, proj, h, rel_bias, attn_sinks, lb_logits, hgrn_norm, norm_post, shards):
    nb = S // ATTN_BLOCK
    bucket = jnp.asarray(_t5_bucket_table())
    gain3 = hgrn_norm.reshape(HGRN_HEADS, 1, HGRN_DIM)

    bias = _bias_table(rel_bias, bucket)
    ya, wout = _attn_fwd(proj, bias, attn_sinks, nseq, nb, shards[:1])
    o, yh, states, wa_t, wh_t = _hgrn_fwd(proj, lb_logits, gain3, nseq, S, shards[1:])
    merged, ua, uh = _branch_merge(proj, ya, yh, wa_t, wh_t)
    dy, dm, dout, loss_cols, d_gpost = _out_norm_loss(merged, x2, tgt2, norm_post, wout)
    dua, duh, dgg, dya, dyh = _branch_bwd(proj, dm, ua, uh, wa_t, wh_t)

    p_out = _tn_matmul(merged, dy, 256, 1024, "dw_out")
    p_a = _tn_matmul(dua, ya, 256, 1024, "dw_branch_attn")
    p_h = _tn_matmul(duh, yh, 256, 1024, "dw_branch_hgrn")
    small_sums = _chip_sums([p_out, p_a, p_h], (128, 128, 128), ("dw_out", "dw_branch_attn", "dw_branch_hgrn"),
                            "sibling_exchange_small")

    dq, dkv, dg, dbias, d_sinks = _attn_bwd(proj, bias, attn_sinks, dya, nseq, nb)
    d_rel_bias = _bias_table_bwd(dbias, bucket)
    dh4, d_gain, d_lbl, r_out, r_a, r_h = _hgrn_bwd(proj, lb_logits, gain3, o, dyh, states, nseq, S, small_sums)
    dproj = [dq, dkv, dg, dh4, dgg]
    p_in = _tn_matmul_pieces(dproj, h, 512, 1024, "dw_in")
    return dproj, dout, p_in, r_out, r_a, r_h, d_gpost, d_lbl, d_gain, d_rel_bias, d_sinks, loss_cols


def _chip_sums(partials, block_rows, names, exchange_name):
    split = [p.reshape(4, 2, p.shape[0] // N_DEV, p.shape[1]) for p in partials]
    from_sibling = _sibling_exchange(split, exchange_name)
    return [_chip_sum(p, f, br, "chip_sum_" + nm) for p, f, br, nm in zip(split, from_sibling, block_rows, names)]


def kernel(x, norm_pre, w_in, rel_bias, attn_sinks, lb_logits, hgrn_norm, w_branch_attn, w_branch_hgrn, w_out, norm_post, loss_target, m_norm_pre, m_w_in, m_rel_bias, m_attn_sinks, m_lb_logits, m_hgrn_norm, m_w_branch_attn, m_w_branch_hgrn, m_w_out, m_norm_post, v_norm_pre, v_w_in, v_rel_bias, v_attn_sinks, v_lb_logits, v_hgrn_norm, v_w_branch_attn, v_w_branch_hgrn, v_w_out, v_norm_post):
    nseq, S, _ = x.shape
    T = nseq * S
    x2 = x.reshape(T, D_MODEL)
    tgt2 = loss_target.reshape(T, D_MODEL)

    h = _prenorm(x2, norm_pre)
    proj, wt_in = _gather_inproj(h, w_in[0].T.astype(BF16))
    shards = [w_out[0].astype(BF16), w_branch_attn[0].T.astype(BF16), w_branch_hgrn[0].T.astype(BF16)]

    (dproj, dout, p_in, r_out, r_a, r_h, d_gpost, d_lbl, d_gain, d_rel_bias, d_sinks, loss_cols) = _local_step(
        nseq, S, x2, tgt2, proj, h, rel_bias, attn_sinks, lb_logits, hgrn_norm, norm_post, shards)

    in_sums = _chip_sums([p_in], (192,), ("dw_in",), "sibling_exchange_w_in")
    grad_x2, d_gpre, r_in = _dh_prenorm_bwd(dproj, wt_in, x2, dout, norm_pre, in_sums)
    g_a = _sum_slots(r_a, 128, "sum_dw_branch_attn").T
    g_h = _sum_slots(r_h, 128, "sum_dw_branch_hgrn").T

    loss_part = 0.5 / D_MODEL * jnp.sum(loss_cols)
    packed = _pack_small(d_gpre, d_gpost, d_lbl, d_gain, d_rel_bias, d_sinks, extra=loss_part)
    total = _all_reduce_small(packed)
    loss = total[3, 1024 + 512 + 16]
    sm_w = _pack_small(norm_pre, norm_post, lb_logits, hgrn_norm, rel_bias, attn_sinks)
    sm_m = _pack_small(m_norm_pre, m_norm_post, m_lb_logits, m_hgrn_norm, m_rel_bias, m_attn_sinks)
    sm_v = _pack_small(v_norm_pre, v_norm_post, v_lb_logits, v_hgrn_norm, v_rel_bias, v_attn_sinks)
    sm_d, sm_nm, sm_nv = _adamw(sm_w, total, sm_m, sm_v, 8, "adamw_small")

    t = lambda a: jnp.swapaxes(a, 1, 2)
    g_in, d_in, nm_in, nv_in = map(t, _adamw_slots(t(w_in), r_in, t(m_w_in), t(v_w_in), 192, "adamw_w_in"))
    g_out, d_out, nm_out, nv_out = _adamw_slots(w_out, r_out, m_w_out, v_w_out, 128, "adamw_w_out")
    d_a, nm_a, nv_a = _adamw(w_branch_attn, g_a, m_w_branch_attn, v_w_branch_attn, 256, "adamw_w_branch_attn")
    d_h, nm_h, nv_h = _adamw(w_branch_hgrn, g_h, m_w_branch_hgrn, v_w_branch_hgrn, 256, "adamw_w_branch_hgrn")

    def group(small, big_in, big_a, big_h, big_out):
        npre, rb, sk, lbl, hn, npost = _unpack_small(small)
        return (npre, big_in, rb, sk, lbl, hn, big_a, big_h, big_out, npost)

    return (loss, grad_x2.reshape(nseq, S, D_MODEL),
            *group(total, g_in, g_a[None], g_h[None], g_out),
            *group(sm_d, d_in, d_a, d_h, d_out),
            *group(sm_nm, nm_in, nm_a, nm_h, nm_out),
            *group(sm_nv, nv_in, nv_a, nv_h, nv_out))
```
